```python
import jax, jax.numpy as jnp
from jax import lax
import numpy as np

D_MODEL = 1024
BATCH = 8
SEQ = 4096
DEPTH = 1

N_META = 16
POOL_WIDTH = D_MODEL // 2
POOL_WINDOWS = (2, 4, 8, 16)
POOL_GROUP = POOL_WIDTH // len(POOL_WINDOWS)
N_HEADS = 8
HEAD_DIM = 64
ATTN_WIDTH = N_HEADS * HEAD_DIM
Q_BLOCK = 128
RMS_EPS = 1e-6
IN_SIZES = (POOL_WIDTH, POOL_WIDTH, ATTN_WIDTH, ATTN_WIDTH, ATTN_WIDTH, ATTN_WIDTH, N_HEADS, D_MODEL, D_MODEL)
N_IN = 2 * POOL_WIDTH + 4 * ATTN_WIDTH + N_HEADS + 2 * D_MODEL

kernel_name = "gated_pool_forgetting_attn_hybrid"


def _split_points():
    pts, acc = [], 0
    for s in IN_SIZES[:-1]:
        acc += s
        pts.append(acc)
    return pts


def rmsnorm(x, g):
    xf = x.astype(jnp.float32)
    xf = xf * lax.rsqrt(jnp.mean(xf * xf, axis=-1, keepdims=True) + RMS_EPS)
    return (xf * g.astype(jnp.float32)).astype(x.dtype)


def causal_multiscale_pool(u, pool_w, pool_scale):
    B, L, _ = u.shape
    groups = jnp.split(u, len(POOL_WINDOWS), axis=-1)
    pos = jnp.arange(L, dtype=jnp.float32)[:, None]
    pooled = []
    for w, ug in zip(POOL_WINDOWS, groups):
        uf = ug.astype(jnp.float32)
        c = jnp.cumsum(uf, axis=1)
        c_prev = jnp.pad(c, ((0, 0), (w, 0), (0, 0)))[:, :L]
        count = jnp.minimum(pos + 1.0, float(w))
        pooled.append((c - c_prev) / count - uf)
    p = jnp.stack(pooled, axis=2).astype(u.dtype)
    y = jnp.einsum('blgc,gcd->blgd', p, pool_w)
    return y.reshape(B, L, POOL_WIDTH) * pool_scale


def forgetting_attention(q, k, v, log_f):
    B, L, H, Dh = q.shape
    pad = (Q_BLOCK - L % Q_BLOCK) % Q_BLOCK
    padw = ((0, 0), (pad, 0), (0, 0), (0, 0))
    qp, kp, vp = jnp.pad(q, padw), jnp.pad(k, padw), jnp.pad(v, padw)
    c = jnp.cumsum(log_f, axis=1)
    c = jnp.transpose(jnp.pad(c, ((0, 0), (pad, 0), (0, 0))), (0, 2, 1))
    Lp = L + pad
    scale = 1.0 / np.sqrt(HEAD_DIM)
    outs = []
    for i in range(Lp // Q_BLOCK):
        q0, q1 = i * Q_BLOCK, (i + 1) * Q_BLOCK
        qb, kb, vb = qp[:, q0:q1], kp[:, :q1], vp[:, :q1]
        s = jnp.einsum('bqhd,bkhd->bhqk', qb, kb).astype(jnp.float32) * scale
        s = s + c[:, :, q0:q1, None] - c[:, :, None, :q1]
        q_idx = jnp.arange(q0, q1)[:, None]
        k_idx = jnp.arange(q1)[None, :]
        valid = (k_idx <= q_idx) & ((k_idx >= pad) | (k_idx == q_idx))
        s = jnp.where(valid, s, -jnp.inf)
        p = jax.nn.softmax(s, axis=-1)
        outs.append(jnp.einsum('bhqk,bkhd->bqhd', p.astype(vb.dtype), vb))
    o = jnp.concatenate(outs, axis=1)[:, pad:]
    return o


def hybrid_layer(x, norm_g, w_in, b_forget, pool_w, pool_scale, w_up_pool, w_up_attn, w_out):
    B, L, _ = x.shape
    h = rmsnorm(x, norm_g)
    proj = jnp.einsum('bld,dn->bln', h, w_in)
    u_pool, z_pool, q, k, v, z_attn, f_logit, g_pool, g_attn = jnp.split(proj, _split_points(), axis=-1)
    y_pool = causal_multiscale_pool(u_pool, pool_w, pool_scale) * jax.nn.silu(z_pool)
    q = q.reshape(B, L, N_HEADS, HEAD_DIM)
    k = k.reshape(B, L, N_HEADS, HEAD_DIM)
    v = v.reshape(B, L, N_HEADS, HEAD_DIM)
    log_f = jax.nn.log_sigmoid((f_logit + b_forget).astype(jnp.float32))
    y_attn = forgetting_attention(q, k, v, log_f).reshape(B, L, ATTN_WIDTH) * jax.nn.silu(z_attn)
    merged = (jax.nn.sigmoid(g_pool) * jnp.einsum('blc,cd->bld', y_pool, w_up_pool)
              + jax.nn.sigmoid(g_attn) * jnp.einsum('blc,cd->bld', y_attn, w_up_attn))
    return x + jnp.einsum('bld,de->ble', merged, w_out)


def _fwd_setup_inputs(seed: int = 0) -> dict:
    key = jax.random.key(seed)
    ks = jax.random.split(key, 12)
    f0 = sum(IN_SIZES[:6])
    x = jax.random.normal(ks[0], (BATCH, SEQ, D_MODEL), jnp.float32)
    meta_tokens = jax.random.normal(ks[1], (N_META, D_MODEL), jnp.float32)
    norm_g = 1.0 + 0.02 * jax.random.normal(ks[2], (DEPTH, D_MODEL), jnp.float32)
    w_in = jax.random.normal(ks[3], (DEPTH, D_MODEL, N_IN), jnp.float32) * D_MODEL ** -0.5
    w_in = w_in.at[:, :, f0:f0 + N_HEADS].multiply(0.1)
    b_forget = (jnp.linspace(1.0, 6.0, N_HEADS, dtype=jnp.float32)[None, :]
                + 0.1 * jax.random.normal(ks[4], (DEPTH, N_HEADS), jnp.float32))
    pool_w = jax.random.normal(ks[5], (DEPTH, len(POOL_WINDOWS), POOL_GROUP, POOL_GROUP), jnp.float32) * POOL_GROUP ** -0.5
    pool_scale = 1.0 + 0.1 * jax.random.normal(ks[6], (DEPTH, POOL_WIDTH), jnp.float32)
    w_up_pool = jax.random.normal(ks[7], (DEPTH, POOL_WIDTH, D_MODEL), jnp.float32) * POOL_WIDTH ** -0.5
    w_up_attn = jax.random.normal(ks[8], (DEPTH, ATTN_WIDTH, D_MODEL), jnp.float32) * ATTN_WIDTH ** -0.5
    w_out = jax.random.normal(ks[9], (DEPTH, D_MODEL, D_MODEL), jnp.float32) * D_MODEL ** -0.5
    final_norm_g = 1.0 + 0.02 * jax.random.normal(ks[10], (D_MODEL,), jnp.float32)
    return {"x": x, "meta_tokens": meta_tokens, "norm_g": norm_g, "w_in": w_in,
            "b_forget": b_forget, "pool_w": pool_w, "pool_scale": pool_scale,
            "w_up_pool": w_up_pool, "w_up_attn": w_up_attn, "w_out": w_out,
            "final_norm_g": final_norm_g}


def _fwd_reference(x, meta_tokens, norm_g, w_in, b_forget, pool_w, pool_scale, w_up_pool, w_up_attn, w_out, final_norm_g):
    B = x.shape[0]
    meta = jnp.broadcast_to(meta_tokens.astype(x.dtype)[None], (B, N_META, D_MODEL))
    h = jnp.concatenate([meta, x], axis=1)
    for l in range(DEPTH):
        h = hybrid_layer(h, norm_g[l], w_in[l], b_forget[l], pool_w[l], pool_scale[l],
                         w_up_pool[l], w_up_attn[l], w_out[l])
    h = rmsnorm(h, final_norm_g)
    return h[:, N_META:]


import jax as _jax
import jax.numpy as _jnp

TWIN_FORMAT = 'train_step'
FWD_PARAMS = ['x', 'meta_tokens', 'norm_g', 'w_in', 'b_forget', 'pool_w', 'pool_scale', 'w_up_pool', 'w_up_attn', 'w_out', 'final_norm_g']
TWIN_WEIGHTS = ['meta_tokens', 'norm_g', 'w_in', 'b_forget', 'pool_w', 'pool_scale', 'w_up_pool', 'w_up_attn', 'w_out', 'final_norm_g']
TWIN_DIFF_INPUT = 'x'
TWIN_INPUTS = ['x', 'meta_tokens', 'norm_g', 'w_in', 'b_forget', 'pool_w', 'pool_scale', 'w_up_pool', 'w_up_attn', 'w_out', 'final_norm_g', 'loss_target', 'm_meta_tokens', 'm_norm_g', 'm_w_in', 'm_b_forget', 'm_pool_w', 'm_pool_scale', 'm_w_up_pool', 'm_w_up_attn', 'm_w_out', 'm_final_norm_g', 'v_meta_tokens', 'v_norm_g', 'v_w_in', 'v_b_forget', 'v_pool_w', 'v_pool_scale', 'v_w_up_pool', 'v_w_up_attn', 'v_w_out', 'v_final_norm_g']
TWIN_OUTPUTS = ['loss', 'grad_x', 'grad_meta_tokens', 'grad_norm_g', 'grad_w_in', 'grad_b_forget', 'grad_pool_w', 'grad_pool_scale', 'grad_w_up_pool', 'grad_w_up_attn', 'grad_w_out', 'grad_final_norm_g', 'delta_meta_tokens', 'delta_norm_g', 'delta_w_in', 'delta_b_forget', 'delta_pool_w', 'delta_pool_scale', 'delta_w_up_pool', 'delta_w_up_attn', 'delta_w_out', 'delta_final_norm_g', 'new_m_meta_tokens', 'new_m_norm_g', 'new_m_w_in', 'new_m_b_forget', 'new_m_pool_w', 'new_m_pool_scale', 'new_m_w_up_pool', 'new_m_w_up_attn', 'new_m_w_out', 'new_m_final_norm_g', 'new_v_meta_tokens', 'new_v_norm_g', 'new_v_w_in', 'new_v_b_forget', 'new_v_pool_w', 'new_v_pool_scale', 'new_v_w_up_pool', 'new_v_w_up_attn', 'new_v_w_out', 'new_v_final_norm_g']
TWIN_LEAF_KINDS = {'loss': 'loss', 'grad_x': 'grad_x', 'grad_meta_tokens': 'grad_w', 'grad_norm_g': 'grad_w', 'grad_w_in': 'grad_w', 'grad_b_forget': 'grad_w', 'grad_pool_w': 'grad_w', 'grad_pool_scale': 'grad_w', 'grad_w_up_pool': 'grad_w', 'grad_w_up_attn': 'grad_w', 'grad_w_out': 'grad_w', 'grad_final_norm_g': 'grad_w', 'delta_meta_tokens': 'delta_w', 'delta_norm_g': 'delta_w', 'delta_w_in': 'delta_w', 'delta_b_forget': 'delta_w', 'delta_pool_w': 'delta_w', 'delta_pool_scale': 'delta_w', 'delta_w_up_pool': 'delta_w', 'delta_w_up_attn': 'delta_w', 'delta_w_out': 'delta_w', 'delta_final_norm_g': 'delta_w', 'new_m_meta_tokens': 'new_m', 'new_m_norm_g': 'new_m', 'new_m_w_in': 'new_m', 'new_m_b_forget': 'new_m', 'new_m_pool_w': 'new_m', 'new_m_pool_scale': 'new_m', 'new_m_w_up_pool': 'new_m', 'new_m_w_up_attn': 'new_m', 'new_m_w_out': 'new_m', 'new_m_final_norm_g': 'new_m', 'new_v_meta_tokens': 'new_v', 'new_v_norm_g': 'new_v', 'new_v_w_in': 'new_v', 'new_v_b_forget': 'new_v', 'new_v_pool_w': 'new_v', 'new_v_pool_scale': 'new_v', 'new_v_w_up_pool': 'new_v', 'new_v_w_up_attn': 'new_v', 'new_v_w_out': 'new_v', 'new_v_final_norm_g': 'new_v'}


def _forward(args):
    return _fwd_reference(*[args[k] for k in FWD_PARAMS])


def _output_shape():
    def fwd():
        inp = _fwd_setup_inputs(0)
        return _fwd_reference(*[inp[k] for k in FWD_PARAMS])
    out = _jax.eval_shape(fwd)
    return out.shape, out.dtype

N_MICROBATCH = 1
ADAM_LR = 0.001
ADAM_B1 = 0.9
ADAM_B2 = 0.999
ADAM_EPS = 1e-08
ADAM_WD = 0.01
ADAM_STEP = 10
PER_EXAMPLE_BATCH_AXIS = {'x': 0, 'loss_target': 0}
SHARED_INPUTS = []
_WEIGHT_DTYPES = {'meta_tokens': _jnp.float32, 'norm_g': _jnp.float32, 'w_in': _jnp.float32, 'b_forget': _jnp.float32, 'pool_w': _jnp.float32, 'pool_scale': _jnp.float32, 'w_up_pool': _jnp.float32, 'w_up_attn': _jnp.float32, 'w_out': _jnp.float32, 'final_norm_g': _jnp.float32}
MOMENT_SCALE = {'meta_tokens': 1.342616e-03, 'norm_g': 7.952257e-02, 'w_in': 3.493546e-02, 'b_forget': 7.059849e-02, 'pool_w': 6.843493e-02, 'pool_scale': 6.987243e-02, 'w_up_pool': 4.857428e-02, 'w_up_attn': 1.592652e-02, 'w_out': 5.105222e-02, 'final_norm_g': 3.201590e+01}


def _to_microbatches(a, axis):
    t = _jnp.moveaxis(a, axis, 0)
    t = t.reshape((N_MICROBATCH, t.shape[0] // N_MICROBATCH) + t.shape[1:])
    return _jnp.moveaxis(t, 1, axis + 1)


def setup_inputs(seed: int = 0) -> dict:
    inp = _fwd_setup_inputs(seed)
    key = _jax.random.fold_in(_jax.random.key(seed), 7919)
    shape, _ = _output_shape()
    out = dict(inp)
    out["loss_target"] = _jax.random.normal(_jax.random.fold_in(key, 0), shape, _jnp.float32)
    for i, name in enumerate(TWIN_WEIGHTS):
        w = inp[name].astype(_jnp.float32)
        if MOMENT_SCALE is None:
            s = _jnp.sqrt(_jnp.mean(_jnp.square(w)) + 1e-30)
        else:
            s = MOMENT_SCALE[name]
        km, kv = _jax.random.split(_jax.random.fold_in(key, i + 1))
        out[name] = w
        out["m_" + name] = s * _jax.random.normal(km, w.shape, _jnp.float32)
        out["v_" + name] = (s * s) * _jax.random.uniform(kv, w.shape, _jnp.float32, 0.5, 1.5)
    if N_MICROBATCH > 1:
        for name, axis in PER_EXAMPLE_BATCH_AXIS.items():
            out[name] = _to_microbatches(out[name], axis)
    return {'x': out['x'], 'meta_tokens': out['meta_tokens'], 'norm_g': out['norm_g'], 'w_in': out['w_in'], 'b_forget': out['b_forget'], 'pool_w': out['pool_w'], 'pool_scale': out['pool_scale'], 'w_up_pool': out['w_up_pool'], 'w_up_attn': out['w_up_attn'], 'w_out': out['w_out'], 'final_norm_g': out['final_norm_g'], 'loss_target': out['loss_target'], 'm_meta_tokens': out['m_meta_tokens'], 'm_norm_g': out['m_norm_g'], 'm_w_in': out['m_w_in'], 'm_b_forget': out['m_b_forget'], 'm_pool_w': out['m_pool_w'], 'm_pool_scale': out['m_pool_scale'], 'm_w_up_pool': out['m_w_up_pool'], 'm_w_up_attn': out['m_w_up_attn'], 'm_w_out': out['m_w_out'], 'm_final_norm_g': out['m_final_norm_g'], 'v_meta_tokens': out['v_meta_tokens'], 'v_norm_g': out['v_norm_g'], 'v_w_in': out['v_w_in'], 'v_b_forget': out['v_b_forget'], 'v_pool_w': out['v_pool_w'], 'v_pool_scale': out['v_pool_scale'], 'v_w_up_pool': out['v_w_up_pool'], 'v_w_up_attn': out['v_w_up_attn'], 'v_w_out': out['v_w_out'], 'v_final_norm_g': out['v_final_norm_g']}


def _loss(weights, diff, rest, loss_target):
    with _jax.named_scope("forward"):
        args = {**rest, TWIN_DIFF_INPUT: diff, **{k: w.astype(_WEIGHT_DTYPES[k]) for k, w in weights.items()}}
        y = _forward(args)
    with _jax.named_scope("loss_head"):
        err = _jnp.square(y.astype(_jnp.float32) - loss_target)
        return 0.5 * _jnp.sum(_jnp.mean(err, axis=-1)) if err.ndim else 0.5 * err


def _adamw(w, g, m, v):
    m = ADAM_B1 * m + (1.0 - ADAM_B1) * g
    v = ADAM_B2 * v + (1.0 - ADAM_B2) * _jnp.square(g)
    m_hat = m / (1.0 - ADAM_B1 ** ADAM_STEP)
    v_hat = v / (1.0 - ADAM_B2 ** ADAM_STEP)
    delta = -ADAM_LR * (m_hat / (_jnp.sqrt(v_hat) + ADAM_EPS) + ADAM_WD * w)
    return delta, m, v


def reference(x, meta_tokens, norm_g, w_in, b_forget, pool_w, pool_scale, w_up_pool, w_up_attn, w_out, final_norm_g, loss_target, m_meta_tokens, m_norm_g, m_w_in, m_b_forget, m_pool_w, m_pool_scale, m_w_up_pool, m_w_up_attn, m_w_out, m_final_norm_g, v_meta_tokens, v_norm_g, v_w_in, v_b_forget, v_pool_w, v_pool_scale, v_w_up_pool, v_w_up_attn, v_w_out, v_final_norm_g):
    given = dict(x=x, meta_tokens=meta_tokens, norm_g=norm_g, w_in=w_in, b_forget=b_forget, pool_w=pool_w, pool_scale=pool_scale, w_up_pool=w_up_pool, w_up_attn=w_up_attn, w_out=w_out, final_norm_g=final_norm_g, loss_target=loss_target, m_meta_tokens=m_meta_tokens, m_norm_g=m_norm_g, m_w_in=m_w_in, m_b_forget=m_b_forget, m_pool_w=m_pool_w, m_pool_scale=m_pool_scale, m_w_up_pool=m_w_up_pool, m_w_up_attn=m_w_up_attn, m_w_out=m_w_out, m_final_norm_g=m_final_norm_g, v_meta_tokens=v_meta_tokens, v_norm_g=v_norm_g, v_w_in=v_w_in, v_b_forget=v_b_forget, v_pool_w=v_pool_w, v_pool_scale=v_pool_scale, v_w_up_pool=v_w_up_pool, v_w_up_attn=v_w_up_attn, v_w_out=v_w_out, v_final_norm_g=v_final_norm_g)
    weights = {n: given[n] for n in TWIN_WEIGHTS}
    shared = {n: given[n] for n in SHARED_INPUTS}
    per_example = {n: given[n] for n in ['x']}
    grad_fn = _jax.value_and_grad(_loss, argnums=(0, 1))

    def one_microbatch(ex, loss_target):
        ex = dict(ex)
        diff = ex.pop(TWIN_DIFF_INPUT)
        return grad_fn(weights, diff, {**shared, **ex}, loss_target)

    if N_MICROBATCH == 1:
        loss, (grad_w, grad_x) = one_microbatch(per_example, given["loss_target"])
    else:
        def body(carry, xs):
            loss_sum, grad_sum = carry
            l_k, (gw_k, gx_k) = one_microbatch(xs[0], xs[1])
            with _jax.named_scope("update"):
                return (loss_sum + l_k, _jax.tree.map(_jnp.add, grad_sum, gw_k)), gx_k

        init = (_jnp.zeros((), _jnp.float32), _jax.tree.map(_jnp.zeros_like, weights))
        (loss, grad_w), grad_x = _jax.lax.scan(body, init, (per_example, given["loss_target"]))
    with _jax.named_scope("update"):
        delta_w, new_m, new_v = {}, {}, {}
        for n in TWIN_WEIGHTS:
            delta_w[n], new_m[n], new_v[n] = _adamw(weights[n], grad_w[n], given["m_" + n], given["v_" + n])
    return (loss, grad_x, *[grad_w[n] for n in TWIN_WEIGHTS], *[delta_w[n] for n in TWIN_WEIGHTS],
            *[new_m[n] for n in TWIN_WEIGHTS], *[new_v[n] for n in TWIN_WEIGHTS])
```

```python
import functools

import jax
import jax.numpy as jnp
from jax import lax
from jax.experimental import pallas as pl
from jax.experimental.pallas import tpu as pltpu

F32 = jnp.float32
BF16 = jnp.bfloat16

D_MODEL = 1024
N_META = 16
POOL_WIDTH = 512
ATTN_WIDTH = 512
N_HEADS = 8
HEAD_DIM = 64
POOL_WINDOWS = (2, 4, 8, 16)
POOL_GROUP = 128
MAX_WINDOW = 16
RMS_EPS = 1e-6
N_MAIN = 5120
N_BEFORE_F = 3072
N_DEV = 8
LANES = 128

ROW_TILE = 256
ATT_TILE = 256
PAD = ROW_TILE - N_META
VMEM_LIMIT = 56 * 1024 * 1024

ADAM_LR = 0.001
ADAM_B1 = 0.9
ADAM_B2 = 0.999
ADAM_EPS = 1e-08
ADAM_WD = 0.01
ADAM_STEP = 10

NEG = -1e30
MESH = pl.DeviceIdType.MESH


def _params(sem=None):
    kw = dict(vmem_limit_bytes=VMEM_LIMIT)
    if sem is not None:
        kw["dimension_semantics"] = sem
    return pltpu.CompilerParams(**kw)


def _const(shape, block_index=None):
    idx = block_index or (0,) * len(shape)
    return pl.BlockSpec(shape, lambda i: idx, pipeline_mode=pl.Buffered(1))


def _sigmoid(x):
    return jax.nn.sigmoid(x)


def _dot(a, b):
    return jnp.dot(a, b, preferred_element_type=F32)


def _dot_nt(a, b):
    return lax.dot_general(a, b, (((1,), (1,)), ((), ())), preferred_element_type=F32)


def _dot_tn(a, b):
    return lax.dot_general(a, b, (((0,), (0,)), ((), ())), preferred_element_type=F32)


def _pool_counts(first_row, rows):
    row = first_row + lax.broadcasted_iota(jnp.int32, (rows, 1), 0)
    pos1 = row - PAD + 1
    return [jnp.clip(pos1, 1, w).astype(F32) for w in POOL_WINDOWS]


def _pool_means(u_ext, u, counts):
    rows = u.shape[0]
    out = []
    for g, w in enumerate(POOL_WINDOWS):
        s = u_ext[:, POOL_GROUP * g:POOL_GROUP * (g + 1)]
        sh = 1
        while sh < w:
            s = s + pltpu.roll(s, sh, axis=0)
            sh *= 2
        out.append(s[MAX_WINDOW:MAX_WINDOW + rows, :] / counts[g] - u[:, POOL_GROUP * g:POOL_GROUP * (g + 1)])
    return out


def _exchange(name, gathers, scatters):
    arrays = list(gathers) + list(scatters)
    n_g, n = len(gathers), len(arrays)

    def body(*refs):
        ins, outs = refs[:n], refs[n:2 * n]
        send_sems, recv_sems, local_sems = refs[2 * n:]
        x, y, c = lax.axis_index("x"), lax.axis_index("y"), lax.axis_index("c")
        me = 4 * x + 2 * y + c
        copies = []
        for a in range(n):
            for r in range(1, N_DEV):
                px = 1 - x if r & 4 else x
                py = 1 - y if r & 2 else y
                pc = 1 - c if r & 1 else c
                src = ins[a] if a < n_g else ins[a].at[4 * px + 2 * py + pc]
                cp = pltpu.make_async_remote_copy(
                    src_ref=src, dst_ref=outs[a].at[me],
                    send_sem=send_sems.at[a, r - 1], recv_sem=recv_sems.at[a, r - 1],
                    device_id=(px, py, pc), device_id_type=MESH)
                cp.start()
                copies.append(cp)
            src = ins[a] if a < n_g else ins[a].at[me]
            mine = pltpu.make_async_copy(src, outs[a].at[me], local_sems.at[a])
            mine.start()
            copies.append(mine)
        for cp in copies:
            cp.wait()

    hbm = pl.BlockSpec(memory_space=pl.ANY)
    out_shape = [jax.ShapeDtypeStruct((N_DEV,) + a.shape, a.dtype) for a in gathers]
    out_shape += [jax.ShapeDtypeStruct(a.shape, a.dtype) for a in scatters]
    return pl.pallas_call(
        body, name=name, out_shape=out_shape,
        in_specs=[hbm] * n, out_specs=[hbm] * n,
        scratch_shapes=[pltpu.SemaphoreType.DMA((n, N_DEV - 1)), pltpu.SemaphoreType.DMA((n, N_DEV - 1)),
                        pltpu.SemaphoreType.DMA((n,))],
    )(*arrays)


def _forward_in(x, tile0, norm_g, w_main, w_f, b_f, pool_w, pool_scale, w_up_pool):
    seq = x.shape[0]
    nt = seq // ROW_TILE + 1
    lp = nt * ROW_TILE
    tm = ROW_TILE

    def body(x_ref, t0_ref, g_ref, wa_ref, wf_ref, bf_ref, pw_ref, sc_ref, wup_ref,
             h_ref, u_ref, zp_ref, q_ref, k_ref, v_ref, cq_ref, ck_ref, sn_ref, ap_ref,
             uext_ref, carry_ref):
        i = pl.program_id(0)

        @pl.when(i == 0)
        def _():
            uext_ref[...] = jnp.zeros_like(uext_ref)
            carry_ref[...] = jnp.zeros_like(carry_ref)

        xt = jnp.where(i == 0, t0_ref[...], x_ref[...])
        r = lax.rsqrt(jnp.mean(xt * xt, axis=-1, keepdims=True) + RMS_EPS)
        h = (xt * r * g_ref[...]).astype(BF16)
        h_ref[...] = h
        pa = _dot(h, wa_ref[...])
        u = pa[:, :512]
        zp = pa[:, 512:1024]
        u_ref[...] = u
        zp_ref[...] = zp
        q_ref[...] = (pa[:, 1024:1536] * 0.125).astype(BF16)
        k_ref[...] = pa[:, 1536:2048].astype(BF16)
        v_ref[...] = pa[:, 2048:2560].astype(BF16)

        uext_ref[0:MAX_WINDOW, :] = uext_ref[tm:tm + MAX_WINDOW, :]
        uext_ref[MAX_WINDOW:, :] = u
        counts = _pool_counts(i * tm, tm)
        ps = _pool_means(uext_ref[...], u, counts)
        ppw = jnp.concatenate([_dot(ps[g].astype(BF16), pw_ref[g]) for g in range(4)], axis=1)
        y_pool = ppw * sc_ref[...] * (zp * _sigmoid(zp))
        ap_ref[...] = _dot(y_pool.astype(BF16), wup_ref[...]).astype(BF16)

        fl = _dot(h, wf_ref[...]) + bf_ref[...]
        row = i * tm + lax.broadcasted_iota(jnp.int32, (tm, LANES), 0)
        rloc = lax.broadcasted_iota(jnp.int32, (tm, LANES), 0)
        lane = lax.broadcasted_iota(jnp.int32, (tm, LANES), 1)
        live = (row >= PAD) & (lane < N_HEADS)
        logf = jnp.minimum(fl, 0.0) - jnp.log1p(jnp.exp(-jnp.abs(fl)))
        cs = jnp.where(live, logf, 0.0)
        sh = 1
        while sh < tm:
            cs = cs + jnp.where(rloc >= sh, pltpu.roll(cs, sh, axis=0), 0.0)
            sh *= 2
        cs = cs + carry_ref[...]
        carry_ref[...] = cs[tm - 1:tm, :]
        cq_ref[...] = cs
        ck_ref[...] = cs.T[:N_HEADS, :]
        sn_ref[...] = jnp.where(live, _sigmoid(-fl), 0.0)

    row_f32 = lambda w: pl.BlockSpec((tm, w), lambda i: (i, 0))
    out_shape = [
        jax.ShapeDtypeStruct((lp, D_MODEL), BF16),
        jax.ShapeDtypeStruct((lp, POOL_WIDTH), F32),
        jax.ShapeDtypeStruct((lp, POOL_WIDTH), F32),
        jax.ShapeDtypeStruct((lp, ATTN_WIDTH), BF16),
        jax.ShapeDtypeStruct((lp, ATTN_WIDTH), BF16),
        jax.ShapeDtypeStruct((lp, ATTN_WIDTH), BF16),
        jax.ShapeDtypeStruct((lp, LANES), F32),
        jax.ShapeDtypeStruct((N_HEADS, lp), F32),
        jax.ShapeDtypeStruct((lp, LANES), F32),
        jax.ShapeDtypeStruct((lp, D_MODEL), BF16),
    ]
    out_specs = [row_f32(D_MODEL), row_f32(512), row_f32(512), row_f32(512), row_f32(512), row_f32(512),
                 row_f32(LANES), pl.BlockSpec((N_HEADS, tm), lambda i: (0, i)), row_f32(LANES), row_f32(D_MODEL)]
    in_specs = [
        pl.BlockSpec((tm, D_MODEL), lambda i: (jnp.maximum(i - 1, 0), 0)),
        _const((tm, D_MODEL)), _const((1, D_MODEL)),
        _const((D_MODEL, 2560)), _const((D_MODEL, LANES)), _const((1, LANES)),
        _const((4, POOL_GROUP, POOL_GROUP)), _const((1, POOL_WIDTH)), _const((POOL_WIDTH, D_MODEL)),
    ]
    return pl.pallas_call(
        body, name="forward_in", grid=(nt,), out_shape=out_shape, in_specs=in_specs, out_specs=out_specs,
        scratch_shapes=[pltpu.VMEM((tm + MAX_WINDOW, POOL_WIDTH), F32), pltpu.VMEM((1, LANES), F32)],
        compiler_params=_params(("arbitrary",)),
    )(x, tile0, norm_g, w_main, w_f, b_f, pool_w, pool_scale, w_up_pool)


def _valid(rows, cols):
    return (cols <= rows) & ((cols >= PAD) | (cols == rows))


def _head_column(block, head):
    lane = lax.broadcasted_iota(jnp.int32, block.shape, 1)
    return jnp.sum(jnp.where(lane == head, block, 0.0), axis=1, keepdims=True)


def _attention_forward(q, k, v, cq, ck):
    lp = q.shape[0]
    tb = ATT_TILE
    nb = lp // tb

    def body(q_ref, k_ref, v_ref, cq_ref, ck_ref, o_ref, lse_ref):
        hp = pl.program_id(0)
        lane = lax.broadcasted_iota(jnp.int32, (1, LANES), 1)

        def q_block(qi, _):
            q0 = pl.multiple_of(qi * tb, tb)
            qb = q_ref[pl.ds(q0, tb), :]
            cqb = cq_ref[pl.ds(q0, tb), :]
            rows = q0 + lax.broadcasted_iota(jnp.int32, (tb, 1), 0)
            outs, lses = [], []
            for e in range(2):
                head = 2 * hp + e
                sel = (lane >= HEAD_DIM * e) & (lane < HEAD_DIM * (e + 1))
                qe = jnp.where(sel, qb, jnp.zeros_like(qb))
                cqe = _head_column(cqb, head)

                def k_block(kj, carry):
                    m, l, acc = carry
                    k0 = pl.multiple_of(kj * tb, tb)
                    kb = k_ref[pl.ds(k0, tb), :]
                    vb = v_ref[pl.ds(k0, tb), :]
                    cke = ck_ref[pl.ds(head, 1), pl.ds(k0, tb)]
                    s = _dot_nt(qe, kb) + cqe - cke
                    cols = k0 + lax.broadcasted_iota(jnp.int32, (1, tb), 1)
                    s = jnp.where(_valid(rows, cols), s, NEG)
                    m_new = jnp.maximum(m, jnp.max(s, axis=1, keepdims=True))
                    p = jnp.exp(s - m_new)
                    alpha = jnp.exp(m - m_new)
                    l = alpha * l + jnp.sum(p, axis=1, keepdims=True)
                    acc = alpha * acc + _dot(p.astype(BF16), vb)
                    return m_new, l, acc

                init = (jnp.full((tb, 1), NEG, F32), jnp.zeros((tb, 1), F32), jnp.zeros((tb, LANES), F32))
                m, l, acc = lax.fori_loop(0, qi + 1, k_block, init)
                outs.append(acc / l)
                lses.append(m + jnp.log(l))
            o_ref[pl.ds(q0, tb), :] = jnp.where(lane < HEAD_DIM, outs[0], outs[1]).astype(BF16)
            lse_ref[pl.ds(q0, tb), :] = jnp.where(lane == 0, lses[0], jnp.where(lane == 1, lses[1], 0.0))
            return 0

        lax.fori_loop(0, nb, q_block, 0)

    pair = pl.BlockSpec((lp, LANES), lambda hp: (0, hp))
    return pl.pallas_call(
        body, name="attention_forward", grid=(N_HEADS // 2,),
        out_shape=[jax.ShapeDtypeStruct((lp, ATTN_WIDTH), BF16), jax.ShapeDtypeStruct((lp, ATTN_WIDTH), F32)],
        in_specs=[pair, pair, pair, _const((lp, LANES)), _const((N_HEADS, lp))],
        out_specs=[pair, pair],
        compiler_params=_params(("arbitrary",)),
    )(q, k, v, cq, ck)


def _attention_backward(q, k, v, do, o, lse, cq, ck):
    lp = q.shape[0]
    tb = ATT_TILE
    nb = lp // tb

    def body(q_ref, k_ref, v_ref, do_ref, o_ref, lse_ref, cq_ref, ck_ref,
             dq_ref, dk_ref, dv_ref, dc_ref, dr_ref, dq_acc, delta_ref):
        hp = pl.program_id(0)
        lane = lax.broadcasted_iota(jnp.int32, (1, LANES), 1)

        def delta_block(qi, _):
            q0 = pl.multiple_of(qi * tb, tb)
            dd = do_ref[pl.ds(q0, tb), :].astype(F32) * o_ref[pl.ds(q0, tb), :].astype(F32)
            d0 = jnp.sum(jnp.where(lane < HEAD_DIM, dd, 0.0), axis=1, keepdims=True)
            d1 = jnp.sum(jnp.where(lane >= HEAD_DIM, dd, 0.0), axis=1, keepdims=True)
            delta_ref[pl.ds(q0, tb), :] = jnp.where(lane == 0, d0, jnp.where(lane == 1, d1, 0.0))
            return 0

        lax.fori_loop(0, nb, delta_block, 0)
        dq_acc[...] = jnp.zeros_like(dq_acc)
        dr_ref[...] = jnp.zeros_like(dr_ref)

        def k_block(kj, _):
            k0 = pl.multiple_of(kj * tb, tb)
            kb = k_ref[pl.ds(k0, tb), :]
            vb = v_ref[pl.ds(k0, tb), :]
            cols = k0 + lax.broadcasted_iota(jnp.int32, (1, tb), 1)
            dk_acc = jnp.zeros((tb, LANES), F32)
            dv_acc = jnp.zeros((tb, LANES), F32)
            for e in range(2):
                head = 2 * hp + e
                sel = (lane >= HEAD_DIM * e) & (lane < HEAD_DIM * (e + 1))
                ke = jnp.where(sel, kb, jnp.zeros_like(kb))
                cke = ck_ref[pl.ds(head, 1), pl.ds(k0, tb)]

                def q_block(qi, carry):
                    dk_a, dv_a, dc_a = carry
                    q0 = pl.multiple_of(qi * tb, tb)
                    qb = q_ref[pl.ds(q0, tb), :]
                    dob = do_ref[pl.ds(q0, tb), :]
                    qe = jnp.where(sel, qb, jnp.zeros_like(qb))
                    doe = jnp.where(sel, dob, jnp.zeros_like(dob))
                    cqe = _head_column(cq_ref[pl.ds(q0, tb), :], head)
                    lse_e = lse_ref[pl.ds(q0, tb), e:e + 1]
                    delta_e = delta_ref[pl.ds(q0, tb), e:e + 1]
                    rows = q0 + lax.broadcasted_iota(jnp.int32, (tb, 1), 0)
                    s = _dot_nt(qe, kb) + cqe - cke
                    p = jnp.where(_valid(rows, cols), jnp.exp(s - lse_e), 0.0)
                    dp = _dot_nt(doe, vb)
                    ds = p * (dp - delta_e)
                    pb = p.astype(BF16)
                    dsb = ds.astype(BF16)
                    dv_a = dv_a + _dot_tn(pb, doe)
                    dk_a = dk_a + _dot_tn(dsb, qe)
                    dq_acc[pl.ds(q0, tb), :] += _dot(dsb, ke)
                    dc_a = dc_a + jnp.sum(ds, axis=0, keepdims=True)
                    dr_ref[pl.ds(q0, tb), :] += jnp.where(lane == e, jnp.sum(ds, axis=1, keepdims=True), 0.0)
                    return dk_a, dv_a, dc_a

                dk_acc, dv_acc, dc = lax.fori_loop(kj, nb, q_block, (dk_acc, dv_acc, jnp.zeros((1, tb), F32)))
                dc_ref[pl.ds(head, 1), pl.ds(k0, tb)] = -dc
            dk_ref[pl.ds(k0, tb), :] = dk_acc.astype(BF16)
            dv_ref[pl.ds(k0, tb), :] = dv_acc.astype(BF16)
            return 0

        lax.fori_loop(0, nb, k_block, 0)
        dq_ref[...] = (dq_acc[...] * 0.125).astype(BF16)

    pair = pl.BlockSpec((lp, LANES), lambda hp: (0, hp))
    wide = jax.ShapeDtypeStruct((lp, ATTN_WIDTH), BF16)
    return pl.pallas_call(
        body, name="attention_backward", grid=(N_HEADS // 2,),
        out_shape=[wide, wide, wide, jax.ShapeDtypeStruct((N_HEADS, lp), F32),
                   jax.ShapeDtypeStruct((lp, ATTN_WIDTH), F32)],
        in_specs=[pair, pair, pair, pair, pair, pair, _const((lp, LANES)), _const((N_HEADS, lp))],
        out_specs=[pair, pair, pair, pl.BlockSpec((N_HEADS, lp), lambda hp: (0, 0)), pair],
        scratch_shapes=[pltpu.VMEM((lp, LANES), F32), pltpu.VMEM((lp, LANES), F32)],
        compiler_params=_params(("arbitrary",)),
    )(q, k, v, do, o, lse, cq, ck)


def _middle(x, target, h, o, a_pool, u, zp, w_main, w_up_pool, w_up_attn, w_out, pool_w, pool_scale, final_g):
    seq = x.shape[0]
    tm = ROW_TILE
    nt = seq // tm + 1
    lp = nt * tm
    halo_blocks = tm // MAX_WINDOW

    def body(x_ref, t_ref, h_ref, o_ref, ap_ref, u_ref, uh_ref, zp_ref,
             wc_ref, wupp_ref, wupa_ref, wout_ref, pw_ref, sc_ref, gf_ref,
             dh2_ref, mg_ref, yp_ref, ya_ref, dap_ref, daa_ref, do_ref, dza_ref, dgp_ref, dga_ref, dzp_ref, dpn_ref,
             loss_ref, dgf_ref, dsc_ref, dpw_ref):
        i = pl.program_id(0)
        tiles = (dh2_ref, mg_ref, yp_ref, ya_ref, dap_ref, daa_ref, do_ref, dza_ref, dgp_ref, dga_ref, dzp_ref, dpn_ref)

        @pl.when(i == 0)
        def _():
            for ref in tiles + (loss_ref, dgf_ref, dsc_ref, dpw_ref):
                ref[...] = jnp.zeros_like(ref)

        @pl.when(i > 0)
        def _():
            xt = x_ref[...]
            hb = h_ref[...]
            pc = _dot(hb, wc_ref[...])
            za, gp, ga = pc[:, :512], pc[:, 512:1536], pc[:, 1536:]
            of = o_ref[...].astype(F32)
            sza = _sigmoid(za)
            silu_za = za * sza
            ya = (of * silu_za).astype(BF16)
            ya_ref[...] = ya
            aa = _dot(ya, wupa_ref[...])
            ap = ap_ref[...].astype(F32)
            sgp, sga = _sigmoid(gp), _sigmoid(ga)
            mg = (sgp * ap + sga * aa).astype(BF16)
            mg_ref[...] = mg
            h2 = xt + _dot(mg, wout_ref[...])
            r2 = lax.rsqrt(jnp.mean(h2 * h2, axis=-1, keepdims=True) + RMS_EPS)
            h2n = h2 * r2
            gf = gf_ref[...]
            diff = h2n * gf - t_ref[...]
            loss_ref[...] += 0.5 * jnp.sum(jnp.mean(diff * diff, axis=-1, keepdims=True), axis=0, keepdims=True)
            dy = diff * (1.0 / D_MODEL)
            dgf_ref[...] += jnp.sum(dy * h2n, axis=0, keepdims=True)
            dyg = dy * gf
            dh2 = r2 * (dyg - h2n * jnp.mean(dyg * h2n, axis=-1, keepdims=True))
            dh2_ref[...] = dh2
            dmg = _dot_nt(dh2.astype(BF16), wout_ref[...])
            dap = (dmg * sgp).astype(BF16)
            daa = (dmg * sga).astype(BF16)
            dap_ref[...] = dap
            daa_ref[...] = daa
            dgp_ref[...] = (dmg * ap * sgp * (1.0 - sgp)).astype(BF16)
            dga_ref[...] = (dmg * aa * sga * (1.0 - sga)).astype(BF16)
            dyp = _dot_nt(dap, wupp_ref[...])
            dya = _dot_nt(daa, wupa_ref[...])
            do_ref[...] = (dya * silu_za).astype(BF16)
            dza_ref[...] = (dya * of * (sza * (1.0 + za * (1.0 - sza)))).astype(BF16)

            u = u_ref[...]
            zp = zp_ref[...]
            counts = _pool_counts(i * tm, tm)
            ps = _pool_means(jnp.concatenate([uh_ref[...], u], axis=0), u, counts)
            pbs = [p.astype(BF16) for p in ps]
            ppw = jnp.concatenate([_dot(pbs[g], pw_ref[g]) for g in range(4)], axis=1)
            sc = sc_ref[...]
            szp = _sigmoid(zp)
            silu_zp = zp * szp
            ypre = ppw * sc
            yp_ref[...] = (ypre * silu_zp).astype(BF16)
            dypre = dyp * silu_zp
            dzp_ref[...] = (dyp * ypre * (szp * (1.0 + zp * (1.0 - szp)))).astype(BF16)
            dsc_ref[...] += jnp.sum(dypre * ppw, axis=0, keepdims=True)
            dppw = (dypre * sc).astype(BF16)
            dpns = []
            for g in range(4):
                dg = dppw[:, POOL_GROUP * g:POOL_GROUP * (g + 1)]
                dpw_ref[g] += _dot_tn(pbs[g], dg)
                dpns.append(_dot_nt(dg, pw_ref[g]) / counts[g])
            dpn_ref[...] = jnp.concatenate(dpns, axis=1)

    real = lambda w: pl.BlockSpec((tm, w), lambda i: (jnp.maximum(i - 1, 0), 0))
    row = lambda w: pl.BlockSpec((tm, w), lambda i: (i, 0))
    in_specs = [
        real(D_MODEL), real(D_MODEL), row(D_MODEL), row(512), row(D_MODEL), row(512),
        pl.BlockSpec((MAX_WINDOW, 512), lambda i: (jnp.maximum(i * halo_blocks - 1, 0), 0)), row(512),
        _const((D_MODEL, 2560), (0, 1)), _const((POOL_WIDTH, D_MODEL)), _const((ATTN_WIDTH, D_MODEL)),
        _const((D_MODEL, D_MODEL)), _const((4, POOL_GROUP, POOL_GROUP)), _const((1, POOL_WIDTH)), _const((1, D_MODEL)),
    ]
    sd = jax.ShapeDtypeStruct
    out_shape = [
        sd((lp, D_MODEL), F32),
        sd((lp, D_MODEL), BF16),
        sd((lp, 512), BF16),
        sd((lp, 512), BF16),
        sd((lp, D_MODEL), BF16),
        sd((lp, D_MODEL), BF16),
        sd((lp, 512), BF16),
        sd((lp, 512), BF16),
        sd((lp, D_MODEL), BF16),
        sd((lp, D_MODEL), BF16),
        sd((lp, 512), BF16),
        sd((lp, 512), F32),
        sd((1, 1), F32),
        sd((1, D_MODEL), F32),
        sd((1, 512), F32),
        sd((4, POOL_GROUP, POOL_GROUP), F32),
    ]
    keep = lambda shape: pl.BlockSpec(shape, lambda i: (0,) * len(shape))
    out_specs = [row(D_MODEL), row(D_MODEL), row(512), row(512), row(D_MODEL), row(D_MODEL), row(512), row(512),
                 row(D_MODEL), row(D_MODEL), row(512), row(512),
                 keep((1, 1)), keep((1, D_MODEL)), keep((1, 512)), keep((4, POOL_GROUP, POOL_GROUP))]
    return pl.pallas_call(
        body, name="middle", grid=(nt,), out_shape=out_shape, in_specs=in_specs, out_specs=out_specs,
        compiler_params=_params(("arbitrary",)),
    )(x, target, h, o, a_pool, u, u, zp, w_main, w_up_pool, w_up_attn, w_out, pool_w, pool_scale, final_g)


def _backward_in(x, tile0, norm_g, dh2, dpn, dzp, dq, dk, dv, dza, dgp, dga, dc, dr, sneg, w_main, w_f):
    seq = x.shape[0]
    tm = ROW_TILE
    nt = seq // tm + 1
    lp = nt * tm
    halo_blocks = tm // MAX_WINDOW
    last_halo = lp // MAX_WINDOW - 1

    def body(x_ref, t0_ref, g_ref, dh2_ref, dpn_ref, dpnh_ref, dzp_ref, dq_ref, dk_ref, dv_ref, dza_ref,
             dgp_ref, dga_ref, dc_ref, dr_ref, sn_ref, wm_ref, wf_ref,
             dproj_ref, df_ref, gx_ref, gmeta_ref, dg_ref, dbf_ref, carry_ref):
        i = pl.program_id(0)
        t = nt - 1 - i

        @pl.when(i == 0)
        def _():
            carry_ref[...] = jnp.zeros_like(carry_ref)
            dg_ref[...] = jnp.zeros_like(dg_ref)
            dbf_ref[...] = jnp.zeros_like(dbf_ref)

        dpn_t = dpn_ref[...]
        ahead = jnp.where(i == 0, jnp.zeros_like(dpnh_ref), dpnh_ref[...])
        ext = jnp.concatenate([dpn_t, ahead], axis=0)
        counts = _pool_counts(t * tm, tm)
        for g, w in enumerate(POOL_WINDOWS):
            s = ext[:, POOL_GROUP * g:POOL_GROUP * (g + 1)]
            sh = 1
            while sh < w:
                s = s + pltpu.roll(s, tm + MAX_WINDOW - sh, axis=0)
                sh *= 2
            du = s[:tm, :] - dpn_t[:, POOL_GROUP * g:POOL_GROUP * (g + 1)] * counts[g]
            dproj_ref[:, POOL_GROUP * g:POOL_GROUP * (g + 1)] = du.astype(BF16)
        dproj_ref[:, 512:1024] = dzp_ref[...]
        dproj_ref[:, 1024:1536] = dq_ref[...]
        dproj_ref[:, 1536:2048] = dk_ref[...]
        dproj_ref[:, 2048:2560] = dv_ref[...]
        dproj_ref[:, 2560:3072] = dza_ref[...]
        dproj_ref[:, 3072:4096] = dgp_ref[...]
        dproj_ref[:, 4096:5120] = dga_ref[...]

        dct = jnp.concatenate([dc_ref[...], jnp.zeros((LANES - N_HEADS, tm), F32)], axis=0).T
        for hp in range(N_HEADS // 2):
            rows_hp = dr_ref[:, LANES * hp:LANES * (hp + 1)]
            dct = dct + (pltpu.roll(rows_hp, 2 * hp, axis=1) if hp else rows_hp)
        rloc = lax.broadcasted_iota(jnp.int32, (tm, LANES), 0)
        sh = 1
        while sh < tm:
            dct = dct + jnp.where(rloc + sh < tm, pltpu.roll(dct, tm - sh, axis=0), 0.0)
            sh *= 2
        dct = dct + carry_ref[...]
        carry_ref[...] = dct[0:1, :]
        df = dct * sn_ref[...]
        dbf_ref[...] += jnp.sum(df, axis=0, keepdims=True)
        dfb = df.astype(BF16)
        df_ref[...] = dfb

        dh = _dot_nt(dproj_ref[...], wm_ref[...]) + _dot_nt(dfb, wf_ref[...])
        xt = jnp.where(t == 0, t0_ref[...], x_ref[...])
        r = lax.rsqrt(jnp.mean(xt * xt, axis=-1, keepdims=True) + RMS_EPS)
        xn = xt * r
        dg_ref[...] += jnp.sum(dh * xn, axis=0, keepdims=True)
        dhg = dh * g_ref[...]
        dx = dh2_ref[...] + r * (dhg - xn * jnp.mean(dhg * xn, axis=-1, keepdims=True))

        @pl.when(t > 0)
        def _():
            gx_ref[...] = dx

        @pl.when(t == 0)
        def _():
            gmeta_ref[...] = dx[PAD:, :]

    rev = lambda w: pl.BlockSpec((tm, w), lambda i: (nt - 1 - i, 0))
    real = pl.BlockSpec((tm, D_MODEL), lambda i: (jnp.maximum(nt - 2 - i, 0), 0))
    in_specs = [
        real, _const((tm, D_MODEL)), _const((1, D_MODEL)), rev(D_MODEL), rev(512),
        pl.BlockSpec((MAX_WINDOW, 512), lambda i: (jnp.minimum((nt - i) * halo_blocks, last_halo), 0)),
        rev(512), rev(512), rev(512), rev(512), rev(512), rev(D_MODEL), rev(D_MODEL),
        pl.BlockSpec((N_HEADS, tm), lambda i: (0, nt - 1 - i)), rev(512), rev(LANES),
        _const((D_MODEL, N_MAIN)), _const((D_MODEL, LANES)),
    ]
    sd = jax.ShapeDtypeStruct
    out_shape = [sd((lp, N_MAIN), BF16), sd((lp, LANES), BF16), sd((seq, D_MODEL), F32), sd((N_META, D_MODEL), F32),
                 sd((1, D_MODEL), F32), sd((1, LANES), F32)]
    keep = lambda shape: pl.BlockSpec(shape, lambda i: (0,) * len(shape))
    out_specs = [rev(N_MAIN), rev(LANES), real, keep((N_META, D_MODEL)), keep((1, D_MODEL)), keep((1, LANES))]
    return pl.pallas_call(
        body, name="backward_in", grid=(nt,), out_shape=out_shape, in_specs=in_specs, out_specs=out_specs,
        scratch_shapes=[pltpu.VMEM((1, LANES), F32)],
        compiler_params=_params(("arbitrary",)),
    )(x, tile0, norm_g, dh2, dpn, dpn, dzp, dq, dk, dv, dza, dgp, dga, dc, dr, sneg, w_main, w_f)


def _matmul_tn(name, a, b, tn):
    lp, m = a.shape
    n = b.shape[1]

    def body(a_ref, b_ref, c_ref):
        c_ref[...] = _dot_tn(a_ref[...].astype(BF16), b_ref[...].astype(BF16))

    return pl.pallas_call(
        body, name=name, grid=(n // tn,), out_shape=jax.ShapeDtypeStruct((m, n), F32),
        in_specs=[_const((lp, m)), pl.BlockSpec((lp, tn), lambda j: (0, j))],
        out_specs=pl.BlockSpec((m, tn), lambda j: (0, j)),
        compiler_params=_params(("arbitrary",)),
    )(a, b)


def _adamw(name, parts, w, m, v, rows):
    r, c = w.shape

    def body(p_ref, w_ref, m_ref, v_ref, g_ref, d_ref, mo_ref, vo_ref):
        g = p_ref[0].astype(F32)
        for s in range(1, N_DEV):
            g = g + p_ref[s].astype(F32)
        m_new = ADAM_B1 * m_ref[...] + (1.0 - ADAM_B1) * g
        v_new = ADAM_B2 * v_ref[...] + (1.0 - ADAM_B2) * (g * g)
        m_hat = m_new / (1.0 - ADAM_B1 ** ADAM_STEP)
        v_hat = v_new / (1.0 - ADAM_B2 ** ADAM_STEP)
        g_ref[...] = g
        d_ref[...] = -ADAM_LR * (m_hat / (jnp.sqrt(v_hat) + ADAM_EPS) + ADAM_WD * w_ref[...])
        mo_ref[...] = m_new
        vo_ref[...] = v_new

    blk = pl.BlockSpec((rows, c), lambda i: (i, 0))
    return pl.pallas_call(
        body, name=name, grid=(r // rows,), out_shape=[jax.ShapeDtypeStruct((r, c), F32)] * 4,
        in_specs=[pl.BlockSpec((N_DEV, rows, c), lambda i: (0, i, 0)), blk, blk, blk],
        out_specs=[blk] * 4,
        compiler_params=_params(("arbitrary",)),
    )(parts, w, m, v)


def _columns_to_slots(a):
    r, c8 = a.shape
    return a.reshape(r, N_DEV, c8 // N_DEV).transpose(1, 0, 2)


def _slots_to_columns(a):
    n, r, c = a.shape
    return a.transpose(1, 0, 2).reshape(r, n * c)


def kernel(x, meta_tokens, norm_g, w_in, b_forget, pool_w, pool_scale, w_up_pool, w_up_attn, w_out, final_norm_g, loss_target, m_meta_tokens, m_norm_g, m_w_in, m_b_forget, m_pool_w, m_pool_scale, m_w_up_pool, m_w_up_attn, m_w_out, m_final_norm_g, v_meta_tokens, v_norm_g, v_w_in, v_b_forget, v_pool_w, v_pool_scale, v_w_up_pool, v_w_up_attn, v_w_out, v_final_norm_g):
    xs = x[0]
    target = loss_target[0]

    g_in, g_upp, g_upa, g_out, g_meta = _exchange(
        "gather_weights",
        [w_in[0].astype(BF16), w_up_pool[0].astype(BF16), w_up_attn[0].astype(BF16), w_out[0].astype(BF16), meta_tokens],
        [])
    w_full = _slots_to_columns(g_in)
    w_main = jnp.concatenate([w_full[:, :N_BEFORE_F], w_full[:, N_BEFORE_F + N_HEADS:]], axis=1)
    w_f = jnp.pad(w_full[:, N_BEFORE_F:N_BEFORE_F + N_HEADS], ((0, 0), (0, LANES - N_HEADS)))
    wupp = _slots_to_columns(g_upp)
    wupa = _slots_to_columns(g_upa)
    wout = g_out.reshape(D_MODEL, D_MODEL)
    meta = _slots_to_columns(g_meta)
    tile0 = jnp.concatenate([jnp.zeros((PAD, D_MODEL), F32), meta], axis=0)
    b_f = jnp.pad(b_forget, ((0, 0), (0, LANES - N_HEADS)))
    pw_b = pool_w[0].astype(BF16)
    final_g = final_norm_g.reshape(1, D_MODEL)

    h, u, zp, q, k, v, cq, ck, sneg, a_pool = _forward_in(xs, tile0, norm_g, w_main, w_f, b_f, pw_b, pool_scale, wupp)
    o, lse = _attention_forward(q, k, v, cq, ck)
    (dh2, mg, yp, ya, dap, daa, do, dza, dgp, dga, dzp, dpn,
     loss_part, d_final_g, d_scale, d_pool_w) = _middle(xs, target, h, o, a_pool, u, zp, w_main, wupp, wupa, wout,
                                                        pw_b, pool_scale, final_g)
    dq, dk, dv, dc, dr = _attention_backward(q, k, v, do, o, lse, cq, ck)
    dproj, df, grad_x, d_meta, d_norm_g, d_bf = _backward_in(xs, tile0, norm_g, dh2, dpn, dzp, dq, dk, dv, dza,
                                                             dgp, dga, dc, dr, sneg, w_main, w_f)
    dw_main = _matmul_tn("grad_w_in", h, dproj, 512)
    dw_f = _matmul_tn("grad_w_forget", h, df, LANES)
    dw_out = _matmul_tn("grad_w_out", mg, dh2, 256)
    dw_upp = _matmul_tn("grad_w_up_pool", yp, dap, 512)
    dw_upa = _matmul_tn("grad_w_up_attn", ya, daa, 512)
    dw_in = jnp.concatenate([dw_main[:, :N_BEFORE_F], dw_f[:, :N_HEADS], dw_main[:, N_BEFORE_F:]], axis=1)

    (p_norm_g, p_bf, p_pool_w, p_scale, p_final_g, p_in, p_upp, p_upa, p_out, p_meta) = _exchange(
        "exchange_gradients",
        [d_norm_g, d_bf, d_pool_w.reshape(4 * POOL_GROUP, POOL_GROUP), d_scale, d_final_g],
        [_columns_to_slots(dw_in), _columns_to_slots(dw_upp), _columns_to_slots(dw_upa),
         dw_out.reshape(N_DEV, D_MODEL // N_DEV, D_MODEL), _columns_to_slots(d_meta)])

    loss = lax.psum(loss_part[0, 0], ("x", "y", "c"))

    def pad_f(a):
        return jnp.pad(a, ((0, 0), (0, LANES - N_HEADS)))

    res = {}
    res["meta_tokens"] = _adamw("adamw_meta", p_meta, meta_tokens, m_meta_tokens, v_meta_tokens, N_META)
    res["norm_g"] = _adamw("adamw_norm_g", p_norm_g, norm_g, m_norm_g, v_norm_g, 1)
    res["w_in"] = _adamw("adamw_w_in", p_in, w_in[0], m_w_in[0], v_w_in[0], 128)
    bf = _adamw("adamw_b_forget", p_bf, pad_f(b_forget), pad_f(m_b_forget), pad_f(v_b_forget), 1)
    res["b_forget"] = [a[:, :N_HEADS] for a in bf]
    pw = _adamw("adamw_pool_w", p_pool_w, pool_w.reshape(512, 128), m_pool_w.reshape(512, 128),
                v_pool_w.reshape(512, 128), 512)
    res["pool_w"] = [a.reshape(pool_w.shape) for a in pw]
    res["pool_scale"] = _adamw("adamw_pool_scale", p_scale, pool_scale, m_pool_scale, v_pool_scale, 1)
    res["w_up_pool"] = _adamw("adamw_w_up_pool", p_upp, w_up_pool[0], m_w_up_pool[0], v_w_up_pool[0], 512)
    res["w_up_attn"] = _adamw("adamw_w_up_attn", p_upa, w_up_attn[0], m_w_up_attn[0], v_w_up_attn[0], 512)
    res["w_out"] = _adamw("adamw_w_out", p_out, w_out[0], m_w_out[0], v_w_out[0], 128)
    fg = _adamw("adamw_final_norm_g", p_final_g, final_g, m_final_norm_g.reshape(1, D_MODEL),
                v_final_norm_g.reshape(1, D_MODEL), 1)
    res["final_norm_g"] = [a.reshape(D_MODEL) for a in fg]
    for name in ("w_in", "w_up_pool", "w_up_attn", "w_out"):
        res[name] = [a[None] for a in res[name]]

    order = ["meta_tokens", "norm_g", "w_in", "b_forget", "pool_w", "pool_scale", "w_up_pool", "w_up_attn", "w_out",
             "final_norm_g"]
    outs = [loss, grad_x[None]]
    for part in range(4):
        outs += [res[name][part] for name in order]
    return tuple(outs)
```

```python
import functools

import jax
import jax.numpy as jnp
from jax import lax
from jax.experimental import pallas as pl
from jax.experimental.pallas import tpu as pltpu

F32 = jnp.float32
BF16 = jnp.bfloat16

D_MODEL = 1024
N_META = 16
POOL_WIDTH = 512
ATTN_WIDTH = 512
N_HEADS = 8
HEAD_DIM = 64
POOL_WINDOWS = (2, 4, 8, 16)
POOL_GROUP = 128
MAX_WINDOW = 16
RMS_EPS = 1e-6
N_MAIN = 5120
N_BEFORE_F = 3072
N_DEV = 8
LANES = 128

ROW_TILE = 256
ATT_TILE = 256
ATT_Q_BLOCKS = 2
PAD = ROW_TILE - N_META
VMEM_LIMIT = 56 * 1024 * 1024

ADAM_LR = 0.001
ADAM_B1 = 0.9
ADAM_B2 = 0.999
ADAM_EPS = 1e-08
ADAM_WD = 0.01
ADAM_STEP = 10

NEG = -1e30
MESH = pl.DeviceIdType.MESH


def _params(sem=None):
    kw = dict(vmem_limit_bytes=VMEM_LIMIT)
    if sem is not None:
        kw["dimension_semantics"] = sem
    return pltpu.CompilerParams(**kw)


def _const(shape, block_index=None):
    idx = block_index or (0,) * len(shape)
    return pl.BlockSpec(shape, lambda i: idx, pipeline_mode=pl.Buffered(1))


def _sigmoid(x):
    return jax.nn.sigmoid(x)


def _dot(a, b):
    return jnp.dot(a, b, preferred_element_type=F32)


def _dot_nt(a, b):
    return lax.dot_general(a, b, (((1,), (1,)), ((), ())), preferred_element_type=F32)


def _dot_tn(a, b):
    return lax.dot_general(a, b, (((0,), (0,)), ((), ())), preferred_element_type=F32)


def _pool_counts(first_row, rows):
    row = first_row + lax.broadcasted_iota(jnp.int32, (rows, 1), 0)
    pos1 = row - PAD + 1
    return [jnp.clip(pos1, 1, w).astype(F32) for w in POOL_WINDOWS]


def _pool_means(u_ext, u, counts):
    rows = u.shape[0]
    out = []
    for g, w in enumerate(POOL_WINDOWS):
        s = u_ext[:, POOL_GROUP * g:POOL_GROUP * (g + 1)]
        sh = 1
        while sh < w:
            s = s + pltpu.roll(s, sh, axis=0)
            sh *= 2
        out.append(s[MAX_WINDOW:MAX_WINDOW + rows, :] / counts[g] - u[:, POOL_GROUP * g:POOL_GROUP * (g + 1)])
    return out


Q_BIAS, Q_ONES, Q_LSE = 64, 67, 70
K_ONES, K_BIAS, K_ONES2 = 64, 67, 70
V_ONES = 64
DO_BIAS = 64


def _lane_ones(lane, ranges):
    hit = None
    for lo, hi in ranges:
        r = (lane >= lo) & (lane < hi)
        hit = r if hit is None else hit | r
    return jnp.where(hit, 1.0, 0.0)


def _put3(base, lane, first, x):
    hi = x.astype(BF16).astype(F32)
    rest = x - hi
    mid = rest.astype(BF16).astype(F32)
    lo = (rest - mid).astype(BF16).astype(F32)
    for j, piece in enumerate((hi, mid, lo)):
        base = jnp.where(lane == first + j, piece, base)
    return base


def _exchange(name, gathers, scatters):
    arrays = list(gathers) + list(scatters)
    n_g, n = len(gathers), len(arrays)

    def body(*refs):
        ins, outs = refs[:n], refs[n:2 * n]
        send_sems, recv_sems, local_sems = refs[2 * n:]
        x, y, c = lax.axis_index("x"), lax.axis_index("y"), lax.axis_index("c")
        me = 4 * x + 2 * y + c
        copies = []
        for a in range(n):
            for r in range(1, N_DEV):
                px = 1 - x if r & 4 else x
                py = 1 - y if r & 2 else y
                pc = 1 - c if r & 1 else c
                src = ins[a] if a < n_g else ins[a].at[4 * px + 2 * py + pc]
                cp = pltpu.make_async_remote_copy(
                    src_ref=src, dst_ref=outs[a].at[me],
                    send_sem=send_sems.at[a, r - 1], recv_sem=recv_sems.at[a, r - 1],
                    device_id=(px, py, pc), device_id_type=MESH)
                cp.start()
                copies.append(cp)
            src = ins[a] if a < n_g else ins[a].at[me]
            mine = pltpu.make_async_copy(src, outs[a].at[me], local_sems.at[a])
            mine.start()
            copies.append(mine)
        for cp in copies:
            cp.wait()

    hbm = pl.BlockSpec(memory_space=pl.ANY)
    out_shape = [jax.ShapeDtypeStruct((N_DEV,) + a.shape, a.dtype) for a in gathers]
    out_shape += [jax.ShapeDtypeStruct(a.shape, a.dtype) for a in scatters]
    return pl.pallas_call(
        body, name=name, out_shape=out_shape,
        in_specs=[hbm] * n, out_specs=[hbm] * n,
        scratch_shapes=[pltpu.SemaphoreType.DMA((n, N_DEV - 1)), pltpu.SemaphoreType.DMA((n, N_DEV - 1)),
                        pltpu.SemaphoreType.DMA((n,))],
    )(*arrays)


def _forward_in(x, tile0, norm_g, w_main, w_f, b_f, pool_w, pool_scale, w_up_pool):
    seq = x.shape[0]
    nt = seq // ROW_TILE + 1
    lp = nt * ROW_TILE
    tm = ROW_TILE

    def body(x_ref, t0_ref, g_ref, wa_ref, wf_ref, bf_ref, pw_ref, sc_ref, wup_ref,
             h_ref, u_ref, zp_ref, q_ref, k_ref, v_ref, sn_ref, ap_ref,
             uext_ref, carry_ref):
        i = pl.program_id(0)

        @pl.when(i == 0)
        def _():
            uext_ref[...] = jnp.zeros_like(uext_ref)
            carry_ref[...] = jnp.zeros_like(carry_ref)

        xt = jnp.where(i == 0, t0_ref[...], x_ref[...])
        r = lax.rsqrt(jnp.mean(xt * xt, axis=-1, keepdims=True) + RMS_EPS)
        h = (xt * r * g_ref[...]).astype(BF16)
        h_ref[...] = h
        pa = _dot(h, wa_ref[...])
        u = pa[:, :512]
        zp = pa[:, 512:1024]
        u_ref[...] = u
        zp_ref[...] = zp

        uext_ref[0:MAX_WINDOW, :] = uext_ref[tm:tm + MAX_WINDOW, :]
        uext_ref[MAX_WINDOW:, :] = u
        counts = _pool_counts(i * tm, tm)
        ps = _pool_means(uext_ref[...], u, counts)
        ppw = jnp.concatenate([_dot(ps[g].astype(BF16), pw_ref[g]) for g in range(4)], axis=1)
        y_pool = ppw * sc_ref[...] * (zp * _sigmoid(zp))
        ap_ref[...] = _dot(y_pool.astype(BF16), wup_ref[...]).astype(BF16)

        fl = _dot(h, wf_ref[...]) + bf_ref[...]
        row = i * tm + lax.broadcasted_iota(jnp.int32, (tm, LANES), 0)
        rloc = lax.broadcasted_iota(jnp.int32, (tm, LANES), 0)
        lane = lax.broadcasted_iota(jnp.int32, (tm, LANES), 1)
        live = (row >= PAD) & (lane < N_HEADS)
        logf = jnp.minimum(fl, 0.0) - jnp.log1p(jnp.exp(-jnp.abs(fl)))
        cs = jnp.where(live, logf, 0.0)
        sh = 1
        while sh < tm:
            cs = cs + jnp.where(rloc >= sh, pltpu.roll(cs, sh, axis=0), 0.0)
            sh *= 2
        cs = cs + carry_ref[...]
        carry_ref[...] = cs[tm - 1:tm, :]
        sn_ref[...] = jnp.where(live, _sigmoid(-fl), 0.0)

        rows1 = i * tm + lax.broadcasted_iota(jnp.int32, (tm, 1), 0)
        ones_q = _lane_ones(lane, ((Q_ONES, Q_ONES + 3),))
        ones_k = _lane_ones(lane, ((K_ONES, K_ONES + 3), (K_ONES2, K_ONES2 + 3)))
        ones_v = _lane_ones(lane, ((V_ONES, V_ONES + 3),))
        for hp in range(N_HEADS // 2):
            qp = pa[:, 1024 + LANES * hp:1024 + LANES * (hp + 1)] * 0.125
            kp = pa[:, 1536 + LANES * hp:1536 + LANES * (hp + 1)]
            vp = pa[:, 2048 + LANES * hp:2048 + LANES * (hp + 1)]
            for e in range(2):
                head = 2 * hp + e
                if e:
                    qp, kp, vp = (pltpu.roll(a, HEAD_DIM, axis=1) for a in (qp, kp, vp))
                c_h = cs[:, head:head + 1]
                q_ref[head] = jnp.where(lane < HEAD_DIM, qp, _put3(ones_q, lane, Q_BIAS, c_h)).astype(BF16)
                minus_ck = jnp.where(rows1 >= PAD, -c_h, NEG)
                k_ref[head] = jnp.where(lane < HEAD_DIM, kp, _put3(ones_k, lane, K_BIAS, minus_ck)).astype(BF16)
                v_ref[head] = jnp.where(lane < HEAD_DIM, vp, ones_v).astype(BF16)

    row_f32 = lambda w: pl.BlockSpec((tm, w), lambda i: (i, 0))
    out_shape = [
        jax.ShapeDtypeStruct((lp, D_MODEL), BF16),
        jax.ShapeDtypeStruct((lp, POOL_WIDTH), F32),
        jax.ShapeDtypeStruct((lp, POOL_WIDTH), F32),
        jax.ShapeDtypeStruct((N_HEADS, lp, LANES), BF16),
        jax.ShapeDtypeStruct((N_HEADS, lp, LANES), BF16),
        jax.ShapeDtypeStruct((N_HEADS, lp, LANES), BF16),
        jax.ShapeDtypeStruct((lp, LANES), F32),
        jax.ShapeDtypeStruct((lp, D_MODEL), BF16),
    ]
    heads = pl.BlockSpec((N_HEADS, tm, LANES), lambda i: (0, i, 0))
    out_specs = [row_f32(D_MODEL), row_f32(512), row_f32(512), heads, heads, heads, row_f32(LANES), row_f32(D_MODEL)]
    in_specs = [
        pl.BlockSpec((tm, D_MODEL), lambda i: (jnp.maximum(i - 1, 0), 0)),
        _const((tm, D_MODEL)), _const((1, D_MODEL)),
        _const((D_MODEL, 2560)), _const((D_MODEL, LANES)), _const((1, LANES)),
        _const((4, POOL_GROUP, POOL_GROUP)), _const((1, POOL_WIDTH)), _const((POOL_WIDTH, D_MODEL)),
    ]
    return pl.pallas_call(
        body, name="forward_in", grid=(nt,), out_shape=out_shape, in_specs=in_specs, out_specs=out_specs,
        scratch_shapes=[pltpu.VMEM((tm + MAX_WINDOW, POOL_WIDTH), F32), pltpu.VMEM((1, LANES), F32)],
        compiler_params=_params(("arbitrary",)),
    )(x, tile0, norm_g, w_main, w_f, b_f, pool_w, pool_scale, w_up_pool)


def _causal(tb):
    return lax.broadcasted_iota(jnp.int32, (tb, tb), 1) <= lax.broadcasted_iota(jnp.int32, (tb, tb), 0)


def _pair_lanes(a0, a1):
    lane = lax.broadcasted_iota(jnp.int32, a0.shape, 1)
    return jnp.where(lane < HEAD_DIM, a0, pltpu.roll(a1, HEAD_DIM, axis=1))


def _attention_forward(q, k, v):
    lp = q.shape[1]
    tk = ATT_TILE
    tq_big = ATT_Q_BLOCKS * tk
    n_big = (lp // tk - 1) // ATT_Q_BLOCKS
    assert lp == tk + n_big * tq_big

    def body(q_ref, k_ref, v_ref, o_ref, lse_ref):
        def q_tile(q0, tq):
            lane = lax.broadcasted_iota(jnp.int32, (1, LANES), 1)
            first = q0 // tk
            qs = [q_ref[e, pl.ds(q0, tq), :] for e in range(2)]

            def block(kj):
                return pl.ds(kj * tk if isinstance(kj, int) else pl.multiple_of(kj * tk, tk), tk)

            def scores(e, kj):
                return _dot_nt(qs[e], k_ref[e, block(kj), :])

            def softmax_step(e, kj, m, acc, s):
                m_new = jnp.maximum(m, jnp.max(s, axis=1, keepdims=True))
                p = jnp.exp(s - m_new)
                pv = _dot(p.astype(BF16), v_ref[e, block(kj), :])
                return m_new, jnp.exp(m - m_new) * acc + pv

            def step(kj, carry):
                new = []
                for e in range(2):
                    m, acc, s = carry[e]
                    s_next = scores(e, kj + 1)
                    new.append(softmax_step(e, kj, m, acc, s) + (s_next,))
                return tuple(new)

            init = tuple((jnp.full((tq, 1), NEG, F32), jnp.zeros((tq, LANES), F32), scores(e, 0)) for e in range(2))
            carry = list(lax.fori_loop(0, first, step, init))
            rows = lax.broadcasted_iota(jnp.int32, (tq, tk), 0)
            cols = lax.broadcasted_iota(jnp.int32, (tq, tk), 1)
            for b in range(tq // tk):
                for e in range(2):
                    m, acc, s = carry[e]
                    s_next = scores(e, first + b + 1) if b + 1 < tq // tk else None
                    m, acc = softmax_step(e, first + b, m, acc, jnp.where(cols + b * tk <= rows, s, NEG))
                    carry[e] = (m, acc, s_next)
            outs, lses = [], []
            for e in range(2):
                m, acc, _ = carry[e]
                l = acc[:, V_ONES:V_ONES + 1]
                outs.append(acc / l)
                lses.append(m + jnp.log(l))
            o_ref[pl.ds(q0, tq), :] = _pair_lanes(outs[0], outs[1]).astype(BF16)
            lse_ref[pl.ds(q0, tq), :] = jnp.where(lane == 0, lses[0], jnp.where(lane == 1, lses[1], 0.0))

        q_tile(0, tk)

        def big_tile(i, _):
            q_tile(pl.multiple_of(tk + i * tq_big, tk), tq_big)
            return 0

        lax.fori_loop(0, n_big, big_tile, 0)

    pair = pl.BlockSpec((lp, LANES), lambda hp: (0, hp))
    heads = pl.BlockSpec((2, lp, LANES), lambda hp: (hp, 0, 0), pipeline_mode=pl.Buffered(1))
    return pl.pallas_call(
        body, name="attention_forward", grid=(N_HEADS // 2,),
        out_shape=[jax.ShapeDtypeStruct((lp, ATTN_WIDTH), BF16), jax.ShapeDtypeStruct((lp, ATTN_WIDTH), F32)],
        in_specs=[heads, heads, heads],
        out_specs=[pair, pair],
        compiler_params=_params(("arbitrary",)),
    )(q, k, v)


def _attention_backward(q, k, v, do, o, lse):
    lp = q.shape[1]
    tb = ATT_TILE
    nb = lp // tb
    tq_big = ATT_Q_BLOCKS * tb
    n_big = (nb - 1) // ATT_Q_BLOCKS
    assert lp == tb + n_big * tq_big

    def body(q_ref, k_ref, v_ref, do_ref, o_ref, lse_ref,
             dq_ref, dk_ref, dv_ref, dc_ref, q2_ref, do2_ref, dk_acc, dv_acc):
        lane = lax.broadcasted_iota(jnp.int32, (tb, LANES), 1)

        def prepare(bi, _):
            r0 = pl.multiple_of(bi * tb, tb)
            rows = r0 + lax.broadcasted_iota(jnp.int32, (tb, 1), 0)
            dob = do_ref[pl.ds(r0, tb), :].astype(F32)
            dd = dob * o_ref[pl.ds(r0, tb), :].astype(F32)
            for e in range(2):
                in_head = (lane >= HEAD_DIM * e) & (lane < HEAD_DIM * (e + 1))
                delta = jnp.sum(jnp.where(in_head, dd, 0.0), axis=1, keepdims=True)
                do_e = pltpu.roll(dob, HEAD_DIM, axis=1) if e else dob
                do2_ref[e, pl.ds(r0, tb), :] = jnp.where(
                    lane < HEAD_DIM, do_e, _put3(jnp.zeros((tb, LANES), F32), lane, DO_BIAS, -delta)).astype(BF16)
                minus_lse = jnp.where(rows >= PAD, -lse_ref[pl.ds(r0, tb), e:e + 1], NEG)
                q2_ref[e, pl.ds(r0, tb), :] = _put3(q_ref[e, pl.ds(r0, tb), :].astype(F32), lane, Q_LSE,
                                                    minus_lse).astype(BF16)
            return 0

        lax.fori_loop(0, nb, prepare, 0)
        dk_acc[...] = jnp.zeros_like(dk_acc)
        dv_acc[...] = jnp.zeros_like(dv_acc)

        def q_tile(q0, tq):
            first = q0 // tb
            qs = [q2_ref[e, pl.ds(q0, tq), :] for e in range(2)]
            dos = [do2_ref[e, pl.ds(q0, tq), :] for e in range(2)]

            def block(kj):
                return pl.ds(kj * tb if isinstance(kj, int) else pl.multiple_of(kj * tb, tb), tb)

            def products(e, kj):
                return _dot_nt(qs[e], k_ref[e, block(kj), :]), _dot_nt(dos[e], v_ref[e, block(kj), :])

            def grads(e, kj, dq, s, dpd, mask):
                p = jnp.exp(s)
                if mask is not None:
                    p = jnp.where(mask, p, 0.0)
                dsb = (p * dpd).astype(BF16)
                dv_acc[e, block(kj), :] += _dot_tn(p.astype(BF16), dos[e])
                dk_acc[e, block(kj), :] += _dot_tn(dsb, qs[e])
                return dq + _dot(dsb, k_ref[e, block(kj), :])

            def step(kj, carry):
                new = []
                for e in range(2):
                    dq, s, dpd = carry[e]
                    nxt = products(e, kj + 1)
                    new.append((grads(e, kj, dq, s, dpd, None),) + nxt)
                return tuple(new)

            init = tuple((jnp.zeros((tq, LANES), F32),) + products(e, 0) for e in range(2))
            carry = list(lax.fori_loop(0, first, step, init))
            rows = lax.broadcasted_iota(jnp.int32, (tq, tb), 0)
            cols = lax.broadcasted_iota(jnp.int32, (tq, tb), 1)
            for b in range(tq // tb):
                for e in range(2):
                    dq, s, dpd = carry[e]
                    nxt = products(e, first + b + 1) if b + 1 < tq // tb else (None, None)
                    carry[e] = (grads(e, first + b, dq, s, dpd, cols + b * tb <= rows),) + nxt
            dq0, dq1 = carry[0][0], carry[1][0]
            lane_q = lax.broadcasted_iota(jnp.int32, (tq, LANES), 1)
            dq_ref[pl.ds(q0, tq), :] = (_pair_lanes(dq0, dq1) * 0.125).astype(BF16)
            dc_ref[pl.ds(q0, tq), :] = jnp.where(lane_q == 0, dq0[:, K_ONES:K_ONES + 1],
                                                 jnp.where(lane_q == 1, dq1[:, K_ONES:K_ONES + 1], 0.0))

        q_tile(0, tb)

        def big_tile(i, _):
            q_tile(pl.multiple_of(tb + i * tq_big, tb), tq_big)
            return 0

        lax.fori_loop(0, n_big, big_tile, 0)

        def finish(bi, _):
            r0 = pl.multiple_of(bi * tb, tb)
            dk0, dk1 = dk_acc[0, pl.ds(r0, tb), :], dk_acc[1, pl.ds(r0, tb), :]
            dk_ref[pl.ds(r0, tb), :] = _pair_lanes(dk0, dk1).astype(BF16)
            dv_ref[pl.ds(r0, tb), :] = _pair_lanes(dv_acc[0, pl.ds(r0, tb), :], dv_acc[1, pl.ds(r0, tb), :]).astype(BF16)
            col_sums = jnp.where(lane == 0, dk0[:, Q_ONES:Q_ONES + 1],
                                 jnp.where(lane == 1, dk1[:, Q_ONES:Q_ONES + 1], 0.0))
            dc_ref[pl.ds(r0, tb), :] = dc_ref[pl.ds(r0, tb), :] - col_sums
            return 0

        lax.fori_loop(0, nb, finish, 0)

    once = pl.Buffered(1)
    pair = pl.BlockSpec((lp, LANES), lambda hp: (0, hp))
    pair_in = pl.BlockSpec((lp, LANES), lambda hp: (0, hp), pipeline_mode=once)
    heads = pl.BlockSpec((2, lp, LANES), lambda hp: (hp, 0, 0), pipeline_mode=once)
    wide = jax.ShapeDtypeStruct((lp, ATTN_WIDTH), BF16)
    return pl.pallas_call(
        body, name="attention_backward", grid=(N_HEADS // 2,),
        out_shape=[wide, wide, wide, jax.ShapeDtypeStruct((lp, ATTN_WIDTH), F32)],
        in_specs=[heads, heads, heads, pair_in, pair_in, pair_in],
        out_specs=[pair, pair, pair, pair],
        scratch_shapes=[pltpu.VMEM((2, lp, LANES), BF16), pltpu.VMEM((2, lp, LANES), BF16),
                        pltpu.VMEM((2, lp, LANES), F32), pltpu.VMEM((2, lp, LANES), F32)],
        compiler_params=_params(("arbitrary",)),
    )(q, k, v, do, o, lse)


def _middle(x, target, h, o, a_pool, u, zp, w_main, w_up_pool, w_up_attn, w_out, pool_w, pool_scale, final_g):
    seq = x.shape[0]
    tm = ROW_TILE
    nt = seq // tm + 1
    lp = nt * tm
    halo_blocks = tm // MAX_WINDOW

    def body(x_ref, t_ref, h_ref, o_ref, ap_ref, u_ref, uh_ref, zp_ref,
             wc_ref, wupp_ref, wupa_ref, wout_ref, pw_ref, sc_ref, gf_ref,
             dh2_ref, mg_ref, yp_ref, ya_ref, dap_ref, daa_ref, do_ref, dza_ref, dgp_ref, dga_ref, dzp_ref, dpn_ref,
             loss_ref, dgf_ref, dsc_ref, dpw_ref):
        i = pl.program_id(0)
        tiles = (dh2_ref, mg_ref, yp_ref, ya_ref, dap_ref, daa_ref, do_ref, dza_ref, dgp_ref, dga_ref, dzp_ref, dpn_ref)

        @pl.when(i == 0)
        def _():
            for ref in tiles + (loss_ref, dgf_ref, dsc_ref, dpw_ref):
                ref[...] = jnp.zeros_like(ref)

        @pl.when(i > 0)
        def _():
            xt = x_ref[...]
            hb = h_ref[...]
            pc = _dot(hb, wc_ref[...])
            za, gp, ga = pc[:, :512], pc[:, 512:1536], pc[:, 1536:]
            of = o_ref[...].astype(F32)
            sza = _sigmoid(za)
            silu_za = za * sza
            ya = (of * silu_za).astype(BF16)
            ya_ref[...] = ya
            aa = _dot(ya, wupa_ref[...])
            ap = ap_ref[...].astype(F32)
            sgp, sga = _sigmoid(gp), _sigmoid(ga)
            mg = (sgp * ap + sga * aa).astype(BF16)
            mg_ref[...] = mg
            h2 = xt + _dot(mg, wout_ref[...])
            r2 = lax.rsqrt(jnp.mean(h2 * h2, axis=-1, keepdims=True) + RMS_EPS)
            h2n = h2 * r2
            gf = gf_ref[...]
            diff = h2n * gf - t_ref[...]
            loss_ref[...] += 0.5 * jnp.sum(jnp.mean(diff * diff, axis=-1, keepdims=True), axis=0, keepdims=True)
            dy = diff * (1.0 / D_MODEL)
            dgf_ref[...] += jnp.sum(dy * h2n, axis=0, keepdims=True)
            dyg = dy * gf
            dh2 = r2 * (dyg - h2n * jnp.mean(dyg * h2n, axis=-1, keepdims=True))
            dh2_ref[...] = dh2
            dmg = _dot_nt(dh2.astype(BF16), wout_ref[...])
            dap = (dmg * sgp).astype(BF16)
            daa = (dmg * sga).astype(BF16)
            dap_ref[...] = dap
            daa_ref[...] = daa
            dgp_ref[...] = (dmg * ap * sgp * (1.0 - sgp)).astype(BF16)
            dga_ref[...] = (dmg * aa * sga * (1.0 - sga)).astype(BF16)
            dyp = _dot_nt(dap, wupp_ref[...])
            dya = _dot_nt(daa, wupa_ref[...])
            do_ref[...] = (dya * silu_za).astype(BF16)
            dza_ref[...] = (dya * of * (sza * (1.0 + za * (1.0 - sza)))).astype(BF16)

            u = u_ref[...]
            zp = zp_ref[...]
            counts = _pool_counts(i * tm, tm)
            ps = _pool_means(jnp.concatenate([uh_ref[...], u], axis=0), u, counts)
            pbs = [p.astype(BF16) for p in ps]
            ppw = jnp.concatenate([_dot(pbs[g], pw_ref[g]) for g in range(4)], axis=1)
            sc = sc_ref[...]
            szp = _sigmoid(zp)
            silu_zp = zp * szp
            ypre = ppw * sc
            yp_ref[...] = (ypre * silu_zp).astype(BF16)
            dypre = dyp * silu_zp
            dzp_ref[...] = (dyp * ypre * (szp * (1.0 + zp * (1.0 - szp)))).astype(BF16)
            dsc_ref[...] += jnp.sum(dypre * ppw, axis=0, keepdims=True)
            dppw = (dypre * sc).astype(BF16)
            dpns = []
            for g in range(4):
                dg = dppw[:, POOL_GROUP * g:POOL_GROUP * (g + 1)]
                dpw_ref[g] += _dot_tn(pbs[g], dg)
                dpns.append(_dot_nt(dg, pw_ref[g]) / counts[g])
            dpn_ref[...] = jnp.concatenate(dpns, axis=1)

    real = lambda w: pl.BlockSpec((tm, w), lambda i: (jnp.maximum(i - 1, 0), 0))
    row = lambda w: pl.BlockSpec((tm, w), lambda i: (i, 0))
    in_specs = [
        real(D_MODEL), real(D_MODEL), row(D_MODEL), row(512), row(D_MODEL), row(512),
        pl.BlockSpec((MAX_WINDOW, 512), lambda i: (jnp.maximum(i * halo_blocks - 1, 0), 0)), row(512),
        _const((D_MODEL, 2560), (0, 1)), _const((POOL_WIDTH, D_MODEL)), _const((ATTN_WIDTH, D_MODEL)),
        _const((D_MODEL, D_MODEL)), _const((4, POOL_GROUP, POOL_GROUP)), _const((1, POOL_WIDTH)), _const((1, D_MODEL)),
    ]
    sd = jax.ShapeDtypeStruct
    out_shape = [
        sd((lp, D_MODEL), F32),
        sd((lp, D_MODEL), BF16),
        sd((lp, 512), BF16),
        sd((lp, 512), BF16),
        sd((lp, D_MODEL), BF16),
        sd((lp, D_MODEL), BF16),
        sd((lp, 512), BF16),
        sd((lp, 512), BF16),
        sd((lp, D_MODEL), BF16),
        sd((lp, D_MODEL), BF16),
        sd((lp, 512), BF16),
        sd((lp, 512), F32),
        sd((1, 1), F32),
        sd((1, D_MODEL), F32),
        sd((1, 512), F32),
        sd((4, POOL_GROUP, POOL_GROUP), F32),
    ]
    keep = lambda shape: pl.BlockSpec(shape, lambda i: (0,) * len(shape))
    out_specs = [row(D_MODEL), row(D_MODEL), row(512), row(512), row(D_MODEL), row(D_MODEL), row(512), row(512),
                 row(D_MODEL), row(D_MODEL), row(512), row(512),
                 keep((1, 1)), keep((1, D_MODEL)), keep((1, 512)), keep((4, POOL_GROUP, POOL_GROUP))]
    return pl.pallas_call(
        body, name="middle", grid=(nt,), out_shape=out_shape, in_specs=in_specs, out_specs=out_specs,
        compiler_params=_params(("arbitrary",)),
    )(x, target, h, o, a_pool, u, u, zp, w_main, w_up_pool, w_up_attn, w_out, pool_w, pool_scale, final_g)


def _backward_in(x, tile0, norm_g, dh2, dpn, dzp, dq, dk, dv, dza, dgp, dga, dc, sneg, w_main, w_f):
    seq = x.shape[0]
    tm = ROW_TILE
    nt = seq // tm + 1
    lp = nt * tm
    halo_blocks = tm // MAX_WINDOW
    last_halo = lp // MAX_WINDOW - 1

    def body(x_ref, t0_ref, g_ref, dh2_ref, dpn_ref, dpnh_ref, dzp_ref, dq_ref, dk_ref, dv_ref, dza_ref,
             dgp_ref, dga_ref, dc_ref, sn_ref, wm_ref, wf_ref,
             dproj_ref, df_ref, gx_ref, gmeta_ref, dg_ref, dbf_ref, carry_ref):
        i = pl.program_id(0)
        t = nt - 1 - i

        @pl.when(i == 0)
        def _():
            carry_ref[...] = jnp.zeros_like(carry_ref)
            dg_ref[...] = jnp.zeros_like(dg_ref)
            dbf_ref[...] = jnp.zeros_like(dbf_ref)

        dpn_t = dpn_ref[...]
        ahead = jnp.where(i == 0, jnp.zeros_like(dpnh_ref), dpnh_ref[...])
        ext = jnp.concatenate([dpn_t, ahead], axis=0)
        counts = _pool_counts(t * tm, tm)
        for g, w in enumerate(POOL_WINDOWS):
            s = ext[:, POOL_GROUP * g:POOL_GROUP * (g + 1)]
            sh = 1
            while sh < w:
                s = s + pltpu.roll(s, tm + MAX_WINDOW - sh, axis=0)
                sh *= 2
            du = s[:tm, :] - dpn_t[:, POOL_GROUP * g:POOL_GROUP * (g + 1)] * counts[g]
            dproj_ref[:, POOL_GROUP * g:POOL_GROUP * (g + 1)] = du.astype(BF16)
        dproj_ref[:, 512:1024] = dzp_ref[...]
        dproj_ref[:, 1024:1536] = dq_ref[...]
        dproj_ref[:, 1536:2048] = dk_ref[...]
        dproj_ref[:, 2048:2560] = dv_ref[...]
        dproj_ref[:, 2560:3072] = dza_ref[...]
        dproj_ref[:, 3072:4096] = dgp_ref[...]
        dproj_ref[:, 4096:5120] = dga_ref[...]

        dct = dc_ref[:, 0:LANES]
        for hp in range(1, N_HEADS // 2):
            dct = dct + pltpu.roll(dc_ref[:, LANES * hp:LANES * (hp + 1)], 2 * hp, axis=1)
        rloc = lax.broadcasted_iota(jnp.int32, (tm, LANES), 0)
        sh = 1
        while sh < tm:
            dct = dct + jnp.where(rloc + sh < tm, pltpu.roll(dct, tm - sh, axis=0), 0.0)
            sh *= 2
        dct = dct + carry_ref[...]
        carry_ref[...] = dct[0:1, :]
        df = dct * sn_ref[...]
        dbf_ref[...] += jnp.sum(df, axis=0, keepdims=True)
        dfb = df.astype(BF16)
        df_ref[...] = dfb

        dh = _dot_nt(dproj_ref[...], wm_ref[...]) + _dot_nt(dfb, wf_ref[...])
        xt = jnp.where(t == 0, t0_ref[...], x_ref[...])
        r = lax.rsqrt(jnp.mean(xt * xt, axis=-1, keepdims=True) + RMS_EPS)
        xn = xt * r
        dg_ref[...] += jnp.sum(dh * xn, axis=0, keepdims=True)
        dhg = dh * g_ref[...]
        dx = dh2_ref[...] + r * (dhg - xn * jnp.mean(dhg * xn, axis=-1, keepdims=True))

        @pl.when(t > 0)
        def _():
            gx_ref[...] = dx

        @pl.when(t == 0)
        def _():
            gmeta_ref[...] = dx[PAD:, :]

    rev = lambda w: pl.BlockSpec((tm, w), lambda i: (nt - 1 - i, 0))
    real = pl.BlockSpec((tm, D_MODEL), lambda i: (jnp.maximum(nt - 2 - i, 0), 0))
    in_specs = [
        real, _const((tm, D_MODEL)), _const((1, D_MODEL)), rev(D_MODEL), rev(512),
        pl.BlockSpec((MAX_WINDOW, 512), lambda i: (jnp.minimum((nt - i) * halo_blocks, last_halo), 0)),
        rev(512), rev(512), rev(512), rev(512), rev(512), rev(D_MODEL), rev(D_MODEL),
        rev(512), rev(LANES),
        _const((D_MODEL, N_MAIN)), _const((D_MODEL, LANES)),
    ]
    sd = jax.ShapeDtypeStruct
    out_shape = [sd((lp, N_MAIN), BF16), sd((lp, LANES), BF16), sd((seq, D_MODEL), F32), sd((N_META, D_MODEL), F32),
                 sd((1, D_MODEL), F32), sd((1, LANES), F32)]
    keep = lambda shape: pl.BlockSpec(shape, lambda i: (0,) * len(shape))
    out_specs = [rev(N_MAIN), rev(LANES), real, keep((N_META, D_MODEL)), keep((1, D_MODEL)), keep((1, LANES))]
    return pl.pallas_call(
        body, name="backward_in", grid=(nt,), out_shape=out_shape, in_specs=in_specs, out_specs=out_specs,
        scratch_shapes=[pltpu.VMEM((1, LANES), F32)],
        compiler_params=_params(("arbitrary",)),
    )(x, tile0, norm_g, dh2, dpn, dpn, dzp, dq, dk, dv, dza, dgp, dga, dc, sneg, w_main, w_f)


def _matmul_tn(name, a, b, tn):
    lp, m = a.shape
    n = b.shape[1]

    def body(a_ref, b_ref, c_ref):
        c_ref[...] = _dot_tn(a_ref[...].astype(BF16), b_ref[...].astype(BF16))

    return pl.pallas_call(
        body, name=name, grid=(n // tn,), out_shape=jax.ShapeDtypeStruct((m, n), F32),
        in_specs=[_const((lp, m)), pl.BlockSpec((lp, tn), lambda j: (0, j))],
        out_specs=pl.BlockSpec((m, tn), lambda j: (0, j)),
        compiler_params=_params(("arbitrary",)),
    )(a, b)


def _adamw(name, parts, w, m, v, rows):
    r, c = w.shape

    def body(p_ref, w_ref, m_ref, v_ref, g_ref, d_ref, mo_ref, vo_ref):
        g = p_ref[0].astype(F32)
        for s in range(1, N_DEV):
            g = g + p_ref[s].astype(F32)
        m_new = ADAM_B1 * m_ref[...] + (1.0 - ADAM_B1) * g
        v_new = ADAM_B2 * v_ref[...] + (1.0 - ADAM_B2) * (g * g)
        m_hat = m_new / (1.0 - ADAM_B1 ** ADAM_STEP)
        v_hat = v_new / (1.0 - ADAM_B2 ** ADAM_STEP)
        g_ref[...] = g
        d_ref[...] = -ADAM_LR * (m_hat / (jnp.sqrt(v_hat) + ADAM_EPS) + ADAM_WD * w_ref[...])
        mo_ref[...] = m_new
        vo_ref[...] = v_new

    blk = pl.BlockSpec((rows, c), lambda i: (i, 0))
    return pl.pallas_call(
        body, name=name, grid=(r // rows,), out_shape=[jax.ShapeDtypeStruct((r, c), F32)] * 4,
        in_specs=[pl.BlockSpec((N_DEV, rows, c), lambda i: (0, i, 0)), blk, blk, blk],
        out_specs=[blk] * 4,
        compiler_params=_params(("arbitrary",)),
    )(parts, w, m, v)


def _columns_to_slots(a):
    r, c8 = a.shape
    return a.reshape(r, N_DEV, c8 // N_DEV).transpose(1, 0, 2)


def _slots_to_columns(a):
    n, r, c = a.shape
    return a.transpose(1, 0, 2).reshape(r, n * c)


def kernel(x, meta_tokens, norm_g, w_in, b_forget, pool_w, pool_scale, w_up_pool, w_up_attn, w_out, final_norm_g, loss_target, m_meta_tokens, m_norm_g, m_w_in, m_b_forget, m_pool_w, m_pool_scale, m_w_up_pool, m_w_up_attn, m_w_out, m_final_norm_g, v_meta_tokens, v_norm_g, v_w_in, v_b_forget, v_pool_w, v_pool_scale, v_w_up_pool, v_w_up_attn, v_w_out, v_final_norm_g):
    xs = x[0]
    target = loss_target[0]

    g_in, g_upp, g_upa, g_out, g_meta = _exchange(
        "gather_weights",
        [w_in[0].astype(BF16), w_up_pool[0].astype(BF16), w_up_attn[0].astype(BF16), w_out[0].astype(BF16), meta_tokens],
        [])
    w_full = _slots_to_columns(g_in)
    w_main = jnp.concatenate([w_full[:, :N_BEFORE_F], w_full[:, N_BEFORE_F + N_HEADS:]], axis=1)
    w_f = jnp.pad(w_full[:, N_BEFORE_F:N_BEFORE_F + N_HEADS], ((0, 0), (0, LANES - N_HEADS)))
    wupp = _slots_to_columns(g_upp)
    wupa = _slots_to_columns(g_upa)
    wout = g_out.reshape(D_MODEL, D_MODEL)
    meta = _slots_to_columns(g_meta)
    tile0 = jnp.concatenate([jnp.zeros((PAD, D_MODEL), F32), meta], axis=0)
    b_f = jnp.pad(b_forget, ((0, 0), (0, LANES - N_HEADS)))
    pw_b = pool_w[0].astype(BF16)
    final_g = final_norm_g.reshape(1, D_MODEL)

    h, u, zp, q, k, v, sneg, a_pool = _forward_in(xs, tile0, norm_g, w_main, w_f, b_f, pw_b, pool_scale, wupp)
    o, lse = _attention_forward(q, k, v)
    (dh2, mg, yp, ya, dap, daa, do, dza, dgp, dga, dzp, dpn,
     loss_part, d_final_g, d_scale, d_pool_w) = _middle(xs, target, h, o, a_pool, u, zp, w_main, wupp, wupa, wout,
                                                        pw_b, pool_scale, final_g)
    dq, dk, dv, dc = _attention_backward(q, k, v, do, o, lse)
    dproj, df, grad_x, d_meta, d_norm_g, d_bf = _backward_in(xs, tile0, norm_g, dh2, dpn, dzp, dq, dk, dv, dza,
                                                             dgp, dga, dc, sneg, w_main, w_f)
    dw_main = _matmul_tn("grad_w_in", h, dproj, 512)
    dw_f = _matmul_tn("grad_w_forget", h, df, LANES)
    dw_out = _matmul_tn("grad_w_out", mg, dh2, 256)
    dw_upp = _matmul_tn("grad_w_up_pool", yp, dap, 512)
    dw_upa = _matmul_tn("grad_w_up_attn", ya, daa, 512)
    dw_in = jnp.concatenate([dw_main[:, :N_BEFORE_F], dw_f[:, :N_HEADS], dw_main[:, N_BEFORE_F:]], axis=1)

    (p_norm_g, p_bf, p_pool_w, p_scale, p_final_g, p_in, p_upp, p_upa, p_out, p_meta) = _exchange(
        "exchange_gradients",
        [d_norm_g, d_bf, d_pool_w.reshape(4 * POOL_GROUP, POOL_GROUP), d_scale, d_final_g],
        [_columns_to_slots(dw_in).astype(BF16), _columns_to_slots(dw_upp).astype(BF16),
         _columns_to_slots(dw_upa).astype(BF16), dw_out.reshape(N_DEV, D_MODEL // N_DEV, D_MODEL).astype(BF16),
         _columns_to_slots(d_meta)])

    loss = lax.psum(loss_part[0, 0], ("x", "y", "c"))

    def pad_f(a):
        return jnp.pad(a, ((0, 0), (0, LANES - N_HEADS)))

    res = {}
    res["meta_tokens"] = _adamw("adamw_meta", p_meta, meta_tokens, m_meta_tokens, v_meta_tokens, N_META)
    res["norm_g"] = _adamw("adamw_norm_g", p_norm_g, norm_g, m_norm_g, v_norm_g, 1)
    res["w_in"] = _adamw("adamw_w_in", p_in, w_in[0], m_w_in[0], v_w_in[0], 128)
    bf = _adamw("adamw_b_forget", p_bf, pad_f(b_forget), pad_f(m_b_forget), pad_f(v_b_forget), 1)
    res["b_forget"] = [a[:, :N_HEADS] for a in bf]
    pw = _adamw("adamw_pool_w", p_pool_w, pool_w.reshape(512, 128), m_pool_w.reshape(512, 128),
                v_pool_w.reshape(512, 128), 512)
    res["pool_w"] = [a.reshape(pool_w.shape) for a in pw]
    res["pool_scale"] = _adamw("adamw_pool_scale", p_scale, pool_scale, m_pool_scale, v_pool_scale, 1)
    res["w_up_pool"] = _adamw("adamw_w_up_pool", p_upp, w_up_pool[0], m_w_up_pool[0], v_w_up_pool[0], 512)
    res["w_up_attn"] = _adamw("adamw_w_up_attn", p_upa, w_up_attn[0], m_w_up_attn[0], v_w_up_attn[0], 512)
    res["w_out"] = _adamw("adamw_w_out", p_out, w_out[0], m_w_out[0], v_w_out[0], 128)
    fg = _adamw("adamw_final_norm_g", p_final_g, final_g, m_final_norm_g.reshape(1, D_MODEL),
                v_final_norm_g.reshape(1, D_MODEL), 1)
    res["final_norm_g"] = [a.reshape(D_MODEL) for a in fg]
    for name in ("w_in", "w_up_pool", "w_up_attn", "w_out"):
        res[name] = [a[None] for a in res[name]]

    order = ["meta_tokens", "norm_g", "w_in", "b_forget", "pool_w", "pool_scale", "w_up_pool", "w_up_attn", "w_out",
             "final_norm_g"]
    outs = [loss, grad_x[None]]
    for part in range(4):
        outs += [res[name][part] for name in order]
    return tuple(outs)
```

```python
import functools

import jax
import jax.numpy as jnp
from jax import lax
from jax.experimental import pallas as pl
from jax.experimental.pallas import tpu as pltpu

F32 = jnp.float32
BF16 = jnp.bfloat16

D_MODEL = 1024
N_META = 16
POOL_WIDTH = 512
ATTN_WIDTH = 512
N_HEADS = 8
HEAD_DIM = 64
POOL_WINDOWS = (2, 4, 8, 16)
POOL_GROUP = 128
MAX_WINDOW = 16
RMS_EPS = 1e-6
N_MAIN = 5120
N_BEFORE_F = 3072
N_DEV = 8
LANES = 128

ROW_TILE = 256
ATT_TILE = 256
ATT_Q_BLOCKS = 2
PAD = ROW_TILE - N_META
VMEM_LIMIT = 56 * 1024 * 1024

ADAM_LR = 0.001
ADAM_B1 = 0.9
ADAM_B2 = 0.999
ADAM_EPS = 1e-08
ADAM_WD = 0.01
ADAM_STEP = 10

NEG = -1e30
MESH = pl.DeviceIdType.MESH


def _params(sem=None):
    kw = dict(vmem_limit_bytes=VMEM_LIMIT)
    if sem is not None:
        kw["dimension_semantics"] = sem
    return pltpu.CompilerParams(**kw)


def _const(shape, block_index=None):
    idx = block_index or (0,) * len(shape)
    return pl.BlockSpec(shape, lambda i: idx, pipeline_mode=pl.Buffered(1))


def _sigmoid(x):
    return jax.nn.sigmoid(x)


def _dot(a, b):
    return jnp.dot(a, b, preferred_element_type=F32)


def _dot_nt(a, b):
    return lax.dot_general(a, b, (((1,), (1,)), ((), ())), preferred_element_type=F32)


def _dot_tn(a, b):
    return lax.dot_general(a, b, (((0,), (0,)), ((), ())), preferred_element_type=F32)


def _pool_counts(first_row, rows):
    row = first_row + lax.broadcasted_iota(jnp.int32, (rows, 1), 0)
    pos1 = row - PAD + 1
    return [jnp.clip(pos1, 1, w).astype(F32) for w in POOL_WINDOWS]


def _pool_means(u_ext, u, counts):
    rows = u.shape[0]
    out = []
    for g, w in enumerate(POOL_WINDOWS):
        s = u_ext[:, POOL_GROUP * g:POOL_GROUP * (g + 1)]
        sh = 1
        while sh < w:
            s = s + pltpu.roll(s, sh, axis=0)
            sh *= 2
        out.append(s[MAX_WINDOW:MAX_WINDOW + rows, :] / counts[g] - u[:, POOL_GROUP * g:POOL_GROUP * (g + 1)])
    return out


Q_BIAS, Q_ONES, Q_LSE = 64, 67, 70
K_ONES, K_BIAS, K_ONES2 = 64, 67, 70
V_ONES = 64
DO_BIAS = 64


def _lane_ones(lane, ranges):
    hit = None
    for lo, hi in ranges:
        r = (lane >= lo) & (lane < hi)
        hit = r if hit is None else hit | r
    return jnp.where(hit, 1.0, 0.0)


def _put3(base, lane, first, x):
    hi = x.astype(BF16).astype(F32)
    rest = x - hi
    mid = rest.astype(BF16).astype(F32)
    lo = (rest - mid).astype(BF16).astype(F32)
    for j, piece in enumerate((hi, mid, lo)):
        base = jnp.where(lane == first + j, piece, base)
    return base


SIBLING = 1
SAME_CORE = (2, 4, 6)
ALL_PEERS = (1, 2, 3, 4, 5, 6, 7)


def _place():
    return lax.axis_index("x"), lax.axis_index("y"), lax.axis_index("c")


def _peer(r):
    x, y, c = _place()
    return (1 - x if r & 4 else x, 1 - y if r & 2 else y, 1 - c if r & 1 else c)


def _device_slot(p):
    return 4 * p[0] + 2 * p[1] + p[2]


def _chip_slot(p):
    return 2 * p[0] + p[1]


def _exchange(name, items):
    n = len(items)

    def body(*refs):
        ins, outs = refs[:n], refs[n:2 * n]
        send_sems, recv_sems, local_sems = refs[2 * n:]
        me = _place()
        copies = []
        for a, (kind, _, peers) in enumerate(items):
            slot = _chip_slot if kind == "chips" else _device_slot
            for r in peers:
                peer = _peer(r)
                src = ins[a] if kind in ("swap", "gather") else ins[a].at[slot(peer)]
                dst = outs[a] if kind == "swap" else outs[a].at[slot(me)]
                cp = pltpu.make_async_remote_copy(
                    src_ref=src, dst_ref=dst, send_sem=send_sems.at[a, r - 1], recv_sem=recv_sems.at[a, r - 1],
                    device_id=peer, device_id_type=MESH)
                cp.start()
                copies.append(cp)
            if kind != "swap":
                src = ins[a] if kind == "gather" else ins[a].at[slot(me)]
                mine = pltpu.make_async_copy(src, outs[a].at[slot(me)], local_sems.at[a])
                mine.start()
                copies.append(mine)
        for cp in copies:
            cp.wait()

    hbm = pl.BlockSpec(memory_space=pl.ANY)
    out_shape = [jax.ShapeDtypeStruct(((N_DEV,) if kind == "gather" else ()) + a.shape, a.dtype)
                 for kind, a, _ in items]
    return pl.pallas_call(
        body, name=name, out_shape=out_shape,
        in_specs=[hbm] * n, out_specs=[hbm] * n,
        scratch_shapes=[pltpu.SemaphoreType.DMA((n, N_DEV - 1)), pltpu.SemaphoreType.DMA((n, N_DEV - 1)),
                        pltpu.SemaphoreType.DMA((n,))],
    )(*[a for _, a, _ in items])


def _gather_two_level(name, arrays):
    n = len(arrays)

    def body(*refs):
        ins, outs = refs[:n], refs[n:2 * n]
        send_sems, recv_sems, local_sems = refs[2 * n:]
        me = _place()
        sibling = _peer(SIBLING)

        def copy(a, k, block, to, src=None):
            rows = outs[a].at[_device_slot(block)]
            return pltpu.make_async_remote_copy(
                src_ref=rows if src is None else src, dst_ref=rows,
                send_sem=send_sems.at[a, k], recv_sem=recv_sems.at[a, k], device_id=to, device_id_type=MESH)

        sends, own = [], []
        for a in range(n):
            mine = pltpu.make_async_copy(ins[a], outs[a].at[_device_slot(me)], local_sems.at[a])
            mine.start()
            own.append(mine)
            for k, r in enumerate((SIBLING,) + SAME_CORE):
                cp = copy(a, k, me, _peer(r), src=ins[a])
                cp.start()
                sends.append(cp)
        for a in range(n):
            for j, r in enumerate(SAME_CORE):
                copy(a, 1 + j, _peer(r), me).wait_recv()
                passed = copy(a, 4 + j, _peer(r), sibling)
                passed.start()
                sends.append(passed)
        for a in range(n):
            copy(a, 0, sibling, me).wait_recv()
            for j, r in enumerate(SAME_CORE):
                copy(a, 4 + j, _peer(r | SIBLING), me).wait_recv()
        for cp in sends:
            cp.wait_send()
        for cp in own:
            cp.wait()

    hbm = pl.BlockSpec(memory_space=pl.ANY)
    return pl.pallas_call(
        body, name=name, out_shape=[jax.ShapeDtypeStruct((N_DEV,) + a.shape, a.dtype) for a in arrays],
        in_specs=[hbm] * n, out_specs=[hbm] * n,
        scratch_shapes=[pltpu.SemaphoreType.DMA((n, N_DEV - 1)), pltpu.SemaphoreType.DMA((n, N_DEV - 1)),
                        pltpu.SemaphoreType.DMA((n,))],
    )(*arrays)


def _forward_in(x, tile0, norm_g, w_main, w_f, b_f, pool_w, pool_scale, w_up_pool):
    seq = x.shape[0]
    nt = seq // ROW_TILE + 1
    lp = nt * ROW_TILE
    tm = ROW_TILE

    def body(x_ref, t0_ref, g_ref, wa_ref, wf_ref, bf_ref, pw_ref, sc_ref, wup_ref,
             h_ref, u_ref, zp_ref, q_ref, k_ref, v_ref, sn_ref, ap_ref,
             uext_ref, carry_ref):
        i = pl.program_id(0)

        @pl.when(i == 0)
        def _():
            uext_ref[...] = jnp.zeros_like(uext_ref)
            carry_ref[...] = jnp.zeros_like(carry_ref)

        xt = jnp.where(i == 0, t0_ref[...], x_ref[...])
        r = lax.rsqrt(jnp.mean(xt * xt, axis=-1, keepdims=True) + RMS_EPS)
        h = (xt * r * g_ref[...]).astype(BF16)
        h_ref[...] = h
        pa = _dot(h, wa_ref[...])
        u = pa[:, :512]
        zp = pa[:, 512:1024]
        u_ref[...] = u
        zp_ref[...] = zp

        uext_ref[0:MAX_WINDOW, :] = uext_ref[tm:tm + MAX_WINDOW, :]
        uext_ref[MAX_WINDOW:, :] = u
        counts = _pool_counts(i * tm, tm)
        ps = _pool_means(uext_ref[...], u, counts)
        ppw = jnp.concatenate([_dot(ps[g].astype(BF16), pw_ref[g]) for g in range(4)], axis=1)
        y_pool = ppw * sc_ref[...] * (zp * _sigmoid(zp))
        ap_ref[...] = _dot(y_pool.astype(BF16), wup_ref[...]).astype(BF16)

        fl = _dot(h, wf_ref[...]) + bf_ref[...]
        row = i * tm + lax.broadcasted_iota(jnp.int32, (tm, LANES), 0)
        rloc = lax.broadcasted_iota(jnp.int32, (tm, LANES), 0)
        lane = lax.broadcasted_iota(jnp.int32, (tm, LANES), 1)
        live = (row >= PAD) & (lane < N_HEADS)
        logf = jnp.minimum(fl, 0.0) - jnp.log1p(jnp.exp(-jnp.abs(fl)))
        cs = jnp.where(live, logf, 0.0)
        sh = 1
        while sh < tm:
            cs = cs + jnp.where(rloc >= sh, pltpu.roll(cs, sh, axis=0), 0.0)
            sh *= 2
        cs = cs + carry_ref[...]
        carry_ref[...] = cs[tm - 1:tm, :]
        sn_ref[...] = jnp.where(live, _sigmoid(-fl), 0.0)

        rows1 = i * tm + lax.broadcasted_iota(jnp.int32, (tm, 1), 0)
        ones_q = _lane_ones(lane, ((Q_ONES, Q_ONES + 3),))
        ones_k = _lane_ones(lane, ((K_ONES, K_ONES + 3), (K_ONES2, K_ONES2 + 3)))
        ones_v = _lane_ones(lane, ((V_ONES, V_ONES + 3),))
        for hp in range(N_HEADS // 2):
            qp = pa[:, 1024 + LANES * hp:1024 + LANES * (hp + 1)] * 0.125
            kp = pa[:, 1536 + LANES * hp:1536 + LANES * (hp + 1)]
            vp = pa[:, 2048 + LANES * hp:2048 + LANES * (hp + 1)]
            for e in range(2):
                head = 2 * hp + e
                if e:
                    qp, kp, vp = (pltpu.roll(a, HEAD_DIM, axis=1) for a in (qp, kp, vp))
                c_h = cs[:, head:head + 1]
                q_ref[head] = jnp.where(lane < HEAD_DIM, qp, _put3(ones_q, lane, Q_BIAS, c_h)).astype(BF16)
                minus_ck = jnp.where(rows1 >= PAD, -c_h, NEG)
                k_ref[head] = jnp.where(lane < HEAD_DIM, kp, _put3(ones_k, lane, K_BIAS, minus_ck)).astype(BF16)
                v_ref[head] = jnp.where(lane < HEAD_DIM, vp, ones_v).astype(BF16)

    row_f32 = lambda w: pl.BlockSpec((tm, w), lambda i: (i, 0))
    out_shape = [
        jax.ShapeDtypeStruct((lp, D_MODEL), BF16),
        jax.ShapeDtypeStruct((lp, POOL_WIDTH), F32),
        jax.ShapeDtypeStruct((lp, POOL_WIDTH), F32),
        jax.ShapeDtypeStruct((N_HEADS, lp, LANES), BF16),
        jax.ShapeDtypeStruct((N_HEADS, lp, LANES), BF16),
        jax.ShapeDtypeStruct((N_HEADS, lp, LANES), BF16),
        jax.ShapeDtypeStruct((lp, LANES), F32),
        jax.ShapeDtypeStruct((lp, D_MODEL), BF16),
    ]
    heads = pl.BlockSpec((N_HEADS, tm, LANES), lambda i: (0, i, 0))
    out_specs = [row_f32(D_MODEL), row_f32(512), row_f32(512), heads, heads, heads, row_f32(LANES), row_f32(D_MODEL)]
    in_specs = [
        pl.BlockSpec((tm, D_MODEL), lambda i: (jnp.maximum(i - 1, 0), 0)),
        _const((tm, D_MODEL)), _const((1, D_MODEL)),
        _const((D_MODEL, 2560)), _const((D_MODEL, LANES)), _const((1, LANES)),
        _const((4, POOL_GROUP, POOL_GROUP)), _const((1, POOL_WIDTH)), _const((POOL_WIDTH, D_MODEL)),
    ]
    return pl.pallas_call(
        body, name="forward_in", grid=(nt,), out_shape=out_shape, in_specs=in_specs, out_specs=out_specs,
        scratch_shapes=[pltpu.VMEM((tm + MAX_WINDOW, POOL_WIDTH), F32), pltpu.VMEM((1, LANES), F32)],
        compiler_params=_params(("arbitrary",)),
    )(x, tile0, norm_g, w_main, w_f, b_f, pool_w, pool_scale, w_up_pool)


def _causal(tb):
    return lax.broadcasted_iota(jnp.int32, (tb, tb), 1) <= lax.broadcasted_iota(jnp.int32, (tb, tb), 0)


def _pair_lanes(a0, a1):
    lane = lax.broadcasted_iota(jnp.int32, a0.shape, 1)
    return jnp.where(lane < HEAD_DIM, a0, pltpu.roll(a1, HEAD_DIM, axis=1))


def _attention_forward(q, k, v):
    lp = q.shape[1]
    tk = ATT_TILE
    tq_big = ATT_Q_BLOCKS * tk
    n_big = (lp // tk - 1) // ATT_Q_BLOCKS
    assert lp == tk + n_big * tq_big

    def body(q_ref, k_ref, v_ref, o_ref, lse_ref):
        def q_tile(q0, tq):
            lane = lax.broadcasted_iota(jnp.int32, (1, LANES), 1)
            first = q0 // tk
            qs = [q_ref[e, pl.ds(q0, tq), :] for e in range(2)]

            def block(kj):
                return pl.ds(kj * tk if isinstance(kj, int) else pl.multiple_of(kj * tk, tk), tk)

            def scores(e, kj):
                return _dot_nt(qs[e], k_ref[e, block(kj), :])

            def softmax_step(e, kj, m, acc, s):
                m_new = jnp.maximum(m, jnp.max(s, axis=1, keepdims=True))
                p = jnp.exp(s - m_new)
                pv = _dot(p.astype(BF16), v_ref[e, block(kj), :])
                return m_new, jnp.exp(m - m_new) * acc + pv

            def step(kj, carry):
                new = []
                for e in range(2):
                    m, acc, s = carry[e]
                    s_next = scores(e, kj + 1)
                    new.append(softmax_step(e, kj, m, acc, s) + (s_next,))
                return tuple(new)

            init = tuple((jnp.full((tq, 1), NEG, F32), jnp.zeros((tq, LANES), F32), scores(e, 0)) for e in range(2))
            carry = list(lax.fori_loop(0, first, step, init))
            rows = lax.broadcasted_iota(jnp.int32, (tq, tk), 0)
            cols = lax.broadcasted_iota(jnp.int32, (tq, tk), 1)
            for b in range(tq // tk):
                for e in range(2):
                    m, acc, s = carry[e]
                    s_next = scores(e, first + b + 1) if b + 1 < tq // tk else None
                    m, acc = softmax_step(e, first + b, m, acc, jnp.where(cols + b * tk <= rows, s, NEG))
                    carry[e] = (m, acc, s_next)
            outs, lses = [], []
            for e in range(2):
                m, acc, _ = carry[e]
                l = acc[:, V_ONES:V_ONES + 1]
                outs.append(acc / l)
                lses.append(m + jnp.log(l))
            o_ref[pl.ds(q0, tq), :] = _pair_lanes(outs[0], outs[1]).astype(BF16)
            lse_ref[pl.ds(q0, tq), :] = jnp.where(lane == 0, lses[0], jnp.where(lane == 1, lses[1], 0.0))

        q_tile(0, tk)

        def big_tile(i, _):
            q_tile(pl.multiple_of(tk + i * tq_big, tk), tq_big)
            return 0

        lax.fori_loop(0, n_big, big_tile, 0)

    pair = pl.BlockSpec((lp, LANES), lambda hp: (0, hp))
    heads = pl.BlockSpec((2, lp, LANES), lambda hp: (hp, 0, 0), pipeline_mode=pl.Buffered(1))
    return pl.pallas_call(
        body, name="attention_forward", grid=(N_HEADS // 2,),
        out_shape=[jax.ShapeDtypeStruct((lp, ATTN_WIDTH), BF16), jax.ShapeDtypeStruct((lp, ATTN_WIDTH), F32)],
        in_specs=[heads, heads, heads],
        out_specs=[pair, pair],
        compiler_params=_params(("arbitrary",)),
    )(q, k, v)


def _attention_backward(q, k, v, do, o, lse):
    lp = q.shape[1]
    tb = ATT_TILE
    nb = lp // tb
    tq_big = ATT_Q_BLOCKS * tb
    n_big = (nb - 1) // ATT_Q_BLOCKS
    assert lp == tb + n_big * tq_big

    def body(q_ref, k_ref, v_ref, do_ref, o_ref, lse_ref,
             dq_ref, dk_ref, dv_ref, dc_ref, q2_ref, do2_ref, dk_acc, dv_acc):
        lane = lax.broadcasted_iota(jnp.int32, (tb, LANES), 1)

        def prepare(bi, _):
            r0 = pl.multiple_of(bi * tb, tb)
            rows = r0 + lax.broadcasted_iota(jnp.int32, (tb, 1), 0)
            dob = do_ref[pl.ds(r0, tb), :].astype(F32)
            dd = dob * o_ref[pl.ds(r0, tb), :].astype(F32)
            for e in range(2):
                in_head = (lane >= HEAD_DIM * e) & (lane < HEAD_DIM * (e + 1))
                delta = jnp.sum(jnp.where(in_head, dd, 0.0), axis=1, keepdims=True)
                do_e = pltpu.roll(dob, HEAD_DIM, axis=1) if e else dob
                do2_ref[e, pl.ds(r0, tb), :] = jnp.where(
                    lane < HEAD_DIM, do_e, _put3(jnp.zeros((tb, LANES), F32), lane, DO_BIAS, -delta)).astype(BF16)
                minus_lse = jnp.where(rows >= PAD, -lse_ref[pl.ds(r0, tb), e:e + 1], NEG)
                q2_ref[e, pl.ds(r0, tb), :] = _put3(q_ref[e, pl.ds(r0, tb), :].astype(F32), lane, Q_LSE,
                                                    minus_lse).astype(BF16)
            return 0

        lax.fori_loop(0, nb, prepare, 0)
        dk_acc[...] = jnp.zeros_like(dk_acc)
        dv_acc[...] = jnp.zeros_like(dv_acc)

        def q_tile(q0, tq):
            first = q0 // tb
            qs = [q2_ref[e, pl.ds(q0, tq), :] for e in range(2)]
            dos = [do2_ref[e, pl.ds(q0, tq), :] for e in range(2)]

            def block(kj):
                return pl.ds(kj * tb if isinstance(kj, int) else pl.multiple_of(kj * tb, tb), tb)

            def products(e, kj):
                return _dot_nt(qs[e], k_ref[e, block(kj), :]), _dot_nt(dos[e], v_ref[e, block(kj), :])

            def grads(e, kj, dq, s, dpd, mask):
                p = jnp.exp(s)
                if mask is not None:
                    p = jnp.where(mask, p, 0.0)
                dsb = (p * dpd).astype(BF16)
                dv_acc[e, block(kj), :] += _dot_tn(p.astype(BF16), dos[e])
                dk_acc[e, block(kj), :] += _dot_tn(dsb, qs[e])
                return dq + _dot(dsb, k_ref[e, block(kj), :])

            def step(kj, carry):
                new = []
                for e in range(2):
                    dq, s, dpd = carry[e]
                    nxt = products(e, kj + 1)
                    new.append((grads(e, kj, dq, s, dpd, None),) + nxt)
                return tuple(new)

            init = tuple((jnp.zeros((tq, LANES), F32),) + products(e, 0) for e in range(2))
            carry = list(lax.fori_loop(0, first, step, init))
            rows = lax.broadcasted_iota(jnp.int32, (tq, tb), 0)
            cols = lax.broadcasted_iota(jnp.int32, (tq, tb), 1)
            for b in range(tq // tb):
                for e in range(2):
                    dq, s, dpd = carry[e]
                    nxt = products(e, first + b + 1) if b + 1 < tq // tb else (None, None)
                    carry[e] = (grads(e, first + b, dq, s, dpd, cols + b * tb <= rows),) + nxt
            dq0, dq1 = carry[0][0], carry[1][0]
            lane_q = lax.broadcasted_iota(jnp.int32, (tq, LANES), 1)
            dq_ref[pl.ds(q0, tq), :] = (_pair_lanes(dq0, dq1) * 0.125).astype(BF16)
            dc_ref[pl.ds(q0, tq), :] = jnp.where(lane_q == 0, dq0[:, K_ONES:K_ONES + 1],
                                                 jnp.where(lane_q == 1, dq1[:, K_ONES:K_ONES + 1], 0.0))

        q_tile(0, tb)

        def big_tile(i, _):
            q_tile(pl.multiple_of(tb + i * tq_big, tb), tq_big)
            return 0

        lax.fori_loop(0, n_big, big_tile, 0)

        def finish(bi, _):
            r0 = pl.multiple_of(bi * tb, tb)
            dk0, dk1 = dk_acc[0, pl.ds(r0, tb), :], dk_acc[1, pl.ds(r0, tb), :]
            dk_ref[pl.ds(r0, tb), :] = _pair_lanes(dk0, dk1).astype(BF16)
            dv_ref[pl.ds(r0, tb), :] = _pair_lanes(dv_acc[0, pl.ds(r0, tb), :], dv_acc[1, pl.ds(r0, tb), :]).astype(BF16)
            col_sums = jnp.where(lane == 0, dk0[:, Q_ONES:Q_ONES + 1],
                                 jnp.where(lane == 1, dk1[:, Q_ONES:Q_ONES + 1], 0.0))
            dc_ref[pl.ds(r0, tb), :] = dc_ref[pl.ds(r0, tb), :] - col_sums
            return 0

        lax.fori_loop(0, nb, finish, 0)

    once = pl.Buffered(1)
    pair = pl.BlockSpec((lp, LANES), lambda hp: (0, hp))
    pair_in = pl.BlockSpec((lp, LANES), lambda hp: (0, hp), pipeline_mode=once)
    heads = pl.BlockSpec((2, lp, LANES), lambda hp: (hp, 0, 0), pipeline_mode=once)
    wide = jax.ShapeDtypeStruct((lp, ATTN_WIDTH), BF16)
    return pl.pallas_call(
        body, name="attention_backward", grid=(N_HEADS // 2,),
        out_shape=[wide, wide, wide, jax.ShapeDtypeStruct((lp, ATTN_WIDTH), F32)],
        in_specs=[heads, heads, heads, pair_in, pair_in, pair_in],
        out_specs=[pair, pair, pair, pair],
        scratch_shapes=[pltpu.VMEM((2, lp, LANES), BF16), pltpu.VMEM((2, lp, LANES), BF16),
                        pltpu.VMEM((2, lp, LANES), F32), pltpu.VMEM((2, lp, LANES), F32)],
        compiler_params=_params(("arbitrary",)),
    )(q, k, v, do, o, lse)


def _middle(x, target, h, o, a_pool, u, zp, w_main, w_up_pool, w_up_attn, w_out, pool_w, pool_scale, final_g):
    seq = x.shape[0]
    tm = ROW_TILE
    nt = seq // tm + 1
    lp = nt * tm
    halo_blocks = tm // MAX_WINDOW

    def body(x_ref, t_ref, h_ref, o_ref, ap_ref, u_ref, uh_ref, zp_ref,
             wc_ref, wupp_ref, wupa_ref, wout_ref, pw_ref, sc_ref, gf_ref,
             dh2_ref, mg_ref, yp_ref, ya_ref, dap_ref, daa_ref, do_ref, dza_ref, dgp_ref, dga_ref, dzp_ref, dpn_ref,
             loss_ref, dgf_ref, dsc_ref, dpw_ref):
        i = pl.program_id(0)
        tiles = (dh2_ref, mg_ref, yp_ref, ya_ref, dap_ref, daa_ref, do_ref, dza_ref, dgp_ref, dga_ref, dzp_ref, dpn_ref)

        @pl.when(i == 0)
        def _():
            for ref in tiles + (loss_ref, dgf_ref, dsc_ref, dpw_ref):
                ref[...] = jnp.zeros_like(ref)

        @pl.when(i > 0)
        def _():
            xt = x_ref[...]
            hb = h_ref[...]
            pc = _dot(hb, wc_ref[...])
            za, gp, ga = pc[:, :512], pc[:, 512:1536], pc[:, 1536:]
            of = o_ref[...].astype(F32)
            sza = _sigmoid(za)
            silu_za = za * sza
            ya = (of * silu_za).astype(BF16)
            ya_ref[...] = ya
            aa = _dot(ya, wupa_ref[...])
            ap = ap_ref[...].astype(F32)
            sgp, sga = _sigmoid(gp), _sigmoid(ga)
            mg = (sgp * ap + sga * aa).astype(BF16)
            mg_ref[...] = mg
            h2 = xt + _dot(mg, wout_ref[...])
            r2 = lax.rsqrt(jnp.mean(h2 * h2, axis=-1, keepdims=True) + RMS_EPS)
            h2n = h2 * r2
            gf = gf_ref[...]
            diff = h2n * gf - t_ref[...]
            loss_ref[...] += 0.5 * jnp.sum(jnp.mean(diff * diff, axis=-1, keepdims=True), axis=0, keepdims=True)
            dy = diff * (1.0 / D_MODEL)
            dgf_ref[...] += jnp.sum(dy * h2n, axis=0, keepdims=True)
            dyg = dy * gf
            dh2 = r2 * (dyg - h2n * jnp.mean(dyg * h2n, axis=-1, keepdims=True))
            dh2_ref[...] = dh2
            dmg = _dot_nt(dh2.astype(BF16), wout_ref[...])
            dap = (dmg * sgp).astype(BF16)
            daa = (dmg * sga).astype(BF16)
            dap_ref[...] = dap
            daa_ref[...] = daa
            dgp_ref[...] = (dmg * ap * sgp * (1.0 - sgp)).astype(BF16)
            dga_ref[...] = (dmg * aa * sga * (1.0 - sga)).astype(BF16)
            dyp = _dot_nt(dap, wupp_ref[...])
            dya = _dot_nt(daa, wupa_ref[...])
            do_ref[...] = (dya * silu_za).astype(BF16)
            dza_ref[...] = (dya * of * (sza * (1.0 + za * (1.0 - sza)))).astype(BF16)

            u = u_ref[...]
            zp = zp_ref[...]
            counts = _pool_counts(i * tm, tm)
            ps = _pool_means(jnp.concatenate([uh_ref[...], u], axis=0), u, counts)
            pbs = [p.astype(BF16) for p in ps]
            ppw = jnp.concatenate([_dot(pbs[g], pw_ref[g]) for g in range(4)], axis=1)
            sc = sc_ref[...]
            szp = _sigmoid(zp)
            silu_zp = zp * szp
            ypre = ppw * sc
            yp_ref[...] = (ypre * silu_zp).astype(BF16)
            dypre = dyp * silu_zp
            dzp_ref[...] = (dyp * ypre * (szp * (1.0 + zp * (1.0 - szp)))).astype(BF16)
            dsc_ref[...] += jnp.sum(dypre * ppw, axis=0, keepdims=True)
            dppw = (dypre * sc).astype(BF16)
            dpns = []
            for g in range(4):
                dg = dppw[:, POOL_GROUP * g:POOL_GROUP * (g + 1)]
                dpw_ref[g] += _dot_tn(pbs[g], dg)
                dpns.append(_dot_nt(dg, pw_ref[g]) / counts[g])
            dpn_ref[...] = jnp.concatenate(dpns, axis=1)

    real = lambda w: pl.BlockSpec((tm, w), lambda i: (jnp.maximum(i - 1, 0), 0))
    row = lambda w: pl.BlockSpec((tm, w), lambda i: (i, 0))
    in_specs = [
        real(D_MODEL), real(D_MODEL), row(D_MODEL), row(512), row(D_MODEL), row(512),
        pl.BlockSpec((MAX_WINDOW, 512), lambda i: (jnp.maximum(i * halo_blocks - 1, 0), 0)), row(512),
        _const((D_MODEL, 2560), (0, 1)), _const((POOL_WIDTH, D_MODEL)), _const((ATTN_WIDTH, D_MODEL)),
        _const((D_MODEL, D_MODEL)), _const((4, POOL_GROUP, POOL_GROUP)), _const((1, POOL_WIDTH)), _const((1, D_MODEL)),
    ]
    sd = jax.ShapeDtypeStruct
    out_shape = [
        sd((lp, D_MODEL), F32),
        sd((lp, D_MODEL), BF16),
        sd((lp, 512), BF16),
        sd((lp, 512), BF16),
        sd((lp, D_MODEL), BF16),
        sd((lp, D_MODEL), BF16),
        sd((lp, 512), BF16),
        sd((lp, 512), BF16),
        sd((lp, D_MODEL), BF16),
        sd((lp, D_MODEL), BF16),
        sd((lp, 512), BF16),
        sd((lp, 512), F32),
        sd((1, 1), F32),
        sd((1, D_MODEL), F32),
        sd((1, 512), F32),
        sd((4, POOL_GROUP, POOL_GROUP), F32),
    ]
    keep = lambda shape: pl.BlockSpec(shape, lambda i: (0,) * len(shape))
    out_specs = [row(D_MODEL), row(D_MODEL), row(512), row(512), row(D_MODEL), row(D_MODEL), row(512), row(512),
                 row(D_MODEL), row(D_MODEL), row(512), row(512),
                 keep((1, 1)), keep((1, D_MODEL)), keep((1, 512)), keep((4, POOL_GROUP, POOL_GROUP))]
    return pl.pallas_call(
        body, name="middle", grid=(nt,), out_shape=out_shape, in_specs=in_specs, out_specs=out_specs,
        compiler_params=_params(("arbitrary",)),
    )(x, target, h, o, a_pool, u, u, zp, w_main, w_up_pool, w_up_attn, w_out, pool_w, pool_scale, final_g)


def _backward_in(x, tile0, norm_g, dh2, dpn, dzp, dq, dk, dv, dza, dgp, dga, dc, sneg, w_main, w_f):
    seq = x.shape[0]
    tm = ROW_TILE
    nt = seq // tm + 1
    lp = nt * tm
    halo_blocks = tm // MAX_WINDOW
    last_halo = lp // MAX_WINDOW - 1

    def body(x_ref, t0_ref, g_ref, dh2_ref, dpn_ref, dpnh_ref, dzp_ref, dq_ref, dk_ref, dv_ref, dza_ref,
             dgp_ref, dga_ref, dc_ref, sn_ref, wm_ref, wf_ref,
             dproj_ref, df_ref, gx_ref, gmeta_ref, dg_ref, dbf_ref, carry_ref):
        i = pl.program_id(0)
        t = nt - 1 - i

        @pl.when(i == 0)
        def _():
            carry_ref[...] = jnp.zeros_like(carry_ref)
            dg_ref[...] = jnp.zeros_like(dg_ref)
            dbf_ref[...] = jnp.zeros_like(dbf_ref)

        dpn_t = dpn_ref[...]
        ahead = jnp.where(i == 0, jnp.zeros_like(dpnh_ref), dpnh_ref[...])
        ext = jnp.concatenate([dpn_t, ahead], axis=0)
        counts = _pool_counts(t * tm, tm)
        for g, w in enumerate(POOL_WINDOWS):
            s = ext[:, POOL_GROUP * g:POOL_GROUP * (g + 1)]
            sh = 1
            while sh < w:
                s = s + pltpu.roll(s, tm + MAX_WINDOW - sh, axis=0)
                sh *= 2
            du = s[:tm, :] - dpn_t[:, POOL_GROUP * g:POOL_GROUP * (g + 1)] * counts[g]
            dproj_ref[:, POOL_GROUP * g:POOL_GROUP * (g + 1)] = du.astype(BF16)
        dproj_ref[:, 512:1024] = dzp_ref[...]
        dproj_ref[:, 1024:1536] = dq_ref[...]
        dproj_ref[:, 1536:2048] = dk_ref[...]
        dproj_ref[:, 2048:2560] = dv_ref[...]
        dproj_ref[:, 2560:3072] = dza_ref[...]
        dproj_ref[:, 3072:4096] = dgp_ref[...]
        dproj_ref[:, 4096:5120] = dga_ref[...]

        dct = dc_ref[:, 0:LANES]
        for hp in range(1, N_HEADS // 2):
            dct = dct + pltpu.roll(dc_ref[:, LANES * hp:LANES * (hp + 1)], 2 * hp, axis=1)
        rloc = lax.broadcasted_iota(jnp.int32, (tm, LANES), 0)
        sh = 1
        while sh < tm:
            dct = dct + jnp.where(rloc + sh < tm, pltpu.roll(dct, tm - sh, axis=0), 0.0)
            sh *= 2
        dct = dct + carry_ref[...]
        carry_ref[...] = dct[0:1, :]
        df = dct * sn_ref[...]
        dbf_ref[...] += jnp.sum(df, axis=0, keepdims=True)
        dfb = df.astype(BF16)
        df_ref[...] = dfb

        dh = _dot_nt(dproj_ref[...], wm_ref[...]) + _dot_nt(dfb, wf_ref[...])
        xt = jnp.where(t == 0, t0_ref[...], x_ref[...])
        r = lax.rsqrt(jnp.mean(xt * xt, axis=-1, keepdims=True) + RMS_EPS)
        xn = xt * r
        dg_ref[...] += jnp.sum(dh * xn, axis=0, keepdims=True)
        dhg = dh * g_ref[...]
        dx = dh2_ref[...] + r * (dhg - xn * jnp.mean(dhg * xn, axis=-1, keepdims=True))

        @pl.when(t > 0)
        def _():
            gx_ref[...] = dx

        @pl.when(t == 0)
        def _():
            gmeta_ref[...] = dx[PAD:, :]

    rev = lambda w: pl.BlockSpec((tm, w), lambda i: (nt - 1 - i, 0))
    real = pl.BlockSpec((tm, D_MODEL), lambda i: (jnp.maximum(nt - 2 - i, 0), 0))
    in_specs = [
        real, _const((tm, D_MODEL)), _const((1, D_MODEL)), rev(D_MODEL), rev(512),
        pl.BlockSpec((MAX_WINDOW, 512), lambda i: (jnp.minimum((nt - i) * halo_blocks, last_halo), 0)),
        rev(512), rev(512), rev(512), rev(512), rev(512), rev(D_MODEL), rev(D_MODEL),
        rev(512), rev(LANES),
        _const((D_MODEL, N_MAIN)), _const((D_MODEL, LANES)),
    ]
    sd = jax.ShapeDtypeStruct
    out_shape = [sd((lp, N_MAIN), BF16), sd((lp, LANES), BF16), sd((seq, D_MODEL), F32), sd((N_META, D_MODEL), F32),
                 sd((1, D_MODEL), F32), sd((1, LANES), F32)]
    keep = lambda shape: pl.BlockSpec(shape, lambda i: (0,) * len(shape))
    out_specs = [rev(N_MAIN), rev(LANES), real, keep((N_META, D_MODEL)), keep((1, D_MODEL)), keep((1, LANES))]
    return pl.pallas_call(
        body, name="backward_in", grid=(nt,), out_shape=out_shape, in_specs=in_specs, out_specs=out_specs,
        scratch_shapes=[pltpu.VMEM((1, LANES), F32)],
        compiler_params=_params(("arbitrary",)),
    )(x, tile0, norm_g, dh2, dpn, dpn, dzp, dq, dk, dv, dza, dgp, dga, dc, sneg, w_main, w_f)


def _matmul_tn(name, a, b, tn):
    lp, m = a.shape
    n = b.shape[1]

    def body(a_ref, b_ref, c_ref):
        c_ref[...] = _dot_tn(a_ref[...].astype(BF16), b_ref[...].astype(BF16))

    return pl.pallas_call(
        body, name=name, grid=(n // tn,), out_shape=jax.ShapeDtypeStruct((m, n), F32),
        in_specs=[_const((lp, m)), pl.BlockSpec((lp, tn), lambda j: (0, j))],
        out_specs=pl.BlockSpec((m, tn), lambda j: (0, j)),
        compiler_params=_params(("arbitrary",)),
    )(a, b)


def _adamw(name, parts, w, m, v, rows):
    r, c = w.shape
    n_parts = parts.shape[0]

    def body(p_ref, w_ref, m_ref, v_ref, g_ref, d_ref, mo_ref, vo_ref):
        g = p_ref[0].astype(F32)
        for s in range(1, n_parts):
            g = g + p_ref[s].astype(F32)
        m_new = ADAM_B1 * m_ref[...] + (1.0 - ADAM_B1) * g
        v_new = ADAM_B2 * v_ref[...] + (1.0 - ADAM_B2) * (g * g)
        m_hat = m_new / (1.0 - ADAM_B1 ** ADAM_STEP)
        v_hat = v_new / (1.0 - ADAM_B2 ** ADAM_STEP)
        g_ref[...] = g
        d_ref[...] = -ADAM_LR * (m_hat / (jnp.sqrt(v_hat) + ADAM_EPS) + ADAM_WD * w_ref[...])
        mo_ref[...] = m_new
        vo_ref[...] = v_new

    blk = pl.BlockSpec((rows, c), lambda i: (i, 0))
    return pl.pallas_call(
        body, name=name, grid=(r // rows,), out_shape=[jax.ShapeDtypeStruct((r, c), F32)] * 4,
        in_specs=[pl.BlockSpec((n_parts, rows, c), lambda i: (0, i, 0)), blk, blk, blk],
        out_specs=[blk] * 4,
        compiler_params=_params(("arbitrary",)),
    )(parts, w, m, v)


def _pair_sum(name, mine, theirs, rows):
    n, r, c = mine.shape

    def body(a_ref, b_ref, o_ref):
        o_ref[...] = (a_ref[...] + b_ref[...].astype(F32)).astype(BF16)

    blk = pl.BlockSpec((1, rows, c), lambda j, i: (j, i, 0))
    return pl.pallas_call(
        body, name=name, grid=(n, r // rows), out_shape=jax.ShapeDtypeStruct((n, r, c), BF16),
        in_specs=[blk, blk], out_specs=blk,
        compiler_params=_params(("arbitrary", "arbitrary")),
    )(mine, theirs)


def _columns_to_slots(a):
    r, c8 = a.shape
    return a.reshape(r, N_DEV, c8 // N_DEV).transpose(1, 0, 2)


def _by_core(slots):
    by_core = slots.reshape((4, 2) + slots.shape[1:]).swapaxes(0, 1)
    c = lax.axis_index("c")
    return (lax.dynamic_index_in_dim(by_core, c, 0, keepdims=False),
            lax.dynamic_index_in_dim(by_core, 1 - c, 0, keepdims=False).astype(BF16))


def _slots_to_columns(a):
    n, r, c = a.shape
    return a.transpose(1, 0, 2).reshape(r, n * c)


def kernel(x, meta_tokens, norm_g, w_in, b_forget, pool_w, pool_scale, w_up_pool, w_up_attn, w_out, final_norm_g, loss_target, m_meta_tokens, m_norm_g, m_w_in, m_b_forget, m_pool_w, m_pool_scale, m_w_up_pool, m_w_up_attn, m_w_out, m_final_norm_g, v_meta_tokens, v_norm_g, v_w_in, v_b_forget, v_pool_w, v_pool_scale, v_w_up_pool, v_w_up_attn, v_w_out, v_final_norm_g):
    xs = x[0]
    target = loss_target[0]

    g_in, g_upp, g_upa, g_out, g_meta = _gather_two_level(
        "gather_weights",
        [w_in[0].astype(BF16), w_up_pool[0].astype(BF16), w_up_attn[0].astype(BF16), w_out[0].astype(BF16), meta_tokens])
    w_full = _slots_to_columns(g_in)
    w_main = jnp.concatenate([w_full[:, :N_BEFORE_F], w_full[:, N_BEFORE_F + N_HEADS:]], axis=1)
    w_f = jnp.pad(w_full[:, N_BEFORE_F:N_BEFORE_F + N_HEADS], ((0, 0), (0, LANES - N_HEADS)))
    wupp = _slots_to_columns(g_upp)
    wupa = _slots_to_columns(g_upa)
    wout = g_out.reshape(D_MODEL, D_MODEL)
    meta = _slots_to_columns(g_meta)
    tile0 = jnp.concatenate([jnp.zeros((PAD, D_MODEL), F32), meta], axis=0)
    b_f = jnp.pad(b_forget, ((0, 0), (0, LANES - N_HEADS)))
    pw_b = pool_w[0].astype(BF16)
    final_g = final_norm_g.reshape(1, D_MODEL)

    h, u, zp, q, k, v, sneg, a_pool = _forward_in(xs, tile0, norm_g, w_main, w_f, b_f, pw_b, pool_scale, wupp)
    o, lse = _attention_forward(q, k, v)
    (dh2, mg, yp, ya, dap, daa, do, dza, dgp, dga, dzp, dpn,
     loss_part, d_final_g, d_scale, d_pool_w) = _middle(xs, target, h, o, a_pool, u, zp, w_main, wupp, wupa, wout,
                                                        pw_b, pool_scale, final_g)
    dq, dk, dv, dc = _attention_backward(q, k, v, do, o, lse)
    dproj, df, grad_x, d_meta, d_norm_g, d_bf = _backward_in(xs, tile0, norm_g, dh2, dpn, dzp, dq, dk, dv, dza,
                                                             dgp, dga, dc, sneg, w_main, w_f)
    dw_main = _matmul_tn("grad_w_in", h, dproj, 512)
    dw_f = _matmul_tn("grad_w_forget", h, df, LANES)
    dw_out = _matmul_tn("grad_w_out", mg, dh2, 256)
    dw_upp = _matmul_tn("grad_w_up_pool", yp, dap, 512)
    dw_upa = _matmul_tn("grad_w_up_attn", ya, daa, 512)
    dw_in = jnp.concatenate([dw_main[:, :N_BEFORE_F], dw_f[:, :N_HEADS], dw_main[:, N_BEFORE_F:]], axis=1)

    big = [_columns_to_slots(dw_in), _columns_to_slots(dw_upp), _columns_to_slots(dw_upa),
           dw_out.reshape(N_DEV, D_MODEL // N_DEV, D_MODEL)]
    mine, for_sibling = zip(*[_by_core(s) for s in big])
    from_sibling = _exchange("swap_with_sibling", [("swap", a, (SIBLING,)) for a in for_sibling])
    chip_rows = (128, 512, 512, 128)
    sums = [_pair_sum(f"pair_sum_{j}", mine[j], from_sibling[j], chip_rows[j]) for j in range(4)]
    (p_in, p_upp, p_upa, p_out, p_meta, p_norm_g, p_bf, p_pool_w, p_scale, p_final_g) = _exchange(
        "exchange_gradients",
        [("chips", s, SAME_CORE) for s in sums]
        + [("scatter", _columns_to_slots(d_meta), ALL_PEERS)]
        + [("gather", a, ALL_PEERS) for a in
           (d_norm_g, d_bf, d_pool_w.reshape(4 * POOL_GROUP, POOL_GROUP), d_scale, d_final_g)])

    loss = lax.psum(loss_part[0, 0], ("x", "y", "c"))

    def pad_f(a):
        return jnp.pad(a, ((0, 0), (0, LANES - N_HEADS)))

    res = {}
    res["meta_tokens"] = _adamw("adamw_meta", p_meta, meta_tokens, m_meta_tokens, v_meta_tokens, N_META)
    res["norm_g"] = _adamw("adamw_norm_g", p_norm_g, norm_g, m_norm_g, v_norm_g, 1)
    res["w_in"] = _adamw("adamw_w_in", p_in, w_in[0], m_w_in[0], v_w_in[0], 128)
    bf = _adamw("adamw_b_forget", p_bf, pad_f(b_forget), pad_f(m_b_forget), pad_f(v_b_forget), 1)
    res["b_forget"] = [a[:, :N_HEADS] for a in bf]
    pw = _adamw("adamw_pool_w", p_pool_w, pool_w.reshape(512, 128), m_pool_w.reshape(512, 128),
                v_pool_w.reshape(512, 128), 512)
    res["pool_w"] = [a.reshape(pool_w.shape) for a in pw]
    res["pool_scale"] = _adamw("adamw_pool_scale", p_scale, pool_scale, m_pool_scale, v_pool_scale, 1)
    res["w_up_pool"] = _adamw("adamw_w_up_pool", p_upp, w_up_pool[0], m_w_up_pool[0], v_w_up_pool[0], 512)
    res["w_up_attn"] = _adamw("adamw_w_up_attn", p_upa, w_up_attn[0], m_w_up_attn[0], v_w_up_attn[0], 512)
    res["w_out"] = _adamw("adamw_w_out", p_out, w_out[0], m_w_out[0], v_w_out[0], 128)
    fg = _adamw("adamw_final_norm_g", p_final_g, final_g, m_final_norm_g.reshape(1, D_MODEL),
                v_final_norm_g.reshape(1, D_MODEL), 1)
    res["final_norm_g"] = [a.reshape(D_MODEL) for a in fg]
    for name in ("w_in", "w_up_pool", "w_up_attn", "w_out"):
        res[name] = [a[None] for a in res[name]]

    order = ["meta_tokens", "norm_g", "w_in", "b_forget", "pool_w", "pool_scale", "w_up_pool", "w_up_attn", "w_out",
             "final_norm_g"]
    outs = [loss, grad_x[None]]
    for part in range(4):
        outs += [res[name][part] for name in order]
    return tuple(outs)
```

```python
import functools

import jax
import jax.numpy as jnp
from jax import lax
from jax.experimental import pallas as pl
from jax.experimental.pallas import tpu as pltpu

F32 = jnp.float32
BF16 = jnp.bfloat16

D_MODEL = 1024
N_META = 16
POOL_WIDTH = 512
ATTN_WIDTH = 512
N_HEADS = 8
HEAD_DIM = 64
POOL_WINDOWS = (2, 4, 8, 16)
POOL_GROUP = 128
MAX_WINDOW = 16
RMS_EPS = 1e-6
N_MAIN = 5120
N_BEFORE_F = 3072
N_DEV = 8
LANES = 128

ROW_TILE = 256
ATT_TILE = 256
ATT_Q_BLOCKS = 2
PAD = ROW_TILE - N_META
VMEM_LIMIT = 56 * 1024 * 1024

ADAM_LR = 0.001
ADAM_B1 = 0.9
ADAM_B2 = 0.999
ADAM_EPS = 1e-08
ADAM_WD = 0.01
ADAM_STEP = 10

NEG = -1e30
MESH = pl.DeviceIdType.MESH


def _params(sem=None):
    kw = dict(vmem_limit_bytes=VMEM_LIMIT)
    if sem is not None:
        kw["dimension_semantics"] = sem
    return pltpu.CompilerParams(**kw)


def _const(shape, block_index=None):
    idx = block_index or (0,) * len(shape)
    return pl.BlockSpec(shape, lambda i: idx, pipeline_mode=pl.Buffered(1))


def _sigmoid(x):
    return jax.nn.sigmoid(x)


def _dot(a, b):
    return jnp.dot(a, b, preferred_element_type=F32)


def _dot_nt(a, b):
    return lax.dot_general(a, b, (((1,), (1,)), ((), ())), preferred_element_type=F32)


def _dot_tn(a, b):
    return lax.dot_general(a, b, (((0,), (0,)), ((), ())), preferred_element_type=F32)


def _pool_counts(first_row, rows):
    row = first_row + lax.broadcasted_iota(jnp.int32, (rows, 1), 0)
    pos1 = row - PAD + 1
    return [jnp.clip(pos1, 1, w).astype(F32) for w in POOL_WINDOWS]


def _pool_means(u_ext, u, counts):
    rows = u.shape[0]
    out = []
    for g, w in enumerate(POOL_WINDOWS):
        s = u_ext[:, POOL_GROUP * g:POOL_GROUP * (g + 1)]
        sh = 1
        while sh < w:
            s = s + pltpu.roll(s, sh, axis=0)
            sh *= 2
        out.append(s[MAX_WINDOW:MAX_WINDOW + rows, :] / counts[g] - u[:, POOL_GROUP * g:POOL_GROUP * (g + 1)])
    return out


Q_BIAS, Q_ONES, Q_LSE = 64, 67, 70
K_ONES, K_BIAS, K_ONES2 = 64, 67, 70
V_ONES = 64
DO_BIAS = 64


def _lane_ones(lane, ranges):
    hit = None
    for lo, hi in ranges:
        r = (lane >= lo) & (lane < hi)
        hit = r if hit is None else hit | r
    return jnp.where(hit, 1.0, 0.0)


def _put3(base, lane, first, x):
    hi = x.astype(BF16).astype(F32)
    rest = x - hi
    mid = rest.astype(BF16).astype(F32)
    lo = (rest - mid).astype(BF16).astype(F32)
    for j, piece in enumerate((hi, mid, lo)):
        base = jnp.where(lane == first + j, piece, base)
    return base


SIBLING = 1
SAME_CORE = (2, 4, 6)
ALL_PEERS = (1, 2, 3, 4, 5, 6, 7)


def _place():
    return lax.axis_index("x"), lax.axis_index("y"), lax.axis_index("c")


def _peer(r):
    x, y, c = _place()
    return (1 - x if r & 4 else x, 1 - y if r & 2 else y, 1 - c if r & 1 else c)


def _device_slot(p):
    return 4 * p[0] + 2 * p[1] + p[2]


def _chip_slot(p):
    return 2 * p[0] + p[1]


def _exchange(name, items):
    n = len(items)

    def body(*refs):
        ins, outs = refs[:n], refs[n:2 * n]
        send_sems, recv_sems, local_sems = refs[2 * n:]
        me = _place()
        copies = []
        for a, (kind, _, peers) in enumerate(items):
            slot = _chip_slot if kind == "chips" else _device_slot
            for r in peers:
                peer = _peer(r)
                src = ins[a] if kind in ("swap", "gather") else ins[a].at[slot(peer)]
                dst = outs[a] if kind == "swap" else outs[a].at[slot(me)]
                cp = pltpu.make_async_remote_copy(
                    src_ref=src, dst_ref=dst, send_sem=send_sems.at[a, r - 1], recv_sem=recv_sems.at[a, r - 1],
                    device_id=peer, device_id_type=MESH)
                cp.start()
                copies.append(cp)
            if kind != "swap":
                src = ins[a] if kind == "gather" else ins[a].at[slot(me)]
                mine = pltpu.make_async_copy(src, outs[a].at[slot(me)], local_sems.at[a])
                mine.start()
                copies.append(mine)
        for cp in copies:
            cp.wait()

    hbm = pl.BlockSpec(memory_space=pl.ANY)
    out_shape = [jax.ShapeDtypeStruct(((N_DEV,) if kind == "gather" else ()) + a.shape, a.dtype)
                 for kind, a, _ in items]
    return pl.pallas_call(
        body, name=name, out_shape=out_shape,
        in_specs=[hbm] * n, out_specs=[hbm] * n,
        scratch_shapes=[pltpu.SemaphoreType.DMA((n, N_DEV - 1)), pltpu.SemaphoreType.DMA((n, N_DEV - 1)),
                        pltpu.SemaphoreType.DMA((n,))],
    )(*[a for _, a, _ in items])


def _gather_two_level(name, arrays):
    n = len(arrays)

    def body(*refs):
        ins, outs = refs[:n], refs[n:2 * n]
        send_sems, recv_sems, local_sems = refs[2 * n:]
        me = _place()
        sibling = _peer(SIBLING)

        def copy(a, k, block, to, src=None):
            rows = outs[a].at[_device_slot(block)]
            return pltpu.make_async_remote_copy(
                src_ref=rows if src is None else src, dst_ref=rows,
                send_sem=send_sems.at[a, k], recv_sem=recv_sems.at[a, k], device_id=to, device_id_type=MESH)

        sends, own = [], []
        for a in range(n):
            mine = pltpu.make_async_copy(ins[a], outs[a].at[_device_slot(me)], local_sems.at[a])
            mine.start()
            own.append(mine)
            for k, r in enumerate((SIBLING,) + SAME_CORE):
                cp = copy(a, k, me, _peer(r), src=ins[a])
                cp.start()
                sends.append(cp)
        for a in range(n):
            for j, r in enumerate(SAME_CORE):
                copy(a, 1 + j, _peer(r), me).wait_recv()
                passed = copy(a, 4 + j, _peer(r), sibling)
                passed.start()
                sends.append(passed)
        for a in range(n):
            copy(a, 0, sibling, me).wait_recv()
            for j, r in enumerate(SAME_CORE):
                copy(a, 4 + j, _peer(r | SIBLING), me).wait_recv()
        for cp in sends:
            cp.wait_send()
        for cp in own:
            cp.wait()

    hbm = pl.BlockSpec(memory_space=pl.ANY)
    return pl.pallas_call(
        body, name=name, out_shape=[jax.ShapeDtypeStruct((N_DEV,) + a.shape, a.dtype) for a in arrays],
        in_specs=[hbm] * n, out_specs=[hbm] * n,
        scratch_shapes=[pltpu.SemaphoreType.DMA((n, N_DEV - 1)), pltpu.SemaphoreType.DMA((n, N_DEV - 1)),
                        pltpu.SemaphoreType.DMA((n,))],
    )(*arrays)


def _forward_in(x, tile0, norm_g, w_main, w_f, b_f, pool_w, pool_scale, w_up_pool):
    seq = x.shape[0]
    nt = seq // ROW_TILE + 1
    lp = nt * ROW_TILE
    tm = ROW_TILE

    def body(x_ref, t0_ref, g_ref, wa_ref, wf_ref, bf_ref, pw_ref, sc_ref, wup_ref,
             h_ref, u_ref, zp_ref, q_ref, k_ref, v_ref, sn_ref, ap_ref,
             uext_ref, carry_ref):
        i = pl.program_id(0)

        @pl.when(i == 0)
        def _():
            uext_ref[...] = jnp.zeros_like(uext_ref)
            carry_ref[...] = jnp.zeros_like(carry_ref)

        xt = jnp.where(i == 0, t0_ref[...], x_ref[...])
        r = lax.rsqrt(jnp.mean(xt * xt, axis=-1, keepdims=True) + RMS_EPS)
        h = (xt * r * g_ref[...]).astype(BF16)
        h_ref[...] = h
        pa = _dot_nt(h, wa_ref[...])
        u = pa[:, :512]
        zp = pa[:, 512:1024]
        u_ref[...] = u
        zp_ref[...] = zp

        uext_ref[0:MAX_WINDOW, :] = uext_ref[tm:tm + MAX_WINDOW, :]
        uext_ref[MAX_WINDOW:, :] = u
        counts = _pool_counts(i * tm, tm)
        ps = _pool_means(uext_ref[...], u, counts)
        ppw = jnp.concatenate([_dot(ps[g].astype(BF16), pw_ref[g]) for g in range(4)], axis=1)
        y_pool = ppw * sc_ref[...] * (zp * _sigmoid(zp))
        ap_ref[...] = _dot(y_pool.astype(BF16), wup_ref[...]).astype(BF16)

        fl = _dot_nt(h, wf_ref[...]) + bf_ref[...]
        row = i * tm + lax.broadcasted_iota(jnp.int32, (tm, LANES), 0)
        rloc = lax.broadcasted_iota(jnp.int32, (tm, LANES), 0)
        lane = lax.broadcasted_iota(jnp.int32, (tm, LANES), 1)
        live = (row >= PAD) & (lane < N_HEADS)
        logf = jnp.minimum(fl, 0.0) - jnp.log1p(jnp.exp(-jnp.abs(fl)))
        cs = jnp.where(live, logf, 0.0)
        sh = 1
        while sh < tm:
            cs = cs + jnp.where(rloc >= sh, pltpu.roll(cs, sh, axis=0), 0.0)
            sh *= 2
        cs = cs + carry_ref[...]
        carry_ref[...] = cs[tm - 1:tm, :]
        sn_ref[...] = jnp.where(live, _sigmoid(-fl), 0.0)

        rows1 = i * tm + lax.broadcasted_iota(jnp.int32, (tm, 1), 0)
        ones_q = _lane_ones(lane, ((Q_ONES, Q_ONES + 3),))
        ones_k = _lane_ones(lane, ((K_ONES, K_ONES + 3), (K_ONES2, K_ONES2 + 3)))
        ones_v = _lane_ones(lane, ((V_ONES, V_ONES + 3),))
        for hp in range(N_HEADS // 2):
            qp = pa[:, 1024 + LANES * hp:1024 + LANES * (hp + 1)] * 0.125
            kp = pa[:, 1536 + LANES * hp:1536 + LANES * (hp + 1)]
            vp = pa[:, 2048 + LANES * hp:2048 + LANES * (hp + 1)]
            for e in range(2):
                head = 2 * hp + e
                if e:
                    qp, kp, vp = (pltpu.roll(a, HEAD_DIM, axis=1) for a in (qp, kp, vp))
                c_h = cs[:, head:head + 1]
                q_ref[head] = jnp.where(lane < HEAD_DIM, qp, _put3(ones_q, lane, Q_BIAS, c_h)).astype(BF16)
                minus_ck = jnp.where(rows1 >= PAD, -c_h, NEG)
                k_ref[head] = jnp.where(lane < HEAD_DIM, kp, _put3(ones_k, lane, K_BIAS, minus_ck)).astype(BF16)
                v_ref[head] = jnp.where(lane < HEAD_DIM, vp, ones_v).astype(BF16)

    row_f32 = lambda w: pl.BlockSpec((tm, w), lambda i: (i, 0))
    out_shape = [
        jax.ShapeDtypeStruct((lp, D_MODEL), BF16),
        jax.ShapeDtypeStruct((lp, POOL_WIDTH), F32),
        jax.ShapeDtypeStruct((lp, POOL_WIDTH), F32),
        jax.ShapeDtypeStruct((N_HEADS, lp, LANES), BF16),
        jax.ShapeDtypeStruct((N_HEADS, lp, LANES), BF16),
        jax.ShapeDtypeStruct((N_HEADS, lp, LANES), BF16),
        jax.ShapeDtypeStruct((lp, LANES), F32),
        jax.ShapeDtypeStruct((lp, D_MODEL), BF16),
    ]
    heads = pl.BlockSpec((N_HEADS, tm, LANES), lambda i: (0, i, 0))
    out_specs = [row_f32(D_MODEL), row_f32(512), row_f32(512), heads, heads, heads, row_f32(LANES), row_f32(D_MODEL)]
    in_specs = [
        pl.BlockSpec((tm, D_MODEL), lambda i: (jnp.maximum(i - 1, 0), 0)),
        _const((tm, D_MODEL)), _const((1, D_MODEL)),
        _const((2560, D_MODEL)), _const((LANES, D_MODEL)), _const((1, LANES)),
        _const((4, POOL_GROUP, POOL_GROUP)), _const((1, POOL_WIDTH)), _const((POOL_WIDTH, D_MODEL)),
    ]
    return pl.pallas_call(
        body, name="forward_in", grid=(nt,), out_shape=out_shape, in_specs=in_specs, out_specs=out_specs,
        scratch_shapes=[pltpu.VMEM((tm + MAX_WINDOW, POOL_WIDTH), F32), pltpu.VMEM((1, LANES), F32)],
        compiler_params=_params(("arbitrary",)),
    )(x, tile0, norm_g, w_main, w_f, b_f, pool_w, pool_scale, w_up_pool)


def _causal(tb):
    return lax.broadcasted_iota(jnp.int32, (tb, tb), 1) <= lax.broadcasted_iota(jnp.int32, (tb, tb), 0)


def _pair_lanes(a0, a1):
    lane = lax.broadcasted_iota(jnp.int32, a0.shape, 1)
    return jnp.where(lane < HEAD_DIM, a0, pltpu.roll(a1, HEAD_DIM, axis=1))


def _attention_forward(q, k, v):
    lp = q.shape[1]
    tk = ATT_TILE
    tq_big = ATT_Q_BLOCKS * tk
    n_big = (lp // tk - 1) // ATT_Q_BLOCKS
    assert lp == tk + n_big * tq_big

    def body(q_ref, k_ref, v_ref, o_ref, lse_ref):
        def q_tile(q0, tq):
            lane = lax.broadcasted_iota(jnp.int32, (1, LANES), 1)
            first = q0 // tk
            qs = [q_ref[e, pl.ds(q0, tq), :] for e in range(2)]

            def block(kj):
                return pl.ds(kj * tk if isinstance(kj, int) else pl.multiple_of(kj * tk, tk), tk)

            def scores(e, kj):
                return _dot_nt(qs[e], k_ref[e, block(kj), :])

            def softmax_step(e, kj, m, acc, s):
                m_new = jnp.maximum(m, jnp.max(s, axis=1, keepdims=True))
                p = jnp.exp(s - m_new)
                pv = _dot(p.astype(BF16), v_ref[e, block(kj), :])
                return m_new, jnp.exp(m - m_new) * acc + pv

            def step(kj, carry):
                new = []
                for e in range(2):
                    m, acc, s = carry[e]
                    s_next = scores(e, kj + 1)
                    new.append(softmax_step(e, kj, m, acc, s) + (s_next,))
                return tuple(new)

            init = tuple((jnp.full((tq, 1), NEG, F32), jnp.zeros((tq, LANES), F32), scores(e, 0)) for e in range(2))
            carry = list(lax.fori_loop(0, first, step, init))
            rows = lax.broadcasted_iota(jnp.int32, (tq, tk), 0)
            cols = lax.broadcasted_iota(jnp.int32, (tq, tk), 1)
            for b in range(tq // tk):
                for e in range(2):
                    m, acc, s = carry[e]
                    s_next = scores(e, first + b + 1) if b + 1 < tq // tk else None
                    m, acc = softmax_step(e, first + b, m, acc, jnp.where(cols + b * tk <= rows, s, NEG))
                    carry[e] = (m, acc, s_next)
            outs, lses = [], []
            for e in range(2):
                m, acc, _ = carry[e]
                l = acc[:, V_ONES:V_ONES + 1]
                outs.append(acc / l)
                lses.append(m + jnp.log(l))
            o_ref[pl.ds(q0, tq), :] = _pair_lanes(outs[0], outs[1]).astype(BF16)
            lse_ref[pl.ds(q0, tq), :] = jnp.where(lane == 0, lses[0], jnp.where(lane == 1, lses[1], 0.0))

        q_tile(0, tk)

        def big_tile(i, _):
            q_tile(pl.multiple_of(tk + i * tq_big, tk), tq_big)
            return 0

        lax.fori_loop(0, n_big, big_tile, 0)

    pair = pl.BlockSpec((lp, LANES), lambda hp: (0, hp))
    heads = pl.BlockSpec((2, lp, LANES), lambda hp: (hp, 0, 0), pipeline_mode=pl.Buffered(1))
    return pl.pallas_call(
        body, name="attention_forward", grid=(N_HEADS // 2,),
        out_shape=[jax.ShapeDtypeStruct((lp, ATTN_WIDTH), BF16), jax.ShapeDtypeStruct((lp, ATTN_WIDTH), F32)],
        in_specs=[heads, heads, heads],
        out_specs=[pair, pair],
        compiler_params=_params(("arbitrary",)),
    )(q, k, v)


def _attention_backward(q, k, v, do, o, lse):
    lp = q.shape[1]
    tb = ATT_TILE
    nb = lp // tb
    tq_big = ATT_Q_BLOCKS * tb
    n_big = (nb - 1) // ATT_Q_BLOCKS
    assert lp == tb + n_big * tq_big

    def body(q_ref, k_ref, v_ref, do_ref, o_ref, lse_ref,
             dq_ref, dk_ref, dv_ref, dc_ref, q2_ref, do2_ref, dk_acc, dv_acc):
        lane = lax.broadcasted_iota(jnp.int32, (tb, LANES), 1)

        def prepare(bi, _):
            r0 = pl.multiple_of(bi * tb, tb)
            rows = r0 + lax.broadcasted_iota(jnp.int32, (tb, 1), 0)
            dob = do_ref[pl.ds(r0, tb), :].astype(F32)
            dd = dob * o_ref[pl.ds(r0, tb), :].astype(F32)
            for e in range(2):
                in_head = (lane >= HEAD_DIM * e) & (lane < HEAD_DIM * (e + 1))
                delta = jnp.sum(jnp.where(in_head, dd, 0.0), axis=1, keepdims=True)
                do_e = pltpu.roll(dob, HEAD_DIM, axis=1) if e else dob
                do2_ref[e, pl.ds(r0, tb), :] = jnp.where(
                    lane < HEAD_DIM, do_e, _put3(jnp.zeros((tb, LANES), F32), lane, DO_BIAS, -delta)).astype(BF16)
                minus_lse = jnp.where(rows >= PAD, -lse_ref[pl.ds(r0, tb), e:e + 1], NEG)
                q2_ref[e, pl.ds(r0, tb), :] = _put3(q_ref[e, pl.ds(r0, tb), :].astype(F32), lane, Q_LSE,
                                                    minus_lse).astype(BF16)
            return 0

        lax.fori_loop(0, nb, prepare, 0)
        dk_acc[...] = jnp.zeros_like(dk_acc)
        dv_acc[...] = jnp.zeros_like(dv_acc)

        def q_tile(q0, tq):
            first = q0 // tb
            qs = [q2_ref[e, pl.ds(q0, tq), :] for e in range(2)]
            dos = [do2_ref[e, pl.ds(q0, tq), :] for e in range(2)]

            def block(kj):
                return pl.ds(kj * tb if isinstance(kj, int) else pl.multiple_of(kj * tb, tb), tb)

            def products(e, kj):
                return _dot_nt(qs[e], k_ref[e, block(kj), :]), _dot_nt(dos[e], v_ref[e, block(kj), :])

            def grads(e, kj, dq, s, dpd, mask):
                p = jnp.exp(s)
                if mask is not None:
                    p = jnp.where(mask, p, 0.0)
                dsb = (p * dpd).astype(BF16)
                dv_acc[e, :, block(kj)] += _dot_tn(dos[e], p.astype(BF16))
                dk_acc[e, :, block(kj)] += _dot_tn(qs[e], dsb)
                return dq + lax.dot_general(k_ref[e, block(kj), :], dsb, (((0,), (1,)), ((), ())),
                                            preferred_element_type=F32)

            def step(kj, carry):
                new = []
                for e in range(2):
                    dq, s, dpd = carry[e]
                    nxt = products(e, kj + 1)
                    new.append((grads(e, kj, dq, s, dpd, None),) + nxt)
                return tuple(new)

            init = tuple((jnp.zeros((LANES, tq), F32),) + products(e, 0) for e in range(2))
            carry = list(lax.fori_loop(0, first, step, init))
            rows = lax.broadcasted_iota(jnp.int32, (tq, tb), 0)
            cols = lax.broadcasted_iota(jnp.int32, (tq, tb), 1)
            for b in range(tq // tb):
                for e in range(2):
                    dq, s, dpd = carry[e]
                    nxt = products(e, first + b + 1) if b + 1 < tq // tb else (None, None)
                    carry[e] = (grads(e, first + b, dq, s, dpd, cols + b * tb <= rows),) + nxt
            dq0, dq1 = carry[0][0].T, carry[1][0].T
            lane_q = lax.broadcasted_iota(jnp.int32, (tq, LANES), 1)
            dq_ref[pl.ds(q0, tq), :] = (_pair_lanes(dq0, dq1) * 0.125).astype(BF16)
            dc_ref[pl.ds(q0, tq), :] = jnp.where(lane_q == 0, dq0[:, K_ONES:K_ONES + 1],
                                                 jnp.where(lane_q == 1, dq1[:, K_ONES:K_ONES + 1], 0.0))

        q_tile(0, tb)

        def big_tile(i, _):
            q_tile(pl.multiple_of(tb + i * tq_big, tb), tq_big)
            return 0

        lax.fori_loop(0, n_big, big_tile, 0)

        def finish(bi, _):
            r0 = pl.multiple_of(bi * tb, tb)
            dk0, dk1 = dk_acc[0, :, pl.ds(r0, tb)].T, dk_acc[1, :, pl.ds(r0, tb)].T
            dk_ref[pl.ds(r0, tb), :] = _pair_lanes(dk0, dk1).astype(BF16)
            dv_ref[pl.ds(r0, tb), :] = _pair_lanes(dv_acc[0, :, pl.ds(r0, tb)].T, dv_acc[1, :, pl.ds(r0, tb)].T).astype(BF16)
            col_sums = jnp.where(lane == 0, dk0[:, Q_ONES:Q_ONES + 1],
                                 jnp.where(lane == 1, dk1[:, Q_ONES:Q_ONES + 1], 0.0))
            dc_ref[pl.ds(r0, tb), :] = dc_ref[pl.ds(r0, tb), :] - col_sums
            return 0

        lax.fori_loop(0, nb, finish, 0)

    once = pl.Buffered(1)
    pair = pl.BlockSpec((lp, LANES), lambda hp: (0, hp))
    pair_in = pl.BlockSpec((lp, LANES), lambda hp: (0, hp), pipeline_mode=once)
    heads = pl.BlockSpec((2, lp, LANES), lambda hp: (hp, 0, 0), pipeline_mode=once)
    wide = jax.ShapeDtypeStruct((lp, ATTN_WIDTH), BF16)
    return pl.pallas_call(
        body, name="attention_backward", grid=(N_HEADS // 2,),
        out_shape=[wide, wide, wide, jax.ShapeDtypeStruct((lp, ATTN_WIDTH), F32)],
        in_specs=[heads, heads, heads, pair_in, pair_in, pair_in],
        out_specs=[pair, pair, pair, pair],
        scratch_shapes=[pltpu.VMEM((2, lp, LANES), BF16), pltpu.VMEM((2, lp, LANES), BF16),
                        pltpu.VMEM((2, LANES, lp), F32), pltpu.VMEM((2, LANES, lp), F32)],
        compiler_params=_params(("arbitrary",)),
    )(q, k, v, do, o, lse)


def _middle(x, target, h, o, a_pool, u, zp, w_main, w_up_pool, w_up_attn, w_out, pool_w, pool_scale, final_g):
    seq = x.shape[0]
    tm = ROW_TILE
    nt = seq // tm + 1
    lp = nt * tm
    halo_blocks = tm // MAX_WINDOW

    def body(x_ref, t_ref, h_ref, o_ref, ap_ref, u_ref, uh_ref, zp_ref,
             wc_ref, wupp_ref, wupa_ref, wout_ref, pw_ref, sc_ref, gf_ref,
             dh2_ref, mg_ref, yp_ref, ya_ref, dap_ref, daa_ref, do_ref, dza_ref, dgp_ref, dga_ref, dzp_ref, dpn_ref,
             loss_ref, dgf_ref, dsc_ref, dpw_ref):
        i = pl.program_id(0)
        tiles = (dh2_ref, mg_ref, yp_ref, ya_ref, dap_ref, daa_ref, do_ref, dza_ref, dgp_ref, dga_ref, dzp_ref, dpn_ref)

        @pl.when(i == 0)
        def _():
            for ref in tiles + (loss_ref, dgf_ref, dsc_ref, dpw_ref):
                ref[...] = jnp.zeros_like(ref)

        @pl.when(i > 0)
        def _():
            xt = x_ref[...]
            hb = h_ref[...]
            pc = _dot_nt(hb, wc_ref[...])
            za, gp, ga = pc[:, :512], pc[:, 512:1536], pc[:, 1536:]
            of = o_ref[...].astype(F32)
            sza = _sigmoid(za)
            silu_za = za * sza
            ya = (of * silu_za).astype(BF16)
            ya_ref[...] = ya
            aa = _dot(ya, wupa_ref[...])
            ap = ap_ref[...].astype(F32)
            sgp, sga = _sigmoid(gp), _sigmoid(ga)
            mg = (sgp * ap + sga * aa).astype(BF16)
            mg_ref[...] = mg
            h2 = xt + _dot(mg, wout_ref[...])
            r2 = lax.rsqrt(jnp.mean(h2 * h2, axis=-1, keepdims=True) + RMS_EPS)
            h2n = h2 * r2
            gf = gf_ref[...]
            diff = h2n * gf - t_ref[...]
            loss_ref[...] += 0.5 * jnp.sum(jnp.mean(diff * diff, axis=-1, keepdims=True), axis=0, keepdims=True)
            dy = diff * (1.0 / D_MODEL)
            dgf_ref[...] += jnp.sum(dy * h2n, axis=0, keepdims=True)
            dyg = dy * gf
            dh2 = r2 * (dyg - h2n * jnp.mean(dyg * h2n, axis=-1, keepdims=True))
            dh2_ref[...] = dh2
            dmg = _dot_nt(dh2.astype(BF16), wout_ref[...])
            dap = (dmg * sgp).astype(BF16)
            daa = (dmg * sga).astype(BF16)
            dap_ref[...] = dap
            daa_ref[...] = daa
            dgp_ref[...] = (dmg * ap * sgp * (1.0 - sgp)).astype(BF16)
            dga_ref[...] = (dmg * aa * sga * (1.0 - sga)).astype(BF16)
            dyp = _dot_nt(dap, wupp_ref[...])
            dya = _dot_nt(daa, wupa_ref[...])
            do_ref[...] = (dya * silu_za).astype(BF16)
            dza_ref[...] = (dya * of * (sza * (1.0 + za * (1.0 - sza)))).astype(BF16)

            u = u_ref[...]
            zp = zp_ref[...]
            counts = _pool_counts(i * tm, tm)
            ps = _pool_means(jnp.concatenate([uh_ref[...], u], axis=0), u, counts)
            pbs = [p.astype(BF16) for p in ps]
            ppw = jnp.concatenate([_dot(pbs[g], pw_ref[g]) for g in range(4)], axis=1)
            sc = sc_ref[...]
            szp = _sigmoid(zp)
            silu_zp = zp * szp
            ypre = ppw * sc
            yp_ref[...] = (ypre * silu_zp).astype(BF16)
            dypre = dyp * silu_zp
            dzp_ref[...] = (dyp * ypre * (szp * (1.0 + zp * (1.0 - szp)))).astype(BF16)
            dsc_ref[...] += jnp.sum(dypre * ppw, axis=0, keepdims=True)
            dppw = (dypre * sc).astype(BF16)
            dpns = []
            for g in range(4):
                dg = dppw[:, POOL_GROUP * g:POOL_GROUP * (g + 1)]
                dpw_ref[g] += _dot_tn(pbs[g], dg)
                dpns.append(_dot_nt(dg, pw_ref[g]) / counts[g])
            dpn_ref[...] = jnp.concatenate(dpns, axis=1)

    real = lambda w: pl.BlockSpec((tm, w), lambda i: (jnp.maximum(i - 1, 0), 0))
    row = lambda w: pl.BlockSpec((tm, w), lambda i: (i, 0))
    in_specs = [
        real(D_MODEL), real(D_MODEL), row(D_MODEL), row(512), row(D_MODEL), row(512),
        pl.BlockSpec((MAX_WINDOW, 512), lambda i: (jnp.maximum(i * halo_blocks - 1, 0), 0)), row(512),
        _const((2560, D_MODEL), (1, 0)), _const((POOL_WIDTH, D_MODEL)), _const((ATTN_WIDTH, D_MODEL)),
        _const((D_MODEL, D_MODEL)), _const((4, POOL_GROUP, POOL_GROUP)), _const((1, POOL_WIDTH)), _const((1, D_MODEL)),
    ]
    sd = jax.ShapeDtypeStruct
    out_shape = [
        sd((lp, D_MODEL), F32),
        sd((lp, D_MODEL), BF16),
        sd((lp, 512), BF16),
        sd((lp, 512), BF16),
        sd((lp, D_MODEL), BF16),
        sd((lp, D_MODEL), BF16),
        sd((lp, 512), BF16),
        sd((lp, 512), BF16),
        sd((lp, D_MODEL), BF16),
        sd((lp, D_MODEL), BF16),
        sd((lp, 512), BF16),
        sd((lp, 512), F32),
        sd((1, 1), F32),
        sd((1, D_MODEL), F32),
        sd((1, 512), F32),
        sd((4, POOL_GROUP, POOL_GROUP), F32),
    ]
    keep = lambda shape: pl.BlockSpec(shape, lambda i: (0,) * len(shape))
    out_specs = [row(D_MODEL), row(D_MODEL), row(512), row(512), row(D_MODEL), row(D_MODEL), row(512), row(512),
                 row(D_MODEL), row(D_MODEL), row(512), row(512),
                 keep((1, 1)), keep((1, D_MODEL)), keep((1, 512)), keep((4, POOL_GROUP, POOL_GROUP))]
    return pl.pallas_call(
        body, name="middle", grid=(nt,), out_shape=out_shape, in_specs=in_specs, out_specs=out_specs,
        compiler_params=_params(("arbitrary",)),
    )(x, target, h, o, a_pool, u, u, zp, w_main, w_up_pool, w_up_attn, w_out, pool_w, pool_scale, final_g)


def _backward_in(x, tile0, norm_g, dh2, dpn, dzp, dq, dk, dv, dza, dgp, dga, dc, sneg, w_main, w_f):
    seq = x.shape[0]
    tm = ROW_TILE
    nt = seq // tm + 1
    lp = nt * tm
    halo_blocks = tm // MAX_WINDOW
    last_halo = lp // MAX_WINDOW - 1

    def body(x_ref, t0_ref, g_ref, dh2_ref, dpn_ref, dpnh_ref, dzp_ref, dq_ref, dk_ref, dv_ref, dza_ref,
             dgp_ref, dga_ref, dc_ref, sn_ref, wm_ref, wf_ref,
             dproj_ref, df_ref, gx_ref, gmeta_ref, dg_ref, dbf_ref, carry_ref):
        i = pl.program_id(0)
        t = nt - 1 - i

        @pl.when(i == 0)
        def _():
            carry_ref[...] = jnp.zeros_like(carry_ref)
            dg_ref[...] = jnp.zeros_like(dg_ref)
            dbf_ref[...] = jnp.zeros_like(dbf_ref)

        dpn_t = dpn_ref[...]
        ahead = jnp.where(i == 0, jnp.zeros_like(dpnh_ref), dpnh_ref[...])
        ext = jnp.concatenate([dpn_t, ahead], axis=0)
        counts = _pool_counts(t * tm, tm)
        for g, w in enumerate(POOL_WINDOWS):
            s = ext[:, POOL_GROUP * g:POOL_GROUP * (g + 1)]
            sh = 1
            while sh < w:
                s = s + pltpu.roll(s, tm + MAX_WINDOW - sh, axis=0)
                sh *= 2
            du = s[:tm, :] - dpn_t[:, POOL_GROUP * g:POOL_GROUP * (g + 1)] * counts[g]
            dproj_ref[:, POOL_GROUP * g:POOL_GROUP * (g + 1)] = du.astype(BF16)
        dproj_ref[:, 512:1024] = dzp_ref[...]
        dproj_ref[:, 1024:1536] = dq_ref[...]
        dproj_ref[:, 1536:2048] = dk_ref[...]
        dproj_ref[:, 2048:2560] = dv_ref[...]
        dproj_ref[:, 2560:3072] = dza_ref[...]
        dproj_ref[:, 3072:4096] = dgp_ref[...]
        dproj_ref[:, 4096:5120] = dga_ref[...]

        dct = dc_ref[:, 0:LANES]
        for hp in range(1, N_HEADS // 2):
            dct = dct + pltpu.roll(dc_ref[:, LANES * hp:LANES * (hp + 1)], 2 * hp, axis=1)
        rloc = lax.broadcasted_iota(jnp.int32, (tm, LANES), 0)
        sh = 1
        while sh < tm:
            dct = dct + jnp.where(rloc + sh < tm, pltpu.roll(dct, tm - sh, axis=0), 0.0)
            sh *= 2
        dct = dct + carry_ref[...]
        carry_ref[...] = dct[0:1, :]
        df = dct * sn_ref[...]
        dbf_ref[...] += jnp.sum(df, axis=0, keepdims=True)
        dfb = df.astype(BF16)
        df_ref[...] = dfb

        dh = _dot(dproj_ref[...], wm_ref[...]) + _dot(dfb, wf_ref[...])
        xt = jnp.where(t == 0, t0_ref[...], x_ref[...])
        r = lax.rsqrt(jnp.mean(xt * xt, axis=-1, keepdims=True) + RMS_EPS)
        xn = xt * r
        dg_ref[...] += jnp.sum(dh * xn, axis=0, keepdims=True)
        dhg = dh * g_ref[...]
        dx = dh2_ref[...] + r * (dhg - xn * jnp.mean(dhg * xn, axis=-1, keepdims=True))

        @pl.when(t > 0)
        def _():
            gx_ref[...] = dx

        @pl.when(t == 0)
        def _():
            gmeta_ref[...] = dx[PAD:, :]

    rev = lambda w: pl.BlockSpec((tm, w), lambda i: (nt - 1 - i, 0))
    real = pl.BlockSpec((tm, D_MODEL), lambda i: (jnp.maximum(nt - 2 - i, 0), 0))
    in_specs = [
        real, _const((tm, D_MODEL)), _const((1, D_MODEL)), rev(D_MODEL), rev(512),
        pl.BlockSpec((MAX_WINDOW, 512), lambda i: (jnp.minimum((nt - i) * halo_blocks, last_halo), 0)),
        rev(512), rev(512), rev(512), rev(512), rev(512), rev(D_MODEL), rev(D_MODEL),
        rev(512), rev(LANES),
        _const((N_MAIN, D_MODEL)), _const((LANES, D_MODEL)),
    ]
    sd = jax.ShapeDtypeStruct
    out_shape = [sd((lp, N_MAIN), BF16), sd((lp, LANES), BF16), sd((seq, D_MODEL), F32), sd((N_META, D_MODEL), F32),
                 sd((1, D_MODEL), F32), sd((1, LANES), F32)]
    keep = lambda shape: pl.BlockSpec(shape, lambda i: (0,) * len(shape))
    out_specs = [rev(N_MAIN), rev(LANES), real, keep((N_META, D_MODEL)), keep((1, D_MODEL)), keep((1, LANES))]
    return pl.pallas_call(
        body, name="backward_in", grid=(nt,), out_shape=out_shape, in_specs=in_specs, out_specs=out_specs,
        scratch_shapes=[pltpu.VMEM((1, LANES), F32)],
        compiler_params=_params(("arbitrary",)),
    )(x, tile0, norm_g, dh2, dpn, dpn, dzp, dq, dk, dv, dza, dgp, dga, dc, sneg, w_main, w_f)


def _matmul_tn(name, a, b, tn):
    lp, m = a.shape
    n = b.shape[1]

    def body(a_ref, b_ref, c_ref):
        c_ref[...] = _dot_tn(a_ref[...].astype(BF16), b_ref[...].astype(BF16))

    return pl.pallas_call(
        body, name=name, grid=(n // tn,), out_shape=jax.ShapeDtypeStruct((m, n), F32),
        in_specs=[_const((lp, m)), pl.BlockSpec((lp, tn), lambda j: (0, j))],
        out_specs=pl.BlockSpec((m, tn), lambda j: (0, j)),
        compiler_params=_params(("arbitrary",)),
    )(a, b)


def _matmul_tn_rows(name, a, b, tm):
    lp, m = a.shape
    n = b.shape[1]

    def body(a_ref, b_ref, c_ref):
        c_ref[...] = _dot_tn(a_ref[...].astype(BF16), b_ref[...].astype(BF16))

    return pl.pallas_call(
        body, name=name, grid=(m // tm,), out_shape=jax.ShapeDtypeStruct((m, n), F32),
        in_specs=[pl.BlockSpec((lp, tm), lambda j: (0, j)), _const((lp, n))],
        out_specs=pl.BlockSpec((tm, n), lambda j: (j, 0)),
        compiler_params=_params(("arbitrary",)),
    )(a, b)


def _adamw(name, parts, w, m, v, rows, cols=None):
    r, c_all = w.shape
    c = cols or c_all
    n_parts = parts.shape[0]

    def body(p_ref, w_ref, m_ref, v_ref, g_ref, d_ref, mo_ref, vo_ref):
        g = p_ref[0].astype(F32)
        for s in range(1, n_parts):
            g = g + p_ref[s].astype(F32)
        m_new = ADAM_B1 * m_ref[...] + (1.0 - ADAM_B1) * g
        v_new = ADAM_B2 * v_ref[...] + (1.0 - ADAM_B2) * (g * g)
        m_hat = m_new / (1.0 - ADAM_B1 ** ADAM_STEP)
        v_hat = v_new / (1.0 - ADAM_B2 ** ADAM_STEP)
        g_ref[...] = g
        d_ref[...] = -ADAM_LR * (m_hat / (jnp.sqrt(v_hat) + ADAM_EPS) + ADAM_WD * w_ref[...])
        mo_ref[...] = m_new
        vo_ref[...] = v_new

    blk = pl.BlockSpec((rows, c), lambda i, j: (i, j))
    return pl.pallas_call(
        body, name=name, grid=(r // rows, c_all // c), out_shape=[jax.ShapeDtypeStruct((r, c_all), F32)] * 4,
        in_specs=[pl.BlockSpec((n_parts, rows, c), lambda i, j: (0, i, j)), blk, blk, blk],
        out_specs=[blk] * 4,
        compiler_params=_params(("arbitrary", "arbitrary")),
    )(parts, w, m, v)


def _pair_sum(name, mine, theirs, rows):
    n, r, c = mine.shape

    def body(a_ref, b_ref, o_ref):
        o_ref[...] = (a_ref[...] + b_ref[...].astype(F32)).astype(BF16)

    blk = pl.BlockSpec((1, rows, c), lambda j, i: (j, i, 0))
    return pl.pallas_call(
        body, name=name, grid=(n, r // rows), out_shape=jax.ShapeDtypeStruct((n, r, c), BF16),
        in_specs=[blk, blk], out_specs=blk,
        compiler_params=_params(("arbitrary", "arbitrary")),
    )(mine, theirs)


def _columns_to_slots(a):
    r, c8 = a.shape
    return a.reshape(r, N_DEV, c8 // N_DEV).transpose(1, 0, 2)


def _by_core(slots):
    by_core = slots.reshape((4, 2) + slots.shape[1:]).swapaxes(0, 1)
    c = lax.axis_index("c")
    return (lax.dynamic_index_in_dim(by_core, c, 0, keepdims=False),
            lax.dynamic_index_in_dim(by_core, 1 - c, 0, keepdims=False).astype(BF16))


def _slots_to_columns(a):
    n, r, c = a.shape
    return a.transpose(1, 0, 2).reshape(r, n * c)


def kernel(x, meta_tokens, norm_g, w_in, b_forget, pool_w, pool_scale, w_up_pool, w_up_attn, w_out, final_norm_g, loss_target, m_meta_tokens, m_norm_g, m_w_in, m_b_forget, m_pool_w, m_pool_scale, m_w_up_pool, m_w_up_attn, m_w_out, m_final_norm_g, v_meta_tokens, v_norm_g, v_w_in, v_b_forget, v_pool_w, v_pool_scale, v_w_up_pool, v_w_up_attn, v_w_out, v_final_norm_g):
    xs = x[0]
    target = loss_target[0]

    g_in, g_upp, g_upa, g_out, g_meta = _gather_two_level(
        "gather_weights",
        [w_in[0].T.astype(BF16), w_up_pool[0].astype(BF16), w_up_attn[0].astype(BF16), w_out[0].astype(BF16),
         meta_tokens])
    w_full = g_in.reshape(N_DEV * g_in.shape[1], D_MODEL)
    w_main = jnp.concatenate([w_full[:N_BEFORE_F], w_full[N_BEFORE_F + N_HEADS:]], axis=0)
    w_f = jnp.pad(w_full[N_BEFORE_F:N_BEFORE_F + N_HEADS], ((0, LANES - N_HEADS), (0, 0)))
    wupp = _slots_to_columns(g_upp)
    wupa = _slots_to_columns(g_upa)
    wout = g_out.reshape(D_MODEL, D_MODEL)
    meta = _slots_to_columns(g_meta)
    tile0 = jnp.concatenate([jnp.zeros((PAD, D_MODEL), F32), meta], axis=0)
    b_f = jnp.pad(b_forget, ((0, 0), (0, LANES - N_HEADS)))
    pw_b = pool_w[0].astype(BF16)
    final_g = final_norm_g.reshape(1, D_MODEL)

    h, u, zp, q, k, v, sneg, a_pool = _forward_in(xs, tile0, norm_g, w_main, w_f, b_f, pw_b, pool_scale, wupp)
    o, lse = _attention_forward(q, k, v)
    (dh2, mg, yp, ya, dap, daa, do, dza, dgp, dga, dzp, dpn,
     loss_part, d_final_g, d_scale, d_pool_w) = _middle(xs, target, h, o, a_pool, u, zp, w_main, wupp, wupa, wout,
                                                        pw_b, pool_scale, final_g)
    dq, dk, dv, dc = _attention_backward(q, k, v, do, o, lse)
    dproj, df, grad_x, d_meta, d_norm_g, d_bf = _backward_in(xs, tile0, norm_g, dh2, dpn, dzp, dq, dk, dv, dza,
                                                             dgp, dga, dc, sneg, w_main, w_f)
    dw_main = _matmul_tn_rows("grad_w_in", dproj, h, 512)
    dw_f = _matmul_tn_rows("grad_w_forget", df, h, LANES)
    dw_out = _matmul_tn("grad_w_out", mg, dh2, 256)
    dw_upp = _matmul_tn("grad_w_up_pool", yp, dap, 512)
    dw_upa = _matmul_tn("grad_w_up_attn", ya, daa, 512)
    dw_in = jnp.concatenate([dw_main[:N_BEFORE_F], dw_f[:N_HEADS], dw_main[N_BEFORE_F:]], axis=0)
    dw_in = dw_in.reshape(N_DEV, dw_in.shape[0] // N_DEV, D_MODEL)

    big = [dw_in, _columns_to_slots(dw_upp), _columns_to_slots(dw_upa),
           dw_out.reshape(N_DEV, D_MODEL // N_DEV, D_MODEL)]
    mine, for_sibling = zip(*[_by_core(s) for s in big])
    from_sibling = _exchange("swap_with_sibling", [("swap", a, (SIBLING,)) for a in for_sibling])
    chip_rows = (dw_in.shape[1], 512, 512, 128)
    sums = [_pair_sum(f"pair_sum_{j}", mine[j], from_sibling[j], chip_rows[j]) for j in range(4)]
    (p_in, p_upp, p_upa, p_out, p_meta, p_norm_g, p_bf, p_pool_w, p_scale, p_final_g) = _exchange(
        "exchange_gradients",
        [("chips", s, SAME_CORE) for s in sums]
        + [("scatter", _columns_to_slots(d_meta), ALL_PEERS)]
        + [("gather", a, ALL_PEERS) for a in
           (d_norm_g, d_bf, d_pool_w.reshape(4 * POOL_GROUP, POOL_GROUP), d_scale, d_final_g)])

    loss = lax.psum(loss_part[0, 0], ("x", "y", "c"))

    def pad_f(a):
        return jnp.pad(a, ((0, 0), (0, LANES - N_HEADS)))

    res = {}
    res["meta_tokens"] = _adamw("adamw_meta", p_meta, meta_tokens, m_meta_tokens, v_meta_tokens, N_META)
    res["norm_g"] = _adamw("adamw_norm_g", p_norm_g, norm_g, m_norm_g, v_norm_g, 1)
    res["w_in"] = [a.T for a in _adamw("adamw_w_in", p_in, w_in[0].T, m_w_in[0].T, v_w_in[0].T, p_in.shape[1], 256)]
    bf = _adamw("adamw_b_forget", p_bf, pad_f(b_forget), pad_f(m_b_forget), pad_f(v_b_forget), 1)
    res["b_forget"] = [a[:, :N_HEADS] for a in bf]
    pw = _adamw("adamw_pool_w", p_pool_w, pool_w.reshape(512, 128), m_pool_w.reshape(512, 128),
                v_pool_w.reshape(512, 128), 512)
    res["pool_w"] = [a.reshape(pool_w.shape) for a in pw]
    res["pool_scale"] = _adamw("adamw_pool_scale", p_scale, pool_scale, m_pool_scale, v_pool_scale, 1)
    res["w_up_pool"] = _adamw("adamw_w_up_pool", p_upp, w_up_pool[0], m_w_up_pool[0], v_w_up_pool[0], 512)
    res["w_up_attn"] = _adamw("adamw_w_up_attn", p_upa, w_up_attn[0], m_w_up_attn[0], v_w_up_attn[0], 512)
    res["w_out"] = _adamw("adamw_w_out", p_out, w_out[0], m_w_out[0], v_w_out[0], 128)
    fg = _adamw("adamw_final_norm_g", p_final_g, final_g, m_final_norm_g.reshape(1, D_MODEL),
                v_final_norm_g.reshape(1, D_MODEL), 1)
    res["final_norm_g"] = [a.reshape(D_MODEL) for a in fg]
    for name in ("w_in", "w_up_pool", "w_up_attn", "w_out"):
        res[name] = [a[None] for a in res[name]]

    order = ["meta_tokens", "norm_g", "w_in", "b_forget", "pool_w", "pool_scale", "w_up_pool", "w_up_attn", "w_out",
             "final_norm_g"]
    outs = [loss, grad_x[None]]
    for part in range(4):
        outs += [res[name][part] for name in order]
    return tuple(outs)
```

```python
import functools

import jax
import jax.numpy as jnp
from jax import lax
from jax.experimental import pallas as pl
from jax.experimental.pallas import tpu as pltpu

F32 = jnp.float32
BF16 = jnp.bfloat16

D_MODEL = 1024
N_META = 16
POOL_WIDTH = 512
ATTN_WIDTH = 512
N_HEADS = 8
HEAD_DIM = 64
POOL_WINDOWS = (2, 4, 8, 16)
POOL_GROUP = 128
MAX_WINDOW = 16
RMS_EPS = 1e-6
N_MAIN = 5120
N_BEFORE_F = 3072
N_DEV = 8
LANES = 128

ROW_TILE = 256
ATT_TILE = 256
ATT_Q_BLOCKS = 2
PAD = ROW_TILE - N_META
VMEM_LIMIT = 56 * 1024 * 1024

ADAM_LR = 0.001
ADAM_B1 = 0.9
ADAM_B2 = 0.999
ADAM_EPS = 1e-08
ADAM_WD = 0.01
ADAM_STEP = 10

NEG = -1e30
MESH = pl.DeviceIdType.MESH


def _params(sem=None):
    kw = dict(vmem_limit_bytes=VMEM_LIMIT)
    if sem is not None:
        kw["dimension_semantics"] = sem
    return pltpu.CompilerParams(**kw)


def _const(shape, block_index=None):
    idx = block_index or (0,) * len(shape)
    return pl.BlockSpec(shape, lambda i: idx, pipeline_mode=pl.Buffered(1))


def _sigmoid(x):
    return jax.nn.sigmoid(x)


def _dot(a, b):
    return jnp.dot(a, b, preferred_element_type=F32)


def _dot_nt(a, b):
    return lax.dot_general(a, b, (((1,), (1,)), ((), ())), preferred_element_type=F32)


def _dot_tn(a, b):
    return lax.dot_general(a, b, (((0,), (0,)), ((), ())), preferred_element_type=F32)


def _pool_counts(first_row, rows):
    row = first_row + lax.broadcasted_iota(jnp.int32, (rows, 1), 0)
    pos1 = row - PAD + 1
    return [jnp.clip(pos1, 1, w).astype(F32) for w in POOL_WINDOWS]


def _pool_means(u_ext, u, counts):
    rows = u.shape[0]
    out = []
    for g, w in enumerate(POOL_WINDOWS):
        s = u_ext[:, POOL_GROUP * g:POOL_GROUP * (g + 1)]
        sh = 1
        while sh < w:
            s = s + pltpu.roll(s, sh, axis=0)
            sh *= 2
        out.append(s[MAX_WINDOW:MAX_WINDOW + rows, :] / counts[g] - u[:, POOL_GROUP * g:POOL_GROUP * (g + 1)])
    return out


Q_BIAS, Q_ONES, Q_LSE = 64, 67, 70
K_ONES, K_BIAS, K_ONES2 = 64, 67, 70
V_ONES = 64
DO_BIAS = 64


def _lane_ones(lane, ranges):
    hit = None
    for lo, hi in ranges:
        r = (lane >= lo) & (lane < hi)
        hit = r if hit is None else hit | r
    return jnp.where(hit, 1.0, 0.0)


def _put3(base, lane, first, x):
    hi = x.astype(BF16).astype(F32)
    rest = x - hi
    mid = rest.astype(BF16).astype(F32)
    lo = (rest - mid).astype(BF16).astype(F32)
    for j, piece in enumerate((hi, mid, lo)):
        base = jnp.where(lane == first + j, piece, base)
    return base


SIBLING = 1
SAME_CORE = (2, 4, 6)
ALL_PEERS = (1, 2, 3, 4, 5, 6, 7)


def _place():
    return lax.axis_index("x"), lax.axis_index("y"), lax.axis_index("c")


def _peer(r):
    x, y, c = _place()
    return (1 - x if r & 4 else x, 1 - y if r & 2 else y, 1 - c if r & 1 else c)


def _device_slot(p):
    return 4 * p[0] + 2 * p[1] + p[2]


def _chip_slot(p):
    return 2 * p[0] + p[1]


def _exchange(name, items):
    n = len(items)

    def body(*refs):
        ins, outs = refs[:n], refs[n:2 * n]
        send_sems, recv_sems, local_sems = refs[2 * n:]
        me = _place()
        copies = []
        for a, (kind, _, peers) in enumerate(items):
            slot = _chip_slot if kind == "chips" else _device_slot
            for r in peers:
                peer = _peer(r)
                src = ins[a] if kind in ("swap", "gather") else ins[a].at[slot(peer)]
                dst = outs[a] if kind == "swap" else outs[a].at[slot(me)]
                cp = pltpu.make_async_remote_copy(
                    src_ref=src, dst_ref=dst, send_sem=send_sems.at[a, r - 1], recv_sem=recv_sems.at[a, r - 1],
                    device_id=peer, device_id_type=MESH)
                cp.start()
                copies.append(cp)
            if kind != "swap":
                src = ins[a] if kind == "gather" else ins[a].at[slot(me)]
                mine = pltpu.make_async_copy(src, outs[a].at[slot(me)], local_sems.at[a])
                mine.start()
                copies.append(mine)
        for cp in copies:
            cp.wait()

    hbm = pl.BlockSpec(memory_space=pl.ANY)
    out_shape = [jax.ShapeDtypeStruct(((N_DEV,) if kind == "gather" else ()) + a.shape, a.dtype)
                 for kind, a, _ in items]
    return pl.pallas_call(
        body, name=name, out_shape=out_shape,
        in_specs=[hbm] * n, out_specs=[hbm] * n,
        scratch_shapes=[pltpu.SemaphoreType.DMA((n, N_DEV - 1)), pltpu.SemaphoreType.DMA((n, N_DEV - 1)),
                        pltpu.SemaphoreType.DMA((n,))],
    )(*[a for _, a, _ in items])


def _gather_two_level(name, arrays):
    n = len(arrays)

    def body(*refs):
        ins, outs = refs[:n], refs[n:2 * n]
        send_sems, recv_sems, local_sems = refs[2 * n:]
        me = _place()
        sibling = _peer(SIBLING)

        def copy(a, k, block, to, src=None):
            rows = outs[a].at[_device_slot(block)]
            return pltpu.make_async_remote_copy(
                src_ref=rows if src is None else src, dst_ref=rows,
                send_sem=send_sems.at[a, k], recv_sem=recv_sems.at[a, k], device_id=to, device_id_type=MESH)

        sends, own = [], []
        for a in range(n):
            mine = pltpu.make_async_copy(ins[a], outs[a].at[_device_slot(me)], local_sems.at[a])
            mine.start()
            own.append(mine)
            for k, r in enumerate((SIBLING,) + SAME_CORE):
                cp = copy(a, k, me, _peer(r), src=ins[a])
                cp.start()
                sends.append(cp)
        for a in range(n):
            for j, r in enumerate(SAME_CORE):
                copy(a, 1 + j, _peer(r), me).wait_recv()
                passed = copy(a, 4 + j, _peer(r), sibling)
                passed.start()
                sends.append(passed)
        for a in range(n):
            copy(a, 0, sibling, me).wait_recv()
            for j, r in enumerate(SAME_CORE):
                copy(a, 4 + j, _peer(r | SIBLING), me).wait_recv()
        for cp in sends:
            cp.wait_send()
        for cp in own:
            cp.wait()

    hbm = pl.BlockSpec(memory_space=pl.ANY)
    return pl.pallas_call(
        body, name=name, out_shape=[jax.ShapeDtypeStruct((N_DEV,) + a.shape, a.dtype) for a in arrays],
        in_specs=[hbm] * n, out_specs=[hbm] * n,
        scratch_shapes=[pltpu.SemaphoreType.DMA((n, N_DEV - 1)), pltpu.SemaphoreType.DMA((n, N_DEV - 1)),
                        pltpu.SemaphoreType.DMA((n,))],
    )(*arrays)


def _forward_in(x, tile0, norm_g, w_main, w_f, b_f, pool_w, pool_scale, w_up_pool):
    seq = x.shape[0]
    nt = seq // ROW_TILE + 1
    lp = nt * ROW_TILE
    tm = ROW_TILE

    def body(x_ref, t0_ref, g_ref, wa_ref, wf_ref, bf_ref, pw_ref, sc_ref, wup_ref,
             h_ref, u_ref, zp_ref, k_ref, v_ref, qt_ref, kt_ref, vt_ref, sn_ref, ap_ref,
             uext_ref, carry_ref):
        i = pl.program_id(0)

        @pl.when(i == 0)
        def _():
            uext_ref[...] = jnp.zeros_like(uext_ref)
            carry_ref[...] = jnp.zeros_like(carry_ref)

        xt = jnp.where(i == 0, t0_ref[...], x_ref[...])
        r = lax.rsqrt(jnp.mean(xt * xt, axis=-1, keepdims=True) + RMS_EPS)
        h = (xt * r * g_ref[...]).astype(BF16)
        h_ref[...] = h
        pa = _dot_nt(h, wa_ref[...])
        u = pa[:, :512]
        zp = pa[:, 512:1024]
        u_ref[...] = u
        zp_ref[...] = zp

        uext_ref[0:MAX_WINDOW, :] = uext_ref[tm:tm + MAX_WINDOW, :]
        uext_ref[MAX_WINDOW:, :] = u
        counts = _pool_counts(i * tm, tm)
        ps = _pool_means(uext_ref[...], u, counts)
        ppw = jnp.concatenate([_dot(ps[g].astype(BF16), pw_ref[g]) for g in range(4)], axis=1)
        y_pool = ppw * sc_ref[...] * (zp * _sigmoid(zp))
        ap_ref[...] = _dot(y_pool.astype(BF16), wup_ref[...]).astype(BF16)

        fl = _dot_nt(h, wf_ref[...]) + bf_ref[...]
        row = i * tm + lax.broadcasted_iota(jnp.int32, (tm, LANES), 0)
        rloc = lax.broadcasted_iota(jnp.int32, (tm, LANES), 0)
        lane = lax.broadcasted_iota(jnp.int32, (tm, LANES), 1)
        live = (row >= PAD) & (lane < N_HEADS)
        logf = jnp.minimum(fl, 0.0) - jnp.log1p(jnp.exp(-jnp.abs(fl)))
        cs = jnp.where(live, logf, 0.0)
        sh = 1
        while sh < tm:
            cs = cs + jnp.where(rloc >= sh, pltpu.roll(cs, sh, axis=0), 0.0)
            sh *= 2
        cs = cs + carry_ref[...]
        carry_ref[...] = cs[tm - 1:tm, :]
        sn_ref[...] = jnp.where(live, _sigmoid(-fl), 0.0)

        rows1 = i * tm + lax.broadcasted_iota(jnp.int32, (tm, 1), 0)
        ones_q = _lane_ones(lane, ((Q_ONES, Q_ONES + 3),))
        ones_k = _lane_ones(lane, ((K_ONES, K_ONES + 3), (K_ONES2, K_ONES2 + 3)))
        ones_v = _lane_ones(lane, ((V_ONES, V_ONES + 3),))
        for hp in range(N_HEADS // 2):
            qp = pa[:, 1024 + LANES * hp:1024 + LANES * (hp + 1)] * 0.125
            kp = pa[:, 1536 + LANES * hp:1536 + LANES * (hp + 1)]
            vp = pa[:, 2048 + LANES * hp:2048 + LANES * (hp + 1)]
            for e in range(2):
                head = 2 * hp + e
                if e:
                    qp, kp, vp = (pltpu.roll(a, HEAD_DIM, axis=1) for a in (qp, kp, vp))
                c_h = cs[:, head:head + 1]
                q_h = jnp.where(lane < HEAD_DIM, qp, _put3(ones_q, lane, Q_BIAS, c_h))
                qt_ref[head] = q_h.T.astype(BF16)
                minus_ck = jnp.where(rows1 >= PAD, -c_h, NEG)
                k_h = jnp.where(lane < HEAD_DIM, kp, _put3(ones_k, lane, K_BIAS, minus_ck))
                k_ref[head] = k_h.astype(BF16)
                kt_ref[head] = k_h.T.astype(BF16)
                v_h = jnp.where(lane < HEAD_DIM, vp, ones_v)
                v_ref[head] = v_h.astype(BF16)
                vt_ref[head] = v_h.T.astype(BF16)

    row_f32 = lambda w: pl.BlockSpec((tm, w), lambda i: (i, 0))
    out_shape = [
        jax.ShapeDtypeStruct((lp, D_MODEL), BF16),
        jax.ShapeDtypeStruct((lp, POOL_WIDTH), F32),
        jax.ShapeDtypeStruct((lp, POOL_WIDTH), F32),
        jax.ShapeDtypeStruct((N_HEADS, lp, LANES), BF16),
        jax.ShapeDtypeStruct((N_HEADS, lp, LANES), BF16),
        jax.ShapeDtypeStruct((N_HEADS, LANES, lp), BF16),
        jax.ShapeDtypeStruct((N_HEADS, LANES, lp), BF16),
        jax.ShapeDtypeStruct((N_HEADS, LANES, lp), BF16),
        jax.ShapeDtypeStruct((lp, LANES), F32),
        jax.ShapeDtypeStruct((lp, D_MODEL), BF16),
    ]
    heads = pl.BlockSpec((N_HEADS, tm, LANES), lambda i: (0, i, 0))
    heads_t = pl.BlockSpec((N_HEADS, LANES, tm), lambda i: (0, 0, i))
    out_specs = [row_f32(D_MODEL), row_f32(512), row_f32(512), heads, heads, heads_t, heads_t, heads_t,
                 row_f32(LANES), row_f32(D_MODEL)]
    in_specs = [
        pl.BlockSpec((tm, D_MODEL), lambda i: (jnp.maximum(i - 1, 0), 0)),
        _const((tm, D_MODEL)), _const((1, D_MODEL)),
        _const((2560, D_MODEL)), _const((LANES, D_MODEL)), _const((1, LANES)),
        _const((4, POOL_GROUP, POOL_GROUP)), _const((1, POOL_WIDTH)), _const((POOL_WIDTH, D_MODEL)),
    ]
    return pl.pallas_call(
        body, name="forward_in", grid=(nt,), out_shape=out_shape, in_specs=in_specs, out_specs=out_specs,
        scratch_shapes=[pltpu.VMEM((tm + MAX_WINDOW, POOL_WIDTH), F32), pltpu.VMEM((1, LANES), F32)],
        compiler_params=_params(("arbitrary",)),
    )(x, tile0, norm_g, w_main, w_f, b_f, pool_w, pool_scale, w_up_pool)


def _causal(tb):
    return lax.broadcasted_iota(jnp.int32, (tb, tb), 1) <= lax.broadcasted_iota(jnp.int32, (tb, tb), 0)


def _pair_lanes(a0, a1):
    lane = lax.broadcasted_iota(jnp.int32, a0.shape, 1)
    return jnp.where(lane < HEAD_DIM, a0, pltpu.roll(a1, HEAD_DIM, axis=1))


def _attention_forward(qt, k, vt):
    lp = k.shape[1]
    tk = ATT_TILE
    tq_big = ATT_Q_BLOCKS * tk
    n_big = (lp // tk - 1) // ATT_Q_BLOCKS
    assert lp == tk + n_big * tq_big and ATT_Q_BLOCKS == 2

    def body(qt_ref, k_ref, vt_ref, o_ref, lse_ref, s_buf, m_scr, acc_scr):
        def q_tile(q0, tq, pairs):
            first = q0 // tk
            qts = [qt_ref[e, :, pl.ds(q0, tq)] for e in range(2)]

            def block(kj):
                return pl.ds(kj * tk if isinstance(kj, int) else pl.multiple_of(kj * tk, tk), tk)

            def step(kj, rd, wr, mask=None):
                for e in range(2):
                    s = s_buf[rd, e, :, 0:tq]
                    if wr is not None:
                        s_buf[wr, e, :, 0:tq] = _dot(k_ref[e, block(kj + 1), :], qts[e])
                    if mask is not None:
                        s = jnp.where(mask, s, NEG)
                    m = m_scr[e, :, 0:tq]
                    m_new = jnp.maximum(m, jnp.max(s, axis=0, keepdims=True))
                    p = jnp.exp(s - m_new)
                    pv = _dot(vt_ref[e, :, block(kj)], p.astype(BF16))
                    acc_scr[e, :, 0:tq] = jnp.exp(m - m_new) * acc_scr[e, :, 0:tq] + pv
                    m_scr[e, :, 0:tq] = m_new

            keys = lax.broadcasted_iota(jnp.int32, (tk, tq), 0)
            queries = lax.broadcasted_iota(jnp.int32, (tk, tq), 1)
            for e in range(2):
                m_scr[e, :, 0:tq] = jnp.full((1, tq), NEG, F32)
                acc_scr[e, :, 0:tq] = jnp.zeros((LANES, tq), F32)
                s_buf[0, e, :, 0:tq] = _dot(k_ref[e, block(0), :], qts[e])
            if pairs is None:
                step(0, 0, None, keys <= queries)
            else:
                step(0, 0, 1)

                def two_steps(t, _):
                    step(1 + 2 * t, 1, 0)
                    step(2 + 2 * t, 0, 1)
                    return 0

                lax.fori_loop(0, pairs, two_steps, 0)
                step(first, 1, 0, keys <= queries)
                step(first + 1, 0, None, keys + tk <= queries)
            outs, lses = [], []
            for e in range(2):
                acc = acc_scr[e, :, 0:tq]
                l = acc[V_ONES:V_ONES + 1, :]
                outs.append((acc / l).T)
                lses.append(m_scr[e, :, 0:tq] + jnp.log(l))
            o_ref[pl.ds(q0, tq), :] = _pair_lanes(outs[0], outs[1]).astype(BF16)
            lse_rows = jnp.concatenate(lses + [jnp.zeros((LANES - 2, tq), F32)], axis=0)
            lse_ref[pl.ds(q0, tq), :] = lse_rows.T

        q_tile(0, tk, None)

        def big_tile(i, _):
            q_tile(pl.multiple_of(tk + i * tq_big, tk), tq_big, i)
            return 0

        lax.fori_loop(0, n_big, big_tile, 0)

    pair = pl.BlockSpec((lp, LANES), lambda hp: (0, hp))
    heads = pl.BlockSpec((2, lp, LANES), lambda hp: (hp, 0, 0), pipeline_mode=pl.Buffered(1))
    heads_t = pl.BlockSpec((2, LANES, lp), lambda hp: (hp, 0, 0), pipeline_mode=pl.Buffered(1))
    return pl.pallas_call(
        body, name="attention_forward", grid=(N_HEADS // 2,),
        out_shape=[jax.ShapeDtypeStruct((lp, ATTN_WIDTH), BF16), jax.ShapeDtypeStruct((lp, ATTN_WIDTH), F32)],
        in_specs=[heads_t, heads, heads_t],
        out_specs=[pair, pair],
        scratch_shapes=[pltpu.VMEM((2, 2, tk, tq_big), F32), pltpu.VMEM((2, 1, tq_big), F32),
                        pltpu.VMEM((2, LANES, tq_big), F32)],
        compiler_params=_params(("arbitrary",)),
    )(qt, k, vt)


def _rows3(first, x):
    sub = lax.broadcasted_iota(jnp.int32, (LANES, x.shape[1]), 0)
    hi = x.astype(BF16).astype(F32)
    rest = x - hi
    mid = rest.astype(BF16).astype(F32)
    lo = (rest - mid).astype(BF16).astype(F32)
    out = jnp.zeros((LANES, x.shape[1]), F32)
    for j, piece in enumerate((hi, mid, lo)):
        out = jnp.where(sub == first + j, piece, out)
    return out


def _attention_backward(qt, k, kt, v, do, o, lse):
    lp = k.shape[1]
    tb = ATT_TILE
    nb = lp // tb
    tq_big = ATT_Q_BLOCKS * tb
    n_big = (nb - 1) // ATT_Q_BLOCKS
    assert lp == tb + n_big * tq_big and ATT_Q_BLOCKS == 2

    def body(qt_ref, k_ref, kt_ref, v_ref, do_ref, o_ref, lse_ref,
             dq_ref, dk_ref, dv_ref, dc_ref, q2_ref, do2_ref, dk_acc, dv_acc, dq_scr, s_buf):
        sub = lax.broadcasted_iota(jnp.int32, (LANES, tb), 0)

        def lanes01(row0, row1):
            n = row0.shape[1]
            return jnp.concatenate([row0, row1, jnp.zeros((LANES - 2, n), F32)], axis=0).T

        def prepare(bi, _):
            r0 = pl.multiple_of(bi * tb, tb)
            queries = r0 + lax.broadcasted_iota(jnp.int32, (1, tb), 1)
            dob = do_ref[pl.ds(r0, tb), :].astype(F32)
            do_t = dob.T
            dd_t = (dob * o_ref[pl.ds(r0, tb), :].astype(F32)).T
            lse_t = lse_ref[pl.ds(r0, tb), :].T
            for e in range(2):
                delta = jnp.sum(dd_t[HEAD_DIM * e:HEAD_DIM * (e + 1), :], axis=0, keepdims=True)
                do_e = jnp.concatenate([do_t[HEAD_DIM * e:HEAD_DIM * (e + 1), :], jnp.zeros((HEAD_DIM, tb), F32)], axis=0)
                do2_ref[e, :, pl.ds(r0, tb)] = jnp.where(sub < HEAD_DIM, do_e, _rows3(DO_BIAS, -delta)).astype(BF16)
                minus_lse = jnp.where(queries >= PAD, -lse_t[e:e + 1, :], NEG)
                keep = (sub < Q_LSE) | (sub >= Q_LSE + 3)
                q2_ref[e, :, pl.ds(r0, tb)] = jnp.where(keep, qt_ref[e, :, pl.ds(r0, tb)].astype(F32),
                                                        _rows3(Q_LSE, minus_lse)).astype(BF16)
            return 0

        lax.fori_loop(0, nb, prepare, 0)
        dk_acc[...] = jnp.zeros_like(dk_acc)
        dv_acc[...] = jnp.zeros_like(dv_acc)

        def q_tile(q0, tq, pairs):
            first = q0 // tb
            qts = [q2_ref[e, :, pl.ds(q0, tq)] for e in range(2)]
            dots = [do2_ref[e, :, pl.ds(q0, tq)] for e in range(2)]

            def block(kj):
                return pl.ds(kj * tb if isinstance(kj, int) else pl.multiple_of(kj * tb, tb), tb)

            def step(kj, rd, wr, mask=None):
                for e in range(2):
                    s = s_buf[rd, e, :, 0:tq]
                    if wr is not None:
                        s_buf[wr, e, :, 0:tq] = _dot(k_ref[e, block(kj + 1), :], qts[e])
                    dpd = _dot(v_ref[e, block(kj), :], dots[e])
                    p = jnp.exp(s)
                    if mask is not None:
                        p = jnp.where(mask, p, 0.0)
                    dsb = (p * dpd).astype(BF16)
                    dv_acc[e, :, block(kj)] += _dot_nt(dots[e], p.astype(BF16))
                    dk_acc[e, :, block(kj)] += _dot_nt(qts[e], dsb)
                    dq_scr[e, :, 0:tq] += _dot(kt_ref[e, :, block(kj)], dsb)

            keys = lax.broadcasted_iota(jnp.int32, (tb, tq), 0)
            queries = lax.broadcasted_iota(jnp.int32, (tb, tq), 1)
            for e in range(2):
                dq_scr[e, :, 0:tq] = jnp.zeros((LANES, tq), F32)
                s_buf[0, e, :, 0:tq] = _dot(k_ref[e, block(0), :], qts[e])
            if pairs is None:
                step(0, 0, None, keys <= queries)
            else:
                step(0, 0, 1)

                def two_steps(t, _):
                    step(1 + 2 * t, 1, 0)
                    step(2 + 2 * t, 0, 1)
                    return 0

                lax.fori_loop(0, pairs, two_steps, 0)
                step(first, 1, 0, keys <= queries)
                step(first + 1, 0, None, keys + tb <= queries)
            dq0, dq1 = dq_scr[0, :, 0:tq], dq_scr[1, :, 0:tq]
            dq_ref[pl.ds(q0, tq), :] = (_pair_lanes(dq0.T, dq1.T) * 0.125).astype(BF16)
            dc_ref[pl.ds(q0, tq), :] = lanes01(dq0[K_ONES:K_ONES + 1, :], dq1[K_ONES:K_ONES + 1, :])

        q_tile(0, tb, None)

        def big_tile(i, _):
            q_tile(pl.multiple_of(tb + i * tq_big, tb), tq_big, i)
            return 0

        lax.fori_loop(0, n_big, big_tile, 0)

        def finish(bi, _):
            r0 = pl.multiple_of(bi * tb, tb)
            dk0, dk1 = dk_acc[0, :, pl.ds(r0, tb)], dk_acc[1, :, pl.ds(r0, tb)]
            dk_ref[pl.ds(r0, tb), :] = _pair_lanes(dk0.T, dk1.T).astype(BF16)
            dv_ref[pl.ds(r0, tb), :] = _pair_lanes(dv_acc[0, :, pl.ds(r0, tb)].T, dv_acc[1, :, pl.ds(r0, tb)].T).astype(BF16)
            dc_ref[pl.ds(r0, tb), :] = dc_ref[pl.ds(r0, tb), :] - lanes01(dk0[Q_ONES:Q_ONES + 1, :], dk1[Q_ONES:Q_ONES + 1, :])
            return 0

        lax.fori_loop(0, nb, finish, 0)

    once = pl.Buffered(1)
    pair = pl.BlockSpec((lp, LANES), lambda hp: (0, hp))
    pair_in = pl.BlockSpec((lp, LANES), lambda hp: (0, hp), pipeline_mode=once)
    heads = pl.BlockSpec((2, lp, LANES), lambda hp: (hp, 0, 0), pipeline_mode=once)
    heads_t = pl.BlockSpec((2, LANES, lp), lambda hp: (hp, 0, 0), pipeline_mode=once)
    wide = jax.ShapeDtypeStruct((lp, ATTN_WIDTH), BF16)
    return pl.pallas_call(
        body, name="attention_backward", grid=(N_HEADS // 2,),
        out_shape=[wide, wide, wide, jax.ShapeDtypeStruct((lp, ATTN_WIDTH), F32)],
        in_specs=[heads_t, heads, heads_t, heads, pair_in, pair_in, pair_in],
        out_specs=[pair, pair, pair, pair],
        scratch_shapes=[pltpu.VMEM((2, LANES, lp), BF16), pltpu.VMEM((2, LANES, lp), BF16),
                        pltpu.VMEM((2, LANES, lp), F32), pltpu.VMEM((2, LANES, lp), F32),
                        pltpu.VMEM((2, LANES, tq_big), F32), pltpu.VMEM((2, 2, tb, tq_big), F32)],
        compiler_params=_params(("arbitrary",)),
    )(qt, k, kt, v, do, o, lse)


def _middle(x, target, h, o, a_pool, u, zp, w_main, w_up_pool, w_up_attn, w_out, pool_w, pool_scale, final_g):
    seq = x.shape[0]
    tm = ROW_TILE
    nt = seq // tm + 1
    lp = nt * tm
    halo_blocks = tm // MAX_WINDOW

    def body(x_ref, t_ref, h_ref, o_ref, ap_ref, u_ref, uh_ref, zp_ref,
             wc_ref, wupp_ref, wupa_ref, wout_ref, pw_ref, sc_ref, gf_ref,
             dh2_ref, mg_ref, yp_ref, ya_ref, dap_ref, daa_ref, do_ref, dza_ref, dgp_ref, dga_ref, dzp_ref, dpn_ref,
             loss_ref, dgf_ref, dsc_ref, dpw_ref):
        i = pl.program_id(0)
        tiles = (dh2_ref, mg_ref, yp_ref, ya_ref, dap_ref, daa_ref, do_ref, dza_ref, dgp_ref, dga_ref, dzp_ref, dpn_ref)

        @pl.when(i == 0)
        def _():
            for ref in tiles + (loss_ref, dgf_ref, dsc_ref, dpw_ref):
                ref[...] = jnp.zeros_like(ref)

        @pl.when(i > 0)
        def _():
            xt = x_ref[...]
            hb = h_ref[...]
            pc = _dot_nt(hb, wc_ref[...])
            za, gp, ga = pc[:, :512], pc[:, 512:1536], pc[:, 1536:]
            of = o_ref[...].astype(F32)
            sza = _sigmoid(za)
            silu_za = za * sza
            ya = (of * silu_za).astype(BF16)
            ya_ref[...] = ya
            aa = _dot(ya, wupa_ref[...])
            ap = ap_ref[...].astype(F32)
            sgp, sga = _sigmoid(gp), _sigmoid(ga)
            mg = (sgp * ap + sga * aa).astype(BF16)
            mg_ref[...] = mg
            h2 = xt + _dot(mg, wout_ref[...])
            r2 = lax.rsqrt(jnp.mean(h2 * h2, axis=-1, keepdims=True) + RMS_EPS)
            h2n = h2 * r2
            gf = gf_ref[...]
            diff = h2n * gf - t_ref[...]
            loss_ref[...] += 0.5 * jnp.sum(jnp.mean(diff * diff, axis=-1, keepdims=True), axis=0, keepdims=True)
            dy = diff * (1.0 / D_MODEL)
            dgf_ref[...] += jnp.sum(dy * h2n, axis=0, keepdims=True)
            dyg = dy * gf
            dh2 = r2 * (dyg - h2n * jnp.mean(dyg * h2n, axis=-1, keepdims=True))
            dh2_ref[...] = dh2
            dmg = _dot_nt(dh2.astype(BF16), wout_ref[...])
            dap = (dmg * sgp).astype(BF16)
            daa = (dmg * sga).astype(BF16)
            dap_ref[...] = dap
            daa_ref[...] = daa
            dgp_ref[...] = (dmg * ap * sgp * (1.0 - sgp)).astype(BF16)
            dga_ref[...] = (dmg * aa * sga * (1.0 - sga)).astype(BF16)
            dyp = _dot_nt(dap, wupp_ref[...])
            dya = _dot_nt(daa, wupa_ref[...])
            do_ref[...] = (dya * silu_za).astype(BF16)
            dza_ref[...] = (dya * of * (sza * (1.0 + za * (1.0 - sza)))).astype(BF16)

            u = u_ref[...]
            zp = zp_ref[...]
            counts = _pool_counts(i * tm, tm)
            ps = _pool_means(jnp.concatenate([uh_ref[...], u], axis=0), u, counts)
            pbs = [p.astype(BF16) for p in ps]
            ppw = jnp.concatenate([_dot(pbs[g], pw_ref[g]) for g in range(4)], axis=1)
            sc = sc_ref[...]
            szp = _sigmoid(zp)
            silu_zp = zp * szp
            ypre = ppw * sc
            yp_ref[...] = (ypre * silu_zp).astype(BF16)
            dypre = dyp * silu_zp
            dzp_ref[...] = (dyp * ypre * (szp * (1.0 + zp * (1.0 - szp)))).astype(BF16)
            dsc_ref[...] += jnp.sum(dypre * ppw, axis=0, keepdims=True)
            dppw = (dypre * sc).astype(BF16)
            dpns = []
            for g in range(4):
                dg = dppw[:, POOL_GROUP * g:POOL_GROUP * (g + 1)]
                dpw_ref[g] += _dot_tn(pbs[g], dg)
                dpns.append(_dot_nt(dg, pw_ref[g]) / counts[g])
            dpn_ref[...] = jnp.concatenate(dpns, axis=1)

    real = lambda w: pl.BlockSpec((tm, w), lambda i: (jnp.maximum(i - 1, 0), 0))
    row = lambda w: pl.BlockSpec((tm, w), lambda i: (i, 0))
    in_specs = [
        real(D_MODEL), real(D_MODEL), row(D_MODEL), row(512), row(D_MODEL), row(512),
        pl.BlockSpec((MAX_WINDOW, 512), lambda i: (jnp.maximum(i * halo_blocks - 1, 0), 0)), row(512),
        _const((2560, D_MODEL), (1, 0)), _const((POOL_WIDTH, D_MODEL)), _const((ATTN_WIDTH, D_MODEL)),
        _const((D_MODEL, D_MODEL)), _const((4, POOL_GROUP, POOL_GROUP)), _const((1, POOL_WIDTH)), _const((1, D_MODEL)),
    ]
    sd = jax.ShapeDtypeStruct
    out_shape = [
        sd((lp, D_MODEL), F32),
        sd((lp, D_MODEL), BF16),
        sd((lp, 512), BF16),
        sd((lp, 512), BF16),
        sd((lp, D_MODEL), BF16),
        sd((lp, D_MODEL), BF16),
        sd((lp, 512), BF16),
        sd((lp, 512), BF16),
        sd((lp, D_MODEL), BF16),
        sd((lp, D_MODEL), BF16),
        sd((lp, 512), BF16),
        sd((lp, 512), F32),
        sd((1, 1), F32),
        sd((1, D_MODEL), F32),
        sd((1, 512), F32),
        sd((4, POOL_GROUP, POOL_GROUP), F32),
    ]
    keep = lambda shape: pl.BlockSpec(shape, lambda i: (0,) * len(shape))
    out_specs = [row(D_MODEL), row(D_MODEL), row(512), row(512), row(D_MODEL), row(D_MODEL), row(512), row(512),
                 row(D_MODEL), row(D_MODEL), row(512), row(512),
                 keep((1, 1)), keep((1, D_MODEL)), keep((1, 512)), keep((4, POOL_GROUP, POOL_GROUP))]
    return pl.pallas_call(
        body, name="middle", grid=(nt,), out_shape=out_shape, in_specs=in_specs, out_specs=out_specs,
        compiler_params=_params(("arbitrary",)),
    )(x, target, h, o, a_pool, u, u, zp, w_main, w_up_pool, w_up_attn, w_out, pool_w, pool_scale, final_g)


def _backward_in(x, tile0, norm_g, dh2, dpn, dzp, dq, dk, dv, dza, dgp, dga, dc, sneg, w_main, w_f):
    seq = x.shape[0]
    tm = ROW_TILE
    nt = seq // tm + 1
    lp = nt * tm
    halo_blocks = tm // MAX_WINDOW
    last_halo = lp // MAX_WINDOW - 1

    def body(x_ref, t0_ref, g_ref, dh2_ref, dpn_ref, dpnh_ref, dzp_ref, dq_ref, dk_ref, dv_ref, dza_ref,
             dgp_ref, dga_ref, dc_ref, sn_ref, wm_ref, wf_ref,
             dproj_ref, df_ref, gx_ref, gmeta_ref, dg_ref, dbf_ref, carry_ref):
        i = pl.program_id(0)
        t = nt - 1 - i

        @pl.when(i == 0)
        def _():
            carry_ref[...] = jnp.zeros_like(carry_ref)
            dg_ref[...] = jnp.zeros_like(dg_ref)
            dbf_ref[...] = jnp.zeros_like(dbf_ref)

        dpn_t = dpn_ref[...]
        ahead = jnp.where(i == 0, jnp.zeros_like(dpnh_ref), dpnh_ref[...])
        ext = jnp.concatenate([dpn_t, ahead], axis=0)
        counts = _pool_counts(t * tm, tm)
        for g, w in enumerate(POOL_WINDOWS):
            s = ext[:, POOL_GROUP * g:POOL_GROUP * (g + 1)]
            sh = 1
            while sh < w:
                s = s + pltpu.roll(s, tm + MAX_WINDOW - sh, axis=0)
                sh *= 2
            du = s[:tm, :] - dpn_t[:, POOL_GROUP * g:POOL_GROUP * (g + 1)] * counts[g]
            dproj_ref[:, POOL_GROUP * g:POOL_GROUP * (g + 1)] = du.astype(BF16)
        dproj_ref[:, 512:1024] = dzp_ref[...]
        dproj_ref[:, 1024:1536] = dq_ref[...]
        dproj_ref[:, 1536:2048] = dk_ref[...]
        dproj_ref[:, 2048:2560] = dv_ref[...]
        dproj_ref[:, 2560:3072] = dza_ref[...]
        dproj_ref[:, 3072:4096] = dgp_ref[...]
        dproj_ref[:, 4096:5120] = dga_ref[...]

        dct = dc_ref[:, 0:LANES]
        for hp in range(1, N_HEADS // 2):
            dct = dct + pltpu.roll(dc_ref[:, LANES * hp:LANES * (hp + 1)], 2 * hp, axis=1)
        rloc = lax.broadcasted_iota(jnp.int32, (tm, LANES), 0)
        sh = 1
        while sh < tm:
            dct = dct + jnp.where(rloc + sh < tm, pltpu.roll(dct, tm - sh, axis=0), 0.0)
            sh *= 2
        dct = dct + carry_ref[...]
        carry_ref[...] = dct[0:1, :]
        df = dct * sn_ref[...]
        dbf_ref[...] += jnp.sum(df, axis=0, keepdims=True)
        dfb = df.astype(BF16)
        df_ref[...] = dfb

        dh = _dot(dproj_ref[...], wm_ref[...]) + _dot(dfb, wf_ref[...])
        xt = jnp.where(t == 0, t0_ref[...], x_ref[...])
        r = lax.rsqrt(jnp.mean(xt * xt, axis=-1, keepdims=True) + RMS_EPS)
        xn = xt * r
        dg_ref[...] += jnp.sum(dh * xn, axis=0, keepdims=True)
        dhg = dh * g_ref[...]
        dx = dh2_ref[...] + r * (dhg - xn * jnp.mean(dhg * xn, axis=-1, keepdims=True))

        @pl.when(t > 0)
        def _():
            gx_ref[...] = dx

        @pl.when(t == 0)
        def _():
            gmeta_ref[...] = dx[PAD:, :]

    rev = lambda w: pl.BlockSpec((tm, w), lambda i: (nt - 1 - i, 0))
    real = pl.BlockSpec((tm, D_MODEL), lambda i: (jnp.maximum(nt - 2 - i, 0), 0))
    in_specs = [
        real, _const((tm, D_MODEL)), _const((1, D_MODEL)), rev(D_MODEL), rev(512),
        pl.BlockSpec((MAX_WINDOW, 512), lambda i: (jnp.minimum((nt - i) * halo_blocks, last_halo), 0)),
        rev(512), rev(512), rev(512), rev(512), rev(512), rev(D_MODEL), rev(D_MODEL),
        rev(512), rev(LANES),
        _const((N_MAIN, D_MODEL)), _const((LANES, D_MODEL)),
    ]
    sd = jax.ShapeDtypeStruct
    out_shape = [sd((lp, N_MAIN), BF16), sd((lp, LANES), BF16), sd((seq, D_MODEL), F32), sd((N_META, D_MODEL), F32),
                 sd((1, D_MODEL), F32), sd((1, LANES), F32)]
    keep = lambda shape: pl.BlockSpec(shape, lambda i: (0,) * len(shape))
    out_specs = [rev(N_MAIN), rev(LANES), real, keep((N_META, D_MODEL)), keep((1, D_MODEL)), keep((1, LANES))]
    return pl.pallas_call(
        body, name="backward_in", grid=(nt,), out_shape=out_shape, in_specs=in_specs, out_specs=out_specs,
        scratch_shapes=[pltpu.VMEM((1, LANES), F32)],
        compiler_params=_params(("arbitrary",)),
    )(x, tile0, norm_g, dh2, dpn, dpn, dzp, dq, dk, dv, dza, dgp, dga, dc, sneg, w_main, w_f)


def _matmul_tn(name, a, b, tn):
    lp, m = a.shape
    n = b.shape[1]

    def body(a_ref, b_ref, c_ref):
        c_ref[...] = _dot_tn(a_ref[...].astype(BF16), b_ref[...].astype(BF16))

    return pl.pallas_call(
        body, name=name, grid=(n // tn,), out_shape=jax.ShapeDtypeStruct((m, n), F32),
        in_specs=[_const((lp, m)), pl.BlockSpec((lp, tn), lambda j: (0, j))],
        out_specs=pl.BlockSpec((m, tn), lambda j: (0, j)),
        compiler_params=_params(("arbitrary",)),
    )(a, b)


def _matmul_tn_rows(name, a, b, tm):
    lp, m = a.shape
    n = b.shape[1]

    def body(a_ref, b_ref, c_ref):
        c_ref[...] = _dot_tn(a_ref[...].astype(BF16), b_ref[...].astype(BF16))

    return pl.pallas_call(
        body, name=name, grid=(m // tm,), out_shape=jax.ShapeDtypeStruct((m, n), F32),
        in_specs=[pl.BlockSpec((lp, tm), lambda j: (0, j)), _const((lp, n))],
        out_specs=pl.BlockSpec((tm, n), lambda j: (j, 0)),
        compiler_params=_params(("arbitrary",)),
    )(a, b)


def _adamw(name, parts, w, m, v, rows, cols=None):
    r, c_all = w.shape
    c = cols or c_all
    n_parts = parts.shape[0]

    def body(p_ref, w_ref, m_ref, v_ref, g_ref, d_ref, mo_ref, vo_ref):
        g = p_ref[0].astype(F32)
        for s in range(1, n_parts):
            g = g + p_ref[s].astype(F32)
        m_new = ADAM_B1 * m_ref[...] + (1.0 - ADAM_B1) * g
        v_new = ADAM_B2 * v_ref[...] + (1.0 - ADAM_B2) * (g * g)
        m_hat = m_new / (1.0 - ADAM_B1 ** ADAM_STEP)
        v_hat = v_new / (1.0 - ADAM_B2 ** ADAM_STEP)
        g_ref[...] = g
        d_ref[...] = -ADAM_LR * (m_hat / (jnp.sqrt(v_hat) + ADAM_EPS) + ADAM_WD * w_ref[...])
        mo_ref[...] = m_new
        vo_ref[...] = v_new

    blk = pl.BlockSpec((rows, c), lambda i, j: (i, j))
    return pl.pallas_call(
        body, name=name, grid=(r // rows, c_all // c), out_shape=[jax.ShapeDtypeStruct((r, c_all), F32)] * 4,
        in_specs=[pl.BlockSpec((n_parts, rows, c), lambda i, j: (0, i, j)), blk, blk, blk],
        out_specs=[blk] * 4,
        compiler_params=_params(("arbitrary", "arbitrary")),
    )(parts, w, m, v)


def _pair_sum(name, mine, theirs, rows):
    n, r, c = mine.shape

    def body(a_ref, b_ref, o_ref):
        o_ref[...] = (a_ref[...] + b_ref[...].astype(F32)).astype(BF16)

    blk = pl.BlockSpec((1, rows, c), lambda j, i: (j, i, 0))
    return pl.pallas_call(
        body, name=name, grid=(n, r // rows), out_shape=jax.ShapeDtypeStruct((n, r, c), BF16),
        in_specs=[blk, blk], out_specs=blk,
        compiler_params=_params(("arbitrary", "arbitrary")),
    )(mine, theirs)


def _columns_to_slots(a):
    r, c8 = a.shape
    return a.reshape(r, N_DEV, c8 // N_DEV).transpose(1, 0, 2)


def _by_core(slots):
    by_core = slots.reshape((4, 2) + slots.shape[1:]).swapaxes(0, 1)
    c = lax.axis_index("c")
    return (lax.dynamic_index_in_dim(by_core, c, 0, keepdims=False),
            lax.dynamic_index_in_dim(by_core, 1 - c, 0, keepdims=False).astype(BF16))


def _slots_to_columns(a):
    n, r, c = a.shape
    return a.transpose(1, 0, 2).reshape(r, n * c)


def kernel(x, meta_tokens, norm_g, w_in, b_forget, pool_w, pool_scale, w_up_pool, w_up_attn, w_out, final_norm_g, loss_target, m_meta_tokens, m_norm_g, m_w_in, m_b_forget, m_pool_w, m_pool_scale, m_w_up_pool, m_w_up_attn, m_w_out, m_final_norm_g, v_meta_tokens, v_norm_g, v_w_in, v_b_forget, v_pool_w, v_pool_scale, v_w_up_pool, v_w_up_attn, v_w_out, v_final_norm_g):
    xs = x[0]
    target = loss_target[0]

    g_in, g_upp, g_upa, g_out, g_meta = _gather_two_level(
        "gather_weights",
        [w_in[0].T.astype(BF16), w_up_pool[0].astype(BF16), w_up_attn[0].astype(BF16), w_out[0].astype(BF16),
         meta_tokens])
    w_full = g_in.reshape(N_DEV * g_in.shape[1], D_MODEL)
    w_main = jnp.concatenate([w_full[:N_BEFORE_F], w_full[N_BEFORE_F + N_HEADS:]], axis=0)
    w_f = jnp.pad(w_full[N_BEFORE_F:N_BEFORE_F + N_HEADS], ((0, LANES - N_HEADS), (0, 0)))
    wupp = _slots_to_columns(g_upp)
    wupa = _slots_to_columns(g_upa)
    wout = g_out.reshape(D_MODEL, D_MODEL)
    meta = _slots_to_columns(g_meta)
    tile0 = jnp.concatenate([jnp.zeros((PAD, D_MODEL), F32), meta], axis=0)
    b_f = jnp.pad(b_forget, ((0, 0), (0, LANES - N_HEADS)))
    pw_b = pool_w[0].astype(BF16)
    final_g = final_norm_g.reshape(1, D_MODEL)

    (h, u, zp, k, v, qt, kt, vt, sneg, a_pool) = _forward_in(xs, tile0, norm_g, w_main, w_f, b_f, pw_b,
                                                              pool_scale, wupp)
    o, lse = _attention_forward(qt, k, vt)
    (dh2, mg, yp, ya, dap, daa, do, dza, dgp, dga, dzp, dpn,
     loss_part, d_final_g, d_scale, d_pool_w) = _middle(xs, target, h, o, a_pool, u, zp, w_main, wupp, wupa, wout,
                                                        pw_b, pool_scale, final_g)
    dq, dk, dv, dc = _attention_backward(qt, k, kt, v, do, o, lse)
    dproj, df, grad_x, d_meta, d_norm_g, d_bf = _backward_in(xs, tile0, norm_g, dh2, dpn, dzp, dq, dk, dv, dza,
                                                             dgp, dga, dc, sneg, w_main, w_f)
    dw_main = _matmul_tn_rows("grad_w_in", dproj, h, 512)
    dw_f = _matmul_tn_rows("grad_w_forget", df, h, LANES)
    dw_out = _matmul_tn("grad_w_out", mg, dh2, 256)
    dw_upp = _matmul_tn("grad_w_up_pool", yp, dap, 512)
    dw_upa = _matmul_tn("grad_w_up_attn", ya, daa, 512)
    dw_in = jnp.concatenate([dw_main[:N_BEFORE_F], dw_f[:N_HEADS], dw_main[N_BEFORE_F:]], axis=0)
    dw_in = dw_in.reshape(N_DEV, dw_in.shape[0] // N_DEV, D_MODEL)

    big = [dw_in, _columns_to_slots(dw_upp), _columns_to_slots(dw_upa),
           dw_out.reshape(N_DEV, D_MODEL // N_DEV, D_MODEL)]
    mine, for_sibling = zip(*[_by_core(s) for s in big])
    from_sibling = _exchange("swap_with_sibling", [("swap", a, (SIBLING,)) for a in for_sibling])
    chip_rows = (dw_in.shape[1], 512, 512, 128)
    sums = [_pair_sum(f"pair_sum_{j}", mine[j], from_sibling[j], chip_rows[j]) for j in range(4)]
    (p_in, p_upp, p_upa, p_out, p_meta, p_norm_g, p_bf, p_pool_w, p_scale, p_final_g) = _exchange(
        "exchange_gradients",
        [("chips", s, SAME_CORE) for s in sums]
        + [("scatter", _columns_to_slots(d_meta), ALL_PEERS)]
        + [("gather", a, ALL_PEERS) for a in
           (d_norm_g, d_bf, d_pool_w.reshape(4 * POOL_GROUP, POOL_GROUP), d_scale, d_final_g)])

    loss = lax.psum(loss_part[0, 0], ("x", "y", "c"))

    def pad_f(a):
        return jnp.pad(a, ((0, 0), (0, LANES - N_HEADS)))

    res = {}
    res["meta_tokens"] = _adamw("adamw_meta", p_meta, meta_tokens, m_meta_tokens, v_meta_tokens, N_META)
    res["norm_g"] = _adamw("adamw_norm_g", p_norm_g, norm_g, m_norm_g, v_norm_g, 1)
    res["w_in"] = [a.T for a in _adamw("adamw_w_in", p_in, w_in[0].T, m_w_in[0].T, v_w_in[0].T, p_in.shape[1], 256)]
    bf = _adamw("adamw_b_forget", p_bf, pad_f(b_forget), pad_f(m_b_forget), pad_f(v_b_forget), 1)
    res["b_forget"] = [a[:, :N_HEADS] for a in bf]
    pw = _adamw("adamw_pool_w", p_pool_w, pool_w.reshape(512, 128), m_pool_w.reshape(512, 128),
                v_pool_w.reshape(512, 128), 512)
    res["pool_w"] = [a.reshape(pool_w.shape) for a in pw]
    res["pool_scale"] = _adamw("adamw_pool_scale", p_scale, pool_scale, m_pool_scale, v_pool_scale, 1)
    res["w_up_pool"] = _adamw("adamw_w_up_pool", p_upp, w_up_pool[0], m_w_up_pool[0], v_w_up_pool[0], 512)
    res["w_up_attn"] = _adamw("adamw_w_up_attn", p_upa, w_up_attn[0], m_w_up_attn[0], v_w_up_attn[0], 512)
    res["w_out"] = _adamw("adamw_w_out", p_out, w_out[0], m_w_out[0], v_w_out[0], 128)
    fg = _adamw("adamw_final_norm_g", p_final_g, final_g, m_final_norm_g.reshape(1, D_MODEL),
                v_final_norm_g.reshape(1, D_MODEL), 1)
    res["final_norm_g"] = [a.reshape(D_MODEL) for a in fg]
    for name in ("w_in", "w_up_pool", "w_up_attn", "w_out"):
        res[name] = [a[None] for a in res[name]]

    order = ["meta_tokens", "norm_g", "w_in", "b_forget", "pool_w", "pool_scale", "w_up_pool", "w_up_attn", "w_out",
             "final_norm_g"]
    outs = [loss, grad_x[None]]
    for part in range(4):
        outs += [res[name][part] for name in order]
    return tuple(outs)
```

```python
import functools

import jax
import jax.numpy as jnp
from jax import lax
from jax.experimental import pallas as pl
from jax.experimental.pallas import tpu as pltpu

F32 = jnp.float32
BF16 = jnp.bfloat16

D_MODEL = 1024
N_META = 16
POOL_WIDTH = 512
ATTN_WIDTH = 512
N_HEADS = 8
HEAD_DIM = 64
POOL_WINDOWS = (2, 4, 8, 16)
POOL_GROUP = 128
MAX_WINDOW = 16
RMS_EPS = 1e-6
N_MAIN = 5120
N_BEFORE_F = 3072
N_DEV = 8
LANES = 128

ROW_TILE = 256
ATT_TILE = 256
ATT_Q_BLOCKS = 2
PAD = ROW_TILE - N_META
VMEM_LIMIT = 56 * 1024 * 1024

ADAM_LR = 0.001
ADAM_B1 = 0.9
ADAM_B2 = 0.999
ADAM_EPS = 1e-08
ADAM_WD = 0.01
ADAM_STEP = 10

NEG = -1e30
MESH = pl.DeviceIdType.MESH


def _params(sem=None):
    kw = dict(vmem_limit_bytes=VMEM_LIMIT)
    if sem is not None:
        kw["dimension_semantics"] = sem
    return pltpu.CompilerParams(**kw)


def _const(shape, block_index=None):
    idx = block_index or (0,) * len(shape)
    return pl.BlockSpec(shape, lambda i: idx, pipeline_mode=pl.Buffered(1))


def _sigmoid(x):
    return jax.nn.sigmoid(x)


def _dot(a, b):
    return jnp.dot(a, b, preferred_element_type=F32)


def _dot_nt(a, b):
    return lax.dot_general(a, b, (((1,), (1,)), ((), ())), preferred_element_type=F32)


def _dot_tn(a, b):
    return lax.dot_general(a, b, (((0,), (0,)), ((), ())), preferred_element_type=F32)


def _pool_counts(first_row, rows):
    row = first_row + lax.broadcasted_iota(jnp.int32, (rows, 1), 0)
    pos1 = row - PAD + 1
    return [jnp.clip(pos1, 1, w).astype(F32) for w in POOL_WINDOWS]


def _pool_means(u_ext, u, counts):
    rows = u.shape[0]
    out = []
    for g, w in enumerate(POOL_WINDOWS):
        s = u_ext[:, POOL_GROUP * g:POOL_GROUP * (g + 1)]
        sh = 1
        while sh < w:
            s = s + pltpu.roll(s, sh, axis=0)
            sh *= 2
        out.append(s[MAX_WINDOW:MAX_WINDOW + rows, :] / counts[g] - u[:, POOL_GROUP * g:POOL_GROUP * (g + 1)])
    return out


Q_BIAS, Q_ONES, Q_LSE = 64, 67, 70
K_ONES, K_BIAS, K_ONES2 = 64, 67, 70
V_ONES = 64
DO_BIAS = 64


def _lane_ones(lane, ranges):
    hit = None
    for lo, hi in ranges:
        r = (lane >= lo) & (lane < hi)
        hit = r if hit is None else hit | r
    return jnp.where(hit, 1.0, 0.0)


def _put3(base, lane, first, x):
    hi = x.astype(BF16).astype(F32)
    rest = x - hi
    mid = rest.astype(BF16).astype(F32)
    lo = (rest - mid).astype(BF16).astype(F32)
    for j, piece in enumerate((hi, mid, lo)):
        base = jnp.where(lane == first + j, piece, base)
    return base


SIBLING = 1
SAME_CORE = (2, 4, 6)
ALL_PEERS = (1, 2, 3, 4, 5, 6, 7)


def _place():
    return lax.axis_index("x"), lax.axis_index("y"), lax.axis_index("c")


def _peer(r):
    x, y, c = _place()
    return (1 - x if r & 4 else x, 1 - y if r & 2 else y, 1 - c if r & 1 else c)


def _device_slot(p):
    return 4 * p[0] + 2 * p[1] + p[2]


def _chip_slot(p):
    return 2 * p[0] + p[1]


def _exchange(name, items):
    n = len(items)

    def body(*refs):
        copies = _exchange_copies(items, refs[:n], refs[n:2 * n], *refs[2 * n:])
        for cp in copies:
            cp.start()
        for cp in copies:
            cp.wait()

    hbm = pl.BlockSpec(memory_space=pl.ANY)
    return pl.pallas_call(
        body, name=name, out_shape=_exchange_results(items),
        in_specs=[hbm] * n, out_specs=[hbm] * n,
        scratch_shapes=_exchange_semaphores(n),
    )(*[a for _, a, _ in items])


def _exchange_results(items):
    return [jax.ShapeDtypeStruct(((N_DEV,) if kind == "gather" else ()) + a.shape, a.dtype) for kind, a, _ in items]


def _exchange_semaphores(n):
    return [pltpu.SemaphoreType.DMA((n, N_DEV - 1)), pltpu.SemaphoreType.DMA((n, N_DEV - 1)),
            pltpu.SemaphoreType.DMA((n,))]


def _exchange_copies(items, ins, outs, send_sems, recv_sems, local_sems):
    me = _place()
    copies = []
    for a, (kind, _, peers) in enumerate(items):
        slot = _chip_slot if kind == "chips" else _device_slot
        for r in peers:
            peer = _peer(r)
            src = ins[a] if kind in ("swap", "gather") else ins[a].at[slot(peer)]
            dst = outs[a] if kind == "swap" else outs[a].at[slot(me)]
            copies.append(pltpu.make_async_remote_copy(
                src_ref=src, dst_ref=dst, send_sem=send_sems.at[a, r - 1], recv_sem=recv_sems.at[a, r - 1],
                device_id=peer, device_id_type=MESH))
        if kind != "swap":
            src = ins[a] if kind == "gather" else ins[a].at[slot(me)]
            copies.append(pltpu.make_async_copy(src, outs[a].at[slot(me)], local_sems.at[a]))
    return copies


def _gather_two_level(name, arrays):
    n = len(arrays)

    def body(*refs):
        ins, outs = refs[:n], refs[n:2 * n]
        send_sems, recv_sems, local_sems = refs[2 * n:]
        me = _place()
        sibling = _peer(SIBLING)

        def copy(a, k, block, to, src=None):
            rows = outs[a].at[_device_slot(block)]
            return pltpu.make_async_remote_copy(
                src_ref=rows if src is None else src, dst_ref=rows,
                send_sem=send_sems.at[a, k], recv_sem=recv_sems.at[a, k], device_id=to, device_id_type=MESH)

        sends, own = [], []
        for a in range(n):
            mine = pltpu.make_async_copy(ins[a], outs[a].at[_device_slot(me)], local_sems.at[a])
            mine.start()
            own.append(mine)
            for k, r in enumerate((SIBLING,) + SAME_CORE):
                cp = copy(a, k, me, _peer(r), src=ins[a])
                cp.start()
                sends.append(cp)
        for a in range(n):
            for j, r in enumerate(SAME_CORE):
                copy(a, 1 + j, _peer(r), me).wait_recv()
                passed = copy(a, 4 + j, _peer(r), sibling)
                passed.start()
                sends.append(passed)
        for a in range(n):
            copy(a, 0, sibling, me).wait_recv()
            for j, r in enumerate(SAME_CORE):
                copy(a, 4 + j, _peer(r | SIBLING), me).wait_recv()
        for cp in sends:
            cp.wait_send()
        for cp in own:
            cp.wait()

    hbm = pl.BlockSpec(memory_space=pl.ANY)
    return pl.pallas_call(
        body, name=name, out_shape=[jax.ShapeDtypeStruct((N_DEV,) + a.shape, a.dtype) for a in arrays],
        in_specs=[hbm] * n, out_specs=[hbm] * n,
        scratch_shapes=[pltpu.SemaphoreType.DMA((n, N_DEV - 1)), pltpu.SemaphoreType.DMA((n, N_DEV - 1)),
                        pltpu.SemaphoreType.DMA((n,))],
    )(*arrays)


def _forward_in(x, tile0, norm_g, w_main, w_f, b_f, pool_w, pool_scale, w_up_pool):
    seq = x.shape[0]
    nt = seq // ROW_TILE + 1
    lp = nt * ROW_TILE
    tm = ROW_TILE

    def body(x_ref, t0_ref, g_ref, wa_ref, wf_ref, bf_ref, pw_ref, sc_ref, wup_ref,
             h_ref, u_ref, zp_ref, k_ref, v_ref, qt_ref, kt_ref, vt_ref, sn_ref, ap_ref,
             uext_ref, carry_ref):
        i = pl.program_id(0)

        @pl.when(i == 0)
        def _():
            uext_ref[...] = jnp.zeros_like(uext_ref)
            carry_ref[...] = jnp.zeros_like(carry_ref)

        xt = jnp.where(i == 0, t0_ref[...], x_ref[...])
        r = lax.rsqrt(jnp.mean(xt * xt, axis=-1, keepdims=True) + RMS_EPS)
        h = (xt * r * g_ref[...]).astype(BF16)
        h_ref[...] = h
        pa = _dot_nt(h, wa_ref[...])
        u = pa[:, :512]
        zp = pa[:, 512:1024]
        u_ref[...] = u
        zp_ref[...] = zp

        uext_ref[0:MAX_WINDOW, :] = uext_ref[tm:tm + MAX_WINDOW, :]
        uext_ref[MAX_WINDOW:, :] = u
        counts = _pool_counts(i * tm, tm)
        ps = _pool_means(uext_ref[...], u, counts)
        ppw = jnp.concatenate([_dot(ps[g].astype(BF16), pw_ref[g]) for g in range(4)], axis=1)
        y_pool = ppw * sc_ref[...] * (zp * _sigmoid(zp))
        ap_ref[...] = _dot(y_pool.astype(BF16), wup_ref[...]).astype(BF16)

        fl = _dot_nt(h, wf_ref[...]) + bf_ref[...]
        row = i * tm + lax.broadcasted_iota(jnp.int32, (tm, LANES), 0)
        rloc = lax.broadcasted_iota(jnp.int32, (tm, LANES), 0)
        lane = lax.broadcasted_iota(jnp.int32, (tm, LANES), 1)
        live = (row >= PAD) & (lane < N_HEADS)
        logf = jnp.minimum(fl, 0.0) - jnp.log1p(jnp.exp(-jnp.abs(fl)))
        cs = jnp.where(live, logf, 0.0)
        sh = 1
        while sh < tm:
            cs = cs + jnp.where(rloc >= sh, pltpu.roll(cs, sh, axis=0), 0.0)
            sh *= 2
        cs = cs + carry_ref[...]
        carry_ref[...] = cs[tm - 1:tm, :]
        sn_ref[...] = jnp.where(live, _sigmoid(-fl), 0.0)

        rows1 = i * tm + lax.broadcasted_iota(jnp.int32, (tm, 1), 0)
        ones_q = _lane_ones(lane, ((Q_ONES, Q_ONES + 3),))
        ones_k = _lane_ones(lane, ((K_ONES, K_ONES + 3), (K_ONES2, K_ONES2 + 3)))
        ones_v = _lane_ones(lane, ((V_ONES, V_ONES + 3),))
        for hp in range(N_HEADS // 2):
            qp = pa[:, 1024 + LANES * hp:1024 + LANES * (hp + 1)] * 0.125
            kp = pa[:, 1536 + LANES * hp:1536 + LANES * (hp + 1)]
            vp = pa[:, 2048 + LANES * hp:2048 + LANES * (hp + 1)]
            for e in range(2):
                head = 2 * hp + e
                if e:
                    qp, kp, vp = (pltpu.roll(a, HEAD_DIM, axis=1) for a in (qp, kp, vp))
                c_h = cs[:, head:head + 1]
                q_h = jnp.where(lane < HEAD_DIM, qp, _put3(ones_q, lane, Q_BIAS, c_h))
                qt_ref[head] = q_h.T.astype(BF16)
                minus_ck = jnp.where(rows1 >= PAD, -c_h, NEG)
                k_h = jnp.where(lane < HEAD_DIM, kp, _put3(ones_k, lane, K_BIAS, minus_ck))
                k_ref[head] = k_h.astype(BF16)
                kt_ref[head] = k_h.T.astype(BF16)
                v_h = jnp.where(lane < HEAD_DIM, vp, ones_v)
                v_ref[head] = v_h.astype(BF16)
                vt_ref[head] = v_h.T.astype(BF16)

    row_f32 = lambda w: pl.BlockSpec((tm, w), lambda i: (i, 0))
    out_shape = [
        jax.ShapeDtypeStruct((lp, D_MODEL), BF16),
        jax.ShapeDtypeStruct((lp, POOL_WIDTH), F32),
        jax.ShapeDtypeStruct((lp, POOL_WIDTH), F32),
        jax.ShapeDtypeStruct((N_HEADS, lp, LANES), BF16),
        jax.ShapeDtypeStruct((N_HEADS, lp, LANES), BF16),
        jax.ShapeDtypeStruct((N_HEADS, LANES, lp), BF16),
        jax.ShapeDtypeStruct((N_HEADS, LANES, lp), BF16),
        jax.ShapeDtypeStruct((N_HEADS, LANES, lp), BF16),
        jax.ShapeDtypeStruct((lp, LANES), F32),
        jax.ShapeDtypeStruct((lp, D_MODEL), BF16),
    ]
    heads = pl.BlockSpec((N_HEADS, tm, LANES), lambda i: (0, i, 0))
    heads_t = pl.BlockSpec((N_HEADS, LANES, tm), lambda i: (0, 0, i))
    out_specs = [row_f32(D_MODEL), row_f32(512), row_f32(512), heads, heads, heads_t, heads_t, heads_t,
                 row_f32(LANES), row_f32(D_MODEL)]
    in_specs = [
        pl.BlockSpec((tm, D_MODEL), lambda i: (jnp.maximum(i - 1, 0), 0)),
        _const((tm, D_MODEL)), _const((1, D_MODEL)),
        _const((2560, D_MODEL)), _const((LANES, D_MODEL)), _const((1, LANES)),
        _const((4, POOL_GROUP, POOL_GROUP)), _const((1, POOL_WIDTH)), _const((POOL_WIDTH, D_MODEL)),
    ]
    return pl.pallas_call(
        body, name="forward_in", grid=(nt,), out_shape=out_shape, in_specs=in_specs, out_specs=out_specs,
        scratch_shapes=[pltpu.VMEM((tm + MAX_WINDOW, POOL_WIDTH), F32), pltpu.VMEM((1, LANES), F32)],
        compiler_params=_params(("arbitrary",)),
    )(x, tile0, norm_g, w_main, w_f, b_f, pool_w, pool_scale, w_up_pool)


def _causal(tb):
    return lax.broadcasted_iota(jnp.int32, (tb, tb), 1) <= lax.broadcasted_iota(jnp.int32, (tb, tb), 0)


def _pair_lanes(a0, a1):
    lane = lax.broadcasted_iota(jnp.int32, a0.shape, 1)
    return jnp.where(lane < HEAD_DIM, a0, pltpu.roll(a1, HEAD_DIM, axis=1))


def _behind(items, ins, outs, sems):
    step, last = pl.program_id(0), pl.num_programs(0) - 1

    @pl.when(step == 0)
    def _():
        for cp in _exchange_copies(items, ins, outs, *sems):
            cp.start()

    def finish():
        @pl.when(step == last)
        def _():
            for cp in _exchange_copies(items, ins, outs, *sems):
                cp.wait()

    return finish


def _attention_forward(qt, k, vt, behind):
    lp = k.shape[1]
    tk = ATT_TILE
    tq_big = ATT_Q_BLOCKS * tk
    n_big = (lp // tk - 1) // ATT_Q_BLOCKS
    assert lp == tk + n_big * tq_big and ATT_Q_BLOCKS == 2
    nx = len(behind)

    def body(qt_ref, k_ref, vt_ref, *rest):
        o_ref, lse_ref = rest[nx:nx + 2]
        s_buf, m_scr, acc_scr = rest[2 * nx + 2:2 * nx + 5]
        finish_exchange = _behind(behind, rest[:nx], rest[nx + 2:2 * nx + 2], rest[2 * nx + 5:])

        def q_tile(q0, tq, pairs):
            first = q0 // tk
            qts = [qt_ref[e, :, pl.ds(q0, tq)] for e in range(2)]

            def block(kj):
                return pl.ds(kj * tk if isinstance(kj, int) else pl.multiple_of(kj * tk, tk), tk)

            def step(kj, rd, wr, mask=None):
                for e in range(2):
                    s = s_buf[rd, e, :, 0:tq]
                    if wr is not None:
                        s_buf[wr, e, :, 0:tq] = _dot(k_ref[e, block(kj + 1), :], qts[e])
                    if mask is not None:
                        s = jnp.where(mask, s, NEG)
                    m = m_scr[e, :, 0:tq]
                    m_new = jnp.maximum(m, jnp.max(s, axis=0, keepdims=True))
                    p = jnp.exp(s - m_new)
                    pv = _dot(vt_ref[e, :, block(kj)], p.astype(BF16))
                    acc_scr[e, :, 0:tq] = jnp.exp(m - m_new) * acc_scr[e, :, 0:tq] + pv
                    m_scr[e, :, 0:tq] = m_new

            keys = lax.broadcasted_iota(jnp.int32, (tk, tq), 0)
            queries = lax.broadcasted_iota(jnp.int32, (tk, tq), 1)
            for e in range(2):
                m_scr[e, :, 0:tq] = jnp.full((1, tq), NEG, F32)
                acc_scr[e, :, 0:tq] = jnp.zeros((LANES, tq), F32)
                s_buf[0, e, :, 0:tq] = _dot(k_ref[e, block(0), :], qts[e])
            if pairs is None:
                step(0, 0, None, keys <= queries)
            else:
                step(0, 0, 1)

                def two_steps(t, _):
                    step(1 + 2 * t, 1, 0)
                    step(2 + 2 * t, 0, 1)
                    return 0

                lax.fori_loop(0, pairs, two_steps, 0)
                step(first, 1, 0, keys <= queries)
                step(first + 1, 0, None, keys + tk <= queries)
            outs, lses = [], []
            for e in range(2):
                acc = acc_scr[e, :, 0:tq]
                l = acc[V_ONES:V_ONES + 1, :]
                outs.append((acc / l).T)
                lses.append(m_scr[e, :, 0:tq] + jnp.log(l))
            o_ref[pl.ds(q0, tq), :] = _pair_lanes(outs[0], outs[1]).astype(BF16)
            lse_rows = jnp.concatenate(lses + [jnp.zeros((LANES - 2, tq), F32)], axis=0)
            lse_ref[pl.ds(q0, tq), :] = lse_rows.T

        q_tile(0, tk, None)

        def big_tile(i, _):
            q_tile(pl.multiple_of(tk + i * tq_big, tk), tq_big, i)
            return 0

        lax.fori_loop(0, n_big, big_tile, 0)
        finish_exchange()

    pair = pl.BlockSpec((lp, LANES), lambda hp: (0, hp))
    heads = pl.BlockSpec((2, lp, LANES), lambda hp: (hp, 0, 0), pipeline_mode=pl.Buffered(1))
    heads_t = pl.BlockSpec((2, LANES, lp), lambda hp: (hp, 0, 0), pipeline_mode=pl.Buffered(1))
    hbm = pl.BlockSpec(memory_space=pl.ANY)
    return pl.pallas_call(
        body, name="attention_forward", grid=(N_HEADS // 2,),
        out_shape=[jax.ShapeDtypeStruct((lp, ATTN_WIDTH), BF16), jax.ShapeDtypeStruct((lp, ATTN_WIDTH), F32)]
        + _exchange_results(behind),
        in_specs=[heads_t, heads, heads_t] + [hbm] * nx,
        out_specs=[pair, pair] + [hbm] * nx,
        scratch_shapes=[pltpu.VMEM((2, 2, tk, tq_big), F32), pltpu.VMEM((2, 1, tq_big), F32),
                        pltpu.VMEM((2, LANES, tq_big), F32)] + _exchange_semaphores(nx),
        compiler_params=_params(("arbitrary",)),
    )(qt, k, vt, *[a for _, a, _ in behind])


def _rows3(first, x):
    sub = lax.broadcasted_iota(jnp.int32, (LANES, x.shape[1]), 0)
    hi = x.astype(BF16).astype(F32)
    rest = x - hi
    mid = rest.astype(BF16).astype(F32)
    lo = (rest - mid).astype(BF16).astype(F32)
    out = jnp.zeros((LANES, x.shape[1]), F32)
    for j, piece in enumerate((hi, mid, lo)):
        out = jnp.where(sub == first + j, piece, out)
    return out


def _attention_backward(qt, k, kt, v, do, o, lse, behind):
    lp = k.shape[1]
    tb = ATT_TILE
    nb = lp // tb
    tq_big = ATT_Q_BLOCKS * tb
    n_big = (nb - 1) // ATT_Q_BLOCKS
    assert lp == tb + n_big * tq_big and ATT_Q_BLOCKS == 2
    nx = len(behind)

    def body(qt_ref, k_ref, kt_ref, v_ref, do_ref, o_ref, lse_ref, *rest):
        dq_ref, dk_ref, dv_ref, dc_ref = rest[nx:nx + 4]
        q2_ref, do2_ref, dk_acc, dv_acc, dq_scr, s_buf = rest[2 * nx + 4:2 * nx + 10]
        finish_exchange = _behind(behind, rest[:nx], rest[nx + 4:2 * nx + 4], rest[2 * nx + 10:])
        sub = lax.broadcasted_iota(jnp.int32, (LANES, tb), 0)

        def lanes01(row0, row1):
            n = row0.shape[1]
            return jnp.concatenate([row0, row1, jnp.zeros((LANES - 2, n), F32)], axis=0).T

        def prepare(bi, _):
            r0 = pl.multiple_of(bi * tb, tb)
            queries = r0 + lax.broadcasted_iota(jnp.int32, (1, tb), 1)
            dob = do_ref[pl.ds(r0, tb), :].astype(F32)
            do_t = dob.T
            dd_t = (dob * o_ref[pl.ds(r0, tb), :].astype(F32)).T
            lse_t = lse_ref[pl.ds(r0, tb), :].T
            for e in range(2):
                delta = jnp.sum(dd_t[HEAD_DIM * e:HEAD_DIM * (e + 1), :], axis=0, keepdims=True)
                do_e = jnp.concatenate([do_t[HEAD_DIM * e:HEAD_DIM * (e + 1), :], jnp.zeros((HEAD_DIM, tb), F32)], axis=0)
                do2_ref[e, :, pl.ds(r0, tb)] = jnp.where(sub < HEAD_DIM, do_e, _rows3(DO_BIAS, -delta)).astype(BF16)
                minus_lse = jnp.where(queries >= PAD, -lse_t[e:e + 1, :], NEG)
                keep = (sub < Q_LSE) | (sub >= Q_LSE + 3)
                q2_ref[e, :, pl.ds(r0, tb)] = jnp.where(keep, qt_ref[e, :, pl.ds(r0, tb)].astype(F32),
                                                        _rows3(Q_LSE, minus_lse)).astype(BF16)
            return 0

        lax.fori_loop(0, nb, prepare, 0)
        dk_acc[...] = jnp.zeros_like(dk_acc)
        dv_acc[...] = jnp.zeros_like(dv_acc)

        def q_tile(q0, tq, pairs):
            first = q0 // tb
            qts = [q2_ref[e, :, pl.ds(q0, tq)] for e in range(2)]
            dots = [do2_ref[e, :, pl.ds(q0, tq)] for e in range(2)]

            def block(kj):
                return pl.ds(kj * tb if isinstance(kj, int) else pl.multiple_of(kj * tb, tb), tb)

            def step(kj, rd, wr, mask=None):
                for e in range(2):
                    s = s_buf[rd, e, :, 0:tq]
                    if wr is not None:
                        s_buf[wr, e, :, 0:tq] = _dot(k_ref[e, block(kj + 1), :], qts[e])
                    dpd = _dot(v_ref[e, block(kj), :], dots[e])
                    p = jnp.exp(s)
                    if mask is not None:
                        p = jnp.where(mask, p, 0.0)
                    dsb = (p * dpd).astype(BF16)
                    dv_acc[e, :, block(kj)] += _dot_nt(dots[e], p.astype(BF16))
                    dk_acc[e, :, block(kj)] += _dot_nt(qts[e], dsb)
                    dq_scr[e, :, 0:tq] += _dot(kt_ref[e, :, block(kj)], dsb)

            keys = lax.broadcasted_iota(jnp.int32, (tb, tq), 0)
            queries = lax.broadcasted_iota(jnp.int32, (tb, tq), 1)
            for e in range(2):
                dq_scr[e, :, 0:tq] = jnp.zeros((LANES, tq), F32)
                s_buf[0, e, :, 0:tq] = _dot(k_ref[e, block(0), :], qts[e])
            if pairs is None:
                step(0, 0, None, keys <= queries)
            else:
                step(0, 0, 1)

                def two_steps(t, _):
                    step(1 + 2 * t, 1, 0)
                    step(2 + 2 * t, 0, 1)
                    return 0

                lax.fori_loop(0, pairs, two_steps, 0)
                step(first, 1, 0, keys <= queries)
                step(first + 1, 0, None, keys + tb <= queries)
            dq0, dq1 = dq_scr[0, :, 0:tq], dq_scr[1, :, 0:tq]
            dq_ref[pl.ds(q0, tq), :] = (_pair_lanes(dq0.T, dq1.T) * 0.125).astype(BF16)
            dc_ref[pl.ds(q0, tq), :] = lanes01(dq0[K_ONES:K_ONES + 1, :], dq1[K_ONES:K_ONES + 1, :])

        q_tile(0, tb, None)

        def big_tile(i, _):
            q_tile(pl.multiple_of(tb + i * tq_big, tb), tq_big, i)
            return 0

        lax.fori_loop(0, n_big, big_tile, 0)

        def finish(bi, _):
            r0 = pl.multiple_of(bi * tb, tb)
            dk0, dk1 = dk_acc[0, :, pl.ds(r0, tb)], dk_acc[1, :, pl.ds(r0, tb)]
            dk_ref[pl.ds(r0, tb), :] = _pair_lanes(dk0.T, dk1.T).astype(BF16)
            dv_ref[pl.ds(r0, tb), :] = _pair_lanes(dv_acc[0, :, pl.ds(r0, tb)].T, dv_acc[1, :, pl.ds(r0, tb)].T).astype(BF16)
            dc_ref[pl.ds(r0, tb), :] = dc_ref[pl.ds(r0, tb), :] - lanes01(dk0[Q_ONES:Q_ONES + 1, :], dk1[Q_ONES:Q_ONES + 1, :])
            return 0

        lax.fori_loop(0, nb, finish, 0)
        finish_exchange()

    once = pl.Buffered(1)
    pair = pl.BlockSpec((lp, LANES), lambda hp: (0, hp))
    pair_in = pl.BlockSpec((lp, LANES), lambda hp: (0, hp), pipeline_mode=once)
    heads = pl.BlockSpec((2, lp, LANES), lambda hp: (hp, 0, 0), pipeline_mode=once)
    heads_t = pl.BlockSpec((2, LANES, lp), lambda hp: (hp, 0, 0), pipeline_mode=once)
    hbm = pl.BlockSpec(memory_space=pl.ANY)
    wide = jax.ShapeDtypeStruct((lp, ATTN_WIDTH), BF16)
    return pl.pallas_call(
        body, name="attention_backward", grid=(N_HEADS // 2,),
        out_shape=[wide, wide, wide, jax.ShapeDtypeStruct((lp, ATTN_WIDTH), F32)] + _exchange_results(behind),
        in_specs=[heads_t, heads, heads_t, heads, pair_in, pair_in, pair_in] + [hbm] * nx,
        out_specs=[pair, pair, pair, pair] + [hbm] * nx,
        scratch_shapes=[pltpu.VMEM((2, LANES, lp), BF16), pltpu.VMEM((2, LANES, lp), BF16),
                        pltpu.VMEM((2, LANES, lp), F32), pltpu.VMEM((2, LANES, lp), F32),
                        pltpu.VMEM((2, LANES, tq_big), F32), pltpu.VMEM((2, 2, tb, tq_big), F32)]
        + _exchange_semaphores(nx),
        compiler_params=_params(("arbitrary",)),
    )(qt, k, kt, v, do, o, lse, *[a for _, a, _ in behind])


def _middle(x, target, h, o, a_pool, u, zp, w_main, w_up_pool, w_up_attn, w_out, pool_w, pool_scale, final_g):
    seq = x.shape[0]
    tm = ROW_TILE
    nt = seq // tm + 1
    lp = nt * tm
    halo_blocks = tm // MAX_WINDOW

    def body(x_ref, t_ref, h_ref, o_ref, ap_ref, u_ref, uh_ref, zp_ref,
             wc_ref, wupp_ref, wupa_ref, wout_ref, pw_ref, sc_ref, gf_ref,
             dh2_ref, mg_ref, yp_ref, ya_ref, dap_ref, daa_ref, do_ref, dza_ref, dgp_ref, dga_ref, dzp_ref, dpn_ref,
             loss_ref, dgf_ref, dsc_ref, dpw_ref):
        i = pl.program_id(0)
        tiles = (dh2_ref, mg_ref, yp_ref, ya_ref, dap_ref, daa_ref, do_ref, dza_ref, dgp_ref, dga_ref, dzp_ref, dpn_ref)

        @pl.when(i == 0)
        def _():
            for ref in tiles + (loss_ref, dgf_ref, dsc_ref, dpw_ref):
                ref[...] = jnp.zeros_like(ref)

        @pl.when(i > 0)
        def _():
            xt = x_ref[...]
            hb = h_ref[...]
            pc = _dot_nt(hb, wc_ref[...])
            za, gp, ga = pc[:, :512], pc[:, 512:1536], pc[:, 1536:]
            of = o_ref[...].astype(F32)
            sza = _sigmoid(za)
            silu_za = za * sza
            ya = (of * silu_za).astype(BF16)
            ya_ref[...] = ya
            aa = _dot(ya, wupa_ref[...])
            ap = ap_ref[...].astype(F32)
            sgp, sga = _sigmoid(gp), _sigmoid(ga)
            mg = (sgp * ap + sga * aa).astype(BF16)
            mg_ref[...] = mg
            h2 = xt + _dot(mg, wout_ref[...])
            r2 = lax.rsqrt(jnp.mean(h2 * h2, axis=-1, keepdims=True) + RMS_EPS)
            h2n = h2 * r2
            gf = gf_ref[...]
            diff = h2n * gf - t_ref[...]
            loss_ref[...] += 0.5 * jnp.sum(jnp.mean(diff * diff, axis=-1, keepdims=True), axis=0, keepdims=True)
            dy = diff * (1.0 / D_MODEL)
            dgf_ref[...] += jnp.sum(dy * h2n, axis=0, keepdims=True)
            dyg = dy * gf
            dh2 = r2 * (dyg - h2n * jnp.mean(dyg * h2n, axis=-1, keepdims=True))
            dh2_ref[...] = dh2
            dmg = _dot_nt(dh2.astype(BF16), wout_ref[...])
            dap = (dmg * sgp).astype(BF16)
            daa = (dmg * sga).astype(BF16)
            dap_ref[...] = dap
            daa_ref[...] = daa
            dgp_ref[...] = (dmg * ap * sgp * (1.0 - sgp)).astype(BF16)
            dga_ref[...] = (dmg * aa * sga * (1.0 - sga)).astype(BF16)
            dyp = _dot_nt(dap, wupp_ref[...])
            dya = _dot_nt(daa, wupa_ref[...])
            do_ref[...] = (dya * silu_za).astype(BF16)
            dza_ref[...] = (dya * of * (sza * (1.0 + za * (1.0 - sza)))).astype(BF16)

            u = u_ref[...]
            zp = zp_ref[...]
            counts = _pool_counts(i * tm, tm)
            ps = _pool_means(jnp.concatenate([uh_ref[...], u], axis=0), u, counts)
            pbs = [p.astype(BF16) for p in ps]
            ppw = jnp.concatenate([_dot(pbs[g], pw_ref[g]) for g in range(4)], axis=1)
            sc = sc_ref[...]
            szp = _sigmoid(zp)
            silu_zp = zp * szp
            ypre = ppw * sc
            yp_ref[...] = (ypre * silu_zp).astype(BF16)
            dypre = dyp * silu_zp
            dzp_ref[...] = (dyp * ypre * (szp * (1.0 + zp * (1.0 - szp)))).astype(BF16)
            dsc_ref[...] += jnp.sum(dypre * ppw, axis=0, keepdims=True)
            dppw = (dypre * sc).astype(BF16)
            dpns = []
            for g in range(4):
                dg = dppw[:, POOL_GROUP * g:POOL_GROUP * (g + 1)]
                dpw_ref[g] += _dot_tn(pbs[g], dg)
                dpns.append(_dot_nt(dg, pw_ref[g]) / counts[g])
            dpn_ref[...] = jnp.concatenate(dpns, axis=1)

    real = lambda w: pl.BlockSpec((tm, w), lambda i: (jnp.maximum(i - 1, 0), 0))
    row = lambda w: pl.BlockSpec((tm, w), lambda i: (i, 0))
    in_specs = [
        real(D_MODEL), real(D_MODEL), row(D_MODEL), row(512), row(D_MODEL), row(512),
        pl.BlockSpec((MAX_WINDOW, 512), lambda i: (jnp.maximum(i * halo_blocks - 1, 0), 0)), row(512),
        _const((2560, D_MODEL), (1, 0)), _const((POOL_WIDTH, D_MODEL)), _const((ATTN_WIDTH, D_MODEL)),
        _const((D_MODEL, D_MODEL)), _const((4, POOL_GROUP, POOL_GROUP)), _const((1, POOL_WIDTH)), _const((1, D_MODEL)),
    ]
    sd = jax.ShapeDtypeStruct
    out_shape = [
        sd((lp, D_MODEL), F32),
        sd((lp, D_MODEL), BF16),
        sd((lp, 512), BF16),
        sd((lp, 512), BF16),
        sd((lp, D_MODEL), BF16),
        sd((lp, D_MODEL), BF16),
        sd((lp, 512), BF16),
        sd((lp, 512), BF16),
        sd((lp, D_MODEL), BF16),
        sd((lp, D_MODEL), BF16),
        sd((lp, 512), BF16),
        sd((lp, 512), F32),
        sd((1, 1), F32),
        sd((1, D_MODEL), F32),
        sd((1, 512), F32),
        sd((4, POOL_GROUP, POOL_GROUP), F32),
    ]
    keep = lambda shape: pl.BlockSpec(shape, lambda i: (0,) * len(shape))
    out_specs = [row(D_MODEL), row(D_MODEL), row(512), row(512), row(D_MODEL), row(D_MODEL), row(512), row(512),
                 row(D_MODEL), row(D_MODEL), row(512), row(512),
                 keep((1, 1)), keep((1, D_MODEL)), keep((1, 512)), keep((4, POOL_GROUP, POOL_GROUP))]
    return pl.pallas_call(
        body, name="middle", grid=(nt,), out_shape=out_shape, in_specs=in_specs, out_specs=out_specs,
        compiler_params=_params(("arbitrary",)),
    )(x, target, h, o, a_pool, u, u, zp, w_main, w_up_pool, w_up_attn, w_out, pool_w, pool_scale, final_g)


def _backward_in(x, tile0, norm_g, dh2, dpn, dzp, dq, dk, dv, dza, dgp, dga, dc, sneg, w_main, w_f):
    seq = x.shape[0]
    tm = ROW_TILE
    nt = seq // tm + 1
    lp = nt * tm
    halo_blocks = tm // MAX_WINDOW
    last_halo = lp // MAX_WINDOW - 1

    def body(x_ref, t0_ref, g_ref, dh2_ref, dpn_ref, dpnh_ref, dzp_ref, dq_ref, dk_ref, dv_ref, dza_ref,
             dgp_ref, dga_ref, dc_ref, sn_ref, wm_ref, wf_ref,
             dproj_ref, df_ref, gx_ref, gmeta_ref, dg_ref, dbf_ref, carry_ref):
        i = pl.program_id(0)
        t = nt - 1 - i

        @pl.when(i == 0)
        def _():
            carry_ref[...] = jnp.zeros_like(carry_ref)
            dg_ref[...] = jnp.zeros_like(dg_ref)
            dbf_ref[...] = jnp.zeros_like(dbf_ref)

        dpn_t = dpn_ref[...]
        ahead = jnp.where(i == 0, jnp.zeros_like(dpnh_ref), dpnh_ref[...])
        ext = jnp.concatenate([dpn_t, ahead], axis=0)
        counts = _pool_counts(t * tm, tm)
        for g, w in enumerate(POOL_WINDOWS):
            s = ext[:, POOL_GROUP * g:POOL_GROUP * (g + 1)]
            sh = 1
            while sh < w:
                s = s + pltpu.roll(s, tm + MAX_WINDOW - sh, axis=0)
                sh *= 2
            du = s[:tm, :] - dpn_t[:, POOL_GROUP * g:POOL_GROUP * (g + 1)] * counts[g]
            dproj_ref[:, POOL_GROUP * g:POOL_GROUP * (g + 1)] = du.astype(BF16)
        dproj_ref[:, 512:1024] = dzp_ref[...]
        dproj_ref[:, 1024:1536] = dq_ref[...]
        dproj_ref[:, 1536:2048] = dk_ref[...]
        dproj_ref[:, 2048:2560] = dv_ref[...]
        dproj_ref[:, 2560:3072] = dza_ref[...]
        dproj_ref[:, 3072:4096] = dgp_ref[...]
        dproj_ref[:, 4096:5120] = dga_ref[...]

        dct = dc_ref[:, 0:LANES]
        for hp in range(1, N_HEADS // 2):
            dct = dct + pltpu.roll(dc_ref[:, LANES * hp:LANES * (hp + 1)], 2 * hp, axis=1)
        rloc = lax.broadcasted_iota(jnp.int32, (tm, LANES), 0)
        sh = 1
        while sh < tm:
            dct = dct + jnp.where(rloc + sh < tm, pltpu.roll(dct, tm - sh, axis=0), 0.0)
            sh *= 2
        dct = dct + carry_ref[...]
        carry_ref[...] = dct[0:1, :]
        df = dct * sn_ref[...]
        dbf_ref[...] += jnp.sum(df, axis=0, keepdims=True)
        dfb = df.astype(BF16)
        df_ref[...] = dfb

        dh = _dot(dproj_ref[...], wm_ref[...]) + _dot(dfb, wf_ref[...])
        xt = jnp.where(t == 0, t0_ref[...], x_ref[...])
        r = lax.rsqrt(jnp.mean(xt * xt, axis=-1, keepdims=True) + RMS_EPS)
        xn = xt * r
        dg_ref[...] += jnp.sum(dh * xn, axis=0, keepdims=True)
        dhg = dh * g_ref[...]
        dx = dh2_ref[...] + r * (dhg - xn * jnp.mean(dhg * xn, axis=-1, keepdims=True))

        @pl.when(t > 0)
        def _():
            gx_ref[...] = dx

        @pl.when(t == 0)
        def _():
            gmeta_ref[...] = dx[PAD:, :]

    rev = lambda w: pl.BlockSpec((tm, w), lambda i: (nt - 1 - i, 0))
    real = pl.BlockSpec((tm, D_MODEL), lambda i: (jnp.maximum(nt - 2 - i, 0), 0))
    in_specs = [
        real, _const((tm, D_MODEL)), _const((1, D_MODEL)), rev(D_MODEL), rev(512),
        pl.BlockSpec((MAX_WINDOW, 512), lambda i: (jnp.minimum((nt - i) * halo_blocks, last_halo), 0)),
        rev(512), rev(512), rev(512), rev(512), rev(512), rev(D_MODEL), rev(D_MODEL),
        rev(512), rev(LANES),
        _const((N_MAIN, D_MODEL)), _const((LANES, D_MODEL)),
    ]
    sd = jax.ShapeDtypeStruct
    out_shape = [sd((lp, N_MAIN), BF16), sd((lp, LANES), BF16), sd((seq, D_MODEL), F32), sd((N_META, D_MODEL), F32),
                 sd((1, D_MODEL), F32), sd((1, LANES), F32)]
    keep = lambda shape: pl.BlockSpec(shape, lambda i: (0,) * len(shape))
    out_specs = [rev(N_MAIN), rev(LANES), real, keep((N_META, D_MODEL)), keep((1, D_MODEL)), keep((1, LANES))]
    return pl.pallas_call(
        body, name="backward_in", grid=(nt,), out_shape=out_shape, in_specs=in_specs, out_specs=out_specs,
        scratch_shapes=[pltpu.VMEM((1, LANES), F32)],
        compiler_params=_params(("arbitrary",)),
    )(x, tile0, norm_g, dh2, dpn, dpn, dzp, dq, dk, dv, dza, dgp, dga, dc, sneg, w_main, w_f)


def _matmul_tn(name, a, b, tn):
    lp, m = a.shape
    n = b.shape[1]

    def body(a_ref, b_ref, c_ref):
        c_ref[...] = _dot_tn(a_ref[...].astype(BF16), b_ref[...].astype(BF16))

    return pl.pallas_call(
        body, name=name, grid=(n // tn,), out_shape=jax.ShapeDtypeStruct((m, n), F32),
        in_specs=[_const((lp, m)), pl.BlockSpec((lp, tn), lambda j: (0, j))],
        out_specs=pl.BlockSpec((m, tn), lambda j: (0, j)),
        compiler_params=_params(("arbitrary",)),
    )(a, b)


def _matmul_tn_rows(name, a, b, tm):
    lp, m = a.shape
    n = b.shape[1]

    def body(a_ref, b_ref, c_ref):
        c_ref[...] = _dot_tn(a_ref[...].astype(BF16), b_ref[...].astype(BF16))

    return pl.pallas_call(
        body, name=name, grid=(m // tm,), out_shape=jax.ShapeDtypeStruct((m, n), F32),
        in_specs=[pl.BlockSpec((lp, tm), lambda j: (0, j)), _const((lp, n))],
        out_specs=pl.BlockSpec((tm, n), lambda j: (j, 0)),
        compiler_params=_params(("arbitrary",)),
    )(a, b)


def _adamw(name, parts, w, m, v, rows, cols=None):
    r, c_all = w.shape
    c = cols or c_all
    n_parts = parts.shape[0]

    def body(p_ref, w_ref, m_ref, v_ref, g_ref, d_ref, mo_ref, vo_ref):
        g = p_ref[0].astype(F32)
        for s in range(1, n_parts):
            g = g + p_ref[s].astype(F32)
        m_new = ADAM_B1 * m_ref[...] + (1.0 - ADAM_B1) * g
        v_new = ADAM_B2 * v_ref[...] + (1.0 - ADAM_B2) * (g * g)
        m_hat = m_new / (1.0 - ADAM_B1 ** ADAM_STEP)
        v_hat = v_new / (1.0 - ADAM_B2 ** ADAM_STEP)
        g_ref[...] = g
        d_ref[...] = -ADAM_LR * (m_hat / (jnp.sqrt(v_hat) + ADAM_EPS) + ADAM_WD * w_ref[...])
        mo_ref[...] = m_new
        vo_ref[...] = v_new

    blk = pl.BlockSpec((rows, c), lambda i, j: (i, j))
    return pl.pallas_call(
        body, name=name, grid=(r // rows, c_all // c), out_shape=[jax.ShapeDtypeStruct((r, c_all), F32)] * 4,
        in_specs=[pl.BlockSpec((n_parts, rows, c), lambda i, j: (0, i, j)), blk, blk, blk],
        out_specs=[blk] * 4,
        compiler_params=_params(("arbitrary", "arbitrary")),
    )(parts, w, m, v)


def _pair_sum(name, mine, theirs, rows):
    n, r, c = mine.shape

    def body(a_ref, b_ref, o_ref):
        o_ref[...] = (a_ref[...] + b_ref[...].astype(F32)).astype(BF16)

    blk = pl.BlockSpec((1, rows, c), lambda j, i: (j, i, 0))
    return pl.pallas_call(
        body, name=name, grid=(n, r // rows), out_shape=jax.ShapeDtypeStruct((n, r, c), BF16),
        in_specs=[blk, blk], out_specs=blk,
        compiler_params=_params(("arbitrary", "arbitrary")),
    )(mine, theirs)


def _columns_to_slots(a):
    r, c8 = a.shape
    return a.reshape(r, N_DEV, c8 // N_DEV).transpose(1, 0, 2)


def _by_core(slots):
    by_core = slots.reshape((4, 2) + slots.shape[1:]).swapaxes(0, 1)
    c = lax.axis_index("c")
    return (lax.dynamic_index_in_dim(by_core, c, 0, keepdims=False),
            lax.dynamic_index_in_dim(by_core, 1 - c, 0, keepdims=False).astype(BF16))


def _slots_to_columns(a):
    n, r, c = a.shape
    return a.transpose(1, 0, 2).reshape(r, n * c)


def kernel(x, meta_tokens, norm_g, w_in, b_forget, pool_w, pool_scale, w_up_pool, w_up_attn, w_out, final_norm_g, loss_target, m_meta_tokens, m_norm_g, m_w_in, m_b_forget, m_pool_w, m_pool_scale, m_w_up_pool, m_w_up_attn, m_w_out, m_final_norm_g, v_meta_tokens, v_norm_g, v_w_in, v_b_forget, v_pool_w, v_pool_scale, v_w_up_pool, v_w_up_attn, v_w_out, v_final_norm_g):
    xs = x[0]
    target = loss_target[0]

    g_in, g_upp, g_meta = _gather_two_level(
        "gather_weights", [w_in[0].T.astype(BF16), w_up_pool[0].astype(BF16), meta_tokens])
    w_full = g_in.reshape(N_DEV * g_in.shape[1], D_MODEL)
    w_main = jnp.concatenate([w_full[:N_BEFORE_F], w_full[N_BEFORE_F + N_HEADS:]], axis=0)
    w_f = jnp.pad(w_full[N_BEFORE_F:N_BEFORE_F + N_HEADS], ((0, LANES - N_HEADS), (0, 0)))
    wupp = _slots_to_columns(g_upp)
    meta = _slots_to_columns(g_meta)
    tile0 = jnp.concatenate([jnp.zeros((PAD, D_MODEL), F32), meta], axis=0)
    b_f = jnp.pad(b_forget, ((0, 0), (0, LANES - N_HEADS)))
    pw_b = pool_w[0].astype(BF16)
    final_g = final_norm_g.reshape(1, D_MODEL)

    (h, u, zp, k, v, qt, kt, vt, sneg, a_pool) = _forward_in(xs, tile0, norm_g, w_main, w_f, b_f, pw_b,
                                                              pool_scale, wupp)
    o, lse, g_upa, g_out = _attention_forward(
        qt, k, vt, [("gather", w_up_attn[0].astype(BF16), ALL_PEERS), ("gather", w_out[0].astype(BF16), ALL_PEERS)])
    wupa = _slots_to_columns(g_upa)
    wout = g_out.reshape(D_MODEL, D_MODEL)
    (dh2, mg, yp, ya, dap, daa, do, dza, dgp, dga, dzp, dpn,
     loss_part, d_final_g, d_scale, d_pool_w) = _middle(xs, target, h, o, a_pool, u, zp, w_main, wupp, wupa, wout,
                                                        pw_b, pool_scale, final_g)
    dw_out = _matmul_tn("grad_w_out", mg, dh2, 256)
    dw_upp = _matmul_tn("grad_w_up_pool", yp, dap, 512)
    dw_upa = _matmul_tn("grad_w_up_attn", ya, daa, 512)
    dq, dk, dv, dc, p_upp, p_upa, p_out, p_pool_w, p_scale, p_final_g = _attention_backward(
        qt, k, kt, v, do, o, lse,
        [("scatter", _columns_to_slots(dw_upp).astype(BF16), ALL_PEERS),
         ("scatter", _columns_to_slots(dw_upa).astype(BF16), ALL_PEERS),
         ("scatter", dw_out.reshape(N_DEV, D_MODEL // N_DEV, D_MODEL).astype(BF16), ALL_PEERS),
         ("gather", d_pool_w.reshape(4 * POOL_GROUP, POOL_GROUP), ALL_PEERS),
         ("gather", d_scale, ALL_PEERS), ("gather", d_final_g, ALL_PEERS)])
    dproj, df, grad_x, d_meta, d_norm_g, d_bf = _backward_in(xs, tile0, norm_g, dh2, dpn, dzp, dq, dk, dv, dza,
                                                             dgp, dga, dc, sneg, w_main, w_f)
    dw_main = _matmul_tn_rows("grad_w_in", dproj, h, 512)
    dw_f = _matmul_tn_rows("grad_w_forget", df, h, LANES)
    dw_in = jnp.concatenate([dw_main[:N_BEFORE_F], dw_f[:N_HEADS], dw_main[N_BEFORE_F:]], axis=0)
    dw_in = dw_in.reshape(N_DEV, dw_in.shape[0] // N_DEV, D_MODEL)

    mine, for_sibling = _by_core(dw_in)
    from_sibling, = _exchange("swap_with_sibling", [("swap", for_sibling, (SIBLING,))])
    pair_sums = _pair_sum("pair_sum", mine, from_sibling, dw_in.shape[1])
    p_in, p_meta, p_norm_g, p_bf = _exchange(
        "exchange_gradients",
        [("chips", pair_sums, SAME_CORE), ("scatter", _columns_to_slots(d_meta), ALL_PEERS),
         ("gather", d_norm_g, ALL_PEERS), ("gather", d_bf, ALL_PEERS)])

    loss = lax.psum(loss_part[0, 0], ("x", "y", "c"))

    def pad_f(a):
        return jnp.pad(a, ((0, 0), (0, LANES - N_HEADS)))

    res = {}
    res["meta_tokens"] = _adamw("adamw_meta", p_meta, meta_tokens, m_meta_tokens, v_meta_tokens, N_META)
    res["norm_g"] = _adamw("adamw_norm_g", p_norm_g, norm_g, m_norm_g, v_norm_g, 1)
    res["w_in"] = [a.T for a in _adamw("adamw_w_in", p_in, w_in[0].T, m_w_in[0].T, v_w_in[0].T, p_in.shape[1], 256)]
    bf = _adamw("adamw_b_forget", p_bf, pad_f(b_forget), pad_f(m_b_forget), pad_f(v_b_forget), 1)
    res["b_forget"] = [a[:, :N_HEADS] for a in bf]
    pw = _adamw("adamw_pool_w", p_pool_w, pool_w.reshape(512, 128), m_pool_w.reshape(512, 128),
                v_pool_w.reshape(512, 128), 512)
    res["pool_w"] = [a.reshape(pool_w.shape) for a in pw]
    res["pool_scale"] = _adamw("adamw_pool_scale", p_scale, pool_scale, m_pool_scale, v_pool_scale, 1)
    res["w_up_pool"] = _adamw("adamw_w_up_pool", p_upp, w_up_pool[0], m_w_up_pool[0], v_w_up_pool[0], 512)
    res["w_up_attn"] = _adamw("adamw_w_up_attn", p_upa, w_up_attn[0], m_w_up_attn[0], v_w_up_attn[0], 512)
    res["w_out"] = _adamw("adamw_w_out", p_out, w_out[0], m_w_out[0], v_w_out[0], 128)
    fg = _adamw("adamw_final_norm_g", p_final_g, final_g, m_final_norm_g.reshape(1, D_MODEL),
                v_final_norm_g.reshape(1, D_MODEL), 1)
    res["final_norm_g"] = [a.reshape(D_MODEL) for a in fg]
    for name in ("w_in", "w_up_pool", "w_up_attn", "w_out"):
        res[name] = [a[None] for a in res[name]]

    order = ["meta_tokens", "norm_g", "w_in", "b_forget", "pool_w", "pool_scale", "w_up_pool", "w_up_attn", "w_out",
             "final_norm_g"]
    outs = [loss, grad_x[None]]
    for part in range(4):
        outs += [res[name][part] for name in order]
    return tuple(outs)
```

```python
import functools

import jax
import jax.numpy as jnp
from jax import lax
from jax.experimental import pallas as pl
from jax.experimental.pallas import tpu as pltpu

F32 = jnp.float32
BF16 = jnp.bfloat16

D_MODEL = 1024
N_META = 16
POOL_WIDTH = 512
ATTN_WIDTH = 512
N_HEADS = 8
HEAD_DIM = 64
POOL_WINDOWS = (2, 4, 8, 16)
POOL_GROUP = 128
MAX_WINDOW = 16
RMS_EPS = 1e-6
N_MAIN = 5120
N_BEFORE_F = 3072
N_DEV = 8
LANES = 128

ROW_TILE = 256
ATT_TILE = 256
ATT_Q_BLOCKS = 2
PAD = ROW_TILE - N_META
VMEM_LIMIT = 56 * 1024 * 1024

ADAM_LR = 0.001
ADAM_B1 = 0.9
ADAM_B2 = 0.999
ADAM_EPS = 1e-08
ADAM_WD = 0.01
ADAM_STEP = 10

NEG = -1e30
MESH = pl.DeviceIdType.MESH


def _params(sem=None):
    kw = dict(vmem_limit_bytes=VMEM_LIMIT)
    if sem is not None:
        kw["dimension_semantics"] = sem
    return pltpu.CompilerParams(**kw)


def _const(shape, block_index=None):
    idx = block_index or (0,) * len(shape)
    return pl.BlockSpec(shape, lambda i: idx, pipeline_mode=pl.Buffered(1))


def _sigmoid(x):
    return jax.nn.sigmoid(x)


def _dot(a, b):
    return jnp.dot(a, b, preferred_element_type=F32)


def _dot_nt(a, b):
    return lax.dot_general(a, b, (((1,), (1,)), ((), ())), preferred_element_type=F32)


def _dot_tn(a, b):
    return lax.dot_general(a, b, (((0,), (0,)), ((), ())), preferred_element_type=F32)


def _pool_counts(first_row, rows):
    row = first_row + lax.broadcasted_iota(jnp.int32, (rows, 1), 0)
    pos1 = row - PAD + 1
    return [jnp.clip(pos1, 1, w).astype(F32) for w in POOL_WINDOWS]


def _pool_means(u_ext, u, counts):
    rows = u.shape[0]
    out = []
    for g, w in enumerate(POOL_WINDOWS):
        s = u_ext[:, POOL_GROUP * g:POOL_GROUP * (g + 1)]
        sh = 1
        while sh < w:
            s = s + pltpu.roll(s, sh, axis=0)
            sh *= 2
        out.append(s[MAX_WINDOW:MAX_WINDOW + rows, :] / counts[g] - u[:, POOL_GROUP * g:POOL_GROUP * (g + 1)])
    return out


Q_BIAS, Q_ONES, Q_LSE = 64, 67, 70
K_ONES, K_BIAS, K_ONES2 = 64, 67, 70
V_ONES = 64
DO_BIAS = 64


def _lane_ones(lane, ranges):
    hit = None
    for lo, hi in ranges:
        r = (lane >= lo) & (lane < hi)
        hit = r if hit is None else hit | r
    return jnp.where(hit, 1.0, 0.0)


def _put3(base, lane, first, x):
    hi = x.astype(BF16).astype(F32)
    rest = x - hi
    mid = rest.astype(BF16).astype(F32)
    lo = (rest - mid).astype(BF16).astype(F32)
    for j, piece in enumerate((hi, mid, lo)):
        base = jnp.where(lane == first + j, piece, base)
    return base


SIBLING = 1
SAME_CORE = (2, 4, 6)
ALL_PEERS = (1, 2, 3, 4, 5, 6, 7)


def _place():
    return lax.axis_index("x"), lax.axis_index("y"), lax.axis_index("c")


def _peer(r):
    x, y, c = _place()
    return (1 - x if r & 4 else x, 1 - y if r & 2 else y, 1 - c if r & 1 else c)


def _device_slot(p):
    return 4 * p[0] + 2 * p[1] + p[2]


def _chip_slot(p):
    return 2 * p[0] + p[1]


def _exchange(name, items):
    n = len(items)

    def body(*refs):
        copies = _exchange_copies(items, refs[:n], refs[n:2 * n], *refs[2 * n:])
        for cp in copies:
            cp.start()
        for cp in copies:
            cp.wait()

    hbm = pl.BlockSpec(memory_space=pl.ANY)
    return pl.pallas_call(
        body, name=name, out_shape=_exchange_results(items),
        in_specs=[hbm] * n, out_specs=[hbm] * n,
        scratch_shapes=_exchange_semaphores(n),
    )(*[a for _, a, _ in items])


def _exchange_results(items):
    return [jax.ShapeDtypeStruct(((N_DEV,) if kind == "gather" else ()) + a.shape, a.dtype) for kind, a, _ in items]


def _exchange_semaphores(n):
    return [pltpu.SemaphoreType.DMA((n, N_DEV - 1)), pltpu.SemaphoreType.DMA((n, N_DEV - 1)),
            pltpu.SemaphoreType.DMA((n,))]


def _exchange_copies(items, ins, outs, send_sems, recv_sems, local_sems):
    me = _place()
    copies = []
    for a, (kind, _, peers) in enumerate(items):
        slot = _chip_slot if kind == "chips" else _device_slot
        for r in peers:
            peer = _peer(r)
            src = ins[a] if kind in ("swap", "gather") else ins[a].at[slot(peer)]
            dst = outs[a] if kind == "swap" else outs[a].at[slot(me)]
            copies.append(pltpu.make_async_remote_copy(
                src_ref=src, dst_ref=dst, send_sem=send_sems.at[a, r - 1], recv_sem=recv_sems.at[a, r - 1],
                device_id=peer, device_id_type=MESH))
        if kind != "swap":
            src = ins[a] if kind == "gather" else ins[a].at[slot(me)]
            copies.append(pltpu.make_async_copy(src, outs[a].at[slot(me)], local_sems.at[a]))
    return copies


def _gather_two_level(name, arrays):
    n = len(arrays)

    def body(*refs):
        ins, outs = refs[:n], refs[n:2 * n]
        send_sems, recv_sems, local_sems = refs[2 * n:]
        me = _place()
        sibling = _peer(SIBLING)

        def copy(a, k, block, to, src=None):
            rows = outs[a].at[_device_slot(block)]
            return pltpu.make_async_remote_copy(
                src_ref=rows if src is None else src, dst_ref=rows,
                send_sem=send_sems.at[a, k], recv_sem=recv_sems.at[a, k], device_id=to, device_id_type=MESH)

        sends, own = [], []
        for a in range(n):
            mine = pltpu.make_async_copy(ins[a], outs[a].at[_device_slot(me)], local_sems.at[a])
            mine.start()
            own.append(mine)
            for k, r in enumerate((SIBLING,) + SAME_CORE):
                cp = copy(a, k, me, _peer(r), src=ins[a])
                cp.start()
                sends.append(cp)
        for a in range(n):
            for j, r in enumerate(SAME_CORE):
                copy(a, 1 + j, _peer(r), me).wait_recv()
                passed = copy(a, 4 + j, _peer(r), sibling)
                passed.start()
                sends.append(passed)
        for a in range(n):
            copy(a, 0, sibling, me).wait_recv()
            for j, r in enumerate(SAME_CORE):
                copy(a, 4 + j, _peer(r | SIBLING), me).wait_recv()
        for cp in sends:
            cp.wait_send()
        for cp in own:
            cp.wait()

    hbm = pl.BlockSpec(memory_space=pl.ANY)
    return pl.pallas_call(
        body, name=name, out_shape=[jax.ShapeDtypeStruct((N_DEV,) + a.shape, a.dtype) for a in arrays],
        in_specs=[hbm] * n, out_specs=[hbm] * n,
        scratch_shapes=[pltpu.SemaphoreType.DMA((n, N_DEV - 1)), pltpu.SemaphoreType.DMA((n, N_DEV - 1)),
                        pltpu.SemaphoreType.DMA((n,))],
    )(*arrays)


def _forward_in(x, tile0, norm_g, w_main, w_f, b_f, pool_w, pool_scale, w_up_pool):
    seq = x.shape[0]
    nt = seq // ROW_TILE + 1
    lp = nt * ROW_TILE
    tm = ROW_TILE

    def body(x_ref, t0_ref, g_ref, wa_ref, wf_ref, bf_ref, pw_ref, sc_ref, wup_ref,
             h_ref, u_ref, zp_ref, k_ref, v_ref, qt_ref, kt_ref, vt_ref, sn_ref, ap_ref,
             uext_ref, carry_ref):
        i = pl.program_id(0)

        @pl.when(i == 0)
        def _():
            uext_ref[...] = jnp.zeros_like(uext_ref)
            carry_ref[...] = jnp.zeros_like(carry_ref)

        xt = jnp.where(i == 0, t0_ref[...], x_ref[...])
        r = lax.rsqrt(jnp.mean(xt * xt, axis=-1, keepdims=True) + RMS_EPS)
        h = (xt * r * g_ref[...]).astype(BF16)
        h_ref[...] = h
        pa = _dot_nt(h, wa_ref[...])
        u = pa[:, :512]
        zp = pa[:, 512:1024]
        u_ref[...] = u
        zp_ref[...] = zp

        uext_ref[0:MAX_WINDOW, :] = uext_ref[tm:tm + MAX_WINDOW, :]
        uext_ref[MAX_WINDOW:, :] = u
        counts = _pool_counts(i * tm, tm)
        ps = _pool_means(uext_ref[...], u, counts)
        ppw = jnp.concatenate([_dot(ps[g].astype(BF16), pw_ref[g]) for g in range(4)], axis=1)
        y_pool = ppw * sc_ref[...] * (zp * _sigmoid(zp))
        ap_ref[...] = _dot(y_pool.astype(BF16), wup_ref[...]).astype(BF16)

        fl = _dot_nt(h, wf_ref[...]) + bf_ref[...]
        row = i * tm + lax.broadcasted_iota(jnp.int32, (tm, LANES), 0)
        rloc = lax.broadcasted_iota(jnp.int32, (tm, LANES), 0)
        lane = lax.broadcasted_iota(jnp.int32, (tm, LANES), 1)
        live = (row >= PAD) & (lane < N_HEADS)
        logf = jnp.minimum(fl, 0.0) - jnp.log1p(jnp.exp(-jnp.abs(fl)))
        cs = jnp.where(live, logf, 0.0)
        sh = 1
        while sh < tm:
            cs = cs + jnp.where(rloc >= sh, pltpu.roll(cs, sh, axis=0), 0.0)
            sh *= 2
        cs = cs + carry_ref[...]
        carry_ref[...] = cs[tm - 1:tm, :]
        sn_ref[...] = jnp.where(live, _sigmoid(-fl), 0.0)

        rows1 = i * tm + lax.broadcasted_iota(jnp.int32, (tm, 1), 0)
        ones_q = _lane_ones(lane, ((Q_ONES, Q_ONES + 3),))
        ones_k = _lane_ones(lane, ((K_ONES, K_ONES + 3), (K_ONES2, K_ONES2 + 3)))
        ones_v = _lane_ones(lane, ((V_ONES, V_ONES + 3),))
        for hp in range(N_HEADS // 2):
            qp = pa[:, 1024 + LANES * hp:1024 + LANES * (hp + 1)] * 0.125
            kp = pa[:, 1536 + LANES * hp:1536 + LANES * (hp + 1)]
            vp = pa[:, 2048 + LANES * hp:2048 + LANES * (hp + 1)]
            for e in range(2):
                head = 2 * hp + e
                if e:
                    qp, kp, vp = (pltpu.roll(a, HEAD_DIM, axis=1) for a in (qp, kp, vp))
                c_h = cs[:, head:head + 1]
                q_h = jnp.where(lane < HEAD_DIM, qp, _put3(ones_q, lane, Q_BIAS, c_h))
                qt_ref[head] = q_h.T.astype(BF16)
                minus_ck = jnp.where(rows1 >= PAD, -c_h, NEG)
                k_h = jnp.where(lane < HEAD_DIM, kp, _put3(ones_k, lane, K_BIAS, minus_ck))
                k_ref[head] = k_h.astype(BF16)
                kt_ref[head] = k_h.T.astype(BF16)
                v_h = jnp.where(lane < HEAD_DIM, vp, ones_v)
                v_ref[head] = v_h.astype(BF16)
                vt_ref[head] = v_h.T.astype(BF16)

    row_f32 = lambda w: pl.BlockSpec((tm, w), lambda i: (i, 0))
    out_shape = [
        jax.ShapeDtypeStruct((lp, D_MODEL), BF16),
        jax.ShapeDtypeStruct((lp, POOL_WIDTH), F32),
        jax.ShapeDtypeStruct((lp, POOL_WIDTH), F32),
        jax.ShapeDtypeStruct((N_HEADS, lp, LANES), BF16),
        jax.ShapeDtypeStruct((N_HEADS, lp, LANES), BF16),
        jax.ShapeDtypeStruct((N_HEADS, LANES, lp), BF16),
        jax.ShapeDtypeStruct((N_HEADS, LANES, lp), BF16),
        jax.ShapeDtypeStruct((N_HEADS, LANES, lp), BF16),
        jax.ShapeDtypeStruct((lp, LANES), F32),
        jax.ShapeDtypeStruct((lp, D_MODEL), BF16),
    ]
    heads = pl.BlockSpec((N_HEADS, tm, LANES), lambda i: (0, i, 0))
    heads_t = pl.BlockSpec((N_HEADS, LANES, tm), lambda i: (0, 0, i))
    out_specs = [row_f32(D_MODEL), row_f32(512), row_f32(512), heads, heads, heads_t, heads_t, heads_t,
                 row_f32(LANES), row_f32(D_MODEL)]
    in_specs = [
        pl.BlockSpec((tm, D_MODEL), lambda i: (jnp.maximum(i - 1, 0), 0)),
        _const((tm, D_MODEL)), _const((1, D_MODEL)),
        _const((2560, D_MODEL)), _const((LANES, D_MODEL)), _const((1, LANES)),
        _const((4, POOL_GROUP, POOL_GROUP)), _const((1, POOL_WIDTH)), _const((POOL_WIDTH, D_MODEL)),
    ]
    return pl.pallas_call(
        body, name="forward_in", grid=(nt,), out_shape=out_shape, in_specs=in_specs, out_specs=out_specs,
        scratch_shapes=[pltpu.VMEM((tm + MAX_WINDOW, POOL_WIDTH), F32), pltpu.VMEM((1, LANES), F32)],
        compiler_params=_params(("arbitrary",)),
    )(x, tile0, norm_g, w_main, w_f, b_f, pool_w, pool_scale, w_up_pool)


def _causal(tb):
    return lax.broadcasted_iota(jnp.int32, (tb, tb), 1) <= lax.broadcasted_iota(jnp.int32, (tb, tb), 0)


def _pair_lanes(a0, a1):
    lane = lax.broadcasted_iota(jnp.int32, a0.shape, 1)
    return jnp.where(lane < HEAD_DIM, a0, pltpu.roll(a1, HEAD_DIM, axis=1))


def _behind(items, ins, outs, sems):
    step, last = pl.program_id(0), pl.num_programs(0) - 1

    @pl.when(step == 0)
    def _():
        for cp in _exchange_copies(items, ins, outs, *sems):
            cp.start()

    def finish():
        @pl.when(step == last)
        def _():
            for cp in _exchange_copies(items, ins, outs, *sems):
                cp.wait()

    return finish


def _attention_forward(qt, k, vt, behind):
    lp = k.shape[1]
    tk = ATT_TILE
    tq_big = ATT_Q_BLOCKS * tk
    n_big = (lp // tk - 1) // ATT_Q_BLOCKS
    assert lp == tk + n_big * tq_big and ATT_Q_BLOCKS == 2
    nx = len(behind)

    def body(qt_ref, k_ref, vt_ref, *rest):
        o_ref, lse_ref = rest[nx:nx + 2]
        s_buf, m_scr, acc_scr = rest[2 * nx + 2:2 * nx + 5]
        finish_exchange = _behind(behind, rest[:nx], rest[nx + 2:2 * nx + 2], rest[2 * nx + 5:])

        def q_tile(q0, tq, pairs):
            first = q0 // tk
            qts = [qt_ref[e, :, pl.ds(q0, tq)] for e in range(2)]

            def block(kj):
                return pl.ds(kj * tk if isinstance(kj, int) else pl.multiple_of(kj * tk, tk), tk)

            def step(kj, rd, wr, mask=None):
                for e in range(2):
                    s = s_buf[rd, e, :, 0:tq]
                    if wr is not None:
                        s_buf[wr, e, :, 0:tq] = _dot(k_ref[e, block(kj + 1), :], qts[e])
                    if mask is not None:
                        s = jnp.where(mask, s, NEG)
                    m = m_scr[e, :, 0:tq]
                    m_new = jnp.maximum(m, jnp.max(s, axis=0, keepdims=True))
                    p = jnp.exp(s - m_new)
                    pv = _dot(vt_ref[e, :, block(kj)], p.astype(BF16))
                    acc_scr[e, :, 0:tq] = jnp.exp(m - m_new) * acc_scr[e, :, 0:tq] + pv
                    m_scr[e, :, 0:tq] = m_new

            keys = lax.broadcasted_iota(jnp.int32, (tk, tq), 0)
            queries = lax.broadcasted_iota(jnp.int32, (tk, tq), 1)
            for e in range(2):
                m_scr[e, :, 0:tq] = jnp.full((1, tq), NEG, F32)
                acc_scr[e, :, 0:tq] = jnp.zeros((LANES, tq), F32)
                s_buf[0, e, :, 0:tq] = _dot(k_ref[e, block(0), :], qts[e])
            if pairs is None:
                step(0, 0, None, keys <= queries)
            else:
                step(0, 0, 1)

                def two_steps(t, _):
                    step(1 + 2 * t, 1, 0)
                    step(2 + 2 * t, 0, 1)
                    return 0

                lax.fori_loop(0, pairs, two_steps, 0)
                step(first, 1, 0, keys <= queries)
                step(first + 1, 0, None, keys + tk <= queries)
            outs, lses = [], []
            for e in range(2):
                acc = acc_scr[e, :, 0:tq]
                l = acc[V_ONES:V_ONES + 1, :]
                outs.append((acc / l).T)
                lses.append(m_scr[e, :, 0:tq] + jnp.log(l))
            o_ref[pl.ds(q0, tq), :] = _pair_lanes(outs[0], outs[1]).astype(BF16)
            lse_rows = jnp.concatenate(lses + [jnp.zeros((LANES - 2, tq), F32)], axis=0)
            lse_ref[pl.ds(q0, tq), :] = lse_rows.T

        q_tile(0, tk, None)

        def big_tile(i, _):
            q_tile(pl.multiple_of(tk + i * tq_big, tk), tq_big, i)
            return 0

        lax.fori_loop(0, n_big, big_tile, 0)
        finish_exchange()

    pair = pl.BlockSpec((lp, LANES), lambda hp: (0, hp))
    heads = pl.BlockSpec((2, lp, LANES), lambda hp: (hp, 0, 0), pipeline_mode=pl.Buffered(1))
    heads_t = pl.BlockSpec((2, LANES, lp), lambda hp: (hp, 0, 0), pipeline_mode=pl.Buffered(1))
    hbm = pl.BlockSpec(memory_space=pl.ANY)
    return pl.pallas_call(
        body, name="attention_forward", grid=(N_HEADS // 2,),
        out_shape=[jax.ShapeDtypeStruct((lp, ATTN_WIDTH), BF16), jax.ShapeDtypeStruct((lp, ATTN_WIDTH), F32)]
        + _exchange_results(behind),
        in_specs=[heads_t, heads, heads_t] + [hbm] * nx,
        out_specs=[pair, pair] + [hbm] * nx,
        scratch_shapes=[pltpu.VMEM((2, 2, tk, tq_big), F32), pltpu.VMEM((2, 1, tq_big), F32),
                        pltpu.VMEM((2, LANES, tq_big), F32)] + _exchange_semaphores(nx),
        compiler_params=_params(("arbitrary",)),
    )(qt, k, vt, *[a for _, a, _ in behind])


def _rows3(first, x):
    sub = lax.broadcasted_iota(jnp.int32, (LANES, x.shape[1]), 0)
    hi = x.astype(BF16).astype(F32)
    rest = x - hi
    mid = rest.astype(BF16).astype(F32)
    lo = (rest - mid).astype(BF16).astype(F32)
    out = jnp.zeros((LANES, x.shape[1]), F32)
    for j, piece in enumerate((hi, mid, lo)):
        out = jnp.where(sub == first + j, piece, out)
    return out


def _attention_backward(qt, k, kt, v, do, o, lse, behind):
    lp = k.shape[1]
    tb = ATT_TILE
    nb = lp // tb
    tq_big = ATT_Q_BLOCKS * tb
    n_big = (nb - 1) // ATT_Q_BLOCKS
    assert lp == tb + n_big * tq_big and ATT_Q_BLOCKS == 2
    nx = len(behind)

    def body(qt_ref, k_ref, kt_ref, v_ref, do_ref, o_ref, lse_ref, *rest):
        dq_ref, dk_ref, dv_ref, dc_ref = rest[nx:nx + 4]
        q2_ref, do2_ref, dk_acc, dv_acc, dq_scr, s_buf = rest[2 * nx + 4:2 * nx + 10]
        finish_exchange = _behind(behind, rest[:nx], rest[nx + 4:2 * nx + 4], rest[2 * nx + 10:])
        sub = lax.broadcasted_iota(jnp.int32, (LANES, tb), 0)

        def lanes01(row0, row1):
            n = row0.shape[1]
            return jnp.concatenate([row0, row1, jnp.zeros((LANES - 2, n), F32)], axis=0).T

        def prepare(bi, _):
            r0 = pl.multiple_of(bi * tb, tb)
            queries = r0 + lax.broadcasted_iota(jnp.int32, (1, tb), 1)
            dob = do_ref[pl.ds(r0, tb), :].astype(F32)
            do_t = dob.T
            dd_t = (dob * o_ref[pl.ds(r0, tb), :].astype(F32)).T
            lse_t = lse_ref[pl.ds(r0, tb), :].T
            for e in range(2):
                delta = jnp.sum(dd_t[HEAD_DIM * e:HEAD_DIM * (e + 1), :], axis=0, keepdims=True)
                do_e = jnp.concatenate([do_t[HEAD_DIM * e:HEAD_DIM * (e + 1), :], jnp.zeros((HEAD_DIM, tb), F32)], axis=0)
                do2_ref[e, :, pl.ds(r0, tb)] = jnp.where(sub < HEAD_DIM, do_e, _rows3(DO_BIAS, -delta)).astype(BF16)
                minus_lse = jnp.where(queries >= PAD, -lse_t[e:e + 1, :], NEG)
                keep = (sub < Q_LSE) | (sub >= Q_LSE + 3)
                q2_ref[e, :, pl.ds(r0, tb)] = jnp.where(keep, qt_ref[e, :, pl.ds(r0, tb)].astype(F32),
                                                        _rows3(Q_LSE, minus_lse)).astype(BF16)
            return 0

        lax.fori_loop(0, nb, prepare, 0)
        dk_acc[...] = jnp.zeros_like(dk_acc)
        dv_acc[...] = jnp.zeros_like(dv_acc)

        def q_tile(q0, tq, pairs):
            first = q0 // tb
            qts = [q2_ref[e, :, pl.ds(q0, tq)] for e in range(2)]
            dots = [do2_ref[e, :, pl.ds(q0, tq)] for e in range(2)]

            def block(kj):
                return pl.ds(kj * tb if isinstance(kj, int) else pl.multiple_of(kj * tb, tb), tb)

            def step(kj, rd, wr, mask=None):
                for e in range(2):
                    s = s_buf[rd, e, :, 0:tq]
                    if wr is not None:
                        s_buf[wr, e, :, 0:tq] = _dot(k_ref[e, block(kj + 1), :], qts[e])
                    dpd = _dot(v_ref[e, block(kj), :], dots[e])
                    p = jnp.exp(s)
                    if mask is not None:
                        p = jnp.where(mask, p, 0.0)
                    dsb = (p * dpd).astype(BF16)
                    dv_acc[e, :, block(kj)] += _dot_nt(dots[e], p.astype(BF16))
                    dk_acc[e, :, block(kj)] += _dot_nt(qts[e], dsb)
                    dq_scr[e, :, 0:tq] += _dot(kt_ref[e, :, block(kj)], dsb)

            keys = lax.broadcasted_iota(jnp.int32, (tb, tq), 0)
            queries = lax.broadcasted_iota(jnp.int32, (tb, tq), 1)
            for e in range(2):
                dq_scr[e, :, 0:tq] = jnp.zeros((LANES, tq), F32)
                s_buf[0, e, :, 0:tq] = _dot(k_ref[e, block(0), :], qts[e])
            if pairs is None:
                step(0, 0, None, keys <= queries)
            else:
                step(0, 0, 1)

                def two_steps(t, _):
                    step(1 + 2 * t, 1, 0)
                    step(2 + 2 * t, 0, 1)
                    return 0

                lax.fori_loop(0, pairs, two_steps, 0)
                step(first, 1, 0, keys <= queries)
                step(first + 1, 0, None, keys + tb <= queries)
            dq0, dq1 = dq_scr[0, :, 0:tq], dq_scr[1, :, 0:tq]
            dq_ref[pl.ds(q0, tq), :] = (_pair_lanes(dq0.T, dq1.T) * 0.125).astype(BF16)
            dc_ref[pl.ds(q0, tq), :] = lanes01(dq0[K_ONES:K_ONES + 1, :], dq1[K_ONES:K_ONES + 1, :])

        q_tile(0, tb, None)

        def big_tile(i, _):
            q_tile(pl.multiple_of(tb + i * tq_big, tb), tq_big, i)
            return 0

        lax.fori_loop(0, n_big, big_tile, 0)

        def finish(bi, _):
            r0 = pl.multiple_of(bi * tb, tb)
            dk0, dk1 = dk_acc[0, :, pl.ds(r0, tb)], dk_acc[1, :, pl.ds(r0, tb)]
            dk_ref[pl.ds(r0, tb), :] = _pair_lanes(dk0.T, dk1.T).astype(BF16)
            dv_ref[pl.ds(r0, tb), :] = _pair_lanes(dv_acc[0, :, pl.ds(r0, tb)].T, dv_acc[1, :, pl.ds(r0, tb)].T).astype(BF16)
            dc_ref[pl.ds(r0, tb), :] = dc_ref[pl.ds(r0, tb), :] - lanes01(dk0[Q_ONES:Q_ONES + 1, :], dk1[Q_ONES:Q_ONES + 1, :])
            return 0

        lax.fori_loop(0, nb, finish, 0)
        finish_exchange()

    once = pl.Buffered(1)
    pair = pl.BlockSpec((lp, LANES), lambda hp: (0, hp))
    pair_in = pl.BlockSpec((lp, LANES), lambda hp: (0, hp), pipeline_mode=once)
    heads = pl.BlockSpec((2, lp, LANES), lambda hp: (hp, 0, 0), pipeline_mode=once)
    heads_t = pl.BlockSpec((2, LANES, lp), lambda hp: (hp, 0, 0), pipeline_mode=once)
    hbm = pl.BlockSpec(memory_space=pl.ANY)
    wide = jax.ShapeDtypeStruct((lp, ATTN_WIDTH), BF16)
    return pl.pallas_call(
        body, name="attention_backward", grid=(N_HEADS // 2,),
        out_shape=[wide, wide, wide, jax.ShapeDtypeStruct((lp, ATTN_WIDTH), F32)] + _exchange_results(behind),
        in_specs=[heads_t, heads, heads_t, heads, pair_in, pair_in, pair_in] + [hbm] * nx,
        out_specs=[pair, pair, pair, pair] + [hbm] * nx,
        scratch_shapes=[pltpu.VMEM((2, LANES, lp), BF16), pltpu.VMEM((2, LANES, lp), BF16),
                        pltpu.VMEM((2, LANES, lp), F32), pltpu.VMEM((2, LANES, lp), F32),
                        pltpu.VMEM((2, LANES, tq_big), F32), pltpu.VMEM((2, 2, tb, tq_big), F32)]
        + _exchange_semaphores(nx),
        compiler_params=_params(("arbitrary",)),
    )(qt, k, kt, v, do, o, lse, *[a for _, a, _ in behind])


def _middle(x, target, h, o, a_pool, u, zp, w_main, w_up_pool, w_up_attn, w_out, pool_w, pool_scale, final_g):
    seq = x.shape[0]
    tm = ROW_TILE
    nt = seq // tm + 1
    lp = nt * tm
    halo_blocks = tm // MAX_WINDOW

    def body(x_ref, t_ref, h_ref, o_ref, ap_ref, u_ref, uh_ref, zp_ref,
             wc_ref, wupp_ref, wupa_ref, wout_ref, pw_ref, sc_ref, gf_ref,
             dh2_ref, mg_ref, yp_ref, ya_ref, dap_ref, daa_ref, do_ref, dza_ref, dgp_ref, dga_ref, dzp_ref, dpn_ref,
             loss_ref, dgf_ref, dsc_ref, dpw_ref):
        i = pl.program_id(0)
        tiles = (dh2_ref, mg_ref, yp_ref, ya_ref, dap_ref, daa_ref, do_ref, dza_ref, dgp_ref, dga_ref, dzp_ref, dpn_ref)

        @pl.when(i == 0)
        def _():
            for ref in tiles + (loss_ref, dgf_ref, dsc_ref, dpw_ref):
                ref[...] = jnp.zeros_like(ref)

        @pl.when(i > 0)
        def _():
            xt = x_ref[...]
            hb = h_ref[...]
            pc = _dot_nt(hb, wc_ref[...])
            za, gp, ga = pc[:, :512], pc[:, 512:1536], pc[:, 1536:]
            of = o_ref[...].astype(F32)
            sza = _sigmoid(za)
            silu_za = za * sza
            ya = (of * silu_za).astype(BF16)
            ya_ref[...] = ya
            aa = _dot(ya, wupa_ref[...])
            ap = ap_ref[...].astype(F32)
            sgp, sga = _sigmoid(gp), _sigmoid(ga)
            mg = (sgp * ap + sga * aa).astype(BF16)
            mg_ref[...] = mg
            h2 = xt + _dot(mg, wout_ref[...])
            r2 = lax.rsqrt(jnp.mean(h2 * h2, axis=-1, keepdims=True) + RMS_EPS)
            h2n = h2 * r2
            gf = gf_ref[...]
            diff = h2n * gf - t_ref[...]
            loss_ref[...] += 0.5 * jnp.sum(jnp.mean(diff * diff, axis=-1, keepdims=True), axis=0, keepdims=True)
            dy = diff * (1.0 / D_MODEL)
            dgf_ref[...] += jnp.sum(dy * h2n, axis=0, keepdims=True)
            dyg = dy * gf
            dh2 = r2 * (dyg - h2n * jnp.mean(dyg * h2n, axis=-1, keepdims=True))
            dh2_ref[...] = dh2
            dmg = _dot_nt(dh2.astype(BF16), wout_ref[...])
            dap = (dmg * sgp).astype(BF16)
            daa = (dmg * sga).astype(BF16)
            dap_ref[...] = dap
            daa_ref[...] = daa
            dgp_ref[...] = (dmg * ap * sgp * (1.0 - sgp)).astype(BF16)
            dga_ref[...] = (dmg * aa * sga * (1.0 - sga)).astype(BF16)
            dyp = _dot_nt(dap, wupp_ref[...])
            dya = _dot_nt(daa, wupa_ref[...])
            do_ref[...] = (dya * silu_za).astype(BF16)
            dza_ref[...] = (dya * of * (sza * (1.0 + za * (1.0 - sza)))).astype(BF16)

            u = u_ref[...]
            zp = zp_ref[...]
            counts = _pool_counts(i * tm, tm)
            ps = _pool_means(jnp.concatenate([uh_ref[...], u], axis=0), u, counts)
            pbs = [p.astype(BF16) for p in ps]
            ppw = jnp.concatenate([_dot(pbs[g], pw_ref[g]) for g in range(4)], axis=1)
            sc = sc_ref[...]
            szp = _sigmoid(zp)
            silu_zp = zp * szp
            ypre = ppw * sc
            yp_ref[...] = (ypre * silu_zp).astype(BF16)
            dypre = dyp * silu_zp
            dzp_ref[...] = (dyp * ypre * (szp * (1.0 + zp * (1.0 - szp)))).astype(BF16)
            dsc_ref[...] += jnp.sum(dypre * ppw, axis=0, keepdims=True)
            dppw = (dypre * sc).astype(BF16)
            dpns = []
            for g in range(4):
                dg = dppw[:, POOL_GROUP * g:POOL_GROUP * (g + 1)]
                dpw_ref[g] += _dot_tn(pbs[g], dg)
                dpns.append(_dot_nt(dg, pw_ref[g]) / counts[g])
            dpn_ref[...] = jnp.concatenate(dpns, axis=1)

    real = lambda w: pl.BlockSpec((tm, w), lambda i: (jnp.maximum(i - 1, 0), 0))
    row = lambda w: pl.BlockSpec((tm, w), lambda i: (i, 0))
    in_specs = [
        real(D_MODEL), real(D_MODEL), row(D_MODEL), row(512), row(D_MODEL), row(512),
        pl.BlockSpec((MAX_WINDOW, 512), lambda i: (jnp.maximum(i * halo_blocks - 1, 0), 0)), row(512),
        _const((2560, D_MODEL), (1, 0)), _const((POOL_WIDTH, D_MODEL)), _const((ATTN_WIDTH, D_MODEL)),
        _const((D_MODEL, D_MODEL)), _const((4, POOL_GROUP, POOL_GROUP)), _const((1, POOL_WIDTH)), _const((1, D_MODEL)),
    ]
    sd = jax.ShapeDtypeStruct
    out_shape = [
        sd((lp, D_MODEL), F32),
        sd((lp, D_MODEL), BF16),
        sd((lp, 512), BF16),
        sd((lp, 512), BF16),
        sd((lp, D_MODEL), BF16),
        sd((lp, D_MODEL), BF16),
        sd((lp, 512), BF16),
        sd((lp, 512), BF16),
        sd((lp, D_MODEL), BF16),
        sd((lp, D_MODEL), BF16),
        sd((lp, 512), BF16),
        sd((lp, 512), F32),
        sd((1, LANES), F32),
        sd((1, D_MODEL), F32),
        sd((1, 512), F32),
        sd((4, POOL_GROUP, POOL_GROUP), F32),
    ]
    keep = lambda shape: pl.BlockSpec(shape, lambda i: (0,) * len(shape))
    out_specs = [row(D_MODEL), row(D_MODEL), row(512), row(512), row(D_MODEL), row(D_MODEL), row(512), row(512),
                 row(D_MODEL), row(D_MODEL), row(512), row(512),
                 keep((1, LANES)), keep((1, D_MODEL)), keep((1, 512)), keep((4, POOL_GROUP, POOL_GROUP))]
    return pl.pallas_call(
        body, name="middle", grid=(nt,), out_shape=out_shape, in_specs=in_specs, out_specs=out_specs,
        compiler_params=_params(("arbitrary",)),
    )(x, target, h, o, a_pool, u, u, zp, w_main, w_up_pool, w_up_attn, w_out, pool_w, pool_scale, final_g)


def _backward_in(x, tile0, norm_g, dh2, dpn, dzp, dq, dk, dv, dza, dgp, dga, dc, sneg, w_main, w_f):
    seq = x.shape[0]
    tm = ROW_TILE
    nt = seq // tm + 1
    lp = nt * tm
    halo_blocks = tm // MAX_WINDOW
    last_halo = lp // MAX_WINDOW - 1

    def body(x_ref, t0_ref, g_ref, dh2_ref, dpn_ref, dpnh_ref, dzp_ref, dq_ref, dk_ref, dv_ref, dza_ref,
             dgp_ref, dga_ref, dc_ref, sn_ref, wm_ref, wf_ref,
             dproj_ref, df_ref, gx_ref, gmeta_ref, dg_ref, dbf_ref, carry_ref):
        i = pl.program_id(0)
        t = nt - 1 - i

        @pl.when(i == 0)
        def _():
            carry_ref[...] = jnp.zeros_like(carry_ref)
            dg_ref[...] = jnp.zeros_like(dg_ref)
            dbf_ref[...] = jnp.zeros_like(dbf_ref)

        dpn_t = dpn_ref[...]
        ahead = jnp.where(i == 0, jnp.zeros_like(dpnh_ref), dpnh_ref[...])
        ext = jnp.concatenate([dpn_t, ahead], axis=0)
        counts = _pool_counts(t * tm, tm)
        for g, w in enumerate(POOL_WINDOWS):
            s = ext[:, POOL_GROUP * g:POOL_GROUP * (g + 1)]
            sh = 1
            while sh < w:
                s = s + pltpu.roll(s, tm + MAX_WINDOW - sh, axis=0)
                sh *= 2
            du = s[:tm, :] - dpn_t[:, POOL_GROUP * g:POOL_GROUP * (g + 1)] * counts[g]
            dproj_ref[:, POOL_GROUP * g:POOL_GROUP * (g + 1)] = du.astype(BF16)
        dproj_ref[:, 512:1024] = dzp_ref[...]
        dproj_ref[:, 1024:1536] = dq_ref[...]
        dproj_ref[:, 1536:2048] = dk_ref[...]
        dproj_ref[:, 2048:2560] = dv_ref[...]
        dproj_ref[:, 2560:3072] = dza_ref[...]
        dproj_ref[:, 3072:4096] = dgp_ref[...]
        dproj_ref[:, 4096:5120] = dga_ref[...]

        dct = dc_ref[:, 0:LANES]
        for hp in range(1, N_HEADS // 2):
            dct = dct + pltpu.roll(dc_ref[:, LANES * hp:LANES * (hp + 1)], 2 * hp, axis=1)
        rloc = lax.broadcasted_iota(jnp.int32, (tm, LANES), 0)
        sh = 1
        while sh < tm:
            dct = dct + jnp.where(rloc + sh < tm, pltpu.roll(dct, tm - sh, axis=0), 0.0)
            sh *= 2
        dct = dct + carry_ref[...]
        carry_ref[...] = dct[0:1, :]
        df = dct * sn_ref[...]
        dbf_ref[...] += jnp.sum(df, axis=0, keepdims=True)
        dfb = df.astype(BF16)
        df_ref[...] = dfb

        dh = _dot(dproj_ref[...], wm_ref[...]) + _dot(dfb, wf_ref[...])
        xt = jnp.where(t == 0, t0_ref[...], x_ref[...])
        r = lax.rsqrt(jnp.mean(xt * xt, axis=-1, keepdims=True) + RMS_EPS)
        xn = xt * r
        dg_ref[...] += jnp.sum(dh * xn, axis=0, keepdims=True)
        dhg = dh * g_ref[...]
        dx = dh2_ref[...] + r * (dhg - xn * jnp.mean(dhg * xn, axis=-1, keepdims=True))

        @pl.when(t > 0)
        def _():
            gx_ref[...] = dx

        @pl.when(t == 0)
        def _():
            gmeta_ref[...] = dx[PAD:, :]

    rev = lambda w: pl.BlockSpec((tm, w), lambda i: (nt - 1 - i, 0))
    real = pl.BlockSpec((tm, D_MODEL), lambda i: (jnp.maximum(nt - 2 - i, 0), 0))
    in_specs = [
        real, _const((tm, D_MODEL)), _const((1, D_MODEL)), rev(D_MODEL), rev(512),
        pl.BlockSpec((MAX_WINDOW, 512), lambda i: (jnp.minimum((nt - i) * halo_blocks, last_halo), 0)),
        rev(512), rev(512), rev(512), rev(512), rev(512), rev(D_MODEL), rev(D_MODEL),
        rev(512), rev(LANES),
        _const((N_MAIN, D_MODEL)), _const((LANES, D_MODEL)),
    ]
    sd = jax.ShapeDtypeStruct
    out_shape = [sd((lp, N_MAIN), BF16), sd((lp, LANES), BF16), sd((seq, D_MODEL), F32), sd((N_META, D_MODEL), F32),
                 sd((1, D_MODEL), F32), sd((1, LANES), F32)]
    keep = lambda shape: pl.BlockSpec(shape, lambda i: (0,) * len(shape))
    out_specs = [rev(N_MAIN), rev(LANES), real, keep((N_META, D_MODEL)), keep((1, D_MODEL)), keep((1, LANES))]
    return pl.pallas_call(
        body, name="backward_in", grid=(nt,), out_shape=out_shape, in_specs=in_specs, out_specs=out_specs,
        scratch_shapes=[pltpu.VMEM((1, LANES), F32)],
        compiler_params=_params(("arbitrary",)),
    )(x, tile0, norm_g, dh2, dpn, dpn, dzp, dq, dk, dv, dza, dgp, dga, dc, sneg, w_main, w_f)


def _matmul_tn(name, a, b, tn):
    lp, m = a.shape
    n = b.shape[1]

    def body(a_ref, b_ref, c_ref):
        c_ref[...] = _dot_tn(a_ref[...].astype(BF16), b_ref[...].astype(BF16))

    return pl.pallas_call(
        body, name=name, grid=(n // tn,), out_shape=jax.ShapeDtypeStruct((m, n), F32),
        in_specs=[_const((lp, m)), pl.BlockSpec((lp, tn), lambda j: (0, j))],
        out_specs=pl.BlockSpec((m, tn), lambda j: (0, j)),
        compiler_params=_params(("arbitrary",)),
    )(a, b)


def _matmul_tn_rows(name, a, b, tm):
    lp, m = a.shape
    n = b.shape[1]

    def body(a_ref, b_ref, c_ref):
        c_ref[...] = _dot_tn(a_ref[...].astype(BF16), b_ref[...].astype(BF16))

    return pl.pallas_call(
        body, name=name, grid=(m // tm,), out_shape=jax.ShapeDtypeStruct((m, n), F32),
        in_specs=[pl.BlockSpec((lp, tm), lambda j: (0, j)), _const((lp, n))],
        out_specs=pl.BlockSpec((tm, n), lambda j: (j, 0)),
        compiler_params=_params(("arbitrary",)),
    )(a, b)


def _adamw_step(p_ref, w_ref, m_ref, v_ref, g_ref, d_ref, mo_ref, vo_ref):
    g = p_ref[0].astype(F32)
    for s in range(1, p_ref.shape[0]):
        g = g + p_ref[s].astype(F32)
    m_new = ADAM_B1 * m_ref[...] + (1.0 - ADAM_B1) * g
    v_new = ADAM_B2 * v_ref[...] + (1.0 - ADAM_B2) * (g * g)
    m_hat = m_new / (1.0 - ADAM_B1 ** ADAM_STEP)
    v_hat = v_new / (1.0 - ADAM_B2 ** ADAM_STEP)
    g_ref[...] = g
    d_ref[...] = -ADAM_LR * (m_hat / (jnp.sqrt(v_hat) + ADAM_EPS) + ADAM_WD * w_ref[...])
    mo_ref[...] = m_new
    vo_ref[...] = v_new


def _adamw_small(name, groups, loss_parts):
    n = len(groups)

    def body(*refs):
        ins, outs = refs[:4 * n + 1], refs[4 * n + 1:]
        for j in range(n):
            _adamw_step(*ins[4 * j:4 * j + 4], *outs[4 * j:4 * j + 4])
        total = ins[-1][0]
        for s in range(1, N_DEV):
            total = total + ins[-1][s]
        outs[-1][...] = total

    vmem = pl.BlockSpec(memory_space=pltpu.VMEM)
    out_shape = [jax.ShapeDtypeStruct(w.shape, F32) for _, w, _, _ in groups for _ in range(4)]
    out_shape.append(jax.ShapeDtypeStruct(loss_parts.shape[1:], F32))
    res = pl.pallas_call(
        body, name=name, out_shape=out_shape, in_specs=[vmem] * (4 * n + 1), out_specs=[vmem] * (4 * n + 1),
        compiler_params=_params(),
    )(*[a for g in groups for a in g], loss_parts)
    return [res[4 * j:4 * j + 4] for j in range(n)], res[-1]


def _adamw(name, parts, w, m, v, rows, cols=None):
    r, c_all = w.shape
    c = cols or c_all
    n_parts = parts.shape[0]

    def body(p_ref, w_ref, m_ref, v_ref, g_ref, d_ref, mo_ref, vo_ref):
        _adamw_step(p_ref, w_ref, m_ref, v_ref, g_ref, d_ref, mo_ref, vo_ref)

    blk = pl.BlockSpec((rows, c), lambda i, j: (i, j))
    return pl.pallas_call(
        body, name=name, grid=(r // rows, c_all // c), out_shape=[jax.ShapeDtypeStruct((r, c_all), F32)] * 4,
        in_specs=[pl.BlockSpec((n_parts, rows, c), lambda i, j: (0, i, j)), blk, blk, blk],
        out_specs=[blk] * 4,
        compiler_params=_params(("arbitrary", "arbitrary")),
    )(parts, w, m, v)


def _pair_sum(name, mine, theirs, rows):
    n, r, c = mine.shape

    def body(a_ref, b_ref, o_ref):
        o_ref[...] = (a_ref[...] + b_ref[...].astype(F32)).astype(BF16)

    blk = pl.BlockSpec((1, rows, c), lambda j, i: (j, i, 0))
    return pl.pallas_call(
        body, name=name, grid=(n, r // rows), out_shape=jax.ShapeDtypeStruct((n, r, c), BF16),
        in_specs=[blk, blk], out_specs=blk,
        compiler_params=_params(("arbitrary", "arbitrary")),
    )(mine, theirs)


def _columns_to_slots(a):
    r, c8 = a.shape
    return a.reshape(r, N_DEV, c8 // N_DEV).transpose(1, 0, 2)


def _by_core(slots):
    by_core = slots.reshape((4, 2) + slots.shape[1:]).swapaxes(0, 1)
    c = lax.axis_index("c")
    return (lax.dynamic_index_in_dim(by_core, c, 0, keepdims=False),
            lax.dynamic_index_in_dim(by_core, 1 - c, 0, keepdims=False).astype(BF16))


def _slots_to_columns(a):
    n, r, c = a.shape
    return a.transpose(1, 0, 2).reshape(r, n * c)


def kernel(x, meta_tokens, norm_g, w_in, b_forget, pool_w, pool_scale, w_up_pool, w_up_attn, w_out, final_norm_g, loss_target, m_meta_tokens, m_norm_g, m_w_in, m_b_forget, m_pool_w, m_pool_scale, m_w_up_pool, m_w_up_attn, m_w_out, m_final_norm_g, v_meta_tokens, v_norm_g, v_w_in, v_b_forget, v_pool_w, v_pool_scale, v_w_up_pool, v_w_up_attn, v_w_out, v_final_norm_g):
    xs = x[0]
    target = loss_target[0]

    g_in, g_upp, g_meta = _gather_two_level(
        "gather_weights", [w_in[0].T.astype(BF16), w_up_pool[0].astype(BF16), meta_tokens])
    w_full = g_in.reshape(N_DEV * g_in.shape[1], D_MODEL)
    w_main = jnp.concatenate([w_full[:N_BEFORE_F], w_full[N_BEFORE_F + N_HEADS:]], axis=0)
    w_f = jnp.pad(w_full[N_BEFORE_F:N_BEFORE_F + N_HEADS], ((0, LANES - N_HEADS), (0, 0)))
    wupp = _slots_to_columns(g_upp)
    meta = _slots_to_columns(g_meta)
    tile0 = jnp.concatenate([jnp.zeros((PAD, D_MODEL), F32), meta], axis=0)
    b_f = jnp.pad(b_forget, ((0, 0), (0, LANES - N_HEADS)))
    pw_b = pool_w[0].astype(BF16)
    final_g = final_norm_g.reshape(1, D_MODEL)

    (h, u, zp, k, v, qt, kt, vt, sneg, a_pool) = _forward_in(xs, tile0, norm_g, w_main, w_f, b_f, pw_b,
                                                              pool_scale, wupp)
    o, lse, g_upa, g_out = _attention_forward(
        qt, k, vt, [("gather", w_up_attn[0].astype(BF16), ALL_PEERS), ("gather", w_out[0].astype(BF16), ALL_PEERS)])
    wupa = _slots_to_columns(g_upa)
    wout = g_out.reshape(D_MODEL, D_MODEL)
    (dh2, mg, yp, ya, dap, daa, do, dza, dgp, dga, dzp, dpn,
     loss_part, d_final_g, d_scale, d_pool_w) = _middle(xs, target, h, o, a_pool, u, zp, w_main, wupp, wupa, wout,
                                                        pw_b, pool_scale, final_g)
    dw_out = _matmul_tn("grad_w_out", mg, dh2, 256)
    dw_upp = _matmul_tn("grad_w_up_pool", yp, dap, 512)
    dw_upa = _matmul_tn("grad_w_up_attn", ya, daa, 512)
    dq, dk, dv, dc, p_upp, p_upa, p_out, p_pool_w, p_scale, p_final_g = _attention_backward(
        qt, k, kt, v, do, o, lse,
        [("scatter", _columns_to_slots(dw_upp).astype(BF16), ALL_PEERS),
         ("scatter", _columns_to_slots(dw_upa).astype(BF16), ALL_PEERS),
         ("scatter", dw_out.reshape(N_DEV, D_MODEL // N_DEV, D_MODEL).astype(BF16), ALL_PEERS),
         ("gather", d_pool_w.reshape(4 * POOL_GROUP, POOL_GROUP), ALL_PEERS),
         ("gather", d_scale, ALL_PEERS), ("gather", d_final_g, ALL_PEERS)])
    dproj, df, grad_x, d_meta, d_norm_g, d_bf = _backward_in(xs, tile0, norm_g, dh2, dpn, dzp, dq, dk, dv, dza,
                                                             dgp, dga, dc, sneg, w_main, w_f)
    dw_main = _matmul_tn_rows("grad_w_in", dproj, h, 512)
    dw_f = _matmul_tn_rows("grad_w_forget", df, h, LANES)
    dw_in = jnp.concatenate([dw_main[:N_BEFORE_F], dw_f[:N_HEADS], dw_main[N_BEFORE_F:]], axis=0)
    dw_in = dw_in.reshape(N_DEV, dw_in.shape[0] // N_DEV, D_MODEL)

    mine, for_sibling = _by_core(dw_in)
    from_sibling, = _exchange("swap_with_sibling", [("swap", for_sibling, (SIBLING,))])
    pair_sums = _pair_sum("pair_sum", mine, from_sibling, dw_in.shape[1])
    p_in, p_meta, p_norm_g, p_bf, p_loss = _exchange(
        "exchange_gradients",
        [("chips", pair_sums, SAME_CORE), ("scatter", _columns_to_slots(d_meta), ALL_PEERS),
         ("gather", d_norm_g, ALL_PEERS), ("gather", d_bf, ALL_PEERS), ("gather", loss_part, ALL_PEERS)])


    def pad_f(a):
        return jnp.pad(a, ((0, 0), (0, LANES - N_HEADS)))

    res = {}
    res["w_in"] = [a.T for a in _adamw("adamw_w_in", p_in, w_in[0].T, m_w_in[0].T, v_w_in[0].T, p_in.shape[1], 256)]
    res["w_up_pool"] = _adamw("adamw_w_up_pool", p_upp, w_up_pool[0], m_w_up_pool[0], v_w_up_pool[0], 512)
    res["w_up_attn"] = _adamw("adamw_w_up_attn", p_upa, w_up_attn[0], m_w_up_attn[0], v_w_up_attn[0], 512)
    res["w_out"] = _adamw("adamw_w_out", p_out, w_out[0], m_w_out[0], v_w_out[0], 128)
    flat = lambda a: a.reshape(4 * POOL_GROUP, POOL_GROUP)
    row = lambda a: a.reshape(1, D_MODEL)
    small, loss_row = _adamw_small(
        "adamw_small",
        [(p_meta, meta_tokens, m_meta_tokens, v_meta_tokens),
         (p_norm_g, norm_g, m_norm_g, v_norm_g),
         (p_bf, pad_f(b_forget), pad_f(m_b_forget), pad_f(v_b_forget)),
         (p_pool_w, flat(pool_w), flat(m_pool_w), flat(v_pool_w)),
         (p_scale, pool_scale, m_pool_scale, v_pool_scale),
         (p_final_g, final_g, row(m_final_norm_g), row(v_final_norm_g))],
        p_loss)
    res["meta_tokens"], res["norm_g"], bf, pw, res["pool_scale"], fg = small
    res["b_forget"] = [a[:, :N_HEADS] for a in bf]
    res["pool_w"] = [a.reshape(pool_w.shape) for a in pw]
    res["final_norm_g"] = [a.reshape(D_MODEL) for a in fg]
    loss = loss_row[0, 0]
    for name in ("w_in", "w_up_pool", "w_up_attn", "w_out"):
        res[name] = [a[None] for a in res[name]]

    order = ["meta_tokens", "norm_g", "w_in", "b_forget", "pool_w", "pool_scale", "w_up_pool", "w_up_attn", "w_out",
             "final_norm_g"]
    outs = [loss, grad_x[None]]
    for part in range(4):
        outs += [res[name][part] for name in order]
    return tuple(outs)
```

```python
import functools

import jax
import jax.numpy as jnp
from jax import lax
from jax.experimental import pallas as pl
from jax.experimental.pallas import tpu as pltpu

F32 = jnp.float32
BF16 = jnp.bfloat16

D_MODEL = 1024
N_META = 16
POOL_WIDTH = 512
ATTN_WIDTH = 512
N_HEADS = 8
HEAD_DIM = 64
POOL_WINDOWS = (2, 4, 8, 16)
POOL_GROUP = 128
MAX_WINDOW = 16
RMS_EPS = 1e-6
N_MAIN = 5120
N_BEFORE_F = 3072
N_DEV = 8
LANES = 128

ROW_TILE = 256
ATT_TILE = 256
ATT_Q_BLOCKS = 4
PAD = ROW_TILE - N_META
VMEM_LIMIT = 56 * 1024 * 1024

ADAM_LR = 0.001
ADAM_B1 = 0.9
ADAM_B2 = 0.999
ADAM_EPS = 1e-08
ADAM_WD = 0.01
ADAM_STEP = 10

NEG = -1e30
MESH = pl.DeviceIdType.MESH


def _params(sem=None):
    kw = dict(vmem_limit_bytes=VMEM_LIMIT)
    if sem is not None:
        kw["dimension_semantics"] = sem
    return pltpu.CompilerParams(**kw)


def _const(shape, block_index=None):
    idx = block_index or (0,) * len(shape)
    return pl.BlockSpec(shape, lambda i: idx, pipeline_mode=pl.Buffered(1))


def _sigmoid(x):
    return jax.nn.sigmoid(x)


def _dot(a, b):
    return jnp.dot(a, b, preferred_element_type=F32)


def _dot_nt(a, b):
    return lax.dot_general(a, b, (((1,), (1,)), ((), ())), preferred_element_type=F32)


def _dot_tn(a, b):
    return lax.dot_general(a, b, (((0,), (0,)), ((), ())), preferred_element_type=F32)


def _pool_counts(first_row, rows):
    row = first_row + lax.broadcasted_iota(jnp.int32, (rows, 1), 0)
    pos1 = row - PAD + 1
    return [jnp.clip(pos1, 1, w).astype(F32) for w in POOL_WINDOWS]


def _pool_means(u_ext, u, counts):
    rows = u.shape[0]
    out = []
    for g, w in enumerate(POOL_WINDOWS):
        s = u_ext[:, POOL_GROUP * g:POOL_GROUP * (g + 1)]
        sh = 1
        while sh < w:
            s = s + pltpu.roll(s, sh, axis=0)
            sh *= 2
        out.append(s[MAX_WINDOW:MAX_WINDOW + rows, :] / counts[g] - u[:, POOL_GROUP * g:POOL_GROUP * (g + 1)])
    return out


Q_BIAS, Q_ONES, Q_LSE = 64, 67, 70
K_ONES, K_BIAS, K_ONES2 = 64, 67, 70
V_ONES = 64
DO_BIAS = 64


def _lane_ones(lane, ranges):
    hit = None
    for lo, hi in ranges:
        r = (lane >= lo) & (lane < hi)
        hit = r if hit is None else hit | r
    return jnp.where(hit, 1.0, 0.0)


def _put3(base, lane, first, x):
    hi = x.astype(BF16).astype(F32)
    rest = x - hi
    mid = rest.astype(BF16).astype(F32)
    lo = (rest - mid).astype(BF16).astype(F32)
    for j, piece in enumerate((hi, mid, lo)):
        base = jnp.where(lane == first + j, piece, base)
    return base


SIBLING = 1
SAME_CORE = (2, 4, 6)
ALL_PEERS = (1, 2, 3, 4, 5, 6, 7)


def _place():
    return lax.axis_index("x"), lax.axis_index("y"), lax.axis_index("c")


def _peer(r):
    x, y, c = _place()
    return (1 - x if r & 4 else x, 1 - y if r & 2 else y, 1 - c if r & 1 else c)


def _device_slot(p):
    return 4 * p[0] + 2 * p[1] + p[2]


def _chip_slot(p):
    return 2 * p[0] + p[1]


def _exchange(name, items):
    n = len(items)

    def body(*refs):
        copies = _exchange_copies(items, refs[:n], refs[n:2 * n], *refs[2 * n:])
        for cp in copies:
            cp.start()
        for cp in copies:
            cp.wait()

    hbm = pl.BlockSpec(memory_space=pl.ANY)
    return pl.pallas_call(
        body, name=name, out_shape=_exchange_results(items),
        in_specs=[hbm] * n, out_specs=[hbm] * n,
        scratch_shapes=_exchange_semaphores(n),
    )(*[a for _, a, _ in items])


def _exchange_results(items):
    return [jax.ShapeDtypeStruct(((N_DEV,) if kind == "gather" else ()) + a.shape, a.dtype) for kind, a, _ in items]


def _exchange_semaphores(n):
    return [pltpu.SemaphoreType.DMA((n, N_DEV - 1)), pltpu.SemaphoreType.DMA((n, N_DEV - 1)),
            pltpu.SemaphoreType.DMA((n,))]


def _exchange_copies(items, ins, outs, send_sems, recv_sems, local_sems):
    me = _place()
    copies = []
    for a, (kind, _, peers) in enumerate(items):
        slot = _chip_slot if kind == "chips" else _device_slot
        for r in peers:
            peer = _peer(r)
            src = ins[a] if kind in ("swap", "gather") else ins[a].at[slot(peer)]
            dst = outs[a] if kind == "swap" else outs[a].at[slot(me)]
            copies.append(pltpu.make_async_remote_copy(
                src_ref=src, dst_ref=dst, send_sem=send_sems.at[a, r - 1], recv_sem=recv_sems.at[a, r - 1],
                device_id=peer, device_id_type=MESH))
        if kind != "swap":
            src = ins[a] if kind == "gather" else ins[a].at[slot(me)]
            copies.append(pltpu.make_async_copy(src, outs[a].at[slot(me)], local_sems.at[a]))
    return copies


def _gather_two_level(name, arrays):
    n = len(arrays)

    def body(*refs):
        ins, outs = refs[:n], refs[n:2 * n]
        send_sems, recv_sems, local_sems = refs[2 * n:]
        me = _place()
        sibling = _peer(SIBLING)

        def copy(a, k, block, to, src=None):
            rows = outs[a].at[_device_slot(block)]
            return pltpu.make_async_remote_copy(
                src_ref=rows if src is None else src, dst_ref=rows,
                send_sem=send_sems.at[a, k], recv_sem=recv_sems.at[a, k], device_id=to, device_id_type=MESH)

        sends, own = [], []
        for a in range(n):
            mine = pltpu.make_async_copy(ins[a], outs[a].at[_device_slot(me)], local_sems.at[a])
            mine.start()
            own.append(mine)
            for k, r in enumerate((SIBLING,) + SAME_CORE):
                cp = copy(a, k, me, _peer(r), src=ins[a])
                cp.start()
                sends.append(cp)
        for a in range(n):
            for j, r in enumerate(SAME_CORE):
                copy(a, 1 + j, _peer(r), me).wait_recv()
                passed = copy(a, 4 + j, _peer(r), sibling)
                passed.start()
                sends.append(passed)
        for a in range(n):
            copy(a, 0, sibling, me).wait_recv()
            for j, r in enumerate(SAME_CORE):
                copy(a, 4 + j, _peer(r | SIBLING), me).wait_recv()
        for cp in sends:
            cp.wait_send()
        for cp in own:
            cp.wait()

    hbm = pl.BlockSpec(memory_space=pl.ANY)
    return pl.pallas_call(
        body, name=name, out_shape=[jax.ShapeDtypeStruct((N_DEV,) + a.shape, a.dtype) for a in arrays],
        in_specs=[hbm] * n, out_specs=[hbm] * n,
        scratch_shapes=[pltpu.SemaphoreType.DMA((n, N_DEV - 1)), pltpu.SemaphoreType.DMA((n, N_DEV - 1)),
                        pltpu.SemaphoreType.DMA((n,))],
    )(*arrays)


def _forward_in(x, tile0, norm_g, w_main, w_f, b_f, pool_w, pool_scale, w_up_pool):
    seq = x.shape[0]
    nt = seq // ROW_TILE + 1
    lp = nt * ROW_TILE
    tm = ROW_TILE

    def body(x_ref, t0_ref, g_ref, wa_ref, wf_ref, bf_ref, pw_ref, sc_ref, wup_ref,
             h_ref, u_ref, zp_ref, k_ref, v_ref, qt_ref, kt_ref, vt_ref, sn_ref, ap_ref,
             uext_ref, carry_ref):
        i = pl.program_id(0)

        @pl.when(i == 0)
        def _():
            uext_ref[...] = jnp.zeros_like(uext_ref)
            carry_ref[...] = jnp.zeros_like(carry_ref)

        xt = jnp.where(i == 0, t0_ref[...], x_ref[...])
        r = lax.rsqrt(jnp.mean(xt * xt, axis=-1, keepdims=True) + RMS_EPS)
        h = (xt * r * g_ref[...]).astype(BF16)
        h_ref[...] = h
        pa = _dot_nt(h, wa_ref[...])
        u = pa[:, :512]
        zp = pa[:, 512:1024]
        u_ref[...] = u
        zp_ref[...] = zp

        uext_ref[0:MAX_WINDOW, :] = uext_ref[tm:tm + MAX_WINDOW, :]
        uext_ref[MAX_WINDOW:, :] = u
        counts = _pool_counts(i * tm, tm)
        ps = _pool_means(uext_ref[...], u, counts)
        ppw = jnp.concatenate([_dot(ps[g].astype(BF16), pw_ref[g]) for g in range(4)], axis=1)
        y_pool = ppw * sc_ref[...] * (zp * _sigmoid(zp))
        ap_ref[...] = _dot(y_pool.astype(BF16), wup_ref[...]).astype(BF16)

        fl = _dot_nt(h, wf_ref[...]) + bf_ref[...]
        row = i * tm + lax.broadcasted_iota(jnp.int32, (tm, LANES), 0)
        rloc = lax.broadcasted_iota(jnp.int32, (tm, LANES), 0)
        lane = lax.broadcasted_iota(jnp.int32, (tm, LANES), 1)
        live = (row >= PAD) & (lane < N_HEADS)
        logf = jnp.minimum(fl, 0.0) - jnp.log1p(jnp.exp(-jnp.abs(fl)))
        cs = jnp.where(live, logf, 0.0)
        sh = 1
        while sh < tm:
            cs = cs + jnp.where(rloc >= sh, pltpu.roll(cs, sh, axis=0), 0.0)
            sh *= 2
        cs = cs + carry_ref[...]
        carry_ref[...] = cs[tm - 1:tm, :]
        sn_ref[...] = jnp.where(live, _sigmoid(-fl), 0.0)

        rows1 = i * tm + lax.broadcasted_iota(jnp.int32, (tm, 1), 0)
        ones_q = _lane_ones(lane, ((Q_ONES, Q_ONES + 3),))
        ones_k = _lane_ones(lane, ((K_ONES, K_ONES + 3), (K_ONES2, K_ONES2 + 3)))
        ones_v = _lane_ones(lane, ((V_ONES, V_ONES + 3),))
        for hp in range(N_HEADS // 2):
            qp = pa[:, 1024 + LANES * hp:1024 + LANES * (hp + 1)] * 0.125
            kp = pa[:, 1536 + LANES * hp:1536 + LANES * (hp + 1)]
            vp = pa[:, 2048 + LANES * hp:2048 + LANES * (hp + 1)]
            for e in range(2):
                head = 2 * hp + e
                if e:
                    qp, kp, vp = (pltpu.roll(a, HEAD_DIM, axis=1) for a in (qp, kp, vp))
                c_h = cs[:, head:head + 1]
                q_h = jnp.where(lane < HEAD_DIM, qp, _put3(ones_q, lane, Q_BIAS, c_h))
                qt_ref[head] = q_h.T.astype(BF16)
                minus_ck = jnp.where(rows1 >= PAD, -c_h, NEG)
                k_h = jnp.where(lane < HEAD_DIM, kp, _put3(ones_k, lane, K_BIAS, minus_ck))
                k_ref[head] = k_h.astype(BF16)
                kt_ref[head] = k_h.T.astype(BF16)
                v_h = jnp.where(lane < HEAD_DIM, vp, ones_v)
                v_ref[head] = v_h.astype(BF16)
                vt_ref[head] = v_h.T.astype(BF16)

    row_f32 = lambda w: pl.BlockSpec((tm, w), lambda i: (i, 0))
    out_shape = [
        jax.ShapeDtypeStruct((lp, D_MODEL), BF16),
        jax.ShapeDtypeStruct((lp, POOL_WIDTH), F32),
        jax.ShapeDtypeStruct((lp, POOL_WIDTH), F32),
        jax.ShapeDtypeStruct((N_HEADS, lp, LANES), BF16),
        jax.ShapeDtypeStruct((N_HEADS, lp, LANES), BF16),
        jax.ShapeDtypeStruct((N_HEADS, LANES, lp), BF16),
        jax.ShapeDtypeStruct((N_HEADS, LANES, lp), BF16),
        jax.ShapeDtypeStruct((N_HEADS, LANES, lp), BF16),
        jax.ShapeDtypeStruct((lp, LANES), F32),
        jax.ShapeDtypeStruct((lp, D_MODEL), BF16),
    ]
    heads = pl.BlockSpec((N_HEADS, tm, LANES), lambda i: (0, i, 0))
    heads_t = pl.BlockSpec((N_HEADS, LANES, tm), lambda i: (0, 0, i))
    out_specs = [row_f32(D_MODEL), row_f32(512), row_f32(512), heads, heads, heads_t, heads_t, heads_t,
                 row_f32(LANES), row_f32(D_MODEL)]
    in_specs = [
        pl.BlockSpec((tm, D_MODEL), lambda i: (jnp.maximum(i - 1, 0), 0)),
        _const((tm, D_MODEL)), _const((1, D_MODEL)),
        _const((2560, D_MODEL)), _const((LANES, D_MODEL)), _const((1, LANES)),
        _const((4, POOL_GROUP, POOL_GROUP)), _const((1, POOL_WIDTH)), _const((POOL_WIDTH, D_MODEL)),
    ]
    return pl.pallas_call(
        body, name="forward_in", grid=(nt,), out_shape=out_shape, in_specs=in_specs, out_specs=out_specs,
        scratch_shapes=[pltpu.VMEM((tm + MAX_WINDOW, POOL_WIDTH), F32), pltpu.VMEM((1, LANES), F32)],
        compiler_params=_params(("arbitrary",)),
    )(x, tile0, norm_g, w_main, w_f, b_f, pool_w, pool_scale, w_up_pool)


def _causal(tb):
    return lax.broadcasted_iota(jnp.int32, (tb, tb), 1) <= lax.broadcasted_iota(jnp.int32, (tb, tb), 0)


def _pair_lanes(a0, a1):
    lane = lax.broadcasted_iota(jnp.int32, a0.shape, 1)
    return jnp.where(lane < HEAD_DIM, a0, pltpu.roll(a1, HEAD_DIM, axis=1))


def _behind(items, ins, outs, sems):
    step, last = pl.program_id(0), pl.num_programs(0) - 1

    @pl.when(step == 0)
    def _():
        for cp in _exchange_copies(items, ins, outs, *sems):
            cp.start()

    def finish():
        @pl.when(step == last)
        def _():
            for cp in _exchange_copies(items, ins, outs, *sems):
                cp.wait()

    return finish


def _attention_forward(qt, k, vt, behind):
    lp = k.shape[1]
    tk = ATT_TILE
    tq_big = ATT_Q_BLOCKS * tk
    n_big = (lp // tk - 1) // ATT_Q_BLOCKS
    assert lp == tk + n_big * tq_big and ATT_Q_BLOCKS % 2 == 0
    nx = len(behind)

    def body(qt_ref, k_ref, vt_ref, *rest):
        o_ref, lse_ref = rest[nx:nx + 2]
        s_buf, m_scr, acc_scr = rest[2 * nx + 2:2 * nx + 5]
        finish_exchange = _behind(behind, rest[:nx], rest[nx + 2:2 * nx + 2], rest[2 * nx + 5:])

        def q_tile(q0, tq, pairs):
            first = q0 // tk
            qts = [qt_ref[e, :, pl.ds(q0, tq)] for e in range(2)]

            def block(kj):
                return pl.ds(kj * tk if isinstance(kj, int) else pl.multiple_of(kj * tk, tk), tk)

            def step(kj, rd, wr, c0=0, diagonal=False):
                c1 = c0 + tk if diagonal else c0
                for e in range(2):
                    s = s_buf[rd, e, :, c0:tq]
                    if wr is not None:
                        s_buf[wr, e, :, c1:tq] = _dot(k_ref[e, block(kj + 1), :], qts[e][:, c1:tq])
                    if diagonal:
                        keys = lax.broadcasted_iota(jnp.int32, s.shape, 0)
                        s = jnp.where(keys <= lax.broadcasted_iota(jnp.int32, s.shape, 1), s, NEG)
                    m = m_scr[e, :, c0:tq]
                    m_new = jnp.maximum(m, jnp.max(s, axis=0, keepdims=True))
                    p = jnp.exp(s - m_new)
                    pv = _dot(vt_ref[e, :, block(kj)], p.astype(BF16))
                    acc_scr[e, :, c0:tq] = jnp.exp(m - m_new) * acc_scr[e, :, c0:tq] + pv
                    m_scr[e, :, c0:tq] = m_new

            for e in range(2):
                m_scr[e, :, 0:tq] = jnp.full((1, tq), NEG, F32)
                acc_scr[e, :, 0:tq] = jnp.zeros((LANES, tq), F32)
                s_buf[0, e, :, 0:tq] = _dot(k_ref[e, block(0), :], qts[e])
            if pairs is None:
                step(0, 0, None, 0, True)
            else:
                step(0, 0, 1)

                def two_steps(t, _):
                    step(1 + 2 * t, 1, 0)
                    step(2 + 2 * t, 0, 1)
                    return 0

                lax.fori_loop(0, pairs, two_steps, 0)
                for b in range(tq // tk):
                    step(first + b, (b + 1) % 2, b % 2 if (b + 1) * tk < tq else None, b * tk, True)
            outs, lses = [], []
            for e in range(2):
                acc = acc_scr[e, :, 0:tq]
                l = acc[V_ONES:V_ONES + 1, :]
                outs.append((acc / l).T)
                lses.append(m_scr[e, :, 0:tq] + jnp.log(l))
            o_ref[pl.ds(q0, tq), :] = _pair_lanes(outs[0], outs[1]).astype(BF16)
            lse_rows = jnp.concatenate(lses + [jnp.zeros((LANES - 2, tq), F32)], axis=0)
            lse_ref[pl.ds(q0, tq), :] = lse_rows.T

        q_tile(0, tk, None)

        def big_tile(i, _):
            q_tile(pl.multiple_of(tk + i * tq_big, tk), tq_big, (ATT_Q_BLOCKS // 2) * i)
            return 0

        lax.fori_loop(0, n_big, big_tile, 0)
        finish_exchange()

    pair = pl.BlockSpec((lp, LANES), lambda hp: (0, hp))
    heads = pl.BlockSpec((2, lp, LANES), lambda hp: (hp, 0, 0), pipeline_mode=pl.Buffered(1))
    heads_t = pl.BlockSpec((2, LANES, lp), lambda hp: (hp, 0, 0), pipeline_mode=pl.Buffered(1))
    hbm = pl.BlockSpec(memory_space=pl.ANY)
    return pl.pallas_call(
        body, name="attention_forward", grid=(N_HEADS // 2,),
        out_shape=[jax.ShapeDtypeStruct((lp, ATTN_WIDTH), BF16), jax.ShapeDtypeStruct((lp, ATTN_WIDTH), F32)]
        + _exchange_results(behind),
        in_specs=[heads_t, heads, heads_t] + [hbm] * nx,
        out_specs=[pair, pair] + [hbm] * nx,
        scratch_shapes=[pltpu.VMEM((2, 2, tk, tq_big), F32), pltpu.VMEM((2, 1, tq_big), F32),
                        pltpu.VMEM((2, LANES, tq_big), F32)] + _exchange_semaphores(nx),
        compiler_params=_params(("arbitrary",)),
    )(qt, k, vt, *[a for _, a, _ in behind])


def _rows3(first, x):
    sub = lax.broadcasted_iota(jnp.int32, (LANES, x.shape[1]), 0)
    hi = x.astype(BF16).astype(F32)
    rest = x - hi
    mid = rest.astype(BF16).astype(F32)
    lo = (rest - mid).astype(BF16).astype(F32)
    out = jnp.zeros((LANES, x.shape[1]), F32)
    for j, piece in enumerate((hi, mid, lo)):
        out = jnp.where(sub == first + j, piece, out)
    return out


def _attention_backward(qt, k, kt, v, do, o, lse, behind):
    lp = k.shape[1]
    tb = ATT_TILE
    nb = lp // tb
    tq_big = ATT_Q_BLOCKS * tb
    n_big = (nb - 1) // ATT_Q_BLOCKS
    assert lp == tb + n_big * tq_big and ATT_Q_BLOCKS % 2 == 0
    nx = len(behind)

    def body(qt_ref, k_ref, kt_ref, v_ref, do_ref, o_ref, lse_ref, *rest):
        dq_ref, dk_ref, dv_ref, dc_ref = rest[nx:nx + 4]
        q2_ref, do2_ref, dk_acc, dv_acc, dq_scr, s_buf = rest[2 * nx + 4:2 * nx + 10]
        finish_exchange = _behind(behind, rest[:nx], rest[nx + 4:2 * nx + 4], rest[2 * nx + 10:])
        sub = lax.broadcasted_iota(jnp.int32, (LANES, tb), 0)

        def lanes01(row0, row1):
            n = row0.shape[1]
            return jnp.concatenate([row0, row1, jnp.zeros((LANES - 2, n), F32)], axis=0).T

        def prepare(bi, _):
            r0 = pl.multiple_of(bi * tb, tb)
            queries = r0 + lax.broadcasted_iota(jnp.int32, (1, tb), 1)
            dob = do_ref[pl.ds(r0, tb), :].astype(F32)
            do_t = dob.T
            dd_t = (dob * o_ref[pl.ds(r0, tb), :].astype(F32)).T
            lse_t = lse_ref[pl.ds(r0, tb), :].T
            for e in range(2):
                delta = jnp.sum(dd_t[HEAD_DIM * e:HEAD_DIM * (e + 1), :], axis=0, keepdims=True)
                do_e = jnp.concatenate([do_t[HEAD_DIM * e:HEAD_DIM * (e + 1), :], jnp.zeros((HEAD_DIM, tb), F32)], axis=0)
                do2_ref[e, :, pl.ds(r0, tb)] = jnp.where(sub < HEAD_DIM, do_e, _rows3(DO_BIAS, -delta)).astype(BF16)
                minus_lse = jnp.where(queries >= PAD, -lse_t[e:e + 1, :], NEG)
                keep = (sub < Q_LSE) | (sub >= Q_LSE + 3)
                q2_ref[e, :, pl.ds(r0, tb)] = jnp.where(keep, qt_ref[e, :, pl.ds(r0, tb)].astype(F32),
                                                        _rows3(Q_LSE, minus_lse)).astype(BF16)
            return 0

        lax.fori_loop(0, nb, prepare, 0)
        dk_acc[...] = jnp.zeros_like(dk_acc)
        dv_acc[...] = jnp.zeros_like(dv_acc)

        def q_tile(q0, tq, pairs):
            first = q0 // tb
            qts = [q2_ref[e, :, pl.ds(q0, tq)] for e in range(2)]
            dots = [do2_ref[e, :, pl.ds(q0, tq)] for e in range(2)]

            def block(kj):
                return pl.ds(kj * tb if isinstance(kj, int) else pl.multiple_of(kj * tb, tb), tb)

            def step(kj, rd, wr, c0=0, diagonal=False):
                c1 = c0 + tb if diagonal else c0
                for e in range(2):
                    s = s_buf[rd, e, :, c0:tq]
                    if wr is not None:
                        s_buf[wr, e, :, c1:tq] = _dot(k_ref[e, block(kj + 1), :], qts[e][:, c1:tq])
                    dpd = _dot(v_ref[e, block(kj), :], dots[e][:, c0:tq])
                    p = jnp.exp(s)
                    if diagonal:
                        keys = lax.broadcasted_iota(jnp.int32, s.shape, 0)
                        p = jnp.where(keys <= lax.broadcasted_iota(jnp.int32, s.shape, 1), p, 0.0)
                    dsb = (p * dpd).astype(BF16)
                    dv_acc[e, :, block(kj)] += _dot_nt(dots[e][:, c0:tq], p.astype(BF16))
                    dk_acc[e, :, block(kj)] += _dot_nt(qts[e][:, c0:tq], dsb)
                    dq_scr[e, :, c0:tq] += _dot(kt_ref[e, :, block(kj)], dsb)

            for e in range(2):
                dq_scr[e, :, 0:tq] = jnp.zeros((LANES, tq), F32)
                s_buf[0, e, :, 0:tq] = _dot(k_ref[e, block(0), :], qts[e])
            if pairs is None:
                step(0, 0, None, 0, True)
            else:
                step(0, 0, 1)

                def two_steps(t, _):
                    step(1 + 2 * t, 1, 0)
                    step(2 + 2 * t, 0, 1)
                    return 0

                lax.fori_loop(0, pairs, two_steps, 0)
                for b in range(tq // tb):
                    step(first + b, (b + 1) % 2, b % 2 if (b + 1) * tb < tq else None, b * tb, True)
            dq0, dq1 = dq_scr[0, :, 0:tq], dq_scr[1, :, 0:tq]
            dq_ref[pl.ds(q0, tq), :] = (_pair_lanes(dq0.T, dq1.T) * 0.125).astype(BF16)
            dc_ref[pl.ds(q0, tq), :] = lanes01(dq0[K_ONES:K_ONES + 1, :], dq1[K_ONES:K_ONES + 1, :])

        q_tile(0, tb, None)

        def big_tile(i, _):
            q_tile(pl.multiple_of(tb + i * tq_big, tb), tq_big, (ATT_Q_BLOCKS // 2) * i)
            return 0

        lax.fori_loop(0, n_big, big_tile, 0)

        def finish(bi, _):
            r0 = pl.multiple_of(bi * tb, tb)
            dk0, dk1 = dk_acc[0, :, pl.ds(r0, tb)], dk_acc[1, :, pl.ds(r0, tb)]
            dk_ref[pl.ds(r0, tb), :] = _pair_lanes(dk0.T, dk1.T).astype(BF16)
            dv_ref[pl.ds(r0, tb), :] = _pair_lanes(dv_acc[0, :, pl.ds(r0, tb)].T, dv_acc[1, :, pl.ds(r0, tb)].T).astype(BF16)
            dc_ref[pl.ds(r0, tb), :] = dc_ref[pl.ds(r0, tb), :] - lanes01(dk0[Q_ONES:Q_ONES + 1, :], dk1[Q_ONES:Q_ONES + 1, :])
            return 0

        lax.fori_loop(0, nb, finish, 0)
        finish_exchange()

    once = pl.Buffered(1)
    pair = pl.BlockSpec((lp, LANES), lambda hp: (0, hp))
    pair_in = pl.BlockSpec((lp, LANES), lambda hp: (0, hp), pipeline_mode=once)
    heads = pl.BlockSpec((2, lp, LANES), lambda hp: (hp, 0, 0), pipeline_mode=once)
    heads_t = pl.BlockSpec((2, LANES, lp), lambda hp: (hp, 0, 0), pipeline_mode=once)
    hbm = pl.BlockSpec(memory_space=pl.ANY)
    wide = jax.ShapeDtypeStruct((lp, ATTN_WIDTH), BF16)
    return pl.pallas_call(
        body, name="attention_backward", grid=(N_HEADS // 2,),
        out_shape=[wide, wide, wide, jax.ShapeDtypeStruct((lp, ATTN_WIDTH), F32)] + _exchange_results(behind),
        in_specs=[heads_t, heads, heads_t, heads, pair_in, pair_in, pair_in] + [hbm] * nx,
        out_specs=[pair, pair, pair, pair] + [hbm] * nx,
        scratch_shapes=[pltpu.VMEM((2, LANES, lp), BF16), pltpu.VMEM((2, LANES, lp), BF16),
                        pltpu.VMEM((2, LANES, lp), F32), pltpu.VMEM((2, LANES, lp), F32),
                        pltpu.VMEM((2, LANES, tq_big), F32), pltpu.VMEM((2, 2, tb, tq_big), F32)]
        + _exchange_semaphores(nx),
        compiler_params=_params(("arbitrary",)),
    )(qt, k, kt, v, do, o, lse, *[a for _, a, _ in behind])


def _middle(x, target, h, o, a_pool, u, zp, w_main, w_up_pool, w_up_attn, w_out, pool_w, pool_scale, final_g):
    seq = x.shape[0]
    tm = ROW_TILE
    nt = seq // tm + 1
    lp = nt * tm
    halo_blocks = tm // MAX_WINDOW

    def body(x_ref, t_ref, h_ref, o_ref, ap_ref, u_ref, uh_ref, zp_ref,
             wc_ref, wupp_ref, wupa_ref, wout_ref, pw_ref, sc_ref, gf_ref,
             dh2_ref, mg_ref, yp_ref, ya_ref, dap_ref, daa_ref, do_ref, dza_ref, dgp_ref, dga_ref, dzp_ref, dpn_ref,
             loss_ref, dgf_ref, dsc_ref, dpw_ref):
        i = pl.program_id(0)
        tiles = (dh2_ref, mg_ref, yp_ref, ya_ref, dap_ref, daa_ref, do_ref, dza_ref, dgp_ref, dga_ref, dzp_ref, dpn_ref)

        @pl.when(i == 0)
        def _():
            for ref in tiles + (loss_ref, dgf_ref, dsc_ref, dpw_ref):
                ref[...] = jnp.zeros_like(ref)

        @pl.when(i > 0)
        def _():
            xt = x_ref[...]
            hb = h_ref[...]
            pc = _dot_nt(hb, wc_ref[...])
            za, gp, ga = pc[:, :512], pc[:, 512:1536], pc[:, 1536:]
            of = o_ref[...].astype(F32)
            sza = _sigmoid(za)
            silu_za = za * sza
            ya = (of * silu_za).astype(BF16)
            ya_ref[...] = ya
            aa = _dot(ya, wupa_ref[...])
            ap = ap_ref[...].astype(F32)
            sgp, sga = _sigmoid(gp), _sigmoid(ga)
            mg = (sgp * ap + sga * aa).astype(BF16)
            mg_ref[...] = mg
            h2 = xt + _dot(mg, wout_ref[...])
            r2 = lax.rsqrt(jnp.mean(h2 * h2, axis=-1, keepdims=True) + RMS_EPS)
            h2n = h2 * r2
            gf = gf_ref[...]
            diff = h2n * gf - t_ref[...]
            loss_ref[...] += 0.5 * jnp.sum(jnp.mean(diff * diff, axis=-1, keepdims=True), axis=0, keepdims=True)
            dy = diff * (1.0 / D_MODEL)
            dgf_ref[...] += jnp.sum(dy * h2n, axis=0, keepdims=True)
            dyg = dy * gf
            dh2 = r2 * (dyg - h2n * jnp.mean(dyg * h2n, axis=-1, keepdims=True))
            dh2_ref[...] = dh2
            dmg = _dot_nt(dh2.astype(BF16), wout_ref[...])
            dap = (dmg * sgp).astype(BF16)
            daa = (dmg * sga).astype(BF16)
            dap_ref[...] = dap
            daa_ref[...] = daa
            dgp_ref[...] = (dmg * ap * sgp * (1.0 - sgp)).astype(BF16)
            dga_ref[...] = (dmg * aa * sga * (1.0 - sga)).astype(BF16)
            dyp = _dot_nt(dap, wupp_ref[...])
            dya = _dot_nt(daa, wupa_ref[...])
            do_ref[...] = (dya * silu_za).astype(BF16)
            dza_ref[...] = (dya * of * (sza * (1.0 + za * (1.0 - sza)))).astype(BF16)

            u = u_ref[...]
            zp = zp_ref[...]
            counts = _pool_counts(i * tm, tm)
            ps = _pool_means(jnp.concatenate([uh_ref[...], u], axis=0), u, counts)
            pbs = [p.astype(BF16) for p in ps]
            ppw = jnp.concatenate([_dot(pbs[g], pw_ref[g]) for g in range(4)], axis=1)
            sc = sc_ref[...]
            szp = _sigmoid(zp)
            silu_zp = zp * szp
            ypre = ppw * sc
            yp_ref[...] = (ypre * silu_zp).astype(BF16)
            dypre = dyp * silu_zp
            dzp_ref[...] = (dyp * ypre * (szp * (1.0 + zp * (1.0 - szp)))).astype(BF16)
            dsc_ref[...] += jnp.sum(dypre * ppw, axis=0, keepdims=True)
            dppw = (dypre * sc).astype(BF16)
            dpns = []
            for g in range(4):
                dg = dppw[:, POOL_GROUP * g:POOL_GROUP * (g + 1)]
                dpw_ref[g] += _dot_tn(pbs[g], dg)
                dpns.append(_dot_nt(dg, pw_ref[g]) / counts[g])
            dpn_ref[...] = jnp.concatenate(dpns, axis=1)

    real = lambda w: pl.BlockSpec((tm, w), lambda i: (jnp.maximum(i - 1, 0), 0))
    row = lambda w: pl.BlockSpec((tm, w), lambda i: (i, 0))
    in_specs = [
        real(D_MODEL), real(D_MODEL), row(D_MODEL), row(512), row(D_MODEL), row(512),
        pl.BlockSpec((MAX_WINDOW, 512), lambda i: (jnp.maximum(i * halo_blocks - 1, 0), 0)), row(512),
        _const((2560, D_MODEL), (1, 0)), _const((POOL_WIDTH, D_MODEL)), _const((ATTN_WIDTH, D_MODEL)),
        _const((D_MODEL, D_MODEL)), _const((4, POOL_GROUP, POOL_GROUP)), _const((1, POOL_WIDTH)), _const((1, D_MODEL)),
    ]
    sd = jax.ShapeDtypeStruct
    out_shape = [
        sd((lp, D_MODEL), F32),
        sd((lp, D_MODEL), BF16),
        sd((lp, 512), BF16),
        sd((lp, 512), BF16),
        sd((lp, D_MODEL), BF16),
        sd((lp, D_MODEL), BF16),
        sd((lp, 512), BF16),
        sd((lp, 512), BF16),
        sd((lp, D_MODEL), BF16),
        sd((lp, D_MODEL), BF16),
        sd((lp, 512), BF16),
        sd((lp, 512), F32),
        sd((1, LANES), F32),
        sd((1, D_MODEL), F32),
        sd((1, 512), F32),
        sd((4, POOL_GROUP, POOL_GROUP), F32),
    ]
    keep = lambda shape: pl.BlockSpec(shape, lambda i: (0,) * len(shape))
    out_specs = [row(D_MODEL), row(D_MODEL), row(512), row(512), row(D_MODEL), row(D_MODEL), row(512), row(512),
                 row(D_MODEL), row(D_MODEL), row(512), row(512),
                 keep((1, LANES)), keep((1, D_MODEL)), keep((1, 512)), keep((4, POOL_GROUP, POOL_GROUP))]
    return pl.pallas_call(
        body, name="middle", grid=(nt,), out_shape=out_shape, in_specs=in_specs, out_specs=out_specs,
        compiler_params=_params(("arbitrary",)),
    )(x, target, h, o, a_pool, u, u, zp, w_main, w_up_pool, w_up_attn, w_out, pool_w, pool_scale, final_g)


def _backward_in(x, tile0, norm_g, dh2, dpn, dzp, dq, dk, dv, dza, dgp, dga, dc, sneg, w_main, w_f):
    seq = x.shape[0]
    tm = ROW_TILE
    nt = seq // tm + 1
    lp = nt * tm
    halo_blocks = tm // MAX_WINDOW
    last_halo = lp // MAX_WINDOW - 1

    def body(x_ref, t0_ref, g_ref, dh2_ref, dpn_ref, dpnh_ref, dzp_ref, dq_ref, dk_ref, dv_ref, dza_ref,
             dgp_ref, dga_ref, dc_ref, sn_ref, wm_ref, wf_ref,
             dproj_ref, df_ref, gx_ref, gmeta_ref, dg_ref, dbf_ref, carry_ref):
        i = pl.program_id(0)
        t = nt - 1 - i

        @pl.when(i == 0)
        def _():
            carry_ref[...] = jnp.zeros_like(carry_ref)
            dg_ref[...] = jnp.zeros_like(dg_ref)
            dbf_ref[...] = jnp.zeros_like(dbf_ref)

        dpn_t = dpn_ref[...]
        ahead = jnp.where(i == 0, jnp.zeros_like(dpnh_ref), dpnh_ref[...])
        ext = jnp.concatenate([dpn_t, ahead], axis=0)
        counts = _pool_counts(t * tm, tm)
        for g, w in enumerate(POOL_WINDOWS):
            s = ext[:, POOL_GROUP * g:POOL_GROUP * (g + 1)]
            sh = 1
            while sh < w:
                s = s + pltpu.roll(s, tm + MAX_WINDOW - sh, axis=0)
                sh *= 2
            du = s[:tm, :] - dpn_t[:, POOL_GROUP * g:POOL_GROUP * (g + 1)] * counts[g]
            dproj_ref[:, POOL_GROUP * g:POOL_GROUP * (g + 1)] = du.astype(BF16)
        dproj_ref[:, 512:1024] = dzp_ref[...]
        dproj_ref[:, 1024:1536] = dq_ref[...]
        dproj_ref[:, 1536:2048] = dk_ref[...]
        dproj_ref[:, 2048:2560] = dv_ref[...]
        dproj_ref[:, 2560:3072] = dza_ref[...]
        dproj_ref[:, 3072:4096] = dgp_ref[...]
        dproj_ref[:, 4096:5120] = dga_ref[...]

        dct = dc_ref[:, 0:LANES]
        for hp in range(1, N_HEADS // 2):
            dct = dct + pltpu.roll(dc_ref[:, LANES * hp:LANES * (hp + 1)], 2 * hp, axis=1)
        rloc = lax.broadcasted_iota(jnp.int32, (tm, LANES), 0)
        sh = 1
        while sh < tm:
            dct = dct + jnp.where(rloc + sh < tm, pltpu.roll(dct, tm - sh, axis=0), 0.0)
            sh *= 2
        dct = dct + carry_ref[...]
        carry_ref[...] = dct[0:1, :]
        df = dct * sn_ref[...]
        dbf_ref[...] += jnp.sum(df, axis=0, keepdims=True)
        dfb = df.astype(BF16)
        df_ref[...] = dfb

        dh = _dot(dproj_ref[...], wm_ref[...]) + _dot(dfb, wf_ref[...])
        xt = jnp.where(t == 0, t0_ref[...], x_ref[...])
        r = lax.rsqrt(jnp.mean(xt * xt, axis=-1, keepdims=True) + RMS_EPS)
        xn = xt * r
        dg_ref[...] += jnp.sum(dh * xn, axis=0, keepdims=True)
        dhg = dh * g_ref[...]
        dx = dh2_ref[...] + r * (dhg - xn * jnp.mean(dhg * xn, axis=-1, keepdims=True))

        @pl.when(t > 0)
        def _():
            gx_ref[...] = dx

        @pl.when(t == 0)
        def _():
            gmeta_ref[...] = dx[PAD:, :]

    rev = lambda w: pl.BlockSpec((tm, w), lambda i: (nt - 1 - i, 0))
    real = pl.BlockSpec((tm, D_MODEL), lambda i: (jnp.maximum(nt - 2 - i, 0), 0))
    in_specs = [
        real, _const((tm, D_MODEL)), _const((1, D_MODEL)), rev(D_MODEL), rev(512),
        pl.BlockSpec((MAX_WINDOW, 512), lambda i: (jnp.minimum((nt - i) * halo_blocks, last_halo), 0)),
        rev(512), rev(512), rev(512), rev(512), rev(512), rev(D_MODEL), rev(D_MODEL),
        rev(512), rev(LANES),
        _const((N_MAIN, D_MODEL)), _const((LANES, D_MODEL)),
    ]
    sd = jax.ShapeDtypeStruct
    out_shape = [sd((lp, N_MAIN), BF16), sd((lp, LANES), BF16), sd((seq, D_MODEL), F32), sd((N_META, D_MODEL), F32),
                 sd((1, D_MODEL), F32), sd((1, LANES), F32)]
    keep = lambda shape: pl.BlockSpec(shape, lambda i: (0,) * len(shape))
    out_specs = [rev(N_MAIN), rev(LANES), real, keep((N_META, D_MODEL)), keep((1, D_MODEL)), keep((1, LANES))]
    return pl.pallas_call(
        body, name="backward_in", grid=(nt,), out_shape=out_shape, in_specs=in_specs, out_specs=out_specs,
        scratch_shapes=[pltpu.VMEM((1, LANES), F32)],
        compiler_params=_params(("arbitrary",)),
    )(x, tile0, norm_g, dh2, dpn, dpn, dzp, dq, dk, dv, dza, dgp, dga, dc, sneg, w_main, w_f)


def _matmul_tn(name, a, b, tn):
    lp, m = a.shape
    n = b.shape[1]

    def body(a_ref, b_ref, c_ref):
        c_ref[...] = _dot_tn(a_ref[...].astype(BF16), b_ref[...].astype(BF16))

    return pl.pallas_call(
        body, name=name, grid=(n // tn,), out_shape=jax.ShapeDtypeStruct((m, n), F32),
        in_specs=[_const((lp, m)), pl.BlockSpec((lp, tn), lambda j: (0, j))],
        out_specs=pl.BlockSpec((m, tn), lambda j: (0, j)),
        compiler_params=_params(("arbitrary",)),
    )(a, b)


def _matmul_tn_rows(name, a, b, tm):
    lp, m = a.shape
    n = b.shape[1]

    def body(a_ref, b_ref, c_ref):
        c_ref[...] = _dot_tn(a_ref[...].astype(BF16), b_ref[...].astype(BF16))

    return pl.pallas_call(
        body, name=name, grid=(m // tm,), out_shape=jax.ShapeDtypeStruct((m, n), F32),
        in_specs=[pl.BlockSpec((lp, tm), lambda j: (0, j)), _const((lp, n))],
        out_specs=pl.BlockSpec((tm, n), lambda j: (j, 0)),
        compiler_params=_params(("arbitrary",)),
    )(a, b)


def _adamw_step(p_ref, w_ref, m_ref, v_ref, g_ref, d_ref, mo_ref, vo_ref):
    g = p_ref[0].astype(F32)
    for s in range(1, p_ref.shape[0]):
        g = g + p_ref[s].astype(F32)
    m_new = ADAM_B1 * m_ref[...] + (1.0 - ADAM_B1) * g
    v_new = ADAM_B2 * v_ref[...] + (1.0 - ADAM_B2) * (g * g)
    m_hat = m_new / (1.0 - ADAM_B1 ** ADAM_STEP)
    v_hat = v_new / (1.0 - ADAM_B2 ** ADAM_STEP)
    g_ref[...] = g
    d_ref[...] = -ADAM_LR * (m_hat / (jnp.sqrt(v_hat) + ADAM_EPS) + ADAM_WD * w_ref[...])
    mo_ref[...] = m_new
    vo_ref[...] = v_new


def _adamw_small(name, groups, loss_parts):
    n = len(groups)

    def body(*refs):
        ins, outs = refs[:4 * n + 1], refs[4 * n + 1:]
        for j in range(n):
            _adamw_step(*ins[4 * j:4 * j + 4], *outs[4 * j:4 * j + 4])
        total = ins[-1][0]
        for s in range(1, N_DEV):
            total = total + ins[-1][s]
        outs[-1][...] = total

    vmem = pl.BlockSpec(memory_space=pltpu.VMEM)
    out_shape = [jax.ShapeDtypeStruct(w.shape, F32) for _, w, _, _ in groups for _ in range(4)]
    out_shape.append(jax.ShapeDtypeStruct(loss_parts.shape[1:], F32))
    res = pl.pallas_call(
        body, name=name, out_shape=out_shape, in_specs=[vmem] * (4 * n + 1), out_specs=[vmem] * (4 * n + 1),
        compiler_params=_params(),
    )(*[a for g in groups for a in g], loss_parts)
    return [res[4 * j:4 * j + 4] for j in range(n)], res[-1]


def _adamw(name, parts, w, m, v, rows, cols=None):
    r, c_all = w.shape
    c = cols or c_all
    n_parts = parts.shape[0]

    def body(p_ref, w_ref, m_ref, v_ref, g_ref, d_ref, mo_ref, vo_ref):
        _adamw_step(p_ref, w_ref, m_ref, v_ref, g_ref, d_ref, mo_ref, vo_ref)

    blk = pl.BlockSpec((rows, c), lambda i, j: (i, j))
    return pl.pallas_call(
        body, name=name, grid=(r // rows, c_all // c), out_shape=[jax.ShapeDtypeStruct((r, c_all), F32)] * 4,
        in_specs=[pl.BlockSpec((n_parts, rows, c), lambda i, j: (0, i, j)), blk, blk, blk],
        out_specs=[blk] * 4,
        compiler_params=_params(("arbitrary", "arbitrary")),
    )(parts, w, m, v)


def _pair_sum(name, mine, theirs, rows):
    n, r, c = mine.shape

    def body(a_ref, b_ref, o_ref):
        o_ref[...] = (a_ref[...] + b_ref[...].astype(F32)).astype(BF16)

    blk = pl.BlockSpec((1, rows, c), lambda j, i: (j, i, 0))
    return pl.pallas_call(
        body, name=name, grid=(n, r // rows), out_shape=jax.ShapeDtypeStruct((n, r, c), BF16),
        in_specs=[blk, blk], out_specs=blk,
        compiler_params=_params(("arbitrary", "arbitrary")),
    )(mine, theirs)


def _columns_to_slots(a):
    r, c8 = a.shape
    return a.reshape(r, N_DEV, c8 // N_DEV).transpose(1, 0, 2)


def _by_core(slots):
    by_core = slots.reshape((4, 2) + slots.shape[1:]).swapaxes(0, 1)
    c = lax.axis_index("c")
    return (lax.dynamic_index_in_dim(by_core, c, 0, keepdims=False),
            lax.dynamic_index_in_dim(by_core, 1 - c, 0, keepdims=False).astype(BF16))


def _slots_to_columns(a):
    n, r, c = a.shape
    return a.transpose(1, 0, 2).reshape(r, n * c)


def kernel(x, meta_tokens, norm_g, w_in, b_forget, pool_w, pool_scale, w_up_pool, w_up_attn, w_out, final_norm_g, loss_target, m_meta_tokens, m_norm_g, m_w_in, m_b_forget, m_pool_w, m_pool_scale, m_w_up_pool, m_w_up_attn, m_w_out, m_final_norm_g, v_meta_tokens, v_norm_g, v_w_in, v_b_forget, v_pool_w, v_pool_scale, v_w_up_pool, v_w_up_attn, v_w_out, v_final_norm_g):
    xs = x[0]
    target = loss_target[0]

    g_in, g_upp, g_meta = _gather_two_level(
        "gather_weights", [w_in[0].T.astype(BF16), w_up_pool[0].astype(BF16), meta_tokens])
    w_full = g_in.reshape(N_DEV * g_in.shape[1], D_MODEL)
    w_main = jnp.concatenate([w_full[:N_BEFORE_F], w_full[N_BEFORE_F + N_HEADS:]], axis=0)
    w_f = jnp.pad(w_full[N_BEFORE_F:N_BEFORE_F + N_HEADS], ((0, LANES - N_HEADS), (0, 0)))
    wupp = _slots_to_columns(g_upp)
    meta = _slots_to_columns(g_meta)
    tile0 = jnp.concatenate([jnp.zeros((PAD, D_MODEL), F32), meta], axis=0)
    b_f = jnp.pad(b_forget, ((0, 0), (0, LANES - N_HEADS)))
    pw_b = pool_w[0].astype(BF16)
    final_g = final_norm_g.reshape(1, D_MODEL)

    (h, u, zp, k, v, qt, kt, vt, sneg, a_pool) = _forward_in(xs, tile0, norm_g, w_main, w_f, b_f, pw_b,
                                                              pool_scale, wupp)
    o, lse, g_upa, g_out = _attention_forward(
        qt, k, vt, [("gather", w_up_attn[0].astype(BF16), ALL_PEERS), ("gather", w_out[0].astype(BF16), ALL_PEERS)])
    wupa = _slots_to_columns(g_upa)
    wout = g_out.reshape(D_MODEL, D_MODEL)
    (dh2, mg, yp, ya, dap, daa, do, dza, dgp, dga, dzp, dpn,
     loss_part, d_final_g, d_scale, d_pool_w) = _middle(xs, target, h, o, a_pool, u, zp, w_main, wupp, wupa, wout,
                                                        pw_b, pool_scale, final_g)
    dw_out = _matmul_tn("grad_w_out", mg, dh2, 256)
    dw_upp = _matmul_tn("grad_w_up_pool", yp, dap, 512)
    dw_upa = _matmul_tn("grad_w_up_attn", ya, daa, 512)
    dq, dk, dv, dc, p_upp, p_upa, p_out, p_pool_w, p_scale, p_final_g = _attention_backward(
        qt, k, kt, v, do, o, lse,
        [("scatter", _columns_to_slots(dw_upp).astype(BF16), ALL_PEERS),
         ("scatter", _columns_to_slots(dw_upa).astype(BF16), ALL_PEERS),
         ("scatter", dw_out.reshape(N_DEV, D_MODEL // N_DEV, D_MODEL).astype(BF16), ALL_PEERS),
         ("gather", d_pool_w.reshape(4 * POOL_GROUP, POOL_GROUP), ALL_PEERS),
         ("gather", d_scale, ALL_PEERS), ("gather", d_final_g, ALL_PEERS)])
    dproj, df, grad_x, d_meta, d_norm_g, d_bf = _backward_in(xs, tile0, norm_g, dh2, dpn, dzp, dq, dk, dv, dza,
                                                             dgp, dga, dc, sneg, w_main, w_f)
    dw_main = _matmul_tn_rows("grad_w_in", dproj, h, 512)
    dw_f = _matmul_tn_rows("grad_w_forget", df, h, LANES)
    dw_in = jnp.concatenate([dw_main[:N_BEFORE_F], dw_f[:N_HEADS], dw_main[N_BEFORE_F:]], axis=0)
    dw_in = dw_in.reshape(N_DEV, dw_in.shape[0] // N_DEV, D_MODEL)

    mine, for_sibling = _by_core(dw_in)
    from_sibling, = _exchange("swap_with_sibling", [("swap", for_sibling, (SIBLING,))])
    pair_sums = _pair_sum("pair_sum", mine, from_sibling, dw_in.shape[1])
    p_in, p_meta, p_norm_g, p_bf, p_loss = _exchange(
        "exchange_gradients",
        [("chips", pair_sums, SAME_CORE), ("scatter", _columns_to_slots(d_meta), ALL_PEERS),
         ("gather", d_norm_g, ALL_PEERS), ("gather", d_bf, ALL_PEERS), ("gather", loss_part, ALL_PEERS)])


    def pad_f(a):
        return jnp.pad(a, ((0, 0), (0, LANES - N_HEADS)))

    res = {}
    res["w_in"] = [a.T for a in _adamw("adamw_w_in", p_in, w_in[0].T, m_w_in[0].T, v_w_in[0].T, p_in.shape[1], 256)]
    res["w_up_pool"] = _adamw("adamw_w_up_pool", p_upp, w_up_pool[0], m_w_up_pool[0], v_w_up_pool[0], 512)
    res["w_up_attn"] = _adamw("adamw_w_up_attn", p_upa, w_up_attn[0], m_w_up_attn[0], v_w_up_attn[0], 512)
    res["w_out"] = _adamw("adamw_w_out", p_out, w_out[0], m_w_out[0], v_w_out[0], 128)
    flat = lambda a: a.reshape(4 * POOL_GROUP, POOL_GROUP)
    row = lambda a: a.reshape(1, D_MODEL)
    small, loss_row = _adamw_small(
        "adamw_small",
        [(p_meta, meta_tokens, m_meta_tokens, v_meta_tokens),
         (p_norm_g, norm_g, m_norm_g, v_norm_g),
         (p_bf, pad_f(b_forget), pad_f(m_b_forget), pad_f(v_b_forget)),
         (p_pool_w, flat(pool_w), flat(m_pool_w), flat(v_pool_w)),
         (p_scale, pool_scale, m_pool_scale, v_pool_scale),
         (p_final_g, final_g, row(m_final_norm_g), row(v_final_norm_g))],
        p_loss)
    res["meta_tokens"], res["norm_g"], bf, pw, res["pool_scale"], fg = small
    res["b_forget"] = [a[:, :N_HEADS] for a in bf]
    res["pool_w"] = [a.reshape(pool_w.shape) for a in pw]
    res["final_norm_g"] = [a.reshape(D_MODEL) for a in fg]
    loss = loss_row[0, 0]
    for name in ("w_in", "w_up_pool", "w_up_attn", "w_out"):
        res[name] = [a[None] for a in res[name]]

    order = ["meta_tokens", "norm_g", "w_in", "b_forget", "pool_w", "pool_scale", "w_up_pool", "w_up_attn", "w_out",
             "final_norm_g"]
    outs = [loss, grad_x[None]]
    for part in range(4):
        outs += [res[name][part] for name in order]
    return tuple(outs)
```

```python
import functools

import jax
import jax.numpy as jnp
from jax import lax
from jax.experimental import pallas as pl
from jax.experimental.pallas import tpu as pltpu

F32 = jnp.float32
BF16 = jnp.bfloat16

D_MODEL = 1024
N_META = 16
POOL_WIDTH = 512
ATTN_WIDTH = 512
N_HEADS = 8
HEAD_DIM = 64
POOL_WINDOWS = (2, 4, 8, 16)
POOL_GROUP = 128
MAX_WINDOW = 16
RMS_EPS = 1e-6
N_MAIN = 5120
N_BEFORE_F = 3072
N_DEV = 8
LANES = 128

ROW_TILE = 256
ATT_TILE = 256
ATT_Q_BLOCKS = 4
PAD = ROW_TILE - N_META
VMEM_LIMIT = 56 * 1024 * 1024

ADAM_LR = 0.001
ADAM_B1 = 0.9
ADAM_B2 = 0.999
ADAM_EPS = 1e-08
ADAM_WD = 0.01
ADAM_STEP = 10

NEG = -1e30
MESH = pl.DeviceIdType.MESH


def _params(sem=None):
    kw = dict(vmem_limit_bytes=VMEM_LIMIT)
    if sem is not None:
        kw["dimension_semantics"] = sem
    return pltpu.CompilerParams(**kw)


def _const(shape, block_index=None):
    idx = block_index or (0,) * len(shape)
    return pl.BlockSpec(shape, lambda i: idx, pipeline_mode=pl.Buffered(1))


def _sigmoid(x):
    return jax.nn.sigmoid(x)


def _dot(a, b):
    return jnp.dot(a, b, preferred_element_type=F32)


def _dot_nt(a, b):
    return lax.dot_general(a, b, (((1,), (1,)), ((), ())), preferred_element_type=F32)


def _dot_tn(a, b):
    return lax.dot_general(a, b, (((0,), (0,)), ((), ())), preferred_element_type=F32)


def _pool_counts(first_row, rows):
    row = first_row + lax.broadcasted_iota(jnp.int32, (rows, 1), 0)
    pos1 = row - PAD + 1
    return [jnp.clip(pos1, 1, w).astype(F32) for w in POOL_WINDOWS]


def _pool_means(u_ext, u, counts):
    rows = u.shape[0]
    out = []
    for g, w in enumerate(POOL_WINDOWS):
        s = u_ext[:, POOL_GROUP * g:POOL_GROUP * (g + 1)]
        sh = 1
        while sh < w:
            s = s + pltpu.roll(s, sh, axis=0)
            sh *= 2
        out.append(s[MAX_WINDOW:MAX_WINDOW + rows, :] / counts[g] - u[:, POOL_GROUP * g:POOL_GROUP * (g + 1)])
    return out


Q_BIAS, Q_ONES, Q_LSE = 64, 67, 70
K_ONES, K_BIAS, K_ONES2 = 64, 67, 70
V_ONES = 64
DO_BIAS = 64


def _lane_ones(lane, ranges):
    hit = None
    for lo, hi in ranges:
        r = (lane >= lo) & (lane < hi)
        hit = r if hit is None else hit | r
    return jnp.where(hit, 1.0, 0.0)


def _put3(base, lane, first, x):
    hi = x.astype(BF16).astype(F32)
    rest = x - hi
    mid = rest.astype(BF16).astype(F32)
    lo = (rest - mid).astype(BF16).astype(F32)
    for j, piece in enumerate((hi, mid, lo)):
        base = jnp.where(lane == first + j, piece, base)
    return base


SIBLING = 1
SAME_CORE = (2, 4, 6)
ALL_PEERS = (1, 2, 3, 4, 5, 6, 7)


def _place():
    return lax.axis_index("x"), lax.axis_index("y"), lax.axis_index("c")


def _peer(r):
    x, y, c = _place()
    return (1 - x if r & 4 else x, 1 - y if r & 2 else y, 1 - c if r & 1 else c)


def _device_slot(p):
    return 4 * p[0] + 2 * p[1] + p[2]


def _chip_slot(p):
    return 2 * p[0] + p[1]


def _exchange(name, items):
    n = len(items)

    def body(*refs):
        copies = _exchange_copies(items, refs[:n], refs[n:2 * n], *refs[2 * n:])
        for cp in copies:
            cp.start()
        for cp in copies:
            cp.wait()

    hbm = pl.BlockSpec(memory_space=pl.ANY)
    return pl.pallas_call(
        body, name=name, out_shape=_exchange_results(items),
        in_specs=[hbm] * n, out_specs=[hbm] * n,
        scratch_shapes=_exchange_semaphores(n),
    )(*[a for _, a, _ in items])


def _exchange_results(items):
    return [jax.ShapeDtypeStruct(((N_DEV,) if kind == "gather" else ()) + a.shape, a.dtype) for kind, a, _ in items]


def _exchange_semaphores(n):
    return [pltpu.SemaphoreType.DMA((n, N_DEV - 1)), pltpu.SemaphoreType.DMA((n, N_DEV - 1)),
            pltpu.SemaphoreType.DMA((n,))]


def _exchange_copies(items, ins, outs, send_sems, recv_sems, local_sems):
    me = _place()
    copies = []
    for a, (kind, _, peers) in enumerate(items):
        slot = _chip_slot if kind == "chips" else _device_slot
        for r in peers:
            peer = _peer(r)
            src = ins[a] if kind in ("swap", "gather") else ins[a].at[slot(peer)]
            dst = outs[a] if kind == "swap" else outs[a].at[slot(me)]
            copies.append(pltpu.make_async_remote_copy(
                src_ref=src, dst_ref=dst, send_sem=send_sems.at[a, r - 1], recv_sem=recv_sems.at[a, r - 1],
                device_id=peer, device_id_type=MESH))
        if kind != "swap":
            src = ins[a] if kind == "gather" else ins[a].at[slot(me)]
            copies.append(pltpu.make_async_copy(src, outs[a].at[slot(me)], local_sems.at[a]))
    return copies


def _gather_two_level(name, arrays):
    n = len(arrays)

    def body(*refs):
        ins, outs = refs[:n], refs[n:2 * n]
        send_sems, recv_sems, local_sems = refs[2 * n:]
        me = _place()
        sibling = _peer(SIBLING)

        def copy(a, k, block, to, src=None):
            rows = outs[a].at[_device_slot(block)]
            return pltpu.make_async_remote_copy(
                src_ref=rows if src is None else src, dst_ref=rows,
                send_sem=send_sems.at[a, k], recv_sem=recv_sems.at[a, k], device_id=to, device_id_type=MESH)

        sends, own = [], []
        for a in range(n):
            mine = pltpu.make_async_copy(ins[a], outs[a].at[_device_slot(me)], local_sems.at[a])
            mine.start()
            own.append(mine)
            for k, r in enumerate((SIBLING,) + SAME_CORE):
                cp = copy(a, k, me, _peer(r), src=ins[a])
                cp.start()
                sends.append(cp)
        for a in range(n):
            for j, r in enumerate(SAME_CORE):
                copy(a, 1 + j, _peer(r), me).wait_recv()
                passed = copy(a, 4 + j, _peer(r), sibling)
                passed.start()
                sends.append(passed)
        for a in range(n):
            copy(a, 0, sibling, me).wait_recv()
            for j, r in enumerate(SAME_CORE):
                copy(a, 4 + j, _peer(r | SIBLING), me).wait_recv()
        for cp in sends:
            cp.wait_send()
        for cp in own:
            cp.wait()

    hbm = pl.BlockSpec(memory_space=pl.ANY)
    return pl.pallas_call(
        body, name=name, out_shape=[jax.ShapeDtypeStruct((N_DEV,) + a.shape, a.dtype) for a in arrays],
        in_specs=[hbm] * n, out_specs=[hbm] * n,
        scratch_shapes=[pltpu.SemaphoreType.DMA((n, N_DEV - 1)), pltpu.SemaphoreType.DMA((n, N_DEV - 1)),
                        pltpu.SemaphoreType.DMA((n,))],
    )(*arrays)


def _forward_in(x, tile0, norm_g, w_main, w_f, b_f, pool_w, pool_scale, w_up_pool):
    seq = x.shape[0]
    nt = seq // ROW_TILE + 1
    lp = nt * ROW_TILE
    tm = ROW_TILE

    def body(x_ref, t0_ref, g_ref, wa_ref, wf_ref, bf_ref, pw_ref, sc_ref, wup_ref,
             h_ref, u_ref, zp_ref, k_ref, v_ref, qt_ref, kt_ref, vt_ref, sn_ref, ap_ref,
             uext_ref, carry_ref):
        i = pl.program_id(0)

        @pl.when(i == 0)
        def _():
            uext_ref[...] = jnp.zeros_like(uext_ref)
            carry_ref[...] = jnp.zeros_like(carry_ref)

        xt = jnp.where(i == 0, t0_ref[...], x_ref[...])
        r = lax.rsqrt(jnp.mean(xt * xt, axis=-1, keepdims=True) + RMS_EPS)
        h = (xt * r * g_ref[...]).astype(BF16)
        h_ref[...] = h
        pa = _dot_nt(h, wa_ref[...])
        u = pa[:, :512]
        zp = pa[:, 512:1024]
        u_ref[...] = u
        zp_ref[...] = zp

        uext_ref[0:MAX_WINDOW, :] = uext_ref[tm:tm + MAX_WINDOW, :]
        uext_ref[MAX_WINDOW:, :] = u
        counts = _pool_counts(i * tm, tm)
        ps = _pool_means(uext_ref[...], u, counts)
        ppw = jnp.concatenate([_dot(ps[g].astype(BF16), pw_ref[g]) for g in range(4)], axis=1)
        y_pool = ppw * sc_ref[...] * (zp * _sigmoid(zp))
        ap_ref[...] = _dot(y_pool.astype(BF16), wup_ref[...]).astype(BF16)

        fl = _dot_nt(h, wf_ref[...]) + bf_ref[...]
        row = i * tm + lax.broadcasted_iota(jnp.int32, (tm, LANES), 0)
        rloc = lax.broadcasted_iota(jnp.int32, (tm, LANES), 0)
        lane = lax.broadcasted_iota(jnp.int32, (tm, LANES), 1)
        live = (row >= PAD) & (lane < N_HEADS)
        logf = jnp.minimum(fl, 0.0) - jnp.log1p(jnp.exp(-jnp.abs(fl)))
        cs = jnp.where(live, logf, 0.0)
        sh = 1
        while sh < tm:
            cs = cs + jnp.where(rloc >= sh, pltpu.roll(cs, sh, axis=0), 0.0)
            sh *= 2
        cs = cs + carry_ref[...]
        carry_ref[...] = cs[tm - 1:tm, :]
        sn_ref[...] = jnp.where(live, _sigmoid(-fl), 0.0)

        rows1 = i * tm + lax.broadcasted_iota(jnp.int32, (tm, 1), 0)
        ones_q = _lane_ones(lane, ((Q_ONES, Q_ONES + 3),))
        ones_k = _lane_ones(lane, ((K_ONES, K_ONES + 3), (K_ONES2, K_ONES2 + 3)))
        ones_v = _lane_ones(lane, ((V_ONES, V_ONES + 3),))
        for hp in range(N_HEADS // 2):
            qp = pa[:, 1024 + LANES * hp:1024 + LANES * (hp + 1)] * 0.125
            kp = pa[:, 1536 + LANES * hp:1536 + LANES * (hp + 1)]
            vp = pa[:, 2048 + LANES * hp:2048 + LANES * (hp + 1)]
            for e in range(2):
                head = 2 * hp + e
                if e:
                    qp, kp, vp = (pltpu.roll(a, HEAD_DIM, axis=1) for a in (qp, kp, vp))
                c_h = cs[:, head:head + 1]
                q_h = jnp.where(lane < HEAD_DIM, qp, _put3(ones_q, lane, Q_BIAS, c_h))
                qt_ref[head] = q_h.T.astype(BF16)
                minus_ck = jnp.where(rows1 >= PAD, -c_h, NEG)
                k_h = jnp.where(lane < HEAD_DIM, kp, _put3(ones_k, lane, K_BIAS, minus_ck))
                k_ref[head] = k_h.astype(BF16)
                kt_ref[head] = k_h.T.astype(BF16)
                v_h = jnp.where(lane < HEAD_DIM, vp, ones_v)
                v_ref[head] = v_h.astype(BF16)
                vt_ref[head] = v_h.T.astype(BF16)

    row_f32 = lambda w: pl.BlockSpec((tm, w), lambda i: (i, 0))
    out_shape = [
        jax.ShapeDtypeStruct((lp, D_MODEL), BF16),
        jax.ShapeDtypeStruct((lp, POOL_WIDTH), F32),
        jax.ShapeDtypeStruct((lp, POOL_WIDTH), F32),
        jax.ShapeDtypeStruct((N_HEADS, lp, LANES), BF16),
        jax.ShapeDtypeStruct((N_HEADS, lp, LANES), BF16),
        jax.ShapeDtypeStruct((N_HEADS, LANES, lp), BF16),
        jax.ShapeDtypeStruct((N_HEADS, LANES, lp), BF16),
        jax.ShapeDtypeStruct((N_HEADS, LANES, lp), BF16),
        jax.ShapeDtypeStruct((lp, LANES), F32),
        jax.ShapeDtypeStruct((lp, D_MODEL), BF16),
    ]
    heads = pl.BlockSpec((N_HEADS, tm, LANES), lambda i: (0, i, 0))
    heads_t = pl.BlockSpec((N_HEADS, LANES, tm), lambda i: (0, 0, i))
    out_specs = [row_f32(D_MODEL), row_f32(512), row_f32(512), heads, heads, heads_t, heads_t, heads_t,
                 row_f32(LANES), row_f32(D_MODEL)]
    in_specs = [
        pl.BlockSpec((tm, D_MODEL), lambda i: (jnp.maximum(i - 1, 0), 0)),
        _const((tm, D_MODEL)), _const((1, D_MODEL)),
        _const((2560, D_MODEL)), _const((LANES, D_MODEL)), _const((1, LANES)),
        _const((4, POOL_GROUP, POOL_GROUP)), _const((1, POOL_WIDTH)), _const((POOL_WIDTH, D_MODEL)),
    ]
    return pl.pallas_call(
        body, name="forward_in", grid=(nt,), out_shape=out_shape, in_specs=in_specs, out_specs=out_specs,
        scratch_shapes=[pltpu.VMEM((tm + MAX_WINDOW, POOL_WIDTH), F32), pltpu.VMEM((1, LANES), F32)],
        compiler_params=_params(("arbitrary",)),
    )(x, tile0, norm_g, w_main, w_f, b_f, pool_w, pool_scale, w_up_pool)


def _causal(tb):
    return lax.broadcasted_iota(jnp.int32, (tb, tb), 1) <= lax.broadcasted_iota(jnp.int32, (tb, tb), 0)


def _pair_lanes(a0, a1):
    lane = lax.broadcasted_iota(jnp.int32, a0.shape, 1)
    return jnp.where(lane < HEAD_DIM, a0, pltpu.roll(a1, HEAD_DIM, axis=1))


def _behind(items, ins, outs, sems):
    step, last = pl.program_id(0), pl.num_programs(0) - 1

    @pl.when(step == 0)
    def _():
        for cp in _exchange_copies(items, ins, outs, *sems):
            cp.start()

    def finish():
        @pl.when(step == last)
        def _():
            for cp in _exchange_copies(items, ins, outs, *sems):
                cp.wait()

    return finish


def _attention_forward(qt, k, vt, behind):
    lp = k.shape[1]
    tk = ATT_TILE
    tq_big = ATT_Q_BLOCKS * tk
    n_big = (lp // tk - 1) // ATT_Q_BLOCKS
    assert lp == tk + n_big * tq_big and ATT_Q_BLOCKS % 2 == 0
    nx = len(behind)

    def body(qt_ref, k_ref, vt_ref, *rest):
        o_ref, lse_ref = rest[nx:nx + 2]
        s_buf, m_scr, acc_scr = rest[2 * nx + 2:2 * nx + 5]
        finish_exchange = _behind(behind, rest[:nx], rest[nx + 2:2 * nx + 2], rest[2 * nx + 5:])

        def q_tile(q0, tq, pairs):
            first = q0 // tk
            qts = [qt_ref[e, :, pl.ds(q0, tq)] for e in range(2)]

            def block(kj):
                return pl.ds(kj * tk if isinstance(kj, int) else pl.multiple_of(kj * tk, tk), tk)

            def step(kj, rd, wr, c0=0, diagonal=False):
                c1 = c0 + tk if diagonal else c0
                for e in range(2):
                    s = s_buf[rd, e, :, c0:tq]
                    if wr is not None:
                        s_buf[wr, e, :, c1:tq] = _dot(k_ref[e, block(kj + 1), :], qts[e][:, c1:tq])
                    if diagonal:
                        keys = lax.broadcasted_iota(jnp.int32, s.shape, 0)
                        s = jnp.where(keys <= lax.broadcasted_iota(jnp.int32, s.shape, 1), s, NEG)
                    m = m_scr[e, :, c0:tq]
                    m_new = jnp.maximum(m, jnp.max(s, axis=0, keepdims=True))
                    p = jnp.exp(s - m_new)
                    pv = _dot(vt_ref[e, :, block(kj)], p.astype(BF16))
                    acc_scr[e, :, c0:tq] = jnp.exp(m - m_new) * acc_scr[e, :, c0:tq] + pv
                    m_scr[e, :, c0:tq] = m_new

            for e in range(2):
                m_scr[e, :, 0:tq] = jnp.full((1, tq), NEG, F32)
                acc_scr[e, :, 0:tq] = jnp.zeros((LANES, tq), F32)
                s_buf[0, e, :, 0:tq] = _dot(k_ref[e, block(0), :], qts[e])
            if pairs is None:
                step(0, 0, None, 0, True)
            else:
                step(0, 0, 1)

                def two_steps(t, _):
                    step(1 + 2 * t, 1, 0)
                    step(2 + 2 * t, 0, 1)
                    return 0

                lax.fori_loop(0, pairs, two_steps, 0)
                for b in range(tq // tk):
                    step(first + b, (b + 1) % 2, b % 2 if (b + 1) * tk < tq else None, b * tk, True)
            outs, lses = [], []
            for e in range(2):
                acc = acc_scr[e, :, 0:tq]
                l = acc[V_ONES:V_ONES + 1, :]
                outs.append((acc / l).T)
                lses.append(m_scr[e, :, 0:tq] + jnp.log(l))
            o_ref[pl.ds(q0, tq), :] = _pair_lanes(outs[0], outs[1]).astype(BF16)
            lse_rows = jnp.concatenate(lses + [jnp.zeros((LANES - 2, tq), F32)], axis=0)
            lse_ref[pl.ds(q0, tq), :] = lse_rows.T

        q_tile(0, tk, None)

        def big_tile(i, _):
            q_tile(pl.multiple_of(tk + i * tq_big, tk), tq_big, (ATT_Q_BLOCKS // 2) * i)
            return 0

        lax.fori_loop(0, n_big, big_tile, 0)
        finish_exchange()

    pair = pl.BlockSpec((lp, LANES), lambda hp: (0, hp))
    heads = pl.BlockSpec((2, lp, LANES), lambda hp: (hp, 0, 0), pipeline_mode=pl.Buffered(1))
    heads_t = pl.BlockSpec((2, LANES, lp), lambda hp: (hp, 0, 0), pipeline_mode=pl.Buffered(1))
    hbm = pl.BlockSpec(memory_space=pl.ANY)
    return pl.pallas_call(
        body, name="attention_forward", grid=(N_HEADS // 2,),
        out_shape=[jax.ShapeDtypeStruct((lp, ATTN_WIDTH), BF16), jax.ShapeDtypeStruct((lp, ATTN_WIDTH), F32)]
        + _exchange_results(behind),
        in_specs=[heads_t, heads, heads_t] + [hbm] * nx,
        out_specs=[pair, pair] + [hbm] * nx,
        scratch_shapes=[pltpu.VMEM((2, 2, tk, tq_big), F32), pltpu.VMEM((2, 1, tq_big), F32),
                        pltpu.VMEM((2, LANES, tq_big), F32)] + _exchange_semaphores(nx),
        compiler_params=_params(("arbitrary",)),
    )(qt, k, vt, *[a for _, a, _ in behind])


def _rows3(first, x):
    sub = lax.broadcasted_iota(jnp.int32, (LANES, x.shape[1]), 0)
    hi = x.astype(BF16).astype(F32)
    rest = x - hi
    mid = rest.astype(BF16).astype(F32)
    lo = (rest - mid).astype(BF16).astype(F32)
    out = jnp.zeros((LANES, x.shape[1]), F32)
    for j, piece in enumerate((hi, mid, lo)):
        out = jnp.where(sub == first + j, piece, out)
    return out


def _attention_backward(qt, k, kt, v, do, o, lse, behind):
    lp = k.shape[1]
    tb = ATT_TILE
    nb = lp // tb
    tq_big = ATT_Q_BLOCKS * tb
    n_big = (nb - 1) // ATT_Q_BLOCKS
    assert lp == tb + n_big * tq_big and ATT_Q_BLOCKS % 2 == 0
    nx = len(behind)

    def body(qt_ref, k_ref, kt_ref, v_ref, do_ref, o_ref, lse_ref, *rest):
        dq_ref, dk_ref, dv_ref, dc_ref = rest[nx:nx + 4]
        q2_ref, do2_ref, dk_acc, dv_acc, dq_scr, s_buf = rest[2 * nx + 4:2 * nx + 10]
        finish_exchange = _behind(behind, rest[:nx], rest[nx + 4:2 * nx + 4], rest[2 * nx + 10:])
        sub = lax.broadcasted_iota(jnp.int32, (LANES, tb), 0)

        def lanes01(row0, row1):
            n = row0.shape[1]
            return jnp.concatenate([row0, row1, jnp.zeros((LANES - 2, n), F32)], axis=0).T

        def prepare(bi, _):
            r0 = pl.multiple_of(bi * tb, tb)
            queries = r0 + lax.broadcasted_iota(jnp.int32, (1, tb), 1)
            dob = do_ref[pl.ds(r0, tb), :].astype(F32)
            do_t = dob.T
            dd_t = (dob * o_ref[pl.ds(r0, tb), :].astype(F32)).T
            lse_t = lse_ref[pl.ds(r0, tb), :].T
            for e in range(2):
                delta = jnp.sum(dd_t[HEAD_DIM * e:HEAD_DIM * (e + 1), :], axis=0, keepdims=True)
                do_e = jnp.concatenate([do_t[HEAD_DIM * e:HEAD_DIM * (e + 1), :], jnp.zeros((HEAD_DIM, tb), F32)], axis=0)
                do2_ref[e, :, pl.ds(r0, tb)] = jnp.where(sub < HEAD_DIM, do_e, _rows3(DO_BIAS, -delta)).astype(BF16)
                minus_lse = jnp.where(queries >= PAD, -lse_t[e:e + 1, :], NEG)
                keep = (sub < Q_LSE) | (sub >= Q_LSE + 3)
                q2_ref[e, :, pl.ds(r0, tb)] = jnp.where(keep, qt_ref[e, :, pl.ds(r0, tb)].astype(F32),
                                                        _rows3(Q_LSE, minus_lse)).astype(BF16)
            return 0

        lax.fori_loop(0, nb, prepare, 0)
        dk_acc[...] = jnp.zeros_like(dk_acc)
        dv_acc[...] = jnp.zeros_like(dv_acc)

        def q_tile(q0, tq, pairs):
            first = q0 // tb
            qts = [q2_ref[e, :, pl.ds(q0, tq)] for e in range(2)]
            dots = [do2_ref[e, :, pl.ds(q0, tq)] for e in range(2)]

            def block(kj):
                return pl.ds(kj * tb if isinstance(kj, int) else pl.multiple_of(kj * tb, tb), tb)

            def step(kj, rd, wr, c0=0, diagonal=False):
                c1 = c0 + tb if diagonal else c0
                for e in range(2):
                    s = s_buf[rd, e, :, c0:tq]
                    if wr is not None:
                        s_buf[wr, e, :, c1:tq] = _dot(k_ref[e, block(kj + 1), :], qts[e][:, c1:tq])
                    dpd = _dot(v_ref[e, block(kj), :], dots[e][:, c0:tq])
                    p = jnp.exp(s)
                    if diagonal:
                        keys = lax.broadcasted_iota(jnp.int32, s.shape, 0)
                        p = jnp.where(keys <= lax.broadcasted_iota(jnp.int32, s.shape, 1), p, 0.0)
                    dsb = (p * dpd).astype(BF16)
                    dv_acc[e, :, block(kj)] += _dot_nt(dots[e][:, c0:tq], p.astype(BF16))
                    dk_acc[e, :, block(kj)] += _dot_nt(qts[e][:, c0:tq], dsb)
                    dq_scr[e, :, c0:tq] += _dot(kt_ref[e, :, block(kj)], dsb)

            for e in range(2):
                dq_scr[e, :, 0:tq] = jnp.zeros((LANES, tq), F32)
                s_buf[0, e, :, 0:tq] = _dot(k_ref[e, block(0), :], qts[e])
            if pairs is None:
                step(0, 0, None, 0, True)
            else:
                step(0, 0, 1)

                def two_steps(t, _):
                    step(1 + 2 * t, 1, 0)
                    step(2 + 2 * t, 0, 1)
                    return 0

                lax.fori_loop(0, pairs, two_steps, 0)
                for b in range(tq // tb):
                    step(first + b, (b + 1) % 2, b % 2 if (b + 1) * tb < tq else None, b * tb, True)
            dq0, dq1 = dq_scr[0, :, 0:tq], dq_scr[1, :, 0:tq]
            dq_ref[pl.ds(q0, tq), :] = (_pair_lanes(dq0.T, dq1.T) * 0.125).astype(BF16)
            dc_ref[pl.ds(q0, tq), :] = lanes01(dq0[K_ONES:K_ONES + 1, :], dq1[K_ONES:K_ONES + 1, :])

        q_tile(0, tb, None)

        def big_tile(i, _):
            q_tile(pl.multiple_of(tb + i * tq_big, tb), tq_big, (ATT_Q_BLOCKS // 2) * i)
            return 0

        lax.fori_loop(0, n_big, big_tile, 0)

        def finish(bi, _):
            r0 = pl.multiple_of(bi * tb, tb)
            dk0, dk1 = dk_acc[0, :, pl.ds(r0, tb)], dk_acc[1, :, pl.ds(r0, tb)]
            dk_ref[pl.ds(r0, tb), :] = _pair_lanes(dk0.T, dk1.T).astype(BF16)
            dv_ref[pl.ds(r0, tb), :] = _pair_lanes(dv_acc[0, :, pl.ds(r0, tb)].T, dv_acc[1, :, pl.ds(r0, tb)].T).astype(BF16)
            dc_ref[pl.ds(r0, tb), :] = dc_ref[pl.ds(r0, tb), :] - lanes01(dk0[Q_ONES:Q_ONES + 1, :], dk1[Q_ONES:Q_ONES + 1, :])
            return 0

        lax.fori_loop(0, nb, finish, 0)
        finish_exchange()

    once = pl.Buffered(1)
    pair = pl.BlockSpec((lp, LANES), lambda hp: (0, hp))
    pair_in = pl.BlockSpec((lp, LANES), lambda hp: (0, hp), pipeline_mode=once)
    heads = pl.BlockSpec((2, lp, LANES), lambda hp: (hp, 0, 0), pipeline_mode=once)
    heads_t = pl.BlockSpec((2, LANES, lp), lambda hp: (hp, 0, 0), pipeline_mode=once)
    hbm = pl.BlockSpec(memory_space=pl.ANY)
    wide = jax.ShapeDtypeStruct((lp, ATTN_WIDTH), BF16)
    return pl.pallas_call(
        body, name="attention_backward", grid=(N_HEADS // 2,),
        out_shape=[wide, wide, wide, jax.ShapeDtypeStruct((lp, ATTN_WIDTH), F32)] + _exchange_results(behind),
        in_specs=[heads_t, heads, heads_t, heads, pair_in, pair_in, pair_in] + [hbm] * nx,
        out_specs=[pair, pair, pair, pair] + [hbm] * nx,
        scratch_shapes=[pltpu.VMEM((2, LANES, lp), BF16), pltpu.VMEM((2, LANES, lp), BF16),
                        pltpu.VMEM((2, LANES, lp), F32), pltpu.VMEM((2, LANES, lp), F32),
                        pltpu.VMEM((2, LANES, tq_big), F32), pltpu.VMEM((2, 2, tb, tq_big), F32)]
        + _exchange_semaphores(nx),
        compiler_params=_params(("arbitrary",)),
    )(qt, k, kt, v, do, o, lse, *[a for _, a, _ in behind])


def _middle(x, target, h, o, a_pool, u, zp, w_main, w_up_pool, w_up_attn, w_out, pool_w, pool_scale, final_g):
    seq = x.shape[0]
    tm = ROW_TILE
    nt = seq // tm + 1
    lp = nt * tm
    halo_blocks = tm // MAX_WINDOW

    def body(x_ref, t_ref, h_ref, o_ref, ap_ref, u_ref, uh_ref, zp_ref,
             wc_ref, wupp_ref, wupa_ref, wout_ref, pw_ref, sc_ref, gf_ref,
             dh2_ref, dh2b_ref, mg_ref, yp_ref, ya_ref, dap_ref, daa_ref, do_ref, dza_ref, dgp_ref, dga_ref, dzp_ref,
             dpn_ref,
             loss_ref, dgf_ref, dsc_ref, dpw_ref):
        i = pl.program_id(0)
        tiles = (dh2_ref, dh2b_ref, mg_ref, yp_ref, ya_ref, dap_ref, daa_ref, do_ref, dza_ref, dgp_ref, dga_ref, dzp_ref,
                 dpn_ref)

        @pl.when(i == 0)
        def _():
            for ref in tiles + (loss_ref, dgf_ref, dsc_ref, dpw_ref):
                ref[...] = jnp.zeros_like(ref)

        @pl.when(i > 0)
        def _():
            xt = x_ref[...]
            hb = h_ref[...]
            pc = _dot_nt(hb, wc_ref[...])
            za, gp, ga = pc[:, :512], pc[:, 512:1536], pc[:, 1536:]
            of = o_ref[...].astype(F32)
            sza = _sigmoid(za)
            silu_za = za * sza
            ya = (of * silu_za).astype(BF16)
            ya_ref[...] = ya
            aa = _dot(ya, wupa_ref[...])
            ap = ap_ref[...].astype(F32)
            sgp, sga = _sigmoid(gp), _sigmoid(ga)
            mg = (sgp * ap + sga * aa).astype(BF16)
            mg_ref[...] = mg
            h2 = xt + _dot(mg, wout_ref[...])
            r2 = lax.rsqrt(jnp.mean(h2 * h2, axis=-1, keepdims=True) + RMS_EPS)
            h2n = h2 * r2
            gf = gf_ref[...]
            diff = h2n * gf - t_ref[...]
            loss_ref[...] += 0.5 * jnp.sum(jnp.mean(diff * diff, axis=-1, keepdims=True), axis=0, keepdims=True)
            dy = diff * (1.0 / D_MODEL)
            dgf_ref[...] += jnp.sum(dy * h2n, axis=0, keepdims=True)
            dyg = dy * gf
            dh2 = r2 * (dyg - h2n * jnp.mean(dyg * h2n, axis=-1, keepdims=True))
            dh2_ref[...] = dh2
            dh2b = dh2.astype(BF16)
            dh2b_ref[...] = dh2b
            dmg = _dot_nt(dh2b, wout_ref[...])
            dap = (dmg * sgp).astype(BF16)
            daa = (dmg * sga).astype(BF16)
            dap_ref[...] = dap
            daa_ref[...] = daa
            dgp_ref[...] = (dmg * ap * sgp * (1.0 - sgp)).astype(BF16)
            dga_ref[...] = (dmg * aa * sga * (1.0 - sga)).astype(BF16)
            dyp = _dot_nt(dap, wupp_ref[...])
            dya = _dot_nt(daa, wupa_ref[...])
            do_ref[...] = (dya * silu_za).astype(BF16)
            dza_ref[...] = (dya * of * (sza * (1.0 + za * (1.0 - sza)))).astype(BF16)

            u = u_ref[...]
            zp = zp_ref[...]
            counts = _pool_counts(i * tm, tm)
            ps = _pool_means(jnp.concatenate([uh_ref[...], u], axis=0), u, counts)
            pbs = [p.astype(BF16) for p in ps]
            ppw = jnp.concatenate([_dot(pbs[g], pw_ref[g]) for g in range(4)], axis=1)
            sc = sc_ref[...]
            szp = _sigmoid(zp)
            silu_zp = zp * szp
            ypre = ppw * sc
            yp_ref[...] = (ypre * silu_zp).astype(BF16)
            dypre = dyp * silu_zp
            dzp_ref[...] = (dyp * ypre * (szp * (1.0 + zp * (1.0 - szp)))).astype(BF16)
            dsc_ref[...] += jnp.sum(dypre * ppw, axis=0, keepdims=True)
            dppw = (dypre * sc).astype(BF16)
            dpns = []
            for g in range(4):
                dg = dppw[:, POOL_GROUP * g:POOL_GROUP * (g + 1)]
                dpw_ref[g] += _dot_tn(pbs[g], dg)
                dpns.append(_dot_nt(dg, pw_ref[g]) / counts[g])
            dpn_ref[...] = jnp.concatenate(dpns, axis=1)

    real = lambda w: pl.BlockSpec((tm, w), lambda i: (jnp.maximum(i - 1, 0), 0))
    row = lambda w: pl.BlockSpec((tm, w), lambda i: (i, 0))
    in_specs = [
        real(D_MODEL), real(D_MODEL), row(D_MODEL), row(512), row(D_MODEL), row(512),
        pl.BlockSpec((MAX_WINDOW, 512), lambda i: (jnp.maximum(i * halo_blocks - 1, 0), 0)), row(512),
        _const((2560, D_MODEL), (1, 0)), _const((POOL_WIDTH, D_MODEL)), _const((ATTN_WIDTH, D_MODEL)),
        _const((D_MODEL, D_MODEL)), _const((4, POOL_GROUP, POOL_GROUP)), _const((1, POOL_WIDTH)), _const((1, D_MODEL)),
    ]
    sd = jax.ShapeDtypeStruct
    out_shape = [
        sd((lp, D_MODEL), F32),
        sd((lp, D_MODEL), BF16),
        sd((lp, D_MODEL), BF16),
        sd((lp, 512), BF16),
        sd((lp, 512), BF16),
        sd((lp, D_MODEL), BF16),
        sd((lp, D_MODEL), BF16),
        sd((lp, 512), BF16),
        sd((lp, 512), BF16),
        sd((lp, D_MODEL), BF16),
        sd((lp, D_MODEL), BF16),
        sd((lp, 512), BF16),
        sd((lp, 512), F32),
        sd((1, LANES), F32),
        sd((1, D_MODEL), F32),
        sd((1, 512), F32),
        sd((4, POOL_GROUP, POOL_GROUP), F32),
    ]
    keep = lambda shape: pl.BlockSpec(shape, lambda i: (0,) * len(shape))
    out_specs = [row(D_MODEL), row(D_MODEL), row(D_MODEL), row(512), row(512), row(D_MODEL), row(D_MODEL), row(512),
                 row(512),
                 row(D_MODEL), row(D_MODEL), row(512), row(512),
                 keep((1, LANES)), keep((1, D_MODEL)), keep((1, 512)), keep((4, POOL_GROUP, POOL_GROUP))]
    return pl.pallas_call(
        body, name="middle", grid=(nt,), out_shape=out_shape, in_specs=in_specs, out_specs=out_specs,
        compiler_params=_params(("arbitrary",)),
    )(x, target, h, o, a_pool, u, u, zp, w_main, w_up_pool, w_up_attn, w_out, pool_w, pool_scale, final_g)


def _backward_in(x, tile0, norm_g, dh2, dpn, dzp, dq, dk, dv, dza, dgp, dga, dc, sneg, w_main, w_f):
    seq = x.shape[0]
    tm = ROW_TILE
    nt = seq // tm + 1
    lp = nt * tm
    halo_blocks = tm // MAX_WINDOW
    last_halo = lp // MAX_WINDOW - 1

    def body(x_ref, t0_ref, g_ref, dh2_ref, dpn_ref, dpnh_ref, dzp_ref, dq_ref, dk_ref, dv_ref, dza_ref,
             dgp_ref, dga_ref, dc_ref, sn_ref, wm_ref, wf_ref,
             dproj_ref, df_ref, gx_ref, gmeta_ref, dg_ref, dbf_ref, carry_ref):
        i = pl.program_id(0)
        t = nt - 1 - i

        @pl.when(i == 0)
        def _():
            carry_ref[...] = jnp.zeros_like(carry_ref)
            dg_ref[...] = jnp.zeros_like(dg_ref)
            dbf_ref[...] = jnp.zeros_like(dbf_ref)

        dpn_t = dpn_ref[...]
        ahead = jnp.where(i == 0, jnp.zeros_like(dpnh_ref), dpnh_ref[...])
        ext = jnp.concatenate([dpn_t, ahead], axis=0)
        counts = _pool_counts(t * tm, tm)
        for g, w in enumerate(POOL_WINDOWS):
            s = ext[:, POOL_GROUP * g:POOL_GROUP * (g + 1)]
            sh = 1
            while sh < w:
                s = s + pltpu.roll(s, tm + MAX_WINDOW - sh, axis=0)
                sh *= 2
            du = s[:tm, :] - dpn_t[:, POOL_GROUP * g:POOL_GROUP * (g + 1)] * counts[g]
            dproj_ref[:, POOL_GROUP * g:POOL_GROUP * (g + 1)] = du.astype(BF16)
        dproj_ref[:, 512:1024] = dzp_ref[...]
        dproj_ref[:, 1024:1536] = dq_ref[...]
        dproj_ref[:, 1536:2048] = dk_ref[...]
        dproj_ref[:, 2048:2560] = dv_ref[...]
        dproj_ref[:, 2560:3072] = dza_ref[...]
        dproj_ref[:, 3072:4096] = dgp_ref[...]
        dproj_ref[:, 4096:5120] = dga_ref[...]

        dct = dc_ref[:, 0:LANES]
        for hp in range(1, N_HEADS // 2):
            dct = dct + pltpu.roll(dc_ref[:, LANES * hp:LANES * (hp + 1)], 2 * hp, axis=1)
        rloc = lax.broadcasted_iota(jnp.int32, (tm, LANES), 0)
        sh = 1
        while sh < tm:
            dct = dct + jnp.where(rloc + sh < tm, pltpu.roll(dct, tm - sh, axis=0), 0.0)
            sh *= 2
        dct = dct + carry_ref[...]
        carry_ref[...] = dct[0:1, :]
        df = dct * sn_ref[...]
        dbf_ref[...] += jnp.sum(df, axis=0, keepdims=True)
        dfb = df.astype(BF16)
        df_ref[...] = dfb

        dh = _dot(dproj_ref[...], wm_ref[...]) + _dot(dfb, wf_ref[...])
        xt = jnp.where(t == 0, t0_ref[...], x_ref[...])
        r = lax.rsqrt(jnp.mean(xt * xt, axis=-1, keepdims=True) + RMS_EPS)
        xn = xt * r
        dg_ref[...] += jnp.sum(dh * xn, axis=0, keepdims=True)
        dhg = dh * g_ref[...]
        dx = dh2_ref[...] + r * (dhg - xn * jnp.mean(dhg * xn, axis=-1, keepdims=True))

        @pl.when(t > 0)
        def _():
            gx_ref[...] = dx

        @pl.when(t == 0)
        def _():
            gmeta_ref[...] = dx[PAD:, :]

    rev = lambda w: pl.BlockSpec((tm, w), lambda i: (nt - 1 - i, 0))
    real = pl.BlockSpec((tm, D_MODEL), lambda i: (jnp.maximum(nt - 2 - i, 0), 0))
    in_specs = [
        real, _const((tm, D_MODEL)), _const((1, D_MODEL)), rev(D_MODEL), rev(512),
        pl.BlockSpec((MAX_WINDOW, 512), lambda i: (jnp.minimum((nt - i) * halo_blocks, last_halo), 0)),
        rev(512), rev(512), rev(512), rev(512), rev(512), rev(D_MODEL), rev(D_MODEL),
        rev(512), rev(LANES),
        _const((N_MAIN, D_MODEL)), _const((LANES, D_MODEL)),
    ]
    sd = jax.ShapeDtypeStruct
    out_shape = [sd((lp, N_MAIN), BF16), sd((lp, LANES), BF16), sd((seq, D_MODEL), F32), sd((N_META, D_MODEL), F32),
                 sd((1, D_MODEL), F32), sd((1, LANES), F32)]
    keep = lambda shape: pl.BlockSpec(shape, lambda i: (0,) * len(shape))
    out_specs = [rev(N_MAIN), rev(LANES), real, keep((N_META, D_MODEL)), keep((1, D_MODEL)), keep((1, LANES))]
    return pl.pallas_call(
        body, name="backward_in", grid=(nt,), out_shape=out_shape, in_specs=in_specs, out_specs=out_specs,
        scratch_shapes=[pltpu.VMEM((1, LANES), F32)],
        compiler_params=_params(("arbitrary",)),
    )(x, tile0, norm_g, dh2, dpn, dpn, dzp, dq, dk, dv, dza, dgp, dga, dc, sneg, w_main, w_f)


def _matmul_tn_rows(name, a, b, tm):
    lp, m = a.shape
    n = b.shape[1]

    def body(a_ref, b_ref, c_ref):
        c_ref[...] = _dot_tn(a_ref[...].astype(BF16), b_ref[...].astype(BF16))

    return pl.pallas_call(
        body, name=name, grid=(m // tm,), out_shape=jax.ShapeDtypeStruct((m, n), F32),
        in_specs=[pl.BlockSpec((lp, tm), lambda j: (0, j)), _const((lp, n))],
        out_specs=pl.BlockSpec((tm, n), lambda j: (j, 0)),
        compiler_params=_params(("arbitrary",)),
    )(a, b)


def _adamw_step(p_ref, w_ref, m_ref, v_ref, g_ref, d_ref, mo_ref, vo_ref):
    g = p_ref[0].astype(F32)
    for s in range(1, p_ref.shape[0]):
        g = g + p_ref[s].astype(F32)
    m_new = ADAM_B1 * m_ref[...] + (1.0 - ADAM_B1) * g
    v_new = ADAM_B2 * v_ref[...] + (1.0 - ADAM_B2) * (g * g)
    m_hat = m_new / (1.0 - ADAM_B1 ** ADAM_STEP)
    v_hat = v_new / (1.0 - ADAM_B2 ** ADAM_STEP)
    g_ref[...] = g
    d_ref[...] = -ADAM_LR * (m_hat / (jnp.sqrt(v_hat) + ADAM_EPS) + ADAM_WD * w_ref[...])
    mo_ref[...] = m_new
    vo_ref[...] = v_new


def _adamw_small(name, groups, loss_parts):
    n = len(groups)

    def body(*refs):
        ins, outs = refs[:4 * n + 1], refs[4 * n + 1:]
        for j in range(n):
            _adamw_step(*ins[4 * j:4 * j + 4], *outs[4 * j:4 * j + 4])
        total = ins[-1][0]
        for s in range(1, N_DEV):
            total = total + ins[-1][s]
        outs[-1][...] = total

    vmem = pl.BlockSpec(memory_space=pltpu.VMEM)
    out_shape = [jax.ShapeDtypeStruct(w.shape, F32) for _, w, _, _ in groups for _ in range(4)]
    out_shape.append(jax.ShapeDtypeStruct(loss_parts.shape[1:], F32))
    res = pl.pallas_call(
        body, name=name, out_shape=out_shape, in_specs=[vmem] * (4 * n + 1), out_specs=[vmem] * (4 * n + 1),
        compiler_params=_params(),
    )(*[a for g in groups for a in g], loss_parts)
    return [res[4 * j:4 * j + 4] for j in range(n)], res[-1]


def _adamw(name, parts, w, m, v, rows, cols=None):
    r, c_all = w.shape
    c = cols or c_all
    n_parts = parts.shape[0]

    def body(p_ref, w_ref, m_ref, v_ref, g_ref, d_ref, mo_ref, vo_ref):
        _adamw_step(p_ref, w_ref, m_ref, v_ref, g_ref, d_ref, mo_ref, vo_ref)

    blk = pl.BlockSpec((rows, c), lambda i, j: (i, j))
    return pl.pallas_call(
        body, name=name, grid=(r // rows, c_all // c), out_shape=[jax.ShapeDtypeStruct((r, c_all), F32)] * 4,
        in_specs=[pl.BlockSpec((n_parts, rows, c), lambda i, j: (0, i, j)), blk, blk, blk],
        out_specs=[blk] * 4,
        compiler_params=_params(("arbitrary", "arbitrary")),
    )(parts, w, m, v)


def _pair_sum(name, slots, theirs, rows):
    n, r, c = theirs.shape
    core = lax.axis_index("c").astype(jnp.int32).reshape(1)

    def body(core_ref, a_ref, b_ref, o_ref):
        o_ref[...] = (a_ref[...] + b_ref[...].astype(F32)).astype(BF16)

    blk = pl.BlockSpec((1, rows, c), lambda j, i, core_ref: (j, i, 0))
    mine = pl.BlockSpec((1, rows, c), lambda j, i, core_ref: (2 * j + core_ref[0], i, 0))
    return pl.pallas_call(
        body, name=name, out_shape=jax.ShapeDtypeStruct((n, r, c), BF16),
        grid_spec=pltpu.PrefetchScalarGridSpec(num_scalar_prefetch=1, grid=(n, r // rows),
                                               in_specs=[mine, blk], out_specs=blk),
        compiler_params=_params(("arbitrary", "arbitrary")),
    )(core, slots, theirs)


def _columns_to_slots(a):
    r, c8 = a.shape
    return a.reshape(r, N_DEV, c8 // N_DEV).transpose(1, 0, 2)


def _for_sibling(slots):
    by_chip = slots.reshape((4, 2) + slots.shape[1:])
    return lax.dynamic_index_in_dim(by_chip, 1 - lax.axis_index("c"), 1, keepdims=False).astype(BF16)


def _slots_to_columns(a):
    n, r, c = a.shape
    return a.transpose(1, 0, 2).reshape(r, n * c)


def kernel(x, meta_tokens, norm_g, w_in, b_forget, pool_w, pool_scale, w_up_pool, w_up_attn, w_out, final_norm_g, loss_target, m_meta_tokens, m_norm_g, m_w_in, m_b_forget, m_pool_w, m_pool_scale, m_w_up_pool, m_w_up_attn, m_w_out, m_final_norm_g, v_meta_tokens, v_norm_g, v_w_in, v_b_forget, v_pool_w, v_pool_scale, v_w_up_pool, v_w_up_attn, v_w_out, v_final_norm_g):
    xs = x[0]
    target = loss_target[0]

    g_in, g_upp, g_meta = _gather_two_level(
        "gather_weights", [w_in[0].T.astype(BF16), w_up_pool[0].astype(BF16), meta_tokens])
    w_full = g_in.reshape(N_DEV * g_in.shape[1], D_MODEL)
    w_main = jnp.concatenate([w_full[:N_BEFORE_F], w_full[N_BEFORE_F + N_HEADS:]], axis=0)
    w_f = jnp.pad(w_full[N_BEFORE_F:N_BEFORE_F + N_HEADS], ((0, LANES - N_HEADS), (0, 0)))
    wupp = _slots_to_columns(g_upp)
    meta = _slots_to_columns(g_meta)
    tile0 = jnp.concatenate([jnp.zeros((PAD, D_MODEL), F32), meta], axis=0)
    b_f = jnp.pad(b_forget, ((0, 0), (0, LANES - N_HEADS)))
    pw_b = pool_w[0].astype(BF16)
    final_g = final_norm_g.reshape(1, D_MODEL)

    (h, u, zp, k, v, qt, kt, vt, sneg, a_pool) = _forward_in(xs, tile0, norm_g, w_main, w_f, b_f, pw_b,
                                                              pool_scale, wupp)
    o, lse, g_upa, g_out = _attention_forward(
        qt, k, vt, [("gather", w_up_attn[0].astype(BF16), ALL_PEERS), ("gather", w_out[0].astype(BF16), ALL_PEERS)])
    wupa = _slots_to_columns(g_upa)
    wout = g_out.reshape(D_MODEL, D_MODEL)
    (dh2, dh2b, mg, yp, ya, dap, daa, do, dza, dgp, dga, dzp, dpn,
     loss_part, d_final_g, d_scale, d_pool_w) = _middle(xs, target, h, o, a_pool, u, zp, w_main, wupp, wupa, wout,
                                                        pw_b, pool_scale, final_g)
    dw_out = _matmul_tn_rows("grad_w_out", mg, dh2b, 256)
    dw_upp = _matmul_tn_rows("grad_w_up_pool", yp, dap, 256)
    dw_upa = _matmul_tn_rows("grad_w_up_attn", ya, daa, 256)
    dq, dk, dv, dc, p_upp, p_upa, p_out, p_pool_w, p_scale, p_final_g = _attention_backward(
        qt, k, kt, v, do, o, lse,
        [("scatter", _columns_to_slots(dw_upp).astype(BF16), ALL_PEERS),
         ("scatter", _columns_to_slots(dw_upa).astype(BF16), ALL_PEERS),
         ("scatter", dw_out.reshape(N_DEV, D_MODEL // N_DEV, D_MODEL).astype(BF16), ALL_PEERS),
         ("gather", d_pool_w.reshape(4 * POOL_GROUP, POOL_GROUP), ALL_PEERS),
         ("gather", d_scale, ALL_PEERS), ("gather", d_final_g, ALL_PEERS)])
    dproj, df, grad_x, d_meta, d_norm_g, d_bf = _backward_in(xs, tile0, norm_g, dh2, dpn, dzp, dq, dk, dv, dza,
                                                             dgp, dga, dc, sneg, w_main, w_f)
    dw_main = _matmul_tn_rows("grad_w_in", dproj, h, 512)
    dw_f = _matmul_tn_rows("grad_w_forget", df, h, LANES)
    dw_in = jnp.concatenate([dw_main[:N_BEFORE_F], dw_f[:N_HEADS], dw_main[N_BEFORE_F:]], axis=0)
    dw_in = dw_in.reshape(N_DEV, dw_in.shape[0] // N_DEV, D_MODEL)

    from_sibling, = _exchange("swap_with_sibling", [("swap", _for_sibling(dw_in), (SIBLING,))])
    pair_sums = _pair_sum("pair_sum", dw_in, from_sibling, dw_in.shape[1])
    p_in, p_meta, p_norm_g, p_bf, p_loss = _exchange(
        "exchange_gradients",
        [("chips", pair_sums, SAME_CORE), ("scatter", _columns_to_slots(d_meta), ALL_PEERS),
         ("gather", d_norm_g, ALL_PEERS), ("gather", d_bf, ALL_PEERS), ("gather", loss_part, ALL_PEERS)])


    def pad_f(a):
        return jnp.pad(a, ((0, 0), (0, LANES - N_HEADS)))

    res = {}
    res["w_in"] = [a.T for a in _adamw("adamw_w_in", p_in, w_in[0].T, m_w_in[0].T, v_w_in[0].T, p_in.shape[1], 256)]
    res["w_up_pool"] = _adamw("adamw_w_up_pool", p_upp, w_up_pool[0], m_w_up_pool[0], v_w_up_pool[0], 512)
    res["w_up_attn"] = _adamw("adamw_w_up_attn", p_upa, w_up_attn[0], m_w_up_attn[0], v_w_up_attn[0], 512)
    res["w_out"] = _adamw("adamw_w_out", p_out, w_out[0], m_w_out[0], v_w_out[0], 128)
    flat = lambda a: a.reshape(4 * POOL_GROUP, POOL_GROUP)
    row = lambda a: a.reshape(1, D_MODEL)
    small, loss_row = _adamw_small(
        "adamw_small",
        [(p_meta, meta_tokens, m_meta_tokens, v_meta_tokens),
         (p_norm_g, norm_g, m_norm_g, v_norm_g),
         (p_bf, pad_f(b_forget), pad_f(m_b_forget), pad_f(v_b_forget)),
         (p_pool_w, flat(pool_w), flat(m_pool_w), flat(v_pool_w)),
         (p_scale, pool_scale, m_pool_scale, v_pool_scale),
         (p_final_g, final_g, row(m_final_norm_g), row(v_final_norm_g))],
        p_loss)
    res["meta_tokens"], res["norm_g"], bf, pw, res["pool_scale"], fg = small
    res["b_forget"] = [a[:, :N_HEADS] for a in bf]
    res["pool_w"] = [a.reshape(pool_w.shape) for a in pw]
    res["final_norm_g"] = [a.reshape(D_MODEL) for a in fg]
    loss = loss_row[0, 0]
    for name in ("w_in", "w_up_pool", "w_up_attn", "w_out"):
        res[name] = [a[None] for a in res[name]]

    order = ["meta_tokens", "norm_g", "w_in", "b_forget", "pool_w", "pool_scale", "w_up_pool", "w_up_attn", "w_out",
             "final_norm_g"]
    outs = [loss, grad_x[None]]
    for part in range(4):
        outs += [res[name][part] for name in order]
    return tuple(outs)
```

```python
import functools

import jax
import jax.numpy as jnp
from jax import lax
from jax.experimental import pallas as pl
from jax.experimental.pallas import tpu as pltpu

F32 = jnp.float32
BF16 = jnp.bfloat16

D_MODEL = 1024
N_META = 16
POOL_WIDTH = 512
ATTN_WIDTH = 512
N_HEADS = 8
HEAD_DIM = 64
POOL_WINDOWS = (2, 4, 8, 16)
POOL_GROUP = 128
MAX_WINDOW = 16
RMS_EPS = 1e-6
N_MAIN = 5120
N_BEFORE_F = 3072
N_DEV = 8
LANES = 128

ROW_TILE = 256
ATT_TILE = 256
ATT_Q_BLOCKS = 4
PAD = ROW_TILE - N_META
VMEM_LIMIT = 56 * 1024 * 1024

ADAM_LR = 0.001
ADAM_B1 = 0.9
ADAM_B2 = 0.999
ADAM_EPS = 1e-08
ADAM_WD = 0.01
ADAM_STEP = 10

NEG = -1e30
MESH = pl.DeviceIdType.MESH


def _params(sem=None):
    kw = dict(vmem_limit_bytes=VMEM_LIMIT)
    if sem is not None:
        kw["dimension_semantics"] = sem
    return pltpu.CompilerParams(**kw)


def _const(shape, block_index=None):
    idx = block_index or (0,) * len(shape)
    return pl.BlockSpec(shape, lambda i: idx, pipeline_mode=pl.Buffered(1))


def _sigmoid(x):
    return jax.nn.sigmoid(x)


def _dot(a, b):
    return jnp.dot(a, b, preferred_element_type=F32)


def _dot_nt(a, b):
    return lax.dot_general(a, b, (((1,), (1,)), ((), ())), preferred_element_type=F32)


def _dot_tn(a, b):
    return lax.dot_general(a, b, (((0,), (0,)), ((), ())), preferred_element_type=F32)


def _pool_counts(first_row, rows):
    row = first_row + lax.broadcasted_iota(jnp.int32, (rows, 1), 0)
    pos1 = row - PAD + 1
    return [jnp.clip(pos1, 1, w).astype(F32) for w in POOL_WINDOWS]


def _pool_means(u_ext, u, counts):
    rows = u.shape[0]
    out = []
    for g, w in enumerate(POOL_WINDOWS):
        s = u_ext[:, POOL_GROUP * g:POOL_GROUP * (g + 1)]
        sh = 1
        while sh < w:
            s = s + pltpu.roll(s, sh, axis=0)
            sh *= 2
        out.append(s[MAX_WINDOW:MAX_WINDOW + rows, :] / counts[g] - u[:, POOL_GROUP * g:POOL_GROUP * (g + 1)])
    return out


Q_BIAS, Q_ONES, Q_LSE = 64, 67, 70
K_ONES, K_BIAS, K_ONES2 = 64, 67, 70
V_ONES = 64
DO_BIAS = 64


def _lane_ones(lane, ranges):
    hit = None
    for lo, hi in ranges:
        r = (lane >= lo) & (lane < hi)
        hit = r if hit is None else hit | r
    return jnp.where(hit, 1.0, 0.0)


def _put3(base, lane, first, x):
    hi = x.astype(BF16).astype(F32)
    rest = x - hi
    mid = rest.astype(BF16).astype(F32)
    lo = (rest - mid).astype(BF16).astype(F32)
    for j, piece in enumerate((hi, mid, lo)):
        base = jnp.where(lane == first + j, piece, base)
    return base


SIBLING = 1
SAME_CORE = (2, 4, 6)
ALL_PEERS = (1, 2, 3, 4, 5, 6, 7)


def _place():
    return lax.axis_index("x"), lax.axis_index("y"), lax.axis_index("c")


def _peer(r):
    x, y, c = _place()
    return (1 - x if r & 4 else x, 1 - y if r & 2 else y, 1 - c if r & 1 else c)


def _device_slot(p):
    return 4 * p[0] + 2 * p[1] + p[2]


def _chip_slot(p):
    return 2 * p[0] + p[1]


def _exchange(name, items):
    n = len(items)

    def body(*refs):
        copies = _exchange_copies(items, refs[:n], refs[n:2 * n], *refs[2 * n:])
        for cp in copies:
            cp.start()
        for cp in copies:
            cp.wait()

    hbm = pl.BlockSpec(memory_space=pl.ANY)
    return pl.pallas_call(
        body, name=name, out_shape=_exchange_results(items),
        in_specs=[hbm] * n, out_specs=[hbm] * n,
        scratch_shapes=_exchange_semaphores(n),
    )(*[a for _, a, _ in items])


def _exchange_results(items):
    return [jax.ShapeDtypeStruct(((N_DEV,) if kind == "gather" else ()) + a.shape, a.dtype) for kind, a, _ in items]


def _exchange_semaphores(n):
    return [pltpu.SemaphoreType.DMA((n, N_DEV - 1)), pltpu.SemaphoreType.DMA((n, N_DEV - 1)),
            pltpu.SemaphoreType.DMA((n,))]


def _exchange_copies(items, ins, outs, send_sems, recv_sems, local_sems):
    me = _place()
    copies = []
    for a, (kind, _, peers) in enumerate(items):
        slot = _chip_slot if kind == "chips" else _device_slot
        for r in peers:
            peer = _peer(r)
            src = ins[a] if kind in ("swap", "gather") else ins[a].at[slot(peer)]
            dst = outs[a] if kind == "swap" else outs[a].at[slot(me)]
            copies.append(pltpu.make_async_remote_copy(
                src_ref=src, dst_ref=dst, send_sem=send_sems.at[a, r - 1], recv_sem=recv_sems.at[a, r - 1],
                device_id=peer, device_id_type=MESH))
        if kind != "swap":
            src = ins[a] if kind == "gather" else ins[a].at[slot(me)]
            copies.append(pltpu.make_async_copy(src, outs[a].at[slot(me)], local_sems.at[a]))
    return copies


def _gather_two_level(name, arrays, halves):
    n = len(arrays)
    x_flip, y_flip, both = 4, 2, 6

    def body(*refs):
        ins, outs = refs[:n], refs[n:2 * n]
        send_sems, recv_sems, local_sems = refs[2 * n:]
        me, sibling = _place(), _peer(SIBLING)
        xn, yn, dg = _peer(x_flip), _peer(y_flip), _peer(both)

        def part(a, block, half):
            rows = outs[a].at[_device_slot(block)]
            if half is None:
                return rows
            return rows.at[pl.ds(0, halves[a])] if half == 0 else rows.at[pl.ds(halves[a], arrays[a].shape[0] - halves[a])]

        def copy(a, k, block, half, to, src=None):
            dst = part(a, block, half)
            return pltpu.make_async_remote_copy(
                src_ref=dst if src is None else src, dst_ref=dst,
                send_sem=send_sems.at[a, k], recv_sem=recv_sems.at[a, k], device_id=to, device_id_type=MESH)

        sends, own = [], []

        def start(cp):
            cp.start()
            sends.append(cp)

        for a in range(n):
            mine = pltpu.make_async_copy(ins[a], outs[a].at[_device_slot(me)], local_sems.at[a])
            mine.start()
            own.append(mine)
            for k, to in enumerate((sibling, xn, yn)):
                start(copy(a, k, me, None, to, src=ins[a]))
        for a in range(n):
            copy(a, 1, xn, None, me).wait_recv()
            start(copy(a, 3, xn, 0, yn))
            start(copy(a, 5, xn, None, sibling))
        for a in range(n):
            copy(a, 2, yn, None, me).wait_recv()
            start(copy(a, 4, yn, 1, xn))
            start(copy(a, 6, yn, None, sibling))
        for a in range(n):
            copy(a, 3, dg, 0, me).wait_recv()
            copy(a, 4, dg, 1, me).wait_recv()
            start(copy(a, 7, dg, None, sibling))
        for a in range(n):
            copy(a, 0, sibling, None, me).wait_recv()
            for k, r in ((5, x_flip), (6, y_flip), (7, both)):
                copy(a, k, _peer(r | SIBLING), None, me).wait_recv()
        for cp in sends:
            cp.wait_send()
        for cp in own:
            cp.wait()

    hbm = pl.BlockSpec(memory_space=pl.ANY)
    return pl.pallas_call(
        body, name=name, out_shape=[jax.ShapeDtypeStruct((N_DEV,) + a.shape, a.dtype) for a in arrays],
        in_specs=[hbm] * n, out_specs=[hbm] * n,
        scratch_shapes=[pltpu.SemaphoreType.DMA((n, 8)), pltpu.SemaphoreType.DMA((n, 8)),
                        pltpu.SemaphoreType.DMA((n,))],
    )(*arrays)


def _forward_in(x, tile0, norm_g, w_main, w_f, b_f, pool_w, pool_scale, w_up_pool):
    seq = x.shape[0]
    nt = seq // ROW_TILE + 1
    lp = nt * ROW_TILE
    tm = ROW_TILE

    def body(x_ref, t0_ref, g_ref, wa_ref, wf_ref, bf_ref, pw_ref, sc_ref, wup_ref,
             h_ref, u_ref, zp_ref, k_ref, v_ref, qt_ref, kt_ref, vt_ref, sn_ref, ap_ref,
             uext_ref, carry_ref):
        i = pl.program_id(0)

        @pl.when(i == 0)
        def _():
            uext_ref[...] = jnp.zeros_like(uext_ref)
            carry_ref[...] = jnp.zeros_like(carry_ref)

        xt = jnp.where(i == 0, t0_ref[...], x_ref[...])
        r = lax.rsqrt(jnp.mean(xt * xt, axis=-1, keepdims=True) + RMS_EPS)
        h = (xt * r * g_ref[...]).astype(BF16)
        h_ref[...] = h
        pa = _dot_nt(h, wa_ref[...])
        u = pa[:, :512]
        zp = pa[:, 512:1024]
        u_ref[...] = u
        zp_ref[...] = zp

        uext_ref[0:MAX_WINDOW, :] = uext_ref[tm:tm + MAX_WINDOW, :]
        uext_ref[MAX_WINDOW:, :] = u
        counts = _pool_counts(i * tm, tm)
        ps = _pool_means(uext_ref[...], u, counts)
        ppw = jnp.concatenate([_dot(ps[g].astype(BF16), pw_ref[g]) for g in range(4)], axis=1)
        y_pool = ppw * sc_ref[...] * (zp * _sigmoid(zp))
        ap_ref[...] = _dot(y_pool.astype(BF16), wup_ref[...]).astype(BF16)

        fl = _dot_nt(h, wf_ref[...]) + bf_ref[...]
        row = i * tm + lax.broadcasted_iota(jnp.int32, (tm, LANES), 0)
        rloc = lax.broadcasted_iota(jnp.int32, (tm, LANES), 0)
        lane = lax.broadcasted_iota(jnp.int32, (tm, LANES), 1)
        live = (row >= PAD) & (lane < N_HEADS)
        logf = jnp.minimum(fl, 0.0) - jnp.log1p(jnp.exp(-jnp.abs(fl)))
        cs = jnp.where(live, logf, 0.0)
        sh = 1
        while sh < tm:
            cs = cs + jnp.where(rloc >= sh, pltpu.roll(cs, sh, axis=0), 0.0)
            sh *= 2
        cs = cs + carry_ref[...]
        carry_ref[...] = cs[tm - 1:tm, :]
        sn_ref[...] = jnp.where(live, _sigmoid(-fl), 0.0)

        rows1 = i * tm + lax.broadcasted_iota(jnp.int32, (tm, 1), 0)
        ones_q = _lane_ones(lane, ((Q_ONES, Q_ONES + 3),))
        ones_k = _lane_ones(lane, ((K_ONES, K_ONES + 3), (K_ONES2, K_ONES2 + 3)))
        ones_v = _lane_ones(lane, ((V_ONES, V_ONES + 3),))
        for hp in range(N_HEADS // 2):
            qp = pa[:, 1024 + LANES * hp:1024 + LANES * (hp + 1)] * 0.125
            kp = pa[:, 1536 + LANES * hp:1536 + LANES * (hp + 1)]
            vp = pa[:, 2048 + LANES * hp:2048 + LANES * (hp + 1)]
            for e in range(2):
                head = 2 * hp + e
                if e:
                    qp, kp, vp = (pltpu.roll(a, HEAD_DIM, axis=1) for a in (qp, kp, vp))
                c_h = cs[:, head:head + 1]
                q_h = jnp.where(lane < HEAD_DIM, qp, _put3(ones_q, lane, Q_BIAS, c_h))
                qt_ref[head] = q_h.T.astype(BF16)
                minus_ck = jnp.where(rows1 >= PAD, -c_h, NEG)
                k_h = jnp.where(lane < HEAD_DIM, kp, _put3(ones_k, lane, K_BIAS, minus_ck))
                k_ref[head] = k_h.astype(BF16)
                kt_ref[head] = k_h.T.astype(BF16)
                v_h = jnp.where(lane < HEAD_DIM, vp, ones_v)
                v_ref[head] = v_h.astype(BF16)
                vt_ref[head] = v_h.T.astype(BF16)

    row_f32 = lambda w: pl.BlockSpec((tm, w), lambda i: (i, 0))
    out_shape = [
        jax.ShapeDtypeStruct((lp, D_MODEL), BF16),
        jax.ShapeDtypeStruct((lp, POOL_WIDTH), F32),
        jax.ShapeDtypeStruct((lp, POOL_WIDTH), F32),
        jax.ShapeDtypeStruct((N_HEADS, lp, LANES), BF16),
        jax.ShapeDtypeStruct((N_HEADS, lp, LANES), BF16),
        jax.ShapeDtypeStruct((N_HEADS, LANES, lp), BF16),
        jax.ShapeDtypeStruct((N_HEADS, LANES, lp), BF16),
        jax.ShapeDtypeStruct((N_HEADS, LANES, lp), BF16),
        jax.ShapeDtypeStruct((lp, LANES), F32),
        jax.ShapeDtypeStruct((lp, D_MODEL), BF16),
    ]
    heads = pl.BlockSpec((N_HEADS, tm, LANES), lambda i: (0, i, 0))
    heads_t = pl.BlockSpec((N_HEADS, LANES, tm), lambda i: (0, 0, i))
    out_specs = [row_f32(D_MODEL), row_f32(512), row_f32(512), heads, heads, heads_t, heads_t, heads_t,
                 row_f32(LANES), row_f32(D_MODEL)]
    in_specs = [
        pl.BlockSpec((tm, D_MODEL), lambda i: (jnp.maximum(i - 1, 0), 0)),
        _const((tm, D_MODEL)), _const((1, D_MODEL)),
        _const((2560, D_MODEL)), _const((LANES, D_MODEL)), _const((1, LANES)),
        _const((4, POOL_GROUP, POOL_GROUP)), _const((1, POOL_WIDTH)), _const((POOL_WIDTH, D_MODEL)),
    ]
    return pl.pallas_call(
        body, name="forward_in", grid=(nt,), out_shape=out_shape, in_specs=in_specs, out_specs=out_specs,
        scratch_shapes=[pltpu.VMEM((tm + MAX_WINDOW, POOL_WIDTH), F32), pltpu.VMEM((1, LANES), F32)],
        compiler_params=_params(("arbitrary",)),
    )(x, tile0, norm_g, w_main, w_f, b_f, pool_w, pool_scale, w_up_pool)


def _causal(tb):
    return lax.broadcasted_iota(jnp.int32, (tb, tb), 1) <= lax.broadcasted_iota(jnp.int32, (tb, tb), 0)


def _pair_lanes(a0, a1):
    lane = lax.broadcasted_iota(jnp.int32, a0.shape, 1)
    return jnp.where(lane < HEAD_DIM, a0, pltpu.roll(a1, HEAD_DIM, axis=1))


def _behind(items, ins, outs, sems):
    step, last = pl.program_id(0), pl.num_programs(0) - 1

    @pl.when(step == 0)
    def _():
        for cp in _exchange_copies(items, ins, outs, *sems):
            cp.start()

    def finish():
        @pl.when(step == last)
        def _():
            for cp in _exchange_copies(items, ins, outs, *sems):
                cp.wait()

    return finish


def _attention_forward(qt, k, vt, behind):
    lp = k.shape[1]
    tk = ATT_TILE
    tq_big = ATT_Q_BLOCKS * tk
    n_big = (lp // tk - 1) // ATT_Q_BLOCKS
    assert lp == tk + n_big * tq_big and ATT_Q_BLOCKS % 2 == 0
    nx = len(behind)

    def body(qt_ref, k_ref, vt_ref, *rest):
        o_ref, lse_ref = rest[nx:nx + 2]
        s_buf, m_scr, acc_scr = rest[2 * nx + 2:2 * nx + 5]
        finish_exchange = _behind(behind, rest[:nx], rest[nx + 2:2 * nx + 2], rest[2 * nx + 5:])

        def q_tile(q0, tq, pairs):
            first = q0 // tk
            qts = [qt_ref[e, :, pl.ds(q0, tq)] for e in range(2)]

            def block(kj):
                return pl.ds(kj * tk if isinstance(kj, int) else pl.multiple_of(kj * tk, tk), tk)

            def step(kj, rd, wr, c0=0, diagonal=False):
                c1 = c0 + tk if diagonal else c0
                for e in range(2):
                    s = s_buf[rd, e, :, c0:tq]
                    if wr is not None:
                        s_buf[wr, e, :, c1:tq] = _dot(k_ref[e, block(kj + 1), :], qts[e][:, c1:tq])
                    if diagonal:
                        keys = lax.broadcasted_iota(jnp.int32, s.shape, 0)
                        s = jnp.where(keys <= lax.broadcasted_iota(jnp.int32, s.shape, 1), s, NEG)
                    m = m_scr[e, :, c0:tq]
                    m_new = jnp.maximum(m, jnp.max(s, axis=0, keepdims=True))
                    p = jnp.exp(s - m_new)
                    pv = _dot(vt_ref[e, :, block(kj)], p.astype(BF16))
                    acc_scr[e, :, c0:tq] = jnp.exp(m - m_new) * acc_scr[e, :, c0:tq] + pv
                    m_scr[e, :, c0:tq] = m_new

            for e in range(2):
                m_scr[e, :, 0:tq] = jnp.full((1, tq), NEG, F32)
                acc_scr[e, :, 0:tq] = jnp.zeros((LANES, tq), F32)
                s_buf[0, e, :, 0:tq] = _dot(k_ref[e, block(0), :], qts[e])
            if pairs is None:
                step(0, 0, None, 0, True)
            else:
                step(0, 0, 1)

                def two_steps(t, _):
                    step(1 + 2 * t, 1, 0)
                    step(2 + 2 * t, 0, 1)
                    return 0

                lax.fori_loop(0, pairs, two_steps, 0)
                for b in range(tq // tk):
                    step(first + b, (b + 1) % 2, b % 2 if (b + 1) * tk < tq else None, b * tk, True)
            outs, lses = [], []
            for e in range(2):
                acc = acc_scr[e, :, 0:tq]
                l = acc[V_ONES:V_ONES + 1, :]
                outs.append((acc / l).T)
                lses.append(m_scr[e, :, 0:tq] + jnp.log(l))
            o_ref[pl.ds(q0, tq), :] = _pair_lanes(outs[0], outs[1]).astype(BF16)
            lse_rows = jnp.concatenate(lses + [jnp.zeros((LANES - 2, tq), F32)], axis=0)
            lse_ref[pl.ds(q0, tq), :] = lse_rows.T

        q_tile(0, tk, None)

        def big_tile(i, _):
            q_tile(pl.multiple_of(tk + i * tq_big, tk), tq_big, (ATT_Q_BLOCKS // 2) * i)
            return 0

        lax.fori_loop(0, n_big, big_tile, 0)
        finish_exchange()

    pair = pl.BlockSpec((lp, LANES), lambda hp: (0, hp))
    heads = pl.BlockSpec((2, lp, LANES), lambda hp: (hp, 0, 0), pipeline_mode=pl.Buffered(1))
    heads_t = pl.BlockSpec((2, LANES, lp), lambda hp: (hp, 0, 0), pipeline_mode=pl.Buffered(1))
    hbm = pl.BlockSpec(memory_space=pl.ANY)
    return pl.pallas_call(
        body, name="attention_forward", grid=(N_HEADS // 2,),
        out_shape=[jax.ShapeDtypeStruct((lp, ATTN_WIDTH), BF16), jax.ShapeDtypeStruct((lp, ATTN_WIDTH), F32)]
        + _exchange_results(behind),
        in_specs=[heads_t, heads, heads_t] + [hbm] * nx,
        out_specs=[pair, pair] + [hbm] * nx,
        scratch_shapes=[pltpu.VMEM((2, 2, tk, tq_big), F32), pltpu.VMEM((2, 1, tq_big), F32),
                        pltpu.VMEM((2, LANES, tq_big), F32)] + _exchange_semaphores(nx),
        compiler_params=_params(("arbitrary",)),
    )(qt, k, vt, *[a for _, a, _ in behind])


def _rows3(first, x):
    sub = lax.broadcasted_iota(jnp.int32, (LANES, x.shape[1]), 0)
    hi = x.astype(BF16).astype(F32)
    rest = x - hi
    mid = rest.astype(BF16).astype(F32)
    lo = (rest - mid).astype(BF16).astype(F32)
    out = jnp.zeros((LANES, x.shape[1]), F32)
    for j, piece in enumerate((hi, mid, lo)):
        out = jnp.where(sub == first + j, piece, out)
    return out


def _attention_backward(qt, k, kt, v, do, o, lse, behind):
    lp = k.shape[1]
    tb = ATT_TILE
    nb = lp // tb
    tq_big = ATT_Q_BLOCKS * tb
    n_big = (nb - 1) // ATT_Q_BLOCKS
    assert lp == tb + n_big * tq_big and ATT_Q_BLOCKS % 2 == 0
    nx = len(behind)

    def body(qt_ref, k_ref, kt_ref, v_ref, do_ref, o_ref, lse_ref, *rest):
        dq_ref, dk_ref, dv_ref, dc_ref = rest[nx:nx + 4]
        q2_ref, do2_ref, dk_acc, dv_acc, dq_scr, s_buf = rest[2 * nx + 4:2 * nx + 10]
        finish_exchange = _behind(behind, rest[:nx], rest[nx + 4:2 * nx + 4], rest[2 * nx + 10:])
        sub = lax.broadcasted_iota(jnp.int32, (LANES, tb), 0)

        def lanes01(row0, row1):
            n = row0.shape[1]
            return jnp.concatenate([row0, row1, jnp.zeros((LANES - 2, n), F32)], axis=0).T

        def prepare(bi, _):
            r0 = pl.multiple_of(bi * tb, tb)
            queries = r0 + lax.broadcasted_iota(jnp.int32, (1, tb), 1)
            dob = do_ref[pl.ds(r0, tb), :].astype(F32)
            do_t = dob.T
            dd_t = (dob * o_ref[pl.ds(r0, tb), :].astype(F32)).T
            lse_t = lse_ref[pl.ds(r0, tb), :].T
            for e in range(2):
                delta = jnp.sum(dd_t[HEAD_DIM * e:HEAD_DIM * (e + 1), :], axis=0, keepdims=True)
                do_e = jnp.concatenate([do_t[HEAD_DIM * e:HEAD_DIM * (e + 1), :], jnp.zeros((HEAD_DIM, tb), F32)], axis=0)
                do2_ref[e, :, pl.ds(r0, tb)] = jnp.where(sub < HEAD_DIM, do_e, _rows3(DO_BIAS, -delta)).astype(BF16)
                minus_lse = jnp.where(queries >= PAD, -lse_t[e:e + 1, :], NEG)
                keep = (sub < Q_LSE) | (sub >= Q_LSE + 3)
                q2_ref[e, :, pl.ds(r0, tb)] = jnp.where(keep, qt_ref[e, :, pl.ds(r0, tb)].astype(F32),
                                                        _rows3(Q_LSE, minus_lse)).astype(BF16)
            return 0

        lax.fori_loop(0, nb, prepare, 0)
        dk_acc[...] = jnp.zeros_like(dk_acc)
        dv_acc[...] = jnp.zeros_like(dv_acc)

        def q_tile(q0, tq, pairs):
            first = q0 // tb
            qts = [q2_ref[e, :, pl.ds(q0, tq)] for e in range(2)]
            dots = [do2_ref[e, :, pl.ds(q0, tq)] for e in range(2)]

            def block(kj):
                return pl.ds(kj * tb if isinstance(kj, int) else pl.multiple_of(kj * tb, tb), tb)

            def step(kj, rd, wr, c0=0, diagonal=False):
                c1 = c0 + tb if diagonal else c0
                for e in range(2):
                    s = s_buf[rd, e, :, c0:tq]
                    if wr is not None:
                        s_buf[wr, e, :, c1:tq] = _dot(k_ref[e, block(kj + 1), :], qts[e][:, c1:tq])
                    dpd = _dot(v_ref[e, block(kj), :], dots[e][:, c0:tq])
                    p = jnp.exp(s)
                    if diagonal:
                        keys = lax.broadcasted_iota(jnp.int32, s.shape, 0)
                        p = jnp.where(keys <= lax.broadcasted_iota(jnp.int32, s.shape, 1), p, 0.0)
                    dsb = (p * dpd).astype(BF16)
                    dv_acc[e, :, block(kj)] += _dot_nt(dots[e][:, c0:tq], p.astype(BF16))
                    dk_acc[e, :, block(kj)] += _dot_nt(qts[e][:, c0:tq], dsb)
                    dq_scr[e, :, c0:tq] += _dot(kt_ref[e, :, block(kj)], dsb)

            for e in range(2):
                dq_scr[e, :, 0:tq] = jnp.zeros((LANES, tq), F32)
                s_buf[0, e, :, 0:tq] = _dot(k_ref[e, block(0), :], qts[e])
            if pairs is None:
                step(0, 0, None, 0, True)
            else:
                step(0, 0, 1)

                def two_steps(t, _):
                    step(1 + 2 * t, 1, 0)
                    step(2 + 2 * t, 0, 1)
                    return 0

                lax.fori_loop(0, pairs, two_steps, 0)
                for b in range(tq // tb):
                    step(first + b, (b + 1) % 2, b % 2 if (b + 1) * tb < tq else None, b * tb, True)
            dq0, dq1 = dq_scr[0, :, 0:tq], dq_scr[1, :, 0:tq]
            dq_ref[pl.ds(q0, tq), :] = (_pair_lanes(dq0.T, dq1.T) * 0.125).astype(BF16)
            dc_ref[pl.ds(q0, tq), :] = lanes01(dq0[K_ONES:K_ONES + 1, :], dq1[K_ONES:K_ONES + 1, :])

        q_tile(0, tb, None)

        def big_tile(i, _):
            q_tile(pl.multiple_of(tb + i * tq_big, tb), tq_big, (ATT_Q_BLOCKS // 2) * i)
            return 0

        lax.fori_loop(0, n_big, big_tile, 0)

        def finish(bi, _):
            r0 = pl.multiple_of(bi * tb, tb)
            dk0, dk1 = dk_acc[0, :, pl.ds(r0, tb)], dk_acc[1, :, pl.ds(r0, tb)]
            dk_ref[pl.ds(r0, tb), :] = _pair_lanes(dk0.T, dk1.T).astype(BF16)
            dv_ref[pl.ds(r0, tb), :] = _pair_lanes(dv_acc[0, :, pl.ds(r0, tb)].T, dv_acc[1, :, pl.ds(r0, tb)].T).astype(BF16)
            dc_ref[pl.ds(r0, tb), :] = dc_ref[pl.ds(r0, tb), :] - lanes01(dk0[Q_ONES:Q_ONES + 1, :], dk1[Q_ONES:Q_ONES + 1, :])
            return 0

        lax.fori_loop(0, nb, finish, 0)
        finish_exchange()

    once = pl.Buffered(1)
    pair = pl.BlockSpec((lp, LANES), lambda hp: (0, hp))
    pair_in = pl.BlockSpec((lp, LANES), lambda hp: (0, hp), pipeline_mode=once)
    heads = pl.BlockSpec((2, lp, LANES), lambda hp: (hp, 0, 0), pipeline_mode=once)
    heads_t = pl.BlockSpec((2, LANES, lp), lambda hp: (hp, 0, 0), pipeline_mode=once)
    hbm = pl.BlockSpec(memory_space=pl.ANY)
    wide = jax.ShapeDtypeStruct((lp, ATTN_WIDTH), BF16)
    return pl.pallas_call(
        body, name="attention_backward", grid=(N_HEADS // 2,),
        out_shape=[wide, wide, wide, jax.ShapeDtypeStruct((lp, ATTN_WIDTH), F32)] + _exchange_results(behind),
        in_specs=[heads_t, heads, heads_t, heads, pair_in, pair_in, pair_in] + [hbm] * nx,
        out_specs=[pair, pair, pair, pair] + [hbm] * nx,
        scratch_shapes=[pltpu.VMEM((2, LANES, lp), BF16), pltpu.VMEM((2, LANES, lp), BF16),
                        pltpu.VMEM((2, LANES, lp), F32), pltpu.VMEM((2, LANES, lp), F32),
                        pltpu.VMEM((2, LANES, tq_big), F32), pltpu.VMEM((2, 2, tb, tq_big), F32)]
        + _exchange_semaphores(nx),
        compiler_params=_params(("arbitrary",)),
    )(qt, k, kt, v, do, o, lse, *[a for _, a, _ in behind])


def _middle(x, target, h, o, a_pool, u, zp, w_main, w_up_pool, w_up_attn, w_out, pool_w, pool_scale, final_g):
    seq = x.shape[0]
    tm = ROW_TILE
    nt = seq // tm + 1
    lp = nt * tm
    halo_blocks = tm // MAX_WINDOW

    def body(x_ref, t_ref, h_ref, o_ref, ap_ref, u_ref, uh_ref, zp_ref,
             wc_ref, wupp_ref, wupa_ref, wout_ref, pw_ref, sc_ref, gf_ref,
             dh2_ref, mg_ref, yp_ref, ya_ref, dap_ref, daa_ref, do_ref, dza_ref, dgp_ref, dga_ref, dzp_ref, dpn_ref,
             loss_ref, dgf_ref, dsc_ref, dpw_ref):
        i = pl.program_id(0)
        tiles = (dh2_ref, mg_ref, yp_ref, ya_ref, dap_ref, daa_ref, do_ref, dza_ref, dgp_ref, dga_ref, dzp_ref, dpn_ref)

        @pl.when(i == 0)
        def _():
            for ref in tiles + (loss_ref, dgf_ref, dsc_ref, dpw_ref):
                ref[...] = jnp.zeros_like(ref)

        @pl.when(i > 0)
        def _():
            xt = x_ref[...]
            hb = h_ref[...]
            pc = _dot_nt(hb, wc_ref[...])
            za, gp, ga = pc[:, :512], pc[:, 512:1536], pc[:, 1536:]
            of = o_ref[...].astype(F32)
            sza = _sigmoid(za)
            silu_za = za * sza
            ya = (of * silu_za).astype(BF16)
            ya_ref[...] = ya
            aa = _dot(ya, wupa_ref[...])
            ap = ap_ref[...].astype(F32)
            sgp, sga = _sigmoid(gp), _sigmoid(ga)
            mg = (sgp * ap + sga * aa).astype(BF16)
            mg_ref[...] = mg
            h2 = xt + _dot(mg, wout_ref[...])
            r2 = lax.rsqrt(jnp.mean(h2 * h2, axis=-1, keepdims=True) + RMS_EPS)
            h2n = h2 * r2
            gf = gf_ref[...]
            diff = h2n * gf - t_ref[...]
            loss_ref[...] += 0.5 * jnp.sum(jnp.mean(diff * diff, axis=-1, keepdims=True), axis=0, keepdims=True)
            dy = diff * (1.0 / D_MODEL)
            dgf_ref[...] += jnp.sum(dy * h2n, axis=0, keepdims=True)
            dyg = dy * gf
            dh2 = r2 * (dyg - h2n * jnp.mean(dyg * h2n, axis=-1, keepdims=True))
            dh2_ref[...] = dh2
            dmg = _dot_nt(dh2.astype(BF16), wout_ref[...])
            dap = (dmg * sgp).astype(BF16)
            daa = (dmg * sga).astype(BF16)
            dap_ref[...] = dap
            daa_ref[...] = daa
            dgp_ref[...] = (dmg * ap * sgp * (1.0 - sgp)).astype(BF16)
            dga_ref[...] = (dmg * aa * sga * (1.0 - sga)).astype(BF16)
            dyp = _dot_nt(dap, wupp_ref[...])
            dya = _dot_nt(daa, wupa_ref[...])
            do_ref[...] = (dya * silu_za).astype(BF16)
            dza_ref[...] = (dya * of * (sza * (1.0 + za * (1.0 - sza)))).astype(BF16)

            u = u_ref[...]
            zp = zp_ref[...]
            counts = _pool_counts(i * tm, tm)
            ps = _pool_means(jnp.concatenate([uh_ref[...], u], axis=0), u, counts)
            pbs = [p.astype(BF16) for p in ps]
            ppw = jnp.concatenate([_dot(pbs[g], pw_ref[g]) for g in range(4)], axis=1)
            sc = sc_ref[...]
            szp = _sigmoid(zp)
            silu_zp = zp * szp
            ypre = ppw * sc
            yp_ref[...] = (ypre * silu_zp).astype(BF16)
            dypre = dyp * silu_zp
            dzp_ref[...] = (dyp * ypre * (szp * (1.0 + zp * (1.0 - szp)))).astype(BF16)
            dsc_ref[...] += jnp.sum(dypre * ppw, axis=0, keepdims=True)
            dppw = (dypre * sc).astype(BF16)
            dpns = []
            for g in range(4):
                dg = dppw[:, POOL_GROUP * g:POOL_GROUP * (g + 1)]
                dpw_ref[g] += _dot_tn(pbs[g], dg)
                dpns.append(_dot_nt(dg, pw_ref[g]) / counts[g])
            dpn_ref[...] = jnp.concatenate(dpns, axis=1)

    real = lambda w: pl.BlockSpec((tm, w), lambda i: (jnp.maximum(i - 1, 0), 0))
    row = lambda w: pl.BlockSpec((tm, w), lambda i: (i, 0))
    in_specs = [
        real(D_MODEL), real(D_MODEL), row(D_MODEL), row(512), row(D_MODEL), row(512),
        pl.BlockSpec((MAX_WINDOW, 512), lambda i: (jnp.maximum(i * halo_blocks - 1, 0), 0)), row(512),
        _const((2560, D_MODEL), (1, 0)), _const((POOL_WIDTH, D_MODEL)), _const((ATTN_WIDTH, D_MODEL)),
        _const((D_MODEL, D_MODEL)), _const((4, POOL_GROUP, POOL_GROUP)), _const((1, POOL_WIDTH)), _const((1, D_MODEL)),
    ]
    sd = jax.ShapeDtypeStruct
    out_shape = [
        sd((lp, D_MODEL), F32),
        sd((lp, D_MODEL), BF16),
        sd((lp, 512), BF16),
        sd((lp, 512), BF16),
        sd((lp, D_MODEL), BF16),
        sd((lp, D_MODEL), BF16),
        sd((lp, 512), BF16),
        sd((lp, 512), BF16),
        sd((lp, D_MODEL), BF16),
        sd((lp, D_MODEL), BF16),
        sd((lp, 512), BF16),
        sd((lp, 512), F32),
        sd((1, LANES), F32),
        sd((1, D_MODEL), F32),
        sd((1, 512), F32),
        sd((4, POOL_GROUP, POOL_GROUP), F32),
    ]
    keep = lambda shape: pl.BlockSpec(shape, lambda i: (0,) * len(shape))
    out_specs = [row(D_MODEL), row(D_MODEL), row(512), row(512), row(D_MODEL), row(D_MODEL), row(512), row(512),
                 row(D_MODEL), row(D_MODEL), row(512), row(512),
                 keep((1, LANES)), keep((1, D_MODEL)), keep((1, 512)), keep((4, POOL_GROUP, POOL_GROUP))]
    return pl.pallas_call(
        body, name="middle", grid=(nt,), out_shape=out_shape, in_specs=in_specs, out_specs=out_specs,
        compiler_params=_params(("arbitrary",)),
    )(x, target, h, o, a_pool, u, u, zp, w_main, w_up_pool, w_up_attn, w_out, pool_w, pool_scale, final_g)


def _backward_in(x, tile0, norm_g, dh2, dpn, dzp, dq, dk, dv, dza, dgp, dga, dc, sneg, w_main, w_f):
    seq = x.shape[0]
    tm = ROW_TILE
    nt = seq // tm + 1
    lp = nt * tm
    halo_blocks = tm // MAX_WINDOW
    last_halo = lp // MAX_WINDOW - 1

    def body(x_ref, t0_ref, g_ref, dh2_ref, dpn_ref, dpnh_ref, dzp_ref, dq_ref, dk_ref, dv_ref, dza_ref,
             dgp_ref, dga_ref, dc_ref, sn_ref, wm_ref, wf_ref,
             dproj_ref, df_ref, gx_ref, gmeta_ref, dg_ref, dbf_ref, carry_ref):
        i = pl.program_id(0)
        t = nt - 1 - i

        @pl.when(i == 0)
        def _():
            carry_ref[...] = jnp.zeros_like(carry_ref)
            dg_ref[...] = jnp.zeros_like(dg_ref)
            dbf_ref[...] = jnp.zeros_like(dbf_ref)

        dpn_t = dpn_ref[...]
        ahead = jnp.where(i == 0, jnp.zeros_like(dpnh_ref), dpnh_ref[...])
        ext = jnp.concatenate([dpn_t, ahead], axis=0)
        counts = _pool_counts(t * tm, tm)
        for g, w in enumerate(POOL_WINDOWS):
            s = ext[:, POOL_GROUP * g:POOL_GROUP * (g + 1)]
            sh = 1
            while sh < w:
                s = s + pltpu.roll(s, tm + MAX_WINDOW - sh, axis=0)
                sh *= 2
            du = s[:tm, :] - dpn_t[:, POOL_GROUP * g:POOL_GROUP * (g + 1)] * counts[g]
            dproj_ref[:, POOL_GROUP * g:POOL_GROUP * (g + 1)] = du.astype(BF16)
        dproj_ref[:, 512:1024] = dzp_ref[...]
        dproj_ref[:, 1024:1536] = dq_ref[...]
        dproj_ref[:, 1536:2048] = dk_ref[...]
        dproj_ref[:, 2048:2560] = dv_ref[...]
        dproj_ref[:, 2560:3072] = dza_ref[...]
        dproj_ref[:, 3072:4096] = dgp_ref[...]
        dproj_ref[:, 4096:5120] = dga_ref[...]

        dct = dc_ref[:, 0:LANES]
        for hp in range(1, N_HEADS // 2):
            dct = dct + pltpu.roll(dc_ref[:, LANES * hp:LANES * (hp + 1)], 2 * hp, axis=1)
        rloc = lax.broadcasted_iota(jnp.int32, (tm, LANES), 0)
        sh = 1
        while sh < tm:
            dct = dct + jnp.where(rloc + sh < tm, pltpu.roll(dct, tm - sh, axis=0), 0.0)
            sh *= 2
        dct = dct + carry_ref[...]
        carry_ref[...] = dct[0:1, :]
        df = dct * sn_ref[...]
        dbf_ref[...] += jnp.sum(df, axis=0, keepdims=True)
        dfb = df.astype(BF16)
        df_ref[...] = dfb

        dh = _dot(dproj_ref[...], wm_ref[...]) + _dot(dfb, wf_ref[...])
        xt = jnp.where(t == 0, t0_ref[...], x_ref[...])
        r = lax.rsqrt(jnp.mean(xt * xt, axis=-1, keepdims=True) + RMS_EPS)
        xn = xt * r
        dg_ref[...] += jnp.sum(dh * xn, axis=0, keepdims=True)
        dhg = dh * g_ref[...]
        dx = dh2_ref[...] + r * (dhg - xn * jnp.mean(dhg * xn, axis=-1, keepdims=True))

        @pl.when(t > 0)
        def _():
            gx_ref[...] = dx

        @pl.when(t == 0)
        def _():
            gmeta_ref[...] = dx[PAD:, :]

    rev = lambda w: pl.BlockSpec((tm, w), lambda i: (nt - 1 - i, 0))
    real = pl.BlockSpec((tm, D_MODEL), lambda i: (jnp.maximum(nt - 2 - i, 0), 0))
    in_specs = [
        real, _const((tm, D_MODEL)), _const((1, D_MODEL)), rev(D_MODEL), rev(512),
        pl.BlockSpec((MAX_WINDOW, 512), lambda i: (jnp.minimum((nt - i) * halo_blocks, last_halo), 0)),
        rev(512), rev(512), rev(512), rev(512), rev(512), rev(D_MODEL), rev(D_MODEL),
        rev(512), rev(LANES),
        _const((N_MAIN, D_MODEL)), _const((LANES, D_MODEL)),
    ]
    sd = jax.ShapeDtypeStruct
    out_shape = [sd((lp, N_MAIN), BF16), sd((lp, LANES), BF16), sd((seq, D_MODEL), F32), sd((N_META, D_MODEL), F32),
                 sd((1, D_MODEL), F32), sd((1, LANES), F32)]
    keep = lambda shape: pl.BlockSpec(shape, lambda i: (0,) * len(shape))
    out_specs = [rev(N_MAIN), rev(LANES), real, keep((N_META, D_MODEL)), keep((1, D_MODEL)), keep((1, LANES))]
    return pl.pallas_call(
        body, name="backward_in", grid=(nt,), out_shape=out_shape, in_specs=in_specs, out_specs=out_specs,
        scratch_shapes=[pltpu.VMEM((1, LANES), F32)],
        compiler_params=_params(("arbitrary",)),
    )(x, tile0, norm_g, dh2, dpn, dpn, dzp, dq, dk, dv, dza, dgp, dga, dc, sneg, w_main, w_f)


def _matmul_tn(name, a, b, tn):
    lp, m = a.shape
    n = b.shape[1]

    def body(a_ref, b_ref, c_ref):
        c_ref[...] = _dot_tn(a_ref[...].astype(BF16), b_ref[...].astype(BF16))

    return pl.pallas_call(
        body, name=name, grid=(n // tn,), out_shape=jax.ShapeDtypeStruct((m, n), F32),
        in_specs=[_const((lp, m)), pl.BlockSpec((lp, tn), lambda j: (0, j))],
        out_specs=pl.BlockSpec((m, tn), lambda j: (0, j)),
        compiler_params=_params(("arbitrary",)),
    )(a, b)


def _matmul_tn_rows(name, a, b, tm):
    lp, m = a.shape
    n = b.shape[1]

    def body(a_ref, b_ref, c_ref):
        c_ref[...] = _dot_tn(a_ref[...].astype(BF16), b_ref[...].astype(BF16))

    return pl.pallas_call(
        body, name=name, grid=(m // tm,), out_shape=jax.ShapeDtypeStruct((m, n), F32),
        in_specs=[pl.BlockSpec((lp, tm), lambda j: (0, j)), _const((lp, n))],
        out_specs=pl.BlockSpec((tm, n), lambda j: (j, 0)),
        compiler_params=_params(("arbitrary",)),
    )(a, b)


def _adamw_step(p_ref, w_ref, m_ref, v_ref, g_ref, d_ref, mo_ref, vo_ref):
    g = p_ref[0].astype(F32)
    for s in range(1, p_ref.shape[0]):
        g = g + p_ref[s].astype(F32)
    m_new = ADAM_B1 * m_ref[...] + (1.0 - ADAM_B1) * g
    v_new = ADAM_B2 * v_ref[...] + (1.0 - ADAM_B2) * (g * g)
    m_hat = m_new / (1.0 - ADAM_B1 ** ADAM_STEP)
    v_hat = v_new / (1.0 - ADAM_B2 ** ADAM_STEP)
    g_ref[...] = g
    d_ref[...] = -ADAM_LR * (m_hat / (jnp.sqrt(v_hat) + ADAM_EPS) + ADAM_WD * w_ref[...])
    mo_ref[...] = m_new
    vo_ref[...] = v_new


def _adamw_small(name, groups, loss_parts):
    n = len(groups)

    def body(*refs):
        ins, outs = refs[:4 * n + 1], refs[4 * n + 1:]
        for j in range(n):
            _adamw_step(*ins[4 * j:4 * j + 4], *outs[4 * j:4 * j + 4])
        total = ins[-1][0]
        for s in range(1, N_DEV):
            total = total + ins[-1][s]
        outs[-1][...] = total

    vmem = pl.BlockSpec(memory_space=pltpu.VMEM)
    out_shape = [jax.ShapeDtypeStruct(w.shape, F32) for _, w, _, _ in groups for _ in range(4)]
    out_shape.append(jax.ShapeDtypeStruct(loss_parts.shape[1:], F32))
    res = pl.pallas_call(
        body, name=name, out_shape=out_shape, in_specs=[vmem] * (4 * n + 1), out_specs=[vmem] * (4 * n + 1),
        compiler_params=_params(),
    )(*[a for g in groups for a in g], loss_parts)
    return [res[4 * j:4 * j + 4] for j in range(n)], res[-1]


def _adamw(name, parts, w, m, v, rows, cols=None):
    r, c_all = w.shape
    c = cols or c_all
    n_parts = parts.shape[0]

    def body(p_ref, w_ref, m_ref, v_ref, g_ref, d_ref, mo_ref, vo_ref):
        _adamw_step(p_ref, w_ref, m_ref, v_ref, g_ref, d_ref, mo_ref, vo_ref)

    blk = pl.BlockSpec((rows, c), lambda i, j: (i, j))
    return pl.pallas_call(
        body, name=name, grid=(r // rows, c_all // c), out_shape=[jax.ShapeDtypeStruct((r, c_all), F32)] * 4,
        in_specs=[pl.BlockSpec((n_parts, rows, c), lambda i, j: (0, i, j)), blk, blk, blk],
        out_specs=[blk] * 4,
        compiler_params=_params(("arbitrary", "arbitrary")),
    )(parts, w, m, v)


def _pair_sum(name, mine, theirs, rows):
    n, r, c = mine.shape

    def body(a_ref, b_ref, o_ref):
        o_ref[...] = (a_ref[...] + b_ref[...].astype(F32)).astype(BF16)

    blk = pl.BlockSpec((1, rows, c), lambda j, i: (j, i, 0))
    return pl.pallas_call(
        body, name=name, grid=(n, r // rows), out_shape=jax.ShapeDtypeStruct((n, r, c), BF16),
        in_specs=[blk, blk], out_specs=blk,
        compiler_params=_params(("arbitrary", "arbitrary")),
    )(mine, theirs)


def _columns_to_slots(a):
    r, c8 = a.shape
    return a.reshape(r, N_DEV, c8 // N_DEV).transpose(1, 0, 2)


def _by_core(slots):
    by_core = slots.reshape((4, 2) + slots.shape[1:]).swapaxes(0, 1)
    c = lax.axis_index("c")
    return (lax.dynamic_index_in_dim(by_core, c, 0, keepdims=False),
            lax.dynamic_index_in_dim(by_core, 1 - c, 0, keepdims=False).astype(BF16))


def _slots_to_columns(a):
    n, r, c = a.shape
    return a.transpose(1, 0, 2).reshape(r, n * c)


def kernel(x, meta_tokens, norm_g, w_in, b_forget, pool_w, pool_scale, w_up_pool, w_up_attn, w_out, final_norm_g, loss_target, m_meta_tokens, m_norm_g, m_w_in, m_b_forget, m_pool_w, m_pool_scale, m_w_up_pool, m_w_up_attn, m_w_out, m_final_norm_g, v_meta_tokens, v_norm_g, v_w_in, v_b_forget, v_pool_w, v_pool_scale, v_w_up_pool, v_w_up_attn, v_w_out, v_final_norm_g):
    xs = x[0]
    target = loss_target[0]

    g_in, g_upp, g_meta = _gather_two_level(
        "gather_weights", [w_in[0].T.astype(BF16), w_up_pool[0].astype(BF16), meta_tokens], (320, 256, 8))
    w_full = g_in.reshape(N_DEV * g_in.shape[1], D_MODEL)
    w_main = jnp.concatenate([w_full[:N_BEFORE_F], w_full[N_BEFORE_F + N_HEADS:]], axis=0)
    w_f = jnp.pad(w_full[N_BEFORE_F:N_BEFORE_F + N_HEADS], ((0, LANES - N_HEADS), (0, 0)))
    wupp = _slots_to_columns(g_upp)
    meta = _slots_to_columns(g_meta)
    tile0 = jnp.concatenate([jnp.zeros((PAD, D_MODEL), F32), meta], axis=0)
    b_f = jnp.pad(b_forget, ((0, 0), (0, LANES - N_HEADS)))
    pw_b = pool_w[0].astype(BF16)
    final_g = final_norm_g.reshape(1, D_MODEL)

    (h, u, zp, k, v, qt, kt, vt, sneg, a_pool) = _forward_in(xs, tile0, norm_g, w_main, w_f, b_f, pw_b,
                                                              pool_scale, wupp)
    o, lse, g_upa, g_out = _attention_forward(
        qt, k, vt, [("gather", w_up_attn[0].astype(BF16), ALL_PEERS), ("gather", w_out[0].astype(BF16), ALL_PEERS)])
    wupa = _slots_to_columns(g_upa)
    wout = g_out.reshape(D_MODEL, D_MODEL)
    (dh2, mg, yp, ya, dap, daa, do, dza, dgp, dga, dzp, dpn,
     loss_part, d_final_g, d_scale, d_pool_w) = _middle(xs, target, h, o, a_pool, u, zp, w_main, wupp, wupa, wout,
                                                        pw_b, pool_scale, final_g)
    dw_out = _matmul_tn("grad_w_out", mg, dh2, 256)
    dw_upp = _matmul_tn("grad_w_up_pool", yp, dap, 512)
    dw_upa = _matmul_tn("grad_w_up_attn", ya, daa, 512)
    dq, dk, dv, dc, p_upp, p_upa, p_out, p_pool_w, p_scale, p_final_g = _attention_backward(
        qt, k, kt, v, do, o, lse,
        [("scatter", _columns_to_slots(dw_upp).astype(BF16), ALL_PEERS),
         ("scatter", _columns_to_slots(dw_upa).astype(BF16), ALL_PEERS),
         ("scatter", dw_out.reshape(N_DEV, D_MODEL // N_DEV, D_MODEL).astype(BF16), ALL_PEERS),
         ("gather", d_pool_w.reshape(4 * POOL_GROUP, POOL_GROUP), ALL_PEERS),
         ("gather", d_scale, ALL_PEERS), ("gather", d_final_g, ALL_PEERS)])
    dproj, df, grad_x, d_meta, d_norm_g, d_bf = _backward_in(xs, tile0, norm_g, dh2, dpn, dzp, dq, dk, dv, dza,
                                                             dgp, dga, dc, sneg, w_main, w_f)
    dw_main = _matmul_tn_rows("grad_w_in", dproj, h, 512)
    dw_f = _matmul_tn_rows("grad_w_forget", df, h, LANES)
    dw_in = jnp.concatenate([dw_main[:N_BEFORE_F], dw_f[:N_HEADS], dw_main[N_BEFORE_F:]], axis=0)
    dw_in = dw_in.reshape(N_DEV, dw_in.shape[0] // N_DEV, D_MODEL)

    mine, for_sibling = _by_core(dw_in)
    from_sibling, = _exchange("swap_with_sibling", [("swap", for_sibling, (SIBLING,))])
    pair_sums = _pair_sum("pair_sum", mine, from_sibling, dw_in.shape[1])
    p_in, p_meta, p_norm_g, p_bf, p_loss = _exchange(
        "exchange_gradients",
        [("chips", pair_sums, SAME_CORE), ("scatter", _columns_to_slots(d_meta), ALL_PEERS),
         ("gather", d_norm_g, ALL_PEERS), ("gather", d_bf, ALL_PEERS), ("gather", loss_part, ALL_PEERS)])


    def pad_f(a):
        return jnp.pad(a, ((0, 0), (0, LANES - N_HEADS)))

    res = {}
    res["w_in"] = [a.T for a in _adamw("adamw_w_in", p_in, w_in[0].T, m_w_in[0].T, v_w_in[0].T, p_in.shape[1], 256)]
    res["w_up_pool"] = _adamw("adamw_w_up_pool", p_upp, w_up_pool[0], m_w_up_pool[0], v_w_up_pool[0], 512)
    res["w_up_attn"] = _adamw("adamw_w_up_attn", p_upa, w_up_attn[0], m_w_up_attn[0], v_w_up_attn[0], 512)
    res["w_out"] = _adamw("adamw_w_out", p_out, w_out[0], m_w_out[0], v_w_out[0], 128)
    flat = lambda a: a.reshape(4 * POOL_GROUP, POOL_GROUP)
    row = lambda a: a.reshape(1, D_MODEL)
    small, loss_row = _adamw_small(
        "adamw_small",
        [(p_meta, meta_tokens, m_meta_tokens, v_meta_tokens),
         (p_norm_g, norm_g, m_norm_g, v_norm_g),
         (p_bf, pad_f(b_forget), pad_f(m_b_forget), pad_f(v_b_forget)),
         (p_pool_w, flat(pool_w), flat(m_pool_w), flat(v_pool_w)),
         (p_scale, pool_scale, m_pool_scale, v_pool_scale),
         (p_final_g, final_g, row(m_final_norm_g), row(v_final_norm_g))],
        p_loss)
    res["meta_tokens"], res["norm_g"], bf, pw, res["pool_scale"], fg = small
    res["b_forget"] = [a[:, :N_HEADS] for a in bf]
    res["pool_w"] = [a.reshape(pool_w.shape) for a in pw]
    res["final_norm_g"] = [a.reshape(D_MODEL) for a in fg]
    loss = loss_row[0, 0]
    for name in ("w_in", "w_up_pool", "w_up_attn", "w_out"):
        res[name] = [a[None] for a in res[name]]

    order = ["meta_tokens", "norm_g", "w_in", "b_forget", "pool_w", "pool_scale", "w_up_pool", "w_up_attn", "w_out",
             "final_norm_g"]
    outs = [loss, grad_x[None]]
    for part in range(4):
        outs += [res[name][part] for name in order]
    return tuple(outs)
```

```python
import functools

import jax
import jax.numpy as jnp
from jax import lax
from jax.experimental import pallas as pl
from jax.experimental.pallas import tpu as pltpu

F32 = jnp.float32
BF16 = jnp.bfloat16

D_MODEL = 1024
N_META = 16
POOL_WIDTH = 512
ATTN_WIDTH = 512
N_HEADS = 8
HEAD_DIM = 64
POOL_WINDOWS = (2, 4, 8, 16)
POOL_GROUP = 128
MAX_WINDOW = 16
RMS_EPS = 1e-6
N_MAIN = 5120
N_BEFORE_F = 3072
N_DEV = 8
LANES = 128

ROW_TILE = 256
ATT_TILE = 256
ATT_Q_BLOCKS = 4
PAD = ROW_TILE - N_META
VMEM_LIMIT = 56 * 1024 * 1024

ADAM_LR = 0.001
ADAM_B1 = 0.9
ADAM_B2 = 0.999
ADAM_EPS = 1e-08
ADAM_WD = 0.01
ADAM_STEP = 10

MID_ZA, MID_GP, MID_GA, MID_WIDTH = 512, 1024, 2048, 3072
NEG = -1e30
MESH = pl.DeviceIdType.MESH


def _params(sem=None):
    kw = dict(vmem_limit_bytes=VMEM_LIMIT)
    if sem is not None:
        kw["dimension_semantics"] = sem
    return pltpu.CompilerParams(**kw)


def _const(shape, block_index=None):
    idx = block_index or (0,) * len(shape)
    return pl.BlockSpec(shape, lambda i: idx, pipeline_mode=pl.Buffered(1))


def _sigmoid(x):
    return jax.nn.sigmoid(x)


def _dot(a, b):
    return jnp.dot(a, b, preferred_element_type=F32)


def _dot_nt(a, b):
    return lax.dot_general(a, b, (((1,), (1,)), ((), ())), preferred_element_type=F32)


def _dot_tn(a, b):
    return lax.dot_general(a, b, (((0,), (0,)), ((), ())), preferred_element_type=F32)


def _pool_counts(first_row, rows):
    row = first_row + lax.broadcasted_iota(jnp.int32, (rows, 1), 0)
    pos1 = row - PAD + 1
    return [jnp.clip(pos1, 1, w).astype(F32) for w in POOL_WINDOWS]


def _pool_means(u_ext, u, counts):
    rows = u.shape[0]
    out = []
    for g, w in enumerate(POOL_WINDOWS):
        s = u_ext[:, POOL_GROUP * g:POOL_GROUP * (g + 1)]
        sh = 1
        while sh < w:
            s = s + pltpu.roll(s, sh, axis=0)
            sh *= 2
        out.append(s[MAX_WINDOW:MAX_WINDOW + rows, :] / counts[g] - u[:, POOL_GROUP * g:POOL_GROUP * (g + 1)])
    return out


Q_BIAS, Q_ONES, Q_LSE = 64, 67, 70
K_ONES, K_BIAS, K_ONES2 = 64, 67, 70
V_ONES = 64
DO_BIAS = 64


def _lane_ones(lane, ranges):
    hit = None
    for lo, hi in ranges:
        r = (lane >= lo) & (lane < hi)
        hit = r if hit is None else hit | r
    return jnp.where(hit, 1.0, 0.0)


def _put3(base, lane, first, x):
    hi = x.astype(BF16).astype(F32)
    rest = x - hi
    mid = rest.astype(BF16).astype(F32)
    lo = (rest - mid).astype(BF16).astype(F32)
    for j, piece in enumerate((hi, mid, lo)):
        base = jnp.where(lane == first + j, piece, base)
    return base


SIBLING = 1
ALL_PEERS = (1, 2, 3, 4, 5, 6, 7)


def _place():
    return lax.axis_index("x"), lax.axis_index("y"), lax.axis_index("c")


def _peer(r):
    x, y, c = _place()
    return (1 - x if r & 4 else x, 1 - y if r & 2 else y, 1 - c if r & 1 else c)


def _device_slot(p):
    return 4 * p[0] + 2 * p[1] + p[2]


def _exchange(name, items):
    n = len(items)

    def body(*refs):
        copies = _exchange_copies(items, refs[:n], refs[n:2 * n], *refs[2 * n:])
        for cp in copies:
            cp.start()
        for cp in copies:
            cp.wait()

    hbm = pl.BlockSpec(memory_space=pl.ANY)
    return pl.pallas_call(
        body, name=name, out_shape=_exchange_results(items),
        in_specs=[hbm] * n, out_specs=[hbm] * n,
        scratch_shapes=_exchange_semaphores(n),
    )(*[a for _, a, _ in items])


def _exchange_results(items):
    return [jax.ShapeDtypeStruct(((N_DEV,) if kind == "gather" else ()) + a.shape, a.dtype) for kind, a, _ in items]


def _exchange_semaphores(n):
    return [pltpu.SemaphoreType.DMA((n, N_DEV - 1)), pltpu.SemaphoreType.DMA((n, N_DEV - 1)),
            pltpu.SemaphoreType.DMA((n,))]


def _exchange_copies(items, ins, outs, send_sems, recv_sems, local_sems):
    me = _place()
    copies = []
    for a, (kind, _, peers) in enumerate(items):
        mine = outs[a].at[_device_slot(me)]
        for r in peers:
            peer = _peer(r)
            src = ins[a] if kind == "gather" else ins[a].at[_device_slot(peer)]
            copies.append(pltpu.make_async_remote_copy(
                src_ref=src, dst_ref=mine, send_sem=send_sems.at[a, r - 1], recv_sem=recv_sems.at[a, r - 1],
                device_id=peer, device_id_type=MESH))
        src = ins[a] if kind == "gather" else ins[a].at[_device_slot(me)]
        copies.append(pltpu.make_async_copy(src, mine, local_sems.at[a]))
    return copies


def _gather_two_level(name, arrays, halves):
    n = len(arrays)
    x_flip, y_flip, both = 4, 2, 6

    def body(*refs):
        ins, outs = refs[:n], refs[n:2 * n]
        send_sems, recv_sems, local_sems = refs[2 * n:]
        me, sibling = _place(), _peer(SIBLING)
        xn, yn, dg = _peer(x_flip), _peer(y_flip), _peer(both)

        def part(a, block, half):
            rows = outs[a].at[_device_slot(block)]
            if half is None:
                return rows
            return rows.at[pl.ds(0, halves[a])] if half == 0 else rows.at[pl.ds(halves[a], arrays[a].shape[0] - halves[a])]

        def copy(a, k, block, half, to, src=None):
            dst = part(a, block, half)
            return pltpu.make_async_remote_copy(
                src_ref=dst if src is None else src, dst_ref=dst,
                send_sem=send_sems.at[a, k], recv_sem=recv_sems.at[a, k], device_id=to, device_id_type=MESH)

        sends, own = [], []

        def start(cp):
            cp.start()
            sends.append(cp)

        for a in range(n):
            mine = pltpu.make_async_copy(ins[a], outs[a].at[_device_slot(me)], local_sems.at[a])
            mine.start()
            own.append(mine)
            for k, to in enumerate((sibling, xn, yn)):
                start(copy(a, k, me, None, to, src=ins[a]))
        for a in range(n):
            copy(a, 1, xn, None, me).wait_recv()
            start(copy(a, 3, xn, 0, yn))
            start(copy(a, 5, xn, None, sibling))
        for a in range(n):
            copy(a, 2, yn, None, me).wait_recv()
            start(copy(a, 4, yn, 1, xn))
            start(copy(a, 6, yn, None, sibling))
        for a in range(n):
            copy(a, 3, dg, 0, me).wait_recv()
            copy(a, 4, dg, 1, me).wait_recv()
            start(copy(a, 7, dg, None, sibling))
        for a in range(n):
            copy(a, 0, sibling, None, me).wait_recv()
            for k, r in ((5, x_flip), (6, y_flip), (7, both)):
                copy(a, k, _peer(r | SIBLING), None, me).wait_recv()
        for cp in sends:
            cp.wait_send()
        for cp in own:
            cp.wait()

    hbm = pl.BlockSpec(memory_space=pl.ANY)
    return pl.pallas_call(
        body, name=name, out_shape=[jax.ShapeDtypeStruct((N_DEV,) + a.shape, a.dtype) for a in arrays],
        in_specs=[hbm] * n, out_specs=[hbm] * n,
        scratch_shapes=[pltpu.SemaphoreType.DMA((n, 8)), pltpu.SemaphoreType.DMA((n, 8)),
                        pltpu.SemaphoreType.DMA((n,))],
    )(*arrays)


def _forward_in(x, tile0, norm_g, w_main, w_f, b_f, pool_w, pool_scale, w_up_pool):
    seq = x.shape[0]
    nt = seq // ROW_TILE + 1
    lp = nt * ROW_TILE
    tm = ROW_TILE

    def body(x_ref, t0_ref, g_ref, wa_ref, wf_ref, bf_ref, pw_ref, sc_ref, wup_ref,
             h_ref, u_ref, zp_ref, k_ref, v_ref, qt_ref, kt_ref, vt_ref, sn_ref, ap_ref,
             uext_ref, carry_ref):
        i = pl.program_id(0)

        @pl.when(i == 0)
        def _():
            uext_ref[...] = jnp.zeros_like(uext_ref)
            carry_ref[...] = jnp.zeros_like(carry_ref)

        xt = jnp.where(i == 0, t0_ref[...], x_ref[...])
        r = lax.rsqrt(jnp.mean(xt * xt, axis=-1, keepdims=True) + RMS_EPS)
        h = (xt * r * g_ref[...]).astype(BF16)
        h_ref[...] = h
        pa = _dot_nt(h, wa_ref[...])
        u = pa[:, :512]
        zp = pa[:, 512:1024]
        u_ref[...] = u
        zp_ref[...] = zp

        uext_ref[0:MAX_WINDOW, :] = uext_ref[tm:tm + MAX_WINDOW, :]
        uext_ref[MAX_WINDOW:, :] = u
        counts = _pool_counts(i * tm, tm)
        ps = _pool_means(uext_ref[...], u, counts)
        ppw = jnp.concatenate([_dot(ps[g].astype(BF16), pw_ref[g]) for g in range(4)], axis=1)
        y_pool = ppw * sc_ref[...] * (zp * _sigmoid(zp))
        ap_ref[...] = _dot(y_pool.astype(BF16), wup_ref[...]).astype(BF16)

        fl = _dot_nt(h, wf_ref[...]) + bf_ref[...]
        row = i * tm + lax.broadcasted_iota(jnp.int32, (tm, LANES), 0)
        rloc = lax.broadcasted_iota(jnp.int32, (tm, LANES), 0)
        lane = lax.broadcasted_iota(jnp.int32, (tm, LANES), 1)
        live = (row >= PAD) & (lane < N_HEADS)
        logf = jnp.minimum(fl, 0.0) - jnp.log1p(jnp.exp(-jnp.abs(fl)))
        cs = jnp.where(live, logf, 0.0)
        sh = 1
        while sh < tm:
            cs = cs + jnp.where(rloc >= sh, pltpu.roll(cs, sh, axis=0), 0.0)
            sh *= 2
        cs = cs + carry_ref[...]
        carry_ref[...] = cs[tm - 1:tm, :]
        sn_ref[...] = jnp.where(live, _sigmoid(-fl), 0.0)

        rows1 = i * tm + lax.broadcasted_iota(jnp.int32, (tm, 1), 0)
        ones_q = _lane_ones(lane, ((Q_ONES, Q_ONES + 3),))
        ones_k = _lane_ones(lane, ((K_ONES, K_ONES + 3), (K_ONES2, K_ONES2 + 3)))
        ones_v = _lane_ones(lane, ((V_ONES, V_ONES + 3),))
        for hp in range(N_HEADS // 2):
            qp = pa[:, 1024 + LANES * hp:1024 + LANES * (hp + 1)] * 0.125
            kp = pa[:, 1536 + LANES * hp:1536 + LANES * (hp + 1)]
            vp = pa[:, 2048 + LANES * hp:2048 + LANES * (hp + 1)]
            for e in range(2):
                head = 2 * hp + e
                if e:
                    qp, kp, vp = (pltpu.roll(a, HEAD_DIM, axis=1) for a in (qp, kp, vp))
                c_h = cs[:, head:head + 1]
                q_h = jnp.where(lane < HEAD_DIM, qp, _put3(ones_q, lane, Q_BIAS, c_h))
                qt_ref[head] = q_h.T.astype(BF16)
                minus_ck = jnp.where(rows1 >= PAD, -c_h, NEG)
                k_h = jnp.where(lane < HEAD_DIM, kp, _put3(ones_k, lane, K_BIAS, minus_ck))
                k_ref[head] = k_h.astype(BF16)
                kt_ref[head] = k_h.T.astype(BF16)
                v_h = jnp.where(lane < HEAD_DIM, vp, ones_v)
                v_ref[head] = v_h.astype(BF16)
                vt_ref[head] = v_h.T.astype(BF16)

    row_f32 = lambda w: pl.BlockSpec((tm, w), lambda i: (i, 0))
    out_shape = [
        jax.ShapeDtypeStruct((lp, D_MODEL), BF16),
        jax.ShapeDtypeStruct((lp, POOL_WIDTH), F32),
        jax.ShapeDtypeStruct((lp, POOL_WIDTH), F32),
        jax.ShapeDtypeStruct((N_HEADS, lp, LANES), BF16),
        jax.ShapeDtypeStruct((N_HEADS, lp, LANES), BF16),
        jax.ShapeDtypeStruct((N_HEADS, LANES, lp), BF16),
        jax.ShapeDtypeStruct((N_HEADS, LANES, lp), BF16),
        jax.ShapeDtypeStruct((N_HEADS, LANES, lp), BF16),
        jax.ShapeDtypeStruct((lp, LANES), F32),
        jax.ShapeDtypeStruct((lp, D_MODEL), BF16),
    ]
    heads = pl.BlockSpec((N_HEADS, tm, LANES), lambda i: (0, i, 0))
    heads_t = pl.BlockSpec((N_HEADS, LANES, tm), lambda i: (0, 0, i))
    out_specs = [row_f32(D_MODEL), row_f32(512), row_f32(512), heads, heads, heads_t, heads_t, heads_t,
                 row_f32(LANES), row_f32(D_MODEL)]
    in_specs = [
        pl.BlockSpec((tm, D_MODEL), lambda i: (jnp.maximum(i - 1, 0), 0)),
        _const((tm, D_MODEL)), _const((1, D_MODEL)),
        _const((2560, D_MODEL)), _const((LANES, D_MODEL)), _const((1, LANES)),
        _const((4, POOL_GROUP, POOL_GROUP)), _const((1, POOL_WIDTH)), _const((POOL_WIDTH, D_MODEL)),
    ]
    return pl.pallas_call(
        body, name="forward_in", grid=(nt,), out_shape=out_shape, in_specs=in_specs, out_specs=out_specs,
        scratch_shapes=[pltpu.VMEM((tm + MAX_WINDOW, POOL_WIDTH), F32), pltpu.VMEM((1, LANES), F32)],
        compiler_params=_params(("arbitrary",)),
    )(x, tile0, norm_g, w_main, w_f, b_f, pool_w, pool_scale, w_up_pool)


def _causal(tb):
    return lax.broadcasted_iota(jnp.int32, (tb, tb), 1) <= lax.broadcasted_iota(jnp.int32, (tb, tb), 0)


def _pair_lanes(a0, a1):
    lane = lax.broadcasted_iota(jnp.int32, a0.shape, 1)
    return jnp.where(lane < HEAD_DIM, a0, pltpu.roll(a1, HEAD_DIM, axis=1))


def _behind(items, ins, outs, sems):
    step, last = pl.program_id(0), pl.num_programs(0) - 1

    @pl.when(step == 0)
    def _():
        for cp in _exchange_copies(items, ins, outs, *sems):
            cp.start()

    def finish():
        @pl.when(step == last)
        def _():
            for cp in _exchange_copies(items, ins, outs, *sems):
                cp.wait()

    return finish


def _attention_forward(qt, k, vt, behind):
    lp = k.shape[1]
    tk = ATT_TILE
    tq_big = ATT_Q_BLOCKS * tk
    n_big = (lp // tk - 1) // ATT_Q_BLOCKS
    assert lp == tk + n_big * tq_big and ATT_Q_BLOCKS % 2 == 0
    nx = len(behind)

    def body(qt_ref, k_ref, vt_ref, *rest):
        o_ref, lse_ref = rest[nx:nx + 2]
        s_buf, m_scr, acc_scr = rest[2 * nx + 2:2 * nx + 5]
        finish_exchange = _behind(behind, rest[:nx], rest[nx + 2:2 * nx + 2], rest[2 * nx + 5:])

        def q_tile(q0, tq, pairs):
            first = q0 // tk
            qts = [qt_ref[e, :, pl.ds(q0, tq)] for e in range(2)]

            def block(kj):
                return pl.ds(kj * tk if isinstance(kj, int) else pl.multiple_of(kj * tk, tk), tk)

            def step(kj, rd, wr, c0=0, diagonal=False):
                c1 = c0 + tk if diagonal else c0
                for e in range(2):
                    s = s_buf[rd, e, :, c0:tq]
                    if wr is not None:
                        s_buf[wr, e, :, c1:tq] = _dot(k_ref[e, block(kj + 1), :], qts[e][:, c1:tq])
                    if diagonal:
                        keys = lax.broadcasted_iota(jnp.int32, s.shape, 0)
                        s = jnp.where(keys <= lax.broadcasted_iota(jnp.int32, s.shape, 1), s, NEG)
                    m = m_scr[e, :, c0:tq]
                    m_new = jnp.maximum(m, jnp.max(s, axis=0, keepdims=True))
                    p = jnp.exp(s - m_new)
                    pv = _dot(vt_ref[e, :, block(kj)], p.astype(BF16))
                    acc_scr[e, :, c0:tq] = jnp.exp(m - m_new) * acc_scr[e, :, c0:tq] + pv
                    m_scr[e, :, c0:tq] = m_new

            for e in range(2):
                m_scr[e, :, 0:tq] = jnp.full((1, tq), NEG, F32)
                acc_scr[e, :, 0:tq] = jnp.zeros((LANES, tq), F32)
                s_buf[0, e, :, 0:tq] = _dot(k_ref[e, block(0), :], qts[e])
            if pairs is None:
                step(0, 0, None, 0, True)
            else:
                step(0, 0, 1)

                def two_steps(t, _):
                    step(1 + 2 * t, 1, 0)
                    step(2 + 2 * t, 0, 1)
                    return 0

                lax.fori_loop(0, pairs, two_steps, 0)
                for b in range(tq // tk):
                    step(first + b, (b + 1) % 2, b % 2 if (b + 1) * tk < tq else None, b * tk, True)
            outs, lses = [], []
            for e in range(2):
                acc = acc_scr[e, :, 0:tq]
                l = acc[V_ONES:V_ONES + 1, :]
                outs.append((acc / l).T)
                lses.append(m_scr[e, :, 0:tq] + jnp.log(l))
            o_ref[pl.ds(q0, tq), :] = _pair_lanes(outs[0], outs[1]).astype(BF16)
            lse_rows = jnp.concatenate(lses + [jnp.zeros((LANES - 2, tq), F32)], axis=0)
            lse_ref[pl.ds(q0, tq), :] = lse_rows.T

        q_tile(0, tk, None)

        def big_tile(i, _):
            q_tile(pl.multiple_of(tk + i * tq_big, tk), tq_big, (ATT_Q_BLOCKS // 2) * i)
            return 0

        lax.fori_loop(0, n_big, big_tile, 0)
        finish_exchange()

    pair = pl.BlockSpec((lp, LANES), lambda hp: (0, hp))
    heads = pl.BlockSpec((2, lp, LANES), lambda hp: (hp, 0, 0), pipeline_mode=pl.Buffered(1))
    heads_t = pl.BlockSpec((2, LANES, lp), lambda hp: (hp, 0, 0), pipeline_mode=pl.Buffered(1))
    hbm = pl.BlockSpec(memory_space=pl.ANY)
    return pl.pallas_call(
        body, name="attention_forward", grid=(N_HEADS // 2,),
        out_shape=[jax.ShapeDtypeStruct((lp, ATTN_WIDTH), BF16), jax.ShapeDtypeStruct((lp, ATTN_WIDTH), F32)]
        + _exchange_results(behind),
        in_specs=[heads_t, heads, heads_t] + [hbm] * nx,
        out_specs=[pair, pair] + [hbm] * nx,
        scratch_shapes=[pltpu.VMEM((2, 2, tk, tq_big), F32), pltpu.VMEM((2, 1, tq_big), F32),
                        pltpu.VMEM((2, LANES, tq_big), F32)] + _exchange_semaphores(nx),
        compiler_params=_params(("arbitrary",)),
    )(qt, k, vt, *[a for _, a, _ in behind])


def _rows3(first, x):
    sub = lax.broadcasted_iota(jnp.int32, (LANES, x.shape[1]), 0)
    hi = x.astype(BF16).astype(F32)
    rest = x - hi
    mid = rest.astype(BF16).astype(F32)
    lo = (rest - mid).astype(BF16).astype(F32)
    out = jnp.zeros((LANES, x.shape[1]), F32)
    for j, piece in enumerate((hi, mid, lo)):
        out = jnp.where(sub == first + j, piece, out)
    return out


def _attention_backward(qt, k, kt, v, do, o, lse, behind):
    lp = k.shape[1]
    tb = ATT_TILE
    nb = lp // tb
    tq_big = ATT_Q_BLOCKS * tb
    n_big = (nb - 1) // ATT_Q_BLOCKS
    assert lp == tb + n_big * tq_big and ATT_Q_BLOCKS % 2 == 0
    nx = len(behind)

    def body(qt_ref, k_ref, kt_ref, v_ref, do_ref, o_ref, lse_ref, *rest):
        dqkv_ref, dc_ref = rest[nx:nx + 2]
        q2_ref, do2_ref, dk_acc, dv_acc, dq_scr, s_buf = rest[2 * nx + 2:2 * nx + 8]
        finish_exchange = _behind(behind, rest[:nx], rest[nx + 2:2 * nx + 2], rest[2 * nx + 8:])
        sub = lax.broadcasted_iota(jnp.int32, (LANES, tb), 0)

        def lanes01(row0, row1):
            n = row0.shape[1]
            return jnp.concatenate([row0, row1, jnp.zeros((LANES - 2, n), F32)], axis=0).T

        def prepare(bi, _):
            r0 = pl.multiple_of(bi * tb, tb)
            queries = r0 + lax.broadcasted_iota(jnp.int32, (1, tb), 1)
            dob = do_ref[pl.ds(r0, tb), :].astype(F32)
            do_t = dob.T
            dd_t = (dob * o_ref[pl.ds(r0, tb), :].astype(F32)).T
            lse_t = lse_ref[pl.ds(r0, tb), :].T
            for e in range(2):
                delta = jnp.sum(dd_t[HEAD_DIM * e:HEAD_DIM * (e + 1), :], axis=0, keepdims=True)
                do_e = jnp.concatenate([do_t[HEAD_DIM * e:HEAD_DIM * (e + 1), :], jnp.zeros((HEAD_DIM, tb), F32)], axis=0)
                do2_ref[e, :, pl.ds(r0, tb)] = jnp.where(sub < HEAD_DIM, do_e, _rows3(DO_BIAS, -delta)).astype(BF16)
                minus_lse = jnp.where(queries >= PAD, -lse_t[e:e + 1, :], NEG)
                keep = (sub < Q_LSE) | (sub >= Q_LSE + 3)
                q2_ref[e, :, pl.ds(r0, tb)] = jnp.where(keep, qt_ref[e, :, pl.ds(r0, tb)].astype(F32),
                                                        _rows3(Q_LSE, minus_lse)).astype(BF16)
            return 0

        lax.fori_loop(0, nb, prepare, 0)
        dk_acc[...] = jnp.zeros_like(dk_acc)
        dv_acc[...] = jnp.zeros_like(dv_acc)

        def q_tile(q0, tq, pairs):
            first = q0 // tb
            qts = [q2_ref[e, :, pl.ds(q0, tq)] for e in range(2)]
            dots = [do2_ref[e, :, pl.ds(q0, tq)] for e in range(2)]

            def block(kj):
                return pl.ds(kj * tb if isinstance(kj, int) else pl.multiple_of(kj * tb, tb), tb)

            def step(kj, rd, wr, c0=0, diagonal=False):
                c1 = c0 + tb if diagonal else c0
                for e in range(2):
                    s = s_buf[rd, e, :, c0:tq]
                    if wr is not None:
                        s_buf[wr, e, :, c1:tq] = _dot(k_ref[e, block(kj + 1), :], qts[e][:, c1:tq])
                    dpd = _dot(v_ref[e, block(kj), :], dots[e][:, c0:tq])
                    p = jnp.exp(s)
                    if diagonal:
                        keys = lax.broadcasted_iota(jnp.int32, s.shape, 0)
                        p = jnp.where(keys <= lax.broadcasted_iota(jnp.int32, s.shape, 1), p, 0.0)
                    dsb = (p * dpd).astype(BF16)
                    dv_acc[e, :, block(kj)] += _dot_nt(dots[e][:, c0:tq], p.astype(BF16))
                    dk_acc[e, :, block(kj)] += _dot_nt(qts[e][:, c0:tq], dsb)
                    dq_scr[e, :, c0:tq] += _dot(kt_ref[e, :, block(kj)], dsb)

            for e in range(2):
                dq_scr[e, :, 0:tq] = jnp.zeros((LANES, tq), F32)
                s_buf[0, e, :, 0:tq] = _dot(k_ref[e, block(0), :], qts[e])
            if pairs is None:
                step(0, 0, None, 0, True)
            else:
                step(0, 0, 1)

                def two_steps(t, _):
                    step(1 + 2 * t, 1, 0)
                    step(2 + 2 * t, 0, 1)
                    return 0

                lax.fori_loop(0, pairs, two_steps, 0)
                for b in range(tq // tb):
                    step(first + b, (b + 1) % 2, b % 2 if (b + 1) * tb < tq else None, b * tb, True)
            dq0, dq1 = dq_scr[0, :, 0:tq], dq_scr[1, :, 0:tq]
            dqkv_ref[0, pl.ds(q0, tq), :] = (_pair_lanes(dq0.T, dq1.T) * 0.125).astype(BF16)
            dc_ref[pl.ds(q0, tq), :] = lanes01(dq0[K_ONES:K_ONES + 1, :], dq1[K_ONES:K_ONES + 1, :])

        q_tile(0, tb, None)

        def big_tile(i, _):
            q_tile(pl.multiple_of(tb + i * tq_big, tb), tq_big, (ATT_Q_BLOCKS // 2) * i)
            return 0

        lax.fori_loop(0, n_big, big_tile, 0)

        def finish(bi, _):
            r0 = pl.multiple_of(bi * tb, tb)
            dk0, dk1 = dk_acc[0, :, pl.ds(r0, tb)], dk_acc[1, :, pl.ds(r0, tb)]
            dqkv_ref[1, pl.ds(r0, tb), :] = _pair_lanes(dk0.T, dk1.T).astype(BF16)
            dqkv_ref[2, pl.ds(r0, tb), :] = _pair_lanes(dv_acc[0, :, pl.ds(r0, tb)].T,
                                                        dv_acc[1, :, pl.ds(r0, tb)].T).astype(BF16)
            dc_ref[pl.ds(r0, tb), :] = dc_ref[pl.ds(r0, tb), :] - lanes01(dk0[Q_ONES:Q_ONES + 1, :], dk1[Q_ONES:Q_ONES + 1, :])
            return 0

        lax.fori_loop(0, nb, finish, 0)
        finish_exchange()

    once = pl.Buffered(1)
    pair = pl.BlockSpec((lp, LANES), lambda hp: (0, hp))
    pair_in = pl.BlockSpec((lp, LANES), lambda hp: (0, hp), pipeline_mode=once)
    heads = pl.BlockSpec((2, lp, LANES), lambda hp: (hp, 0, 0), pipeline_mode=once)
    heads_t = pl.BlockSpec((2, LANES, lp), lambda hp: (hp, 0, 0), pipeline_mode=once)
    hbm = pl.BlockSpec(memory_space=pl.ANY)
    return pl.pallas_call(
        body, name="attention_backward", grid=(N_HEADS // 2,),
        out_shape=[jax.ShapeDtypeStruct((3, lp, ATTN_WIDTH), BF16), jax.ShapeDtypeStruct((lp, ATTN_WIDTH), F32)]
        + _exchange_results(behind),
        in_specs=[heads_t, heads, heads_t, heads, pair_in, pair_in, pair_in] + [hbm] * nx,
        out_specs=[pl.BlockSpec((3, lp, LANES), lambda hp: (0, 0, hp)), pair] + [hbm] * nx,
        scratch_shapes=[pltpu.VMEM((2, LANES, lp), BF16), pltpu.VMEM((2, LANES, lp), BF16),
                        pltpu.VMEM((2, LANES, lp), F32), pltpu.VMEM((2, LANES, lp), F32),
                        pltpu.VMEM((2, LANES, tq_big), F32), pltpu.VMEM((2, 2, tb, tq_big), F32)]
        + _exchange_semaphores(nx),
        compiler_params=_params(("arbitrary",)),
    )(qt, k, kt, v, do, o, lse, *[a for _, a, _ in behind])


def _middle(x, target, h, o, a_pool, u, zp, w_main, w_up_pool, w_up_attn, w_out, pool_w, pool_scale, final_g):
    seq = x.shape[0]
    tm = ROW_TILE
    nt = seq // tm + 1
    lp = nt * tm
    halo_blocks = tm // MAX_WINDOW

    def body(x_ref, t_ref, h_ref, o_ref, ap_ref, u_ref, uh_ref, zp_ref,
             wc_ref, wupp_ref, wupa_ref, wout_ref, pw_ref, sc_ref, gf_ref,
             dh2_ref, mg_ref, yp_ref, ya_ref, dap_ref, daa_ref, do_ref, dmid_ref, dpn_ref,
             loss_ref, dgf_ref, dsc_ref, dpw_ref):
        i = pl.program_id(0)
        tiles = (dh2_ref, mg_ref, yp_ref, ya_ref, dap_ref, daa_ref, do_ref, dmid_ref, dpn_ref)

        @pl.when(i == 0)
        def _():
            for ref in tiles + (loss_ref, dgf_ref, dsc_ref, dpw_ref):
                ref[...] = jnp.zeros_like(ref)

        @pl.when(i > 0)
        def _():
            xt = x_ref[...]
            hb = h_ref[...]
            pc = _dot_nt(hb, wc_ref[...])
            za, gp, ga = pc[:, :512], pc[:, 512:1536], pc[:, 1536:]
            of = o_ref[...].astype(F32)
            sza = _sigmoid(za)
            silu_za = za * sza
            ya = (of * silu_za).astype(BF16)
            ya_ref[...] = ya
            aa = _dot(ya, wupa_ref[...])
            ap = ap_ref[...].astype(F32)
            sgp, sga = _sigmoid(gp), _sigmoid(ga)
            mg = (sgp * ap + sga * aa).astype(BF16)
            mg_ref[...] = mg
            h2 = xt + _dot(mg, wout_ref[...])
            r2 = lax.rsqrt(jnp.mean(h2 * h2, axis=-1, keepdims=True) + RMS_EPS)
            h2n = h2 * r2
            gf = gf_ref[...]
            diff = h2n * gf - t_ref[...]
            loss_ref[...] += 0.5 * jnp.sum(jnp.mean(diff * diff, axis=-1, keepdims=True), axis=0, keepdims=True)
            dy = diff * (1.0 / D_MODEL)
            dgf_ref[...] += jnp.sum(dy * h2n, axis=0, keepdims=True)
            dyg = dy * gf
            dh2 = r2 * (dyg - h2n * jnp.mean(dyg * h2n, axis=-1, keepdims=True))
            dh2_ref[...] = dh2
            dmg = _dot_nt(dh2.astype(BF16), wout_ref[...])
            dap = (dmg * sgp).astype(BF16)
            daa = (dmg * sga).astype(BF16)
            dap_ref[...] = dap
            daa_ref[...] = daa
            dmid_ref[:, MID_GP:MID_GA] = (dmg * ap * sgp * (1.0 - sgp)).astype(BF16)
            dmid_ref[:, MID_GA:] = (dmg * aa * sga * (1.0 - sga)).astype(BF16)
            dyp = _dot_nt(dap, wupp_ref[...])
            dya = _dot_nt(daa, wupa_ref[...])
            do_ref[...] = (dya * silu_za).astype(BF16)
            dmid_ref[:, MID_ZA:MID_GP] = (dya * of * (sza * (1.0 + za * (1.0 - sza)))).astype(BF16)

            u = u_ref[...]
            zp = zp_ref[...]
            counts = _pool_counts(i * tm, tm)
            ps = _pool_means(jnp.concatenate([uh_ref[...], u], axis=0), u, counts)
            pbs = [p.astype(BF16) for p in ps]
            ppw = jnp.concatenate([_dot(pbs[g], pw_ref[g]) for g in range(4)], axis=1)
            sc = sc_ref[...]
            szp = _sigmoid(zp)
            silu_zp = zp * szp
            ypre = ppw * sc
            yp_ref[...] = (ypre * silu_zp).astype(BF16)
            dypre = dyp * silu_zp
            dmid_ref[:, :MID_ZA] = (dyp * ypre * (szp * (1.0 + zp * (1.0 - szp)))).astype(BF16)
            dsc_ref[...] += jnp.sum(dypre * ppw, axis=0, keepdims=True)
            dppw = (dypre * sc).astype(BF16)
            dpns = []
            for g in range(4):
                dg = dppw[:, POOL_GROUP * g:POOL_GROUP * (g + 1)]
                dpw_ref[g] += _dot_tn(pbs[g], dg)
                dpns.append(_dot_nt(dg, pw_ref[g]) / counts[g])
            dpn_ref[...] = jnp.concatenate(dpns, axis=1)

    real = lambda w: pl.BlockSpec((tm, w), lambda i: (jnp.maximum(i - 1, 0), 0))
    row = lambda w: pl.BlockSpec((tm, w), lambda i: (i, 0))
    in_specs = [
        real(D_MODEL), real(D_MODEL), row(D_MODEL), row(512), row(D_MODEL), row(512),
        pl.BlockSpec((MAX_WINDOW, 512), lambda i: (jnp.maximum(i * halo_blocks - 1, 0), 0)), row(512),
        _const((2560, D_MODEL), (1, 0)), _const((POOL_WIDTH, D_MODEL)), _const((ATTN_WIDTH, D_MODEL)),
        _const((D_MODEL, D_MODEL)), _const((4, POOL_GROUP, POOL_GROUP)), _const((1, POOL_WIDTH)), _const((1, D_MODEL)),
    ]
    sd = jax.ShapeDtypeStruct
    out_shape = [
        sd((lp, D_MODEL), F32),
        sd((lp, D_MODEL), BF16),
        sd((lp, 512), BF16),
        sd((lp, 512), BF16),
        sd((lp, D_MODEL), BF16),
        sd((lp, D_MODEL), BF16),
        sd((lp, 512), BF16),
        sd((lp, MID_WIDTH), BF16),
        sd((lp, 512), F32),
        sd((1, LANES), F32),
        sd((1, D_MODEL), F32),
        sd((1, 512), F32),
        sd((4, POOL_GROUP, POOL_GROUP), F32),
    ]
    keep = lambda shape: pl.BlockSpec(shape, lambda i: (0,) * len(shape))
    out_specs = [row(D_MODEL), row(D_MODEL), row(512), row(512), row(D_MODEL), row(D_MODEL), row(512),
                 row(MID_WIDTH), row(512),
                 keep((1, LANES)), keep((1, D_MODEL)), keep((1, 512)), keep((4, POOL_GROUP, POOL_GROUP))]
    return pl.pallas_call(
        body, name="middle", grid=(nt,), out_shape=out_shape, in_specs=in_specs, out_specs=out_specs,
        compiler_params=_params(("arbitrary",)),
    )(x, target, h, o, a_pool, u, u, zp, w_main, w_up_pool, w_up_attn, w_out, pool_w, pool_scale, final_g)


DUF_WIDTH = POOL_WIDTH + LANES


def _sequence_grads(dpn, dc, sneg):
    lp = dpn.shape[0]
    tm = ROW_TILE
    nt = lp // tm
    halo_blocks = tm // MAX_WINDOW
    last_halo = lp // MAX_WINDOW - 1

    def body(dpn_ref, dpnh_ref, dc_ref, sn_ref, duf_ref, dbf_ref, carry_ref):
        i = pl.program_id(0)
        t = nt - 1 - i

        @pl.when(i == 0)
        def _():
            carry_ref[...] = jnp.zeros_like(carry_ref)
            dbf_ref[...] = jnp.zeros_like(dbf_ref)

        dpn_t = dpn_ref[...]
        ahead = jnp.where(i == 0, jnp.zeros_like(dpnh_ref), dpnh_ref[...])
        ext = jnp.concatenate([dpn_t, ahead], axis=0)
        counts = _pool_counts(t * tm, tm)
        for g, w in enumerate(POOL_WINDOWS):
            s = ext[:, POOL_GROUP * g:POOL_GROUP * (g + 1)]
            sh = 1
            while sh < w:
                s = s + pltpu.roll(s, tm + MAX_WINDOW - sh, axis=0)
                sh *= 2
            du = s[:tm, :] - dpn_t[:, POOL_GROUP * g:POOL_GROUP * (g + 1)] * counts[g]
            duf_ref[:, POOL_GROUP * g:POOL_GROUP * (g + 1)] = du.astype(BF16)

        dct = dc_ref[:, 0:LANES]
        for hp in range(1, N_HEADS // 2):
            dct = dct + pltpu.roll(dc_ref[:, LANES * hp:LANES * (hp + 1)], 2 * hp, axis=1)
        rloc = lax.broadcasted_iota(jnp.int32, (tm, LANES), 0)
        sh = 1
        while sh < tm:
            dct = dct + jnp.where(rloc + sh < tm, pltpu.roll(dct, tm - sh, axis=0), 0.0)
            sh *= 2
        dct = dct + carry_ref[...]
        carry_ref[...] = dct[0:1, :]
        df = dct * sn_ref[...]
        dbf_ref[...] += jnp.sum(df, axis=0, keepdims=True)
        duf_ref[:, POOL_WIDTH:] = df.astype(BF16)

    rev = lambda w: pl.BlockSpec((tm, w), lambda i: (nt - 1 - i, 0))
    return pl.pallas_call(
        body, name="sequence_grads", grid=(nt,),
        out_shape=[jax.ShapeDtypeStruct((lp, DUF_WIDTH), BF16), jax.ShapeDtypeStruct((1, LANES), F32)],
        in_specs=[rev(512),
                  pl.BlockSpec((MAX_WINDOW, 512), lambda i: (jnp.minimum((nt - i) * halo_blocks, last_halo), 0)),
                  rev(512), rev(LANES)],
        out_specs=[rev(DUF_WIDTH), pl.BlockSpec((1, LANES), lambda i: (0, 0))],
        scratch_shapes=[pltpu.VMEM((1, LANES), F32)],
        compiler_params=_params(("arbitrary",)),
    )(dpn, dpn, dc, sneg)


def _backward_in(x, tile0, norm_g, dh2, duf, dqkv, dmid, w_main, w_f, behind):
    seq = x.shape[0]
    tm = ROW_TILE
    nt = seq // tm + 1
    nx = len(behind)

    def body(x_ref, t0_ref, g_ref, dh2_ref, du_ref, df_ref, dqkv_ref, dzp_ref, dza_ref, dgp_ref, dga_ref,
             wm_ref, wf_ref, *rest):
        gx_ref, gmeta_ref, dg_ref = rest[nx:nx + 3]
        dproj_ref = rest[2 * nx + 3]
        finish_exchange = _behind(behind, rest[:nx], rest[nx + 3:2 * nx + 3], rest[2 * nx + 4:])
        t = pl.program_id(0)

        @pl.when(t == 0)
        def _():
            dg_ref[...] = jnp.zeros_like(dg_ref)

        dproj_ref[:, 0:512] = du_ref[...]
        dproj_ref[:, 512:1024] = dzp_ref[...]
        dproj_ref[:, 1024:1536] = dqkv_ref[0]
        dproj_ref[:, 1536:2048] = dqkv_ref[1]
        dproj_ref[:, 2048:2560] = dqkv_ref[2]
        dproj_ref[:, 2560:3072] = dza_ref[...]
        dproj_ref[:, 3072:4096] = dgp_ref[...]
        dproj_ref[:, 4096:5120] = dga_ref[...]
        dh = _dot(dproj_ref[...], wm_ref[...]) + _dot(df_ref[...], wf_ref[...])
        xt = jnp.where(t == 0, t0_ref[...], x_ref[...])
        r = lax.rsqrt(jnp.mean(xt * xt, axis=-1, keepdims=True) + RMS_EPS)
        xn = xt * r
        dg_ref[...] += jnp.sum(dh * xn, axis=0, keepdims=True)
        dhg = dh * g_ref[...]
        dx = dh2_ref[...] + r * (dhg - xn * jnp.mean(dhg * xn, axis=-1, keepdims=True))

        @pl.when(t > 0)
        def _():
            gx_ref[...] = dx

        @pl.when(t == 0)
        def _():
            gmeta_ref[...] = dx[PAD:, :]
            gx_ref[...] = jnp.zeros_like(gx_ref)

        finish_exchange()

    row = lambda w, j=0: pl.BlockSpec((tm, w), lambda i: (i, j))
    real = pl.BlockSpec((tm, D_MODEL), lambda i: (jnp.maximum(i - 1, 0), 0))
    hbm = pl.BlockSpec(memory_space=pl.ANY)
    in_specs = [
        real, _const((tm, D_MODEL)), _const((1, D_MODEL)), row(D_MODEL),
        row(POOL_WIDTH), row(LANES, POOL_WIDTH // LANES), pl.BlockSpec((3, tm, ATTN_WIDTH), lambda i: (0, i, 0)),
        row(512, 0), row(512, 1), row(1024, 1), row(1024, 2),
        _const((N_MAIN, D_MODEL)), _const((LANES, D_MODEL)),
    ] + [hbm] * nx
    sd = jax.ShapeDtypeStruct
    out_shape = [sd((seq, D_MODEL), F32), sd((N_META, D_MODEL), F32), sd((1, D_MODEL), F32)] + _exchange_results(behind)
    keep = lambda shape: pl.BlockSpec(shape, lambda i: (0,) * len(shape))
    out_specs = [real, keep((N_META, D_MODEL)), keep((1, D_MODEL))] + [hbm] * nx
    return pl.pallas_call(
        body, name="backward_in", grid=(nt,), out_shape=out_shape, in_specs=in_specs, out_specs=out_specs,
        scratch_shapes=[pltpu.VMEM((tm, N_MAIN), BF16)] + _exchange_semaphores(nx),
        compiler_params=_params(("arbitrary",)),
    )(x, tile0, norm_g, dh2, duf, duf, dqkv, dmid, dmid, dmid, dmid, w_main, w_f, *[a for _, a, _ in behind])


def _matmul_tn(name, a, b, tn):
    lp, m = a.shape
    n = b.shape[1]

    def body(a_ref, b_ref, c_ref):
        c_ref[...] = _dot_tn(a_ref[...].astype(BF16), b_ref[...].astype(BF16))

    return pl.pallas_call(
        body, name=name, grid=(n // tn,), out_shape=jax.ShapeDtypeStruct((m, n), F32),
        in_specs=[_const((lp, m)), pl.BlockSpec((lp, tn), lambda j: (0, j))],
        out_specs=pl.BlockSpec((m, tn), lambda j: (0, j)),
        compiler_params=_params(("arbitrary",)),
    )(a, b)


def _matmul_tn_rows(name, a, b, tm):
    lp, m = a.shape
    n = b.shape[1]

    def body(a_ref, b_ref, c_ref):
        c_ref[...] = _dot_tn(a_ref[...].astype(BF16), b_ref[...].astype(BF16))

    return pl.pallas_call(
        body, name=name, grid=(m // tm,), out_shape=jax.ShapeDtypeStruct((m, n), F32),
        in_specs=[pl.BlockSpec((lp, tm), lambda j: (0, j)), _const((lp, n))],
        out_specs=pl.BlockSpec((tm, n), lambda j: (j, 0)),
        compiler_params=_params(("arbitrary",)),
    )(a, b)


def _matmul_tn_stack(name, a, b):
    n_blocks, lp, m = a.shape
    n = b.shape[1]

    def body(a_ref, b_ref, c_ref):
        c_ref[...] = _dot_tn(a_ref[...], b_ref[...])

    return pl.pallas_call(
        body, name=name, grid=(n_blocks,), out_shape=jax.ShapeDtypeStruct((n_blocks * m, n), F32),
        in_specs=[pl.BlockSpec((None, lp, m), lambda j: (j, 0, 0)), _const((lp, n))],
        out_specs=pl.BlockSpec((m, n), lambda j: (j, 0)),
        compiler_params=_params(("arbitrary",)),
    )(a, b)


def _adamw_step(p_ref, w_ref, m_ref, v_ref, g_ref, d_ref, mo_ref, vo_ref):
    g = p_ref[0].astype(F32)
    for s in range(1, p_ref.shape[0]):
        g = g + p_ref[s].astype(F32)
    m_new = ADAM_B1 * m_ref[...] + (1.0 - ADAM_B1) * g
    v_new = ADAM_B2 * v_ref[...] + (1.0 - ADAM_B2) * (g * g)
    m_hat = m_new / (1.0 - ADAM_B1 ** ADAM_STEP)
    v_hat = v_new / (1.0 - ADAM_B2 ** ADAM_STEP)
    g_ref[...] = g
    d_ref[...] = -ADAM_LR * (m_hat / (jnp.sqrt(v_hat) + ADAM_EPS) + ADAM_WD * w_ref[...])
    mo_ref[...] = m_new
    vo_ref[...] = v_new


def _adamw_small(name, groups, loss_parts):
    n = len(groups)

    def body(*refs):
        ins, outs = refs[:4 * n + 1], refs[4 * n + 1:]
        for j in range(n):
            _adamw_step(*ins[4 * j:4 * j + 4], *outs[4 * j:4 * j + 4])
        total = ins[-1][0]
        for s in range(1, N_DEV):
            total = total + ins[-1][s]
        outs[-1][...] = total

    vmem = pl.BlockSpec(memory_space=pltpu.VMEM)
    out_shape = [jax.ShapeDtypeStruct(w.shape, F32) for _, w, _, _ in groups for _ in range(4)]
    out_shape.append(jax.ShapeDtypeStruct(loss_parts.shape[1:], F32))
    res = pl.pallas_call(
        body, name=name, out_shape=out_shape, in_specs=[vmem] * (4 * n + 1), out_specs=[vmem] * (4 * n + 1),
        compiler_params=_params(),
    )(*[a for g in groups for a in g], loss_parts)
    return [res[4 * j:4 * j + 4] for j in range(n)], res[-1]


def _adamw(name, parts, w, m, v, rows, cols=None):
    r, c_all = w.shape
    c = cols or c_all
    n_parts = parts.shape[0]

    def body(p_ref, w_ref, m_ref, v_ref, g_ref, d_ref, mo_ref, vo_ref):
        _adamw_step(p_ref, w_ref, m_ref, v_ref, g_ref, d_ref, mo_ref, vo_ref)

    blk = pl.BlockSpec((rows, c), lambda i, j: (i, j))
    return pl.pallas_call(
        body, name=name, grid=(r // rows, c_all // c), out_shape=[jax.ShapeDtypeStruct((r, c_all), F32)] * 4,
        in_specs=[pl.BlockSpec((n_parts, rows, c), lambda i, j: (0, i, j)), blk, blk, blk],
        out_specs=[blk] * 4,
        compiler_params=_params(("arbitrary", "arbitrary")),
    )(parts, w, m, v)


def _columns_to_slots(a):
    r, c8 = a.shape
    return a.reshape(r, N_DEV, c8 // N_DEV).transpose(1, 0, 2)


def _slots_to_columns(a):
    n, r, c = a.shape
    return a.transpose(1, 0, 2).reshape(r, n * c)


def kernel(x, meta_tokens, norm_g, w_in, b_forget, pool_w, pool_scale, w_up_pool, w_up_attn, w_out, final_norm_g, loss_target, m_meta_tokens, m_norm_g, m_w_in, m_b_forget, m_pool_w, m_pool_scale, m_w_up_pool, m_w_up_attn, m_w_out, m_final_norm_g, v_meta_tokens, v_norm_g, v_w_in, v_b_forget, v_pool_w, v_pool_scale, v_w_up_pool, v_w_up_attn, v_w_out, v_final_norm_g):
    xs = x[0]
    target = loss_target[0]

    g_in, g_upp, g_meta = _gather_two_level(
        "gather_weights", [w_in[0].T.astype(BF16), w_up_pool[0].astype(BF16), meta_tokens], (320, 256, 8))
    w_full = g_in.reshape(N_DEV * g_in.shape[1], D_MODEL)
    w_main = jnp.concatenate([w_full[:N_BEFORE_F], w_full[N_BEFORE_F + N_HEADS:]], axis=0)
    w_f = jnp.pad(w_full[N_BEFORE_F:N_BEFORE_F + N_HEADS], ((0, LANES - N_HEADS), (0, 0)))
    wupp = _slots_to_columns(g_upp)
    meta = _slots_to_columns(g_meta)
    tile0 = jnp.concatenate([jnp.zeros((PAD, D_MODEL), F32), meta], axis=0)
    b_f = jnp.pad(b_forget, ((0, 0), (0, LANES - N_HEADS)))
    pw_b = pool_w[0].astype(BF16)
    final_g = final_norm_g.reshape(1, D_MODEL)

    (h, u, zp, k, v, qt, kt, vt, sneg, a_pool) = _forward_in(xs, tile0, norm_g, w_main, w_f, b_f, pw_b,
                                                              pool_scale, wupp)
    o, lse, g_upa, g_out = _attention_forward(
        qt, k, vt, [("gather", w_up_attn[0].astype(BF16), ALL_PEERS), ("gather", w_out[0].astype(BF16), ALL_PEERS)])
    wupa = _slots_to_columns(g_upa)
    wout = g_out.reshape(D_MODEL, D_MODEL)
    (dh2, mg, yp, ya, dap, daa, do, dmid, dpn,
     loss_part, d_final_g, d_scale, d_pool_w) = _middle(xs, target, h, o, a_pool, u, zp, w_main, wupp, wupa, wout,
                                                        pw_b, pool_scale, final_g)
    dw_out = _matmul_tn("grad_w_out", mg, dh2, 256)
    dw_upp = _matmul_tn("grad_w_up_pool", yp, dap, 512)
    dw_upa = _matmul_tn("grad_w_up_attn", ya, daa, 512)
    dqkv, dc, p_upp, p_upa, p_out, p_pool_w, p_scale, p_final_g = _attention_backward(
        qt, k, kt, v, do, o, lse,
        [("scatter", _columns_to_slots(dw_upp).astype(BF16), ALL_PEERS),
         ("scatter", _columns_to_slots(dw_upa).astype(BF16), ALL_PEERS),
         ("scatter", dw_out.reshape(N_DEV, D_MODEL // N_DEV, D_MODEL).astype(BF16), ALL_PEERS),
         ("gather", d_pool_w.reshape(4 * POOL_GROUP, POOL_GROUP), ALL_PEERS),
         ("gather", d_scale, ALL_PEERS), ("gather", d_final_g, ALL_PEERS)])
    duf, d_bf = _sequence_grads(dpn, dc, sneg)
    g_uf = _matmul_tn_rows("grad_w_in_pool_forget", duf, h, LANES)
    g_qkv = _matmul_tn_stack("grad_w_in_attention", dqkv, h)
    g_mid = _matmul_tn_rows("grad_w_in_gates", dmid, h, 512)
    dw_in = jnp.concatenate([g_uf[:POOL_WIDTH], g_mid[:MID_ZA], g_qkv, g_mid[MID_ZA:MID_GP],
                             g_uf[POOL_WIDTH:POOL_WIDTH + N_HEADS], g_mid[MID_GP:]], axis=0)
    dw_in = dw_in.reshape(N_DEV, dw_in.shape[0] // N_DEV, D_MODEL)
    grad_x, d_meta, d_norm_g, p_in, p_bf, p_loss = _backward_in(
        xs, tile0, norm_g, dh2, duf, dqkv, dmid, w_main, w_f,
        [("scatter", dw_in.astype(BF16), ALL_PEERS), ("gather", d_bf, ALL_PEERS), ("gather", loss_part, ALL_PEERS)])
    p_meta, p_norm_g = _exchange(
        "exchange_gradients", [("scatter", _columns_to_slots(d_meta), ALL_PEERS), ("gather", d_norm_g, ALL_PEERS)])


    def pad_f(a):
        return jnp.pad(a, ((0, 0), (0, LANES - N_HEADS)))

    res = {}
    res["w_in"] = [a.T for a in _adamw("adamw_w_in", p_in, w_in[0].T, m_w_in[0].T, v_w_in[0].T, p_in.shape[1], 256)]
    res["w_up_pool"] = _adamw("adamw_w_up_pool", p_upp, w_up_pool[0], m_w_up_pool[0], v_w_up_pool[0], 512)
    res["w_up_attn"] = _adamw("adamw_w_up_attn", p_upa, w_up_attn[0], m_w_up_attn[0], v_w_up_attn[0], 512)
    res["w_out"] = _adamw("adamw_w_out", p_out, w_out[0], m_w_out[0], v_w_out[0], 128)
    flat = lambda a: a.reshape(4 * POOL_GROUP, POOL_GROUP)
    row = lambda a: a.reshape(1, D_MODEL)
    small, loss_row = _adamw_small(
        "adamw_small",
        [(p_meta, meta_tokens, m_meta_tokens, v_meta_tokens),
         (p_norm_g, norm_g, m_norm_g, v_norm_g),
         (p_bf, pad_f(b_forget), pad_f(m_b_forget), pad_f(v_b_forget)),
         (p_pool_w, flat(pool_w), flat(m_pool_w), flat(v_pool_w)),
         (p_scale, pool_scale, m_pool_scale, v_pool_scale),
         (p_final_g, final_g, row(m_final_norm_g), row(v_final_norm_g))],
        p_loss)
    res["meta_tokens"], res["norm_g"], bf, pw, res["pool_scale"], fg = small
    res["b_forget"] = [a[:, :N_HEADS] for a in bf]
    res["pool_w"] = [a.reshape(pool_w.shape) for a in pw]
    res["final_norm_g"] = [a.reshape(D_MODEL) for a in fg]
    loss = loss_row[0, 0]
    for name in ("w_in", "w_up_pool", "w_up_attn", "w_out"):
        res[name] = [a[None] for a in res[name]]

    order = ["meta_tokens", "norm_g", "w_in", "b_forget", "pool_w", "pool_scale", "w_up_pool", "w_up_attn", "w_out",
             "final_norm_g"]
    outs = [loss, grad_x[None]]
    for part in range(4):
        outs += [res[name][part] for name in order]
    return tuple(outs)
```

```python
import functools

import jax
import jax.numpy as jnp
from jax import lax
from jax.experimental import pallas as pl
from jax.experimental.pallas import tpu as pltpu

F32 = jnp.float32
BF16 = jnp.bfloat16

D_MODEL = 1024
N_META = 16
POOL_WIDTH = 512
ATTN_WIDTH = 512
N_HEADS = 8
HEAD_DIM = 64
POOL_WINDOWS = (2, 4, 8, 16)
POOL_GROUP = 128
MAX_WINDOW = 16
RMS_EPS = 1e-6
N_MAIN = 5120
N_BEFORE_F = 3072
N_DEV = 8
LANES = 128

ROW_TILE = 256
ATT_TILE = 256
ATT_Q_BLOCKS = 4
PAD = ROW_TILE - N_META
VMEM_LIMIT = 56 * 1024 * 1024

ADAM_LR = 0.001
ADAM_B1 = 0.9
ADAM_B2 = 0.999
ADAM_EPS = 1e-08
ADAM_WD = 0.01
ADAM_STEP = 10

MID_ZA, MID_GP, MID_GA, MID_WIDTH = 512, 1024, 2048, 3072
NEG = -1e30
MESH = pl.DeviceIdType.MESH


def _params(sem=None):
    kw = dict(vmem_limit_bytes=VMEM_LIMIT)
    if sem is not None:
        kw["dimension_semantics"] = sem
    return pltpu.CompilerParams(**kw)


def _const(shape, block_index=None):
    idx = block_index or (0,) * len(shape)
    return pl.BlockSpec(shape, lambda i: idx, pipeline_mode=pl.Buffered(1))


def _sigmoid(x):
    return jax.nn.sigmoid(x)


def _dot(a, b):
    return jnp.dot(a, b, preferred_element_type=F32)


def _dot_nt(a, b):
    return lax.dot_general(a, b, (((1,), (1,)), ((), ())), preferred_element_type=F32)


def _dot_tn(a, b):
    return lax.dot_general(a, b, (((0,), (0,)), ((), ())), preferred_element_type=F32)


def _pool_counts(first_row, rows):
    row = first_row + lax.broadcasted_iota(jnp.int32, (rows, 1), 0)
    pos1 = row - PAD + 1
    return [jnp.clip(pos1, 1, w).astype(F32) for w in POOL_WINDOWS]


def _pool_means(u_ext, u, counts):
    rows = u.shape[0]
    out = []
    for g, w in enumerate(POOL_WINDOWS):
        s = u_ext[:, POOL_GROUP * g:POOL_GROUP * (g + 1)]
        sh = 1
        while sh < w:
            s = s + pltpu.roll(s, sh, axis=0)
            sh *= 2
        out.append(s[MAX_WINDOW:MAX_WINDOW + rows, :] / counts[g] - u[:, POOL_GROUP * g:POOL_GROUP * (g + 1)])
    return out


Q_BIAS, Q_ONES, Q_LSE = 64, 67, 70
K_ONES, K_BIAS, K_ONES2 = 64, 67, 70
V_ONES = 64
DO_BIAS = 64


def _lane_ones(lane, ranges):
    hit = None
    for lo, hi in ranges:
        r = (lane >= lo) & (lane < hi)
        hit = r if hit is None else hit | r
    return jnp.where(hit, 1.0, 0.0)


def _put3(base, lane, first, x):
    hi = x.astype(BF16).astype(F32)
    rest = x - hi
    mid = rest.astype(BF16).astype(F32)
    lo = (rest - mid).astype(BF16).astype(F32)
    for j, piece in enumerate((hi, mid, lo)):
        base = jnp.where(lane == first + j, piece, base)
    return base


SIBLING = 1
SAME_CORE = (2, 4, 6)
ALL_PEERS = (1, 2, 3, 4, 5, 6, 7)


def _place():
    return lax.axis_index("x"), lax.axis_index("y"), lax.axis_index("c")


def _peer(r):
    x, y, c = _place()
    return (1 - x if r & 4 else x, 1 - y if r & 2 else y, 1 - c if r & 1 else c)


def _device_slot(p):
    return 4 * p[0] + 2 * p[1] + p[2]


def _chip_slot(p):
    return 2 * p[0] + p[1]


def _exchange(name, items):
    n = len(items)

    def body(*refs):
        copies = _exchange_copies(items, refs[:n], refs[n:2 * n], *refs[2 * n:])
        for cp in copies:
            cp.start()
        for cp in copies:
            cp.wait()

    hbm = pl.BlockSpec(memory_space=pl.ANY)
    return pl.pallas_call(
        body, name=name, out_shape=_exchange_results(items),
        in_specs=[hbm] * n, out_specs=[hbm] * n,
        scratch_shapes=_exchange_semaphores(n),
    )(*[a for _, a, _ in items])


def _exchange_results(items):
    return [jax.ShapeDtypeStruct(((N_DEV,) if kind == "gather" else ()) + a.shape, a.dtype) for kind, a, _ in items]


def _exchange_semaphores(n):
    return [pltpu.SemaphoreType.DMA((n, N_DEV - 1)), pltpu.SemaphoreType.DMA((n, N_DEV - 1)),
            pltpu.SemaphoreType.DMA((n,))]


def _exchange_copies(items, ins, outs, send_sems, recv_sems, local_sems):
    me = _place()
    copies = []
    for a, (kind, _, peers) in enumerate(items):
        slot = _chip_slot if kind == "chips" else _device_slot
        for r in peers:
            peer = _peer(r)
            src = ins[a] if kind in ("swap", "gather") else ins[a].at[slot(peer)]
            dst = outs[a] if kind == "swap" else outs[a].at[slot(me)]
            copies.append(pltpu.make_async_remote_copy(
                src_ref=src, dst_ref=dst, send_sem=send_sems.at[a, r - 1], recv_sem=recv_sems.at[a, r - 1],
                device_id=peer, device_id_type=MESH))
        if kind != "swap":
            src = ins[a] if kind == "gather" else ins[a].at[slot(me)]
            copies.append(pltpu.make_async_copy(src, outs[a].at[slot(me)], local_sems.at[a]))
    return copies


def _gather_two_level(name, arrays, halves):
    n = len(arrays)
    x_flip, y_flip, both = 4, 2, 6

    def body(*refs):
        ins, outs = refs[:n], refs[n:2 * n]
        send_sems, recv_sems, local_sems = refs[2 * n:]
        me, sibling = _place(), _peer(SIBLING)
        xn, yn, dg = _peer(x_flip), _peer(y_flip), _peer(both)

        def part(a, block, half):
            rows = outs[a].at[_device_slot(block)]
            if half is None:
                return rows
            return rows.at[pl.ds(0, halves[a])] if half == 0 else rows.at[pl.ds(halves[a], arrays[a].shape[0] - halves[a])]

        def copy(a, k, block, half, to, src=None):
            dst = part(a, block, half)
            return pltpu.make_async_remote_copy(
                src_ref=dst if src is None else src, dst_ref=dst,
                send_sem=send_sems.at[a, k], recv_sem=recv_sems.at[a, k], device_id=to, device_id_type=MESH)

        sends, own = [], []

        def start(cp):
            cp.start()
            sends.append(cp)

        for a in range(n):
            mine = pltpu.make_async_copy(ins[a], outs[a].at[_device_slot(me)], local_sems.at[a])
            mine.start()
            own.append(mine)
            for k, to in enumerate((sibling, xn, yn)):
                start(copy(a, k, me, None, to, src=ins[a]))
        for a in range(n):
            copy(a, 1, xn, None, me).wait_recv()
            start(copy(a, 3, xn, 0, yn))
            start(copy(a, 5, xn, None, sibling))
        for a in range(n):
            copy(a, 2, yn, None, me).wait_recv()
            start(copy(a, 4, yn, 1, xn))
            start(copy(a, 6, yn, None, sibling))
        for a in range(n):
            copy(a, 3, dg, 0, me).wait_recv()
            copy(a, 4, dg, 1, me).wait_recv()
            start(copy(a, 7, dg, None, sibling))
        for a in range(n):
            copy(a, 0, sibling, None, me).wait_recv()
            for k, r in ((5, x_flip), (6, y_flip), (7, both)):
                copy(a, k, _peer(r | SIBLING), None, me).wait_recv()
        for cp in sends:
            cp.wait_send()
        for cp in own:
            cp.wait()

    hbm = pl.BlockSpec(memory_space=pl.ANY)
    return pl.pallas_call(
        body, name=name, out_shape=[jax.ShapeDtypeStruct((N_DEV,) + a.shape, a.dtype) for a in arrays],
        in_specs=[hbm] * n, out_specs=[hbm] * n,
        scratch_shapes=[pltpu.SemaphoreType.DMA((n, 8)), pltpu.SemaphoreType.DMA((n, 8)),
                        pltpu.SemaphoreType.DMA((n,))],
    )(*arrays)


def _forward_in(x, tile0, norm_g, w_main, w_f, b_f, pool_w, pool_scale, w_up_pool):
    seq = x.shape[0]
    nt = seq // ROW_TILE + 1
    lp = nt * ROW_TILE
    tm = ROW_TILE

    def body(x_ref, t0_ref, g_ref, wa_ref, wf_ref, bf_ref, pw_ref, sc_ref, wup_ref,
             h_ref, u_ref, zp_ref, k_ref, v_ref, qt_ref, kt_ref, vt_ref, sn_ref, ap_ref,
             uext_ref, carry_ref):
        i = pl.program_id(0)

        @pl.when(i == 0)
        def _():
            uext_ref[...] = jnp.zeros_like(uext_ref)
            carry_ref[...] = jnp.zeros_like(carry_ref)

        xt = jnp.where(i == 0, t0_ref[...], x_ref[...])
        r = lax.rsqrt(jnp.mean(xt * xt, axis=-1, keepdims=True) + RMS_EPS)
        h = (xt * r * g_ref[...]).astype(BF16)
        h_ref[...] = h
        pa = _dot_nt(h, wa_ref[...])
        u = pa[:, :512]
        zp = pa[:, 512:1024]
        u_ref[...] = u
        zp_ref[...] = zp

        uext_ref[0:MAX_WINDOW, :] = uext_ref[tm:tm + MAX_WINDOW, :]
        uext_ref[MAX_WINDOW:, :] = u
        counts = _pool_counts(i * tm, tm)
        ps = _pool_means(uext_ref[...], u, counts)
        ppw = jnp.concatenate([_dot(ps[g].astype(BF16), pw_ref[g]) for g in range(4)], axis=1)
        y_pool = ppw * sc_ref[...] * (zp * _sigmoid(zp))
        ap_ref[...] = _dot(y_pool.astype(BF16), wup_ref[...]).astype(BF16)

        fl = _dot_nt(h, wf_ref[...]) + bf_ref[...]
        row = i * tm + lax.broadcasted_iota(jnp.int32, (tm, LANES), 0)
        rloc = lax.broadcasted_iota(jnp.int32, (tm, LANES), 0)
        lane = lax.broadcasted_iota(jnp.int32, (tm, LANES), 1)
        live = (row >= PAD) & (lane < N_HEADS)
        logf = jnp.minimum(fl, 0.0) - jnp.log1p(jnp.exp(-jnp.abs(fl)))
        cs = jnp.where(live, logf, 0.0)
        sh = 1
        while sh < tm:
            cs = cs + jnp.where(rloc >= sh, pltpu.roll(cs, sh, axis=0), 0.0)
            sh *= 2
        cs = cs + carry_ref[...]
        carry_ref[...] = cs[tm - 1:tm, :]
        sn_ref[...] = jnp.where(live, _sigmoid(-fl), 0.0)

        rows1 = i * tm + lax.broadcasted_iota(jnp.int32, (tm, 1), 0)
        ones_q = _lane_ones(lane, ((Q_ONES, Q_ONES + 3),))
        ones_k = _lane_ones(lane, ((K_ONES, K_ONES + 3), (K_ONES2, K_ONES2 + 3)))
        ones_v = _lane_ones(lane, ((V_ONES, V_ONES + 3),))
        for hp in range(N_HEADS // 2):
            qp = pa[:, 1024 + LANES * hp:1024 + LANES * (hp + 1)] * 0.125
            kp = pa[:, 1536 + LANES * hp:1536 + LANES * (hp + 1)]
            vp = pa[:, 2048 + LANES * hp:2048 + LANES * (hp + 1)]
            for e in range(2):
                head = 2 * hp + e
                if e:
                    qp, kp, vp = (pltpu.roll(a, HEAD_DIM, axis=1) for a in (qp, kp, vp))
                c_h = cs[:, head:head + 1]
                q_h = jnp.where(lane < HEAD_DIM, qp, _put3(ones_q, lane, Q_BIAS, c_h))
                qt_ref[head] = q_h.T.astype(BF16)
                minus_ck = jnp.where(rows1 >= PAD, -c_h, NEG)
                k_h = jnp.where(lane < HEAD_DIM, kp, _put3(ones_k, lane, K_BIAS, minus_ck))
                k_ref[head] = k_h.astype(BF16)
                kt_ref[head] = k_h.T.astype(BF16)
                v_h = jnp.where(lane < HEAD_DIM, vp, ones_v)
                v_ref[head] = v_h.astype(BF16)
                vt_ref[head] = v_h.T.astype(BF16)

    row_f32 = lambda w: pl.BlockSpec((tm, w), lambda i: (i, 0))
    out_shape = [
        jax.ShapeDtypeStruct((lp, D_MODEL), BF16),
        jax.ShapeDtypeStruct((lp, POOL_WIDTH), F32),
        jax.ShapeDtypeStruct((lp, POOL_WIDTH), F32),
        jax.ShapeDtypeStruct((N_HEADS, lp, LANES), BF16),
        jax.ShapeDtypeStruct((N_HEADS, lp, LANES), BF16),
        jax.ShapeDtypeStruct((N_HEADS, LANES, lp), BF16),
        jax.ShapeDtypeStruct((N_HEADS, LANES, lp), BF16),
        jax.ShapeDtypeStruct((N_HEADS, LANES, lp), BF16),
        jax.ShapeDtypeStruct((lp, LANES), F32),
        jax.ShapeDtypeStruct((lp, D_MODEL), BF16),
    ]
    heads = pl.BlockSpec((N_HEADS, tm, LANES), lambda i: (0, i, 0))
    heads_t = pl.BlockSpec((N_HEADS, LANES, tm), lambda i: (0, 0, i))
    out_specs = [row_f32(D_MODEL), row_f32(512), row_f32(512), heads, heads, heads_t, heads_t, heads_t,
                 row_f32(LANES), row_f32(D_MODEL)]
    in_specs = [
        pl.BlockSpec((tm, D_MODEL), lambda i: (jnp.maximum(i - 1, 0), 0)),
        _const((tm, D_MODEL)), _const((1, D_MODEL)),
        _const((2560, D_MODEL)), _const((LANES, D_MODEL)), _const((1, LANES)),
        _const((4, POOL_GROUP, POOL_GROUP)), _const((1, POOL_WIDTH)), _const((POOL_WIDTH, D_MODEL)),
    ]
    return pl.pallas_call(
        body, name="forward_in", grid=(nt,), out_shape=out_shape, in_specs=in_specs, out_specs=out_specs,
        scratch_shapes=[pltpu.VMEM((tm + MAX_WINDOW, POOL_WIDTH), F32), pltpu.VMEM((1, LANES), F32)],
        compiler_params=_params(("arbitrary",)),
    )(x, tile0, norm_g, w_main, w_f, b_f, pool_w, pool_scale, w_up_pool)


def _causal(tb):
    return lax.broadcasted_iota(jnp.int32, (tb, tb), 1) <= lax.broadcasted_iota(jnp.int32, (tb, tb), 0)


def _pair_lanes(a0, a1):
    lane = lax.broadcasted_iota(jnp.int32, a0.shape, 1)
    return jnp.where(lane < HEAD_DIM, a0, pltpu.roll(a1, HEAD_DIM, axis=1))


def _behind(items, ins, outs, sems):
    step, last = pl.program_id(0), pl.num_programs(0) - 1

    @pl.when(step == 0)
    def _():
        for cp in _exchange_copies(items, ins, outs, *sems):
            cp.start()

    def finish():
        @pl.when(step == last)
        def _():
            for cp in _exchange_copies(items, ins, outs, *sems):
                cp.wait()

    return finish


def _attention_forward(qt, k, vt, behind):
    lp = k.shape[1]
    tk = ATT_TILE
    tq_big = ATT_Q_BLOCKS * tk
    n_big = (lp // tk - 1) // ATT_Q_BLOCKS
    assert lp == tk + n_big * tq_big and ATT_Q_BLOCKS % 2 == 0
    nx = len(behind)

    def body(qt_ref, k_ref, vt_ref, *rest):
        o_ref, lse_ref = rest[nx:nx + 2]
        s_buf, m_scr, acc_scr = rest[2 * nx + 2:2 * nx + 5]
        finish_exchange = _behind(behind, rest[:nx], rest[nx + 2:2 * nx + 2], rest[2 * nx + 5:])

        def q_tile(q0, tq, pairs):
            first = q0 // tk
            qts = [qt_ref[e, :, pl.ds(q0, tq)] for e in range(2)]

            def block(kj):
                return pl.ds(kj * tk if isinstance(kj, int) else pl.multiple_of(kj * tk, tk), tk)

            def step(kj, rd, wr, c0=0, diagonal=False):
                c1 = c0 + tk if diagonal else c0
                for e in range(2):
                    s = s_buf[rd, e, :, c0:tq]
                    if wr is not None:
                        s_buf[wr, e, :, c1:tq] = _dot(k_ref[e, block(kj + 1), :], qts[e][:, c1:tq])
                    if diagonal:
                        keys = lax.broadcasted_iota(jnp.int32, s.shape, 0)
                        s = jnp.where(keys <= lax.broadcasted_iota(jnp.int32, s.shape, 1), s, NEG)
                    m = m_scr[e, :, c0:tq]
                    m_new = jnp.maximum(m, jnp.max(s, axis=0, keepdims=True))
                    p = jnp.exp(s - m_new)
                    pv = _dot(vt_ref[e, :, block(kj)], p.astype(BF16))
                    acc_scr[e, :, c0:tq] = jnp.exp(m - m_new) * acc_scr[e, :, c0:tq] + pv
                    m_scr[e, :, c0:tq] = m_new

            for e in range(2):
                m_scr[e, :, 0:tq] = jnp.full((1, tq), NEG, F32)
                acc_scr[e, :, 0:tq] = jnp.zeros((LANES, tq), F32)
                s_buf[0, e, :, 0:tq] = _dot(k_ref[e, block(0), :], qts[e])
            if pairs is None:
                step(0, 0, None, 0, True)
            else:
                step(0, 0, 1)

                def two_steps(t, _):
                    step(1 + 2 * t, 1, 0)
                    step(2 + 2 * t, 0, 1)
                    return 0

                lax.fori_loop(0, pairs, two_steps, 0)
                for b in range(tq // tk):
                    step(first + b, (b + 1) % 2, b % 2 if (b + 1) * tk < tq else None, b * tk, True)
            outs, lses = [], []
            for e in range(2):
                acc = acc_scr[e, :, 0:tq]
                l = acc[V_ONES:V_ONES + 1, :]
                outs.append((acc / l).T)
                lses.append(m_scr[e, :, 0:tq] + jnp.log(l))
            o_ref[pl.ds(q0, tq), :] = _pair_lanes(outs[0], outs[1]).astype(BF16)
            lse_rows = jnp.concatenate(lses + [jnp.zeros((LANES - 2, tq), F32)], axis=0)
            lse_ref[pl.ds(q0, tq), :] = lse_rows.T

        q_tile(0, tk, None)

        def big_tile(i, _):
            q_tile(pl.multiple_of(tk + i * tq_big, tk), tq_big, (ATT_Q_BLOCKS // 2) * i)
            return 0

        lax.fori_loop(0, n_big, big_tile, 0)
        finish_exchange()

    pair = pl.BlockSpec((lp, LANES), lambda hp: (0, hp))
    heads = pl.BlockSpec((2, lp, LANES), lambda hp: (hp, 0, 0), pipeline_mode=pl.Buffered(1))
    heads_t = pl.BlockSpec((2, LANES, lp), lambda hp: (hp, 0, 0), pipeline_mode=pl.Buffered(1))
    hbm = pl.BlockSpec(memory_space=pl.ANY)
    return pl.pallas_call(
        body, name="attention_forward", grid=(N_HEADS // 2,),
        out_shape=[jax.ShapeDtypeStruct((lp, ATTN_WIDTH), BF16), jax.ShapeDtypeStruct((lp, ATTN_WIDTH), F32)]
        + _exchange_results(behind),
        in_specs=[heads_t, heads, heads_t] + [hbm] * nx,
        out_specs=[pair, pair] + [hbm] * nx,
        scratch_shapes=[pltpu.VMEM((2, 2, tk, tq_big), F32), pltpu.VMEM((2, 1, tq_big), F32),
                        pltpu.VMEM((2, LANES, tq_big), F32)] + _exchange_semaphores(nx),
        compiler_params=_params(("arbitrary",)),
    )(qt, k, vt, *[a for _, a, _ in behind])


def _rows3(first, x):
    sub = lax.broadcasted_iota(jnp.int32, (LANES, x.shape[1]), 0)
    hi = x.astype(BF16).astype(F32)
    rest = x - hi
    mid = rest.astype(BF16).astype(F32)
    lo = (rest - mid).astype(BF16).astype(F32)
    out = jnp.zeros((LANES, x.shape[1]), F32)
    for j, piece in enumerate((hi, mid, lo)):
        out = jnp.where(sub == first + j, piece, out)
    return out


def _attention_backward(qt, k, kt, v, do, o, lse, behind):
    lp = k.shape[1]
    tb = ATT_TILE
    nb = lp // tb
    tq_big = ATT_Q_BLOCKS * tb
    n_big = (nb - 1) // ATT_Q_BLOCKS
    assert lp == tb + n_big * tq_big and ATT_Q_BLOCKS % 2 == 0
    nx = len(behind)

    def body(qt_ref, k_ref, kt_ref, v_ref, do_ref, o_ref, lse_ref, *rest):
        dqkv_ref, dc_ref = rest[nx:nx + 2]
        q2_ref, do2_ref, dk_acc, dv_acc, dq_scr, s_buf = rest[2 * nx + 2:2 * nx + 8]
        finish_exchange = _behind(behind, rest[:nx], rest[nx + 2:2 * nx + 2], rest[2 * nx + 8:])
        sub = lax.broadcasted_iota(jnp.int32, (LANES, tb), 0)

        def lanes01(row0, row1):
            n = row0.shape[1]
            return jnp.concatenate([row0, row1, jnp.zeros((LANES - 2, n), F32)], axis=0).T

        def prepare(bi, _):
            r0 = pl.multiple_of(bi * tb, tb)
            queries = r0 + lax.broadcasted_iota(jnp.int32, (1, tb), 1)
            dob = do_ref[pl.ds(r0, tb), :].astype(F32)
            do_t = dob.T
            dd_t = (dob * o_ref[pl.ds(r0, tb), :].astype(F32)).T
            lse_t = lse_ref[pl.ds(r0, tb), :].T
            for e in range(2):
                delta = jnp.sum(dd_t[HEAD_DIM * e:HEAD_DIM * (e + 1), :], axis=0, keepdims=True)
                do_e = jnp.concatenate([do_t[HEAD_DIM * e:HEAD_DIM * (e + 1), :], jnp.zeros((HEAD_DIM, tb), F32)], axis=0)
                do2_ref[e, :, pl.ds(r0, tb)] = jnp.where(sub < HEAD_DIM, do_e, _rows3(DO_BIAS, -delta)).astype(BF16)
                minus_lse = jnp.where(queries >= PAD, -lse_t[e:e + 1, :], NEG)
                keep = (sub < Q_LSE) | (sub >= Q_LSE + 3)
                q2_ref[e, :, pl.ds(r0, tb)] = jnp.where(keep, qt_ref[e, :, pl.ds(r0, tb)].astype(F32),
                                                        _rows3(Q_LSE, minus_lse)).astype(BF16)
            return 0

        lax.fori_loop(0, nb, prepare, 0)
        dk_acc[...] = jnp.zeros_like(dk_acc)
        dv_acc[...] = jnp.zeros_like(dv_acc)

        def q_tile(q0, tq, pairs):
            first = q0 // tb
            qts = [q2_ref[e, :, pl.ds(q0, tq)] for e in range(2)]
            dots = [do2_ref[e, :, pl.ds(q0, tq)] for e in range(2)]

            def block(kj):
                return pl.ds(kj * tb if isinstance(kj, int) else pl.multiple_of(kj * tb, tb), tb)

            def step(kj, rd, wr, c0=0, diagonal=False):
                c1 = c0 + tb if diagonal else c0
                for e in range(2):
                    s = s_buf[rd, e, :, c0:tq]
                    if wr is not None:
                        s_buf[wr, e, :, c1:tq] = _dot(k_ref[e, block(kj + 1), :], qts[e][:, c1:tq])
                    dpd = _dot(v_ref[e, block(kj), :], dots[e][:, c0:tq])
                    p = jnp.exp(s)
                    if diagonal:
                        keys = lax.broadcasted_iota(jnp.int32, s.shape, 0)
                        p = jnp.where(keys <= lax.broadcasted_iota(jnp.int32, s.shape, 1), p, 0.0)
                    dsb = (p * dpd).astype(BF16)
                    dv_acc[e, :, block(kj)] += _dot_nt(dots[e][:, c0:tq], p.astype(BF16))
                    dk_acc[e, :, block(kj)] += _dot_nt(qts[e][:, c0:tq], dsb)
                    dq_scr[e, :, c0:tq] += _dot(kt_ref[e, :, block(kj)], dsb)

            for e in range(2):
                dq_scr[e, :, 0:tq] = jnp.zeros((LANES, tq), F32)
                s_buf[0, e, :, 0:tq] = _dot(k_ref[e, block(0), :], qts[e])
            if pairs is None:
                step(0, 0, None, 0, True)
            else:
                step(0, 0, 1)

                def two_steps(t, _):
                    step(1 + 2 * t, 1, 0)
                    step(2 + 2 * t, 0, 1)
                    return 0

                lax.fori_loop(0, pairs, two_steps, 0)
                for b in range(tq // tb):
                    step(first + b, (b + 1) % 2, b % 2 if (b + 1) * tb < tq else None, b * tb, True)
            dq0, dq1 = dq_scr[0, :, 0:tq], dq_scr[1, :, 0:tq]
            dqkv_ref[0, pl.ds(q0, tq), :] = (_pair_lanes(dq0.T, dq1.T) * 0.125).astype(BF16)
            dc_ref[pl.ds(q0, tq), :] = lanes01(dq0[K_ONES:K_ONES + 1, :], dq1[K_ONES:K_ONES + 1, :])

        q_tile(0, tb, None)

        def big_tile(i, _):
            q_tile(pl.multiple_of(tb + i * tq_big, tb), tq_big, (ATT_Q_BLOCKS // 2) * i)
            return 0

        lax.fori_loop(0, n_big, big_tile, 0)

        def finish(bi, _):
            r0 = pl.multiple_of(bi * tb, tb)
            dk0, dk1 = dk_acc[0, :, pl.ds(r0, tb)], dk_acc[1, :, pl.ds(r0, tb)]
            dqkv_ref[1, pl.ds(r0, tb), :] = _pair_lanes(dk0.T, dk1.T).astype(BF16)
            dqkv_ref[2, pl.ds(r0, tb), :] = _pair_lanes(dv_acc[0, :, pl.ds(r0, tb)].T,
                                                        dv_acc[1, :, pl.ds(r0, tb)].T).astype(BF16)
            dc_ref[pl.ds(r0, tb), :] = dc_ref[pl.ds(r0, tb), :] - lanes01(dk0[Q_ONES:Q_ONES + 1, :], dk1[Q_ONES:Q_ONES + 1, :])
            return 0

        lax.fori_loop(0, nb, finish, 0)
        finish_exchange()

    once = pl.Buffered(1)
    pair = pl.BlockSpec((lp, LANES), lambda hp: (0, hp))
    pair_in = pl.BlockSpec((lp, LANES), lambda hp: (0, hp), pipeline_mode=once)
    heads = pl.BlockSpec((2, lp, LANES), lambda hp: (hp, 0, 0), pipeline_mode=once)
    heads_t = pl.BlockSpec((2, LANES, lp), lambda hp: (hp, 0, 0), pipeline_mode=once)
    hbm = pl.BlockSpec(memory_space=pl.ANY)
    return pl.pallas_call(
        body, name="attention_backward", grid=(N_HEADS // 2,),
        out_shape=[jax.ShapeDtypeStruct((3, lp, ATTN_WIDTH), BF16), jax.ShapeDtypeStruct((lp, ATTN_WIDTH), F32)]
        + _exchange_results(behind),
        in_specs=[heads_t, heads, heads_t, heads, pair_in, pair_in, pair_in] + [hbm] * nx,
        out_specs=[pl.BlockSpec((3, lp, LANES), lambda hp: (0, 0, hp)), pair] + [hbm] * nx,
        scratch_shapes=[pltpu.VMEM((2, LANES, lp), BF16), pltpu.VMEM((2, LANES, lp), BF16),
                        pltpu.VMEM((2, LANES, lp), F32), pltpu.VMEM((2, LANES, lp), F32),
                        pltpu.VMEM((2, LANES, tq_big), F32), pltpu.VMEM((2, 2, tb, tq_big), F32)]
        + _exchange_semaphores(nx),
        compiler_params=_params(("arbitrary",)),
    )(qt, k, kt, v, do, o, lse, *[a for _, a, _ in behind])


def _middle(x, target, h, o, a_pool, u, zp, w_main, w_up_pool, w_up_attn, w_out, pool_w, pool_scale, final_g):
    seq = x.shape[0]
    tm = ROW_TILE
    nt = seq // tm + 1
    lp = nt * tm
    halo_blocks = tm // MAX_WINDOW

    def body(x_ref, t_ref, h_ref, o_ref, ap_ref, u_ref, uh_ref, zp_ref,
             wc_ref, wupp_ref, wupa_ref, wout_ref, pw_ref, sc_ref, gf_ref,
             dh2_ref, mg_ref, yp_ref, ya_ref, dap_ref, daa_ref, do_ref, dmid_ref, dpn_ref,
             loss_ref, dgf_ref, dsc_ref, dpw_ref):
        i = pl.program_id(0)
        tiles = (dh2_ref, mg_ref, yp_ref, ya_ref, dap_ref, daa_ref, do_ref, dmid_ref, dpn_ref)

        @pl.when(i == 0)
        def _():
            for ref in tiles + (loss_ref, dgf_ref, dsc_ref, dpw_ref):
                ref[...] = jnp.zeros_like(ref)

        @pl.when(i > 0)
        def _():
            xt = x_ref[...]
            hb = h_ref[...]
            pc = _dot_nt(hb, wc_ref[...])
            za, gp, ga = pc[:, :512], pc[:, 512:1536], pc[:, 1536:]
            of = o_ref[...].astype(F32)
            sza = _sigmoid(za)
            silu_za = za * sza
            ya = (of * silu_za).astype(BF16)
            ya_ref[...] = ya
            aa = _dot(ya, wupa_ref[...])
            ap = ap_ref[...].astype(F32)
            sgp, sga = _sigmoid(gp), _sigmoid(ga)
            mg = (sgp * ap + sga * aa).astype(BF16)
            mg_ref[...] = mg
            h2 = xt + _dot(mg, wout_ref[...])
            r2 = lax.rsqrt(jnp.mean(h2 * h2, axis=-1, keepdims=True) + RMS_EPS)
            h2n = h2 * r2
            gf = gf_ref[...]
            diff = h2n * gf - t_ref[...]
            loss_ref[...] += 0.5 * jnp.sum(jnp.mean(diff * diff, axis=-1, keepdims=True), axis=0, keepdims=True)
            dy = diff * (1.0 / D_MODEL)
            dgf_ref[...] += jnp.sum(dy * h2n, axis=0, keepdims=True)
            dyg = dy * gf
            dh2 = r2 * (dyg - h2n * jnp.mean(dyg * h2n, axis=-1, keepdims=True))
            dh2_ref[...] = dh2
            dmg = _dot_nt(dh2.astype(BF16), wout_ref[...])
            dap = (dmg * sgp).astype(BF16)
            daa = (dmg * sga).astype(BF16)
            dap_ref[...] = dap
            daa_ref[...] = daa
            dmid_ref[:, MID_GP:MID_GA] = (dmg * ap * sgp * (1.0 - sgp)).astype(BF16)
            dmid_ref[:, MID_GA:] = (dmg * aa * sga * (1.0 - sga)).astype(BF16)
            dyp = _dot_nt(dap, wupp_ref[...])
            dya = _dot_nt(daa, wupa_ref[...])
            do_ref[...] = (dya * silu_za).astype(BF16)
            dmid_ref[:, MID_ZA:MID_GP] = (dya * of * (sza * (1.0 + za * (1.0 - sza)))).astype(BF16)

            u = u_ref[...]
            zp = zp_ref[...]
            counts = _pool_counts(i * tm, tm)
            ps = _pool_means(jnp.concatenate([uh_ref[...], u], axis=0), u, counts)
            pbs = [p.astype(BF16) for p in ps]
            ppw = jnp.concatenate([_dot(pbs[g], pw_ref[g]) for g in range(4)], axis=1)
            sc = sc_ref[...]
            szp = _sigmoid(zp)
            silu_zp = zp * szp
            ypre = ppw * sc
            yp_ref[...] = (ypre * silu_zp).astype(BF16)
            dypre = dyp * silu_zp
            dmid_ref[:, :MID_ZA] = (dyp * ypre * (szp * (1.0 + zp * (1.0 - szp)))).astype(BF16)
            dsc_ref[...] += jnp.sum(dypre * ppw, axis=0, keepdims=True)
            dppw = (dypre * sc).astype(BF16)
            dpns = []
            for g in range(4):
                dg = dppw[:, POOL_GROUP * g:POOL_GROUP * (g + 1)]
                dpw_ref[g] += _dot_tn(pbs[g], dg)
                dpns.append(_dot_nt(dg, pw_ref[g]) / counts[g])
            dpn_ref[...] = jnp.concatenate(dpns, axis=1)

    real = lambda w: pl.BlockSpec((tm, w), lambda i: (jnp.maximum(i - 1, 0), 0))
    row = lambda w: pl.BlockSpec((tm, w), lambda i: (i, 0))
    in_specs = [
        real(D_MODEL), real(D_MODEL), row(D_MODEL), row(512), row(D_MODEL), row(512),
        pl.BlockSpec((MAX_WINDOW, 512), lambda i: (jnp.maximum(i * halo_blocks - 1, 0), 0)), row(512),
        _const((2560, D_MODEL), (1, 0)), _const((POOL_WIDTH, D_MODEL)), _const((ATTN_WIDTH, D_MODEL)),
        _const((D_MODEL, D_MODEL)), _const((4, POOL_GROUP, POOL_GROUP)), _const((1, POOL_WIDTH)), _const((1, D_MODEL)),
    ]
    sd = jax.ShapeDtypeStruct
    out_shape = [
        sd((lp, D_MODEL), F32),
        sd((lp, D_MODEL), BF16),
        sd((lp, 512), BF16),
        sd((lp, 512), BF16),
        sd((lp, D_MODEL), BF16),
        sd((lp, D_MODEL), BF16),
        sd((lp, 512), BF16),
        sd((lp, MID_WIDTH), BF16),
        sd((lp, 512), F32),
        sd((1, LANES), F32),
        sd((1, D_MODEL), F32),
        sd((1, 512), F32),
        sd((4, POOL_GROUP, POOL_GROUP), F32),
    ]
    keep = lambda shape: pl.BlockSpec(shape, lambda i: (0,) * len(shape))
    out_specs = [row(D_MODEL), row(D_MODEL), row(512), row(512), row(D_MODEL), row(D_MODEL), row(512),
                 row(MID_WIDTH), row(512),
                 keep((1, LANES)), keep((1, D_MODEL)), keep((1, 512)), keep((4, POOL_GROUP, POOL_GROUP))]
    return pl.pallas_call(
        body, name="middle", grid=(nt,), out_shape=out_shape, in_specs=in_specs, out_specs=out_specs,
        compiler_params=_params(("arbitrary",)),
    )(x, target, h, o, a_pool, u, u, zp, w_main, w_up_pool, w_up_attn, w_out, pool_w, pool_scale, final_g)


DUF_WIDTH = POOL_WIDTH + LANES


def _sequence_grads(dpn, dc, sneg):
    lp = dpn.shape[0]
    tm = ROW_TILE
    nt = lp // tm
    halo_blocks = tm // MAX_WINDOW
    last_halo = lp // MAX_WINDOW - 1

    def body(dpn_ref, dpnh_ref, dc_ref, sn_ref, duf_ref, dbf_ref, carry_ref):
        i = pl.program_id(0)
        t = nt - 1 - i

        @pl.when(i == 0)
        def _():
            carry_ref[...] = jnp.zeros_like(carry_ref)
            dbf_ref[...] = jnp.zeros_like(dbf_ref)

        dpn_t = dpn_ref[...]
        ahead = jnp.where(i == 0, jnp.zeros_like(dpnh_ref), dpnh_ref[...])
        ext = jnp.concatenate([dpn_t, ahead], axis=0)
        counts = _pool_counts(t * tm, tm)
        for g, w in enumerate(POOL_WINDOWS):
            s = ext[:, POOL_GROUP * g:POOL_GROUP * (g + 1)]
            sh = 1
            while sh < w:
                s = s + pltpu.roll(s, tm + MAX_WINDOW - sh, axis=0)
                sh *= 2
            du = s[:tm, :] - dpn_t[:, POOL_GROUP * g:POOL_GROUP * (g + 1)] * counts[g]
            duf_ref[:, POOL_GROUP * g:POOL_GROUP * (g + 1)] = du.astype(BF16)

        dct = dc_ref[:, 0:LANES]
        for hp in range(1, N_HEADS // 2):
            dct = dct + pltpu.roll(dc_ref[:, LANES * hp:LANES * (hp + 1)], 2 * hp, axis=1)
        rloc = lax.broadcasted_iota(jnp.int32, (tm, LANES), 0)
        sh = 1
        while sh < tm:
            dct = dct + jnp.where(rloc + sh < tm, pltpu.roll(dct, tm - sh, axis=0), 0.0)
            sh *= 2
        dct = dct + carry_ref[...]
        carry_ref[...] = dct[0:1, :]
        df = dct * sn_ref[...]
        dbf_ref[...] += jnp.sum(df, axis=0, keepdims=True)
        duf_ref[:, POOL_WIDTH:] = df.astype(BF16)

    rev = lambda w: pl.BlockSpec((tm, w), lambda i: (nt - 1 - i, 0))
    return pl.pallas_call(
        body, name="sequence_grads", grid=(nt,),
        out_shape=[jax.ShapeDtypeStruct((lp, DUF_WIDTH), BF16), jax.ShapeDtypeStruct((1, LANES), F32)],
        in_specs=[rev(512),
                  pl.BlockSpec((MAX_WINDOW, 512), lambda i: (jnp.minimum((nt - i) * halo_blocks, last_halo), 0)),
                  rev(512), rev(LANES)],
        out_specs=[rev(DUF_WIDTH), pl.BlockSpec((1, LANES), lambda i: (0, 0))],
        scratch_shapes=[pltpu.VMEM((1, LANES), F32)],
        compiler_params=_params(("arbitrary",)),
    )(dpn, dpn, dc, sneg)


def _backward_in(x, tile0, norm_g, dh2, duf, dqkv, dmid, w_main, w_f, behind):
    seq = x.shape[0]
    tm = ROW_TILE
    nt = seq // tm + 1
    nx = len(behind)

    def body(x_ref, t0_ref, g_ref, dh2_ref, du_ref, df_ref, dqkv_ref, dzp_ref, dza_ref, dgp_ref, dga_ref,
             wm_ref, wf_ref, *rest):
        gx_ref, gmeta_ref, dg_ref = rest[nx:nx + 3]
        dproj_ref = rest[2 * nx + 3]
        finish_exchange = _behind(behind, rest[:nx], rest[nx + 3:2 * nx + 3], rest[2 * nx + 4:])
        t = pl.program_id(0)

        @pl.when(t == 0)
        def _():
            dg_ref[...] = jnp.zeros_like(dg_ref)

        dproj_ref[:, 0:512] = du_ref[...]
        dproj_ref[:, 512:1024] = dzp_ref[...]
        dproj_ref[:, 1024:1536] = dqkv_ref[0]
        dproj_ref[:, 1536:2048] = dqkv_ref[1]
        dproj_ref[:, 2048:2560] = dqkv_ref[2]
        dproj_ref[:, 2560:3072] = dza_ref[...]
        dproj_ref[:, 3072:4096] = dgp_ref[...]
        dproj_ref[:, 4096:5120] = dga_ref[...]
        dh = _dot(dproj_ref[...], wm_ref[...]) + _dot(df_ref[...], wf_ref[...])
        xt = jnp.where(t == 0, t0_ref[...], x_ref[...])
        r = lax.rsqrt(jnp.mean(xt * xt, axis=-1, keepdims=True) + RMS_EPS)
        xn = xt * r
        dg_ref[...] += jnp.sum(dh * xn, axis=0, keepdims=True)
        dhg = dh * g_ref[...]
        dx = dh2_ref[...] + r * (dhg - xn * jnp.mean(dhg * xn, axis=-1, keepdims=True))

        @pl.when(t > 0)
        def _():
            gx_ref[...] = dx

        @pl.when(t == 0)
        def _():
            gmeta_ref[...] = dx[PAD:, :]
            gx_ref[...] = jnp.zeros_like(gx_ref)

        finish_exchange()

    row = lambda w, j=0: pl.BlockSpec((tm, w), lambda i: (i, j))
    real = pl.BlockSpec((tm, D_MODEL), lambda i: (jnp.maximum(i - 1, 0), 0))
    hbm = pl.BlockSpec(memory_space=pl.ANY)
    in_specs = [
        real, _const((tm, D_MODEL)), _const((1, D_MODEL)), row(D_MODEL),
        row(POOL_WIDTH), row(LANES, POOL_WIDTH // LANES), pl.BlockSpec((3, tm, ATTN_WIDTH), lambda i: (0, i, 0)),
        row(512, 0), row(512, 1), row(1024, 1), row(1024, 2),
        _const((N_MAIN, D_MODEL)), _const((LANES, D_MODEL)),
    ] + [hbm] * nx
    sd = jax.ShapeDtypeStruct
    out_shape = [sd((seq, D_MODEL), F32), sd((N_META, D_MODEL), F32), sd((1, D_MODEL), F32)] + _exchange_results(behind)
    keep = lambda shape: pl.BlockSpec(shape, lambda i: (0,) * len(shape))
    out_specs = [real, keep((N_META, D_MODEL)), keep((1, D_MODEL))] + [hbm] * nx
    return pl.pallas_call(
        body, name="backward_in", grid=(nt,), out_shape=out_shape, in_specs=in_specs, out_specs=out_specs,
        scratch_shapes=[pltpu.VMEM((tm, N_MAIN), BF16)] + _exchange_semaphores(nx),
        compiler_params=_params(("arbitrary",)),
    )(x, tile0, norm_g, dh2, duf, duf, dqkv, dmid, dmid, dmid, dmid, w_main, w_f, *[a for _, a, _ in behind])


def _matmul_tn(name, a, b, tn):
    lp, m = a.shape
    n = b.shape[1]

    def body(a_ref, b_ref, c_ref):
        c_ref[...] = _dot_tn(a_ref[...].astype(BF16), b_ref[...].astype(BF16))

    return pl.pallas_call(
        body, name=name, grid=(n // tn,), out_shape=jax.ShapeDtypeStruct((m, n), F32),
        in_specs=[_const((lp, m)), pl.BlockSpec((lp, tn), lambda j: (0, j))],
        out_specs=pl.BlockSpec((m, tn), lambda j: (0, j)),
        compiler_params=_params(("arbitrary",)),
    )(a, b)


def _matmul_tn_rows(name, a, b, tm):
    lp, m = a.shape
    n = b.shape[1]

    def body(a_ref, b_ref, c_ref):
        c_ref[...] = _dot_tn(a_ref[...].astype(BF16), b_ref[...].astype(BF16))

    return pl.pallas_call(
        body, name=name, grid=(m // tm,), out_shape=jax.ShapeDtypeStruct((m, n), F32),
        in_specs=[pl.BlockSpec((lp, tm), lambda j: (0, j)), _const((lp, n))],
        out_specs=pl.BlockSpec((tm, n), lambda j: (j, 0)),
        compiler_params=_params(("arbitrary",)),
    )(a, b)


def _matmul_tn_stack(name, a, b):
    n_blocks, lp, m = a.shape
    n = b.shape[1]

    def body(a_ref, b_ref, c_ref):
        c_ref[...] = _dot_tn(a_ref[...], b_ref[...])

    return pl.pallas_call(
        body, name=name, grid=(n_blocks,), out_shape=jax.ShapeDtypeStruct((n_blocks * m, n), F32),
        in_specs=[pl.BlockSpec((None, lp, m), lambda j: (j, 0, 0)), _const((lp, n))],
        out_specs=pl.BlockSpec((m, n), lambda j: (j, 0)),
        compiler_params=_params(("arbitrary",)),
    )(a, b)


def _adamw_step(p_ref, w_ref, m_ref, v_ref, g_ref, d_ref, mo_ref, vo_ref):
    g = p_ref[0].astype(F32)
    for s in range(1, p_ref.shape[0]):
        g = g + p_ref[s].astype(F32)
    m_new = ADAM_B1 * m_ref[...] + (1.0 - ADAM_B1) * g
    v_new = ADAM_B2 * v_ref[...] + (1.0 - ADAM_B2) * (g * g)
    m_hat = m_new / (1.0 - ADAM_B1 ** ADAM_STEP)
    v_hat = v_new / (1.0 - ADAM_B2 ** ADAM_STEP)
    g_ref[...] = g
    d_ref[...] = -ADAM_LR * (m_hat / (jnp.sqrt(v_hat) + ADAM_EPS) + ADAM_WD * w_ref[...])
    mo_ref[...] = m_new
    vo_ref[...] = v_new


def _adamw_small(name, groups, loss_parts):
    n = len(groups)

    def body(*refs):
        ins, outs = refs[:4 * n + 1], refs[4 * n + 1:]
        for j in range(n):
            _adamw_step(*ins[4 * j:4 * j + 4], *outs[4 * j:4 * j + 4])
        total = ins[-1][0]
        for s in range(1, N_DEV):
            total = total + ins[-1][s]
        outs[-1][...] = total

    vmem = pl.BlockSpec(memory_space=pltpu.VMEM)
    out_shape = [jax.ShapeDtypeStruct(w.shape, F32) for _, w, _, _ in groups for _ in range(4)]
    out_shape.append(jax.ShapeDtypeStruct(loss_parts.shape[1:], F32))
    res = pl.pallas_call(
        body, name=name, out_shape=out_shape, in_specs=[vmem] * (4 * n + 1), out_specs=[vmem] * (4 * n + 1),
        compiler_params=_params(),
    )(*[a for g in groups for a in g], loss_parts)
    return [res[4 * j:4 * j + 4] for j in range(n)], res[-1]


def _adamw(name, parts, w, m, v, rows, cols=None):
    r, c_all = w.shape
    c = cols or c_all
    n_parts = parts.shape[0]

    def body(p_ref, w_ref, m_ref, v_ref, g_ref, d_ref, mo_ref, vo_ref):
        _adamw_step(p_ref, w_ref, m_ref, v_ref, g_ref, d_ref, mo_ref, vo_ref)

    blk = pl.BlockSpec((rows, c), lambda i, j: (i, j))
    return pl.pallas_call(
        body, name=name, grid=(r // rows, c_all // c), out_shape=[jax.ShapeDtypeStruct((r, c_all), F32)] * 4,
        in_specs=[pl.BlockSpec((n_parts, rows, c), lambda i, j: (0, i, j)), blk, blk, blk],
        out_specs=[blk] * 4,
        compiler_params=_params(("arbitrary", "arbitrary")),
    )(parts, w, m, v)


def _pair_sum(name, mine, theirs, rows):
    n, r, c = mine.shape

    def body(a_ref, b_ref, o_ref):
        o_ref[...] = (a_ref[...].astype(F32) + b_ref[...].astype(F32)).astype(BF16)

    blk = pl.BlockSpec((1, rows, c), lambda j, i: (j, i, 0))
    return pl.pallas_call(
        body, name=name, grid=(n, r // rows), out_shape=jax.ShapeDtypeStruct((n, r, c), BF16),
        in_specs=[blk, blk], out_specs=blk,
        compiler_params=_params(("arbitrary", "arbitrary")),
    )(mine, theirs)


def _by_core(slots):
    by_core = slots.reshape((4, 2) + slots.shape[1:]).swapaxes(0, 1)
    c = lax.axis_index("c")
    return (lax.dynamic_index_in_dim(by_core, c, 0, keepdims=False),
            lax.dynamic_index_in_dim(by_core, 1 - c, 0, keepdims=False))


def _columns_to_slots(a):
    r, c8 = a.shape
    return a.reshape(r, N_DEV, c8 // N_DEV).transpose(1, 0, 2)


def _slots_to_columns(a):
    n, r, c = a.shape
    return a.transpose(1, 0, 2).reshape(r, n * c)


def kernel(x, meta_tokens, norm_g, w_in, b_forget, pool_w, pool_scale, w_up_pool, w_up_attn, w_out, final_norm_g, loss_target, m_meta_tokens, m_norm_g, m_w_in, m_b_forget, m_pool_w, m_pool_scale, m_w_up_pool, m_w_up_attn, m_w_out, m_final_norm_g, v_meta_tokens, v_norm_g, v_w_in, v_b_forget, v_pool_w, v_pool_scale, v_w_up_pool, v_w_up_attn, v_w_out, v_final_norm_g):
    xs = x[0]
    target = loss_target[0]

    g_in, g_upp, g_meta = _gather_two_level(
        "gather_weights", [w_in[0].T.astype(BF16), w_up_pool[0].astype(BF16), meta_tokens], (320, 256, 8))
    w_full = g_in.reshape(N_DEV * g_in.shape[1], D_MODEL)
    w_main = jnp.concatenate([w_full[:N_BEFORE_F], w_full[N_BEFORE_F + N_HEADS:]], axis=0)
    w_f = jnp.pad(w_full[N_BEFORE_F:N_BEFORE_F + N_HEADS], ((0, LANES - N_HEADS), (0, 0)))
    wupp = _slots_to_columns(g_upp)
    meta = _slots_to_columns(g_meta)
    tile0 = jnp.concatenate([jnp.zeros((PAD, D_MODEL), F32), meta], axis=0)
    b_f = jnp.pad(b_forget, ((0, 0), (0, LANES - N_HEADS)))
    pw_b = pool_w[0].astype(BF16)
    final_g = final_norm_g.reshape(1, D_MODEL)

    (h, u, zp, k, v, qt, kt, vt, sneg, a_pool) = _forward_in(xs, tile0, norm_g, w_main, w_f, b_f, pw_b,
                                                              pool_scale, wupp)
    o, lse, g_upa, g_out = _attention_forward(
        qt, k, vt, [("gather", w_up_attn[0].astype(BF16), ALL_PEERS), ("gather", w_out[0].astype(BF16), ALL_PEERS)])
    wupa = _slots_to_columns(g_upa)
    wout = g_out.reshape(D_MODEL, D_MODEL)
    (dh2, mg, yp, ya, dap, daa, do, dmid, dpn,
     loss_part, d_final_g, d_scale, d_pool_w) = _middle(xs, target, h, o, a_pool, u, zp, w_main, wupp, wupa, wout,
                                                        pw_b, pool_scale, final_g)
    dw_out = _matmul_tn("grad_w_out", mg, dh2, 256)
    dw_upp = _matmul_tn("grad_w_up_pool", yp, dap, 512)
    dw_upa = _matmul_tn("grad_w_up_attn", ya, daa, 512)
    dqkv, dc, p_upp, p_upa, p_out, p_pool_w, p_scale, p_final_g = _attention_backward(
        qt, k, kt, v, do, o, lse,
        [("scatter", _columns_to_slots(dw_upp).astype(BF16), ALL_PEERS),
         ("scatter", _columns_to_slots(dw_upa).astype(BF16), ALL_PEERS),
         ("scatter", dw_out.reshape(N_DEV, D_MODEL // N_DEV, D_MODEL).astype(BF16), ALL_PEERS),
         ("gather", d_pool_w.reshape(4 * POOL_GROUP, POOL_GROUP), ALL_PEERS),
         ("gather", d_scale, ALL_PEERS), ("gather", d_final_g, ALL_PEERS)])
    duf, d_bf = _sequence_grads(dpn, dc, sneg)
    g_uf = _matmul_tn_rows("grad_w_in_pool_forget", duf, h, LANES)
    g_qkv = _matmul_tn_stack("grad_w_in_attention", dqkv, h)
    g_mid = _matmul_tn_rows("grad_w_in_gates", dmid, h, 512)
    dw_in = jnp.concatenate([g_uf[:POOL_WIDTH], g_mid[:MID_ZA], g_qkv, g_mid[MID_ZA:MID_GP],
                             g_uf[POOL_WIDTH:POOL_WIDTH + N_HEADS], g_mid[MID_GP:]], axis=0)
    dw_in = dw_in.reshape(N_DEV, dw_in.shape[0] // N_DEV, D_MODEL)
    mine, for_sibling = _by_core(dw_in)
    from_sibling, = _exchange("swap_with_sibling", [("swap", for_sibling.astype(BF16), (SIBLING,))])
    pair_sums = _pair_sum("pair_sum", mine, from_sibling, dw_in.shape[1])
    grad_x, d_meta, d_norm_g, p_in, p_bf, p_loss = _backward_in(
        xs, tile0, norm_g, dh2, duf, dqkv, dmid, w_main, w_f,
        [("chips", pair_sums, SAME_CORE), ("gather", d_bf, ALL_PEERS), ("gather", loss_part, ALL_PEERS)])
    p_meta, p_norm_g = _exchange(
        "exchange_gradients", [("scatter", _columns_to_slots(d_meta), ALL_PEERS), ("gather", d_norm_g, ALL_PEERS)])


    def pad_f(a):
        return jnp.pad(a, ((0, 0), (0, LANES - N_HEADS)))

    res = {}
    res["w_in"] = [a.T for a in _adamw("adamw_w_in", p_in, w_in[0].T, m_w_in[0].T, v_w_in[0].T, p_in.shape[1], 256)]
    res["w_up_pool"] = _adamw("adamw_w_up_pool", p_upp, w_up_pool[0], m_w_up_pool[0], v_w_up_pool[0], 512)
    res["w_up_attn"] = _adamw("adamw_w_up_attn", p_upa, w_up_attn[0], m_w_up_attn[0], v_w_up_attn[0], 512)
    res["w_out"] = _adamw("adamw_w_out", p_out, w_out[0], m_w_out[0], v_w_out[0], 128)
    flat = lambda a: a.reshape(4 * POOL_GROUP, POOL_GROUP)
    row = lambda a: a.reshape(1, D_MODEL)
    small, loss_row = _adamw_small(
        "adamw_small",
        [(p_meta, meta_tokens, m_meta_tokens, v_meta_tokens),
         (p_norm_g, norm_g, m_norm_g, v_norm_g),
         (p_bf, pad_f(b_forget), pad_f(m_b_forget), pad_f(v_b_forget)),
         (p_pool_w, flat(pool_w), flat(m_pool_w), flat(v_pool_w)),
         (p_scale, pool_scale, m_pool_scale, v_pool_scale),
         (p_final_g, final_g, row(m_final_norm_g), row(v_final_norm_g))],
        p_loss)
    res["meta_tokens"], res["norm_g"], bf, pw, res["pool_scale"], fg = small
    res["b_forget"] = [a[:, :N_HEADS] for a in bf]
    res["pool_w"] = [a.reshape(pool_w.shape) for a in pw]
    res["final_norm_g"] = [a.reshape(D_MODEL) for a in fg]
    loss = loss_row[0, 0]
    for name in ("w_in", "w_up_pool", "w_up_attn", "w_out"):
        res[name] = [a[None] for a in res[name]]

    order = ["meta_tokens", "norm_g", "w_in", "b_forget", "pool_w", "pool_scale", "w_up_pool", "w_up_attn", "w_out",
             "final_norm_g"]
    outs = [loss, grad_x[None]]
    for part in range(4):
        outs += [res[name][part] for name in order]
    return tuple(outs)
```

```python
import functools

import jax
import jax.numpy as jnp
from jax import lax
from jax.experimental import pallas as pl
from jax.experimental.pallas import tpu as pltpu

F32 = jnp.float32
BF16 = jnp.bfloat16

D_MODEL = 1024
N_META = 16
POOL_WIDTH = 512
ATTN_WIDTH = 512
N_HEADS = 8
HEAD_DIM = 64
POOL_WINDOWS = (2, 4, 8, 16)
POOL_GROUP = 128
MAX_WINDOW = 16
RMS_EPS = 1e-6
N_MAIN = 5120
N_BEFORE_F = 3072
N_DEV = 8
LANES = 128

ROW_TILE = 256
ATT_TILE = 256
ATT_Q_BLOCKS = 4
PAD = ROW_TILE - N_META
VMEM_LIMIT = 56 * 1024 * 1024

ADAM_LR = 0.001
ADAM_B1 = 0.9
ADAM_B2 = 0.999
ADAM_EPS = 1e-08
ADAM_WD = 0.01
ADAM_STEP = 10

MID_ZA, MID_GP, MID_GA, MID_WIDTH = 512, 1024, 2048, 3072
NEG = -1e30
MESH = pl.DeviceIdType.MESH


def _params(sem=None):
    kw = dict(vmem_limit_bytes=VMEM_LIMIT)
    if sem is not None:
        kw["dimension_semantics"] = sem
    return pltpu.CompilerParams(**kw)


def _const(shape, block_index=None):
    idx = block_index or (0,) * len(shape)
    return pl.BlockSpec(shape, lambda i: idx, pipeline_mode=pl.Buffered(1))


def _sigmoid(x):
    return jax.nn.sigmoid(x)


def _dot(a, b):
    return jnp.dot(a, b, preferred_element_type=F32)


def _dot_nt(a, b):
    return lax.dot_general(a, b, (((1,), (1,)), ((), ())), preferred_element_type=F32)


def _dot_tn(a, b):
    return lax.dot_general(a, b, (((0,), (0,)), ((), ())), preferred_element_type=F32)


def _pool_counts(first_row, rows):
    row = first_row + lax.broadcasted_iota(jnp.int32, (rows, 1), 0)
    pos1 = row - PAD + 1
    return [jnp.clip(pos1, 1, w).astype(F32) for w in POOL_WINDOWS]


def _pool_means(u_ext, u, counts):
    rows = u.shape[0]
    out = []
    for g, w in enumerate(POOL_WINDOWS):
        s = u_ext[:, POOL_GROUP * g:POOL_GROUP * (g + 1)]
        sh = 1
        while sh < w:
            s = s + pltpu.roll(s, sh, axis=0)
            sh *= 2
        out.append(s[MAX_WINDOW:MAX_WINDOW + rows, :] / counts[g] - u[:, POOL_GROUP * g:POOL_GROUP * (g + 1)])
    return out


Q_BIAS, Q_ONES, Q_LSE = 64, 67, 70
K_ONES, K_BIAS, K_ONES2 = 64, 67, 70
V_ONES = 64
DO_BIAS = 64


def _lane_ones(lane, ranges):
    hit = None
    for lo, hi in ranges:
        r = (lane >= lo) & (lane < hi)
        hit = r if hit is None else hit | r
    return jnp.where(hit, 1.0, 0.0)


def _put3(base, lane, first, x):
    hi = x.astype(BF16).astype(F32)
    rest = x - hi
    mid = rest.astype(BF16).astype(F32)
    lo = (rest - mid).astype(BF16).astype(F32)
    for j, piece in enumerate((hi, mid, lo)):
        base = jnp.where(lane == first + j, piece, base)
    return base


SIBLING = 1
SAME_CORE = (2, 4, 6)
ALL_PEERS = (1, 2, 3, 4, 5, 6, 7)


def _place():
    return lax.axis_index("x"), lax.axis_index("y"), lax.axis_index("c")


def _peer(r):
    x, y, c = _place()
    return (1 - x if r & 4 else x, 1 - y if r & 2 else y, 1 - c if r & 1 else c)


def _device_slot(p):
    return 4 * p[0] + 2 * p[1] + p[2]


def _chip_slot(p):
    return 2 * p[0] + p[1]


def _exchange(name, items):
    n = len(items)

    def body(*refs):
        copies = _exchange_copies(items, refs[:n], refs[n:2 * n], *refs[2 * n:])
        for cp in copies:
            cp.start()
        for cp in copies:
            cp.wait()

    hbm = pl.BlockSpec(memory_space=pl.ANY)
    return pl.pallas_call(
        body, name=name, out_shape=_exchange_results(items),
        in_specs=[hbm] * n, out_specs=[hbm] * n,
        scratch_shapes=_exchange_semaphores(n),
    )(*[a for _, a, _ in items])


def _exchange_results(items):
    return [jax.ShapeDtypeStruct(((N_DEV,) if kind == "gather" else ()) + a.shape, a.dtype) for kind, a, _ in items]


def _exchange_semaphores(n):
    return [pltpu.SemaphoreType.DMA((n, N_DEV - 1)), pltpu.SemaphoreType.DMA((n, N_DEV - 1)),
            pltpu.SemaphoreType.DMA((n,))]


def _exchange_copies(items, ins, outs, send_sems, recv_sems, local_sems):
    me = _place()
    copies = []
    for a, (kind, _, peers) in enumerate(items):
        slot = _chip_slot if kind == "chips" else _device_slot
        for r in peers:
            peer = _peer(r)
            src = ins[a] if kind in ("swap", "gather") else ins[a].at[slot(peer)]
            dst = outs[a] if kind == "swap" else outs[a].at[slot(me)]
            copies.append(pltpu.make_async_remote_copy(
                src_ref=src, dst_ref=dst, send_sem=send_sems.at[a, r - 1], recv_sem=recv_sems.at[a, r - 1],
                device_id=peer, device_id_type=MESH))
        if kind != "swap":
            src = ins[a] if kind == "gather" else ins[a].at[slot(me)]
            copies.append(pltpu.make_async_copy(src, outs[a].at[slot(me)], local_sems.at[a]))
    return copies


def _gather_two_level(name, arrays, halves):
    n = len(arrays)
    x_flip, y_flip, both = 4, 2, 6

    def body(*refs):
        ins, outs = refs[:n], refs[n:2 * n]
        send_sems, recv_sems, local_sems = refs[2 * n:]
        me, sibling = _place(), _peer(SIBLING)
        xn, yn, dg = _peer(x_flip), _peer(y_flip), _peer(both)

        def part(a, block, half):
            rows = outs[a].at[_device_slot(block)]
            if half is None:
                return rows
            return rows.at[pl.ds(0, halves[a])] if half == 0 else rows.at[pl.ds(halves[a], arrays[a].shape[0] - halves[a])]

        def copy(a, k, block, half, to, src=None):
            dst = part(a, block, half)
            return pltpu.make_async_remote_copy(
                src_ref=dst if src is None else src, dst_ref=dst,
                send_sem=send_sems.at[a, k], recv_sem=recv_sems.at[a, k], device_id=to, device_id_type=MESH)

        sends, own = [], []

        def start(cp):
            cp.start()
            sends.append(cp)

        for a in range(n):
            mine = pltpu.make_async_copy(ins[a], outs[a].at[_device_slot(me)], local_sems.at[a])
            mine.start()
            own.append(mine)
            for k, to in enumerate((sibling, xn, yn)):
                start(copy(a, k, me, None, to, src=ins[a]))
        for a in range(n):
            copy(a, 1, xn, None, me).wait_recv()
            start(copy(a, 3, xn, 0, yn))
            start(copy(a, 5, xn, None, sibling))
        for a in range(n):
            copy(a, 2, yn, None, me).wait_recv()
            start(copy(a, 4, yn, 1, xn))
            start(copy(a, 6, yn, None, sibling))
        for a in range(n):
            copy(a, 3, dg, 0, me).wait_recv()
            copy(a, 4, dg, 1, me).wait_recv()
            start(copy(a, 7, dg, None, sibling))
        for a in range(n):
            copy(a, 0, sibling, None, me).wait_recv()
            for k, r in ((5, x_flip), (6, y_flip), (7, both)):
                copy(a, k, _peer(r | SIBLING), None, me).wait_recv()
        for cp in sends:
            cp.wait_send()
        for cp in own:
            cp.wait()

    hbm = pl.BlockSpec(memory_space=pl.ANY)
    return pl.pallas_call(
        body, name=name, out_shape=[jax.ShapeDtypeStruct((N_DEV,) + a.shape, a.dtype) for a in arrays],
        in_specs=[hbm] * n, out_specs=[hbm] * n,
        scratch_shapes=[pltpu.SemaphoreType.DMA((n, 8)), pltpu.SemaphoreType.DMA((n, 8)),
                        pltpu.SemaphoreType.DMA((n,))],
    )(*arrays)


def _forward_in(x, tile0, norm_g, w_main, w_f, b_f):
    seq = x.shape[0]
    nt = seq // ROW_TILE + 1
    lp = nt * ROW_TILE
    tm = ROW_TILE

    def body(x_ref, t0_ref, g_ref, wa_ref, wf_ref, bf_ref,
             h_ref, u_ref, zp_ref, k_ref, v_ref, qt_ref, kt_ref, vt_ref, sn_ref, carry_ref):
        i = pl.program_id(0)

        @pl.when(i == 0)
        def _():
            carry_ref[...] = jnp.zeros_like(carry_ref)

        xt = jnp.where(i == 0, t0_ref[...], x_ref[...])
        r = lax.rsqrt(jnp.mean(xt * xt, axis=-1, keepdims=True) + RMS_EPS)
        h = (xt * r * g_ref[...]).astype(BF16)
        h_ref[...] = h
        pa = _dot_nt(h, wa_ref[...])
        u_ref[...] = pa[:, :512]
        zp_ref[...] = pa[:, 512:1024]

        fl = _dot_nt(h, wf_ref[...]) + bf_ref[...]
        row = i * tm + lax.broadcasted_iota(jnp.int32, (tm, LANES), 0)
        rloc = lax.broadcasted_iota(jnp.int32, (tm, LANES), 0)
        lane = lax.broadcasted_iota(jnp.int32, (tm, LANES), 1)
        live = (row >= PAD) & (lane < N_HEADS)
        logf = jnp.minimum(fl, 0.0) - jnp.log1p(jnp.exp(-jnp.abs(fl)))
        cs = jnp.where(live, logf, 0.0)
        sh = 1
        while sh < tm:
            cs = cs + jnp.where(rloc >= sh, pltpu.roll(cs, sh, axis=0), 0.0)
            sh *= 2
        cs = cs + carry_ref[...]
        carry_ref[...] = cs[tm - 1:tm, :]
        sn_ref[...] = jnp.where(live, _sigmoid(-fl), 0.0)

        rows1 = i * tm + lax.broadcasted_iota(jnp.int32, (tm, 1), 0)
        ones_q = _lane_ones(lane, ((Q_ONES, Q_ONES + 3),))
        ones_k = _lane_ones(lane, ((K_ONES, K_ONES + 3), (K_ONES2, K_ONES2 + 3)))
        ones_v = _lane_ones(lane, ((V_ONES, V_ONES + 3),))
        for hp in range(N_HEADS // 2):
            qp = pa[:, 1024 + LANES * hp:1024 + LANES * (hp + 1)] * 0.125
            kp = pa[:, 1536 + LANES * hp:1536 + LANES * (hp + 1)]
            vp = pa[:, 2048 + LANES * hp:2048 + LANES * (hp + 1)]
            for e in range(2):
                head = 2 * hp + e
                if e:
                    qp, kp, vp = (pltpu.roll(a, HEAD_DIM, axis=1) for a in (qp, kp, vp))
                c_h = cs[:, head:head + 1]
                q_h = jnp.where(lane < HEAD_DIM, qp, _put3(ones_q, lane, Q_BIAS, c_h))
                qt_ref[head] = q_h.T.astype(BF16)
                minus_ck = jnp.where(rows1 >= PAD, -c_h, NEG)
                k_h = jnp.where(lane < HEAD_DIM, kp, _put3(ones_k, lane, K_BIAS, minus_ck))
                k_ref[head] = k_h.astype(BF16)
                kt_ref[head] = k_h.T.astype(BF16)
                v_h = jnp.where(lane < HEAD_DIM, vp, ones_v)
                v_ref[head] = v_h.astype(BF16)
                vt_ref[head] = v_h.T.astype(BF16)

    row_f32 = lambda w: pl.BlockSpec((tm, w), lambda i: (i, 0))
    out_shape = [
        jax.ShapeDtypeStruct((lp, D_MODEL), BF16),
        jax.ShapeDtypeStruct((lp, POOL_WIDTH), F32),
        jax.ShapeDtypeStruct((lp, POOL_WIDTH), F32),
        jax.ShapeDtypeStruct((N_HEADS, lp, LANES), BF16),
        jax.ShapeDtypeStruct((N_HEADS, lp, LANES), BF16),
        jax.ShapeDtypeStruct((N_HEADS, LANES, lp), BF16),
        jax.ShapeDtypeStruct((N_HEADS, LANES, lp), BF16),
        jax.ShapeDtypeStruct((N_HEADS, LANES, lp), BF16),
        jax.ShapeDtypeStruct((lp, LANES), F32),
    ]
    heads = pl.BlockSpec((N_HEADS, tm, LANES), lambda i: (0, i, 0))
    heads_t = pl.BlockSpec((N_HEADS, LANES, tm), lambda i: (0, 0, i))
    out_specs = [row_f32(D_MODEL), row_f32(512), row_f32(512), heads, heads, heads_t, heads_t, heads_t,
                 row_f32(LANES)]
    in_specs = [
        pl.BlockSpec((tm, D_MODEL), lambda i: (jnp.maximum(i - 1, 0), 0)),
        _const((tm, D_MODEL)), _const((1, D_MODEL)),
        _const((2560, D_MODEL)), _const((LANES, D_MODEL)), _const((1, LANES)),
    ]
    return pl.pallas_call(
        body, name="forward_in", grid=(nt,), out_shape=out_shape, in_specs=in_specs, out_specs=out_specs,
        scratch_shapes=[pltpu.VMEM((1, LANES), F32)],
        compiler_params=_params(("arbitrary",)),
    )(x, tile0, norm_g, w_main, w_f, b_f)


def _causal(tb):
    return lax.broadcasted_iota(jnp.int32, (tb, tb), 1) <= lax.broadcasted_iota(jnp.int32, (tb, tb), 0)


def _pair_lanes(a0, a1):
    lane = lax.broadcasted_iota(jnp.int32, a0.shape, 1)
    return jnp.where(lane < HEAD_DIM, a0, pltpu.roll(a1, HEAD_DIM, axis=1))


def _behind(items, ins, outs, sems):
    step, last = pl.program_id(0), pl.num_programs(0) - 1

    @pl.when(step == 0)
    def _():
        for cp in _exchange_copies(items, ins, outs, *sems):
            cp.start()

    def finish():
        @pl.when(step == last)
        def _():
            for cp in _exchange_copies(items, ins, outs, *sems):
                cp.wait()

    return finish


def _attention_forward(qt, k, vt, behind):
    lp = k.shape[1]
    tk = ATT_TILE
    tq_big = ATT_Q_BLOCKS * tk
    n_big = (lp // tk - 1) // ATT_Q_BLOCKS
    assert lp == tk + n_big * tq_big and ATT_Q_BLOCKS % 2 == 0
    nx = len(behind)

    def body(qt_ref, k_ref, vt_ref, *rest):
        o_ref, lse_ref = rest[nx:nx + 2]
        s_buf, m_scr, acc_scr = rest[2 * nx + 2:2 * nx + 5]
        finish_exchange = _behind(behind, rest[:nx], rest[nx + 2:2 * nx + 2], rest[2 * nx + 5:])

        def q_tile(q0, tq, pairs):
            first = q0 // tk
            qts = [qt_ref[e, :, pl.ds(q0, tq)] for e in range(2)]

            def block(kj):
                return pl.ds(kj * tk if isinstance(kj, int) else pl.multiple_of(kj * tk, tk), tk)

            def step(kj, rd, wr, c0=0, diagonal=False):
                c1 = c0 + tk if diagonal else c0
                for e in range(2):
                    s = s_buf[rd, e, :, c0:tq]
                    if wr is not None:
                        s_buf[wr, e, :, c1:tq] = _dot(k_ref[e, block(kj + 1), :], qts[e][:, c1:tq])
                    if diagonal:
                        keys = lax.broadcasted_iota(jnp.int32, s.shape, 0)
                        s = jnp.where(keys <= lax.broadcasted_iota(jnp.int32, s.shape, 1), s, NEG)
                    m = m_scr[e, :, c0:tq]
                    m_new = jnp.maximum(m, jnp.max(s, axis=0, keepdims=True))
                    p = jnp.exp(s - m_new)
                    pv = _dot(vt_ref[e, :, block(kj)], p.astype(BF16))
                    acc_scr[e, :, c0:tq] = jnp.exp(m - m_new) * acc_scr[e, :, c0:tq] + pv
                    m_scr[e, :, c0:tq] = m_new

            for e in range(2):
                m_scr[e, :, 0:tq] = jnp.full((1, tq), NEG, F32)
                acc_scr[e, :, 0:tq] = jnp.zeros((LANES, tq), F32)
                s_buf[0, e, :, 0:tq] = _dot(k_ref[e, block(0), :], qts[e])
            if pairs is None:
                step(0, 0, None, 0, True)
            else:
                step(0, 0, 1)

                def two_steps(t, _):
                    step(1 + 2 * t, 1, 0)
                    step(2 + 2 * t, 0, 1)
                    return 0

                lax.fori_loop(0, pairs, two_steps, 0)
                for b in range(tq // tk):
                    step(first + b, (b + 1) % 2, b % 2 if (b + 1) * tk < tq else None, b * tk, True)
            outs, lses = [], []
            for e in range(2):
                acc = acc_scr[e, :, 0:tq]
                l = acc[V_ONES:V_ONES + 1, :]
                outs.append((acc / l).T)
                lses.append(m_scr[e, :, 0:tq] + jnp.log(l))
            o_ref[pl.ds(q0, tq), :] = _pair_lanes(outs[0], outs[1]).astype(BF16)
            lse_rows = jnp.concatenate(lses + [jnp.zeros((LANES - 2, tq), F32)], axis=0)
            lse_ref[pl.ds(q0, tq), :] = lse_rows.T

        q_tile(0, tk, None)

        def big_tile(i, _):
            q_tile(pl.multiple_of(tk + i * tq_big, tk), tq_big, (ATT_Q_BLOCKS // 2) * i)
            return 0

        lax.fori_loop(0, n_big, big_tile, 0)
        finish_exchange()

    pair = pl.BlockSpec((lp, LANES), lambda hp: (0, hp))
    heads = pl.BlockSpec((2, lp, LANES), lambda hp: (hp, 0, 0), pipeline_mode=pl.Buffered(1))
    heads_t = pl.BlockSpec((2, LANES, lp), lambda hp: (hp, 0, 0), pipeline_mode=pl.Buffered(1))
    hbm = pl.BlockSpec(memory_space=pl.ANY)
    return pl.pallas_call(
        body, name="attention_forward", grid=(N_HEADS // 2,),
        out_shape=[jax.ShapeDtypeStruct((lp, ATTN_WIDTH), BF16), jax.ShapeDtypeStruct((lp, ATTN_WIDTH), F32)]
        + _exchange_results(behind),
        in_specs=[heads_t, heads, heads_t] + [hbm] * nx,
        out_specs=[pair, pair] + [hbm] * nx,
        scratch_shapes=[pltpu.VMEM((2, 2, tk, tq_big), F32), pltpu.VMEM((2, 1, tq_big), F32),
                        pltpu.VMEM((2, LANES, tq_big), F32)] + _exchange_semaphores(nx),
        compiler_params=_params(("arbitrary",)),
    )(qt, k, vt, *[a for _, a, _ in behind])


def _rows3(first, x):
    sub = lax.broadcasted_iota(jnp.int32, (LANES, x.shape[1]), 0)
    hi = x.astype(BF16).astype(F32)
    rest = x - hi
    mid = rest.astype(BF16).astype(F32)
    lo = (rest - mid).astype(BF16).astype(F32)
    out = jnp.zeros((LANES, x.shape[1]), F32)
    for j, piece in enumerate((hi, mid, lo)):
        out = jnp.where(sub == first + j, piece, out)
    return out


def _attention_backward(qt, k, kt, v, do, o, lse, behind):
    lp = k.shape[1]
    tb = ATT_TILE
    nb = lp // tb
    tq_big = ATT_Q_BLOCKS * tb
    n_big = (nb - 1) // ATT_Q_BLOCKS
    assert lp == tb + n_big * tq_big and ATT_Q_BLOCKS % 2 == 0
    nx = len(behind)

    def body(qt_ref, k_ref, kt_ref, v_ref, do_ref, o_ref, lse_ref, *rest):
        dqkv_ref, dc_ref = rest[nx:nx + 2]
        q2_ref, do2_ref, dk_acc, dv_acc, dq_scr, s_buf = rest[2 * nx + 2:2 * nx + 8]
        finish_exchange = _behind(behind, rest[:nx], rest[nx + 2:2 * nx + 2], rest[2 * nx + 8:])
        sub = lax.broadcasted_iota(jnp.int32, (LANES, tb), 0)

        def lanes01(row0, row1):
            n = row0.shape[1]
            return jnp.concatenate([row0, row1, jnp.zeros((LANES - 2, n), F32)], axis=0).T

        def prepare(bi, _):
            r0 = pl.multiple_of(bi * tb, tb)
            queries = r0 + lax.broadcasted_iota(jnp.int32, (1, tb), 1)
            dob = do_ref[pl.ds(r0, tb), :].astype(F32)
            do_t = dob.T
            dd_t = (dob * o_ref[pl.ds(r0, tb), :].astype(F32)).T
            lse_t = lse_ref[pl.ds(r0, tb), :].T
            for e in range(2):
                delta = jnp.sum(dd_t[HEAD_DIM * e:HEAD_DIM * (e + 1), :], axis=0, keepdims=True)
                do_e = jnp.concatenate([do_t[HEAD_DIM * e:HEAD_DIM * (e + 1), :], jnp.zeros((HEAD_DIM, tb), F32)], axis=0)
                do2_ref[e, :, pl.ds(r0, tb)] = jnp.where(sub < HEAD_DIM, do_e, _rows3(DO_BIAS, -delta)).astype(BF16)
                minus_lse = jnp.where(queries >= PAD, -lse_t[e:e + 1, :], NEG)
                keep = (sub < Q_LSE) | (sub >= Q_LSE + 3)
                q2_ref[e, :, pl.ds(r0, tb)] = jnp.where(keep, qt_ref[e, :, pl.ds(r0, tb)].astype(F32),
                                                        _rows3(Q_LSE, minus_lse)).astype(BF16)
            return 0

        lax.fori_loop(0, nb, prepare, 0)
        dk_acc[...] = jnp.zeros_like(dk_acc)
        dv_acc[...] = jnp.zeros_like(dv_acc)

        def q_tile(q0, tq, pairs):
            first = q0 // tb
            qts = [q2_ref[e, :, pl.ds(q0, tq)] for e in range(2)]
            dots = [do2_ref[e, :, pl.ds(q0, tq)] for e in range(2)]

            def block(kj):
                return pl.ds(kj * tb if isinstance(kj, int) else pl.multiple_of(kj * tb, tb), tb)

            def step(kj, rd, wr, c0=0, diagonal=False):
                c1 = c0 + tb if diagonal else c0
                for e in range(2):
                    s = s_buf[rd, e, :, c0:tq]
                    if wr is not None:
                        s_buf[wr, e, :, c1:tq] = _dot(k_ref[e, block(kj + 1), :], qts[e][:, c1:tq])
                    dpd = _dot(v_ref[e, block(kj), :], dots[e][:, c0:tq])
                    p = jnp.exp(s)
                    if diagonal:
                        keys = lax.broadcasted_iota(jnp.int32, s.shape, 0)
                        p = jnp.where(keys <= lax.broadcasted_iota(jnp.int32, s.shape, 1), p, 0.0)
                    dsb = (p * dpd).astype(BF16)
                    dv_acc[e, :, block(kj)] += _dot_nt(dots[e][:, c0:tq], p.astype(BF16))
                    dk_acc[e, :, block(kj)] += _dot_nt(qts[e][:, c0:tq], dsb)
                    dq_scr[e, :, c0:tq] += _dot(kt_ref[e, :, block(kj)], dsb)

            for e in range(2):
                dq_scr[e, :, 0:tq] = jnp.zeros((LANES, tq), F32)
                s_buf[0, e, :, 0:tq] = _dot(k_ref[e, block(0), :], qts[e])
            if pairs is None:
                step(0, 0, None, 0, True)
            else:
                step(0, 0, 1)

                def two_steps(t, _):
                    step(1 + 2 * t, 1, 0)
                    step(2 + 2 * t, 0, 1)
                    return 0

                lax.fori_loop(0, pairs, two_steps, 0)
                for b in range(tq // tb):
                    step(first + b, (b + 1) % 2, b % 2 if (b + 1) * tb < tq else None, b * tb, True)
            dq0, dq1 = dq_scr[0, :, 0:tq], dq_scr[1, :, 0:tq]
            dqkv_ref[0, pl.ds(q0, tq), :] = (_pair_lanes(dq0.T, dq1.T) * 0.125).astype(BF16)
            dc_ref[pl.ds(q0, tq), :] = lanes01(dq0[K_ONES:K_ONES + 1, :], dq1[K_ONES:K_ONES + 1, :])

        q_tile(0, tb, None)

        def big_tile(i, _):
            q_tile(pl.multiple_of(tb + i * tq_big, tb), tq_big, (ATT_Q_BLOCKS // 2) * i)
            return 0

        lax.fori_loop(0, n_big, big_tile, 0)

        def finish(bi, _):
            r0 = pl.multiple_of(bi * tb, tb)
            dk0, dk1 = dk_acc[0, :, pl.ds(r0, tb)], dk_acc[1, :, pl.ds(r0, tb)]
            dqkv_ref[1, pl.ds(r0, tb), :] = _pair_lanes(dk0.T, dk1.T).astype(BF16)
            dqkv_ref[2, pl.ds(r0, tb), :] = _pair_lanes(dv_acc[0, :, pl.ds(r0, tb)].T,
                                                        dv_acc[1, :, pl.ds(r0, tb)].T).astype(BF16)
            dc_ref[pl.ds(r0, tb), :] = dc_ref[pl.ds(r0, tb), :] - lanes01(dk0[Q_ONES:Q_ONES + 1, :], dk1[Q_ONES:Q_ONES + 1, :])
            return 0

        lax.fori_loop(0, nb, finish, 0)
        finish_exchange()

    once = pl.Buffered(1)
    pair = pl.BlockSpec((lp, LANES), lambda hp: (0, hp))
    pair_in = pl.BlockSpec((lp, LANES), lambda hp: (0, hp), pipeline_mode=once)
    heads = pl.BlockSpec((2, lp, LANES), lambda hp: (hp, 0, 0), pipeline_mode=once)
    heads_t = pl.BlockSpec((2, LANES, lp), lambda hp: (hp, 0, 0), pipeline_mode=once)
    hbm = pl.BlockSpec(memory_space=pl.ANY)
    return pl.pallas_call(
        body, name="attention_backward", grid=(N_HEADS // 2,),
        out_shape=[jax.ShapeDtypeStruct((3, lp, ATTN_WIDTH), BF16), jax.ShapeDtypeStruct((lp, ATTN_WIDTH), F32)]
        + _exchange_results(behind),
        in_specs=[heads_t, heads, heads_t, heads, pair_in, pair_in, pair_in] + [hbm] * nx,
        out_specs=[pl.BlockSpec((3, lp, LANES), lambda hp: (0, 0, hp)), pair] + [hbm] * nx,
        scratch_shapes=[pltpu.VMEM((2, LANES, lp), BF16), pltpu.VMEM((2, LANES, lp), BF16),
                        pltpu.VMEM((2, LANES, lp), F32), pltpu.VMEM((2, LANES, lp), F32),
                        pltpu.VMEM((2, LANES, tq_big), F32), pltpu.VMEM((2, 2, tb, tq_big), F32)]
        + _exchange_semaphores(nx),
        compiler_params=_params(("arbitrary",)),
    )(qt, k, kt, v, do, o, lse, *[a for _, a, _ in behind])


def _middle(x, target, h, o, u, zp, w_main, w_up_pool, w_up_attn, w_out, pool_w, pool_scale, final_g):
    seq = x.shape[0]
    tm = ROW_TILE
    nt = seq // tm + 1
    lp = nt * tm
    halo_blocks = tm // MAX_WINDOW

    def body(x_ref, t_ref, h_ref, o_ref, u_ref, uh_ref, zp_ref,
             wc_ref, wupp_ref, wupa_ref, wout_ref, pw_ref, sc_ref, gf_ref,
             dh2_ref, mg_ref, yp_ref, ya_ref, dap_ref, daa_ref, do_ref, dmid_ref, dpn_ref,
             loss_ref, dgf_ref, dsc_ref, dpw_ref):
        i = pl.program_id(0)
        tiles = (dh2_ref, mg_ref, yp_ref, ya_ref, dap_ref, daa_ref, do_ref, dmid_ref, dpn_ref)

        @pl.when(i == 0)
        def _():
            for ref in tiles + (loss_ref, dgf_ref, dsc_ref, dpw_ref):
                ref[...] = jnp.zeros_like(ref)

        @pl.when(i > 0)
        def _():
            xt = x_ref[...]
            hb = h_ref[...]
            pc = _dot_nt(hb, wc_ref[...])
            za, gp, ga = pc[:, :512], pc[:, 512:1536], pc[:, 1536:]
            of = o_ref[...].astype(F32)
            sza = _sigmoid(za)
            silu_za = za * sza
            ya = (of * silu_za).astype(BF16)
            ya_ref[...] = ya
            aa = _dot(ya, wupa_ref[...])

            u = u_ref[...]
            zp = zp_ref[...]
            counts = _pool_counts(i * tm, tm)
            ps = _pool_means(jnp.concatenate([uh_ref[...], u], axis=0), u, counts)
            pbs = [p.astype(BF16) for p in ps]
            ppw = jnp.concatenate([_dot(pbs[g], pw_ref[g]) for g in range(4)], axis=1)
            sc = sc_ref[...]
            szp = _sigmoid(zp)
            silu_zp = zp * szp
            ypre = ppw * sc
            yp = (ypre * silu_zp).astype(BF16)
            yp_ref[...] = yp
            ap = _dot(yp, wupp_ref[...])

            sgp, sga = _sigmoid(gp), _sigmoid(ga)
            mg = (sgp * ap + sga * aa).astype(BF16)
            mg_ref[...] = mg
            h2 = xt + _dot(mg, wout_ref[...])
            r2 = lax.rsqrt(jnp.mean(h2 * h2, axis=-1, keepdims=True) + RMS_EPS)
            h2n = h2 * r2
            gf = gf_ref[...]
            diff = h2n * gf - t_ref[...]
            loss_ref[...] += 0.5 * jnp.sum(jnp.mean(diff * diff, axis=-1, keepdims=True), axis=0, keepdims=True)
            dy = diff * (1.0 / D_MODEL)
            dgf_ref[...] += jnp.sum(dy * h2n, axis=0, keepdims=True)
            dyg = dy * gf
            dh2 = r2 * (dyg - h2n * jnp.mean(dyg * h2n, axis=-1, keepdims=True))
            dh2_ref[...] = dh2
            dmg = _dot_nt(dh2.astype(BF16), wout_ref[...])
            dap = (dmg * sgp).astype(BF16)
            daa = (dmg * sga).astype(BF16)
            dap_ref[...] = dap
            daa_ref[...] = daa
            dmid_ref[:, MID_GP:MID_GA] = (dmg * ap * sgp * (1.0 - sgp)).astype(BF16)
            dmid_ref[:, MID_GA:] = (dmg * aa * sga * (1.0 - sga)).astype(BF16)
            dyp = _dot_nt(dap, wupp_ref[...])
            dya = _dot_nt(daa, wupa_ref[...])
            do_ref[...] = (dya * silu_za).astype(BF16)
            dmid_ref[:, MID_ZA:MID_GP] = (dya * of * (sza * (1.0 + za * (1.0 - sza)))).astype(BF16)

            dypre = dyp * silu_zp
            dmid_ref[:, :MID_ZA] = (dyp * ypre * (szp * (1.0 + zp * (1.0 - szp)))).astype(BF16)
            dsc_ref[...] += jnp.sum(dypre * ppw, axis=0, keepdims=True)
            dppw = (dypre * sc).astype(BF16)
            dpns = []
            for g in range(4):
                dg = dppw[:, POOL_GROUP * g:POOL_GROUP * (g + 1)]
                dpw_ref[g] += _dot_tn(pbs[g], dg)
                dpns.append(_dot_nt(dg, pw_ref[g]) / counts[g])
            dpn_ref[...] = jnp.concatenate(dpns, axis=1)

    real = lambda w: pl.BlockSpec((tm, w), lambda i: (jnp.maximum(i - 1, 0), 0))
    row = lambda w: pl.BlockSpec((tm, w), lambda i: (i, 0))
    in_specs = [
        real(D_MODEL), real(D_MODEL), row(D_MODEL), row(512), row(512),
        pl.BlockSpec((MAX_WINDOW, 512), lambda i: (jnp.maximum(i * halo_blocks - 1, 0), 0)), row(512),
        _const((2560, D_MODEL), (1, 0)), _const((POOL_WIDTH, D_MODEL)), _const((ATTN_WIDTH, D_MODEL)),
        _const((D_MODEL, D_MODEL)), _const((4, POOL_GROUP, POOL_GROUP)), _const((1, POOL_WIDTH)), _const((1, D_MODEL)),
    ]
    sd = jax.ShapeDtypeStruct
    out_shape = [
        sd((lp, D_MODEL), F32),
        sd((lp, D_MODEL), BF16),
        sd((lp, 512), BF16),
        sd((lp, 512), BF16),
        sd((lp, D_MODEL), BF16),
        sd((lp, D_MODEL), BF16),
        sd((lp, 512), BF16),
        sd((lp, MID_WIDTH), BF16),
        sd((lp, 512), F32),
        sd((1, LANES), F32),
        sd((1, D_MODEL), F32),
        sd((1, 512), F32),
        sd((4, POOL_GROUP, POOL_GROUP), F32),
    ]
    keep = lambda shape: pl.BlockSpec(shape, lambda i: (0,) * len(shape))
    out_specs = [row(D_MODEL), row(D_MODEL), row(512), row(512), row(D_MODEL), row(D_MODEL), row(512),
                 row(MID_WIDTH), row(512),
                 keep((1, LANES)), keep((1, D_MODEL)), keep((1, 512)), keep((4, POOL_GROUP, POOL_GROUP))]
    return pl.pallas_call(
        body, name="middle", grid=(nt,), out_shape=out_shape, in_specs=in_specs, out_specs=out_specs,
        compiler_params=_params(("arbitrary",)),
    )(x, target, h, o, u, u, zp, w_main, w_up_pool, w_up_attn, w_out, pool_w, pool_scale, final_g)


DUF_WIDTH = POOL_WIDTH + LANES


def _sequence_grads(dpn, dc, sneg):
    lp = dpn.shape[0]
    tm = ROW_TILE
    nt = lp // tm
    halo_blocks = tm // MAX_WINDOW
    last_halo = lp // MAX_WINDOW - 1

    def body(dpn_ref, dpnh_ref, dc_ref, sn_ref, duf_ref, dbf_ref, carry_ref):
        i = pl.program_id(0)
        t = nt - 1 - i

        @pl.when(i == 0)
        def _():
            carry_ref[...] = jnp.zeros_like(carry_ref)
            dbf_ref[...] = jnp.zeros_like(dbf_ref)

        dpn_t = dpn_ref[...]
        ahead = jnp.where(i == 0, jnp.zeros_like(dpnh_ref), dpnh_ref[...])
        ext = jnp.concatenate([dpn_t, ahead], axis=0)
        counts = _pool_counts(t * tm, tm)
        for g, w in enumerate(POOL_WINDOWS):
            s = ext[:, POOL_GROUP * g:POOL_GROUP * (g + 1)]
            sh = 1
            while sh < w:
                s = s + pltpu.roll(s, tm + MAX_WINDOW - sh, axis=0)
                sh *= 2
            du = s[:tm, :] - dpn_t[:, POOL_GROUP * g:POOL_GROUP * (g + 1)] * counts[g]
            duf_ref[:, POOL_GROUP * g:POOL_GROUP * (g + 1)] = du.astype(BF16)

        dct = dc_ref[:, 0:LANES]
        for hp in range(1, N_HEADS // 2):
            dct = dct + pltpu.roll(dc_ref[:, LANES * hp:LANES * (hp + 1)], 2 * hp, axis=1)
        rloc = lax.broadcasted_iota(jnp.int32, (tm, LANES), 0)
        sh = 1
        while sh < tm:
            dct = dct + jnp.where(rloc + sh < tm, pltpu.roll(dct, tm - sh, axis=0), 0.0)
            sh *= 2
        dct = dct + carry_ref[...]
        carry_ref[...] = dct[0:1, :]
        df = dct * sn_ref[...]
        dbf_ref[...] += jnp.sum(df, axis=0, keepdims=True)
        duf_ref[:, POOL_WIDTH:] = df.astype(BF16)

    rev = lambda w: pl.BlockSpec((tm, w), lambda i: (nt - 1 - i, 0))
    return pl.pallas_call(
        body, name="sequence_grads", grid=(nt,),
        out_shape=[jax.ShapeDtypeStruct((lp, DUF_WIDTH), BF16), jax.ShapeDtypeStruct((1, LANES), F32)],
        in_specs=[rev(512),
                  pl.BlockSpec((MAX_WINDOW, 512), lambda i: (jnp.minimum((nt - i) * halo_blocks, last_halo), 0)),
                  rev(512), rev(LANES)],
        out_specs=[rev(DUF_WIDTH), pl.BlockSpec((1, LANES), lambda i: (0, 0))],
        scratch_shapes=[pltpu.VMEM((1, LANES), F32)],
        compiler_params=_params(("arbitrary",)),
    )(dpn, dpn, dc, sneg)


def _backward_in(x, tile0, norm_g, dh2, duf, dqkv, dmid, w_main, w_f, behind):
    seq = x.shape[0]
    tm = ROW_TILE
    nt = seq // tm + 1
    nx = len(behind)

    def body(x_ref, t0_ref, g_ref, dh2_ref, du_ref, df_ref, dqkv_ref, dzp_ref, dza_ref, dgp_ref, dga_ref,
             wm_ref, wf_ref, *rest):
        gx_ref, gmeta_ref, dg_ref = rest[nx:nx + 3]
        dproj_ref = rest[2 * nx + 3]
        finish_exchange = _behind(behind, rest[:nx], rest[nx + 3:2 * nx + 3], rest[2 * nx + 4:])
        t = pl.program_id(0)

        @pl.when(t == 0)
        def _():
            dg_ref[...] = jnp.zeros_like(dg_ref)

        dproj_ref[:, 0:512] = du_ref[...]
        dproj_ref[:, 512:1024] = dzp_ref[...]
        dproj_ref[:, 1024:1536] = dqkv_ref[0]
        dproj_ref[:, 1536:2048] = dqkv_ref[1]
        dproj_ref[:, 2048:2560] = dqkv_ref[2]
        dproj_ref[:, 2560:3072] = dza_ref[...]
        dproj_ref[:, 3072:4096] = dgp_ref[...]
        dproj_ref[:, 4096:5120] = dga_ref[...]
        dh = _dot(dproj_ref[...], wm_ref[...]) + _dot(df_ref[...], wf_ref[...])
        xt = jnp.where(t == 0, t0_ref[...], x_ref[...])
        r = lax.rsqrt(jnp.mean(xt * xt, axis=-1, keepdims=True) + RMS_EPS)
        xn = xt * r
        dg_ref[...] += jnp.sum(dh * xn, axis=0, keepdims=True)
        dhg = dh * g_ref[...]
        dx = dh2_ref[...] + r * (dhg - xn * jnp.mean(dhg * xn, axis=-1, keepdims=True))

        @pl.when(t > 0)
        def _():
            gx_ref[...] = dx

        @pl.when(t == 0)
        def _():
            gmeta_ref[...] = dx[PAD:, :]
            gx_ref[...] = jnp.zeros_like(gx_ref)

        finish_exchange()

    row = lambda w, j=0: pl.BlockSpec((tm, w), lambda i: (i, j))
    real = pl.BlockSpec((tm, D_MODEL), lambda i: (jnp.maximum(i - 1, 0), 0))
    hbm = pl.BlockSpec(memory_space=pl.ANY)
    in_specs = [
        real, _const((tm, D_MODEL)), _const((1, D_MODEL)), row(D_MODEL),
        row(POOL_WIDTH), row(LANES, POOL_WIDTH // LANES), pl.BlockSpec((3, tm, ATTN_WIDTH), lambda i: (0, i, 0)),
        row(512, 0), row(512, 1), row(1024, 1), row(1024, 2),
        _const((N_MAIN, D_MODEL)), _const((LANES, D_MODEL)),
    ] + [hbm] * nx
    sd = jax.ShapeDtypeStruct
    out_shape = [sd((seq, D_MODEL), F32), sd((N_META, D_MODEL), F32), sd((1, D_MODEL), F32)] + _exchange_results(behind)
    keep = lambda shape: pl.BlockSpec(shape, lambda i: (0,) * len(shape))
    out_specs = [real, keep((N_META, D_MODEL)), keep((1, D_MODEL))] + [hbm] * nx
    return pl.pallas_call(
        body, name="backward_in", grid=(nt,), out_shape=out_shape, in_specs=in_specs, out_specs=out_specs,
        scratch_shapes=[pltpu.VMEM((tm, N_MAIN), BF16)] + _exchange_semaphores(nx),
        compiler_params=_params(("arbitrary",)),
    )(x, tile0, norm_g, dh2, duf, duf, dqkv, dmid, dmid, dmid, dmid, w_main, w_f, *[a for _, a, _ in behind])


def _matmul_tn(name, a, b, tn):
    lp, m = a.shape
    n = b.shape[1]

    def body(a_ref, b_ref, c_ref):
        c_ref[...] = _dot_tn(a_ref[...].astype(BF16), b_ref[...].astype(BF16))

    return pl.pallas_call(
        body, name=name, grid=(n // tn,), out_shape=jax.ShapeDtypeStruct((m, n), F32),
        in_specs=[_const((lp, m)), pl.BlockSpec((lp, tn), lambda j: (0, j))],
        out_specs=pl.BlockSpec((m, tn), lambda j: (0, j)),
        compiler_params=_params(("arbitrary",)),
    )(a, b)


def _matmul_tn_rows(name, a, b, tm):
    lp, m = a.shape
    n = b.shape[1]

    def body(a_ref, b_ref, c_ref):
        c_ref[...] = _dot_tn(a_ref[...].astype(BF16), b_ref[...].astype(BF16))

    return pl.pallas_call(
        body, name=name, grid=(m // tm,), out_shape=jax.ShapeDtypeStruct((m, n), F32),
        in_specs=[pl.BlockSpec((lp, tm), lambda j: (0, j)), _const((lp, n))],
        out_specs=pl.BlockSpec((tm, n), lambda j: (j, 0)),
        compiler_params=_params(("arbitrary",)),
    )(a, b)


def _matmul_tn_stack(name, a, b):
    n_blocks, lp, m = a.shape
    n = b.shape[1]

    def body(a_ref, b_ref, c_ref):
        c_ref[...] = _dot_tn(a_ref[...], b_ref[...])

    return pl.pallas_call(
        body, name=name, grid=(n_blocks,), out_shape=jax.ShapeDtypeStruct((n_blocks * m, n), F32),
        in_specs=[pl.BlockSpec((None, lp, m), lambda j: (j, 0, 0)), _const((lp, n))],
        out_specs=pl.BlockSpec((m, n), lambda j: (j, 0)),
        compiler_params=_params(("arbitrary",)),
    )(a, b)


def _adamw_step(p_ref, w_ref, m_ref, v_ref, g_ref, d_ref, mo_ref, vo_ref):
    g = p_ref[0].astype(F32)
    for s in range(1, p_ref.shape[0]):
        g = g + p_ref[s].astype(F32)
    m_new = ADAM_B1 * m_ref[...] + (1.0 - ADAM_B1) * g
    v_new = ADAM_B2 * v_ref[...] + (1.0 - ADAM_B2) * (g * g)
    m_hat = m_new / (1.0 - ADAM_B1 ** ADAM_STEP)
    v_hat = v_new / (1.0 - ADAM_B2 ** ADAM_STEP)
    g_ref[...] = g
    d_ref[...] = -ADAM_LR * (m_hat / (jnp.sqrt(v_hat) + ADAM_EPS) + ADAM_WD * w_ref[...])
    mo_ref[...] = m_new
    vo_ref[...] = v_new


def _adamw_small(name, groups, loss_parts):
    n = len(groups)

    def body(*refs):
        ins, outs = refs[:4 * n + 1], refs[4 * n + 1:]
        for j in range(n):
            _adamw_step(*ins[4 * j:4 * j + 4], *outs[4 * j:4 * j + 4])
        total = ins[-1][0]
        for s in range(1, N_DEV):
            total = total + ins[-1][s]
        outs[-1][...] = total

    vmem = pl.BlockSpec(memory_space=pltpu.VMEM)
    out_shape = [jax.ShapeDtypeStruct(w.shape, F32) for _, w, _, _ in groups for _ in range(4)]
    out_shape.append(jax.ShapeDtypeStruct(loss_parts.shape[1:], F32))
    res = pl.pallas_call(
        body, name=name, out_shape=out_shape, in_specs=[vmem] * (4 * n + 1), out_specs=[vmem] * (4 * n + 1),
        compiler_params=_params(),
    )(*[a for g in groups for a in g], loss_parts)
    return [res[4 * j:4 * j + 4] for j in range(n)], res[-1]


def _adamw(name, parts, w, m, v, rows, cols=None):
    r, c_all = w.shape
    c = cols or c_all
    n_parts = parts.shape[0]

    def body(p_ref, w_ref, m_ref, v_ref, g_ref, d_ref, mo_ref, vo_ref):
        _adamw_step(p_ref, w_ref, m_ref, v_ref, g_ref, d_ref, mo_ref, vo_ref)

    blk = pl.BlockSpec((rows, c), lambda i, j: (i, j))
    return pl.pallas_call(
        body, name=name, grid=(r // rows, c_all // c), out_shape=[jax.ShapeDtypeStruct((r, c_all), F32)] * 4,
        in_specs=[pl.BlockSpec((n_parts, rows, c), lambda i, j: (0, i, j)), blk, blk, blk],
        out_specs=[blk] * 4,
        compiler_params=_params(("arbitrary", "arbitrary")),
    )(parts, w, m, v)


def _pair_sum(name, mine, theirs, rows):
    n, r, c = mine.shape

    def body(a_ref, b_ref, o_ref):
        o_ref[...] = (a_ref[...].astype(F32) + b_ref[...].astype(F32)).astype(BF16)

    blk = pl.BlockSpec((1, rows, c), lambda j, i: (j, i, 0))
    return pl.pallas_call(
        body, name=name, grid=(n, r // rows), out_shape=jax.ShapeDtypeStruct((n, r, c), BF16),
        in_specs=[blk, blk], out_specs=blk,
        compiler_params=_params(("arbitrary", "arbitrary")),
    )(mine, theirs)


def _by_core(slots):
    by_core = slots.reshape((4, 2) + slots.shape[1:]).swapaxes(0, 1)
    c = lax.axis_index("c")
    return (lax.dynamic_index_in_dim(by_core, c, 0, keepdims=False),
            lax.dynamic_index_in_dim(by_core, 1 - c, 0, keepdims=False))


def _columns_to_slots(a):
    r, c8 = a.shape
    return a.reshape(r, N_DEV, c8 // N_DEV).transpose(1, 0, 2)


def _slots_to_columns(a):
    n, r, c = a.shape
    return a.transpose(1, 0, 2).reshape(r, n * c)


def kernel(x, meta_tokens, norm_g, w_in, b_forget, pool_w, pool_scale, w_up_pool, w_up_attn, w_out, final_norm_g, loss_target, m_meta_tokens, m_norm_g, m_w_in, m_b_forget, m_pool_w, m_pool_scale, m_w_up_pool, m_w_up_attn, m_w_out, m_final_norm_g, v_meta_tokens, v_norm_g, v_w_in, v_b_forget, v_pool_w, v_pool_scale, v_w_up_pool, v_w_up_attn, v_w_out, v_final_norm_g):
    xs = x[0]
    target = loss_target[0]

    g_in, g_meta = _gather_two_level("gather_weights", [w_in[0].T.astype(BF16), meta_tokens], (320, 8))
    w_full = g_in.reshape(N_DEV * g_in.shape[1], D_MODEL)
    w_main = jnp.concatenate([w_full[:N_BEFORE_F], w_full[N_BEFORE_F + N_HEADS:]], axis=0)
    w_f = jnp.pad(w_full[N_BEFORE_F:N_BEFORE_F + N_HEADS], ((0, LANES - N_HEADS), (0, 0)))
    meta = _slots_to_columns(g_meta)
    tile0 = jnp.concatenate([jnp.zeros((PAD, D_MODEL), F32), meta], axis=0)
    b_f = jnp.pad(b_forget, ((0, 0), (0, LANES - N_HEADS)))
    pw_b = pool_w[0].astype(BF16)
    final_g = final_norm_g.reshape(1, D_MODEL)

    h, u, zp, k, v, qt, kt, vt, sneg = _forward_in(xs, tile0, norm_g, w_main, w_f, b_f)
    o, lse, g_upp, g_upa, g_out = _attention_forward(
        qt, k, vt, [("gather", w.astype(BF16), ALL_PEERS) for w in (w_up_pool[0], w_up_attn[0], w_out[0])])
    wupp = _slots_to_columns(g_upp)
    wupa = _slots_to_columns(g_upa)
    wout = g_out.reshape(D_MODEL, D_MODEL)
    (dh2, mg, yp, ya, dap, daa, do, dmid, dpn,
     loss_part, d_final_g, d_scale, d_pool_w) = _middle(xs, target, h, o, u, zp, w_main, wupp, wupa, wout,
                                                        pw_b, pool_scale, final_g)
    dw_out = _matmul_tn("grad_w_out", mg, dh2, 256)
    dw_upp = _matmul_tn("grad_w_up_pool", yp, dap, 512)
    dw_upa = _matmul_tn("grad_w_up_attn", ya, daa, 512)
    dqkv, dc, p_upp, p_upa, p_out, p_pool_w, p_scale, p_final_g = _attention_backward(
        qt, k, kt, v, do, o, lse,
        [("scatter", _columns_to_slots(dw_upp).astype(BF16), ALL_PEERS),
         ("scatter", _columns_to_slots(dw_upa).astype(BF16), ALL_PEERS),
         ("scatter", dw_out.reshape(N_DEV, D_MODEL // N_DEV, D_MODEL).astype(BF16), ALL_PEERS),
         ("gather", d_pool_w.reshape(4 * POOL_GROUP, POOL_GROUP), ALL_PEERS),
         ("gather", d_scale, ALL_PEERS), ("gather", d_final_g, ALL_PEERS)])
    duf, d_bf = _sequence_grads(dpn, dc, sneg)
    g_uf = _matmul_tn_rows("grad_w_in_pool_forget", duf, h, DUF_WIDTH)
    g_qkv = _matmul_tn_stack("grad_w_in_attention", dqkv, h)
    g_mid = _matmul_tn_rows("grad_w_in_gates", dmid, h, 512)
    dw_in = jnp.concatenate([g_uf[:POOL_WIDTH], g_mid[:MID_ZA], g_qkv, g_mid[MID_ZA:MID_GP],
                             g_uf[POOL_WIDTH:POOL_WIDTH + N_HEADS], g_mid[MID_GP:]], axis=0)
    dw_in = dw_in.reshape(N_DEV, dw_in.shape[0] // N_DEV, D_MODEL)
    mine, for_sibling = _by_core(dw_in)
    from_sibling, = _exchange("swap_with_sibling", [("swap", for_sibling.astype(BF16), (SIBLING,))])
    pair_sums = _pair_sum("pair_sum", mine, from_sibling, dw_in.shape[1])
    grad_x, d_meta, d_norm_g, p_in, p_bf, p_loss = _backward_in(
        xs, tile0, norm_g, dh2, duf, dqkv, dmid, w_main, w_f,
        [("chips", pair_sums, SAME_CORE), ("gather", d_bf, ALL_PEERS), ("gather", loss_part, ALL_PEERS)])
    p_meta, p_norm_g = _exchange(
        "exchange_gradients", [("scatter", _columns_to_slots(d_meta), ALL_PEERS), ("gather", d_norm_g, ALL_PEERS)])


    def pad_f(a):
        return jnp.pad(a, ((0, 0), (0, LANES - N_HEADS)))

    res = {}
    res["w_in"] = [a.T for a in _adamw("adamw_w_in", p_in, w_in[0].T, m_w_in[0].T, v_w_in[0].T, p_in.shape[1], 256)]
    res["w_up_pool"] = _adamw("adamw_w_up_pool", p_upp, w_up_pool[0], m_w_up_pool[0], v_w_up_pool[0], 512)
    res["w_up_attn"] = _adamw("adamw_w_up_attn", p_upa, w_up_attn[0], m_w_up_attn[0], v_w_up_attn[0], 512)
    res["w_out"] = _adamw("adamw_w_out", p_out, w_out[0], m_w_out[0], v_w_out[0], 128)
    flat = lambda a: a.reshape(4 * POOL_GROUP, POOL_GROUP)
    row = lambda a: a.reshape(1, D_MODEL)
    small, loss_row = _adamw_small(
        "adamw_small",
        [(p_meta, meta_tokens, m_meta_tokens, v_meta_tokens),
         (p_norm_g, norm_g, m_norm_g, v_norm_g),
         (p_bf, pad_f(b_forget), pad_f(m_b_forget), pad_f(v_b_forget)),
         (p_pool_w, flat(pool_w), flat(m_pool_w), flat(v_pool_w)),
         (p_scale, pool_scale, m_pool_scale, v_pool_scale),
         (p_final_g, final_g, row(m_final_norm_g), row(v_final_norm_g))],
        p_loss)
    res["meta_tokens"], res["norm_g"], bf, pw, res["pool_scale"], fg = small
    res["b_forget"] = [a[:, :N_HEADS] for a in bf]
    res["pool_w"] = [a.reshape(pool_w.shape) for a in pw]
    res["final_norm_g"] = [a.reshape(D_MODEL) for a in fg]
    loss = loss_row[0, 0]
    for name in ("w_in", "w_up_pool", "w_up_attn", "w_out"):
        res[name] = [a[None] for a in res[name]]

    order = ["meta_tokens", "norm_g", "w_in", "b_forget", "pool_w", "pool_scale", "w_up_pool", "w_up_attn", "w_out",
             "final_norm_g"]
    outs = [loss, grad_x[None]]
    for part in range(4):
        outs += [res[name][part] for name in order]
    return tuple(outs)
```

```python
import jax
import jax.numpy as jnp
from jax import lax
from jax.experimental import pallas as pl
from jax.experimental.pallas import tpu as pltpu

F32 = jnp.float32
BF16 = jnp.bfloat16

D_MODEL = 1024
N_META = 16
POOL_WIDTH = 512
ATTN_WIDTH = 512
N_HEADS = 8
HEAD_DIM = 64
POOL_WINDOWS = (2, 4, 8, 16)
POOL_GROUP = 128
MAX_WINDOW = 16
RMS_EPS = 1e-6
N_MAIN = 5120
N_BEFORE_F = 3072
N_DEV = 8
LANES = 128

ROW_TILE = 256
ATT_TILE = 256
ATT_Q_BLOCKS_FWD = 8
ATT_Q_BLOCKS_BWD = 4
PAD = ROW_TILE - N_META
VMEM_LIMIT = 56 * 1024 * 1024

ADAM_LR = 0.001
ADAM_B1 = 0.9
ADAM_B2 = 0.999
ADAM_EPS = 1e-08
ADAM_WD = 0.01
ADAM_STEP = 10

MID_ZA, MID_GP, MID_GA, MID_WIDTH = 512, 1024, 2048, 3072
NEG = -1e30
MESH = pl.DeviceIdType.MESH


def _params(sem=None):
    kw = dict(vmem_limit_bytes=VMEM_LIMIT)
    if sem is not None:
        kw["dimension_semantics"] = sem
    return pltpu.CompilerParams(**kw)


def _const(shape, block_index=None):
    idx = block_index or (0,) * len(shape)
    return pl.BlockSpec(shape, lambda i: idx, pipeline_mode=pl.Buffered(1))


def _sigmoid(x):
    return jax.nn.sigmoid(x)


def _dot(a, b):
    return jnp.dot(a, b, preferred_element_type=F32)


def _dot_nt(a, b):
    return lax.dot_general(a, b, (((1,), (1,)), ((), ())), preferred_element_type=F32)


def _dot_tn(a, b):
    return lax.dot_general(a, b, (((0,), (0,)), ((), ())), preferred_element_type=F32)


def _pool_counts(first_row, rows):
    row = first_row + lax.broadcasted_iota(jnp.int32, (rows, 1), 0)
    pos1 = row - PAD + 1
    return [jnp.clip(pos1, 1, w).astype(F32) for w in POOL_WINDOWS]


def _pool_means(u_ext, u, counts):
    rows = u.shape[0]
    out = []
    for g, w in enumerate(POOL_WINDOWS):
        s = u_ext[:, POOL_GROUP * g:POOL_GROUP * (g + 1)]
        sh = 1
        while sh < w:
            s = s + pltpu.roll(s, sh, axis=0)
            sh *= 2
        out.append(s[MAX_WINDOW:MAX_WINDOW + rows, :] / counts[g] - u[:, POOL_GROUP * g:POOL_GROUP * (g + 1)])
    return out


Q_BIAS, Q_ONES, Q_LSE = 64, 67, 70
K_ONES, K_BIAS, K_ONES2 = 64, 67, 70
V_ONES = 64
DO_BIAS = 64


def _lane_ones(lane, ranges):
    hit = None
    for lo, hi in ranges:
        r = (lane >= lo) & (lane < hi)
        hit = r if hit is None else hit | r
    return jnp.where(hit, 1.0, 0.0)


def _put3(base, lane, first, x):
    hi = x.astype(BF16).astype(F32)
    rest = x - hi
    mid = rest.astype(BF16).astype(F32)
    lo = (rest - mid).astype(BF16).astype(F32)
    for j, piece in enumerate((hi, mid, lo)):
        base = jnp.where(lane == first + j, piece, base)
    return base


SIBLING = 1
SAME_CORE = (2, 4, 6)
ALL_PEERS = (1, 2, 3, 4, 5, 6, 7)


def _place():
    return lax.axis_index("x"), lax.axis_index("y"), lax.axis_index("c")


def _peer(r):
    x, y, c = _place()
    return (1 - x if r & 4 else x, 1 - y if r & 2 else y, 1 - c if r & 1 else c)


def _device_slot(p):
    return 4 * p[0] + 2 * p[1] + p[2]


def _chip_slot(p):
    return 2 * p[0] + p[1]


def _exchange(name, items):
    n = len(items)

    def body(*refs):
        copies = _exchange_copies(items, refs[:n], refs[n:2 * n], *refs[2 * n:])
        for cp in copies:
            cp.start()
        for cp in copies:
            cp.wait()

    hbm = pl.BlockSpec(memory_space=pl.ANY)
    return pl.pallas_call(
        body, name=name, out_shape=_exchange_results(items),
        in_specs=[hbm] * n, out_specs=[hbm] * n,
        scratch_shapes=_exchange_semaphores(n),
    )(*[a for _, a, _ in items])


def _exchange_results(items):
    return [jax.ShapeDtypeStruct(((N_DEV,) if kind == "gather" else ()) + a.shape, a.dtype) for kind, a, _ in items]


def _exchange_semaphores(n):
    return [pltpu.SemaphoreType.DMA((n, N_DEV - 1)), pltpu.SemaphoreType.DMA((n, N_DEV - 1)),
            pltpu.SemaphoreType.DMA((n,))]


def _exchange_copies(items, ins, outs, send_sems, recv_sems, local_sems):
    me = _place()
    copies = []
    for a, (kind, _, peers) in enumerate(items):
        slot = _chip_slot if kind == "chips" else _device_slot
        for r in peers:
            peer = _peer(r)
            src = ins[a] if kind in ("swap", "gather") else ins[a].at[slot(peer)]
            dst = outs[a] if kind == "swap" else outs[a].at[slot(me)]
            copies.append(pltpu.make_async_remote_copy(
                src_ref=src, dst_ref=dst, send_sem=send_sems.at[a, r - 1], recv_sem=recv_sems.at[a, r - 1],
                device_id=peer, device_id_type=MESH))
        if kind != "swap":
            src = ins[a] if kind == "gather" else ins[a].at[slot(me)]
            copies.append(pltpu.make_async_copy(src, outs[a].at[slot(me)], local_sems.at[a]))
    return copies


def _gather_two_level(name, arrays, halves):
    n = len(arrays)
    x_flip, y_flip, both = 4, 2, 6

    def body(*refs):
        ins, outs = refs[:n], refs[n:2 * n]
        send_sems, recv_sems, local_sems = refs[2 * n:]
        me, sibling = _place(), _peer(SIBLING)
        xn, yn, dg = _peer(x_flip), _peer(y_flip), _peer(both)

        def part(a, block, half):
            rows = outs[a].at[_device_slot(block)]
            if half is None:
                return rows
            return rows.at[pl.ds(0, halves[a])] if half == 0 else rows.at[pl.ds(halves[a], arrays[a].shape[0] - halves[a])]

        def copy(a, k, block, half, to, src=None):
            dst = part(a, block, half)
            return pltpu.make_async_remote_copy(
                src_ref=dst if src is None else src, dst_ref=dst,
                send_sem=send_sems.at[a, k], recv_sem=recv_sems.at[a, k], device_id=to, device_id_type=MESH)

        sends, own = [], []

        def start(cp):
            cp.start()
            sends.append(cp)

        for a in range(n):
            mine = pltpu.make_async_copy(ins[a], outs[a].at[_device_slot(me)], local_sems.at[a])
            mine.start()
            own.append(mine)
            for k, to in enumerate((sibling, xn, yn)):
                start(copy(a, k, me, None, to, src=ins[a]))
        for a in range(n):
            copy(a, 1, xn, None, me).wait_recv()
            start(copy(a, 3, xn, 0, yn))
            start(copy(a, 5, xn, None, sibling))
        for a in range(n):
            copy(a, 2, yn, None, me).wait_recv()
            start(copy(a, 4, yn, 1, xn))
            start(copy(a, 6, yn, None, sibling))
        for a in range(n):
            copy(a, 3, dg, 0, me).wait_recv()
            copy(a, 4, dg, 1, me).wait_recv()
            start(copy(a, 7, dg, None, sibling))
        for a in range(n):
            copy(a, 0, sibling, None, me).wait_recv()
            for k, r in ((5, x_flip), (6, y_flip), (7, both)):
                copy(a, k, _peer(r | SIBLING), None, me).wait_recv()
        for cp in sends:
            cp.wait_send()
        for cp in own:
            cp.wait()

    hbm = pl.BlockSpec(memory_space=pl.ANY)
    return pl.pallas_call(
        body, name=name, out_shape=[jax.ShapeDtypeStruct((N_DEV,) + a.shape, a.dtype) for a in arrays],
        in_specs=[hbm] * n, out_specs=[hbm] * n,
        scratch_shapes=[pltpu.SemaphoreType.DMA((n, 8)), pltpu.SemaphoreType.DMA((n, 8)),
                        pltpu.SemaphoreType.DMA((n,))],
    )(*arrays)


def _forward_in(x, tile0, norm_g, w_main, w_f, b_f):
    seq = x.shape[0]
    nt = seq // ROW_TILE + 1
    lp = nt * ROW_TILE
    tm = ROW_TILE

    def body(x_ref, t0_ref, g_ref, wa_ref, wf_ref, bf_ref,
             h_ref, u_ref, zp_ref, k_ref, v_ref, qt_ref, kt_ref, vt_ref, sn_ref, carry_ref):
        i = pl.program_id(0)

        @pl.when(i == 0)
        def _():
            carry_ref[...] = jnp.zeros_like(carry_ref)

        xt = jnp.where(i == 0, t0_ref[...], x_ref[...])
        r = lax.rsqrt(jnp.mean(xt * xt, axis=-1, keepdims=True) + RMS_EPS)
        h = (xt * r * g_ref[...]).astype(BF16)
        h_ref[...] = h
        pa = _dot_nt(h, wa_ref[...])
        u_ref[...] = pa[:, :512]
        zp_ref[...] = pa[:, 512:1024]

        fl = _dot_nt(h, wf_ref[...]) + bf_ref[...]
        row = i * tm + lax.broadcasted_iota(jnp.int32, (tm, LANES), 0)
        rloc = lax.broadcasted_iota(jnp.int32, (tm, LANES), 0)
        lane = lax.broadcasted_iota(jnp.int32, (tm, LANES), 1)
        live = (row >= PAD) & (lane < N_HEADS)
        logf = jnp.minimum(fl, 0.0) - jnp.log1p(jnp.exp(-jnp.abs(fl)))
        cs = jnp.where(live, logf, 0.0)
        sh = 1
        while sh < tm:
            cs = cs + jnp.where(rloc >= sh, pltpu.roll(cs, sh, axis=0), 0.0)
            sh *= 2
        cs = cs + carry_ref[...]
        carry_ref[...] = cs[tm - 1:tm, :]
        sn_ref[...] = jnp.where(live, _sigmoid(-fl), 0.0)

        rows1 = i * tm + lax.broadcasted_iota(jnp.int32, (tm, 1), 0)
        ones_q = _lane_ones(lane, ((Q_ONES, Q_ONES + 3),))
        ones_k = _lane_ones(lane, ((K_ONES, K_ONES + 3), (K_ONES2, K_ONES2 + 3)))
        ones_v = _lane_ones(lane, ((V_ONES, V_ONES + 3),))
        for hp in range(N_HEADS // 2):
            qp = pa[:, 1024 + LANES * hp:1024 + LANES * (hp + 1)] * 0.125
            kp = pa[:, 1536 + LANES * hp:1536 + LANES * (hp + 1)]
            vp = pa[:, 2048 + LANES * hp:2048 + LANES * (hp + 1)]
            for e in range(2):
                head = 2 * hp + e
                if e:
                    qp, kp, vp = (pltpu.roll(a, HEAD_DIM, axis=1) for a in (qp, kp, vp))
                c_h = cs[:, head:head + 1]
                q_h = jnp.where(lane < HEAD_DIM, qp, _put3(ones_q, lane, Q_BIAS, c_h))
                qt_ref[head] = q_h.T.astype(BF16)
                minus_ck = jnp.where(rows1 >= PAD, -c_h, NEG)
                k_h = jnp.where(lane < HEAD_DIM, kp, _put3(ones_k, lane, K_BIAS, minus_ck))
                k_ref[head] = k_h.astype(BF16)
                kt_ref[head] = k_h.T.astype(BF16)
                v_h = jnp.where(lane < HEAD_DIM, vp, ones_v)
                v_ref[head] = v_h.astype(BF16)
                vt_ref[head] = v_h.T.astype(BF16)

    row_f32 = lambda w: pl.BlockSpec((tm, w), lambda i: (i, 0))
    out_shape = [
        jax.ShapeDtypeStruct((lp, D_MODEL), BF16),
        jax.ShapeDtypeStruct((lp, POOL_WIDTH), F32),
        jax.ShapeDtypeStruct((lp, POOL_WIDTH), F32),
        jax.ShapeDtypeStruct((N_HEADS, lp, LANES), BF16),
        jax.ShapeDtypeStruct((N_HEADS, lp, LANES), BF16),
        jax.ShapeDtypeStruct((N_HEADS, LANES, lp), BF16),
        jax.ShapeDtypeStruct((N_HEADS, LANES, lp), BF16),
        jax.ShapeDtypeStruct((N_HEADS, LANES, lp), BF16),
        jax.ShapeDtypeStruct((lp, LANES), F32),
    ]
    heads = pl.BlockSpec((N_HEADS, tm, LANES), lambda i: (0, i, 0))
    heads_t = pl.BlockSpec((N_HEADS, LANES, tm), lambda i: (0, 0, i))
    out_specs = [row_f32(D_MODEL), row_f32(512), row_f32(512), heads, heads, heads_t, heads_t, heads_t,
                 row_f32(LANES)]
    in_specs = [
        pl.BlockSpec((tm, D_MODEL), lambda i: (jnp.maximum(i - 1, 0), 0)),
        _const((tm, D_MODEL)), _const((1, D_MODEL)),
        _const((2560, D_MODEL)), _const((LANES, D_MODEL)), _const((1, LANES)),
    ]
    return pl.pallas_call(
        body, name="forward_in", grid=(nt,), out_shape=out_shape, in_specs=in_specs, out_specs=out_specs,
        scratch_shapes=[pltpu.VMEM((1, LANES), F32)],
        compiler_params=_params(("arbitrary",)),
    )(x, tile0, norm_g, w_main, w_f, b_f)


def _pair_lanes(a0, a1):
    lane = lax.broadcasted_iota(jnp.int32, a0.shape, 1)
    return jnp.where(lane < HEAD_DIM, a0, pltpu.roll(a1, HEAD_DIM, axis=1))


def _behind(items, ins, outs, sems):
    step, last = pl.program_id(0), pl.num_programs(0) - 1

    @pl.when(step == 0)
    def _():
        for cp in _exchange_copies(items, ins, outs, *sems):
            cp.start()

    def finish():
        @pl.when(step == last)
        def _():
            for cp in _exchange_copies(items, ins, outs, *sems):
                cp.wait()

    return finish


def _attention_forward(qt, k, vt, behind):
    lp = k.shape[1]
    tk = ATT_TILE
    q_blocks = ATT_Q_BLOCKS_FWD if (lp // tk - 1) % ATT_Q_BLOCKS_FWD == 0 else ATT_Q_BLOCKS_BWD
    tq_big = q_blocks * tk
    n_big = (lp // tk - 1) // q_blocks
    assert lp == tk + n_big * tq_big and q_blocks % 2 == 0
    nx = len(behind)

    def body(qt_ref, k_ref, vt_ref, *rest):
        o_ref, lse_ref = rest[nx:nx + 2]
        s_buf, m_scr, acc_scr = rest[2 * nx + 2:2 * nx + 5]
        finish_exchange = _behind(behind, rest[:nx], rest[nx + 2:2 * nx + 2], rest[2 * nx + 5:])

        def q_tile(q0, tq, pairs):
            first = q0 // tk
            qts = [qt_ref[e, :, pl.ds(q0, tq)] for e in range(2)]

            def block(kj):
                return pl.ds(kj * tk if isinstance(kj, int) else pl.multiple_of(kj * tk, tk), tk)

            def step(kj, rd, wr, c0=0, diagonal=False):
                c1 = c0 + tk if diagonal else c0
                for e in range(2):
                    s = s_buf[rd, e, :, c0:tq]
                    if wr is not None:
                        s_buf[wr, e, :, c1:tq] = _dot(k_ref[e, block(kj + 1), :], qts[e][:, c1:tq])
                    if diagonal:
                        keys = lax.broadcasted_iota(jnp.int32, s.shape, 0)
                        s = jnp.where(keys <= lax.broadcasted_iota(jnp.int32, s.shape, 1), s, NEG)
                    m = m_scr[e, :, c0:tq]
                    m_new = jnp.maximum(m, jnp.max(s, axis=0, keepdims=True))
                    p = jnp.exp(s - m_new)
                    pv = _dot(vt_ref[e, :, block(kj)], p.astype(BF16))
                    acc_scr[e, :, c0:tq] = jnp.exp(m - m_new) * acc_scr[e, :, c0:tq] + pv
                    m_scr[e, :, c0:tq] = m_new

            for e in range(2):
                m_scr[e, :, 0:tq] = jnp.full((1, tq), NEG, F32)
                acc_scr[e, :, 0:tq] = jnp.zeros((LANES, tq), F32)
                s_buf[0, e, :, 0:tq] = _dot(k_ref[e, block(0), :], qts[e])
            if pairs is None:
                step(0, 0, None, 0, True)
            else:
                step(0, 0, 1)

                def two_steps(t, _):
                    step(1 + 2 * t, 1, 0)
                    step(2 + 2 * t, 0, 1)
                    return 0

                lax.fori_loop(0, pairs, two_steps, 0)
                for b in range(tq // tk):
                    step(first + b, (b + 1) % 2, b % 2 if (b + 1) * tk < tq else None, b * tk, True)
            outs, lses = [], []
            for e in range(2):
                acc = acc_scr[e, :, 0:tq]
                l = acc[V_ONES:V_ONES + 1, :]
                outs.append((acc / l).T)
                lses.append(m_scr[e, :, 0:tq] + jnp.log(l))
            o_ref[pl.ds(q0, tq), :] = _pair_lanes(outs[0], outs[1]).astype(BF16)
            lse_rows = jnp.concatenate(lses + [jnp.zeros((LANES - 2, tq), F32)], axis=0)
            lse_ref[pl.ds(q0, tq), :] = lse_rows.T

        q_tile(0, tk, None)

        def big_tile(i, _):
            q_tile(pl.multiple_of(tk + i * tq_big, tk), tq_big, (q_blocks // 2) * i)
            return 0

        lax.fori_loop(0, n_big, big_tile, 0)
        finish_exchange()

    pair = pl.BlockSpec((lp, LANES), lambda hp: (0, hp))
    heads = pl.BlockSpec((2, lp, LANES), lambda hp: (hp, 0, 0), pipeline_mode=pl.Buffered(1))
    heads_t = pl.BlockSpec((2, LANES, lp), lambda hp: (hp, 0, 0), pipeline_mode=pl.Buffered(1))
    hbm = pl.BlockSpec(memory_space=pl.ANY)
    return pl.pallas_call(
        body, name="attention_forward", grid=(N_HEADS // 2,),
        out_shape=[jax.ShapeDtypeStruct((lp, ATTN_WIDTH), BF16), jax.ShapeDtypeStruct((lp, ATTN_WIDTH), F32)]
        + _exchange_results(behind),
        in_specs=[heads_t, heads, heads_t] + [hbm] * nx,
        out_specs=[pair, pair] + [hbm] * nx,
        scratch_shapes=[pltpu.VMEM((2, 2, tk, tq_big), F32), pltpu.VMEM((2, 1, tq_big), F32),
                        pltpu.VMEM((2, LANES, tq_big), F32)] + _exchange_semaphores(nx),
        compiler_params=_params(("arbitrary",)),
    )(qt, k, vt, *[a for _, a, _ in behind])


def _rows3(first, x):
    sub = lax.broadcasted_iota(jnp.int32, (LANES, x.shape[1]), 0)
    hi = x.astype(BF16).astype(F32)
    rest = x - hi
    mid = rest.astype(BF16).astype(F32)
    lo = (rest - mid).astype(BF16).astype(F32)
    out = jnp.zeros((LANES, x.shape[1]), F32)
    for j, piece in enumerate((hi, mid, lo)):
        out = jnp.where(sub == first + j, piece, out)
    return out


def _attention_backward(qt, k, kt, v, do, o, lse, behind):
    lp = k.shape[1]
    tb = ATT_TILE
    nb = lp // tb
    tq_big = ATT_Q_BLOCKS_BWD * tb
    n_big = (nb - 1) // ATT_Q_BLOCKS_BWD
    assert lp == tb + n_big * tq_big and ATT_Q_BLOCKS_BWD % 2 == 0
    nx = len(behind)

    def body(qt_ref, k_ref, kt_ref, v_ref, do_ref, o_ref, lse_ref, *rest):
        dqkv_ref, dc_ref = rest[nx:nx + 2]
        q2_ref, do2_ref, dk_acc, dv_acc, dq_scr, s_buf = rest[2 * nx + 2:2 * nx + 8]
        finish_exchange = _behind(behind, rest[:nx], rest[nx + 2:2 * nx + 2], rest[2 * nx + 8:])
        sub = lax.broadcasted_iota(jnp.int32, (LANES, tb), 0)

        def lanes01(row0, row1):
            n = row0.shape[1]
            return jnp.concatenate([row0, row1, jnp.zeros((LANES - 2, n), F32)], axis=0).T

        def prepare(bi, _):
            r0 = pl.multiple_of(bi * tb, tb)
            queries = r0 + lax.broadcasted_iota(jnp.int32, (1, tb), 1)
            dob = do_ref[pl.ds(r0, tb), :].astype(F32)
            do_t = dob.T
            dd_t = (dob * o_ref[pl.ds(r0, tb), :].astype(F32)).T
            lse_t = lse_ref[pl.ds(r0, tb), :].T
            for e in range(2):
                delta = jnp.sum(dd_t[HEAD_DIM * e:HEAD_DIM * (e + 1), :], axis=0, keepdims=True)
                do_e = jnp.concatenate([do_t[HEAD_DIM * e:HEAD_DIM * (e + 1), :], jnp.zeros((HEAD_DIM, tb), F32)], axis=0)
                do2_ref[e, :, pl.ds(r0, tb)] = jnp.where(sub < HEAD_DIM, do_e, _rows3(DO_BIAS, -delta)).astype(BF16)
                minus_lse = jnp.where(queries >= PAD, -lse_t[e:e + 1, :], NEG)
                keep = (sub < Q_LSE) | (sub >= Q_LSE + 3)
                q2_ref[e, :, pl.ds(r0, tb)] = jnp.where(keep, qt_ref[e, :, pl.ds(r0, tb)].astype(F32),
                                                        _rows3(Q_LSE, minus_lse)).astype(BF16)
            return 0

        lax.fori_loop(0, nb, prepare, 0)
        dk_acc[...] = jnp.zeros_like(dk_acc)
        dv_acc[...] = jnp.zeros_like(dv_acc)

        def q_tile(q0, tq, pairs):
            first = q0 // tb
            qts = [q2_ref[e, :, pl.ds(q0, tq)] for e in range(2)]
            dots = [do2_ref[e, :, pl.ds(q0, tq)] for e in range(2)]

            def block(kj):
                return pl.ds(kj * tb if isinstance(kj, int) else pl.multiple_of(kj * tb, tb), tb)

            def step(kj, rd, wr, c0=0, diagonal=False):
                c1 = c0 + tb if diagonal else c0
                for e in range(2):
                    s = s_buf[rd, e, :, c0:tq]
                    if wr is not None:
                        s_buf[wr, e, :, c1:tq] = _dot(k_ref[e, block(kj + 1), :], qts[e][:, c1:tq])
                    dpd = _dot(v_ref[e, block(kj), :], dots[e][:, c0:tq])
                    p = jnp.exp(s)
                    if diagonal:
                        keys = lax.broadcasted_iota(jnp.int32, s.shape, 0)
                        p = jnp.where(keys <= lax.broadcasted_iota(jnp.int32, s.shape, 1), p, 0.0)
                    dsb = (p * dpd).astype(BF16)
                    dv_acc[e, :, block(kj)] += _dot_nt(dots[e][:, c0:tq], p.astype(BF16))
                    dk_acc[e, :, block(kj)] += _dot_nt(qts[e][:, c0:tq], dsb)
                    dq_scr[e, :, c0:tq] += _dot(kt_ref[e, :, block(kj)], dsb)

            for e in range(2):
                dq_scr[e, :, 0:tq] = jnp.zeros((LANES, tq), F32)
                s_buf[0, e, :, 0:tq] = _dot(k_ref[e, block(0), :], qts[e])
            if pairs is None:
                step(0, 0, None, 0, True)
            else:
                step(0, 0, 1)

                def two_steps(t, _):
                    step(1 + 2 * t, 1, 0)
                    step(2 + 2 * t, 0, 1)
                    return 0

                lax.fori_loop(0, pairs, two_steps, 0)
                for b in range(tq // tb):
                    step(first + b, (b + 1) % 2, b % 2 if (b + 1) * tb < tq else None, b * tb, True)
            dq0, dq1 = dq_scr[0, :, 0:tq], dq_scr[1, :, 0:tq]
            dqkv_ref[0, pl.ds(q0, tq), :] = (_pair_lanes(dq0.T, dq1.T) * 0.125).astype(BF16)
            dc_ref[pl.ds(q0, tq), :] = lanes01(dq0[K_ONES:K_ONES + 1, :], dq1[K_ONES:K_ONES + 1, :])

        q_tile(0, tb, None)

        def big_tile(i, _):
            q_tile(pl.multiple_of(tb + i * tq_big, tb), tq_big, (ATT_Q_BLOCKS_BWD // 2) * i)
            return 0

        lax.fori_loop(0, n_big, big_tile, 0)

        def finish(bi, _):
            r0 = pl.multiple_of(bi * tb, tb)
            dk0, dk1 = dk_acc[0, :, pl.ds(r0, tb)], dk_acc[1, :, pl.ds(r0, tb)]
            dqkv_ref[1, pl.ds(r0, tb), :] = _pair_lanes(dk0.T, dk1.T).astype(BF16)
            dqkv_ref[2, pl.ds(r0, tb), :] = _pair_lanes(dv_acc[0, :, pl.ds(r0, tb)].T,
                                                        dv_acc[1, :, pl.ds(r0, tb)].T).astype(BF16)
            dc_ref[pl.ds(r0, tb), :] = dc_ref[pl.ds(r0, tb), :] - lanes01(dk0[Q_ONES:Q_ONES + 1, :], dk1[Q_ONES:Q_ONES + 1, :])
            return 0

        lax.fori_loop(0, nb, finish, 0)
        finish_exchange()

    once = pl.Buffered(1)
    pair = pl.BlockSpec((lp, LANES), lambda hp: (0, hp))
    pair_in = pl.BlockSpec((lp, LANES), lambda hp: (0, hp), pipeline_mode=once)
    heads = pl.BlockSpec((2, lp, LANES), lambda hp: (hp, 0, 0), pipeline_mode=once)
    heads_t = pl.BlockSpec((2, LANES, lp), lambda hp: (hp, 0, 0), pipeline_mode=once)
    hbm = pl.BlockSpec(memory_space=pl.ANY)
    return pl.pallas_call(
        body, name="attention_backward", grid=(N_HEADS // 2,),
        out_shape=[jax.ShapeDtypeStruct((3, lp, ATTN_WIDTH), BF16), jax.ShapeDtypeStruct((lp, ATTN_WIDTH), F32)]
        + _exchange_results(behind),
        in_specs=[heads_t, heads, heads_t, heads, pair_in, pair_in, pair_in] + [hbm] * nx,
        out_specs=[pl.BlockSpec((3, lp, LANES), lambda hp: (0, 0, hp)), pair] + [hbm] * nx,
        scratch_shapes=[pltpu.VMEM((2, LANES, lp), BF16), pltpu.VMEM((2, LANES, lp), BF16),
                        pltpu.VMEM((2, LANES, lp), F32), pltpu.VMEM((2, LANES, lp), F32),
                        pltpu.VMEM((2, LANES, tq_big), F32), pltpu.VMEM((2, 2, tb, tq_big), F32)]
        + _exchange_semaphores(nx),
        compiler_params=_params(("arbitrary",)),
    )(qt, k, kt, v, do, o, lse, *[a for _, a, _ in behind])


def _middle(x, target, h, o, u, zp, w_main, w_up_pool, w_up_attn, w_out, pool_w, pool_scale, final_g):
    seq = x.shape[0]
    tm = ROW_TILE
    nt = seq // tm + 1
    lp = nt * tm
    halo_blocks = tm // MAX_WINDOW

    def body(x_ref, t_ref, h_ref, o_ref, u_ref, uh_ref, zp_ref,
             wc_ref, wupp_ref, wupa_ref, wout_ref, pw_ref, sc_ref, gf_ref,
             dh2_ref, mg_ref, yp_ref, ya_ref, dap_ref, daa_ref, do_ref, dmid_ref, dpn_ref,
             loss_ref, dgf_ref, dsc_ref, dpw_ref):
        i = pl.program_id(0)
        tiles = (dh2_ref, mg_ref, yp_ref, ya_ref, dap_ref, daa_ref, do_ref, dmid_ref, dpn_ref)

        @pl.when(i == 0)
        def _():
            for ref in tiles + (loss_ref, dgf_ref, dsc_ref, dpw_ref):
                ref[...] = jnp.zeros_like(ref)

        @pl.when(i > 0)
        def _():
            xt = x_ref[...]
            hb = h_ref[...]
            pc = _dot_nt(hb, wc_ref[...])
            za, gp, ga = pc[:, :512], pc[:, 512:1536], pc[:, 1536:]
            of = o_ref[...].astype(F32)
            sza = _sigmoid(za)
            silu_za = za * sza
            ya = (of * silu_za).astype(BF16)
            ya_ref[...] = ya
            aa = _dot(ya, wupa_ref[...])

            u = u_ref[...]
            zp = zp_ref[...]
            counts = _pool_counts(i * tm, tm)
            ps = _pool_means(jnp.concatenate([uh_ref[...], u], axis=0), u, counts)
            pbs = [p.astype(BF16) for p in ps]
            ppw = jnp.concatenate([_dot(pbs[g], pw_ref[g]) for g in range(4)], axis=1)
            sc = sc_ref[...]
            szp = _sigmoid(zp)
            silu_zp = zp * szp
            ypre = ppw * sc
            yp = (ypre * silu_zp).astype(BF16)
            yp_ref[...] = yp
            ap = _dot(yp, wupp_ref[...])

            sgp, sga = _sigmoid(gp), _sigmoid(ga)
            mg = (sgp * ap + sga * aa).astype(BF16)
            mg_ref[...] = mg
            h2 = xt + _dot(mg, wout_ref[...])
            r2 = lax.rsqrt(jnp.mean(h2 * h2, axis=-1, keepdims=True) + RMS_EPS)
            h2n = h2 * r2
            gf = gf_ref[...]
            diff = h2n * gf - t_ref[...]
            loss_ref[...] += 0.5 * jnp.sum(jnp.mean(diff * diff, axis=-1, keepdims=True), axis=0, keepdims=True)
            dy = diff * (1.0 / D_MODEL)
            dgf_ref[...] += jnp.sum(dy * h2n, axis=0, keepdims=True)
            dyg = dy * gf
            dh2 = r2 * (dyg - h2n * jnp.mean(dyg * h2n, axis=-1, keepdims=True))
            dh2_ref[...] = dh2
            dmg = _dot_nt(dh2.astype(BF16), wout_ref[...])
            dap = (dmg * sgp).astype(BF16)
            daa = (dmg * sga).astype(BF16)
            dap_ref[...] = dap
            daa_ref[...] = daa
            dmid_ref[:, MID_GP:MID_GA] = (dmg * ap * sgp * (1.0 - sgp)).astype(BF16)
            dmid_ref[:, MID_GA:] = (dmg * aa * sga * (1.0 - sga)).astype(BF16)
            dyp = _dot_nt(dap, wupp_ref[...])
            dya = _dot_nt(daa, wupa_ref[...])
            do_ref[...] = (dya * silu_za).astype(BF16)
            dmid_ref[:, MID_ZA:MID_GP] = (dya * of * (sza * (1.0 + za * (1.0 - sza)))).astype(BF16)

            dypre = dyp * silu_zp
            dmid_ref[:, :MID_ZA] = (dyp * ypre * (szp * (1.0 + zp * (1.0 - szp)))).astype(BF16)
            dsc_ref[...] += jnp.sum(dypre * ppw, axis=0, keepdims=True)
            dppw = (dypre * sc).astype(BF16)
            dpns = []
            for g in range(4):
                dg = dppw[:, POOL_GROUP * g:POOL_GROUP * (g + 1)]
                dpw_ref[g] += _dot_tn(pbs[g], dg)
                dpns.append(_dot_nt(dg, pw_ref[g]) / counts[g])
            dpn_ref[...] = jnp.concatenate(dpns, axis=1)

    real = lambda w: pl.BlockSpec((tm, w), lambda i: (jnp.maximum(i - 1, 0), 0))
    row = lambda w: pl.BlockSpec((tm, w), lambda i: (i, 0))
    in_specs = [
        real(D_MODEL), real(D_MODEL), row(D_MODEL), row(512), row(512),
        pl.BlockSpec((MAX_WINDOW, 512), lambda i: (jnp.maximum(i * halo_blocks - 1, 0), 0)), row(512),
        _const((2560, D_MODEL), (1, 0)), _const((POOL_WIDTH, D_MODEL)), _const((ATTN_WIDTH, D_MODEL)),
        _const((D_MODEL, D_MODEL)), _const((4, POOL_GROUP, POOL_GROUP)), _const((1, POOL_WIDTH)), _const((1, D_MODEL)),
    ]
    sd = jax.ShapeDtypeStruct
    out_shape = [
        sd((lp, D_MODEL), F32),
        sd((lp, D_MODEL), BF16),
        sd((lp, 512), BF16),
        sd((lp, 512), BF16),
        sd((lp, D_MODEL), BF16),
        sd((lp, D_MODEL), BF16),
        sd((lp, 512), BF16),
        sd((lp, MID_WIDTH), BF16),
        sd((lp, 512), F32),
        sd((1, LANES), F32),
        sd((1, D_MODEL), F32),
        sd((1, 512), F32),
        sd((4, POOL_GROUP, POOL_GROUP), F32),
    ]
    keep = lambda shape: pl.BlockSpec(shape, lambda i: (0,) * len(shape))
    out_specs = [row(D_MODEL), row(D_MODEL), row(512), row(512), row(D_MODEL), row(D_MODEL), row(512),
                 row(MID_WIDTH), row(512),
                 keep((1, LANES)), keep((1, D_MODEL)), keep((1, 512)), keep((4, POOL_GROUP, POOL_GROUP))]
    return pl.pallas_call(
        body, name="middle", grid=(nt,), out_shape=out_shape, in_specs=in_specs, out_specs=out_specs,
        compiler_params=_params(("arbitrary",)),
    )(x, target, h, o, u, u, zp, w_main, w_up_pool, w_up_attn, w_out, pool_w, pool_scale, final_g)


DUF_WIDTH = POOL_WIDTH + LANES


def _sequence_grads(dpn, dc, sneg):
    lp = dpn.shape[0]
    tm = ROW_TILE
    nt = lp // tm
    halo_blocks = tm // MAX_WINDOW
    last_halo = lp // MAX_WINDOW - 1

    def body(dpn_ref, dpnh_ref, dc_ref, sn_ref, duf_ref, dbf_ref, carry_ref):
        i = pl.program_id(0)
        t = nt - 1 - i

        @pl.when(i == 0)
        def _():
            carry_ref[...] = jnp.zeros_like(carry_ref)
            dbf_ref[...] = jnp.zeros_like(dbf_ref)

        dpn_t = dpn_ref[...]
        ahead = jnp.where(i == 0, jnp.zeros_like(dpnh_ref), dpnh_ref[...])
        ext = jnp.concatenate([dpn_t, ahead], axis=0)
        counts = _pool_counts(t * tm, tm)
        for g, w in enumerate(POOL_WINDOWS):
            s = ext[:, POOL_GROUP * g:POOL_GROUP * (g + 1)]
            sh = 1
            while sh < w:
                s = s + pltpu.roll(s, tm + MAX_WINDOW - sh, axis=0)
                sh *= 2
            du = s[:tm, :] - dpn_t[:, POOL_GROUP * g:POOL_GROUP * (g + 1)] * counts[g]
            duf_ref[:, POOL_GROUP * g:POOL_GROUP * (g + 1)] = du.astype(BF16)

        dct = dc_ref[:, 0:LANES]
        for hp in range(1, N_HEADS // 2):
            dct = dct + pltpu.roll(dc_ref[:, LANES * hp:LANES * (hp + 1)], 2 * hp, axis=1)
        rloc = lax.broadcasted_iota(jnp.int32, (tm, LANES), 0)
        sh = 1
        while sh < tm:
            dct = dct + jnp.where(rloc + sh < tm, pltpu.roll(dct, tm - sh, axis=0), 0.0)
            sh *= 2
        dct = dct + carry_ref[...]
        carry_ref[...] = dct[0:1, :]
        df = dct * sn_ref[...]
        dbf_ref[...] += jnp.sum(df, axis=0, keepdims=True)
        duf_ref[:, POOL_WIDTH:] = df.astype(BF16)

    rev = lambda w: pl.BlockSpec((tm, w), lambda i: (nt - 1 - i, 0))
    return pl.pallas_call(
        body, name="sequence_grads", grid=(nt,),
        out_shape=[jax.ShapeDtypeStruct((lp, DUF_WIDTH), BF16), jax.ShapeDtypeStruct((1, LANES), F32)],
        in_specs=[rev(512),
                  pl.BlockSpec((MAX_WINDOW, 512), lambda i: (jnp.minimum((nt - i) * halo_blocks, last_halo), 0)),
                  rev(512), rev(LANES)],
        out_specs=[rev(DUF_WIDTH), pl.BlockSpec((1, LANES), lambda i: (0, 0))],
        scratch_shapes=[pltpu.VMEM((1, LANES), F32)],
        compiler_params=_params(("arbitrary",)),
    )(dpn, dpn, dc, sneg)


def _backward_in(x, tile0, norm_g, dh2, duf, dqkv, dmid, w_main, w_f, behind):
    seq = x.shape[0]
    tm = ROW_TILE
    nt = seq // tm + 1
    nx = len(behind)

    def body(x_ref, t0_ref, g_ref, dh2_ref, du_ref, df_ref, dqkv_ref, dzp_ref, dza_ref, dgp_ref, dga_ref,
             wm_ref, wf_ref, *rest):
        gx_ref, gmeta_ref, dg_ref = rest[nx:nx + 3]
        dproj_ref = rest[2 * nx + 3]
        finish_exchange = _behind(behind, rest[:nx], rest[nx + 3:2 * nx + 3], rest[2 * nx + 4:])
        t = pl.program_id(0)

        @pl.when(t == 0)
        def _():
            dg_ref[...] = jnp.zeros_like(dg_ref)

        dproj_ref[:, 0:512] = du_ref[...]
        dproj_ref[:, 512:1024] = dzp_ref[...]
        dproj_ref[:, 1024:1536] = dqkv_ref[0]
        dproj_ref[:, 1536:2048] = dqkv_ref[1]
        dproj_ref[:, 2048:2560] = dqkv_ref[2]
        dproj_ref[:, 2560:3072] = dza_ref[...]
        dproj_ref[:, 3072:4096] = dgp_ref[...]
        dproj_ref[:, 4096:5120] = dga_ref[...]
        dh = _dot(dproj_ref[...], wm_ref[...]) + _dot(df_ref[...], wf_ref[...])
        xt = jnp.where(t == 0, t0_ref[...], x_ref[...])
        r = lax.rsqrt(jnp.mean(xt * xt, axis=-1, keepdims=True) + RMS_EPS)
        xn = xt * r
        dg_ref[...] += jnp.sum(dh * xn, axis=0, keepdims=True)
        dhg = dh * g_ref[...]
        dx = dh2_ref[...] + r * (dhg - xn * jnp.mean(dhg * xn, axis=-1, keepdims=True))

        @pl.when(t > 0)
        def _():
            gx_ref[...] = dx

        @pl.when(t == 0)
        def _():
            gmeta_ref[...] = dx[PAD:, :]
            gx_ref[...] = jnp.zeros_like(gx_ref)

        finish_exchange()

    row = lambda w, j=0: pl.BlockSpec((tm, w), lambda i: (i, j))
    real = pl.BlockSpec((tm, D_MODEL), lambda i: (jnp.maximum(i - 1, 0), 0))
    hbm = pl.BlockSpec(memory_space=pl.ANY)
    in_specs = [
        real, _const((tm, D_MODEL)), _const((1, D_MODEL)), row(D_MODEL),
        row(POOL_WIDTH), row(LANES, POOL_WIDTH // LANES), pl.BlockSpec((3, tm, ATTN_WIDTH), lambda i: (0, i, 0)),
        row(512, 0), row(512, 1), row(1024, 1), row(1024, 2),
        _const((N_MAIN, D_MODEL)), _const((LANES, D_MODEL)),
    ] + [hbm] * nx
    sd = jax.ShapeDtypeStruct
    out_shape = [sd((seq, D_MODEL), F32), sd((N_META, D_MODEL), F32), sd((1, D_MODEL), F32)] + _exchange_results(behind)
    keep = lambda shape: pl.BlockSpec(shape, lambda i: (0,) * len(shape))
    out_specs = [real, keep((N_META, D_MODEL)), keep((1, D_MODEL))] + [hbm] * nx
    return pl.pallas_call(
        body, name="backward_in", grid=(nt,), out_shape=out_shape, in_specs=in_specs, out_specs=out_specs,
        scratch_shapes=[pltpu.VMEM((tm, N_MAIN), BF16)] + _exchange_semaphores(nx),
        compiler_params=_params(("arbitrary",)),
    )(x, tile0, norm_g, dh2, duf, duf, dqkv, dmid, dmid, dmid, dmid, w_main, w_f, *[a for _, a, _ in behind])


def _matmul_tn(name, a, b, tn):
    lp, m = a.shape
    n = b.shape[1]

    def body(a_ref, b_ref, c_ref):
        c_ref[...] = _dot_tn(a_ref[...].astype(BF16), b_ref[...].astype(BF16))

    return pl.pallas_call(
        body, name=name, grid=(n // tn,), out_shape=jax.ShapeDtypeStruct((m, n), F32),
        in_specs=[_const((lp, m)), pl.BlockSpec((lp, tn), lambda j: (0, j))],
        out_specs=pl.BlockSpec((m, tn), lambda j: (0, j)),
        compiler_params=_params(("arbitrary",)),
    )(a, b)


def _matmul_tn_rows(name, a, b, tm):
    lp, m = a.shape
    n = b.shape[1]

    def body(a_ref, b_ref, c_ref):
        c_ref[...] = _dot_tn(a_ref[...].astype(BF16), b_ref[...].astype(BF16))

    return pl.pallas_call(
        body, name=name, grid=(m // tm,), out_shape=jax.ShapeDtypeStruct((m, n), F32),
        in_specs=[pl.BlockSpec((lp, tm), lambda j: (0, j)), _const((lp, n))],
        out_specs=pl.BlockSpec((tm, n), lambda j: (j, 0)),
        compiler_params=_params(("arbitrary",)),
    )(a, b)


def _matmul_tn_stack(name, a, b):
    n_blocks, lp, m = a.shape
    n = b.shape[1]

    def body(a_ref, b_ref, c_ref):
        c_ref[...] = _dot_tn(a_ref[...], b_ref[...])

    return pl.pallas_call(
        body, name=name, grid=(n_blocks,), out_shape=jax.ShapeDtypeStruct((n_blocks * m, n), F32),
        in_specs=[pl.BlockSpec((None, lp, m), lambda j: (j, 0, 0)), _const((lp, n))],
        out_specs=pl.BlockSpec((m, n), lambda j: (j, 0)),
        compiler_params=_params(("arbitrary",)),
    )(a, b)


def _adamw_step(p_ref, w_ref, m_ref, v_ref, g_ref, d_ref, mo_ref, vo_ref):
    g = p_ref[0].astype(F32)
    for s in range(1, p_ref.shape[0]):
        g = g + p_ref[s].astype(F32)
    m_new = ADAM_B1 * m_ref[...] + (1.0 - ADAM_B1) * g
    v_new = ADAM_B2 * v_ref[...] + (1.0 - ADAM_B2) * (g * g)
    m_hat = m_new / (1.0 - ADAM_B1 ** ADAM_STEP)
    v_hat = v_new / (1.0 - ADAM_B2 ** ADAM_STEP)
    g_ref[...] = g
    d_ref[...] = -ADAM_LR * (m_hat / (jnp.sqrt(v_hat) + ADAM_EPS) + ADAM_WD * w_ref[...])
    mo_ref[...] = m_new
    vo_ref[...] = v_new


def _adamw_small(name, groups, loss_parts):
    n = len(groups)

    def body(*refs):
        ins, outs = refs[:4 * n + 1], refs[4 * n + 1:]
        for j in range(n):
            _adamw_step(*ins[4 * j:4 * j + 4], *outs[4 * j:4 * j + 4])
        total = ins[-1][0]
        for s in range(1, N_DEV):
            total = total + ins[-1][s]
        outs[-1][...] = total

    vmem = pl.BlockSpec(memory_space=pltpu.VMEM)
    out_shape = [jax.ShapeDtypeStruct(w.shape, F32) for _, w, _, _ in groups for _ in range(4)]
    out_shape.append(jax.ShapeDtypeStruct(loss_parts.shape[1:], F32))
    res = pl.pallas_call(
        body, name=name, out_shape=out_shape, in_specs=[vmem] * (4 * n + 1), out_specs=[vmem] * (4 * n + 1),
        compiler_params=_params(),
    )(*[a for g in groups for a in g], loss_parts)
    return [res[4 * j:4 * j + 4] for j in range(n)], res[-1]


def _adamw(name, parts, w, m, v, rows, cols=None):
    r, c_all = w.shape
    c = cols or c_all
    n_parts = parts.shape[0]

    def body(p_ref, w_ref, m_ref, v_ref, g_ref, d_ref, mo_ref, vo_ref):
        _adamw_step(p_ref, w_ref, m_ref, v_ref, g_ref, d_ref, mo_ref, vo_ref)

    blk = pl.BlockSpec((rows, c), lambda i, j: (i, j))
    return pl.pallas_call(
        body, name=name, grid=(r // rows, c_all // c), out_shape=[jax.ShapeDtypeStruct((r, c_all), F32)] * 4,
        in_specs=[pl.BlockSpec((n_parts, rows, c), lambda i, j: (0, i, j)), blk, blk, blk],
        out_specs=[blk] * 4,
        compiler_params=_params(("arbitrary", "arbitrary")),
    )(parts, w, m, v)


def _pair_sum(name, mine, theirs, rows):
    n, r, c = mine.shape

    def body(a_ref, b_ref, o_ref):
        o_ref[...] = (a_ref[...].astype(F32) + b_ref[...].astype(F32)).astype(BF16)

    blk = pl.BlockSpec((1, rows, c), lambda j, i: (j, i, 0))
    return pl.pallas_call(
        body, name=name, grid=(n, r // rows), out_shape=jax.ShapeDtypeStruct((n, r, c), BF16),
        in_specs=[blk, blk], out_specs=blk,
        compiler_params=_params(("arbitrary", "arbitrary")),
    )(mine, theirs)


def _by_core(slots):
    by_core = slots.reshape((4, 2) + slots.shape[1:]).swapaxes(0, 1)
    c = lax.axis_index("c")
    return (lax.dynamic_index_in_dim(by_core, c, 0, keepdims=False),
            lax.dynamic_index_in_dim(by_core, 1 - c, 0, keepdims=False))


def _columns_to_slots(a):
    r, c8 = a.shape
    return a.reshape(r, N_DEV, c8 // N_DEV).transpose(1, 0, 2)


def _slots_to_columns(a):
    n, r, c = a.shape
    return a.transpose(1, 0, 2).reshape(r, n * c)


def kernel(x, meta_tokens, norm_g, w_in, b_forget, pool_w, pool_scale, w_up_pool, w_up_attn, w_out, final_norm_g, loss_target, m_meta_tokens, m_norm_g, m_w_in, m_b_forget, m_pool_w, m_pool_scale, m_w_up_pool, m_w_up_attn, m_w_out, m_final_norm_g, v_meta_tokens, v_norm_g, v_w_in, v_b_forget, v_pool_w, v_pool_scale, v_w_up_pool, v_w_up_attn, v_w_out, v_final_norm_g):
    xs = x[0]
    target = loss_target[0]

    g_in, g_meta = _gather_two_level("gather_weights", [w_in[0].T.astype(BF16), meta_tokens], (320, 8))
    w_full = g_in.reshape(N_DEV * g_in.shape[1], D_MODEL)
    w_main = jnp.concatenate([w_full[:N_BEFORE_F], w_full[N_BEFORE_F + N_HEADS:]], axis=0)
    w_f = jnp.pad(w_full[N_BEFORE_F:N_BEFORE_F + N_HEADS], ((0, LANES - N_HEADS), (0, 0)))
    meta = _slots_to_columns(g_meta)
    tile0 = jnp.concatenate([jnp.zeros((PAD, D_MODEL), F32), meta], axis=0)
    b_f = jnp.pad(b_forget, ((0, 0), (0, LANES - N_HEADS)))
    pw_b = pool_w[0].astype(BF16)
    final_g = final_norm_g.reshape(1, D_MODEL)

    h, u, zp, k, v, qt, kt, vt, sneg = _forward_in(xs, tile0, norm_g, w_main, w_f, b_f)
    o, lse, g_upp, g_upa, g_out = _attention_forward(
        qt, k, vt, [("gather", w.astype(BF16), ALL_PEERS) for w in (w_up_pool[0], w_up_attn[0], w_out[0])])
    wupp = _slots_to_columns(g_upp)
    wupa = _slots_to_columns(g_upa)
    wout = g_out.reshape(D_MODEL, D_MODEL)
    (dh2, mg, yp, ya, dap, daa, do, dmid, dpn,
     loss_part, d_final_g, d_scale, d_pool_w) = _middle(xs, target, h, o, u, zp, w_main, wupp, wupa, wout,
                                                        pw_b, pool_scale, final_g)
    dw_out = _matmul_tn("grad_w_out", mg, dh2, 256)
    dw_upp = _matmul_tn("grad_w_up_pool", yp, dap, 512)
    dw_upa = _matmul_tn("grad_w_up_attn", ya, daa, 512)
    dqkv, dc, p_upp, p_upa, p_out, p_pool_w, p_scale, p_final_g = _attention_backward(
        qt, k, kt, v, do, o, lse,
        [("scatter", _columns_to_slots(dw_upp).astype(BF16), ALL_PEERS),
         ("scatter", _columns_to_slots(dw_upa).astype(BF16), ALL_PEERS),
         ("scatter", dw_out.reshape(N_DEV, D_MODEL // N_DEV, D_MODEL).astype(BF16), ALL_PEERS),
         ("gather", d_pool_w.reshape(4 * POOL_GROUP, POOL_GROUP), ALL_PEERS),
         ("gather", d_scale, ALL_PEERS), ("gather", d_final_g, ALL_PEERS)])
    duf, d_bf = _sequence_grads(dpn, dc, sneg)
    g_uf = _matmul_tn_rows("grad_w_in_pool_forget", duf, h, DUF_WIDTH)
    g_qkv = _matmul_tn_stack("grad_w_in_attention", dqkv, h)
    g_mid = _matmul_tn_rows("grad_w_in_gates", dmid, h, 512)
    dw_in = jnp.concatenate([g_uf[:POOL_WIDTH], g_mid[:MID_ZA], g_qkv, g_mid[MID_ZA:MID_GP],
                             g_uf[POOL_WIDTH:POOL_WIDTH + N_HEADS], g_mid[MID_GP:]], axis=0)
    dw_in = dw_in.reshape(N_DEV, dw_in.shape[0] // N_DEV, D_MODEL)
    mine, for_sibling = _by_core(dw_in)
    from_sibling, = _exchange("swap_with_sibling", [("swap", for_sibling.astype(BF16), (SIBLING,))])
    pair_sums = _pair_sum("pair_sum", mine, from_sibling, dw_in.shape[1])
    grad_x, d_meta, d_norm_g, p_in, p_bf, p_loss = _backward_in(
        xs, tile0, norm_g, dh2, duf, dqkv, dmid, w_main, w_f,
        [("chips", pair_sums, SAME_CORE), ("gather", d_bf, ALL_PEERS), ("gather", loss_part, ALL_PEERS)])
    p_meta, p_norm_g = _exchange(
        "exchange_gradients", [("scatter", _columns_to_slots(d_meta), ALL_PEERS), ("gather", d_norm_g, ALL_PEERS)])


    def pad_f(a):
        return jnp.pad(a, ((0, 0), (0, LANES - N_HEADS)))

    res = {}
    res["w_in"] = [a.T for a in _adamw("adamw_w_in", p_in, w_in[0].T, m_w_in[0].T, v_w_in[0].T, p_in.shape[1], 256)]
    res["w_up_pool"] = _adamw("adamw_w_up_pool", p_upp, w_up_pool[0], m_w_up_pool[0], v_w_up_pool[0], 512)
    res["w_up_attn"] = _adamw("adamw_w_up_attn", p_upa, w_up_attn[0], m_w_up_attn[0], v_w_up_attn[0], 512)
    res["w_out"] = _adamw("adamw_w_out", p_out, w_out[0], m_w_out[0], v_w_out[0], 128)
    flat = lambda a: a.reshape(4 * POOL_GROUP, POOL_GROUP)
    row = lambda a: a.reshape(1, D_MODEL)
    small, loss_row = _adamw_small(
        "adamw_small",
        [(p_meta, meta_tokens, m_meta_tokens, v_meta_tokens),
         (p_norm_g, norm_g, m_norm_g, v_norm_g),
         (p_bf, pad_f(b_forget), pad_f(m_b_forget), pad_f(v_b_forget)),
         (p_pool_w, flat(pool_w), flat(m_pool_w), flat(v_pool_w)),
         (p_scale, pool_scale, m_pool_scale, v_pool_scale),
         (p_final_g, final_g, row(m_final_norm_g), row(v_final_norm_g))],
        p_loss)
    res["meta_tokens"], res["norm_g"], bf, pw, res["pool_scale"], fg = small
    res["b_forget"] = [a[:, :N_HEADS] for a in bf]
    res["pool_w"] = [a.reshape(pool_w.shape) for a in pw]
    res["final_norm_g"] = [a.reshape(D_MODEL) for a in fg]
    loss = loss_row[0, 0]
    for name in ("w_in", "w_up_pool", "w_up_attn", "w_out"):
        res[name] = [a[None] for a in res[name]]

    order = ["meta_tokens", "norm_g", "w_in", "b_forget", "pool_w", "pool_scale", "w_up_pool", "w_up_attn", "w_out",
             "final_norm_g"]
    outs = [loss, grad_x[None]]
    for part in range(4):
        outs += [res[name][part] for name in order]
    return tuple(outs)
```

```python
import jax
import jax.numpy as jnp
from jax import lax
from jax.experimental import pallas as pl
from jax.experimental.pallas import tpu as pltpu

F32 = jnp.float32
BF16 = jnp.bfloat16

D_MODEL = 1024
N_META = 16
POOL_WIDTH = 512
ATTN_WIDTH = 512
N_HEADS = 8
HEAD_DIM = 64
POOL_WINDOWS = (2, 4, 8, 16)
POOL_GROUP = 128
MAX_WINDOW = 16
RMS_EPS = 1e-6
N_MAIN = 5120
N_BEFORE_F = 3072
N_DEV = 8
LANES = 128

ROW_TILE = 256
ATT_TILE = 256
ATT_Q_BLOCKS_FWD = 8
ATT_Q_BLOCKS_BWD = 8
PAD = ROW_TILE - N_META
VMEM_LIMIT = 56 * 1024 * 1024

ADAM_LR = 0.001
ADAM_B1 = 0.9
ADAM_B2 = 0.999
ADAM_EPS = 1e-08
ADAM_WD = 0.01
ADAM_STEP = 10

MID_ZA, MID_GP, MID_GA, MID_WIDTH = 512, 1024, 2048, 3072
NEG = -1e30
MESH = pl.DeviceIdType.MESH


def _params(sem=None):
    kw = dict(vmem_limit_bytes=VMEM_LIMIT)
    if sem is not None:
        kw["dimension_semantics"] = sem
    return pltpu.CompilerParams(**kw)


def _const(shape, block_index=None):
    idx = block_index or (0,) * len(shape)
    return pl.BlockSpec(shape, lambda i: idx, pipeline_mode=pl.Buffered(1))


def _sigmoid(x):
    return jax.nn.sigmoid(x)


def _dot(a, b):
    return jnp.dot(a, b, preferred_element_type=F32)


def _dot_nt(a, b):
    return lax.dot_general(a, b, (((1,), (1,)), ((), ())), preferred_element_type=F32)


def _dot_tn(a, b):
    return lax.dot_general(a, b, (((0,), (0,)), ((), ())), preferred_element_type=F32)


def _pool_counts(first_row, rows):
    row = first_row + lax.broadcasted_iota(jnp.int32, (rows, 1), 0)
    pos1 = row - PAD + 1
    return [jnp.clip(pos1, 1, w).astype(F32) for w in POOL_WINDOWS]


def _pool_means(u_ext, u, counts):
    rows = u.shape[0]
    out = []
    for g, w in enumerate(POOL_WINDOWS):
        s = u_ext[:, POOL_GROUP * g:POOL_GROUP * (g + 1)]
        sh = 1
        while sh < w:
            s = s + pltpu.roll(s, sh, axis=0)
            sh *= 2
        out.append(s[MAX_WINDOW:MAX_WINDOW + rows, :] / counts[g] - u[:, POOL_GROUP * g:POOL_GROUP * (g + 1)])
    return out


Q_BIAS, Q_ONES, Q_LSE = 64, 67, 70
K_ONES, K_BIAS, K_ONES2 = 64, 67, 70
V_ONES = 64
DO_BIAS = 64


def _lane_ones(lane, ranges):
    hit = None
    for lo, hi in ranges:
        r = (lane >= lo) & (lane < hi)
        hit = r if hit is None else hit | r
    return jnp.where(hit, 1.0, 0.0)


def _put3(base, lane, first, x):
    hi = x.astype(BF16).astype(F32)
    rest = x - hi
    mid = rest.astype(BF16).astype(F32)
    lo = (rest - mid).astype(BF16).astype(F32)
    for j, piece in enumerate((hi, mid, lo)):
        base = jnp.where(lane == first + j, piece, base)
    return base


SIBLING = 1
SAME_CORE = (2, 4, 6)
ALL_PEERS = (1, 2, 3, 4, 5, 6, 7)


def _place():
    return lax.axis_index("x"), lax.axis_index("y"), lax.axis_index("c")


def _peer(r):
    x, y, c = _place()
    return (1 - x if r & 4 else x, 1 - y if r & 2 else y, 1 - c if r & 1 else c)


def _device_slot(p):
    return 4 * p[0] + 2 * p[1] + p[2]


def _chip_slot(p):
    return 2 * p[0] + p[1]


def _exchange(name, items):
    n = len(items)

    def body(*refs):
        copies = _exchange_copies(items, refs[:n], refs[n:2 * n], *refs[2 * n:])
        for cp in copies:
            cp.start()
        for cp in copies:
            cp.wait()

    hbm = pl.BlockSpec(memory_space=pl.ANY)
    return pl.pallas_call(
        body, name=name, out_shape=_exchange_results(items),
        in_specs=[hbm] * n, out_specs=[hbm] * n,
        scratch_shapes=_exchange_semaphores(n),
    )(*[a for _, a, _ in items])


def _exchange_results(items):
    return [jax.ShapeDtypeStruct(((N_DEV,) if kind == "gather" else ()) + a.shape, a.dtype) for kind, a, _ in items]


def _exchange_semaphores(n):
    return [pltpu.SemaphoreType.DMA((n, N_DEV - 1)), pltpu.SemaphoreType.DMA((n, N_DEV - 1)),
            pltpu.SemaphoreType.DMA((n,))]


def _exchange_copies(items, ins, outs, send_sems, recv_sems, local_sems):
    me = _place()
    copies = []
    for a, (kind, _, peers) in enumerate(items):
        slot = _chip_slot if kind == "chips" else _device_slot
        for r in peers:
            peer = _peer(r)
            src = ins[a] if kind in ("swap", "gather") else ins[a].at[slot(peer)]
            dst = outs[a] if kind == "swap" else outs[a].at[slot(me)]
            copies.append(pltpu.make_async_remote_copy(
                src_ref=src, dst_ref=dst, send_sem=send_sems.at[a, r - 1], recv_sem=recv_sems.at[a, r - 1],
                device_id=peer, device_id_type=MESH))
        if kind != "swap":
            src = ins[a] if kind == "gather" else ins[a].at[slot(me)]
            copies.append(pltpu.make_async_copy(src, outs[a].at[slot(me)], local_sems.at[a]))
    return copies


def _gather_two_level(name, arrays, halves):
    n = len(arrays)
    x_flip, y_flip, both = 4, 2, 6

    def body(*refs):
        ins, outs = refs[:n], refs[n:2 * n]
        send_sems, recv_sems, local_sems = refs[2 * n:]
        me, sibling = _place(), _peer(SIBLING)
        xn, yn, dg = _peer(x_flip), _peer(y_flip), _peer(both)

        def part(a, block, half):
            rows = outs[a].at[_device_slot(block)]
            if half is None:
                return rows
            return rows.at[pl.ds(0, halves[a])] if half == 0 else rows.at[pl.ds(halves[a], arrays[a].shape[0] - halves[a])]

        def copy(a, k, block, half, to, src=None):
            dst = part(a, block, half)
            return pltpu.make_async_remote_copy(
                src_ref=dst if src is None else src, dst_ref=dst,
                send_sem=send_sems.at[a, k], recv_sem=recv_sems.at[a, k], device_id=to, device_id_type=MESH)

        sends, own = [], []

        def start(cp):
            cp.start()
            sends.append(cp)

        for a in range(n):
            mine = pltpu.make_async_copy(ins[a], outs[a].at[_device_slot(me)], local_sems.at[a])
            mine.start()
            own.append(mine)
            for k, to in enumerate((sibling, xn, yn)):
                start(copy(a, k, me, None, to, src=ins[a]))
        for a in range(n):
            copy(a, 1, xn, None, me).wait_recv()
            start(copy(a, 3, xn, 0, yn))
            start(copy(a, 5, xn, None, sibling))
        for a in range(n):
            copy(a, 2, yn, None, me).wait_recv()
            start(copy(a, 4, yn, 1, xn))
            start(copy(a, 6, yn, None, sibling))
        for a in range(n):
            copy(a, 3, dg, 0, me).wait_recv()
            copy(a, 4, dg, 1, me).wait_recv()
            start(copy(a, 7, dg, None, sibling))
        for a in range(n):
            copy(a, 0, sibling, None, me).wait_recv()
            for k, r in ((5, x_flip), (6, y_flip), (7, both)):
                copy(a, k, _peer(r | SIBLING), None, me).wait_recv()
        for cp in sends:
            cp.wait_send()
        for cp in own:
            cp.wait()

    hbm = pl.BlockSpec(memory_space=pl.ANY)
    return pl.pallas_call(
        body, name=name, out_shape=[jax.ShapeDtypeStruct((N_DEV,) + a.shape, a.dtype) for a in arrays],
        in_specs=[hbm] * n, out_specs=[hbm] * n,
        scratch_shapes=[pltpu.SemaphoreType.DMA((n, 8)), pltpu.SemaphoreType.DMA((n, 8)),
                        pltpu.SemaphoreType.DMA((n,))],
    )(*arrays)


def _forward_in(x, tile0, norm_g, w_main, w_f, b_f):
    seq = x.shape[0]
    nt = seq // ROW_TILE + 1
    lp = nt * ROW_TILE
    tm = ROW_TILE

    def body(x_ref, t0_ref, g_ref, wa_ref, wf_ref, bf_ref,
             h_ref, u_ref, zp_ref, k_ref, v_ref, qt_ref, kt_ref, vt_ref, sn_ref, carry_ref):
        i = pl.program_id(0)

        @pl.when(i == 0)
        def _():
            carry_ref[...] = jnp.zeros_like(carry_ref)

        xt = jnp.where(i == 0, t0_ref[...], x_ref[...])
        r = lax.rsqrt(jnp.mean(xt * xt, axis=-1, keepdims=True) + RMS_EPS)
        h = (xt * r * g_ref[...]).astype(BF16)
        h_ref[...] = h
        pa = _dot_nt(h, wa_ref[...])
        u_ref[...] = pa[:, :512]
        zp_ref[...] = pa[:, 512:1024]

        fl = _dot_nt(h, wf_ref[...]) + bf_ref[...]
        row = i * tm + lax.broadcasted_iota(jnp.int32, (tm, LANES), 0)
        rloc = lax.broadcasted_iota(jnp.int32, (tm, LANES), 0)
        lane = lax.broadcasted_iota(jnp.int32, (tm, LANES), 1)
        live = (row >= PAD) & (lane < N_HEADS)
        logf = jnp.minimum(fl, 0.0) - jnp.log1p(jnp.exp(-jnp.abs(fl)))
        cs = jnp.where(live, logf, 0.0)
        sh = 1
        while sh < tm:
            cs = cs + jnp.where(rloc >= sh, pltpu.roll(cs, sh, axis=0), 0.0)
            sh *= 2
        cs = cs + carry_ref[...]
        carry_ref[...] = cs[tm - 1:tm, :]
        sn_ref[...] = jnp.where(live, _sigmoid(-fl), 0.0)

        rows1 = i * tm + lax.broadcasted_iota(jnp.int32, (tm, 1), 0)
        ones_q = _lane_ones(lane, ((Q_ONES, Q_ONES + 3),))
        ones_k = _lane_ones(lane, ((K_ONES, K_ONES + 3), (K_ONES2, K_ONES2 + 3)))
        ones_v = _lane_ones(lane, ((V_ONES, V_ONES + 3),))
        for hp in range(N_HEADS // 2):
            qp = pa[:, 1024 + LANES * hp:1024 + LANES * (hp + 1)] * 0.125
            kp = pa[:, 1536 + LANES * hp:1536 + LANES * (hp + 1)]
            vp = pa[:, 2048 + LANES * hp:2048 + LANES * (hp + 1)]
            for e in range(2):
                head = 2 * hp + e
                if e:
                    qp, kp, vp = (pltpu.roll(a, HEAD_DIM, axis=1) for a in (qp, kp, vp))
                c_h = cs[:, head:head + 1]
                q_h = jnp.where(lane < HEAD_DIM, qp, _put3(ones_q, lane, Q_BIAS, c_h))
                qt_ref[head] = q_h.T.astype(BF16)
                minus_ck = jnp.where(rows1 >= PAD, -c_h, NEG)
                k_h = jnp.where(lane < HEAD_DIM, kp, _put3(ones_k, lane, K_BIAS, minus_ck))
                k_ref[head] = k_h.astype(BF16)
                kt_ref[head] = k_h.T.astype(BF16)
                v_h = jnp.where(lane < HEAD_DIM, vp, ones_v)
                v_ref[head] = v_h.astype(BF16)
                vt_ref[head] = v_h.T.astype(BF16)

    row_f32 = lambda w: pl.BlockSpec((tm, w), lambda i: (i, 0))
    out_shape = [
        jax.ShapeDtypeStruct((lp, D_MODEL), BF16),
        jax.ShapeDtypeStruct((lp, POOL_WIDTH), F32),
        jax.ShapeDtypeStruct((lp, POOL_WIDTH), F32),
        jax.ShapeDtypeStruct((N_HEADS, lp, LANES), BF16),
        jax.ShapeDtypeStruct((N_HEADS, lp, LANES), BF16),
        jax.ShapeDtypeStruct((N_HEADS, LANES, lp), BF16),
        jax.ShapeDtypeStruct((N_HEADS, LANES, lp), BF16),
        jax.ShapeDtypeStruct((N_HEADS, LANES, lp), BF16),
        jax.ShapeDtypeStruct((lp, LANES), F32),
    ]
    heads = pl.BlockSpec((N_HEADS, tm, LANES), lambda i: (0, i, 0))
    heads_t = pl.BlockSpec((N_HEADS, LANES, tm), lambda i: (0, 0, i))
    out_specs = [row_f32(D_MODEL), row_f32(512), row_f32(512), heads, heads, heads_t, heads_t, heads_t,
                 row_f32(LANES)]
    in_specs = [
        pl.BlockSpec((tm, D_MODEL), lambda i: (jnp.maximum(i - 1, 0), 0)),
        _const((tm, D_MODEL)), _const((1, D_MODEL)),
        _const((2560, D_MODEL)), _const((LANES, D_MODEL)), _const((1, LANES)),
    ]
    return pl.pallas_call(
        body, name="forward_in", grid=(nt,), out_shape=out_shape, in_specs=in_specs, out_specs=out_specs,
        scratch_shapes=[pltpu.VMEM((1, LANES), F32)],
        compiler_params=_params(("arbitrary",)),
    )(x, tile0, norm_g, w_main, w_f, b_f)


def _pair_lanes(a0, a1):
    lane = lax.broadcasted_iota(jnp.int32, a0.shape, 1)
    return jnp.where(lane < HEAD_DIM, a0, pltpu.roll(a1, HEAD_DIM, axis=1))


def _behind(items, ins, outs, sems):
    step, last = pl.program_id(0), pl.num_programs(0) - 1

    @pl.when(step == 0)
    def _():
        for cp in _exchange_copies(items, ins, outs, *sems):
            cp.start()

    def finish():
        @pl.when(step == last)
        def _():
            for cp in _exchange_copies(items, ins, outs, *sems):
                cp.wait()

    return finish


def _attention_forward(qt, k, vt, behind):
    lp = k.shape[1]
    tk = ATT_TILE
    q_blocks = ATT_Q_BLOCKS_FWD if (lp // tk - 1) % ATT_Q_BLOCKS_FWD == 0 else ATT_Q_BLOCKS_BWD
    tq_big = q_blocks * tk
    n_big = (lp // tk - 1) // q_blocks
    assert lp == tk + n_big * tq_big and q_blocks % 2 == 0
    nx = len(behind)

    def body(qt_ref, k_ref, vt_ref, *rest):
        o_ref, lse_ref = rest[nx:nx + 2]
        s_buf, m_scr, acc_scr = rest[2 * nx + 2:2 * nx + 5]
        finish_exchange = _behind(behind, rest[:nx], rest[nx + 2:2 * nx + 2], rest[2 * nx + 5:])

        def q_tile(q0, tq, pairs):
            first = q0 // tk
            qts = [qt_ref[e, :, pl.ds(q0, tq)] for e in range(2)]

            def block(kj):
                return pl.ds(kj * tk if isinstance(kj, int) else pl.multiple_of(kj * tk, tk), tk)

            def step(kj, rd, wr, c0=0, diagonal=False):
                c1 = c0 + tk if diagonal else c0
                for e in range(2):
                    s = s_buf[rd, e, :, c0:tq]
                    if wr is not None:
                        s_buf[wr, e, :, c1:tq] = _dot(k_ref[e, block(kj + 1), :], qts[e][:, c1:tq])
                    if diagonal:
                        keys = lax.broadcasted_iota(jnp.int32, s.shape, 0)
                        s = jnp.where(keys <= lax.broadcasted_iota(jnp.int32, s.shape, 1), s, NEG)
                    m = m_scr[e, :, c0:tq]
                    m_new = jnp.maximum(m, jnp.max(s, axis=0, keepdims=True))
                    p = jnp.exp(s - m_new)
                    pv = _dot(vt_ref[e, :, block(kj)], p.astype(BF16))
                    acc_scr[e, :, c0:tq] = jnp.exp(m - m_new) * acc_scr[e, :, c0:tq] + pv
                    m_scr[e, :, c0:tq] = m_new

            for e in range(2):
                m_scr[e, :, 0:tq] = jnp.full((1, tq), NEG, F32)
                acc_scr[e, :, 0:tq] = jnp.zeros((LANES, tq), F32)
                s_buf[0, e, :, 0:tq] = _dot(k_ref[e, block(0), :], qts[e])
            if pairs is None:
                step(0, 0, None, 0, True)
            else:
                step(0, 0, 1)

                def two_steps(t, _):
                    step(1 + 2 * t, 1, 0)
                    step(2 + 2 * t, 0, 1)
                    return 0

                lax.fori_loop(0, pairs, two_steps, 0)
                for b in range(tq // tk):
                    step(first + b, (b + 1) % 2, b % 2 if (b + 1) * tk < tq else None, b * tk, True)
            outs, lses = [], []
            for e in range(2):
                acc = acc_scr[e, :, 0:tq]
                l = acc[V_ONES:V_ONES + 1, :]
                outs.append((acc / l).T)
                lses.append(m_scr[e, :, 0:tq] + jnp.log(l))
            o_ref[pl.ds(q0, tq), :] = _pair_lanes(outs[0], outs[1]).astype(BF16)
            lse_rows = jnp.concatenate(lses + [jnp.zeros((LANES - 2, tq), F32)], axis=0)
            lse_ref[pl.ds(q0, tq), :] = lse_rows.T

        q_tile(0, tk, None)

        def big_tile(i, _):
            q_tile(pl.multiple_of(tk + i * tq_big, tk), tq_big, (q_blocks // 2) * i)
            return 0

        lax.fori_loop(0, n_big, big_tile, 0)
        finish_exchange()

    pair = pl.BlockSpec((lp, LANES), lambda hp: (0, hp))
    heads = pl.BlockSpec((2, lp, LANES), lambda hp: (hp, 0, 0), pipeline_mode=pl.Buffered(1))
    heads_t = pl.BlockSpec((2, LANES, lp), lambda hp: (hp, 0, 0), pipeline_mode=pl.Buffered(1))
    hbm = pl.BlockSpec(memory_space=pl.ANY)
    return pl.pallas_call(
        body, name="attention_forward", grid=(N_HEADS // 2,),
        out_shape=[jax.ShapeDtypeStruct((lp, ATTN_WIDTH), BF16), jax.ShapeDtypeStruct((lp, ATTN_WIDTH), F32)]
        + _exchange_results(behind),
        in_specs=[heads_t, heads, heads_t] + [hbm] * nx,
        out_specs=[pair, pair] + [hbm] * nx,
        scratch_shapes=[pltpu.VMEM((2, 2, tk, tq_big), F32), pltpu.VMEM((2, 1, tq_big), F32),
                        pltpu.VMEM((2, LANES, tq_big), F32)] + _exchange_semaphores(nx),
        compiler_params=_params(("arbitrary",)),
    )(qt, k, vt, *[a for _, a, _ in behind])


def _rows3(first, x):
    sub = lax.broadcasted_iota(jnp.int32, (LANES, x.shape[1]), 0)
    hi = x.astype(BF16).astype(F32)
    rest = x - hi
    mid = rest.astype(BF16).astype(F32)
    lo = (rest - mid).astype(BF16).astype(F32)
    out = jnp.zeros((LANES, x.shape[1]), F32)
    for j, piece in enumerate((hi, mid, lo)):
        out = jnp.where(sub == first + j, piece, out)
    return out


def _attention_backward(qt, k, kt, v, do, o, lse, behind):
    lp = k.shape[1]
    tb = ATT_TILE
    nb = lp // tb
    tq_big = ATT_Q_BLOCKS_BWD * tb
    n_big = (nb - 1) // ATT_Q_BLOCKS_BWD
    assert lp == tb + n_big * tq_big and ATT_Q_BLOCKS_BWD % 2 == 0
    nx = len(behind)

    def body(qt_ref, k_ref, kt_ref, v_ref, do_ref, o_ref, lse_ref, *rest):
        dqkv_ref, dc_ref = rest[nx:nx + 2]
        q2_ref, do2_ref, dk_acc, dv_acc, dq_scr, s_buf = rest[2 * nx + 2:2 * nx + 8]
        finish_exchange = _behind(behind, rest[:nx], rest[nx + 2:2 * nx + 2], rest[2 * nx + 8:])
        sub = lax.broadcasted_iota(jnp.int32, (LANES, tb), 0)

        def lanes01(row0, row1):
            n = row0.shape[1]
            return jnp.concatenate([row0, row1, jnp.zeros((LANES - 2, n), F32)], axis=0).T

        def prepare(bi, _):
            r0 = pl.multiple_of(bi * tb, tb)
            queries = r0 + lax.broadcasted_iota(jnp.int32, (1, tb), 1)
            dob = do_ref[pl.ds(r0, tb), :].astype(F32)
            do_t = dob.T
            dd_t = (dob * o_ref[pl.ds(r0, tb), :].astype(F32)).T
            lse_t = lse_ref[pl.ds(r0, tb), :].T
            for e in range(2):
                delta = jnp.sum(dd_t[HEAD_DIM * e:HEAD_DIM * (e + 1), :], axis=0, keepdims=True)
                do_e = jnp.concatenate([do_t[HEAD_DIM * e:HEAD_DIM * (e + 1), :], jnp.zeros((HEAD_DIM, tb), F32)], axis=0)
                do2_ref[e, :, pl.ds(r0, tb)] = jnp.where(sub < HEAD_DIM, do_e, _rows3(DO_BIAS, -delta)).astype(BF16)
                minus_lse = jnp.where(queries >= PAD, -lse_t[e:e + 1, :], NEG)
                keep = (sub < Q_LSE) | (sub >= Q_LSE + 3)
                q2_ref[e, :, pl.ds(r0, tb)] = jnp.where(keep, qt_ref[e, :, pl.ds(r0, tb)].astype(F32),
                                                        _rows3(Q_LSE, minus_lse)).astype(BF16)
            return 0

        lax.fori_loop(0, nb, prepare, 0)
        dk_acc[...] = jnp.zeros_like(dk_acc)
        dv_acc[...] = jnp.zeros_like(dv_acc)

        def q_tile(q0, tq, pairs):
            first = q0 // tb
            qts = [q2_ref[e, :, pl.ds(q0, tq)] for e in range(2)]
            dots = [do2_ref[e, :, pl.ds(q0, tq)] for e in range(2)]

            def block(kj):
                return pl.ds(kj * tb if isinstance(kj, int) else pl.multiple_of(kj * tb, tb), tb)

            def step(kj, rd, wr, c0=0, diagonal=False):
                c1 = c0 + tb if diagonal else c0
                for e in range(2):
                    s = s_buf[rd, e, :, c0:tq]
                    if wr is not None:
                        s_buf[wr, e, :, c1:tq] = _dot(k_ref[e, block(kj + 1), :], qts[e][:, c1:tq])
                    dpd = _dot(v_ref[e, block(kj), :], dots[e][:, c0:tq])
                    p = jnp.exp(s)
                    if diagonal:
                        keys = lax.broadcasted_iota(jnp.int32, s.shape, 0)
                        p = jnp.where(keys <= lax.broadcasted_iota(jnp.int32, s.shape, 1), p, 0.0)
                    dsb = (p * dpd).astype(BF16)
                    dv_acc[e, :, block(kj)] += _dot_nt(dots[e][:, c0:tq], p.astype(BF16))
                    dk_acc[e, :, block(kj)] += _dot_nt(qts[e][:, c0:tq], dsb)
                    dq_scr[e, :, c0:tq] += _dot(kt_ref[e, :, block(kj)], dsb)

            for e in range(2):
                dq_scr[e, :, 0:tq] = jnp.zeros((LANES, tq), F32)
                s_buf[0, e, :, 0:tq] = _dot(k_ref[e, block(0), :], qts[e])
            if pairs is None:
                step(0, 0, None, 0, True)
            else:
                step(0, 0, 1)

                def two_steps(t, _):
                    step(1 + 2 * t, 1, 0)
                    step(2 + 2 * t, 0, 1)
                    return 0

                lax.fori_loop(0, pairs, two_steps, 0)
                for b in range(tq // tb):
                    step(first + b, (b + 1) % 2, b % 2 if (b + 1) * tb < tq else None, b * tb, True)
            dq0, dq1 = dq_scr[0, :, 0:tq], dq_scr[1, :, 0:tq]
            dqkv_ref[0, pl.ds(q0, tq), :] = (_pair_lanes(dq0.T, dq1.T) * 0.125).astype(BF16)
            dc_ref[pl.ds(q0, tq), :] = lanes01(dq0[K_ONES:K_ONES + 1, :], dq1[K_ONES:K_ONES + 1, :])

        q_tile(0, tb, None)

        def big_tile(i, _):
            q_tile(pl.multiple_of(tb + i * tq_big, tb), tq_big, (ATT_Q_BLOCKS_BWD // 2) * i)
            return 0

        lax.fori_loop(0, n_big, big_tile, 0)

        def finish(bi, _):
            r0 = pl.multiple_of(bi * tb, tb)
            dk0, dk1 = dk_acc[0, :, pl.ds(r0, tb)], dk_acc[1, :, pl.ds(r0, tb)]
            dqkv_ref[1, pl.ds(r0, tb), :] = _pair_lanes(dk0.T, dk1.T).astype(BF16)
            dqkv_ref[2, pl.ds(r0, tb), :] = _pair_lanes(dv_acc[0, :, pl.ds(r0, tb)].T,
                                                        dv_acc[1, :, pl.ds(r0, tb)].T).astype(BF16)
            dc_ref[pl.ds(r0, tb), :] = dc_ref[pl.ds(r0, tb), :] - lanes01(dk0[Q_ONES:Q_ONES + 1, :], dk1[Q_ONES:Q_ONES + 1, :])
            return 0

        lax.fori_loop(0, nb, finish, 0)
        finish_exchange()

    once = pl.Buffered(1)
    pair = pl.BlockSpec((lp, LANES), lambda hp: (0, hp))
    pair_in = pl.BlockSpec((lp, LANES), lambda hp: (0, hp), pipeline_mode=once)
    heads = pl.BlockSpec((2, lp, LANES), lambda hp: (hp, 0, 0), pipeline_mode=once)
    heads_t = pl.BlockSpec((2, LANES, lp), lambda hp: (hp, 0, 0), pipeline_mode=once)
    hbm = pl.BlockSpec(memory_space=pl.ANY)
    return pl.pallas_call(
        body, name="attention_backward", grid=(N_HEADS // 2,),
        out_shape=[jax.ShapeDtypeStruct((3, lp, ATTN_WIDTH), BF16), jax.ShapeDtypeStruct((lp, ATTN_WIDTH), F32)]
        + _exchange_results(behind),
        in_specs=[heads_t, heads, heads_t, heads, pair_in, pair_in, pair_in] + [hbm] * nx,
        out_specs=[pl.BlockSpec((3, lp, LANES), lambda hp: (0, 0, hp)), pair] + [hbm] * nx,
        scratch_shapes=[pltpu.VMEM((2, LANES, lp), BF16), pltpu.VMEM((2, LANES, lp), BF16),
                        pltpu.VMEM((2, LANES, lp), F32), pltpu.VMEM((2, LANES, lp), F32),
                        pltpu.VMEM((2, LANES, tq_big), F32), pltpu.VMEM((2, 2, tb, tq_big), F32)]
        + _exchange_semaphores(nx),
        compiler_params=_params(("arbitrary",)),
    )(qt, k, kt, v, do, o, lse, *[a for _, a, _ in behind])


def _middle(x, target, h, o, u, zp, w_main, w_up_pool, w_up_attn, w_out, pool_w, pool_scale, final_g):
    seq = x.shape[0]
    tm = ROW_TILE
    nt = seq // tm + 1
    lp = nt * tm
    halo_blocks = tm // MAX_WINDOW

    def body(x_ref, t_ref, h_ref, o_ref, u_ref, uh_ref, zp_ref,
             wc_ref, wupp_ref, wupa_ref, wout_ref, pw_ref, sc_ref, gf_ref,
             dh2_ref, mg_ref, yp_ref, ya_ref, dap_ref, daa_ref, do_ref, dmid_ref, dpn_ref,
             loss_ref, dgf_ref, dsc_ref, dpw_ref):
        i = pl.program_id(0)
        tiles = (dh2_ref, mg_ref, yp_ref, ya_ref, dap_ref, daa_ref, do_ref, dmid_ref, dpn_ref)

        @pl.when(i == 0)
        def _():
            for ref in tiles + (loss_ref, dgf_ref, dsc_ref, dpw_ref):
                ref[...] = jnp.zeros_like(ref)

        @pl.when(i > 0)
        def _():
            xt = x_ref[...]
            hb = h_ref[...]
            pc = _dot_nt(hb, wc_ref[...])
            za, gp, ga = pc[:, :512], pc[:, 512:1536], pc[:, 1536:]
            of = o_ref[...].astype(F32)
            sza = _sigmoid(za)
            silu_za = za * sza
            ya = (of * silu_za).astype(BF16)
            ya_ref[...] = ya
            aa = _dot(ya, wupa_ref[...])

            u = u_ref[...]
            zp = zp_ref[...]
            counts = _pool_counts(i * tm, tm)
            ps = _pool_means(jnp.concatenate([uh_ref[...], u], axis=0), u, counts)
            pbs = [p.astype(BF16) for p in ps]
            ppw = jnp.concatenate([_dot(pbs[g], pw_ref[g]) for g in range(4)], axis=1)
            sc = sc_ref[...]
            szp = _sigmoid(zp)
            silu_zp = zp * szp
            ypre = ppw * sc
            yp = (ypre * silu_zp).astype(BF16)
            yp_ref[...] = yp
            ap = _dot(yp, wupp_ref[...])

            sgp, sga = _sigmoid(gp), _sigmoid(ga)
            mg = (sgp * ap + sga * aa).astype(BF16)
            mg_ref[...] = mg
            h2 = xt + _dot(mg, wout_ref[...])
            r2 = lax.rsqrt(jnp.mean(h2 * h2, axis=-1, keepdims=True) + RMS_EPS)
            h2n = h2 * r2
            gf = gf_ref[...]
            diff = h2n * gf - t_ref[...]
            loss_ref[...] += 0.5 * jnp.sum(jnp.mean(diff * diff, axis=-1, keepdims=True), axis=0, keepdims=True)
            dy = diff * (1.0 / D_MODEL)
            dgf_ref[...] += jnp.sum(dy * h2n, axis=0, keepdims=True)
            dyg = dy * gf
            dh2 = r2 * (dyg - h2n * jnp.mean(dyg * h2n, axis=-1, keepdims=True))
            dh2_ref[...] = dh2
            dmg = _dot_nt(dh2.astype(BF16), wout_ref[...])
            dap = (dmg * sgp).astype(BF16)
            daa = (dmg * sga).astype(BF16)
            dap_ref[...] = dap
            daa_ref[...] = daa
            dmid_ref[:, MID_GP:MID_GA] = (dmg * ap * sgp * (1.0 - sgp)).astype(BF16)
            dmid_ref[:, MID_GA:] = (dmg * aa * sga * (1.0 - sga)).astype(BF16)
            dyp = _dot_nt(dap, wupp_ref[...])
            dya = _dot_nt(daa, wupa_ref[...])
            do_ref[...] = (dya * silu_za).astype(BF16)
            dmid_ref[:, MID_ZA:MID_GP] = (dya * of * (sza * (1.0 + za * (1.0 - sza)))).astype(BF16)

            dypre = dyp * silu_zp
            dmid_ref[:, :MID_ZA] = (dyp * ypre * (szp * (1.0 + zp * (1.0 - szp)))).astype(BF16)
            dsc_ref[...] += jnp.sum(dypre * ppw, axis=0, keepdims=True)
            dppw = (dypre * sc).astype(BF16)
            dpns = []
            for g in range(4):
                dg = dppw[:, POOL_GROUP * g:POOL_GROUP * (g + 1)]
                dpw_ref[g] += _dot_tn(pbs[g], dg)
                dpns.append(_dot_nt(dg, pw_ref[g]) / counts[g])
            dpn_ref[...] = jnp.concatenate(dpns, axis=1)

    real = lambda w: pl.BlockSpec((tm, w), lambda i: (jnp.maximum(i - 1, 0), 0))
    row = lambda w: pl.BlockSpec((tm, w), lambda i: (i, 0))
    in_specs = [
        real(D_MODEL), real(D_MODEL), row(D_MODEL), row(512), row(512),
        pl.BlockSpec((MAX_WINDOW, 512), lambda i: (jnp.maximum(i * halo_blocks - 1, 0), 0)), row(512),
        _const((2560, D_MODEL), (1, 0)), _const((POOL_WIDTH, D_MODEL)), _const((ATTN_WIDTH, D_MODEL)),
        _const((D_MODEL, D_MODEL)), _const((4, POOL_GROUP, POOL_GROUP)), _const((1, POOL_WIDTH)), _const((1, D_MODEL)),
    ]
    sd = jax.ShapeDtypeStruct
    out_shape = [
        sd((lp, D_MODEL), F32),
        sd((lp, D_MODEL), BF16),
        sd((lp, 512), BF16),
        sd((lp, 512), BF16),
        sd((lp, D_MODEL), BF16),
        sd((lp, D_MODEL), BF16),
        sd((lp, 512), BF16),
        sd((lp, MID_WIDTH), BF16),
        sd((lp, 512), F32),
        sd((1, LANES), F32),
        sd((1, D_MODEL), F32),
        sd((1, 512), F32),
        sd((4, POOL_GROUP, POOL_GROUP), F32),
    ]
    keep = lambda shape: pl.BlockSpec(shape, lambda i: (0,) * len(shape))
    out_specs = [row(D_MODEL), row(D_MODEL), row(512), row(512), row(D_MODEL), row(D_MODEL), row(512),
                 row(MID_WIDTH), row(512),
                 keep((1, LANES)), keep((1, D_MODEL)), keep((1, 512)), keep((4, POOL_GROUP, POOL_GROUP))]
    return pl.pallas_call(
        body, name="middle", grid=(nt,), out_shape=out_shape, in_specs=in_specs, out_specs=out_specs,
        compiler_params=_params(("arbitrary",)),
    )(x, target, h, o, u, u, zp, w_main, w_up_pool, w_up_attn, w_out, pool_w, pool_scale, final_g)


DUF_WIDTH = POOL_WIDTH + LANES


def _sequence_grads(dpn, dc, sneg):
    lp = dpn.shape[0]
    tm = ROW_TILE
    nt = lp // tm
    halo_blocks = tm // MAX_WINDOW
    last_halo = lp // MAX_WINDOW - 1

    def body(dpn_ref, dpnh_ref, dc_ref, sn_ref, duf_ref, dbf_ref, carry_ref):
        i = pl.program_id(0)
        t = nt - 1 - i

        @pl.when(i == 0)
        def _():
            carry_ref[...] = jnp.zeros_like(carry_ref)
            dbf_ref[...] = jnp.zeros_like(dbf_ref)

        dpn_t = dpn_ref[...]
        ahead = jnp.where(i == 0, jnp.zeros_like(dpnh_ref), dpnh_ref[...])
        ext = jnp.concatenate([dpn_t, ahead], axis=0)
        counts = _pool_counts(t * tm, tm)
        for g, w in enumerate(POOL_WINDOWS):
            s = ext[:, POOL_GROUP * g:POOL_GROUP * (g + 1)]
            sh = 1
            while sh < w:
                s = s + pltpu.roll(s, tm + MAX_WINDOW - sh, axis=0)
                sh *= 2
            du = s[:tm, :] - dpn_t[:, POOL_GROUP * g:POOL_GROUP * (g + 1)] * counts[g]
            duf_ref[:, POOL_GROUP * g:POOL_GROUP * (g + 1)] = du.astype(BF16)

        dct = dc_ref[:, 0:LANES]
        for hp in range(1, N_HEADS // 2):
            dct = dct + pltpu.roll(dc_ref[:, LANES * hp:LANES * (hp + 1)], 2 * hp, axis=1)
        rloc = lax.broadcasted_iota(jnp.int32, (tm, LANES), 0)
        sh = 1
        while sh < tm:
            dct = dct + jnp.where(rloc + sh < tm, pltpu.roll(dct, tm - sh, axis=0), 0.0)
            sh *= 2
        dct = dct + carry_ref[...]
        carry_ref[...] = dct[0:1, :]
        df = dct * sn_ref[...]
        dbf_ref[...] += jnp.sum(df, axis=0, keepdims=True)
        duf_ref[:, POOL_WIDTH:] = df.astype(BF16)

    rev = lambda w: pl.BlockSpec((tm, w), lambda i: (nt - 1 - i, 0))
    return pl.pallas_call(
        body, name="sequence_grads", grid=(nt,),
        out_shape=[jax.ShapeDtypeStruct((lp, DUF_WIDTH), BF16), jax.ShapeDtypeStruct((1, LANES), F32)],
        in_specs=[rev(512),
                  pl.BlockSpec((MAX_WINDOW, 512), lambda i: (jnp.minimum((nt - i) * halo_blocks, last_halo), 0)),
                  rev(512), rev(LANES)],
        out_specs=[rev(DUF_WIDTH), pl.BlockSpec((1, LANES), lambda i: (0, 0))],
        scratch_shapes=[pltpu.VMEM((1, LANES), F32)],
        compiler_params=_params(("arbitrary",)),
    )(dpn, dpn, dc, sneg)


def _backward_in(x, tile0, norm_g, dh2, duf, dqkv, dmid, w_main, w_f, behind):
    seq = x.shape[0]
    tm = ROW_TILE
    nt = seq // tm + 1
    nx = len(behind)

    def body(x_ref, t0_ref, g_ref, dh2_ref, du_ref, df_ref, dqkv_ref, dzp_ref, dza_ref, dgp_ref, dga_ref,
             wm_ref, wf_ref, *rest):
        gx_ref, gmeta_ref, dg_ref = rest[nx:nx + 3]
        dproj_ref = rest[2 * nx + 3]
        finish_exchange = _behind(behind, rest[:nx], rest[nx + 3:2 * nx + 3], rest[2 * nx + 4:])
        t = pl.program_id(0)

        @pl.when(t == 0)
        def _():
            dg_ref[...] = jnp.zeros_like(dg_ref)

        dproj_ref[:, 0:512] = du_ref[...]
        dproj_ref[:, 512:1024] = dzp_ref[...]
        dproj_ref[:, 1024:1536] = dqkv_ref[0]
        dproj_ref[:, 1536:2048] = dqkv_ref[1]
        dproj_ref[:, 2048:2560] = dqkv_ref[2]
        dproj_ref[:, 2560:3072] = dza_ref[...]
        dproj_ref[:, 3072:4096] = dgp_ref[...]
        dproj_ref[:, 4096:5120] = dga_ref[...]
        dh = _dot(dproj_ref[...], wm_ref[...]) + _dot(df_ref[...], wf_ref[...])
        xt = jnp.where(t == 0, t0_ref[...], x_ref[...])
        r = lax.rsqrt(jnp.mean(xt * xt, axis=-1, keepdims=True) + RMS_EPS)
        xn = xt * r
        dg_ref[...] += jnp.sum(dh * xn, axis=0, keepdims=True)
        dhg = dh * g_ref[...]
        dx = dh2_ref[...] + r * (dhg - xn * jnp.mean(dhg * xn, axis=-1, keepdims=True))

        @pl.when(t > 0)
        def _():
            gx_ref[...] = dx

        @pl.when(t == 0)
        def _():
            gmeta_ref[...] = dx[PAD:, :]
            gx_ref[...] = jnp.zeros_like(gx_ref)

        finish_exchange()

    row = lambda w, j=0: pl.BlockSpec((tm, w), lambda i: (i, j))
    real = pl.BlockSpec((tm, D_MODEL), lambda i: (jnp.maximum(i - 1, 0), 0))
    hbm = pl.BlockSpec(memory_space=pl.ANY)
    in_specs = [
        real, _const((tm, D_MODEL)), _const((1, D_MODEL)), row(D_MODEL),
        row(POOL_WIDTH), row(LANES, POOL_WIDTH // LANES), pl.BlockSpec((3, tm, ATTN_WIDTH), lambda i: (0, i, 0)),
        row(512, 0), row(512, 1), row(1024, 1), row(1024, 2),
        _const((N_MAIN, D_MODEL)), _const((LANES, D_MODEL)),
    ] + [hbm] * nx
    sd = jax.ShapeDtypeStruct
    out_shape = [sd((seq, D_MODEL), F32), sd((N_META, D_MODEL), F32), sd((1, D_MODEL), F32)] + _exchange_results(behind)
    keep = lambda shape: pl.BlockSpec(shape, lambda i: (0,) * len(shape))
    out_specs = [real, keep((N_META, D_MODEL)), keep((1, D_MODEL))] + [hbm] * nx
    return pl.pallas_call(
        body, name="backward_in", grid=(nt,), out_shape=out_shape, in_specs=in_specs, out_specs=out_specs,
        scratch_shapes=[pltpu.VMEM((tm, N_MAIN), BF16)] + _exchange_semaphores(nx),
        compiler_params=_params(("arbitrary",)),
    )(x, tile0, norm_g, dh2, duf, duf, dqkv, dmid, dmid, dmid, dmid, w_main, w_f, *[a for _, a, _ in behind])


def _matmul_tn(name, a, b, tn):
    lp, m = a.shape
    n = b.shape[1]

    def body(a_ref, b_ref, c_ref):
        c_ref[...] = _dot_tn(a_ref[...].astype(BF16), b_ref[...].astype(BF16))

    return pl.pallas_call(
        body, name=name, grid=(n // tn,), out_shape=jax.ShapeDtypeStruct((m, n), F32),
        in_specs=[_const((lp, m)), pl.BlockSpec((lp, tn), lambda j: (0, j))],
        out_specs=pl.BlockSpec((m, tn), lambda j: (0, j)),
        compiler_params=_params(("arbitrary",)),
    )(a, b)


def _matmul_tn_rows(name, a, b, tm):
    lp, m = a.shape
    n = b.shape[1]

    def body(a_ref, b_ref, c_ref):
        c_ref[...] = _dot_tn(a_ref[...].astype(BF16), b_ref[...].astype(BF16))

    return pl.pallas_call(
        body, name=name, grid=(m // tm,), out_shape=jax.ShapeDtypeStruct((m, n), F32),
        in_specs=[pl.BlockSpec((lp, tm), lambda j: (0, j)), _const((lp, n))],
        out_specs=pl.BlockSpec((tm, n), lambda j: (j, 0)),
        compiler_params=_params(("arbitrary",)),
    )(a, b)


def _matmul_tn_stack(name, a, b):
    n_blocks, lp, m = a.shape
    n = b.shape[1]

    def body(a_ref, b_ref, c_ref):
        c_ref[...] = _dot_tn(a_ref[...], b_ref[...])

    return pl.pallas_call(
        body, name=name, grid=(n_blocks,), out_shape=jax.ShapeDtypeStruct((n_blocks * m, n), F32),
        in_specs=[pl.BlockSpec((None, lp, m), lambda j: (j, 0, 0)), _const((lp, n))],
        out_specs=pl.BlockSpec((m, n), lambda j: (j, 0)),
        compiler_params=_params(("arbitrary",)),
    )(a, b)


def _adamw_step(p_ref, w_ref, m_ref, v_ref, g_ref, d_ref, mo_ref, vo_ref):
    g = p_ref[0].astype(F32)
    for s in range(1, p_ref.shape[0]):
        g = g + p_ref[s].astype(F32)
    m_new = ADAM_B1 * m_ref[...] + (1.0 - ADAM_B1) * g
    v_new = ADAM_B2 * v_ref[...] + (1.0 - ADAM_B2) * (g * g)
    m_hat = m_new / (1.0 - ADAM_B1 ** ADAM_STEP)
    v_hat = v_new / (1.0 - ADAM_B2 ** ADAM_STEP)
    g_ref[...] = g
    d_ref[...] = -ADAM_LR * (m_hat / (jnp.sqrt(v_hat) + ADAM_EPS) + ADAM_WD * w_ref[...])
    mo_ref[...] = m_new
    vo_ref[...] = v_new


def _adamw_small(name, groups, loss_parts):
    n = len(groups)

    def body(*refs):
        ins, outs = refs[:4 * n + 1], refs[4 * n + 1:]
        for j in range(n):
            _adamw_step(*ins[4 * j:4 * j + 4], *outs[4 * j:4 * j + 4])
        total = ins[-1][0]
        for s in range(1, N_DEV):
            total = total + ins[-1][s]
        outs[-1][...] = total

    vmem = pl.BlockSpec(memory_space=pltpu.VMEM)
    out_shape = [jax.ShapeDtypeStruct(w.shape, F32) for _, w, _, _ in groups for _ in range(4)]
    out_shape.append(jax.ShapeDtypeStruct(loss_parts.shape[1:], F32))
    res = pl.pallas_call(
        body, name=name, out_shape=out_shape, in_specs=[vmem] * (4 * n + 1), out_specs=[vmem] * (4 * n + 1),
        compiler_params=_params(),
    )(*[a for g in groups for a in g], loss_parts)
    return [res[4 * j:4 * j + 4] for j in range(n)], res[-1]


def _adamw(name, parts, w, m, v, rows, cols=None):
    r, c_all = w.shape
    c = cols or c_all
    n_parts = parts.shape[0]

    def body(p_ref, w_ref, m_ref, v_ref, g_ref, d_ref, mo_ref, vo_ref):
        _adamw_step(p_ref, w_ref, m_ref, v_ref, g_ref, d_ref, mo_ref, vo_ref)

    blk = pl.BlockSpec((rows, c), lambda i, j: (i, j))
    return pl.pallas_call(
        body, name=name, grid=(r // rows, c_all // c), out_shape=[jax.ShapeDtypeStruct((r, c_all), F32)] * 4,
        in_specs=[pl.BlockSpec((n_parts, rows, c), lambda i, j: (0, i, j)), blk, blk, blk],
        out_specs=[blk] * 4,
        compiler_params=_params(("arbitrary", "arbitrary")),
    )(parts, w, m, v)


def _pair_sum(name, mine, theirs, rows):
    n, r, c = mine.shape

    def body(a_ref, b_ref, o_ref):
        o_ref[...] = (a_ref[...].astype(F32) + b_ref[...].astype(F32)).astype(BF16)

    blk = pl.BlockSpec((1, rows, c), lambda j, i: (j, i, 0))
    return pl.pallas_call(
        body, name=name, grid=(n, r // rows), out_shape=jax.ShapeDtypeStruct((n, r, c), BF16),
        in_specs=[blk, blk], out_specs=blk,
        compiler_params=_params(("arbitrary", "arbitrary")),
    )(mine, theirs)


def _by_core(slots):
    by_core = slots.reshape((4, 2) + slots.shape[1:]).swapaxes(0, 1)
    c = lax.axis_index("c")
    return (lax.dynamic_index_in_dim(by_core, c, 0, keepdims=False),
            lax.dynamic_index_in_dim(by_core, 1 - c, 0, keepdims=False))


def _columns_to_slots(a):
    r, c8 = a.shape
    return a.reshape(r, N_DEV, c8 // N_DEV).transpose(1, 0, 2)


def _slots_to_columns(a):
    n, r, c = a.shape
    return a.transpose(1, 0, 2).reshape(r, n * c)


def kernel(x, meta_tokens, norm_g, w_in, b_forget, pool_w, pool_scale, w_up_pool, w_up_attn, w_out, final_norm_g, loss_target, m_meta_tokens, m_norm_g, m_w_in, m_b_forget, m_pool_w, m_pool_scale, m_w_up_pool, m_w_up_attn, m_w_out, m_final_norm_g, v_meta_tokens, v_norm_g, v_w_in, v_b_forget, v_pool_w, v_pool_scale, v_w_up_pool, v_w_up_attn, v_w_out, v_final_norm_g):
    xs = x[0]
    target = loss_target[0]

    g_in, g_meta = _gather_two_level("gather_weights", [w_in[0].T.astype(BF16), meta_tokens], (320, 8))
    w_full = g_in.reshape(N_DEV * g_in.shape[1], D_MODEL)
    w_main = jnp.concatenate([w_full[:N_BEFORE_F], w_full[N_BEFORE_F + N_HEADS:]], axis=0)
    w_f = jnp.pad(w_full[N_BEFORE_F:N_BEFORE_F + N_HEADS], ((0, LANES - N_HEADS), (0, 0)))
    meta = _slots_to_columns(g_meta)
    tile0 = jnp.concatenate([jnp.zeros((PAD, D_MODEL), F32), meta], axis=0)
    b_f = jnp.pad(b_forget, ((0, 0), (0, LANES - N_HEADS)))
    pw_b = pool_w[0].astype(BF16)
    final_g = final_norm_g.reshape(1, D_MODEL)

    h, u, zp, k, v, qt, kt, vt, sneg = _forward_in(xs, tile0, norm_g, w_main, w_f, b_f)
    o, lse, g_upp, g_upa, g_out = _attention_forward(
        qt, k, vt, [("gather", w.astype(BF16), ALL_PEERS) for w in (w_up_pool[0], w_up_attn[0], w_out[0])])
    wupp = _slots_to_columns(g_upp)
    wupa = _slots_to_columns(g_upa)
    wout = g_out.reshape(D_MODEL, D_MODEL)
    (dh2, mg, yp, ya, dap, daa, do, dmid, dpn,
     loss_part, d_final_g, d_scale, d_pool_w) = _middle(xs, target, h, o, u, zp, w_main, wupp, wupa, wout,
                                                        pw_b, pool_scale, final_g)
    dw_out = _matmul_tn("grad_w_out", mg, dh2, 256)
    dw_upp = _matmul_tn("grad_w_up_pool", yp, dap, 512)
    dw_upa = _matmul_tn("grad_w_up_attn", ya, daa, 512)
    dqkv, dc, p_upp, p_upa, p_out, p_pool_w, p_scale, p_final_g = _attention_backward(
        qt, k, kt, v, do, o, lse,
        [("scatter", _columns_to_slots(dw_upp).astype(BF16), ALL_PEERS),
         ("scatter", _columns_to_slots(dw_upa).astype(BF16), ALL_PEERS),
         ("scatter", dw_out.reshape(N_DEV, D_MODEL // N_DEV, D_MODEL).astype(BF16), ALL_PEERS),
         ("gather", d_pool_w.reshape(4 * POOL_GROUP, POOL_GROUP), ALL_PEERS),
         ("gather", d_scale, ALL_PEERS), ("gather", d_final_g, ALL_PEERS)])
    duf, d_bf = _sequence_grads(dpn, dc, sneg)
    g_uf = _matmul_tn_rows("grad_w_in_pool_forget", duf, h, DUF_WIDTH)
    g_qkv = _matmul_tn_stack("grad_w_in_attention", dqkv, h)
    g_mid = _matmul_tn_rows("grad_w_in_gates", dmid, h, 512)
    dw_in = jnp.concatenate([g_uf[:POOL_WIDTH], g_mid[:MID_ZA], g_qkv, g_mid[MID_ZA:MID_GP],
                             g_uf[POOL_WIDTH:POOL_WIDTH + N_HEADS], g_mid[MID_GP:]], axis=0)
    dw_in = dw_in.reshape(N_DEV, dw_in.shape[0] // N_DEV, D_MODEL)
    mine, for_sibling = _by_core(dw_in)
    from_sibling, = _exchange("swap_with_sibling", [("swap", for_sibling.astype(BF16), (SIBLING,))])
    pair_sums = _pair_sum("pair_sum", mine, from_sibling, dw_in.shape[1])
    grad_x, d_meta, d_norm_g, p_in, p_bf, p_loss = _backward_in(
        xs, tile0, norm_g, dh2, duf, dqkv, dmid, w_main, w_f,
        [("chips", pair_sums, SAME_CORE), ("gather", d_bf, ALL_PEERS), ("gather", loss_part, ALL_PEERS)])
    p_meta, p_norm_g = _exchange(
        "exchange_gradients", [("scatter", _columns_to_slots(d_meta), ALL_PEERS), ("gather", d_norm_g, ALL_PEERS)])


    def pad_f(a):
        return jnp.pad(a, ((0, 0), (0, LANES - N_HEADS)))

    res = {}
    res["w_in"] = [a.T for a in _adamw("adamw_w_in", p_in, w_in[0].T, m_w_in[0].T, v_w_in[0].T, p_in.shape[1], 256)]
    res["w_up_pool"] = _adamw("adamw_w_up_pool", p_upp, w_up_pool[0], m_w_up_pool[0], v_w_up_pool[0], 512)
    res["w_up_attn"] = _adamw("adamw_w_up_attn", p_upa, w_up_attn[0], m_w_up_attn[0], v_w_up_attn[0], 512)
    res["w_out"] = _adamw("adamw_w_out", p_out, w_out[0], m_w_out[0], v_w_out[0], 128)
    flat = lambda a: a.reshape(4 * POOL_GROUP, POOL_GROUP)
    row = lambda a: a.reshape(1, D_MODEL)
    small, loss_row = _adamw_small(
        "adamw_small",
        [(p_meta, meta_tokens, m_meta_tokens, v_meta_tokens),
         (p_norm_g, norm_g, m_norm_g, v_norm_g),
         (p_bf, pad_f(b_forget), pad_f(m_b_forget), pad_f(v_b_forget)),
         (p_pool_w, flat(pool_w), flat(m_pool_w), flat(v_pool_w)),
         (p_scale, pool_scale, m_pool_scale, v_pool_scale),
         (p_final_g, final_g, row(m_final_norm_g), row(v_final_norm_g))],
        p_loss)
    res["meta_tokens"], res["norm_g"], bf, pw, res["pool_scale"], fg = small
    res["b_forget"] = [a[:, :N_HEADS] for a in bf]
    res["pool_w"] = [a.reshape(pool_w.shape) for a in pw]
    res["final_norm_g"] = [a.reshape(D_MODEL) for a in fg]
    loss = loss_row[0, 0]
    for name in ("w_in", "w_up_pool", "w_up_attn", "w_out"):
        res[name] = [a[None] for a in res[name]]

    order = ["meta_tokens", "norm_g", "w_in", "b_forget", "pool_w", "pool_scale", "w_up_pool", "w_up_attn", "w_out",
             "final_norm_g"]
    outs = [loss, grad_x[None]]
    for part in range(4):
        outs += [res[name][part] for name in order]
    return tuple(outs)
```

```python
import jax
import jax.numpy as jnp
from jax import lax
from jax.experimental import pallas as pl
from jax.experimental.pallas import tpu as pltpu

F32 = jnp.float32
BF16 = jnp.bfloat16

D_MODEL = 1024
N_META = 16
POOL_WIDTH = 512
ATTN_WIDTH = 512
N_HEADS = 8
HEAD_DIM = 64
POOL_WINDOWS = (2, 4, 8, 16)
POOL_GROUP = 128
MAX_WINDOW = 16
RMS_EPS = 1e-6
N_MAIN = 5120
N_BEFORE_F = 3072
N_DEV = 8
LANES = 128

ROW_TILE = 256
ATT_TILE = 256
ATT_Q_BLOCKS_FWD = 8
ATT_Q_BLOCKS_BWD = 4
PAD = ROW_TILE - N_META
FIRST_KEY = PAD // LANES * LANES
VMEM_LIMIT = 56 * 1024 * 1024

ADAM_LR = 0.001
ADAM_B1 = 0.9
ADAM_B2 = 0.999
ADAM_EPS = 1e-08
ADAM_WD = 0.01
ADAM_STEP = 10

MID_ZA, MID_GP, MID_GA, MID_WIDTH = 512, 1024, 2048, 3072
NEG = -1e30
MESH = pl.DeviceIdType.MESH


def _params(sem=None):
    kw = dict(vmem_limit_bytes=VMEM_LIMIT)
    if sem is not None:
        kw["dimension_semantics"] = sem
    return pltpu.CompilerParams(**kw)


def _const(shape, block_index=None):
    idx = block_index or (0,) * len(shape)
    return pl.BlockSpec(shape, lambda i: idx, pipeline_mode=pl.Buffered(1))


def _sigmoid(x):
    return jax.nn.sigmoid(x)


def _dot(a, b):
    return jnp.dot(a, b, preferred_element_type=F32)


def _dot_nt(a, b):
    return lax.dot_general(a, b, (((1,), (1,)), ((), ())), preferred_element_type=F32)


def _dot_tn(a, b):
    return lax.dot_general(a, b, (((0,), (0,)), ((), ())), preferred_element_type=F32)


def _pool_counts(first_row, rows):
    row = first_row + lax.broadcasted_iota(jnp.int32, (rows, 1), 0)
    pos1 = row - PAD + 1
    return [jnp.clip(pos1, 1, w).astype(F32) for w in POOL_WINDOWS]


def _pool_means(u_ext, u, counts):
    rows = u.shape[0]
    out = []
    for g, w in enumerate(POOL_WINDOWS):
        s = u_ext[:, POOL_GROUP * g:POOL_GROUP * (g + 1)]
        sh = 1
        while sh < w:
            s = s + pltpu.roll(s, sh, axis=0)
            sh *= 2
        out.append(s[MAX_WINDOW:MAX_WINDOW + rows, :] / counts[g] - u[:, POOL_GROUP * g:POOL_GROUP * (g + 1)])
    return out


Q_BIAS, Q_ONES, Q_LSE = 64, 67, 70
K_ONES, K_BIAS, K_ONES2 = 64, 67, 70
V_ONES = 64
DO_BIAS = 64


def _lane_ones(lane, ranges):
    hit = None
    for lo, hi in ranges:
        r = (lane >= lo) & (lane < hi)
        hit = r if hit is None else hit | r
    return jnp.where(hit, 1.0, 0.0)


def _put3(base, lane, first, x):
    hi = x.astype(BF16).astype(F32)
    rest = x - hi
    mid = rest.astype(BF16).astype(F32)
    lo = (rest - mid).astype(BF16).astype(F32)
    for j, piece in enumerate((hi, mid, lo)):
        base = jnp.where(lane == first + j, piece, base)
    return base


SIBLING = 1
SAME_CORE = (2, 4, 6)
ALL_PEERS = (1, 2, 3, 4, 5, 6, 7)


def _place():
    return lax.axis_index("x"), lax.axis_index("y"), lax.axis_index("c")


def _peer(r):
    x, y, c = _place()
    return (1 - x if r & 4 else x, 1 - y if r & 2 else y, 1 - c if r & 1 else c)


def _device_slot(p):
    return 4 * p[0] + 2 * p[1] + p[2]


def _chip_slot(p):
    return 2 * p[0] + p[1]


def _exchange(name, items):
    n = len(items)

    def body(*refs):
        copies = _exchange_copies(items, refs[:n], refs[n:2 * n], *refs[2 * n:])
        for cp in copies:
            cp.start()
        for cp in copies:
            cp.wait()

    hbm = pl.BlockSpec(memory_space=pl.ANY)
    return pl.pallas_call(
        body, name=name, out_shape=_exchange_results(items),
        in_specs=[hbm] * n, out_specs=[hbm] * n,
        scratch_shapes=_exchange_semaphores(n),
    )(*[a for _, a, _ in items])


def _exchange_results(items):
    return [jax.ShapeDtypeStruct(((N_DEV,) if kind == "gather" else ()) + a.shape, a.dtype) for kind, a, _ in items]


def _exchange_semaphores(n):
    return [pltpu.SemaphoreType.DMA((n, N_DEV - 1)), pltpu.SemaphoreType.DMA((n, N_DEV - 1)),
            pltpu.SemaphoreType.DMA((n,))]


def _exchange_copies(items, ins, outs, send_sems, recv_sems, local_sems):
    me = _place()
    copies = []
    for a, (kind, _, peers) in enumerate(items):
        slot = _chip_slot if kind == "chips" else _device_slot
        for r in peers:
            peer = _peer(r)
            src = ins[a] if kind in ("swap", "gather") else ins[a].at[slot(peer)]
            dst = outs[a] if kind == "swap" else outs[a].at[slot(me)]
            copies.append(pltpu.make_async_remote_copy(
                src_ref=src, dst_ref=dst, send_sem=send_sems.at[a, r - 1], recv_sem=recv_sems.at[a, r - 1],
                device_id=peer, device_id_type=MESH))
        if kind != "swap":
            src = ins[a] if kind == "gather" else ins[a].at[slot(me)]
            copies.append(pltpu.make_async_copy(src, outs[a].at[slot(me)], local_sems.at[a]))
    return copies


def _gather_two_level(name, arrays, halves):
    n = len(arrays)
    x_flip, y_flip, both = 4, 2, 6

    def body(*refs):
        ins, outs = refs[:n], refs[n:2 * n]
        send_sems, recv_sems, local_sems = refs[2 * n:]
        me, sibling = _place(), _peer(SIBLING)
        xn, yn, dg = _peer(x_flip), _peer(y_flip), _peer(both)

        def part(a, block, half):
            rows = outs[a].at[_device_slot(block)]
            if half is None:
                return rows
            return rows.at[pl.ds(0, halves[a])] if half == 0 else rows.at[pl.ds(halves[a], arrays[a].shape[0] - halves[a])]

        def copy(a, k, block, half, to, src=None):
            dst = part(a, block, half)
            return pltpu.make_async_remote_copy(
                src_ref=dst if src is None else src, dst_ref=dst,
                send_sem=send_sems.at[a, k], recv_sem=recv_sems.at[a, k], device_id=to, device_id_type=MESH)

        sends, own = [], []

        def start(cp):
            cp.start()
            sends.append(cp)

        for a in range(n):
            mine = pltpu.make_async_copy(ins[a], outs[a].at[_device_slot(me)], local_sems.at[a])
            mine.start()
            own.append(mine)
            for k, to in enumerate((sibling, xn, yn)):
                start(copy(a, k, me, None, to, src=ins[a]))
        for a in range(n):
            copy(a, 1, xn, None, me).wait_recv()
            start(copy(a, 3, xn, 0, yn))
            start(copy(a, 5, xn, None, sibling))
        for a in range(n):
            copy(a, 2, yn, None, me).wait_recv()
            start(copy(a, 4, yn, 1, xn))
            start(copy(a, 6, yn, None, sibling))
        for a in range(n):
            copy(a, 3, dg, 0, me).wait_recv()
            copy(a, 4, dg, 1, me).wait_recv()
            start(copy(a, 7, dg, None, sibling))
        for a in range(n):
            copy(a, 0, sibling, None, me).wait_recv()
            for k, r in ((5, x_flip), (6, y_flip), (7, both)):
                copy(a, k, _peer(r | SIBLING), None, me).wait_recv()
        for cp in sends:
            cp.wait_send()
        for cp in own:
            cp.wait()

    hbm = pl.BlockSpec(memory_space=pl.ANY)
    return pl.pallas_call(
        body, name=name, out_shape=[jax.ShapeDtypeStruct((N_DEV,) + a.shape, a.dtype) for a in arrays],
        in_specs=[hbm] * n, out_specs=[hbm] * n,
        scratch_shapes=[pltpu.SemaphoreType.DMA((n, 8)), pltpu.SemaphoreType.DMA((n, 8)),
                        pltpu.SemaphoreType.DMA((n,))],
    )(*arrays)


def _forward_in(x, tile0, norm_g, w_main, w_f, b_f):
    seq = x.shape[0]
    nt = seq // ROW_TILE + 1
    lp = nt * ROW_TILE
    tm = ROW_TILE

    def body(x_ref, t0_ref, g_ref, wa_ref, wf_ref, bf_ref,
             h_ref, u_ref, zp_ref, k_ref, v_ref, qt_ref, kt_ref, vt_ref, sn_ref, carry_ref):
        i = pl.program_id(0)

        @pl.when(i == 0)
        def _():
            carry_ref[...] = jnp.zeros_like(carry_ref)

        xt = jnp.where(i == 0, t0_ref[...], x_ref[...])
        r = lax.rsqrt(jnp.mean(xt * xt, axis=-1, keepdims=True) + RMS_EPS)
        h = (xt * r * g_ref[...]).astype(BF16)
        h_ref[...] = h
        pa = _dot_nt(h, wa_ref[...])
        u_ref[...] = pa[:, :512]
        zp_ref[...] = pa[:, 512:1024]

        fl = _dot_nt(h, wf_ref[...]) + bf_ref[...]
        row = i * tm + lax.broadcasted_iota(jnp.int32, (tm, LANES), 0)
        rloc = lax.broadcasted_iota(jnp.int32, (tm, LANES), 0)
        lane = lax.broadcasted_iota(jnp.int32, (tm, LANES), 1)
        live = (row >= PAD) & (lane < N_HEADS)
        logf = jnp.minimum(fl, 0.0) - jnp.log1p(jnp.exp(-jnp.abs(fl)))
        cs = jnp.where(live, logf, 0.0)
        sh = 1
        while sh < tm:
            cs = cs + jnp.where(rloc >= sh, pltpu.roll(cs, sh, axis=0), 0.0)
            sh *= 2
        cs = cs + carry_ref[...]
        carry_ref[...] = cs[tm - 1:tm, :]
        sn_ref[...] = jnp.where(live, _sigmoid(-fl), 0.0)

        rows1 = i * tm + lax.broadcasted_iota(jnp.int32, (tm, 1), 0)
        ones_q = _lane_ones(lane, ((Q_ONES, Q_ONES + 3),))
        ones_k = _lane_ones(lane, ((K_ONES, K_ONES + 3), (K_ONES2, K_ONES2 + 3)))
        ones_v = _lane_ones(lane, ((V_ONES, V_ONES + 3),))
        for hp in range(N_HEADS // 2):
            qp = pa[:, 1024 + LANES * hp:1024 + LANES * (hp + 1)] * 0.125
            kp = pa[:, 1536 + LANES * hp:1536 + LANES * (hp + 1)]
            vp = pa[:, 2048 + LANES * hp:2048 + LANES * (hp + 1)]
            for e in range(2):
                head = 2 * hp + e
                if e:
                    qp, kp, vp = (pltpu.roll(a, HEAD_DIM, axis=1) for a in (qp, kp, vp))
                c_h = cs[:, head:head + 1]
                q_h = jnp.where(lane < HEAD_DIM, qp, _put3(ones_q, lane, Q_BIAS, c_h))
                qt_ref[head] = q_h.T.astype(BF16)
                minus_ck = jnp.where(rows1 >= PAD, -c_h, NEG)
                k_h = jnp.where(lane < HEAD_DIM, kp, _put3(ones_k, lane, K_BIAS, minus_ck))
                k_ref[head] = k_h.astype(BF16)
                kt_ref[head] = k_h.T.astype(BF16)
                v_h = jnp.where(lane < HEAD_DIM, vp, ones_v)
                v_ref[head] = v_h.astype(BF16)
                vt_ref[head] = v_h.T.astype(BF16)

    row_f32 = lambda w: pl.BlockSpec((tm, w), lambda i: (i, 0))
    out_shape = [
        jax.ShapeDtypeStruct((lp, D_MODEL), BF16),
        jax.ShapeDtypeStruct((lp, POOL_WIDTH), F32),
        jax.ShapeDtypeStruct((lp, POOL_WIDTH), F32),
        jax.ShapeDtypeStruct((N_HEADS, lp, LANES), BF16),
        jax.ShapeDtypeStruct((N_HEADS, lp, LANES), BF16),
        jax.ShapeDtypeStruct((N_HEADS, LANES, lp), BF16),
        jax.ShapeDtypeStruct((N_HEADS, LANES, lp), BF16),
        jax.ShapeDtypeStruct((N_HEADS, LANES, lp), BF16),
        jax.ShapeDtypeStruct((lp, LANES), F32),
    ]
    heads = pl.BlockSpec((N_HEADS, tm, LANES), lambda i: (0, i, 0))
    heads_t = pl.BlockSpec((N_HEADS, LANES, tm), lambda i: (0, 0, i))
    out_specs = [row_f32(D_MODEL), row_f32(512), row_f32(512), heads, heads, heads_t, heads_t, heads_t,
                 row_f32(LANES)]
    in_specs = [
        pl.BlockSpec((tm, D_MODEL), lambda i: (jnp.maximum(i - 1, 0), 0)),
        _const((tm, D_MODEL)), _const((1, D_MODEL)),
        _const((2560, D_MODEL)), _const((LANES, D_MODEL)), _const((1, LANES)),
    ]
    return pl.pallas_call(
        body, name="forward_in", grid=(nt,), out_shape=out_shape, in_specs=in_specs, out_specs=out_specs,
        scratch_shapes=[pltpu.VMEM((1, LANES), F32)],
        compiler_params=_params(("arbitrary",)),
    )(x, tile0, norm_g, w_main, w_f, b_f)


def _pair_lanes(a0, a1):
    lane = lax.broadcasted_iota(jnp.int32, a0.shape, 1)
    return jnp.where(lane < HEAD_DIM, a0, pltpu.roll(a1, HEAD_DIM, axis=1))


def _behind(items, ins, outs, sems):
    step, last = pl.program_id(0), pl.num_programs(0) - 1

    @pl.when(step == 0)
    def _():
        for cp in _exchange_copies(items, ins, outs, *sems):
            cp.start()

    def finish():
        @pl.when(step == last)
        def _():
            for cp in _exchange_copies(items, ins, outs, *sems):
                cp.wait()

    return finish


def _attention_forward(qt, k, vt, behind):
    lp = k.shape[1]
    tk = ATT_TILE
    q_blocks = ATT_Q_BLOCKS_FWD if (lp // tk - 1) % ATT_Q_BLOCKS_FWD == 0 else ATT_Q_BLOCKS_BWD
    tq_big = q_blocks * tk
    n_big = (lp // tk - 1) // q_blocks
    assert lp == tk + n_big * tq_big and q_blocks % 2 == 0
    nx = len(behind)

    def body(qt_ref, k_ref, vt_ref, *rest):
        o_ref, lse_ref = rest[nx:nx + 2]
        s_buf, m_scr, acc_scr = rest[2 * nx + 2:2 * nx + 5]
        finish_exchange = _behind(behind, rest[:nx], rest[nx + 2:2 * nx + 2], rest[2 * nx + 5:])

        def q_tile(q0, tq, pairs):
            first = q0 // tk
            qts = [qt_ref[e, :, pl.ds(q0, tq)] for e in range(2)]

            def block(kj):
                return pl.ds(kj * tk if isinstance(kj, int) else pl.multiple_of(kj * tk, tk), tk)

            def step(kj, rd, wr, c0=0, diagonal=False, keys=None):
                c1 = c0 + tk if diagonal else c0
                keys = block(kj) if keys is None else keys
                for e in range(2):
                    s = s_buf[rd, e, 0:keys.size, c0:tq]
                    if wr is not None:
                        s_buf[wr, e, :, c1:tq] = _dot(k_ref[e, block(kj + 1), :], qts[e][:, c1:tq])
                    if diagonal:
                        key = lax.broadcasted_iota(jnp.int32, s.shape, 0)
                        s = jnp.where(key <= lax.broadcasted_iota(jnp.int32, s.shape, 1), s, NEG)
                    m = m_scr[e, :, c0:tq]
                    m_new = jnp.maximum(m, jnp.max(s, axis=0, keepdims=True))
                    p = jnp.exp(s - m_new)
                    pv = _dot(vt_ref[e, :, keys], p.astype(BF16))
                    acc_scr[e, :, c0:tq] = jnp.exp(m - m_new) * acc_scr[e, :, c0:tq] + pv
                    m_scr[e, :, c0:tq] = m_new

            keys0 = block(0) if pairs is None else pl.ds(FIRST_KEY, tk - FIRST_KEY)
            for e in range(2):
                m_scr[e, :, 0:tq] = jnp.full((1, tq), NEG, F32)
                acc_scr[e, :, 0:tq] = jnp.zeros((LANES, tq), F32)
                s_buf[0, e, 0:keys0.size, 0:tq] = _dot(k_ref[e, keys0, :], qts[e])
            if pairs is None:
                step(0, 0, None, 0, True)
            else:
                step(0, 0, 1, keys=keys0)

                def two_steps(t, _):
                    step(1 + 2 * t, 1, 0)
                    step(2 + 2 * t, 0, 1)
                    return 0

                lax.fori_loop(0, pairs, two_steps, 0)
                for b in range(tq // tk):
                    step(first + b, (b + 1) % 2, b % 2 if (b + 1) * tk < tq else None, b * tk, True)
            outs, lses = [], []
            for e in range(2):
                acc = acc_scr[e, :, 0:tq]
                l = acc[V_ONES:V_ONES + 1, :]
                outs.append((acc / l).T)
                lses.append(m_scr[e, :, 0:tq] + jnp.log(l))
            o_ref[pl.ds(q0, tq), :] = _pair_lanes(outs[0], outs[1]).astype(BF16)
            lse_rows = jnp.concatenate(lses + [jnp.zeros((LANES - 2, tq), F32)], axis=0)
            lse_ref[pl.ds(q0, tq), :] = lse_rows.T

        q_tile(0, tk, None)

        def big_tile(i, _):
            q_tile(pl.multiple_of(tk + i * tq_big, tk), tq_big, (q_blocks // 2) * i)
            return 0

        lax.fori_loop(0, n_big, big_tile, 0)
        finish_exchange()

    pair = pl.BlockSpec((lp, LANES), lambda hp: (0, hp))
    heads = pl.BlockSpec((2, lp, LANES), lambda hp: (hp, 0, 0), pipeline_mode=pl.Buffered(1))
    heads_t = pl.BlockSpec((2, LANES, lp), lambda hp: (hp, 0, 0), pipeline_mode=pl.Buffered(1))
    hbm = pl.BlockSpec(memory_space=pl.ANY)
    return pl.pallas_call(
        body, name="attention_forward", grid=(N_HEADS // 2,),
        out_shape=[jax.ShapeDtypeStruct((lp, ATTN_WIDTH), BF16), jax.ShapeDtypeStruct((lp, ATTN_WIDTH), F32)]
        + _exchange_results(behind),
        in_specs=[heads_t, heads, heads_t] + [hbm] * nx,
        out_specs=[pair, pair] + [hbm] * nx,
        scratch_shapes=[pltpu.VMEM((2, 2, tk, tq_big), F32), pltpu.VMEM((2, 1, tq_big), F32),
                        pltpu.VMEM((2, LANES, tq_big), F32)] + _exchange_semaphores(nx),
        compiler_params=_params(("arbitrary",)),
    )(qt, k, vt, *[a for _, a, _ in behind])


def _rows3(first, x):
    sub = lax.broadcasted_iota(jnp.int32, (LANES, x.shape[1]), 0)
    hi = x.astype(BF16).astype(F32)
    rest = x - hi
    mid = rest.astype(BF16).astype(F32)
    lo = (rest - mid).astype(BF16).astype(F32)
    out = jnp.zeros((LANES, x.shape[1]), F32)
    for j, piece in enumerate((hi, mid, lo)):
        out = jnp.where(sub == first + j, piece, out)
    return out


def _attention_backward(qt, k, kt, v, do, o, lse, behind):
    lp = k.shape[1]
    tb = ATT_TILE
    nb = lp // tb
    tq_big = ATT_Q_BLOCKS_BWD * tb
    n_big = (nb - 1) // ATT_Q_BLOCKS_BWD
    assert lp == tb + n_big * tq_big and ATT_Q_BLOCKS_BWD % 2 == 0
    nx = len(behind)

    def body(qt_ref, k_ref, kt_ref, v_ref, do_ref, o_ref, lse_ref, *rest):
        dqkv_ref, dc_ref = rest[nx:nx + 2]
        q2_ref, do2_ref, dk_acc, dv_acc, dq_scr, s_buf = rest[2 * nx + 2:2 * nx + 8]
        finish_exchange = _behind(behind, rest[:nx], rest[nx + 2:2 * nx + 2], rest[2 * nx + 8:])
        sub = lax.broadcasted_iota(jnp.int32, (LANES, tb), 0)

        def lanes01(row0, row1):
            n = row0.shape[1]
            return jnp.concatenate([row0, row1, jnp.zeros((LANES - 2, n), F32)], axis=0).T

        def prepare(bi, _):
            r0 = pl.multiple_of(bi * tb, tb)
            queries = r0 + lax.broadcasted_iota(jnp.int32, (1, tb), 1)
            dob = do_ref[pl.ds(r0, tb), :].astype(F32)
            do_t = dob.T
            dd_t = (dob * o_ref[pl.ds(r0, tb), :].astype(F32)).T
            lse_t = lse_ref[pl.ds(r0, tb), :].T
            for e in range(2):
                delta = jnp.sum(dd_t[HEAD_DIM * e:HEAD_DIM * (e + 1), :], axis=0, keepdims=True)
                do_e = jnp.concatenate([do_t[HEAD_DIM * e:HEAD_DIM * (e + 1), :], jnp.zeros((HEAD_DIM, tb), F32)], axis=0)
                do2_ref[e, :, pl.ds(r0, tb)] = jnp.where(sub < HEAD_DIM, do_e, _rows3(DO_BIAS, -delta)).astype(BF16)
                minus_lse = jnp.where(queries >= PAD, -lse_t[e:e + 1, :], NEG)
                keep = (sub < Q_LSE) | (sub >= Q_LSE + 3)
                q2_ref[e, :, pl.ds(r0, tb)] = jnp.where(keep, qt_ref[e, :, pl.ds(r0, tb)].astype(F32),
                                                        _rows3(Q_LSE, minus_lse)).astype(BF16)
            return 0

        lax.fori_loop(0, nb, prepare, 0)
        dk_acc[...] = jnp.zeros_like(dk_acc)
        dv_acc[...] = jnp.zeros_like(dv_acc)

        def q_tile(q0, tq, pairs):
            first = q0 // tb
            qts = [q2_ref[e, :, pl.ds(q0, tq)] for e in range(2)]
            dots = [do2_ref[e, :, pl.ds(q0, tq)] for e in range(2)]

            def block(kj):
                return pl.ds(kj * tb if isinstance(kj, int) else pl.multiple_of(kj * tb, tb), tb)

            def step(kj, rd, wr, c0=0, diagonal=False, keys=None):
                c1 = c0 + tb if diagonal else c0
                keys = block(kj) if keys is None else keys
                for e in range(2):
                    s = s_buf[rd, e, 0:keys.size, c0:tq]
                    if wr is not None:
                        s_buf[wr, e, :, c1:tq] = _dot(k_ref[e, block(kj + 1), :], qts[e][:, c1:tq])
                    dpd = _dot(v_ref[e, keys, :], dots[e][:, c0:tq])
                    p = jnp.exp(s)
                    if diagonal:
                        key = lax.broadcasted_iota(jnp.int32, s.shape, 0)
                        p = jnp.where(key <= lax.broadcasted_iota(jnp.int32, s.shape, 1), p, 0.0)
                    dsb = (p * dpd).astype(BF16)
                    dv_acc[e, :, keys] += _dot_nt(dots[e][:, c0:tq], p.astype(BF16))
                    dk_acc[e, :, keys] += _dot_nt(qts[e][:, c0:tq], dsb)
                    dq_scr[e, :, c0:tq] += _dot(kt_ref[e, :, keys], dsb)

            keys0 = block(0) if pairs is None else pl.ds(FIRST_KEY, tb - FIRST_KEY)
            for e in range(2):
                dq_scr[e, :, 0:tq] = jnp.zeros((LANES, tq), F32)
                s_buf[0, e, 0:keys0.size, 0:tq] = _dot(k_ref[e, keys0, :], qts[e])
            if pairs is None:
                step(0, 0, None, 0, True)
            else:
                step(0, 0, 1, keys=keys0)

                def two_steps(t, _):
                    step(1 + 2 * t, 1, 0)
                    step(2 + 2 * t, 0, 1)
                    return 0

                lax.fori_loop(0, pairs, two_steps, 0)
                for b in range(tq // tb):
                    step(first + b, (b + 1) % 2, b % 2 if (b + 1) * tb < tq else None, b * tb, True)
            dq0, dq1 = dq_scr[0, :, 0:tq], dq_scr[1, :, 0:tq]
            dqkv_ref[0, pl.ds(q0, tq), :] = (_pair_lanes(dq0.T, dq1.T) * 0.125).astype(BF16)
            dc_ref[pl.ds(q0, tq), :] = lanes01(dq0[K_ONES:K_ONES + 1, :], dq1[K_ONES:K_ONES + 1, :])

        q_tile(0, tb, None)

        def big_tile(i, _):
            q_tile(pl.multiple_of(tb + i * tq_big, tb), tq_big, (ATT_Q_BLOCKS_BWD // 2) * i)
            return 0

        lax.fori_loop(0, n_big, big_tile, 0)

        def finish(bi, _):
            r0 = pl.multiple_of(bi * tb, tb)
            dk0, dk1 = dk_acc[0, :, pl.ds(r0, tb)], dk_acc[1, :, pl.ds(r0, tb)]
            dqkv_ref[1, pl.ds(r0, tb), :] = _pair_lanes(dk0.T, dk1.T).astype(BF16)
            dqkv_ref[2, pl.ds(r0, tb), :] = _pair_lanes(dv_acc[0, :, pl.ds(r0, tb)].T,
                                                        dv_acc[1, :, pl.ds(r0, tb)].T).astype(BF16)
            dc_ref[pl.ds(r0, tb), :] = dc_ref[pl.ds(r0, tb), :] - lanes01(dk0[Q_ONES:Q_ONES + 1, :], dk1[Q_ONES:Q_ONES + 1, :])
            return 0

        lax.fori_loop(0, nb, finish, 0)
        finish_exchange()

    once = pl.Buffered(1)
    pair = pl.BlockSpec((lp, LANES), lambda hp: (0, hp))
    pair_in = pl.BlockSpec((lp, LANES), lambda hp: (0, hp), pipeline_mode=once)
    heads = pl.BlockSpec((2, lp, LANES), lambda hp: (hp, 0, 0), pipeline_mode=once)
    heads_t = pl.BlockSpec((2, LANES, lp), lambda hp: (hp, 0, 0), pipeline_mode=once)
    hbm = pl.BlockSpec(memory_space=pl.ANY)
    return pl.pallas_call(
        body, name="attention_backward", grid=(N_HEADS // 2,),
        out_shape=[jax.ShapeDtypeStruct((3, lp, ATTN_WIDTH), BF16), jax.ShapeDtypeStruct((lp, ATTN_WIDTH), F32)]
        + _exchange_results(behind),
        in_specs=[heads_t, heads, heads_t, heads, pair_in, pair_in, pair_in] + [hbm] * nx,
        out_specs=[pl.BlockSpec((3, lp, LANES), lambda hp: (0, 0, hp)), pair] + [hbm] * nx,
        scratch_shapes=[pltpu.VMEM((2, LANES, lp), BF16), pltpu.VMEM((2, LANES, lp), BF16),
                        pltpu.VMEM((2, LANES, lp), F32), pltpu.VMEM((2, LANES, lp), F32),
                        pltpu.VMEM((2, LANES, tq_big), F32), pltpu.VMEM((2, 2, tb, tq_big), F32)]
        + _exchange_semaphores(nx),
        compiler_params=_params(("arbitrary",)),
    )(qt, k, kt, v, do, o, lse, *[a for _, a, _ in behind])


def _middle(x, target, h, o, u, zp, w_main, w_up_pool, w_up_attn, w_out, pool_w, pool_scale, final_g):
    seq = x.shape[0]
    tm = ROW_TILE
    nt = seq // tm + 1
    lp = nt * tm
    halo_blocks = tm // MAX_WINDOW

    def body(x_ref, t_ref, h_ref, o_ref, u_ref, uh_ref, zp_ref,
             wc_ref, wupp_ref, wupa_ref, wout_ref, pw_ref, sc_ref, gf_ref,
             dh2_ref, mg_ref, yp_ref, ya_ref, dap_ref, daa_ref, do_ref, dmid_ref, dpn_ref,
             loss_ref, dgf_ref, dsc_ref, dpw_ref):
        i = pl.program_id(0)
        tiles = (dh2_ref, mg_ref, yp_ref, ya_ref, dap_ref, daa_ref, do_ref, dmid_ref, dpn_ref)

        @pl.when(i == 0)
        def _():
            for ref in tiles + (loss_ref, dgf_ref, dsc_ref, dpw_ref):
                ref[...] = jnp.zeros_like(ref)

        @pl.when(i > 0)
        def _():
            xt = x_ref[...]
            hb = h_ref[...]
            pc = _dot_nt(hb, wc_ref[...])
            za, gp, ga = pc[:, :512], pc[:, 512:1536], pc[:, 1536:]
            of = o_ref[...].astype(F32)
            sza = _sigmoid(za)
            silu_za = za * sza
            ya = (of * silu_za).astype(BF16)
            ya_ref[...] = ya
            aa = _dot(ya, wupa_ref[...])

            u = u_ref[...]
            zp = zp_ref[...]
            counts = _pool_counts(i * tm, tm)
            ps = _pool_means(jnp.concatenate([uh_ref[...], u], axis=0), u, counts)
            pbs = [p.astype(BF16) for p in ps]
            ppw = jnp.concatenate([_dot(pbs[g], pw_ref[g]) for g in range(4)], axis=1)
            sc = sc_ref[...]
            szp = _sigmoid(zp)
            silu_zp = zp * szp
            ypre = ppw * sc
            yp = (ypre * silu_zp).astype(BF16)
            yp_ref[...] = yp
            ap = _dot(yp, wupp_ref[...])

            sgp, sga = _sigmoid(gp), _sigmoid(ga)
            mg = (sgp * ap + sga * aa).astype(BF16)
            mg_ref[...] = mg
            h2 = xt + _dot(mg, wout_ref[...])
            r2 = lax.rsqrt(jnp.mean(h2 * h2, axis=-1, keepdims=True) + RMS_EPS)
            h2n = h2 * r2
            gf = gf_ref[...]
            diff = h2n * gf - t_ref[...]
            loss_ref[...] += 0.5 * jnp.sum(jnp.mean(diff * diff, axis=-1, keepdims=True), axis=0, keepdims=True)
            dy = diff * (1.0 / D_MODEL)
            dgf_ref[...] += jnp.sum(dy * h2n, axis=0, keepdims=True)
            dyg = dy * gf
            dh2 = r2 * (dyg - h2n * jnp.mean(dyg * h2n, axis=-1, keepdims=True))
            dh2_ref[...] = dh2
            dmg = _dot_nt(dh2.astype(BF16), wout_ref[...])
            dap = (dmg * sgp).astype(BF16)
            daa = (dmg * sga).astype(BF16)
            dap_ref[...] = dap
            daa_ref[...] = daa
            dmid_ref[:, MID_GP:MID_GA] = (dmg * ap * sgp * (1.0 - sgp)).astype(BF16)
            dmid_ref[:, MID_GA:] = (dmg * aa * sga * (1.0 - sga)).astype(BF16)
            dyp = _dot_nt(dap, wupp_ref[...])
            dya = _dot_nt(daa, wupa_ref[...])
            do_ref[...] = (dya * silu_za).astype(BF16)
            dmid_ref[:, MID_ZA:MID_GP] = (dya * of * (sza * (1.0 + za * (1.0 - sza)))).astype(BF16)

            dypre = dyp * silu_zp
            dmid_ref[:, :MID_ZA] = (dyp * ypre * (szp * (1.0 + zp * (1.0 - szp)))).astype(BF16)
            dsc_ref[...] += jnp.sum(dypre * ppw, axis=0, keepdims=True)
            dppw = (dypre * sc).astype(BF16)
            dpns = []
            for g in range(4):
                dg = dppw[:, POOL_GROUP * g:POOL_GROUP * (g + 1)]
                dpw_ref[g] += _dot_tn(pbs[g], dg)
                dpns.append(_dot_nt(dg, pw_ref[g]) / counts[g])
            dpn_ref[...] = jnp.concatenate(dpns, axis=1)

    real = lambda w: pl.BlockSpec((tm, w), lambda i: (jnp.maximum(i - 1, 0), 0))
    row = lambda w: pl.BlockSpec((tm, w), lambda i: (i, 0))
    in_specs = [
        real(D_MODEL), real(D_MODEL), row(D_MODEL), row(512), row(512),
        pl.BlockSpec((MAX_WINDOW, 512), lambda i: (jnp.maximum(i * halo_blocks - 1, 0), 0)), row(512),
        _const((2560, D_MODEL), (1, 0)), _const((POOL_WIDTH, D_MODEL)), _const((ATTN_WIDTH, D_MODEL)),
        _const((D_MODEL, D_MODEL)), _const((4, POOL_GROUP, POOL_GROUP)), _const((1, POOL_WIDTH)), _const((1, D_MODEL)),
    ]
    sd = jax.ShapeDtypeStruct
    out_shape = [
        sd((lp, D_MODEL), F32),
        sd((lp, D_MODEL), BF16),
        sd((lp, 512), BF16),
        sd((lp, 512), BF16),
        sd((lp, D_MODEL), BF16),
        sd((lp, D_MODEL), BF16),
        sd((lp, 512), BF16),
        sd((lp, MID_WIDTH), BF16),
        sd((lp, 512), F32),
        sd((1, LANES), F32),
        sd((1, D_MODEL), F32),
        sd((1, 512), F32),
        sd((4, POOL_GROUP, POOL_GROUP), F32),
    ]
    keep = lambda shape: pl.BlockSpec(shape, lambda i: (0,) * len(shape))
    out_specs = [row(D_MODEL), row(D_MODEL), row(512), row(512), row(D_MODEL), row(D_MODEL), row(512),
                 row(MID_WIDTH), row(512),
                 keep((1, LANES)), keep((1, D_MODEL)), keep((1, 512)), keep((4, POOL_GROUP, POOL_GROUP))]
    return pl.pallas_call(
        body, name="middle", grid=(nt,), out_shape=out_shape, in_specs=in_specs, out_specs=out_specs,
        compiler_params=_params(("arbitrary",)),
    )(x, target, h, o, u, u, zp, w_main, w_up_pool, w_up_attn, w_out, pool_w, pool_scale, final_g)


DUF_WIDTH = POOL_WIDTH + LANES


def _sequence_grads(dpn, dc, sneg):
    lp = dpn.shape[0]
    tm = ROW_TILE
    nt = lp // tm
    halo_blocks = tm // MAX_WINDOW
    last_halo = lp // MAX_WINDOW - 1

    def body(dpn_ref, dpnh_ref, dc_ref, sn_ref, duf_ref, dbf_ref, carry_ref):
        i = pl.program_id(0)
        t = nt - 1 - i

        @pl.when(i == 0)
        def _():
            carry_ref[...] = jnp.zeros_like(carry_ref)
            dbf_ref[...] = jnp.zeros_like(dbf_ref)

        dpn_t = dpn_ref[...]
        ahead = jnp.where(i == 0, jnp.zeros_like(dpnh_ref), dpnh_ref[...])
        ext = jnp.concatenate([dpn_t, ahead], axis=0)
        counts = _pool_counts(t * tm, tm)
        for g, w in enumerate(POOL_WINDOWS):
            s = ext[:, POOL_GROUP * g:POOL_GROUP * (g + 1)]
            sh = 1
            while sh < w:
                s = s + pltpu.roll(s, tm + MAX_WINDOW - sh, axis=0)
                sh *= 2
            du = s[:tm, :] - dpn_t[:, POOL_GROUP * g:POOL_GROUP * (g + 1)] * counts[g]
            duf_ref[:, POOL_GROUP * g:POOL_GROUP * (g + 1)] = du.astype(BF16)

        dct = dc_ref[:, 0:LANES]
        for hp in range(1, N_HEADS // 2):
            dct = dct + pltpu.roll(dc_ref[:, LANES * hp:LANES * (hp + 1)], 2 * hp, axis=1)
        rloc = lax.broadcasted_iota(jnp.int32, (tm, LANES), 0)
        sh = 1
        while sh < tm:
            dct = dct + jnp.where(rloc + sh < tm, pltpu.roll(dct, tm - sh, axis=0), 0.0)
            sh *= 2
        dct = dct + carry_ref[...]
        carry_ref[...] = dct[0:1, :]
        df = dct * sn_ref[...]
        dbf_ref[...] += jnp.sum(df, axis=0, keepdims=True)
        duf_ref[:, POOL_WIDTH:] = df.astype(BF16)

    rev = lambda w: pl.BlockSpec((tm, w), lambda i: (nt - 1 - i, 0))
    return pl.pallas_call(
        body, name="sequence_grads", grid=(nt,),
        out_shape=[jax.ShapeDtypeStruct((lp, DUF_WIDTH), BF16), jax.ShapeDtypeStruct((1, LANES), F32)],
        in_specs=[rev(512),
                  pl.BlockSpec((MAX_WINDOW, 512), lambda i: (jnp.minimum((nt - i) * halo_blocks, last_halo), 0)),
                  rev(512), rev(LANES)],
        out_specs=[rev(DUF_WIDTH), pl.BlockSpec((1, LANES), lambda i: (0, 0))],
        scratch_shapes=[pltpu.VMEM((1, LANES), F32)],
        compiler_params=_params(("arbitrary",)),
    )(dpn, dpn, dc, sneg)


def _backward_in(x, tile0, norm_g, dh2, duf, dqkv, dmid, w_main, w_f, behind):
    seq = x.shape[0]
    tm = ROW_TILE
    nt = seq // tm + 1
    nx = len(behind)

    def body(x_ref, t0_ref, g_ref, dh2_ref, du_ref, df_ref, dqkv_ref, dzp_ref, dza_ref, dgp_ref, dga_ref,
             wm_ref, wf_ref, *rest):
        gx_ref, gmeta_ref, dg_ref = rest[nx:nx + 3]
        dproj_ref = rest[2 * nx + 3]
        finish_exchange = _behind(behind, rest[:nx], rest[nx + 3:2 * nx + 3], rest[2 * nx + 4:])
        t = pl.program_id(0)

        @pl.when(t == 0)
        def _():
            dg_ref[...] = jnp.zeros_like(dg_ref)

        dproj_ref[:, 0:512] = du_ref[...]
        dproj_ref[:, 512:1024] = dzp_ref[...]
        dproj_ref[:, 1024:1536] = dqkv_ref[0]
        dproj_ref[:, 1536:2048] = dqkv_ref[1]
        dproj_ref[:, 2048:2560] = dqkv_ref[2]
        dproj_ref[:, 2560:3072] = dza_ref[...]
        dproj_ref[:, 3072:4096] = dgp_ref[...]
        dproj_ref[:, 4096:5120] = dga_ref[...]
        dh = _dot(dproj_ref[...], wm_ref[...]) + _dot(df_ref[...], wf_ref[...])
        xt = jnp.where(t == 0, t0_ref[...], x_ref[...])
        r = lax.rsqrt(jnp.mean(xt * xt, axis=-1, keepdims=True) + RMS_EPS)
        xn = xt * r
        dg_ref[...] += jnp.sum(dh * xn, axis=0, keepdims=True)
        dhg = dh * g_ref[...]
        dx = dh2_ref[...] + r * (dhg - xn * jnp.mean(dhg * xn, axis=-1, keepdims=True))

        @pl.when(t > 0)
        def _():
            gx_ref[...] = dx

        @pl.when(t == 0)
        def _():
            gmeta_ref[...] = dx[PAD:, :]
            gx_ref[...] = jnp.zeros_like(gx_ref)

        finish_exchange()

    row = lambda w, j=0: pl.BlockSpec((tm, w), lambda i: (i, j))
    real = pl.BlockSpec((tm, D_MODEL), lambda i: (jnp.maximum(i - 1, 0), 0))
    hbm = pl.BlockSpec(memory_space=pl.ANY)
    in_specs = [
        real, _const((tm, D_MODEL)), _const((1, D_MODEL)), row(D_MODEL),
        row(POOL_WIDTH), row(LANES, POOL_WIDTH // LANES), pl.BlockSpec((3, tm, ATTN_WIDTH), lambda i: (0, i, 0)),
        row(512, 0), row(512, 1), row(1024, 1), row(1024, 2),
        _const((N_MAIN, D_MODEL)), _const((LANES, D_MODEL)),
    ] + [hbm] * nx
    sd = jax.ShapeDtypeStruct
    out_shape = [sd((seq, D_MODEL), F32), sd((N_META, D_MODEL), F32), sd((1, D_MODEL), F32)] + _exchange_results(behind)
    keep = lambda shape: pl.BlockSpec(shape, lambda i: (0,) * len(shape))
    out_specs = [real, keep((N_META, D_MODEL)), keep((1, D_MODEL))] + [hbm] * nx
    return pl.pallas_call(
        body, name="backward_in", grid=(nt,), out_shape=out_shape, in_specs=in_specs, out_specs=out_specs,
        scratch_shapes=[pltpu.VMEM((tm, N_MAIN), BF16)] + _exchange_semaphores(nx),
        compiler_params=_params(("arbitrary",)),
    )(x, tile0, norm_g, dh2, duf, duf, dqkv, dmid, dmid, dmid, dmid, w_main, w_f, *[a for _, a, _ in behind])


def _matmul_tn(name, a, b, tn):
    lp, m = a.shape
    n = b.shape[1]

    def body(a_ref, b_ref, c_ref):
        c_ref[...] = _dot_tn(a_ref[...].astype(BF16), b_ref[...].astype(BF16))

    return pl.pallas_call(
        body, name=name, grid=(n // tn,), out_shape=jax.ShapeDtypeStruct((m, n), F32),
        in_specs=[_const((lp, m)), pl.BlockSpec((lp, tn), lambda j: (0, j))],
        out_specs=pl.BlockSpec((m, tn), lambda j: (0, j)),
        compiler_params=_params(("arbitrary",)),
    )(a, b)


def _matmul_tn_rows(name, a, b, tm):
    lp, m = a.shape
    n = b.shape[1]

    def body(a_ref, b_ref, c_ref):
        c_ref[...] = _dot_tn(a_ref[...].astype(BF16), b_ref[...].astype(BF16))

    return pl.pallas_call(
        body, name=name, grid=(m // tm,), out_shape=jax.ShapeDtypeStruct((m, n), F32),
        in_specs=[pl.BlockSpec((lp, tm), lambda j: (0, j)), _const((lp, n))],
        out_specs=pl.BlockSpec((tm, n), lambda j: (j, 0)),
        compiler_params=_params(("arbitrary",)),
    )(a, b)


def _matmul_tn_stack(name, a, b):
    n_blocks, lp, m = a.shape
    n = b.shape[1]

    def body(a_ref, b_ref, c_ref):
        c_ref[...] = _dot_tn(a_ref[...], b_ref[...])

    return pl.pallas_call(
        body, name=name, grid=(n_blocks,), out_shape=jax.ShapeDtypeStruct((n_blocks * m, n), F32),
        in_specs=[pl.BlockSpec((None, lp, m), lambda j: (j, 0, 0)), _const((lp, n))],
        out_specs=pl.BlockSpec((m, n), lambda j: (j, 0)),
        compiler_params=_params(("arbitrary",)),
    )(a, b)


def _adamw_step(p_ref, w_ref, m_ref, v_ref, g_ref, d_ref, mo_ref, vo_ref):
    g = p_ref[0].astype(F32)
    for s in range(1, p_ref.shape[0]):
        g = g + p_ref[s].astype(F32)
    m_new = ADAM_B1 * m_ref[...] + (1.0 - ADAM_B1) * g
    v_new = ADAM_B2 * v_ref[...] + (1.0 - ADAM_B2) * (g * g)
    m_hat = m_new / (1.0 - ADAM_B1 ** ADAM_STEP)
    v_hat = v_new / (1.0 - ADAM_B2 ** ADAM_STEP)
    g_ref[...] = g
    d_ref[...] = -ADAM_LR * (m_hat / (jnp.sqrt(v_hat) + ADAM_EPS) + ADAM_WD * w_ref[...])
    mo_ref[...] = m_new
    vo_ref[...] = v_new


def _adamw_small(name, groups, loss_parts):
    n = len(groups)

    def body(*refs):
        ins, outs = refs[:4 * n + 1], refs[4 * n + 1:]
        for j in range(n):
            _adamw_step(*ins[4 * j:4 * j + 4], *outs[4 * j:4 * j + 4])
        total = ins[-1][0]
        for s in range(1, N_DEV):
            total = total + ins[-1][s]
        outs[-1][...] = total

    vmem = pl.BlockSpec(memory_space=pltpu.VMEM)
    out_shape = [jax.ShapeDtypeStruct(w.shape, F32) for _, w, _, _ in groups for _ in range(4)]
    out_shape.append(jax.ShapeDtypeStruct(loss_parts.shape[1:], F32))
    res = pl.pallas_call(
        body, name=name, out_shape=out_shape, in_specs=[vmem] * (4 * n + 1), out_specs=[vmem] * (4 * n + 1),
        compiler_params=_params(),
    )(*[a for g in groups for a in g], loss_parts)
    return [res[4 * j:4 * j + 4] for j in range(n)], res[-1]


def _adamw(name, parts, w, m, v, rows, cols=None):
    r, c_all = w.shape
    c = cols or c_all
    n_parts = parts.shape[0]

    def body(p_ref, w_ref, m_ref, v_ref, g_ref, d_ref, mo_ref, vo_ref):
        _adamw_step(p_ref, w_ref, m_ref, v_ref, g_ref, d_ref, mo_ref, vo_ref)

    blk = pl.BlockSpec((rows, c), lambda i, j: (i, j))
    return pl.pallas_call(
        body, name=name, grid=(r // rows, c_all // c), out_shape=[jax.ShapeDtypeStruct((r, c_all), F32)] * 4,
        in_specs=[pl.BlockSpec((n_parts, rows, c), lambda i, j: (0, i, j)), blk, blk, blk],
        out_specs=[blk] * 4,
        compiler_params=_params(("arbitrary", "arbitrary")),
    )(parts, w, m, v)


def _pair_sum(name, mine, theirs, rows):
    n, r, c = mine.shape

    def body(a_ref, b_ref, o_ref):
        o_ref[...] = (a_ref[...].astype(F32) + b_ref[...].astype(F32)).astype(BF16)

    blk = pl.BlockSpec((1, rows, c), lambda j, i: (j, i, 0))
    return pl.pallas_call(
        body, name=name, grid=(n, r // rows), out_shape=jax.ShapeDtypeStruct((n, r, c), BF16),
        in_specs=[blk, blk], out_specs=blk,
        compiler_params=_params(("arbitrary", "arbitrary")),
    )(mine, theirs)


def _by_core(slots):
    by_core = slots.reshape((4, 2) + slots.shape[1:]).swapaxes(0, 1)
    c = lax.axis_index("c")
    return (lax.dynamic_index_in_dim(by_core, c, 0, keepdims=False),
            lax.dynamic_index_in_dim(by_core, 1 - c, 0, keepdims=False))


def _columns_to_slots(a):
    r, c8 = a.shape
    return a.reshape(r, N_DEV, c8 // N_DEV).transpose(1, 0, 2)


def _slots_to_columns(a):
    n, r, c = a.shape
    return a.transpose(1, 0, 2).reshape(r, n * c)


def kernel(x, meta_tokens, norm_g, w_in, b_forget, pool_w, pool_scale, w_up_pool, w_up_attn, w_out, final_norm_g, loss_target, m_meta_tokens, m_norm_g, m_w_in, m_b_forget, m_pool_w, m_pool_scale, m_w_up_pool, m_w_up_attn, m_w_out, m_final_norm_g, v_meta_tokens, v_norm_g, v_w_in, v_b_forget, v_pool_w, v_pool_scale, v_w_up_pool, v_w_up_attn, v_w_out, v_final_norm_g):
    xs = x[0]
    target = loss_target[0]

    g_in, g_meta = _gather_two_level("gather_weights", [w_in[0].T.astype(BF16), meta_tokens], (320, 8))
    w_full = g_in.reshape(N_DEV * g_in.shape[1], D_MODEL)
    w_main = jnp.concatenate([w_full[:N_BEFORE_F], w_full[N_BEFORE_F + N_HEADS:]], axis=0)
    w_f = jnp.pad(w_full[N_BEFORE_F:N_BEFORE_F + N_HEADS], ((0, LANES - N_HEADS), (0, 0)))
    meta = _slots_to_columns(g_meta)
    tile0 = jnp.concatenate([jnp.zeros((PAD, D_MODEL), F32), meta], axis=0)
    b_f = jnp.pad(b_forget, ((0, 0), (0, LANES - N_HEADS)))
    pw_b = pool_w[0].astype(BF16)
    final_g = final_norm_g.reshape(1, D_MODEL)

    h, u, zp, k, v, qt, kt, vt, sneg = _forward_in(xs, tile0, norm_g, w_main, w_f, b_f)
    o, lse, g_upp, g_upa, g_out = _attention_forward(
        qt, k, vt, [("gather", w.astype(BF16), ALL_PEERS) for w in (w_up_pool[0], w_up_attn[0], w_out[0])])
    wupp = _slots_to_columns(g_upp)
    wupa = _slots_to_columns(g_upa)
    wout = g_out.reshape(D_MODEL, D_MODEL)
    (dh2, mg, yp, ya, dap, daa, do, dmid, dpn,
     loss_part, d_final_g, d_scale, d_pool_w) = _middle(xs, target, h, o, u, zp, w_main, wupp, wupa, wout,
                                                        pw_b, pool_scale, final_g)
    dw_out = _matmul_tn("grad_w_out", mg, dh2, 256)
    dw_upp = _matmul_tn("grad_w_up_pool", yp, dap, 512)
    dw_upa = _matmul_tn("grad_w_up_attn", ya, daa, 512)
    dqkv, dc, p_upp, p_upa, p_out, p_pool_w, p_scale, p_final_g = _attention_backward(
        qt, k, kt, v, do, o, lse,
        [("scatter", _columns_to_slots(dw_upp).astype(BF16), ALL_PEERS),
         ("scatter", _columns_to_slots(dw_upa).astype(BF16), ALL_PEERS),
         ("scatter", dw_out.reshape(N_DEV, D_MODEL // N_DEV, D_MODEL).astype(BF16), ALL_PEERS),
         ("gather", d_pool_w.reshape(4 * POOL_GROUP, POOL_GROUP), ALL_PEERS),
         ("gather", d_scale, ALL_PEERS), ("gather", d_final_g, ALL_PEERS)])
    duf, d_bf = _sequence_grads(dpn, dc, sneg)
    g_uf = _matmul_tn_rows("grad_w_in_pool_forget", duf, h, DUF_WIDTH)
    g_qkv = _matmul_tn_stack("grad_w_in_attention", dqkv, h)
    g_mid = _matmul_tn_rows("grad_w_in_gates", dmid, h, 512)
    dw_in = jnp.concatenate([g_uf[:POOL_WIDTH], g_mid[:MID_ZA], g_qkv, g_mid[MID_ZA:MID_GP],
                             g_uf[POOL_WIDTH:POOL_WIDTH + N_HEADS], g_mid[MID_GP:]], axis=0)
    dw_in = dw_in.reshape(N_DEV, dw_in.shape[0] // N_DEV, D_MODEL)
    mine, for_sibling = _by_core(dw_in)
    from_sibling, = _exchange("swap_with_sibling", [("swap", for_sibling.astype(BF16), (SIBLING,))])
    pair_sums = _pair_sum("pair_sum", mine, from_sibling, dw_in.shape[1])
    grad_x, d_meta, d_norm_g, p_in, p_bf, p_loss = _backward_in(
        xs, tile0, norm_g, dh2, duf, dqkv, dmid, w_main, w_f,
        [("chips", pair_sums, SAME_CORE), ("gather", d_bf, ALL_PEERS), ("gather", loss_part, ALL_PEERS)])
    p_meta, p_norm_g = _exchange(
        "exchange_gradients", [("scatter", _columns_to_slots(d_meta), ALL_PEERS), ("gather", d_norm_g, ALL_PEERS)])


    def pad_f(a):
        return jnp.pad(a, ((0, 0), (0, LANES - N_HEADS)))

    res = {}
    res["w_in"] = [a.T for a in _adamw("adamw_w_in", p_in, w_in[0].T, m_w_in[0].T, v_w_in[0].T, p_in.shape[1], 256)]
    res["w_up_pool"] = _adamw("adamw_w_up_pool", p_upp, w_up_pool[0], m_w_up_pool[0], v_w_up_pool[0], 512)
    res["w_up_attn"] = _adamw("adamw_w_up_attn", p_upa, w_up_attn[0], m_w_up_attn[0], v_w_up_attn[0], 512)
    res["w_out"] = _adamw("adamw_w_out", p_out, w_out[0], m_w_out[0], v_w_out[0], 128)
    flat = lambda a: a.reshape(4 * POOL_GROUP, POOL_GROUP)
    row = lambda a: a.reshape(1, D_MODEL)
    small, loss_row = _adamw_small(
        "adamw_small",
        [(p_meta, meta_tokens, m_meta_tokens, v_meta_tokens),
         (p_norm_g, norm_g, m_norm_g, v_norm_g),
         (p_bf, pad_f(b_forget), pad_f(m_b_forget), pad_f(v_b_forget)),
         (p_pool_w, flat(pool_w), flat(m_pool_w), flat(v_pool_w)),
         (p_scale, pool_scale, m_pool_scale, v_pool_scale),
         (p_final_g, final_g, row(m_final_norm_g), row(v_final_norm_g))],
        p_loss)
    res["meta_tokens"], res["norm_g"], bf, pw, res["pool_scale"], fg = small
    res["b_forget"] = [a[:, :N_HEADS] for a in bf]
    res["pool_w"] = [a.reshape(pool_w.shape) for a in pw]
    res["final_norm_g"] = [a.reshape(D_MODEL) for a in fg]
    loss = loss_row[0, 0]
    for name in ("w_in", "w_up_pool", "w_up_attn", "w_out"):
        res[name] = [a[None] for a in res[name]]

    order = ["meta_tokens", "norm_g", "w_in", "b_forget", "pool_w", "pool_scale", "w_up_pool", "w_up_attn", "w_out",
             "final_norm_g"]
    outs = [loss, grad_x[None]]
    for part in range(4):
        outs += [res[name][part] for name in order]
    return tuple(outs)
```

```python
import jax
import jax.numpy as jnp
from jax import lax
from jax.experimental import pallas as pl
from jax.experimental.pallas import tpu as pltpu

F32 = jnp.float32
BF16 = jnp.bfloat16

D_MODEL = 1024
N_META = 16
POOL_WIDTH = 512
ATTN_WIDTH = 512
N_HEADS = 8
HEAD_DIM = 64
POOL_WINDOWS = (2, 4, 8, 16)
POOL_GROUP = 128
MAX_WINDOW = 16
RMS_EPS = 1e-6
N_MAIN = 5120
N_BEFORE_F = 3072
N_DEV = 8
LANES = 128

ROW_TILE = 256
ATT_TILE = 256
ATT_Q_BLOCKS_FWD = 8
ATT_Q_BLOCKS_BWD = 4
PAD = ROW_TILE - N_META
FIRST_KEY = PAD // LANES * LANES
VMEM_LIMIT = 56 * 1024 * 1024

ADAM_LR = 0.001
ADAM_B1 = 0.9
ADAM_B2 = 0.999
ADAM_EPS = 1e-08
ADAM_WD = 0.01
ADAM_STEP = 10

MID_ZA, MID_GP, MID_GA, MID_WIDTH = 512, 1024, 2048, 3072
NEG = -1e30
MESH = pl.DeviceIdType.MESH


def _params(sem=None):
    kw = dict(vmem_limit_bytes=VMEM_LIMIT)
    if sem is not None:
        kw["dimension_semantics"] = sem
    return pltpu.CompilerParams(**kw)


def _const(shape, block_index=None):
    idx = block_index or (0,) * len(shape)
    return pl.BlockSpec(shape, lambda i: idx, pipeline_mode=pl.Buffered(1))


def _sigmoid(x):
    return jax.nn.sigmoid(x)


def _dot(a, b):
    return jnp.dot(a, b, preferred_element_type=F32)


def _dot_nt(a, b):
    return lax.dot_general(a, b, (((1,), (1,)), ((), ())), preferred_element_type=F32)


def _dot_tn(a, b):
    return lax.dot_general(a, b, (((0,), (0,)), ((), ())), preferred_element_type=F32)


def _pool_counts(first_row, rows):
    row = first_row + lax.broadcasted_iota(jnp.int32, (rows, 1), 0)
    pos1 = row - PAD + 1
    return [jnp.clip(pos1, 1, w).astype(F32) for w in POOL_WINDOWS]


def _pool_means(u_ext, u, counts):
    rows = u.shape[0]
    out = []
    for g, w in enumerate(POOL_WINDOWS):
        s = u_ext[:, POOL_GROUP * g:POOL_GROUP * (g + 1)]
        sh = 1
        while sh < w:
            s = s + pltpu.roll(s, sh, axis=0)
            sh *= 2
        out.append(s[MAX_WINDOW:MAX_WINDOW + rows, :] / counts[g] - u[:, POOL_GROUP * g:POOL_GROUP * (g + 1)])
    return out


Q_BIAS, Q_ONES, Q_LSE = 64, 67, 70
K_ONES, K_BIAS, K_ONES2 = 64, 67, 70
V_ONES = 64
DO_BIAS = 64


def _lane_ones(lane, ranges):
    hit = None
    for lo, hi in ranges:
        r = (lane >= lo) & (lane < hi)
        hit = r if hit is None else hit | r
    return jnp.where(hit, 1.0, 0.0)


def _put3(base, lane, first, x):
    hi = x.astype(BF16).astype(F32)
    rest = x - hi
    mid = rest.astype(BF16).astype(F32)
    lo = (rest - mid).astype(BF16).astype(F32)
    for j, piece in enumerate((hi, mid, lo)):
        base = jnp.where(lane == first + j, piece, base)
    return base


SIBLING = 1
SAME_CORE = (2, 4, 6)
ALL_PEERS = (1, 2, 3, 4, 5, 6, 7)


def _place():
    return lax.axis_index("x"), lax.axis_index("y"), lax.axis_index("c")


def _peer(r):
    x, y, c = _place()
    return (1 - x if r & 4 else x, 1 - y if r & 2 else y, 1 - c if r & 1 else c)


def _device_slot(p):
    return 4 * p[0] + 2 * p[1] + p[2]


def _chip_slot(p):
    return 2 * p[0] + p[1]


def _exchange(name, items):
    n = len(items)

    def body(*refs):
        copies = _exchange_copies(items, refs[:n], refs[n:2 * n], *refs[2 * n:])
        for cp in copies:
            cp.start()
        for cp in copies:
            cp.wait()

    hbm = pl.BlockSpec(memory_space=pl.ANY)
    return pl.pallas_call(
        body, name=name, out_shape=_exchange_results(items),
        in_specs=[hbm] * n, out_specs=[hbm] * n,
        scratch_shapes=_exchange_semaphores(n),
    )(*[a for _, a, _ in items])


def _exchange_results(items):
    return [jax.ShapeDtypeStruct(((N_DEV,) if kind == "gather" else ()) + a.shape, a.dtype) for kind, a, _ in items]


def _exchange_semaphores(n):
    return [pltpu.SemaphoreType.DMA((n, N_DEV - 1)), pltpu.SemaphoreType.DMA((n, N_DEV - 1)),
            pltpu.SemaphoreType.DMA((n,))]


def _exchange_copies(items, ins, outs, send_sems, recv_sems, local_sems):
    me = _place()
    copies = []
    for a, (kind, _, peers) in enumerate(items):
        slot = _chip_slot if kind == "chips" else _device_slot
        for r in peers:
            peer = _peer(r)
            src = ins[a] if kind in ("swap", "gather") else ins[a].at[slot(peer)]
            dst = outs[a] if kind == "swap" else outs[a].at[slot(me)]
            copies.append(pltpu.make_async_remote_copy(
                src_ref=src, dst_ref=dst, send_sem=send_sems.at[a, r - 1], recv_sem=recv_sems.at[a, r - 1],
                device_id=peer, device_id_type=MESH))
        if kind != "swap":
            src = ins[a] if kind == "gather" else ins[a].at[slot(me)]
            copies.append(pltpu.make_async_copy(src, outs[a].at[slot(me)], local_sems.at[a]))
    return copies


def _gather_two_level(name, arrays, halves):
    n = len(arrays)
    x_flip, y_flip, both = 4, 2, 6
    to_sibling, to_x, to_y, on_over_y, on_over_x, x_to_sibling, y_to_sibling, d_to_sibling = range(8)

    def body(*refs):
        ins, outs = refs[:n], refs[n:2 * n]
        send_sems, recv_sems, local_sems = refs[2 * n:]
        me, sibling = _place(), _peer(SIBLING)
        xn, yn, dg = _peer(x_flip), _peer(y_flip), _peer(both)

        def rows(a, h):
            return pl.ds(0, halves[a]) if h == 0 else pl.ds(halves[a], arrays[a].shape[0] - halves[a])

        def copy(a, h, k, block, to, own=False):
            dst = outs[a].at[_device_slot(block)].at[rows(a, h)]
            return pltpu.make_async_remote_copy(
                src_ref=ins[a].at[rows(a, h)] if own else dst, dst_ref=dst,
                send_sem=send_sems.at[2 * a + h, k], recv_sem=recv_sems.at[2 * a + h, k],
                device_id=to, device_id_type=MESH)

        sends, mine = [], []

        def start(cp):
            cp.start()
            sends.append(cp)

        for a in range(n):
            cp = pltpu.make_async_copy(ins[a], outs[a].at[_device_slot(me)], local_sems.at[a])
            cp.start()
            mine.append(cp)
        for a in range(n):
            start(copy(a, 0, to_x, me, xn, own=True))
            start(copy(a, 1, to_y, me, yn, own=True))
        for a in range(n):
            start(copy(a, 0, to_y, me, yn, own=True))
            start(copy(a, 1, to_x, me, xn, own=True))
        for a in range(n):
            for h in range(2):
                start(copy(a, h, to_sibling, me, sibling, own=True))
        for a in range(n):
            copy(a, 0, to_x, xn, me).wait_recv()
            start(copy(a, 0, on_over_y, xn, yn))
            start(copy(a, 0, x_to_sibling, xn, sibling))
            copy(a, 1, to_y, yn, me).wait_recv()
            start(copy(a, 1, on_over_x, yn, xn))
            start(copy(a, 1, y_to_sibling, yn, sibling))
        for a in range(n):
            copy(a, 0, to_y, yn, me).wait_recv()
            start(copy(a, 0, y_to_sibling, yn, sibling))
            copy(a, 1, to_x, xn, me).wait_recv()
            start(copy(a, 1, x_to_sibling, xn, sibling))
        for a in range(n):
            copy(a, 0, on_over_y, dg, me).wait_recv()
            start(copy(a, 0, d_to_sibling, dg, sibling))
            copy(a, 1, on_over_x, dg, me).wait_recv()
            start(copy(a, 1, d_to_sibling, dg, sibling))
        for a in range(n):
            for h in range(2):
                copy(a, h, to_sibling, sibling, me).wait_recv()
                for k, r in ((x_to_sibling, x_flip), (y_to_sibling, y_flip), (d_to_sibling, both)):
                    copy(a, h, k, _peer(r | SIBLING), me).wait_recv()
        for cp in sends:
            cp.wait_send()
        for cp in mine:
            cp.wait()

    hbm = pl.BlockSpec(memory_space=pl.ANY)
    return pl.pallas_call(
        body, name=name, out_shape=[jax.ShapeDtypeStruct((N_DEV,) + a.shape, a.dtype) for a in arrays],
        in_specs=[hbm] * n, out_specs=[hbm] * n,
        scratch_shapes=[pltpu.SemaphoreType.DMA((2 * n, 8)), pltpu.SemaphoreType.DMA((2 * n, 8)),
                        pltpu.SemaphoreType.DMA((n,))],
    )(*arrays)


def _forward_in(x, tile0, norm_g, w_main, w_f, b_f):
    seq = x.shape[0]
    nt = seq // ROW_TILE + 1
    lp = nt * ROW_TILE
    tm = ROW_TILE

    def body(x_ref, t0_ref, g_ref, wa_ref, wf_ref, bf_ref,
             h_ref, u_ref, zp_ref, k_ref, v_ref, qt_ref, kt_ref, vt_ref, sn_ref, carry_ref):
        i = pl.program_id(0)

        @pl.when(i == 0)
        def _():
            carry_ref[...] = jnp.zeros_like(carry_ref)

        xt = jnp.where(i == 0, t0_ref[...], x_ref[...])
        r = lax.rsqrt(jnp.mean(xt * xt, axis=-1, keepdims=True) + RMS_EPS)
        h = (xt * r * g_ref[...]).astype(BF16)
        h_ref[...] = h
        pa = _dot_nt(h, wa_ref[...])
        u_ref[...] = pa[:, :512]
        zp_ref[...] = pa[:, 512:1024]

        fl = _dot_nt(h, wf_ref[...]) + bf_ref[...]
        row = i * tm + lax.broadcasted_iota(jnp.int32, (tm, LANES), 0)
        rloc = lax.broadcasted_iota(jnp.int32, (tm, LANES), 0)
        lane = lax.broadcasted_iota(jnp.int32, (tm, LANES), 1)
        live = (row >= PAD) & (lane < N_HEADS)
        logf = jnp.minimum(fl, 0.0) - jnp.log1p(jnp.exp(-jnp.abs(fl)))
        cs = jnp.where(live, logf, 0.0)
        sh = 1
        while sh < tm:
            cs = cs + jnp.where(rloc >= sh, pltpu.roll(cs, sh, axis=0), 0.0)
            sh *= 2
        cs = cs + carry_ref[...]
        carry_ref[...] = cs[tm - 1:tm, :]
        sn_ref[...] = jnp.where(live, _sigmoid(-fl), 0.0)

        rows1 = i * tm + lax.broadcasted_iota(jnp.int32, (tm, 1), 0)
        ones_q = _lane_ones(lane, ((Q_ONES, Q_ONES + 3),))
        ones_k = _lane_ones(lane, ((K_ONES, K_ONES + 3), (K_ONES2, K_ONES2 + 3)))
        ones_v = _lane_ones(lane, ((V_ONES, V_ONES + 3),))
        for hp in range(N_HEADS // 2):
            qp = pa[:, 1024 + LANES * hp:1024 + LANES * (hp + 1)] * 0.125
            kp = pa[:, 1536 + LANES * hp:1536 + LANES * (hp + 1)]
            vp = pa[:, 2048 + LANES * hp:2048 + LANES * (hp + 1)]
            for e in range(2):
                head = 2 * hp + e
                if e:
                    qp, kp, vp = (pltpu.roll(a, HEAD_DIM, axis=1) for a in (qp, kp, vp))
                c_h = cs[:, head:head + 1]
                q_h = jnp.where(lane < HEAD_DIM, qp, _put3(ones_q, lane, Q_BIAS, c_h))
                qt_ref[head] = q_h.T.astype(BF16)
                minus_ck = jnp.where(rows1 >= PAD, -c_h, NEG)
                k_h = jnp.where(lane < HEAD_DIM, kp, _put3(ones_k, lane, K_BIAS, minus_ck))
                k_ref[head] = k_h.astype(BF16)
                kt_ref[head] = k_h.T.astype(BF16)
                v_h = jnp.where(lane < HEAD_DIM, vp, ones_v)
                v_ref[head] = v_h.astype(BF16)
                vt_ref[head] = v_h.T.astype(BF16)

    row_f32 = lambda w: pl.BlockSpec((tm, w), lambda i: (i, 0))
    out_shape = [
        jax.ShapeDtypeStruct((lp, D_MODEL), BF16),
        jax.ShapeDtypeStruct((lp, POOL_WIDTH), F32),
        jax.ShapeDtypeStruct((lp, POOL_WIDTH), F32),
        jax.ShapeDtypeStruct((N_HEADS, lp, LANES), BF16),
        jax.ShapeDtypeStruct((N_HEADS, lp, LANES), BF16),
        jax.ShapeDtypeStruct((N_HEADS, LANES, lp), BF16),
        jax.ShapeDtypeStruct((N_HEADS, LANES, lp), BF16),
        jax.ShapeDtypeStruct((N_HEADS, LANES, lp), BF16),
        jax.ShapeDtypeStruct((lp, LANES), F32),
    ]
    heads = pl.BlockSpec((N_HEADS, tm, LANES), lambda i: (0, i, 0))
    heads_t = pl.BlockSpec((N_HEADS, LANES, tm), lambda i: (0, 0, i))
    out_specs = [row_f32(D_MODEL), row_f32(512), row_f32(512), heads, heads, heads_t, heads_t, heads_t,
                 row_f32(LANES)]
    in_specs = [
        pl.BlockSpec((tm, D_MODEL), lambda i: (jnp.maximum(i - 1, 0), 0)),
        _const((tm, D_MODEL)), _const((1, D_MODEL)),
        _const((2560, D_MODEL)), _const((LANES, D_MODEL)), _const((1, LANES)),
    ]
    return pl.pallas_call(
        body, name="forward_in", grid=(nt,), out_shape=out_shape, in_specs=in_specs, out_specs=out_specs,
        scratch_shapes=[pltpu.VMEM((1, LANES), F32)],
        compiler_params=_params(("arbitrary",)),
    )(x, tile0, norm_g, w_main, w_f, b_f)


def _pair_lanes(a0, a1):
    lane = lax.broadcasted_iota(jnp.int32, a0.shape, 1)
    return jnp.where(lane < HEAD_DIM, a0, pltpu.roll(a1, HEAD_DIM, axis=1))


def _behind(items, ins, outs, sems):
    step, last = pl.program_id(0), pl.num_programs(0) - 1

    @pl.when(step == 0)
    def _():
        for cp in _exchange_copies(items, ins, outs, *sems):
            cp.start()

    def finish():
        @pl.when(step == last)
        def _():
            for cp in _exchange_copies(items, ins, outs, *sems):
                cp.wait()

    return finish


def _attention_forward(qt, k, vt, behind):
    lp = k.shape[1]
    tk = ATT_TILE
    q_blocks = ATT_Q_BLOCKS_FWD if (lp // tk - 1) % ATT_Q_BLOCKS_FWD == 0 else ATT_Q_BLOCKS_BWD
    tq_big = q_blocks * tk
    n_big = (lp // tk - 1) // q_blocks
    assert lp == tk + n_big * tq_big and q_blocks % 2 == 0
    nx = len(behind)

    def body(qt_ref, k_ref, vt_ref, *rest):
        o_ref, lse_ref = rest[nx:nx + 2]
        s_buf, m_scr, acc_scr = rest[2 * nx + 2:2 * nx + 5]
        finish_exchange = _behind(behind, rest[:nx], rest[nx + 2:2 * nx + 2], rest[2 * nx + 5:])

        def q_tile(q0, tq, pairs):
            first = q0 // tk
            qts = [qt_ref[e, :, pl.ds(q0, tq)] for e in range(2)]

            def block(kj):
                return pl.ds(kj * tk if isinstance(kj, int) else pl.multiple_of(kj * tk, tk), tk)

            def step(kj, rd, wr, c0=0, diagonal=False, keys=None):
                c1 = c0 + tk if diagonal else c0
                keys = block(kj) if keys is None else keys
                for e in range(2):
                    s = s_buf[rd, e, 0:keys.size, c0:tq]
                    if wr is not None:
                        s_buf[wr, e, :, c1:tq] = _dot(k_ref[e, block(kj + 1), :], qts[e][:, c1:tq])
                    if diagonal:
                        key = lax.broadcasted_iota(jnp.int32, s.shape, 0)
                        s = jnp.where(key <= lax.broadcasted_iota(jnp.int32, s.shape, 1), s, NEG)
                    m = m_scr[e, :, c0:tq]
                    m_new = jnp.maximum(m, jnp.max(s, axis=0, keepdims=True))
                    p = jnp.exp(s - m_new)
                    pv = _dot(vt_ref[e, :, keys], p.astype(BF16))
                    acc_scr[e, :, c0:tq] = jnp.exp(m - m_new) * acc_scr[e, :, c0:tq] + pv
                    m_scr[e, :, c0:tq] = m_new

            keys0 = block(0) if pairs is None else pl.ds(FIRST_KEY, tk - FIRST_KEY)
            for e in range(2):
                m_scr[e, :, 0:tq] = jnp.full((1, tq), NEG, F32)
                acc_scr[e, :, 0:tq] = jnp.zeros((LANES, tq), F32)
                s_buf[0, e, 0:keys0.size, 0:tq] = _dot(k_ref[e, keys0, :], qts[e])
            if pairs is None:
                step(0, 0, None, 0, True)
            else:
                step(0, 0, 1, keys=keys0)

                def two_steps(t, _):
                    step(1 + 2 * t, 1, 0)
                    step(2 + 2 * t, 0, 1)
                    return 0

                lax.fori_loop(0, pairs, two_steps, 0)
                for b in range(tq // tk):
                    step(first + b, (b + 1) % 2, b % 2 if (b + 1) * tk < tq else None, b * tk, True)
            outs, lses = [], []
            for e in range(2):
                acc = acc_scr[e, :, 0:tq]
                l = acc[V_ONES:V_ONES + 1, :]
                outs.append((acc / l).T)
                lses.append(m_scr[e, :, 0:tq] + jnp.log(l))
            o_ref[pl.ds(q0, tq), :] = _pair_lanes(outs[0], outs[1]).astype(BF16)
            lse_rows = jnp.concatenate(lses + [jnp.zeros((LANES - 2, tq), F32)], axis=0)
            lse_ref[pl.ds(q0, tq), :] = lse_rows.T

        q_tile(0, tk, None)

        def big_tile(i, _):
            q_tile(pl.multiple_of(tk + i * tq_big, tk), tq_big, (q_blocks // 2) * i)
            return 0

        lax.fori_loop(0, n_big, big_tile, 0)
        finish_exchange()

    pair = pl.BlockSpec((lp, LANES), lambda hp: (0, hp))
    heads = pl.BlockSpec((2, lp, LANES), lambda hp: (hp, 0, 0), pipeline_mode=pl.Buffered(1))
    heads_t = pl.BlockSpec((2, LANES, lp), lambda hp: (hp, 0, 0), pipeline_mode=pl.Buffered(1))
    hbm = pl.BlockSpec(memory_space=pl.ANY)
    return pl.pallas_call(
        body, name="attention_forward", grid=(N_HEADS // 2,),
        out_shape=[jax.ShapeDtypeStruct((lp, ATTN_WIDTH), BF16), jax.ShapeDtypeStruct((lp, ATTN_WIDTH), F32)]
        + _exchange_results(behind),
        in_specs=[heads_t, heads, heads_t] + [hbm] * nx,
        out_specs=[pair, pair] + [hbm] * nx,
        scratch_shapes=[pltpu.VMEM((2, 2, tk, tq_big), F32), pltpu.VMEM((2, 1, tq_big), F32),
                        pltpu.VMEM((2, LANES, tq_big), F32)] + _exchange_semaphores(nx),
        compiler_params=_params(("arbitrary",)),
    )(qt, k, vt, *[a for _, a, _ in behind])


def _rows3(first, x):
    sub = lax.broadcasted_iota(jnp.int32, (LANES, x.shape[1]), 0)
    hi = x.astype(BF16).astype(F32)
    rest = x - hi
    mid = rest.astype(BF16).astype(F32)
    lo = (rest - mid).astype(BF16).astype(F32)
    out = jnp.zeros((LANES, x.shape[1]), F32)
    for j, piece in enumerate((hi, mid, lo)):
        out = jnp.where(sub == first + j, piece, out)
    return out


def _attention_backward(qt, k, kt, v, do, o, lse, behind):
    lp = k.shape[1]
    tb = ATT_TILE
    nb = lp // tb
    tq_big = ATT_Q_BLOCKS_BWD * tb
    n_big = (nb - 1) // ATT_Q_BLOCKS_BWD
    assert lp == tb + n_big * tq_big and ATT_Q_BLOCKS_BWD % 2 == 0
    nx = len(behind)

    def body(qt_ref, k_ref, kt_ref, v_ref, do_ref, o_ref, lse_ref, *rest):
        dqkv_ref, dc_ref = rest[nx:nx + 2]
        q2_ref, do2_ref, dk_acc, dv_acc, dq_scr, s_buf = rest[2 * nx + 2:2 * nx + 8]
        finish_exchange = _behind(behind, rest[:nx], rest[nx + 2:2 * nx + 2], rest[2 * nx + 8:])
        sub = lax.broadcasted_iota(jnp.int32, (LANES, tb), 0)

        def lanes01(row0, row1):
            n = row0.shape[1]
            return jnp.concatenate([row0, row1, jnp.zeros((LANES - 2, n), F32)], axis=0).T

        def prepare(bi, _):
            r0 = pl.multiple_of(bi * tb, tb)
            queries = r0 + lax.broadcasted_iota(jnp.int32, (1, tb), 1)
            dob = do_ref[pl.ds(r0, tb), :].astype(F32)
            do_t = dob.T
            dd_t = (dob * o_ref[pl.ds(r0, tb), :].astype(F32)).T
            lse_t = lse_ref[pl.ds(r0, tb), :].T
            for e in range(2):
                delta = jnp.sum(dd_t[HEAD_DIM * e:HEAD_DIM * (e + 1), :], axis=0, keepdims=True)
                do_e = jnp.concatenate([do_t[HEAD_DIM * e:HEAD_DIM * (e + 1), :], jnp.zeros((HEAD_DIM, tb), F32)], axis=0)
                do2_ref[e, :, pl.ds(r0, tb)] = jnp.where(sub < HEAD_DIM, do_e, _rows3(DO_BIAS, -delta)).astype(BF16)
                minus_lse = jnp.where(queries >= PAD, -lse_t[e:e + 1, :], NEG)
                keep = (sub < Q_LSE) | (sub >= Q_LSE + 3)
                q2_ref[e, :, pl.ds(r0, tb)] = jnp.where(keep, qt_ref[e, :, pl.ds(r0, tb)].astype(F32),
                                                        _rows3(Q_LSE, minus_lse)).astype(BF16)
            return 0

        lax.fori_loop(0, nb, prepare, 0)
        dk_acc[...] = jnp.zeros_like(dk_acc)
        dv_acc[...] = jnp.zeros_like(dv_acc)

        def q_tile(q0, tq, pairs):
            first = q0 // tb
            qts = [q2_ref[e, :, pl.ds(q0, tq)] for e in range(2)]
            dots = [do2_ref[e, :, pl.ds(q0, tq)] for e in range(2)]

            def block(kj):
                return pl.ds(kj * tb if isinstance(kj, int) else pl.multiple_of(kj * tb, tb), tb)

            def step(kj, rd, wr, c0=0, diagonal=False, keys=None):
                c1 = c0 + tb if diagonal else c0
                keys = block(kj) if keys is None else keys
                for e in range(2):
                    s = s_buf[rd, e, 0:keys.size, c0:tq]
                    if wr is not None:
                        s_buf[wr, e, :, c1:tq] = _dot(k_ref[e, block(kj + 1), :], qts[e][:, c1:tq])
                    dpd = _dot(v_ref[e, keys, :], dots[e][:, c0:tq])
                    p = jnp.exp(s)
                    if diagonal:
                        key = lax.broadcasted_iota(jnp.int32, s.shape, 0)
                        p = jnp.where(key <= lax.broadcasted_iota(jnp.int32, s.shape, 1), p, 0.0)
                    dsb = (p * dpd).astype(BF16)
                    dv_acc[e, :, keys] += _dot_nt(dots[e][:, c0:tq], p.astype(BF16))
                    dk_acc[e, :, keys] += _dot_nt(qts[e][:, c0:tq], dsb)
                    dq_scr[e, :, c0:tq] += _dot(kt_ref[e, :, keys], dsb)

            keys0 = block(0) if pairs is None else pl.ds(FIRST_KEY, tb - FIRST_KEY)
            for e in range(2):
                dq_scr[e, :, 0:tq] = jnp.zeros((LANES, tq), F32)
                s_buf[0, e, 0:keys0.size, 0:tq] = _dot(k_ref[e, keys0, :], qts[e])
            if pairs is None:
                step(0, 0, None, 0, True)
            else:
                step(0, 0, 1, keys=keys0)

                def two_steps(t, _):
                    step(1 + 2 * t, 1, 0)
                    step(2 + 2 * t, 0, 1)
                    return 0

                lax.fori_loop(0, pairs, two_steps, 0)
                for b in range(tq // tb):
                    step(first + b, (b + 1) % 2, b % 2 if (b + 1) * tb < tq else None, b * tb, True)
            dq0, dq1 = dq_scr[0, :, 0:tq], dq_scr[1, :, 0:tq]
            dqkv_ref[0, pl.ds(q0, tq), :] = (_pair_lanes(dq0.T, dq1.T) * 0.125).astype(BF16)
            dc_ref[pl.ds(q0, tq), :] = lanes01(dq0[K_ONES:K_ONES + 1, :], dq1[K_ONES:K_ONES + 1, :])

        q_tile(0, tb, None)

        def big_tile(i, _):
            q_tile(pl.multiple_of(tb + i * tq_big, tb), tq_big, (ATT_Q_BLOCKS_BWD // 2) * i)
            return 0

        lax.fori_loop(0, n_big, big_tile, 0)

        def finish(bi, _):
            r0 = pl.multiple_of(bi * tb, tb)
            dk0, dk1 = dk_acc[0, :, pl.ds(r0, tb)], dk_acc[1, :, pl.ds(r0, tb)]
            dqkv_ref[1, pl.ds(r0, tb), :] = _pair_lanes(dk0.T, dk1.T).astype(BF16)
            dqkv_ref[2, pl.ds(r0, tb), :] = _pair_lanes(dv_acc[0, :, pl.ds(r0, tb)].T,
                                                        dv_acc[1, :, pl.ds(r0, tb)].T).astype(BF16)
            dc_ref[pl.ds(r0, tb), :] = dc_ref[pl.ds(r0, tb), :] - lanes01(dk0[Q_ONES:Q_ONES + 1, :], dk1[Q_ONES:Q_ONES + 1, :])
            return 0

        lax.fori_loop(0, nb, finish, 0)
        finish_exchange()

    once = pl.Buffered(1)
    pair = pl.BlockSpec((lp, LANES), lambda hp: (0, hp))
    pair_in = pl.BlockSpec((lp, LANES), lambda hp: (0, hp), pipeline_mode=once)
    heads = pl.BlockSpec((2, lp, LANES), lambda hp: (hp, 0, 0), pipeline_mode=once)
    heads_t = pl.BlockSpec((2, LANES, lp), lambda hp: (hp, 0, 0), pipeline_mode=once)
    hbm = pl.BlockSpec(memory_space=pl.ANY)
    return pl.pallas_call(
        body, name="attention_backward", grid=(N_HEADS // 2,),
        out_shape=[jax.ShapeDtypeStruct((3, lp, ATTN_WIDTH), BF16), jax.ShapeDtypeStruct((lp, ATTN_WIDTH), F32)]
        + _exchange_results(behind),
        in_specs=[heads_t, heads, heads_t, heads, pair_in, pair_in, pair_in] + [hbm] * nx,
        out_specs=[pl.BlockSpec((3, lp, LANES), lambda hp: (0, 0, hp)), pair] + [hbm] * nx,
        scratch_shapes=[pltpu.VMEM((2, LANES, lp), BF16), pltpu.VMEM((2, LANES, lp), BF16),
                        pltpu.VMEM((2, LANES, lp), F32), pltpu.VMEM((2, LANES, lp), F32),
                        pltpu.VMEM((2, LANES, tq_big), F32), pltpu.VMEM((2, 2, tb, tq_big), F32)]
        + _exchange_semaphores(nx),
        compiler_params=_params(("arbitrary",)),
    )(qt, k, kt, v, do, o, lse, *[a for _, a, _ in behind])


def _middle(x, target, h, o, u, zp, w_main, w_up_pool, w_up_attn, w_out, pool_w, pool_scale, final_g):
    seq = x.shape[0]
    tm = ROW_TILE
    nt = seq // tm + 1
    lp = nt * tm
    halo_blocks = tm // MAX_WINDOW

    def body(x_ref, t_ref, h_ref, o_ref, u_ref, uh_ref, zp_ref,
             wc_ref, wupp_ref, wupa_ref, wout_ref, pw_ref, sc_ref, gf_ref,
             dh2_ref, mg_ref, yp_ref, ya_ref, dap_ref, daa_ref, do_ref, dmid_ref, dpn_ref,
             loss_ref, dgf_ref, dsc_ref, dpw_ref):
        i = pl.program_id(0)
        tiles = (dh2_ref, mg_ref, yp_ref, ya_ref, dap_ref, daa_ref, do_ref, dmid_ref, dpn_ref)

        @pl.when(i == 0)
        def _():
            for ref in tiles + (loss_ref, dgf_ref, dsc_ref, dpw_ref):
                ref[...] = jnp.zeros_like(ref)

        @pl.when(i > 0)
        def _():
            xt = x_ref[...]
            hb = h_ref[...]
            pc = _dot_nt(hb, wc_ref[...])
            za, gp, ga = pc[:, :512], pc[:, 512:1536], pc[:, 1536:]
            of = o_ref[...].astype(F32)
            sza = _sigmoid(za)
            silu_za = za * sza
            ya = (of * silu_za).astype(BF16)
            ya_ref[...] = ya
            aa = _dot(ya, wupa_ref[...])

            u = u_ref[...]
            zp = zp_ref[...]
            counts = _pool_counts(i * tm, tm)
            ps = _pool_means(jnp.concatenate([uh_ref[...], u], axis=0), u, counts)
            pbs = [p.astype(BF16) for p in ps]
            ppw = jnp.concatenate([_dot(pbs[g], pw_ref[g]) for g in range(4)], axis=1)
            sc = sc_ref[...]
            szp = _sigmoid(zp)
            silu_zp = zp * szp
            ypre = ppw * sc
            yp = (ypre * silu_zp).astype(BF16)
            yp_ref[...] = yp
            ap = _dot(yp, wupp_ref[...])

            sgp, sga = _sigmoid(gp), _sigmoid(ga)
            mg = (sgp * ap + sga * aa).astype(BF16)
            mg_ref[...] = mg
            h2 = xt + _dot(mg, wout_ref[...])
            r2 = lax.rsqrt(jnp.mean(h2 * h2, axis=-1, keepdims=True) + RMS_EPS)
            h2n = h2 * r2
            gf = gf_ref[...]
            diff = h2n * gf - t_ref[...]
            loss_ref[...] += 0.5 * jnp.sum(jnp.mean(diff * diff, axis=-1, keepdims=True), axis=0, keepdims=True)
            dy = diff * (1.0 / D_MODEL)
            dgf_ref[...] += jnp.sum(dy * h2n, axis=0, keepdims=True)
            dyg = dy * gf
            dh2 = r2 * (dyg - h2n * jnp.mean(dyg * h2n, axis=-1, keepdims=True))
            dh2_ref[...] = dh2
            dmg = _dot_nt(dh2.astype(BF16), wout_ref[...])
            dap = (dmg * sgp).astype(BF16)
            daa = (dmg * sga).astype(BF16)
            dap_ref[...] = dap
            daa_ref[...] = daa
            dmid_ref[:, MID_GP:MID_GA] = (dmg * ap * sgp * (1.0 - sgp)).astype(BF16)
            dmid_ref[:, MID_GA:] = (dmg * aa * sga * (1.0 - sga)).astype(BF16)
            dyp = _dot_nt(dap, wupp_ref[...])
            dya = _dot_nt(daa, wupa_ref[...])
            do_ref[...] = (dya * silu_za).astype(BF16)
            dmid_ref[:, MID_ZA:MID_GP] = (dya * of * (sza * (1.0 + za * (1.0 - sza)))).astype(BF16)

            dypre = dyp * silu_zp
            dmid_ref[:, :MID_ZA] = (dyp * ypre * (szp * (1.0 + zp * (1.0 - szp)))).astype(BF16)
            dsc_ref[...] += jnp.sum(dypre * ppw, axis=0, keepdims=True)
            dppw = (dypre * sc).astype(BF16)
            dpns = []
            for g in range(4):
                dg = dppw[:, POOL_GROUP * g:POOL_GROUP * (g + 1)]
                dpw_ref[g] += _dot_tn(pbs[g], dg)
                dpns.append(_dot_nt(dg, pw_ref[g]) / counts[g])
            dpn_ref[...] = jnp.concatenate(dpns, axis=1)

    real = lambda w: pl.BlockSpec((tm, w), lambda i: (jnp.maximum(i - 1, 0), 0))
    row = lambda w: pl.BlockSpec((tm, w), lambda i: (i, 0))
    in_specs = [
        real(D_MODEL), real(D_MODEL), row(D_MODEL), row(512), row(512),
        pl.BlockSpec((MAX_WINDOW, 512), lambda i: (jnp.maximum(i * halo_blocks - 1, 0), 0)), row(512),
        _const((2560, D_MODEL), (1, 0)), _const((POOL_WIDTH, D_MODEL)), _const((ATTN_WIDTH, D_MODEL)),
        _const((D_MODEL, D_MODEL)), _const((4, POOL_GROUP, POOL_GROUP)), _const((1, POOL_WIDTH)), _const((1, D_MODEL)),
    ]
    sd = jax.ShapeDtypeStruct
    out_shape = [
        sd((lp, D_MODEL), F32),
        sd((lp, D_MODEL), BF16),
        sd((lp, 512), BF16),
        sd((lp, 512), BF16),
        sd((lp, D_MODEL), BF16),
        sd((lp, D_MODEL), BF16),
        sd((lp, 512), BF16),
        sd((lp, MID_WIDTH), BF16),
        sd((lp, 512), F32),
        sd((1, LANES), F32),
        sd((1, D_MODEL), F32),
        sd((1, 512), F32),
        sd((4, POOL_GROUP, POOL_GROUP), F32),
    ]
    keep = lambda shape: pl.BlockSpec(shape, lambda i: (0,) * len(shape))
    out_specs = [row(D_MODEL), row(D_MODEL), row(512), row(512), row(D_MODEL), row(D_MODEL), row(512),
                 row(MID_WIDTH), row(512),
                 keep((1, LANES)), keep((1, D_MODEL)), keep((1, 512)), keep((4, POOL_GROUP, POOL_GROUP))]
    return pl.pallas_call(
        body, name="middle", grid=(nt,), out_shape=out_shape, in_specs=in_specs, out_specs=out_specs,
        compiler_params=_params(("arbitrary",)),
    )(x, target, h, o, u, u, zp, w_main, w_up_pool, w_up_attn, w_out, pool_w, pool_scale, final_g)


DUF_WIDTH = POOL_WIDTH + LANES


def _sequence_grads(dpn, dc, sneg):
    lp = dpn.shape[0]
    tm = ROW_TILE
    nt = lp // tm
    halo_blocks = tm // MAX_WINDOW
    last_halo = lp // MAX_WINDOW - 1

    def body(dpn_ref, dpnh_ref, dc_ref, sn_ref, duf_ref, dbf_ref, carry_ref):
        i = pl.program_id(0)
        t = nt - 1 - i

        @pl.when(i == 0)
        def _():
            carry_ref[...] = jnp.zeros_like(carry_ref)
            dbf_ref[...] = jnp.zeros_like(dbf_ref)

        dpn_t = dpn_ref[...]
        ahead = jnp.where(i == 0, jnp.zeros_like(dpnh_ref), dpnh_ref[...])
        ext = jnp.concatenate([dpn_t, ahead], axis=0)
        counts = _pool_counts(t * tm, tm)
        for g, w in enumerate(POOL_WINDOWS):
            s = ext[:, POOL_GROUP * g:POOL_GROUP * (g + 1)]
            sh = 1
            while sh < w:
                s = s + pltpu.roll(s, tm + MAX_WINDOW - sh, axis=0)
                sh *= 2
            du = s[:tm, :] - dpn_t[:, POOL_GROUP * g:POOL_GROUP * (g + 1)] * counts[g]
            duf_ref[:, POOL_GROUP * g:POOL_GROUP * (g + 1)] = du.astype(BF16)

        dct = dc_ref[:, 0:LANES]
        for hp in range(1, N_HEADS // 2):
            dct = dct + pltpu.roll(dc_ref[:, LANES * hp:LANES * (hp + 1)], 2 * hp, axis=1)
        rloc = lax.broadcasted_iota(jnp.int32, (tm, LANES), 0)
        sh = 1
        while sh < tm:
            dct = dct + jnp.where(rloc + sh < tm, pltpu.roll(dct, tm - sh, axis=0), 0.0)
            sh *= 2
        dct = dct + carry_ref[...]
        carry_ref[...] = dct[0:1, :]
        df = dct * sn_ref[...]
        dbf_ref[...] += jnp.sum(df, axis=0, keepdims=True)
        duf_ref[:, POOL_WIDTH:] = df.astype(BF16)

    rev = lambda w: pl.BlockSpec((tm, w), lambda i: (nt - 1 - i, 0))
    return pl.pallas_call(
        body, name="sequence_grads", grid=(nt,),
        out_shape=[jax.ShapeDtypeStruct((lp, DUF_WIDTH), BF16), jax.ShapeDtypeStruct((1, LANES), F32)],
        in_specs=[rev(512),
                  pl.BlockSpec((MAX_WINDOW, 512), lambda i: (jnp.minimum((nt - i) * halo_blocks, last_halo), 0)),
                  rev(512), rev(LANES)],
        out_specs=[rev(DUF_WIDTH), pl.BlockSpec((1, LANES), lambda i: (0, 0))],
        scratch_shapes=[pltpu.VMEM((1, LANES), F32)],
        compiler_params=_params(("arbitrary",)),
    )(dpn, dpn, dc, sneg)


def _backward_in(x, tile0, norm_g, dh2, duf, dqkv, dmid, w_main, w_f, behind):
    seq = x.shape[0]
    tm = ROW_TILE
    nt = seq // tm + 1
    nx = len(behind)

    def body(x_ref, t0_ref, g_ref, dh2_ref, du_ref, df_ref, dqkv_ref, dzp_ref, dza_ref, dgp_ref, dga_ref,
             wm_ref, wf_ref, *rest):
        gx_ref, gmeta_ref, dg_ref = rest[nx:nx + 3]
        dproj_ref = rest[2 * nx + 3]
        finish_exchange = _behind(behind, rest[:nx], rest[nx + 3:2 * nx + 3], rest[2 * nx + 4:])
        t = pl.program_id(0)

        @pl.when(t == 0)
        def _():
            dg_ref[...] = jnp.zeros_like(dg_ref)

        dproj_ref[:, 0:512] = du_ref[...]
        dproj_ref[:, 512:1024] = dzp_ref[...]
        dproj_ref[:, 1024:1536] = dqkv_ref[0]
        dproj_ref[:, 1536:2048] = dqkv_ref[1]
        dproj_ref[:, 2048:2560] = dqkv_ref[2]
        dproj_ref[:, 2560:3072] = dza_ref[...]
        dproj_ref[:, 3072:4096] = dgp_ref[...]
        dproj_ref[:, 4096:5120] = dga_ref[...]
        dh = _dot(dproj_ref[...], wm_ref[...]) + _dot(df_ref[...], wf_ref[...])
        xt = jnp.where(t == 0, t0_ref[...], x_ref[...])
        r = lax.rsqrt(jnp.mean(xt * xt, axis=-1, keepdims=True) + RMS_EPS)
        xn = xt * r
        dg_ref[...] += jnp.sum(dh * xn, axis=0, keepdims=True)
        dhg = dh * g_ref[...]
        dx = dh2_ref[...] + r * (dhg - xn * jnp.mean(dhg * xn, axis=-1, keepdims=True))

        @pl.when(t > 0)
        def _():
            gx_ref[...] = dx

        @pl.when(t == 0)
        def _():
            gmeta_ref[...] = dx[PAD:, :]
            gx_ref[...] = jnp.zeros_like(gx_ref)

        finish_exchange()

    row = lambda w, j=0: pl.BlockSpec((tm, w), lambda i: (i, j))
    real = pl.BlockSpec((tm, D_MODEL), lambda i: (jnp.maximum(i - 1, 0), 0))
    hbm = pl.BlockSpec(memory_space=pl.ANY)
    in_specs = [
        real, _const((tm, D_MODEL)), _const((1, D_MODEL)), row(D_MODEL),
        row(POOL_WIDTH), row(LANES, POOL_WIDTH // LANES), pl.BlockSpec((3, tm, ATTN_WIDTH), lambda i: (0, i, 0)),
        row(512, 0), row(512, 1), row(1024, 1), row(1024, 2),
        _const((N_MAIN, D_MODEL)), _const((LANES, D_MODEL)),
    ] + [hbm] * nx
    sd = jax.ShapeDtypeStruct
    out_shape = [sd((seq, D_MODEL), F32), sd((N_META, D_MODEL), F32), sd((1, D_MODEL), F32)] + _exchange_results(behind)
    keep = lambda shape: pl.BlockSpec(shape, lambda i: (0,) * len(shape))
    out_specs = [real, keep((N_META, D_MODEL)), keep((1, D_MODEL))] + [hbm] * nx
    return pl.pallas_call(
        body, name="backward_in", grid=(nt,), out_shape=out_shape, in_specs=in_specs, out_specs=out_specs,
        scratch_shapes=[pltpu.VMEM((tm, N_MAIN), BF16)] + _exchange_semaphores(nx),
        compiler_params=_params(("arbitrary",)),
    )(x, tile0, norm_g, dh2, duf, duf, dqkv, dmid, dmid, dmid, dmid, w_main, w_f, *[a for _, a, _ in behind])


def _matmul_tn(name, a, b, tn):
    lp, m = a.shape
    n = b.shape[1]

    def body(a_ref, b_ref, c_ref):
        c_ref[...] = _dot_tn(a_ref[...].astype(BF16), b_ref[...].astype(BF16))

    return pl.pallas_call(
        body, name=name, grid=(n // tn,), out_shape=jax.ShapeDtypeStruct((m, n), F32),
        in_specs=[_const((lp, m)), pl.BlockSpec((lp, tn), lambda j: (0, j))],
        out_specs=pl.BlockSpec((m, tn), lambda j: (0, j)),
        compiler_params=_params(("arbitrary",)),
    )(a, b)


def _matmul_tn_rows(name, a, b, tm):
    lp, m = a.shape
    n = b.shape[1]

    def body(a_ref, b_ref, c_ref):
        c_ref[...] = _dot_tn(a_ref[...].astype(BF16), b_ref[...].astype(BF16))

    return pl.pallas_call(
        body, name=name, grid=(m // tm,), out_shape=jax.ShapeDtypeStruct((m, n), F32),
        in_specs=[pl.BlockSpec((lp, tm), lambda j: (0, j)), _const((lp, n))],
        out_specs=pl.BlockSpec((tm, n), lambda j: (j, 0)),
        compiler_params=_params(("arbitrary",)),
    )(a, b)


def _matmul_tn_stack(name, a, b):
    n_blocks, lp, m = a.shape
    n = b.shape[1]

    def body(a_ref, b_ref, c_ref):
        c_ref[...] = _dot_tn(a_ref[...], b_ref[...])

    return pl.pallas_call(
        body, name=name, grid=(n_blocks,), out_shape=jax.ShapeDtypeStruct((n_blocks * m, n), F32),
        in_specs=[pl.BlockSpec((None, lp, m), lambda j: (j, 0, 0)), _const((lp, n))],
        out_specs=pl.BlockSpec((m, n), lambda j: (j, 0)),
        compiler_params=_params(("arbitrary",)),
    )(a, b)


def _adamw_step(p_ref, w_ref, m_ref, v_ref, g_ref, d_ref, mo_ref, vo_ref):
    g = p_ref[0].astype(F32)
    for s in range(1, p_ref.shape[0]):
        g = g + p_ref[s].astype(F32)
    m_new = ADAM_B1 * m_ref[...] + (1.0 - ADAM_B1) * g
    v_new = ADAM_B2 * v_ref[...] + (1.0 - ADAM_B2) * (g * g)
    m_hat = m_new / (1.0 - ADAM_B1 ** ADAM_STEP)
    v_hat = v_new / (1.0 - ADAM_B2 ** ADAM_STEP)
    g_ref[...] = g
    d_ref[...] = -ADAM_LR * (m_hat / (jnp.sqrt(v_hat) + ADAM_EPS) + ADAM_WD * w_ref[...])
    mo_ref[...] = m_new
    vo_ref[...] = v_new


def _adamw_small(name, groups, loss_parts):
    n = len(groups)

    def body(*refs):
        ins, outs = refs[:4 * n + 1], refs[4 * n + 1:]
        for j in range(n):
            _adamw_step(*ins[4 * j:4 * j + 4], *outs[4 * j:4 * j + 4])
        total = ins[-1][0]
        for s in range(1, N_DEV):
            total = total + ins[-1][s]
        outs[-1][...] = total

    vmem = pl.BlockSpec(memory_space=pltpu.VMEM)
    out_shape = [jax.ShapeDtypeStruct(w.shape, F32) for _, w, _, _ in groups for _ in range(4)]
    out_shape.append(jax.ShapeDtypeStruct(loss_parts.shape[1:], F32))
    res = pl.pallas_call(
        body, name=name, out_shape=out_shape, in_specs=[vmem] * (4 * n + 1), out_specs=[vmem] * (4 * n + 1),
        compiler_params=_params(),
    )(*[a for g in groups for a in g], loss_parts)
    return [res[4 * j:4 * j + 4] for j in range(n)], res[-1]


def _adamw(name, parts, w, m, v, rows, cols=None):
    r, c_all = w.shape
    c = cols or c_all
    n_parts = parts.shape[0]

    def body(p_ref, w_ref, m_ref, v_ref, g_ref, d_ref, mo_ref, vo_ref):
        _adamw_step(p_ref, w_ref, m_ref, v_ref, g_ref, d_ref, mo_ref, vo_ref)

    blk = pl.BlockSpec((rows, c), lambda i, j: (i, j))
    return pl.pallas_call(
        body, name=name, grid=(r // rows, c_all // c), out_shape=[jax.ShapeDtypeStruct((r, c_all), F32)] * 4,
        in_specs=[pl.BlockSpec((n_parts, rows, c), lambda i, j: (0, i, j)), blk, blk, blk],
        out_specs=[blk] * 4,
        compiler_params=_params(("arbitrary", "arbitrary")),
    )(parts, w, m, v)


def _pair_sum(name, mine, theirs, rows):
    n, r, c = mine.shape

    def body(a_ref, b_ref, o_ref):
        o_ref[...] = (a_ref[...].astype(F32) + b_ref[...].astype(F32)).astype(BF16)

    blk = pl.BlockSpec((1, rows, c), lambda j, i: (j, i, 0))
    return pl.pallas_call(
        body, name=name, grid=(n, r // rows), out_shape=jax.ShapeDtypeStruct((n, r, c), BF16),
        in_specs=[blk, blk], out_specs=blk,
        compiler_params=_params(("arbitrary", "arbitrary")),
    )(mine, theirs)


def _by_core(slots):
    by_core = slots.reshape((4, 2) + slots.shape[1:]).swapaxes(0, 1)
    c = lax.axis_index("c")
    return (lax.dynamic_index_in_dim(by_core, c, 0, keepdims=False),
            lax.dynamic_index_in_dim(by_core, 1 - c, 0, keepdims=False))


def _columns_to_slots(a):
    r, c8 = a.shape
    return a.reshape(r, N_DEV, c8 // N_DEV).transpose(1, 0, 2)


def _slots_to_columns(a):
    n, r, c = a.shape
    return a.transpose(1, 0, 2).reshape(r, n * c)


def kernel(x, meta_tokens, norm_g, w_in, b_forget, pool_w, pool_scale, w_up_pool, w_up_attn, w_out, final_norm_g, loss_target, m_meta_tokens, m_norm_g, m_w_in, m_b_forget, m_pool_w, m_pool_scale, m_w_up_pool, m_w_up_attn, m_w_out, m_final_norm_g, v_meta_tokens, v_norm_g, v_w_in, v_b_forget, v_pool_w, v_pool_scale, v_w_up_pool, v_w_up_attn, v_w_out, v_final_norm_g):
    xs = x[0]
    target = loss_target[0]

    g_in, g_meta = _gather_two_level("gather_weights", [w_in[0].T.astype(BF16), meta_tokens], (320, 8))
    w_full = g_in.reshape(N_DEV * g_in.shape[1], D_MODEL)
    w_main = jnp.concatenate([w_full[:N_BEFORE_F], w_full[N_BEFORE_F + N_HEADS:]], axis=0)
    w_f = jnp.pad(w_full[N_BEFORE_F:N_BEFORE_F + N_HEADS], ((0, LANES - N_HEADS), (0, 0)))
    meta = _slots_to_columns(g_meta)
    tile0 = jnp.concatenate([jnp.zeros((PAD, D_MODEL), F32), meta], axis=0)
    b_f = jnp.pad(b_forget, ((0, 0), (0, LANES - N_HEADS)))
    pw_b = pool_w[0].astype(BF16)
    final_g = final_norm_g.reshape(1, D_MODEL)

    h, u, zp, k, v, qt, kt, vt, sneg = _forward_in(xs, tile0, norm_g, w_main, w_f, b_f)
    o, lse, g_upp, g_upa, g_out = _attention_forward(
        qt, k, vt, [("gather", w.astype(BF16), ALL_PEERS) for w in (w_up_pool[0], w_up_attn[0], w_out[0])])
    wupp = _slots_to_columns(g_upp)
    wupa = _slots_to_columns(g_upa)
    wout = g_out.reshape(D_MODEL, D_MODEL)
    (dh2, mg, yp, ya, dap, daa, do, dmid, dpn,
     loss_part, d_final_g, d_scale, d_pool_w) = _middle(xs, target, h, o, u, zp, w_main, wupp, wupa, wout,
                                                        pw_b, pool_scale, final_g)
    dw_out = _matmul_tn("grad_w_out", mg, dh2, 256)
    dw_upp = _matmul_tn("grad_w_up_pool", yp, dap, 512)
    dw_upa = _matmul_tn("grad_w_up_attn", ya, daa, 512)
    dqkv, dc, p_upp, p_upa, p_out, p_pool_w, p_scale, p_final_g = _attention_backward(
        qt, k, kt, v, do, o, lse,
        [("scatter", _columns_to_slots(dw_upp).astype(BF16), ALL_PEERS),
         ("scatter", _columns_to_slots(dw_upa).astype(BF16), ALL_PEERS),
         ("scatter", dw_out.reshape(N_DEV, D_MODEL // N_DEV, D_MODEL).astype(BF16), ALL_PEERS),
         ("gather", d_pool_w.reshape(4 * POOL_GROUP, POOL_GROUP), ALL_PEERS),
         ("gather", d_scale, ALL_PEERS), ("gather", d_final_g, ALL_PEERS)])
    duf, d_bf = _sequence_grads(dpn, dc, sneg)
    g_uf = _matmul_tn_rows("grad_w_in_pool_forget", duf, h, DUF_WIDTH)
    g_qkv = _matmul_tn_stack("grad_w_in_attention", dqkv, h)
    g_mid = _matmul_tn_rows("grad_w_in_gates", dmid, h, 512)
    dw_in = jnp.concatenate([g_uf[:POOL_WIDTH], g_mid[:MID_ZA], g_qkv, g_mid[MID_ZA:MID_GP],
                             g_uf[POOL_WIDTH:POOL_WIDTH + N_HEADS], g_mid[MID_GP:]], axis=0)
    dw_in = dw_in.reshape(N_DEV, dw_in.shape[0] // N_DEV, D_MODEL)
    mine, for_sibling = _by_core(dw_in)
    from_sibling, = _exchange("swap_with_sibling", [("swap", for_sibling.astype(BF16), (SIBLING,))])
    pair_sums = _pair_sum("pair_sum", mine, from_sibling, dw_in.shape[1])
    grad_x, d_meta, d_norm_g, p_in, p_bf, p_loss = _backward_in(
        xs, tile0, norm_g, dh2, duf, dqkv, dmid, w_main, w_f,
        [("chips", pair_sums, SAME_CORE), ("gather", d_bf, ALL_PEERS), ("gather", loss_part, ALL_PEERS)])
    p_meta, p_norm_g = _exchange(
        "exchange_gradients", [("scatter", _columns_to_slots(d_meta), ALL_PEERS), ("gather", d_norm_g, ALL_PEERS)])


    def pad_f(a):
        return jnp.pad(a, ((0, 0), (0, LANES - N_HEADS)))

    res = {}
    res["w_in"] = [a.T for a in _adamw("adamw_w_in", p_in, w_in[0].T, m_w_in[0].T, v_w_in[0].T, p_in.shape[1], 256)]
    res["w_up_pool"] = _adamw("adamw_w_up_pool", p_upp, w_up_pool[0], m_w_up_pool[0], v_w_up_pool[0], 512)
    res["w_up_attn"] = _adamw("adamw_w_up_attn", p_upa, w_up_attn[0], m_w_up_attn[0], v_w_up_attn[0], 512)
    res["w_out"] = _adamw("adamw_w_out", p_out, w_out[0], m_w_out[0], v_w_out[0], 128)
    flat = lambda a: a.reshape(4 * POOL_GROUP, POOL_GROUP)
    row = lambda a: a.reshape(1, D_MODEL)
    small, loss_row = _adamw_small(
        "adamw_small",
        [(p_meta, meta_tokens, m_meta_tokens, v_meta_tokens),
         (p_norm_g, norm_g, m_norm_g, v_norm_g),
         (p_bf, pad_f(b_forget), pad_f(m_b_forget), pad_f(v_b_forget)),
         (p_pool_w, flat(pool_w), flat(m_pool_w), flat(v_pool_w)),
         (p_scale, pool_scale, m_pool_scale, v_pool_scale),
         (p_final_g, final_g, row(m_final_norm_g), row(v_final_norm_g))],
        p_loss)
    res["meta_tokens"], res["norm_g"], bf, pw, res["pool_scale"], fg = small
    res["b_forget"] = [a[:, :N_HEADS] for a in bf]
    res["pool_w"] = [a.reshape(pool_w.shape) for a in pw]
    res["final_norm_g"] = [a.reshape(D_MODEL) for a in fg]
    loss = loss_row[0, 0]
    for name in ("w_in", "w_up_pool", "w_up_attn", "w_out"):
        res[name] = [a[None] for a in res[name]]

    order = ["meta_tokens", "norm_g", "w_in", "b_forget", "pool_w", "pool_scale", "w_up_pool", "w_up_attn", "w_out",
             "final_norm_g"]
    outs = [loss, grad_x[None]]
    for part in range(4):
        outs += [res[name][part] for name in order]
    return tuple(outs)
```

```python
import jax
import jax.numpy as jnp
from jax import lax
from jax.experimental import pallas as pl
from jax.experimental.pallas import tpu as pltpu

F32 = jnp.float32
BF16 = jnp.bfloat16

D_MODEL = 1024
N_META = 16
POOL_WIDTH = 512
ATTN_WIDTH = 512
N_HEADS = 8
HEAD_DIM = 64
POOL_WINDOWS = (2, 4, 8, 16)
POOL_GROUP = 128
MAX_WINDOW = 16
RMS_EPS = 1e-6
N_MAIN = 5120
N_BEFORE_F = 3072
N_DEV = 8
LANES = 128

ROW_TILE = 256
ATT_TILE = 256
ATT_Q_BLOCKS_FWD = 8
ATT_Q_BLOCKS_BWD = 4
PAD = ROW_TILE - N_META
FIRST_KEY = PAD // LANES * LANES
VMEM_LIMIT = 56 * 1024 * 1024

ADAM_LR = 0.001
ADAM_B1 = 0.9
ADAM_B2 = 0.999
ADAM_EPS = 1e-08
ADAM_WD = 0.01
ADAM_STEP = 10

MID_ZA, MID_GP, MID_GA, MID_WIDTH = 512, 1024, 2048, 3072
NEG = -1e30
MESH = pl.DeviceIdType.MESH


def _params(sem=None):
    kw = dict(vmem_limit_bytes=VMEM_LIMIT)
    if sem is not None:
        kw["dimension_semantics"] = sem
    return pltpu.CompilerParams(**kw)


def _const(shape, block_index=None):
    idx = block_index or (0,) * len(shape)
    return pl.BlockSpec(shape, lambda i: idx, pipeline_mode=pl.Buffered(1))


def _sigmoid(x):
    return jax.nn.sigmoid(x)


def _dot(a, b):
    return jnp.dot(a, b, preferred_element_type=F32)


def _dot_nt(a, b):
    return lax.dot_general(a, b, (((1,), (1,)), ((), ())), preferred_element_type=F32)


def _dot_tn(a, b):
    return lax.dot_general(a, b, (((0,), (0,)), ((), ())), preferred_element_type=F32)


def _pool_counts(first_row, rows):
    row = first_row + lax.broadcasted_iota(jnp.int32, (rows, 1), 0)
    pos1 = row - PAD + 1
    return [jnp.clip(pos1, 1, w).astype(F32) for w in POOL_WINDOWS]


def _pool_means(u_ext, u, counts):
    rows = u.shape[0]
    out = []
    for g, w in enumerate(POOL_WINDOWS):
        s = u_ext[:, POOL_GROUP * g:POOL_GROUP * (g + 1)]
        sh = 1
        while sh < w:
            s = s + pltpu.roll(s, sh, axis=0)
            sh *= 2
        out.append(s[MAX_WINDOW:MAX_WINDOW + rows, :] / counts[g] - u[:, POOL_GROUP * g:POOL_GROUP * (g + 1)])
    return out


Q_BIAS, Q_ONES, Q_LSE = 64, 67, 70
K_ONES, K_BIAS, K_ONES2 = 64, 67, 70
V_ONES = 64
DO_BIAS = 64


def _lane_ones(lane, ranges):
    hit = None
    for lo, hi in ranges:
        r = (lane >= lo) & (lane < hi)
        hit = r if hit is None else hit | r
    return jnp.where(hit, 1.0, 0.0)


def _put3(base, lane, first, x):
    hi = x.astype(BF16).astype(F32)
    rest = x - hi
    mid = rest.astype(BF16).astype(F32)
    lo = (rest - mid).astype(BF16).astype(F32)
    for j, piece in enumerate((hi, mid, lo)):
        base = jnp.where(lane == first + j, piece, base)
    return base


SIBLING = 1
SAME_CORE = (2, 4, 6)
ALL_PEERS = (1, 2, 3, 4, 5, 6, 7)


def _place():
    return lax.axis_index("x"), lax.axis_index("y"), lax.axis_index("c")


def _peer(r):
    x, y, c = _place()
    return (1 - x if r & 4 else x, 1 - y if r & 2 else y, 1 - c if r & 1 else c)


def _device_slot(p):
    return 4 * p[0] + 2 * p[1] + p[2]


def _chip_slot(p):
    return 2 * p[0] + p[1]


def _exchange(name, items):
    n = len(items)

    def body(*refs):
        copies = _exchange_copies(items, refs[:n], refs[n:2 * n], *refs[2 * n:])
        for cp in copies:
            cp.start()
        for cp in copies:
            cp.wait()

    hbm = pl.BlockSpec(memory_space=pl.ANY)
    return pl.pallas_call(
        body, name=name, out_shape=_exchange_results(items),
        in_specs=[hbm] * n, out_specs=[hbm] * n,
        scratch_shapes=_exchange_semaphores(n),
    )(*[a for _, a, _ in items])


def _exchange_results(items):
    return [jax.ShapeDtypeStruct(((N_DEV,) if kind == "gather" else ()) + a.shape, a.dtype) for kind, a, _ in items]


def _exchange_semaphores(n):
    return [pltpu.SemaphoreType.DMA((n, N_DEV - 1)), pltpu.SemaphoreType.DMA((n, N_DEV - 1)),
            pltpu.SemaphoreType.DMA((n,))]


def _exchange_copies(items, ins, outs, send_sems, recv_sems, local_sems):
    me = _place()
    copies = []
    for a, (kind, _, peers) in enumerate(items):
        slot = _chip_slot if kind == "chips" else _device_slot
        for r in peers:
            peer = _peer(r)
            src = ins[a] if kind in ("swap", "gather") else ins[a].at[slot(peer)]
            dst = outs[a] if kind == "swap" else outs[a].at[slot(me)]
            copies.append(pltpu.make_async_remote_copy(
                src_ref=src, dst_ref=dst, send_sem=send_sems.at[a, r - 1], recv_sem=recv_sems.at[a, r - 1],
                device_id=peer, device_id_type=MESH))
        if kind != "swap":
            src = ins[a] if kind == "gather" else ins[a].at[slot(me)]
            copies.append(pltpu.make_async_copy(src, outs[a].at[slot(me)], local_sems.at[a]))
    return copies


def _gather_two_level(name, arrays, halves):
    n = len(arrays)
    x_flip, y_flip, both = 4, 2, 6
    to_sibling, to_x, to_y, on_over_y, on_over_x, x_to_sibling, y_to_sibling, d_to_sibling = range(8)

    def body(*refs):
        ins, outs = refs[:n], refs[n:2 * n]
        send_sems, recv_sems, local_sems = refs[2 * n:]
        me, sibling = _place(), _peer(SIBLING)
        xn, yn, dg = _peer(x_flip), _peer(y_flip), _peer(both)

        def rows(a, h):
            return pl.ds(0, halves[a]) if h == 0 else pl.ds(halves[a], arrays[a].shape[0] - halves[a])

        def copy(a, h, k, block, to, own=False):
            dst = outs[a].at[_device_slot(block)].at[rows(a, h)]
            return pltpu.make_async_remote_copy(
                src_ref=ins[a].at[rows(a, h)] if own else dst, dst_ref=dst,
                send_sem=send_sems.at[2 * a + h, k], recv_sem=recv_sems.at[2 * a + h, k],
                device_id=to, device_id_type=MESH)

        sends, mine = [], []

        def start(cp):
            cp.start()
            sends.append(cp)

        for a in range(n):
            cp = pltpu.make_async_copy(ins[a], outs[a].at[_device_slot(me)], local_sems.at[a])
            cp.start()
            mine.append(cp)
        for a in range(n):
            start(copy(a, 0, to_x, me, xn, own=True))
            start(copy(a, 1, to_y, me, yn, own=True))
        for a in range(n):
            start(copy(a, 0, to_y, me, yn, own=True))
            start(copy(a, 1, to_x, me, xn, own=True))
        for a in range(n):
            for h in range(2):
                start(copy(a, h, to_sibling, me, sibling, own=True))
        for a in range(n):
            copy(a, 0, to_x, xn, me).wait_recv()
            start(copy(a, 0, on_over_y, xn, yn))
            start(copy(a, 0, x_to_sibling, xn, sibling))
            copy(a, 1, to_y, yn, me).wait_recv()
            start(copy(a, 1, on_over_x, yn, xn))
            start(copy(a, 1, y_to_sibling, yn, sibling))
        for a in range(n):
            copy(a, 0, to_y, yn, me).wait_recv()
            start(copy(a, 0, y_to_sibling, yn, sibling))
            copy(a, 1, to_x, xn, me).wait_recv()
            start(copy(a, 1, x_to_sibling, xn, sibling))
        for a in range(n):
            copy(a, 0, on_over_y, dg, me).wait_recv()
            start(copy(a, 0, d_to_sibling, dg, sibling))
            copy(a, 1, on_over_x, dg, me).wait_recv()
            start(copy(a, 1, d_to_sibling, dg, sibling))
        for a in range(n):
            for h in range(2):
                copy(a, h, to_sibling, sibling, me).wait_recv()
                for k, r in ((x_to_sibling, x_flip), (y_to_sibling, y_flip), (d_to_sibling, both)):
                    copy(a, h, k, _peer(r | SIBLING), me).wait_recv()
        for cp in sends:
            cp.wait_send()
        for cp in mine:
            cp.wait()

    hbm = pl.BlockSpec(memory_space=pl.ANY)
    return pl.pallas_call(
        body, name=name, out_shape=[jax.ShapeDtypeStruct((N_DEV,) + a.shape, a.dtype) for a in arrays],
        in_specs=[hbm] * n, out_specs=[hbm] * n,
        scratch_shapes=[pltpu.SemaphoreType.DMA((2 * n, 8)), pltpu.SemaphoreType.DMA((2 * n, 8)),
                        pltpu.SemaphoreType.DMA((n,))],
    )(*arrays)


def _forward_in(x, tile0, norm_g, w_main, w_f, b_f):
    seq = x.shape[0]
    nt = seq // ROW_TILE + 1
    lp = nt * ROW_TILE
    tm = ROW_TILE

    def body(x_ref, t0_ref, g_ref, wa_ref, wf_ref, bf_ref,
             h_ref, u_ref, zp_ref, k_ref, v_ref, qt_ref, kt_ref, vt_ref, sn_ref, carry_ref):
        i = pl.program_id(0)

        @pl.when(i == 0)
        def _():
            carry_ref[...] = jnp.zeros_like(carry_ref)

        xt = jnp.where(i == 0, t0_ref[...], x_ref[...])
        r = lax.rsqrt(jnp.mean(xt * xt, axis=-1, keepdims=True) + RMS_EPS)
        h = (xt * r * g_ref[...]).astype(BF16)
        h_ref[...] = h
        pa = _dot_nt(h, wa_ref[...])
        u_ref[...] = pa[:, :512]
        zp_ref[...] = pa[:, 512:1024]

        fl = _dot_nt(h, wf_ref[...]) + bf_ref[...]
        row = i * tm + lax.broadcasted_iota(jnp.int32, (tm, LANES), 0)
        rloc = lax.broadcasted_iota(jnp.int32, (tm, LANES), 0)
        lane = lax.broadcasted_iota(jnp.int32, (tm, LANES), 1)
        live = (row >= PAD) & (lane < N_HEADS)
        logf = jnp.minimum(fl, 0.0) - jnp.log1p(jnp.exp(-jnp.abs(fl)))
        cs = jnp.where(live, logf, 0.0)
        sh = 1
        while sh < tm:
            cs = cs + jnp.where(rloc >= sh, pltpu.roll(cs, sh, axis=0), 0.0)
            sh *= 2
        cs = cs + carry_ref[...]
        carry_ref[...] = cs[tm - 1:tm, :]
        sn_ref[...] = jnp.where(live, _sigmoid(-fl), 0.0)

        rows1 = i * tm + lax.broadcasted_iota(jnp.int32, (tm, 1), 0)
        cols1 = i * tm + lax.broadcasted_iota(jnp.int32, (1, tm), 1)
        sub = lax.broadcasted_iota(jnp.int32, (LANES, tm), 0)
        ones_k = _lane_ones(lane, ((K_ONES, K_ONES + 3), (K_ONES2, K_ONES2 + 3)))
        ones_v = _lane_ones(lane, ((V_ONES, V_ONES + 3),))
        ones_q_t = _lane_ones(sub, ((Q_ONES, Q_ONES + 3),))
        ones_k_t = _lane_ones(sub, ((K_ONES, K_ONES + 3), (K_ONES2, K_ONES2 + 3)))
        ones_v_t = _lane_ones(sub, ((V_ONES, V_ONES + 3),))
        cs_t = cs.T
        for hp in range(N_HEADS // 2):
            kp = pa[:, 1536 + LANES * hp:1536 + LANES * (hp + 1)]
            vp = pa[:, 2048 + LANES * hp:2048 + LANES * (hp + 1)]
            qp_t = (pa[:, 1024 + LANES * hp:1024 + LANES * (hp + 1)] * 0.125).T
            kp_t, vp_t = kp.T, vp.T
            for e in range(2):
                head = 2 * hp + e
                if e:
                    kp, vp = pltpu.roll(kp, HEAD_DIM, axis=1), pltpu.roll(vp, HEAD_DIM, axis=1)
                    qp_t, kp_t, vp_t = (pltpu.roll(a, HEAD_DIM, axis=0) for a in (qp_t, kp_t, vp_t))
                c_row = cs_t[head:head + 1, :]
                minus_ck_row = jnp.where(cols1 >= PAD, -c_row, NEG)
                qt_ref[head] = jnp.where(sub < HEAD_DIM, qp_t, _rows3(Q_BIAS, c_row) + ones_q_t).astype(BF16)
                kt_ref[head] = jnp.where(sub < HEAD_DIM, kp_t, _rows3(K_BIAS, minus_ck_row) + ones_k_t).astype(BF16)
                vt_ref[head] = jnp.where(sub < HEAD_DIM, vp_t, ones_v_t).astype(BF16)
                minus_ck = jnp.where(rows1 >= PAD, -cs[:, head:head + 1], NEG)
                k_ref[head] = jnp.where(lane < HEAD_DIM, kp, _put3(ones_k, lane, K_BIAS, minus_ck)).astype(BF16)
                v_ref[head] = jnp.where(lane < HEAD_DIM, vp, ones_v).astype(BF16)

    row_f32 = lambda w: pl.BlockSpec((tm, w), lambda i: (i, 0))
    out_shape = [
        jax.ShapeDtypeStruct((lp, D_MODEL), BF16),
        jax.ShapeDtypeStruct((lp, POOL_WIDTH), F32),
        jax.ShapeDtypeStruct((lp, POOL_WIDTH), F32),
        jax.ShapeDtypeStruct((N_HEADS, lp, LANES), BF16),
        jax.ShapeDtypeStruct((N_HEADS, lp, LANES), BF16),
        jax.ShapeDtypeStruct((N_HEADS, LANES, lp), BF16),
        jax.ShapeDtypeStruct((N_HEADS, LANES, lp), BF16),
        jax.ShapeDtypeStruct((N_HEADS, LANES, lp), BF16),
        jax.ShapeDtypeStruct((lp, LANES), F32),
    ]
    heads = pl.BlockSpec((N_HEADS, tm, LANES), lambda i: (0, i, 0))
    heads_t = pl.BlockSpec((N_HEADS, LANES, tm), lambda i: (0, 0, i))
    out_specs = [row_f32(D_MODEL), row_f32(512), row_f32(512), heads, heads, heads_t, heads_t, heads_t,
                 row_f32(LANES)]
    in_specs = [
        pl.BlockSpec((tm, D_MODEL), lambda i: (jnp.maximum(i - 1, 0), 0)),
        _const((tm, D_MODEL)), _const((1, D_MODEL)),
        _const((2560, D_MODEL)), _const((LANES, D_MODEL)), _const((1, LANES)),
    ]
    return pl.pallas_call(
        body, name="forward_in", grid=(nt,), out_shape=out_shape, in_specs=in_specs, out_specs=out_specs,
        scratch_shapes=[pltpu.VMEM((1, LANES), F32)],
        compiler_params=_params(("arbitrary",)),
    )(x, tile0, norm_g, w_main, w_f, b_f)


def _pair_lanes(a0, a1):
    lane = lax.broadcasted_iota(jnp.int32, a0.shape, 1)
    return jnp.where(lane < HEAD_DIM, a0, pltpu.roll(a1, HEAD_DIM, axis=1))


def _behind(items, ins, outs, sems):
    step, last = pl.program_id(0), pl.num_programs(0) - 1

    @pl.when(step == 0)
    def _():
        for cp in _exchange_copies(items, ins, outs, *sems):
            cp.start()

    def finish():
        @pl.when(step == last)
        def _():
            for cp in _exchange_copies(items, ins, outs, *sems):
                cp.wait()

    return finish


def _attention_forward(qt, k, vt, behind):
    lp = k.shape[1]
    tk = ATT_TILE
    q_blocks = ATT_Q_BLOCKS_FWD if (lp // tk - 1) % ATT_Q_BLOCKS_FWD == 0 else ATT_Q_BLOCKS_BWD
    tq_big = q_blocks * tk
    n_big = (lp // tk - 1) // q_blocks
    assert lp == tk + n_big * tq_big and q_blocks % 2 == 0
    nx = len(behind)

    def body(qt_ref, k_ref, vt_ref, *rest):
        o_ref, lse_ref = rest[nx:nx + 2]
        s_buf, m_scr, acc_scr = rest[2 * nx + 2:2 * nx + 5]
        finish_exchange = _behind(behind, rest[:nx], rest[nx + 2:2 * nx + 2], rest[2 * nx + 5:])

        def q_tile(q0, tq, pairs):
            first = q0 // tk
            qts = [qt_ref[e, :, pl.ds(q0, tq)] for e in range(2)]

            def block(kj):
                return pl.ds(kj * tk if isinstance(kj, int) else pl.multiple_of(kj * tk, tk), tk)

            def step(kj, rd, wr, c0=0, diagonal=False, keys=None):
                c1 = c0 + tk if diagonal else c0
                keys = block(kj) if keys is None else keys
                for e in range(2):
                    s = s_buf[rd, e, 0:keys.size, c0:tq]
                    if wr is not None:
                        s_buf[wr, e, :, c1:tq] = _dot(k_ref[e, block(kj + 1), :], qts[e][:, c1:tq])
                    if diagonal:
                        key = lax.broadcasted_iota(jnp.int32, s.shape, 0)
                        s = jnp.where(key <= lax.broadcasted_iota(jnp.int32, s.shape, 1), s, NEG)
                    m = m_scr[e, :, c0:tq]
                    m_new = jnp.maximum(m, jnp.max(s, axis=0, keepdims=True))
                    p = jnp.exp(s - m_new)
                    pv = _dot(vt_ref[e, :, keys], p.astype(BF16))
                    acc_scr[e, :, c0:tq] = jnp.exp(m - m_new) * acc_scr[e, :, c0:tq] + pv
                    m_scr[e, :, c0:tq] = m_new

            keys0 = block(0) if pairs is None else pl.ds(FIRST_KEY, tk - FIRST_KEY)
            for e in range(2):
                m_scr[e, :, 0:tq] = jnp.full((1, tq), NEG, F32)
                acc_scr[e, :, 0:tq] = jnp.zeros((LANES, tq), F32)
                s_buf[0, e, 0:keys0.size, 0:tq] = _dot(k_ref[e, keys0, :], qts[e])
            if pairs is None:
                step(0, 0, None, 0, True)
            else:
                step(0, 0, 1, keys=keys0)

                def two_steps(t, _):
                    step(1 + 2 * t, 1, 0)
                    step(2 + 2 * t, 0, 1)
                    return 0

                lax.fori_loop(0, pairs, two_steps, 0)
                for b in range(tq // tk):
                    step(first + b, (b + 1) % 2, b % 2 if (b + 1) * tk < tq else None, b * tk, True)
            outs, lses = [], []
            for e in range(2):
                acc = acc_scr[e, :, 0:tq]
                l = acc[V_ONES:V_ONES + 1, :]
                outs.append((acc / l).T)
                lses.append(m_scr[e, :, 0:tq] + jnp.log(l))
            o_ref[pl.ds(q0, tq), :] = _pair_lanes(outs[0], outs[1]).astype(BF16)
            lse_rows = jnp.concatenate(lses + [jnp.zeros((LANES - 2, tq), F32)], axis=0)
            lse_ref[pl.ds(q0, tq), :] = lse_rows.T

        q_tile(0, tk, None)

        def big_tile(i, _):
            q_tile(pl.multiple_of(tk + i * tq_big, tk), tq_big, (q_blocks // 2) * i)
            return 0

        lax.fori_loop(0, n_big, big_tile, 0)
        finish_exchange()

    pair = pl.BlockSpec((lp, LANES), lambda hp: (0, hp))
    heads = pl.BlockSpec((2, lp, LANES), lambda hp: (hp, 0, 0), pipeline_mode=pl.Buffered(1))
    heads_t = pl.BlockSpec((2, LANES, lp), lambda hp: (hp, 0, 0), pipeline_mode=pl.Buffered(1))
    hbm = pl.BlockSpec(memory_space=pl.ANY)
    return pl.pallas_call(
        body, name="attention_forward", grid=(N_HEADS // 2,),
        out_shape=[jax.ShapeDtypeStruct((lp, ATTN_WIDTH), BF16), jax.ShapeDtypeStruct((lp, ATTN_WIDTH), F32)]
        + _exchange_results(behind),
        in_specs=[heads_t, heads, heads_t] + [hbm] * nx,
        out_specs=[pair, pair] + [hbm] * nx,
        scratch_shapes=[pltpu.VMEM((2, 2, tk, tq_big), F32), pltpu.VMEM((2, 1, tq_big), F32),
                        pltpu.VMEM((2, LANES, tq_big), F32)] + _exchange_semaphores(nx),
        compiler_params=_params(("arbitrary",)),
    )(qt, k, vt, *[a for _, a, _ in behind])


def _rows3(first, x):
    sub = lax.broadcasted_iota(jnp.int32, (LANES, x.shape[1]), 0)
    hi = x.astype(BF16).astype(F32)
    rest = x - hi
    mid = rest.astype(BF16).astype(F32)
    lo = (rest - mid).astype(BF16).astype(F32)
    out = jnp.zeros((LANES, x.shape[1]), F32)
    for j, piece in enumerate((hi, mid, lo)):
        out = jnp.where(sub == first + j, piece, out)
    return out


def _attention_backward(qt, k, kt, v, do, o, lse, behind):
    lp = k.shape[1]
    tb = ATT_TILE
    nb = lp // tb
    tq_big = ATT_Q_BLOCKS_BWD * tb
    n_big = (nb - 1) // ATT_Q_BLOCKS_BWD
    assert lp == tb + n_big * tq_big and ATT_Q_BLOCKS_BWD % 2 == 0
    nx = len(behind)

    def body(qt_ref, k_ref, kt_ref, v_ref, do_ref, o_ref, lse_ref, *rest):
        dqkv_ref, dc_ref = rest[nx:nx + 2]
        q2_ref, do2_ref, dk_acc, dv_acc, dq_scr, s_buf = rest[2 * nx + 2:2 * nx + 8]
        finish_exchange = _behind(behind, rest[:nx], rest[nx + 2:2 * nx + 2], rest[2 * nx + 8:])
        sub = lax.broadcasted_iota(jnp.int32, (LANES, tb), 0)

        def lanes01(row0, row1):
            n = row0.shape[1]
            return jnp.concatenate([row0, row1, jnp.zeros((LANES - 2, n), F32)], axis=0).T

        def prepare(bi, _):
            r0 = pl.multiple_of(bi * tb, tb)
            queries = r0 + lax.broadcasted_iota(jnp.int32, (1, tb), 1)
            dob = do_ref[pl.ds(r0, tb), :].astype(F32)
            do_t = dob.T
            dd_t = (dob * o_ref[pl.ds(r0, tb), :].astype(F32)).T
            lse_t = lse_ref[pl.ds(r0, tb), :].T
            for e in range(2):
                delta = jnp.sum(dd_t[HEAD_DIM * e:HEAD_DIM * (e + 1), :], axis=0, keepdims=True)
                do_e = jnp.concatenate([do_t[HEAD_DIM * e:HEAD_DIM * (e + 1), :], jnp.zeros((HEAD_DIM, tb), F32)], axis=0)
                do2_ref[e, :, pl.ds(r0, tb)] = jnp.where(sub < HEAD_DIM, do_e, _rows3(DO_BIAS, -delta)).astype(BF16)
                minus_lse = jnp.where(queries >= PAD, -lse_t[e:e + 1, :], NEG)
                keep = (sub < Q_LSE) | (sub >= Q_LSE + 3)
                q2_ref[e, :, pl.ds(r0, tb)] = jnp.where(keep, qt_ref[e, :, pl.ds(r0, tb)].astype(F32),
                                                        _rows3(Q_LSE, minus_lse)).astype(BF16)
            return 0

        lax.fori_loop(0, nb, prepare, 0)
        dk_acc[...] = jnp.zeros_like(dk_acc)
        dv_acc[...] = jnp.zeros_like(dv_acc)

        def q_tile(q0, tq, pairs):
            first = q0 // tb
            qts = [q2_ref[e, :, pl.ds(q0, tq)] for e in range(2)]
            dots = [do2_ref[e, :, pl.ds(q0, tq)] for e in range(2)]

            def block(kj):
                return pl.ds(kj * tb if isinstance(kj, int) else pl.multiple_of(kj * tb, tb), tb)

            def step(kj, rd, wr, c0=0, diagonal=False, keys=None):
                c1 = c0 + tb if diagonal else c0
                keys = block(kj) if keys is None else keys
                for e in range(2):
                    s = s_buf[rd, e, 0:keys.size, c0:tq]
                    if wr is not None:
                        s_buf[wr, e, :, c1:tq] = _dot(k_ref[e, block(kj + 1), :], qts[e][:, c1:tq])
                    dpd = _dot(v_ref[e, keys, :], dots[e][:, c0:tq])
                    p = jnp.exp(s)
                    if diagonal:
                        key = lax.broadcasted_iota(jnp.int32, s.shape, 0)
                        p = jnp.where(key <= lax.broadcasted_iota(jnp.int32, s.shape, 1), p, 0.0)
                    dsb = (p * dpd).astype(BF16)
                    dv_acc[e, :, keys] += _dot_nt(dots[e][:, c0:tq], p.astype(BF16))
                    dk_acc[e, :, keys] += _dot_nt(qts[e][:, c0:tq], dsb)
                    dq_scr[e, :, c0:tq] += _dot(kt_ref[e, :, keys], dsb)

            keys0 = block(0) if pairs is None else pl.ds(FIRST_KEY, tb - FIRST_KEY)
            for e in range(2):
                dq_scr[e, :, 0:tq] = jnp.zeros((LANES, tq), F32)
                s_buf[0, e, 0:keys0.size, 0:tq] = _dot(k_ref[e, keys0, :], qts[e])
            if pairs is None:
                step(0, 0, None, 0, True)
            else:
                step(0, 0, 1, keys=keys0)

                def two_steps(t, _):
                    step(1 + 2 * t, 1, 0)
                    step(2 + 2 * t, 0, 1)
                    return 0

                lax.fori_loop(0, pairs, two_steps, 0)
                for b in range(tq // tb):
                    step(first + b, (b + 1) % 2, b % 2 if (b + 1) * tb < tq else None, b * tb, True)
            dq0, dq1 = dq_scr[0, :, 0:tq], dq_scr[1, :, 0:tq]
            dqkv_ref[0, pl.ds(q0, tq), :] = (_pair_lanes(dq0.T, dq1.T) * 0.125).astype(BF16)
            dc_ref[pl.ds(q0, tq), :] = lanes01(dq0[K_ONES:K_ONES + 1, :], dq1[K_ONES:K_ONES + 1, :])

        q_tile(0, tb, None)

        def big_tile(i, _):
            q_tile(pl.multiple_of(tb + i * tq_big, tb), tq_big, (ATT_Q_BLOCKS_BWD // 2) * i)
            return 0

        lax.fori_loop(0, n_big, big_tile, 0)

        def finish(bi, _):
            r0 = pl.multiple_of(bi * tb, tb)
            dk0, dk1 = dk_acc[0, :, pl.ds(r0, tb)], dk_acc[1, :, pl.ds(r0, tb)]
            dqkv_ref[1, pl.ds(r0, tb), :] = _pair_lanes(dk0.T, dk1.T).astype(BF16)
            dqkv_ref[2, pl.ds(r0, tb), :] = _pair_lanes(dv_acc[0, :, pl.ds(r0, tb)].T,
                                                        dv_acc[1, :, pl.ds(r0, tb)].T).astype(BF16)
            dc_ref[pl.ds(r0, tb), :] = dc_ref[pl.ds(r0, tb), :] - lanes01(dk0[Q_ONES:Q_ONES + 1, :], dk1[Q_ONES:Q_ONES + 1, :])
            return 0

        lax.fori_loop(0, nb, finish, 0)
        finish_exchange()

    once = pl.Buffered(1)
    pair = pl.BlockSpec((lp, LANES), lambda hp: (0, hp))
    pair_in = pl.BlockSpec((lp, LANES), lambda hp: (0, hp), pipeline_mode=once)
    heads = pl.BlockSpec((2, lp, LANES), lambda hp: (hp, 0, 0), pipeline_mode=once)
    heads_t = pl.BlockSpec((2, LANES, lp), lambda hp: (hp, 0, 0), pipeline_mode=once)
    hbm = pl.BlockSpec(memory_space=pl.ANY)
    return pl.pallas_call(
        body, name="attention_backward", grid=(N_HEADS // 2,),
        out_shape=[jax.ShapeDtypeStruct((3, lp, ATTN_WIDTH), BF16), jax.ShapeDtypeStruct((lp, ATTN_WIDTH), F32)]
        + _exchange_results(behind),
        in_specs=[heads_t, heads, heads_t, heads, pair_in, pair_in, pair_in] + [hbm] * nx,
        out_specs=[pl.BlockSpec((3, lp, LANES), lambda hp: (0, 0, hp)), pair] + [hbm] * nx,
        scratch_shapes=[pltpu.VMEM((2, LANES, lp), BF16), pltpu.VMEM((2, LANES, lp), BF16),
                        pltpu.VMEM((2, LANES, lp), F32), pltpu.VMEM((2, LANES, lp), F32),
                        pltpu.VMEM((2, LANES, tq_big), F32), pltpu.VMEM((2, 2, tb, tq_big), F32)]
        + _exchange_semaphores(nx),
        compiler_params=_params(("arbitrary",)),
    )(qt, k, kt, v, do, o, lse, *[a for _, a, _ in behind])


def _middle(x, target, h, o, u, zp, w_main, w_up_pool, w_up_attn, w_out, pool_w, pool_scale, final_g):
    seq = x.shape[0]
    tm = ROW_TILE
    nt = seq // tm + 1
    lp = nt * tm
    halo_blocks = tm // MAX_WINDOW

    def body(x_ref, t_ref, h_ref, o_ref, u_ref, uh_ref, zp_ref,
             wc_ref, wupp_ref, wupa_ref, wout_ref, pw_ref, sc_ref, gf_ref,
             dh2_ref, mg_ref, yp_ref, ya_ref, dap_ref, daa_ref, do_ref, dmid_ref, dpn_ref,
             loss_ref, dgf_ref, dsc_ref, dpw_ref):
        i = pl.program_id(0)
        tiles = (dh2_ref, mg_ref, yp_ref, ya_ref, dap_ref, daa_ref, do_ref, dmid_ref, dpn_ref)

        @pl.when(i == 0)
        def _():
            for ref in tiles + (loss_ref, dgf_ref, dsc_ref, dpw_ref):
                ref[...] = jnp.zeros_like(ref)

        @pl.when(i > 0)
        def _():
            xt = x_ref[...]
            hb = h_ref[...]
            pc = _dot_nt(hb, wc_ref[...])
            za, gp, ga = pc[:, :512], pc[:, 512:1536], pc[:, 1536:]
            of = o_ref[...].astype(F32)
            sza = _sigmoid(za)
            silu_za = za * sza
            ya = (of * silu_za).astype(BF16)
            ya_ref[...] = ya
            aa = _dot(ya, wupa_ref[...])

            u = u_ref[...]
            zp = zp_ref[...]
            counts = _pool_counts(i * tm, tm)
            ps = _pool_means(jnp.concatenate([uh_ref[...], u], axis=0), u, counts)
            pbs = [p.astype(BF16) for p in ps]
            ppw = jnp.concatenate([_dot(pbs[g], pw_ref[g]) for g in range(4)], axis=1)
            sc = sc_ref[...]
            szp = _sigmoid(zp)
            silu_zp = zp * szp
            ypre = ppw * sc
            yp = (ypre * silu_zp).astype(BF16)
            yp_ref[...] = yp
            ap = _dot(yp, wupp_ref[...])

            sgp, sga = _sigmoid(gp), _sigmoid(ga)
            mg = (sgp * ap + sga * aa).astype(BF16)
            mg_ref[...] = mg
            h2 = xt + _dot(mg, wout_ref[...])
            r2 = lax.rsqrt(jnp.mean(h2 * h2, axis=-1, keepdims=True) + RMS_EPS)
            h2n = h2 * r2
            gf = gf_ref[...]
            diff = h2n * gf - t_ref[...]
            loss_ref[...] += 0.5 * jnp.sum(jnp.mean(diff * diff, axis=-1, keepdims=True), axis=0, keepdims=True)
            dy = diff * (1.0 / D_MODEL)
            dgf_ref[...] += jnp.sum(dy * h2n, axis=0, keepdims=True)
            dyg = dy * gf
            dh2 = r2 * (dyg - h2n * jnp.mean(dyg * h2n, axis=-1, keepdims=True))
            dh2_ref[...] = dh2
            dmg = _dot_nt(dh2.astype(BF16), wout_ref[...])
            dap = (dmg * sgp).astype(BF16)
            daa = (dmg * sga).astype(BF16)
            dap_ref[...] = dap
            daa_ref[...] = daa
            dmid_ref[:, MID_GP:MID_GA] = (dmg * ap * sgp * (1.0 - sgp)).astype(BF16)
            dmid_ref[:, MID_GA:] = (dmg * aa * sga * (1.0 - sga)).astype(BF16)
            dyp = _dot_nt(dap, wupp_ref[...])
            dya = _dot_nt(daa, wupa_ref[...])
            do_ref[...] = (dya * silu_za).astype(BF16)
            dmid_ref[:, MID_ZA:MID_GP] = (dya * of * (sza * (1.0 + za * (1.0 - sza)))).astype(BF16)

            dypre = dyp * silu_zp
            dmid_ref[:, :MID_ZA] = (dyp * ypre * (szp * (1.0 + zp * (1.0 - szp)))).astype(BF16)
            dsc_ref[...] += jnp.sum(dypre * ppw, axis=0, keepdims=True)
            dppw = (dypre * sc).astype(BF16)
            dpns = []
            for g in range(4):
                dg = dppw[:, POOL_GROUP * g:POOL_GROUP * (g + 1)]
                dpw_ref[g] += _dot_tn(pbs[g], dg)
                dpns.append(_dot_nt(dg, pw_ref[g]) / counts[g])
            dpn_ref[...] = jnp.concatenate(dpns, axis=1)

    real = lambda w: pl.BlockSpec((tm, w), lambda i: (jnp.maximum(i - 1, 0), 0))
    row = lambda w: pl.BlockSpec((tm, w), lambda i: (i, 0))
    in_specs = [
        real(D_MODEL), real(D_MODEL), row(D_MODEL), row(512), row(512),
        pl.BlockSpec((MAX_WINDOW, 512), lambda i: (jnp.maximum(i * halo_blocks - 1, 0), 0)), row(512),
        _const((2560, D_MODEL), (1, 0)), _const((POOL_WIDTH, D_MODEL)), _const((ATTN_WIDTH, D_MODEL)),
        _const((D_MODEL, D_MODEL)), _const((4, POOL_GROUP, POOL_GROUP)), _const((1, POOL_WIDTH)), _const((1, D_MODEL)),
    ]
    sd = jax.ShapeDtypeStruct
    out_shape = [
        sd((lp, D_MODEL), F32),
        sd((lp, D_MODEL), BF16),
        sd((lp, 512), BF16),
        sd((lp, 512), BF16),
        sd((lp, D_MODEL), BF16),
        sd((lp, D_MODEL), BF16),
        sd((lp, 512), BF16),
        sd((lp, MID_WIDTH), BF16),
        sd((lp, 512), F32),
        sd((1, LANES), F32),
        sd((1, D_MODEL), F32),
        sd((1, 512), F32),
        sd((4, POOL_GROUP, POOL_GROUP), F32),
    ]
    keep = lambda shape: pl.BlockSpec(shape, lambda i: (0,) * len(shape))
    out_specs = [row(D_MODEL), row(D_MODEL), row(512), row(512), row(D_MODEL), row(D_MODEL), row(512),
                 row(MID_WIDTH), row(512),
                 keep((1, LANES)), keep((1, D_MODEL)), keep((1, 512)), keep((4, POOL_GROUP, POOL_GROUP))]
    return pl.pallas_call(
        body, name="middle", grid=(nt,), out_shape=out_shape, in_specs=in_specs, out_specs=out_specs,
        compiler_params=_params(("arbitrary",)),
    )(x, target, h, o, u, u, zp, w_main, w_up_pool, w_up_attn, w_out, pool_w, pool_scale, final_g)


DUF_WIDTH = POOL_WIDTH + LANES


def _sequence_grads(dpn, dc, sneg):
    lp = dpn.shape[0]
    tm = ROW_TILE
    nt = lp // tm
    halo_blocks = tm // MAX_WINDOW
    last_halo = lp // MAX_WINDOW - 1

    def body(dpn_ref, dpnh_ref, dc_ref, sn_ref, duf_ref, dbf_ref, carry_ref):
        i = pl.program_id(0)
        t = nt - 1 - i

        @pl.when(i == 0)
        def _():
            carry_ref[...] = jnp.zeros_like(carry_ref)
            dbf_ref[...] = jnp.zeros_like(dbf_ref)

        dpn_t = dpn_ref[...]
        ahead = jnp.where(i == 0, jnp.zeros_like(dpnh_ref), dpnh_ref[...])
        ext = jnp.concatenate([dpn_t, ahead], axis=0)
        counts = _pool_counts(t * tm, tm)
        for g, w in enumerate(POOL_WINDOWS):
            s = ext[:, POOL_GROUP * g:POOL_GROUP * (g + 1)]
            sh = 1
            while sh < w:
                s = s + pltpu.roll(s, tm + MAX_WINDOW - sh, axis=0)
                sh *= 2
            du = s[:tm, :] - dpn_t[:, POOL_GROUP * g:POOL_GROUP * (g + 1)] * counts[g]
            duf_ref[:, POOL_GROUP * g:POOL_GROUP * (g + 1)] = du.astype(BF16)

        dct = dc_ref[:, 0:LANES]
        for hp in range(1, N_HEADS // 2):
            dct = dct + pltpu.roll(dc_ref[:, LANES * hp:LANES * (hp + 1)], 2 * hp, axis=1)
        rloc = lax.broadcasted_iota(jnp.int32, (tm, LANES), 0)
        sh = 1
        while sh < tm:
            dct = dct + jnp.where(rloc + sh < tm, pltpu.roll(dct, tm - sh, axis=0), 0.0)
            sh *= 2
        dct = dct + carry_ref[...]
        carry_ref[...] = dct[0:1, :]
        df = dct * sn_ref[...]
        dbf_ref[...] += jnp.sum(df, axis=0, keepdims=True)
        duf_ref[:, POOL_WIDTH:] = df.astype(BF16)

    rev = lambda w: pl.BlockSpec((tm, w), lambda i: (nt - 1 - i, 0))
    return pl.pallas_call(
        body, name="sequence_grads", grid=(nt,),
        out_shape=[jax.ShapeDtypeStruct((lp, DUF_WIDTH), BF16), jax.ShapeDtypeStruct((1, LANES), F32)],
        in_specs=[rev(512),
                  pl.BlockSpec((MAX_WINDOW, 512), lambda i: (jnp.minimum((nt - i) * halo_blocks, last_halo), 0)),
                  rev(512), rev(LANES)],
        out_specs=[rev(DUF_WIDTH), pl.BlockSpec((1, LANES), lambda i: (0, 0))],
        scratch_shapes=[pltpu.VMEM((1, LANES), F32)],
        compiler_params=_params(("arbitrary",)),
    )(dpn, dpn, dc, sneg)


def _backward_in(x, tile0, norm_g, dh2, duf, dqkv, dmid, w_main, w_f, behind):
    seq = x.shape[0]
    tm = ROW_TILE
    nt = seq // tm + 1
    nx = len(behind)

    def body(x_ref, t0_ref, g_ref, dh2_ref, du_ref, df_ref, dqkv_ref, dzp_ref, dza_ref, dgp_ref, dga_ref,
             wm_ref, wf_ref, *rest):
        gx_ref, gmeta_ref, dg_ref = rest[nx:nx + 3]
        dproj_ref = rest[2 * nx + 3]
        finish_exchange = _behind(behind, rest[:nx], rest[nx + 3:2 * nx + 3], rest[2 * nx + 4:])
        t = pl.program_id(0)

        @pl.when(t == 0)
        def _():
            dg_ref[...] = jnp.zeros_like(dg_ref)

        dproj_ref[:, 0:512] = du_ref[...]
        dproj_ref[:, 512:1024] = dzp_ref[...]
        dproj_ref[:, 1024:1536] = dqkv_ref[0]
        dproj_ref[:, 1536:2048] = dqkv_ref[1]
        dproj_ref[:, 2048:2560] = dqkv_ref[2]
        dproj_ref[:, 2560:3072] = dza_ref[...]
        dproj_ref[:, 3072:4096] = dgp_ref[...]
        dproj_ref[:, 4096:5120] = dga_ref[...]
        dh = _dot(dproj_ref[...], wm_ref[...]) + _dot(df_ref[...], wf_ref[...])
        xt = jnp.where(t == 0, t0_ref[...], x_ref[...])
        r = lax.rsqrt(jnp.mean(xt * xt, axis=-1, keepdims=True) + RMS_EPS)
        xn = xt * r
        dg_ref[...] += jnp.sum(dh * xn, axis=0, keepdims=True)
        dhg = dh * g_ref[...]
        dx = dh2_ref[...] + r * (dhg - xn * jnp.mean(dhg * xn, axis=-1, keepdims=True))

        @pl.when(t > 0)
        def _():
            gx_ref[...] = dx

        @pl.when(t == 0)
        def _():
            gmeta_ref[...] = dx[PAD:, :]
            gx_ref[...] = jnp.zeros_like(gx_ref)

        finish_exchange()

    row = lambda w, j=0: pl.BlockSpec((tm, w), lambda i: (i, j))
    real = pl.BlockSpec((tm, D_MODEL), lambda i: (jnp.maximum(i - 1, 0), 0))
    hbm = pl.BlockSpec(memory_space=pl.ANY)
    in_specs = [
        real, _const((tm, D_MODEL)), _const((1, D_MODEL)), row(D_MODEL),
        row(POOL_WIDTH), row(LANES, POOL_WIDTH // LANES), pl.BlockSpec((3, tm, ATTN_WIDTH), lambda i: (0, i, 0)),
        row(512, 0), row(512, 1), row(1024, 1), row(1024, 2),
        _const((N_MAIN, D_MODEL)), _const((LANES, D_MODEL)),
    ] + [hbm] * nx
    sd = jax.ShapeDtypeStruct
    out_shape = [sd((seq, D_MODEL), F32), sd((N_META, D_MODEL), F32), sd((1, D_MODEL), F32)] + _exchange_results(behind)
    keep = lambda shape: pl.BlockSpec(shape, lambda i: (0,) * len(shape))
    out_specs = [real, keep((N_META, D_MODEL)), keep((1, D_MODEL))] + [hbm] * nx
    return pl.pallas_call(
        body, name="backward_in", grid=(nt,), out_shape=out_shape, in_specs=in_specs, out_specs=out_specs,
        scratch_shapes=[pltpu.VMEM((tm, N_MAIN), BF16)] + _exchange_semaphores(nx),
        compiler_params=_params(("arbitrary",)),
    )(x, tile0, norm_g, dh2, duf, duf, dqkv, dmid, dmid, dmid, dmid, w_main, w_f, *[a for _, a, _ in behind])


def _matmul_tn(name, a, b, tn):
    lp, m = a.shape
    n = b.shape[1]

    def body(a_ref, b_ref, c_ref):
        c_ref[...] = _dot_tn(a_ref[...].astype(BF16), b_ref[...].astype(BF16))

    return pl.pallas_call(
        body, name=name, grid=(n // tn,), out_shape=jax.ShapeDtypeStruct((m, n), F32),
        in_specs=[_const((lp, m)), pl.BlockSpec((lp, tn), lambda j: (0, j))],
        out_specs=pl.BlockSpec((m, tn), lambda j: (0, j)),
        compiler_params=_params(("arbitrary",)),
    )(a, b)


def _matmul_tn_rows(name, a, b, tm):
    lp, m = a.shape
    n = b.shape[1]

    def body(a_ref, b_ref, c_ref):
        c_ref[...] = _dot_tn(a_ref[...].astype(BF16), b_ref[...].astype(BF16))

    return pl.pallas_call(
        body, name=name, grid=(m // tm,), out_shape=jax.ShapeDtypeStruct((m, n), F32),
        in_specs=[pl.BlockSpec((lp, tm), lambda j: (0, j)), _const((lp, n))],
        out_specs=pl.BlockSpec((tm, n), lambda j: (j, 0)),
        compiler_params=_params(("arbitrary",)),
    )(a, b)


def _matmul_tn_stack(name, a, b):
    n_blocks, lp, m = a.shape
    n = b.shape[1]

    def body(a_ref, b_ref, c_ref):
        c_ref[...] = _dot_tn(a_ref[...], b_ref[...])

    return pl.pallas_call(
        body, name=name, grid=(n_blocks,), out_shape=jax.ShapeDtypeStruct((n_blocks * m, n), F32),
        in_specs=[pl.BlockSpec((None, lp, m), lambda j: (j, 0, 0)), _const((lp, n))],
        out_specs=pl.BlockSpec((m, n), lambda j: (j, 0)),
        compiler_params=_params(("arbitrary",)),
    )(a, b)


def _adamw_step(p_ref, w_ref, m_ref, v_ref, g_ref, d_ref, mo_ref, vo_ref):
    g = p_ref[0].astype(F32)
    for s in range(1, p_ref.shape[0]):
        g = g + p_ref[s].astype(F32)
    m_new = ADAM_B1 * m_ref[...] + (1.0 - ADAM_B1) * g
    v_new = ADAM_B2 * v_ref[...] + (1.0 - ADAM_B2) * (g * g)
    m_hat = m_new / (1.0 - ADAM_B1 ** ADAM_STEP)
    v_hat = v_new / (1.0 - ADAM_B2 ** ADAM_STEP)
    g_ref[...] = g
    d_ref[...] = -ADAM_LR * (m_hat / (jnp.sqrt(v_hat) + ADAM_EPS) + ADAM_WD * w_ref[...])
    mo_ref[...] = m_new
    vo_ref[...] = v_new


def _adamw_small(name, groups, loss_parts):
    n = len(groups)

    def body(*refs):
        ins, outs = refs[:4 * n + 1], refs[4 * n + 1:]
        for j in range(n):
            _adamw_step(*ins[4 * j:4 * j + 4], *outs[4 * j:4 * j + 4])
        total = ins[-1][0]
        for s in range(1, N_DEV):
            total = total + ins[-1][s]
        outs[-1][...] = total

    vmem = pl.BlockSpec(memory_space=pltpu.VMEM)
    out_shape = [jax.ShapeDtypeStruct(w.shape, F32) for _, w, _, _ in groups for _ in range(4)]
    out_shape.append(jax.ShapeDtypeStruct(loss_parts.shape[1:], F32))
    res = pl.pallas_call(
        body, name=name, out_shape=out_shape, in_specs=[vmem] * (4 * n + 1), out_specs=[vmem] * (4 * n + 1),
        compiler_params=_params(),
    )(*[a for g in groups for a in g], loss_parts)
    return [res[4 * j:4 * j + 4] for j in range(n)], res[-1]


def _adamw(name, parts, w, m, v, rows, cols=None):
    r, c_all = w.shape
    c = cols or c_all
    n_parts = parts.shape[0]

    def body(p_ref, w_ref, m_ref, v_ref, g_ref, d_ref, mo_ref, vo_ref):
        _adamw_step(p_ref, w_ref, m_ref, v_ref, g_ref, d_ref, mo_ref, vo_ref)

    blk = pl.BlockSpec((rows, c), lambda i, j: (i, j))
    return pl.pallas_call(
        body, name=name, grid=(r // rows, c_all // c), out_shape=[jax.ShapeDtypeStruct((r, c_all), F32)] * 4,
        in_specs=[pl.BlockSpec((n_parts, rows, c), lambda i, j: (0, i, j)), blk, blk, blk],
        out_specs=[blk] * 4,
        compiler_params=_params(("arbitrary", "arbitrary")),
    )(parts, w, m, v)


def _pair_sum(name, mine, theirs, rows):
    n, r, c = mine.shape

    def body(a_ref, b_ref, o_ref):
        o_ref[...] = (a_ref[...].astype(F32) + b_ref[...].astype(F32)).astype(BF16)

    blk = pl.BlockSpec((1, rows, c), lambda j, i: (j, i, 0))
    return pl.pallas_call(
        body, name=name, grid=(n, r // rows), out_shape=jax.ShapeDtypeStruct((n, r, c), BF16),
        in_specs=[blk, blk], out_specs=blk,
        compiler_params=_params(("arbitrary", "arbitrary")),
    )(mine, theirs)


def _by_core(slots):
    by_core = slots.reshape((4, 2) + slots.shape[1:]).swapaxes(0, 1)
    c = lax.axis_index("c")
    return (lax.dynamic_index_in_dim(by_core, c, 0, keepdims=False),
            lax.dynamic_index_in_dim(by_core, 1 - c, 0, keepdims=False))


def _columns_to_slots(a):
    r, c8 = a.shape
    return a.reshape(r, N_DEV, c8 // N_DEV).transpose(1, 0, 2)


def _slots_to_columns(a):
    n, r, c = a.shape
    return a.transpose(1, 0, 2).reshape(r, n * c)


def kernel(x, meta_tokens, norm_g, w_in, b_forget, pool_w, pool_scale, w_up_pool, w_up_attn, w_out, final_norm_g, loss_target, m_meta_tokens, m_norm_g, m_w_in, m_b_forget, m_pool_w, m_pool_scale, m_w_up_pool, m_w_up_attn, m_w_out, m_final_norm_g, v_meta_tokens, v_norm_g, v_w_in, v_b_forget, v_pool_w, v_pool_scale, v_w_up_pool, v_w_up_attn, v_w_out, v_final_norm_g):
    xs = x[0]
    target = loss_target[0]

    g_in, g_meta = _gather_two_level("gather_weights", [w_in[0].T.astype(BF16), meta_tokens], (320, 8))
    w_full = g_in.reshape(N_DEV * g_in.shape[1], D_MODEL)
    w_main = jnp.concatenate([w_full[:N_BEFORE_F], w_full[N_BEFORE_F + N_HEADS:]], axis=0)
    w_f = jnp.pad(w_full[N_BEFORE_F:N_BEFORE_F + N_HEADS], ((0, LANES - N_HEADS), (0, 0)))
    meta = _slots_to_columns(g_meta)
    tile0 = jnp.concatenate([jnp.zeros((PAD, D_MODEL), F32), meta], axis=0)
    b_f = jnp.pad(b_forget, ((0, 0), (0, LANES - N_HEADS)))
    pw_b = pool_w[0].astype(BF16)
    final_g = final_norm_g.reshape(1, D_MODEL)

    h, u, zp, k, v, qt, kt, vt, sneg = _forward_in(xs, tile0, norm_g, w_main, w_f, b_f)
    o, lse, g_upp, g_upa, g_out = _attention_forward(
        qt, k, vt, [("gather", w.astype(BF16), ALL_PEERS) for w in (w_up_pool[0], w_up_attn[0], w_out[0])])
    wupp = _slots_to_columns(g_upp)
    wupa = _slots_to_columns(g_upa)
    wout = g_out.reshape(D_MODEL, D_MODEL)
    (dh2, mg, yp, ya, dap, daa, do, dmid, dpn,
     loss_part, d_final_g, d_scale, d_pool_w) = _middle(xs, target, h, o, u, zp, w_main, wupp, wupa, wout,
                                                        pw_b, pool_scale, final_g)
    dw_out = _matmul_tn("grad_w_out", mg, dh2, 256)
    dw_upp = _matmul_tn("grad_w_up_pool", yp, dap, 512)
    dw_upa = _matmul_tn("grad_w_up_attn", ya, daa, 512)
    dqkv, dc, p_upp, p_upa, p_out, p_pool_w, p_scale, p_final_g = _attention_backward(
        qt, k, kt, v, do, o, lse,
        [("scatter", _columns_to_slots(dw_upp).astype(BF16), ALL_PEERS),
         ("scatter", _columns_to_slots(dw_upa).astype(BF16), ALL_PEERS),
         ("scatter", dw_out.reshape(N_DEV, D_MODEL // N_DEV, D_MODEL).astype(BF16), ALL_PEERS),
         ("gather", d_pool_w.reshape(4 * POOL_GROUP, POOL_GROUP), ALL_PEERS),
         ("gather", d_scale, ALL_PEERS), ("gather", d_final_g, ALL_PEERS)])
    duf, d_bf = _sequence_grads(dpn, dc, sneg)
    g_uf = _matmul_tn_rows("grad_w_in_pool_forget", duf, h, DUF_WIDTH)
    g_qkv = _matmul_tn_stack("grad_w_in_attention", dqkv, h)
    g_mid = _matmul_tn_rows("grad_w_in_gates", dmid, h, 512)
    dw_in = jnp.concatenate([g_uf[:POOL_WIDTH], g_mid[:MID_ZA], g_qkv, g_mid[MID_ZA:MID_GP],
                             g_uf[POOL_WIDTH:POOL_WIDTH + N_HEADS], g_mid[MID_GP:]], axis=0)
    dw_in = dw_in.reshape(N_DEV, dw_in.shape[0] // N_DEV, D_MODEL)
    mine, for_sibling = _by_core(dw_in)
    from_sibling, = _exchange("swap_with_sibling", [("swap", for_sibling.astype(BF16), (SIBLING,))])
    pair_sums = _pair_sum("pair_sum", mine, from_sibling, dw_in.shape[1])
    grad_x, d_meta, d_norm_g, p_in, p_bf, p_loss = _backward_in(
        xs, tile0, norm_g, dh2, duf, dqkv, dmid, w_main, w_f,
        [("chips", pair_sums, SAME_CORE), ("gather", d_bf, ALL_PEERS), ("gather", loss_part, ALL_PEERS)])
    p_meta, p_norm_g = _exchange(
        "exchange_gradients", [("scatter", _columns_to_slots(d_meta), ALL_PEERS), ("gather", d_norm_g, ALL_PEERS)])


    def pad_f(a):
        return jnp.pad(a, ((0, 0), (0, LANES - N_HEADS)))

    res = {}
    res["w_in"] = [a.T for a in _adamw("adamw_w_in", p_in, w_in[0].T, m_w_in[0].T, v_w_in[0].T, p_in.shape[1], 256)]
    res["w_up_pool"] = _adamw("adamw_w_up_pool", p_upp, w_up_pool[0], m_w_up_pool[0], v_w_up_pool[0], 512)
    res["w_up_attn"] = _adamw("adamw_w_up_attn", p_upa, w_up_attn[0], m_w_up_attn[0], v_w_up_attn[0], 512)
    res["w_out"] = _adamw("adamw_w_out", p_out, w_out[0], m_w_out[0], v_w_out[0], 128)
    flat = lambda a: a.reshape(4 * POOL_GROUP, POOL_GROUP)
    row = lambda a: a.reshape(1, D_MODEL)
    small, loss_row = _adamw_small(
        "adamw_small",
        [(p_meta, meta_tokens, m_meta_tokens, v_meta_tokens),
         (p_norm_g, norm_g, m_norm_g, v_norm_g),
         (p_bf, pad_f(b_forget), pad_f(m_b_forget), pad_f(v_b_forget)),
         (p_pool_w, flat(pool_w), flat(m_pool_w), flat(v_pool_w)),
         (p_scale, pool_scale, m_pool_scale, v_pool_scale),
         (p_final_g, final_g, row(m_final_norm_g), row(v_final_norm_g))],
        p_loss)
    res["meta_tokens"], res["norm_g"], bf, pw, res["pool_scale"], fg = small
    res["b_forget"] = [a[:, :N_HEADS] for a in bf]
    res["pool_w"] = [a.reshape(pool_w.shape) for a in pw]
    res["final_norm_g"] = [a.reshape(D_MODEL) for a in fg]
    loss = loss_row[0, 0]
    for name in ("w_in", "w_up_pool", "w_up_attn", "w_out"):
        res[name] = [a[None] for a in res[name]]

    order = ["meta_tokens", "norm_g", "w_in", "b_forget", "pool_w", "pool_scale", "w_up_pool", "w_up_attn", "w_out",
             "final_norm_g"]
    outs = [loss, grad_x[None]]
    for part in range(4):
        outs += [res[name][part] for name in order]
    return tuple(outs)
```

```python
import jax
import jax.numpy as jnp
from jax import lax
from jax.experimental import pallas as pl
from jax.experimental.pallas import tpu as pltpu

F32 = jnp.float32
BF16 = jnp.bfloat16

D_MODEL = 1024
N_META = 16
POOL_WIDTH = 512
ATTN_WIDTH = 512
N_HEADS = 8
HEAD_DIM = 64
POOL_WINDOWS = (2, 4, 8, 16)
POOL_GROUP = 128
MAX_WINDOW = 16
RMS_EPS = 1e-6
N_MAIN = 5120
N_BEFORE_F = 3072
N_DEV = 8
LANES = 128

ROW_TILE = 256
ATT_TILE = 256
ATT_Q_BLOCKS_FWD = 8
ATT_Q_BLOCKS_BWD = 4
PAD = ROW_TILE - N_META
FIRST_KEY = PAD // LANES * LANES
VMEM_LIMIT = 56 * 1024 * 1024

ADAM_LR = 0.001
ADAM_B1 = 0.9
ADAM_B2 = 0.999
ADAM_EPS = 1e-08
ADAM_WD = 0.01
ADAM_STEP = 10

MID_ZA, MID_GP, MID_GA, MID_WIDTH = 512, 1024, 2048, 3072
NEG = -1e30
MESH = pl.DeviceIdType.MESH


def _params(sem=None):
    kw = dict(vmem_limit_bytes=VMEM_LIMIT)
    if sem is not None:
        kw["dimension_semantics"] = sem
    return pltpu.CompilerParams(**kw)


def _const(shape, block_index=None):
    idx = block_index or (0,) * len(shape)
    return pl.BlockSpec(shape, lambda i: idx, pipeline_mode=pl.Buffered(1))


def _sigmoid(x):
    return jax.nn.sigmoid(x)


def _dot(a, b):
    return jnp.dot(a, b, preferred_element_type=F32)


def _dot_nt(a, b):
    return lax.dot_general(a, b, (((1,), (1,)), ((), ())), preferred_element_type=F32)


def _dot_tn(a, b):
    return lax.dot_general(a, b, (((0,), (0,)), ((), ())), preferred_element_type=F32)


def _pool_counts(first_row, rows):
    row = first_row + lax.broadcasted_iota(jnp.int32, (rows, 1), 0)
    pos1 = row - PAD + 1
    return [jnp.clip(pos1, 1, w).astype(F32) for w in POOL_WINDOWS]


def _pool_means(u_ext, u, counts):
    rows = u.shape[0]
    out = []
    for g, w in enumerate(POOL_WINDOWS):
        s = u_ext[:, POOL_GROUP * g:POOL_GROUP * (g + 1)]
        sh = 1
        while sh < w:
            s = s + pltpu.roll(s, sh, axis=0)
            sh *= 2
        out.append(s[MAX_WINDOW:MAX_WINDOW + rows, :] / counts[g] - u[:, POOL_GROUP * g:POOL_GROUP * (g + 1)])
    return out


Q_BIAS, Q_ONES, Q_LSE = 64, 67, 70
K_ONES, K_BIAS, K_ONES2 = 64, 67, 70
V_ONES = 64
DO_BIAS = 64


def _lane_ones(lane, ranges):
    hit = None
    for lo, hi in ranges:
        r = (lane >= lo) & (lane < hi)
        hit = r if hit is None else hit | r
    return jnp.where(hit, 1.0, 0.0)


def _put3(base, lane, first, x):
    hi = x.astype(BF16).astype(F32)
    rest = x - hi
    mid = rest.astype(BF16).astype(F32)
    lo = (rest - mid).astype(BF16).astype(F32)
    for j, piece in enumerate((hi, mid, lo)):
        base = jnp.where(lane == first + j, piece, base)
    return base


SIBLING = 1
SAME_CORE = (2, 4, 6)
ALL_PEERS = (1, 2, 3, 4, 5, 6, 7)


def _place():
    return lax.axis_index("x"), lax.axis_index("y"), lax.axis_index("c")


def _peer(r):
    x, y, c = _place()
    return (1 - x if r & 4 else x, 1 - y if r & 2 else y, 1 - c if r & 1 else c)


def _device_slot(p):
    return 4 * p[0] + 2 * p[1] + p[2]


def _chip_slot(p):
    return 2 * p[0] + p[1]


def _exchange(name, items):
    n = len(items)

    def body(*refs):
        copies = _exchange_copies(items, refs[:n], refs[n:2 * n], *refs[2 * n:])
        for cp in copies:
            cp.start()
        for cp in copies:
            cp.wait()

    hbm = pl.BlockSpec(memory_space=pl.ANY)
    return pl.pallas_call(
        body, name=name, out_shape=_exchange_results(items),
        in_specs=[hbm] * n, out_specs=[hbm] * n,
        scratch_shapes=_exchange_semaphores(n),
    )(*[a for _, a, _ in items])


def _exchange_results(items):
    return [jax.ShapeDtypeStruct(((N_DEV,) if kind == "gather" else ()) + a.shape, a.dtype) for kind, a, _ in items]


def _exchange_semaphores(n):
    return [pltpu.SemaphoreType.DMA((n, N_DEV - 1)), pltpu.SemaphoreType.DMA((n, N_DEV - 1)),
            pltpu.SemaphoreType.DMA((n,))]


def _exchange_copies(items, ins, outs, send_sems, recv_sems, local_sems):
    me = _place()
    copies = []
    for a, (kind, _, peers) in enumerate(items):
        slot = _chip_slot if kind == "chips" else _device_slot
        for r in peers:
            peer = _peer(r)
            src = ins[a] if kind in ("swap", "gather") else ins[a].at[slot(peer)]
            dst = outs[a] if kind == "swap" else outs[a].at[slot(me)]
            copies.append(pltpu.make_async_remote_copy(
                src_ref=src, dst_ref=dst, send_sem=send_sems.at[a, r - 1], recv_sem=recv_sems.at[a, r - 1],
                device_id=peer, device_id_type=MESH))
        if kind != "swap":
            src = ins[a] if kind == "gather" else ins[a].at[slot(me)]
            copies.append(pltpu.make_async_copy(src, outs[a].at[slot(me)], local_sems.at[a]))
    return copies


def _gather_two_level(name, arrays, halves):
    n = len(arrays)
    x_flip, y_flip, both = 4, 2, 6
    to_sibling, to_x, to_y, on_over_y, on_over_x, x_to_sibling, y_to_sibling, d_to_sibling = range(8)

    def body(*refs):
        ins, outs = refs[:n], refs[n:2 * n]
        send_sems, recv_sems, local_sems = refs[2 * n:]
        me, sibling = _place(), _peer(SIBLING)
        xn, yn, dg = _peer(x_flip), _peer(y_flip), _peer(both)

        def rows(a, h):
            return pl.ds(0, halves[a]) if h == 0 else pl.ds(halves[a], arrays[a].shape[0] - halves[a])

        def copy(a, h, k, block, to, own=False):
            dst = outs[a].at[_device_slot(block)].at[rows(a, h)]
            return pltpu.make_async_remote_copy(
                src_ref=ins[a].at[rows(a, h)] if own else dst, dst_ref=dst,
                send_sem=send_sems.at[2 * a + h, k], recv_sem=recv_sems.at[2 * a + h, k],
                device_id=to, device_id_type=MESH)

        sends, mine = [], []

        def start(cp):
            cp.start()
            sends.append(cp)

        for a in range(n):
            cp = pltpu.make_async_copy(ins[a], outs[a].at[_device_slot(me)], local_sems.at[a])
            cp.start()
            mine.append(cp)
        for a in range(n):
            start(copy(a, 0, to_x, me, xn, own=True))
            start(copy(a, 1, to_y, me, yn, own=True))
        for a in range(n):
            start(copy(a, 0, to_y, me, yn, own=True))
            start(copy(a, 1, to_x, me, xn, own=True))
        for a in range(n):
            for h in range(2):
                start(copy(a, h, to_sibling, me, sibling, own=True))
        for a in range(n):
            copy(a, 0, to_x, xn, me).wait_recv()
            start(copy(a, 0, on_over_y, xn, yn))
            start(copy(a, 0, x_to_sibling, xn, sibling))
            copy(a, 1, to_y, yn, me).wait_recv()
            start(copy(a, 1, on_over_x, yn, xn))
            start(copy(a, 1, y_to_sibling, yn, sibling))
        for a in range(n):
            copy(a, 0, to_y, yn, me).wait_recv()
            start(copy(a, 0, y_to_sibling, yn, sibling))
            copy(a, 1, to_x, xn, me).wait_recv()
            start(copy(a, 1, x_to_sibling, xn, sibling))
        for a in range(n):
            copy(a, 0, on_over_y, dg, me).wait_recv()
            start(copy(a, 0, d_to_sibling, dg, sibling))
            copy(a, 1, on_over_x, dg, me).wait_recv()
            start(copy(a, 1, d_to_sibling, dg, sibling))
        for a in range(n):
            for h in range(2):
                copy(a, h, to_sibling, sibling, me).wait_recv()
                for k, r in ((x_to_sibling, x_flip), (y_to_sibling, y_flip), (d_to_sibling, both)):
                    copy(a, h, k, _peer(r | SIBLING), me).wait_recv()
        for cp in sends:
            cp.wait_send()
        for cp in mine:
            cp.wait()

    hbm = pl.BlockSpec(memory_space=pl.ANY)
    return pl.pallas_call(
        body, name=name, out_shape=[jax.ShapeDtypeStruct((N_DEV,) + a.shape, a.dtype) for a in arrays],
        in_specs=[hbm] * n, out_specs=[hbm] * n,
        scratch_shapes=[pltpu.SemaphoreType.DMA((2 * n, 8)), pltpu.SemaphoreType.DMA((2 * n, 8)),
                        pltpu.SemaphoreType.DMA((n,))],
    )(*arrays)


def _forward_in(x, tile0, norm_g, w_main, w_f, b_f):
    seq = x.shape[0]
    nt = seq // ROW_TILE + 1
    lp = nt * ROW_TILE
    tm = ROW_TILE

    def body(x_ref, t0_ref, g_ref, wa_ref, wf_ref, bf_ref,
             h_ref, u_ref, zp_ref, k_ref, v_ref, qt_ref, kt_ref, vt_ref, sn_ref, carry_ref):
        i = pl.program_id(0)

        @pl.when(i == 0)
        def _():
            carry_ref[...] = jnp.zeros_like(carry_ref)

        xt = jnp.where(i == 0, t0_ref[...], x_ref[...])
        r = lax.rsqrt(jnp.mean(xt * xt, axis=-1, keepdims=True) + RMS_EPS)
        h = (xt * r * g_ref[...]).astype(BF16)
        h_ref[...] = h
        pa = _dot_nt(h, wa_ref[...])
        u_ref[...] = pa[:, :512]
        zp_ref[...] = pa[:, 512:1024]

        fl = _dot_nt(h, wf_ref[...]) + bf_ref[...]
        row = i * tm + lax.broadcasted_iota(jnp.int32, (tm, LANES), 0)
        rloc = lax.broadcasted_iota(jnp.int32, (tm, LANES), 0)
        lane = lax.broadcasted_iota(jnp.int32, (tm, LANES), 1)
        live = (row >= PAD) & (lane < N_HEADS)
        logf = jnp.minimum(fl, 0.0) - jnp.log1p(jnp.exp(-jnp.abs(fl)))
        cs = jnp.where(live, logf, 0.0)
        sh = 1
        while sh < tm:
            cs = cs + jnp.where(rloc >= sh, pltpu.roll(cs, sh, axis=0), 0.0)
            sh *= 2
        cs = cs + carry_ref[...]
        carry_ref[...] = cs[tm - 1:tm, :]
        sn_ref[...] = jnp.where(live, _sigmoid(-fl), 0.0)

        rows1 = i * tm + lax.broadcasted_iota(jnp.int32, (tm, 1), 0)
        cols1 = i * tm + lax.broadcasted_iota(jnp.int32, (1, tm), 1)
        sub = lax.broadcasted_iota(jnp.int32, (LANES, tm), 0)
        ones_k = _lane_ones(lane, ((K_ONES, K_ONES + 3), (K_ONES2, K_ONES2 + 3)))
        ones_v = _lane_ones(lane, ((V_ONES, V_ONES + 3),))
        ones_q_t = _lane_ones(sub, ((Q_ONES, Q_ONES + 3),))
        ones_k_t = _lane_ones(sub, ((K_ONES, K_ONES + 3), (K_ONES2, K_ONES2 + 3)))
        ones_v_t = _lane_ones(sub, ((V_ONES, V_ONES + 3),))
        cs_t = cs.T
        for hp in range(N_HEADS // 2):
            kp = pa[:, 1536 + LANES * hp:1536 + LANES * (hp + 1)]
            vp = pa[:, 2048 + LANES * hp:2048 + LANES * (hp + 1)]
            qp_t = (pa[:, 1024 + LANES * hp:1024 + LANES * (hp + 1)] * 0.125).T
            kp_t, vp_t = kp.T, vp.T
            for e in range(2):
                head = 2 * hp + e
                if e:
                    kp, vp = pltpu.roll(kp, HEAD_DIM, axis=1), pltpu.roll(vp, HEAD_DIM, axis=1)
                    qp_t, kp_t, vp_t = (pltpu.roll(a, HEAD_DIM, axis=0) for a in (qp_t, kp_t, vp_t))
                c_row = cs_t[head:head + 1, :]
                minus_ck_row = jnp.where(cols1 >= PAD, -c_row, NEG)
                qt_ref[head] = jnp.where(sub < HEAD_DIM, qp_t, _rows3(Q_BIAS, c_row) + ones_q_t).astype(BF16)
                kt_ref[head] = jnp.where(sub < HEAD_DIM, kp_t, _rows3(K_BIAS, minus_ck_row) + ones_k_t).astype(BF16)
                vt_ref[head] = jnp.where(sub < HEAD_DIM, vp_t, ones_v_t).astype(BF16)
                minus_ck = jnp.where(rows1 >= PAD, -cs[:, head:head + 1], NEG)
                k_ref[head] = jnp.where(lane < HEAD_DIM, kp, _put3(ones_k, lane, K_BIAS, minus_ck)).astype(BF16)
                v_ref[head] = jnp.where(lane < HEAD_DIM, vp, ones_v).astype(BF16)

    row_f32 = lambda w: pl.BlockSpec((tm, w), lambda i: (i, 0))
    out_shape = [
        jax.ShapeDtypeStruct((lp, D_MODEL), BF16),
        jax.ShapeDtypeStruct((lp, POOL_WIDTH), F32),
        jax.ShapeDtypeStruct((lp, POOL_WIDTH), F32),
        jax.ShapeDtypeStruct((N_HEADS, lp, LANES), BF16),
        jax.ShapeDtypeStruct((N_HEADS, lp, LANES), BF16),
        jax.ShapeDtypeStruct((N_HEADS, LANES, lp), BF16),
        jax.ShapeDtypeStruct((N_HEADS, LANES, lp), BF16),
        jax.ShapeDtypeStruct((N_HEADS, LANES, lp), BF16),
        jax.ShapeDtypeStruct((lp, LANES), F32),
    ]
    heads = pl.BlockSpec((N_HEADS, tm, LANES), lambda i: (0, i, 0))
    heads_t = pl.BlockSpec((N_HEADS, LANES, tm), lambda i: (0, 0, i))
    out_specs = [row_f32(D_MODEL), row_f32(512), row_f32(512), heads, heads, heads_t, heads_t, heads_t,
                 row_f32(LANES)]
    in_specs = [
        pl.BlockSpec((tm, D_MODEL), lambda i: (jnp.maximum(i - 1, 0), 0)),
        _const((tm, D_MODEL)), _const((1, D_MODEL)),
        _const((2560, D_MODEL)), _const((LANES, D_MODEL)), _const((1, LANES)),
    ]
    return pl.pallas_call(
        body, name="forward_in", grid=(nt,), out_shape=out_shape, in_specs=in_specs, out_specs=out_specs,
        scratch_shapes=[pltpu.VMEM((1, LANES), F32)],
        compiler_params=_params(("arbitrary",)),
    )(x, tile0, norm_g, w_main, w_f, b_f)


def _pair_lanes(a0_t, a1_t):
    return jnp.concatenate([a0_t[:HEAD_DIM], a1_t[:HEAD_DIM]], axis=0).T


def _behind(items, ins, outs, sems):
    step, last = pl.program_id(0), pl.num_programs(0) - 1

    @pl.when(step == 0)
    def _():
        for cp in _exchange_copies(items, ins, outs, *sems):
            cp.start()

    def finish():
        @pl.when(step == last)
        def _():
            for cp in _exchange_copies(items, ins, outs, *sems):
                cp.wait()

    return finish


def _attention_forward(qt, k, vt, behind):
    lp = k.shape[1]
    tk = ATT_TILE
    q_blocks = ATT_Q_BLOCKS_FWD if (lp // tk - 1) % ATT_Q_BLOCKS_FWD == 0 else ATT_Q_BLOCKS_BWD
    tq_big = q_blocks * tk
    n_big = (lp // tk - 1) // q_blocks
    assert lp == tk + n_big * tq_big and q_blocks % 2 == 0
    nx = len(behind)

    def body(qt_ref, k_ref, vt_ref, *rest):
        o_ref, lse_ref = rest[nx:nx + 2]
        s_buf, m_scr, acc_scr = rest[2 * nx + 2:2 * nx + 5]
        finish_exchange = _behind(behind, rest[:nx], rest[nx + 2:2 * nx + 2], rest[2 * nx + 5:])

        def q_tile(q0, tq, pairs):
            first = q0 // tk
            qts = [qt_ref[e, :, pl.ds(q0, tq)] for e in range(2)]

            def block(kj):
                return pl.ds(kj * tk if isinstance(kj, int) else pl.multiple_of(kj * tk, tk), tk)

            def step(kj, rd, wr, c0=0, diagonal=False, keys=None):
                c1 = c0 + tk if diagonal else c0
                keys = block(kj) if keys is None else keys
                for e in range(2):
                    s = s_buf[rd, e, 0:keys.size, c0:tq]
                    if wr is not None:
                        s_buf[wr, e, :, c1:tq] = _dot(k_ref[e, block(kj + 1), :], qts[e][:, c1:tq])
                    if diagonal:
                        key = lax.broadcasted_iota(jnp.int32, s.shape, 0)
                        s = jnp.where(key <= lax.broadcasted_iota(jnp.int32, s.shape, 1), s, NEG)
                    m = m_scr[e, :, c0:tq]
                    m_new = jnp.maximum(m, jnp.max(s, axis=0, keepdims=True))
                    p = jnp.exp(s - m_new)
                    pv = _dot(vt_ref[e, :, keys], p.astype(BF16))
                    acc_scr[e, :, c0:tq] = jnp.exp(m - m_new) * acc_scr[e, :, c0:tq] + pv
                    m_scr[e, :, c0:tq] = m_new

            keys0 = block(0) if pairs is None else pl.ds(FIRST_KEY, tk - FIRST_KEY)
            for e in range(2):
                m_scr[e, :, 0:tq] = jnp.full((1, tq), NEG, F32)
                acc_scr[e, :, 0:tq] = jnp.zeros((LANES, tq), F32)
                s_buf[0, e, 0:keys0.size, 0:tq] = _dot(k_ref[e, keys0, :], qts[e])
            if pairs is None:
                step(0, 0, None, 0, True)
            else:
                step(0, 0, 1, keys=keys0)

                def two_steps(t, _):
                    step(1 + 2 * t, 1, 0)
                    step(2 + 2 * t, 0, 1)
                    return 0

                lax.fori_loop(0, pairs, two_steps, 0)
                for b in range(tq // tk):
                    step(first + b, (b + 1) % 2, b % 2 if (b + 1) * tk < tq else None, b * tk, True)
            outs, lses = [], []
            for e in range(2):
                acc = acc_scr[e, :, 0:tq]
                l = acc[V_ONES:V_ONES + 1, :]
                outs.append(acc / l)
                lses.append(m_scr[e, :, 0:tq] + jnp.log(l))
            o_ref[pl.ds(q0, tq), :] = _pair_lanes(outs[0], outs[1]).astype(BF16)
            lse_rows = jnp.concatenate(lses + [jnp.zeros((LANES - 2, tq), F32)], axis=0)
            lse_ref[pl.ds(q0, tq), :] = lse_rows.T

        q_tile(0, tk, None)

        def big_tile(i, _):
            q_tile(pl.multiple_of(tk + i * tq_big, tk), tq_big, (q_blocks // 2) * i)
            return 0

        lax.fori_loop(0, n_big, big_tile, 0)
        finish_exchange()

    pair = pl.BlockSpec((lp, LANES), lambda hp: (0, hp))
    heads = pl.BlockSpec((2, lp, LANES), lambda hp: (hp, 0, 0), pipeline_mode=pl.Buffered(1))
    heads_t = pl.BlockSpec((2, LANES, lp), lambda hp: (hp, 0, 0), pipeline_mode=pl.Buffered(1))
    hbm = pl.BlockSpec(memory_space=pl.ANY)
    return pl.pallas_call(
        body, name="attention_forward", grid=(N_HEADS // 2,),
        out_shape=[jax.ShapeDtypeStruct((lp, ATTN_WIDTH), BF16), jax.ShapeDtypeStruct((lp, ATTN_WIDTH), F32)]
        + _exchange_results(behind),
        in_specs=[heads_t, heads, heads_t] + [hbm] * nx,
        out_specs=[pair, pair] + [hbm] * nx,
        scratch_shapes=[pltpu.VMEM((2, 2, tk, tq_big), F32), pltpu.VMEM((2, 1, tq_big), F32),
                        pltpu.VMEM((2, LANES, tq_big), F32)] + _exchange_semaphores(nx),
        compiler_params=_params(("arbitrary",)),
    )(qt, k, vt, *[a for _, a, _ in behind])


def _rows3(first, x):
    sub = lax.broadcasted_iota(jnp.int32, (LANES, x.shape[1]), 0)
    hi = x.astype(BF16).astype(F32)
    rest = x - hi
    mid = rest.astype(BF16).astype(F32)
    lo = (rest - mid).astype(BF16).astype(F32)
    out = jnp.zeros((LANES, x.shape[1]), F32)
    for j, piece in enumerate((hi, mid, lo)):
        out = jnp.where(sub == first + j, piece, out)
    return out


def _attention_backward(qt, k, kt, v, do, o, lse, behind):
    lp = k.shape[1]
    tb = ATT_TILE
    nb = lp // tb
    tq_big = ATT_Q_BLOCKS_BWD * tb
    n_big = (nb - 1) // ATT_Q_BLOCKS_BWD
    assert lp == tb + n_big * tq_big and ATT_Q_BLOCKS_BWD % 2 == 0
    nx = len(behind)

    def body(qt_ref, k_ref, kt_ref, v_ref, do_ref, o_ref, lse_ref, *rest):
        dqkv_ref, dc_ref = rest[nx:nx + 2]
        q2_ref, do2_ref, dk_acc, dv_acc, dq_scr, s_buf = rest[2 * nx + 2:2 * nx + 8]
        finish_exchange = _behind(behind, rest[:nx], rest[nx + 2:2 * nx + 2], rest[2 * nx + 8:])
        sub = lax.broadcasted_iota(jnp.int32, (LANES, tb), 0)

        def lanes01(row0, row1):
            n = row0.shape[1]
            return jnp.concatenate([row0, row1, jnp.zeros((LANES - 2, n), F32)], axis=0).T

        def prepare(bi, _):
            r0 = pl.multiple_of(bi * tb, tb)
            queries = r0 + lax.broadcasted_iota(jnp.int32, (1, tb), 1)
            dob = do_ref[pl.ds(r0, tb), :].astype(F32)
            do_t = dob.T
            dd_t = (dob * o_ref[pl.ds(r0, tb), :].astype(F32)).T
            lse_t = lse_ref[pl.ds(r0, tb), :].T
            for e in range(2):
                delta = jnp.sum(dd_t[HEAD_DIM * e:HEAD_DIM * (e + 1), :], axis=0, keepdims=True)
                do_e = jnp.concatenate([do_t[HEAD_DIM * e:HEAD_DIM * (e + 1), :], jnp.zeros((HEAD_DIM, tb), F32)], axis=0)
                do2_ref[e, :, pl.ds(r0, tb)] = jnp.where(sub < HEAD_DIM, do_e, _rows3(DO_BIAS, -delta)).astype(BF16)
                minus_lse = jnp.where(queries >= PAD, -lse_t[e:e + 1, :], NEG)
                keep = (sub < Q_LSE) | (sub >= Q_LSE + 3)
                q2_ref[e, :, pl.ds(r0, tb)] = jnp.where(keep, qt_ref[e, :, pl.ds(r0, tb)].astype(F32),
                                                        _rows3(Q_LSE, minus_lse)).astype(BF16)
            return 0

        lax.fori_loop(0, nb, prepare, 0)
        dk_acc[...] = jnp.zeros_like(dk_acc)
        dv_acc[...] = jnp.zeros_like(dv_acc)

        def q_tile(q0, tq, pairs):
            first = q0 // tb
            qts = [q2_ref[e, :, pl.ds(q0, tq)] for e in range(2)]
            dots = [do2_ref[e, :, pl.ds(q0, tq)] for e in range(2)]

            def block(kj):
                return pl.ds(kj * tb if isinstance(kj, int) else pl.multiple_of(kj * tb, tb), tb)

            def step(kj, rd, wr, c0=0, diagonal=False, keys=None):
                c1 = c0 + tb if diagonal else c0
                keys = block(kj) if keys is None else keys
                for e in range(2):
                    s = s_buf[rd, e, 0:keys.size, c0:tq]
                    if wr is not None:
                        s_buf[wr, e, :, c1:tq] = _dot(k_ref[e, block(kj + 1), :], qts[e][:, c1:tq])
                    dpd = _dot(v_ref[e, keys, :], dots[e][:, c0:tq])
                    p = jnp.exp(s)
                    if diagonal:
                        key = lax.broadcasted_iota(jnp.int32, s.shape, 0)
                        p = jnp.where(key <= lax.broadcasted_iota(jnp.int32, s.shape, 1), p, 0.0)
                    dsb = (p * dpd).astype(BF16)
                    dv_acc[e, :, keys] += _dot_nt(dots[e][:, c0:tq], p.astype(BF16))
                    dk_acc[e, :, keys] += _dot_nt(qts[e][:, c0:tq], dsb)
                    dq_scr[e, :, c0:tq] += _dot(kt_ref[e, :, keys], dsb)

            keys0 = block(0) if pairs is None else pl.ds(FIRST_KEY, tb - FIRST_KEY)
            for e in range(2):
                dq_scr[e, :, 0:tq] = jnp.zeros((LANES, tq), F32)
                s_buf[0, e, 0:keys0.size, 0:tq] = _dot(k_ref[e, keys0, :], qts[e])
            if pairs is None:
                step(0, 0, None, 0, True)
            else:
                step(0, 0, 1, keys=keys0)

                def two_steps(t, _):
                    step(1 + 2 * t, 1, 0)
                    step(2 + 2 * t, 0, 1)
                    return 0

                lax.fori_loop(0, pairs, two_steps, 0)
                for b in range(tq // tb):
                    step(first + b, (b + 1) % 2, b % 2 if (b + 1) * tb < tq else None, b * tb, True)
            dq0, dq1 = dq_scr[0, :, 0:tq], dq_scr[1, :, 0:tq]
            dqkv_ref[0, pl.ds(q0, tq), :] = (_pair_lanes(dq0, dq1) * 0.125).astype(BF16)
            dc_ref[pl.ds(q0, tq), :] = lanes01(dq0[K_ONES:K_ONES + 1, :], dq1[K_ONES:K_ONES + 1, :])

        q_tile(0, tb, None)

        def big_tile(i, _):
            q_tile(pl.multiple_of(tb + i * tq_big, tb), tq_big, (ATT_Q_BLOCKS_BWD // 2) * i)
            return 0

        lax.fori_loop(0, n_big, big_tile, 0)

        def finish(bi, _):
            r0 = pl.multiple_of(bi * tb, tb)
            dk0, dk1 = dk_acc[0, :, pl.ds(r0, tb)], dk_acc[1, :, pl.ds(r0, tb)]
            dqkv_ref[1, pl.ds(r0, tb), :] = _pair_lanes(dk0, dk1).astype(BF16)
            dqkv_ref[2, pl.ds(r0, tb), :] = _pair_lanes(dv_acc[0, :, pl.ds(r0, tb)],
                                                        dv_acc[1, :, pl.ds(r0, tb)]).astype(BF16)
            dc_ref[pl.ds(r0, tb), :] = dc_ref[pl.ds(r0, tb), :] - lanes01(dk0[Q_ONES:Q_ONES + 1, :], dk1[Q_ONES:Q_ONES + 1, :])
            return 0

        lax.fori_loop(0, nb, finish, 0)
        finish_exchange()

    once = pl.Buffered(1)
    pair = pl.BlockSpec((lp, LANES), lambda hp: (0, hp))
    pair_in = pl.BlockSpec((lp, LANES), lambda hp: (0, hp), pipeline_mode=once)
    heads = pl.BlockSpec((2, lp, LANES), lambda hp: (hp, 0, 0), pipeline_mode=once)
    heads_t = pl.BlockSpec((2, LANES, lp), lambda hp: (hp, 0, 0), pipeline_mode=once)
    hbm = pl.BlockSpec(memory_space=pl.ANY)
    return pl.pallas_call(
        body, name="attention_backward", grid=(N_HEADS // 2,),
        out_shape=[jax.ShapeDtypeStruct((3, lp, ATTN_WIDTH), BF16), jax.ShapeDtypeStruct((lp, ATTN_WIDTH), F32)]
        + _exchange_results(behind),
        in_specs=[heads_t, heads, heads_t, heads, pair_in, pair_in, pair_in] + [hbm] * nx,
        out_specs=[pl.BlockSpec((3, lp, LANES), lambda hp: (0, 0, hp)), pair] + [hbm] * nx,
        scratch_shapes=[pltpu.VMEM((2, LANES, lp), BF16), pltpu.VMEM((2, LANES, lp), BF16),
                        pltpu.VMEM((2, LANES, lp), F32), pltpu.VMEM((2, LANES, lp), F32),
                        pltpu.VMEM((2, LANES, tq_big), F32), pltpu.VMEM((2, 2, tb, tq_big), F32)]
        + _exchange_semaphores(nx),
        compiler_params=_params(("arbitrary",)),
    )(qt, k, kt, v, do, o, lse, *[a for _, a, _ in behind])


def _middle(x, target, h, o, u, zp, w_main, w_up_pool, w_up_attn, w_out, pool_w, pool_scale, final_g):
    seq = x.shape[0]
    tm = ROW_TILE
    nt = seq // tm + 1
    lp = nt * tm
    halo_blocks = tm // MAX_WINDOW

    def body(x_ref, t_ref, h_ref, o_ref, u_ref, uh_ref, zp_ref,
             wc_ref, wupp_ref, wupa_ref, wout_ref, pw_ref, sc_ref, gf_ref,
             dh2_ref, mg_ref, yp_ref, ya_ref, dap_ref, daa_ref, do_ref, dmid_ref, dpn_ref,
             loss_ref, dgf_ref, dsc_ref, dpw_ref):
        i = pl.program_id(0)
        tiles = (dh2_ref, mg_ref, yp_ref, ya_ref, dap_ref, daa_ref, do_ref, dmid_ref, dpn_ref)

        @pl.when(i == 0)
        def _():
            for ref in tiles + (loss_ref, dgf_ref, dsc_ref, dpw_ref):
                ref[...] = jnp.zeros_like(ref)

        @pl.when(i > 0)
        def _():
            xt = x_ref[...]
            hb = h_ref[...]
            pc = _dot_nt(hb, wc_ref[...])
            za, gp, ga = pc[:, :512], pc[:, 512:1536], pc[:, 1536:]
            of = o_ref[...].astype(F32)
            sza = _sigmoid(za)
            silu_za = za * sza
            ya = (of * silu_za).astype(BF16)
            ya_ref[...] = ya
            aa = _dot(ya, wupa_ref[...])

            u = u_ref[...]
            zp = zp_ref[...]
            counts = _pool_counts(i * tm, tm)
            ps = _pool_means(jnp.concatenate([uh_ref[...], u], axis=0), u, counts)
            pbs = [p.astype(BF16) for p in ps]
            ppw = jnp.concatenate([_dot(pbs[g], pw_ref[g]) for g in range(4)], axis=1)
            sc = sc_ref[...]
            szp = _sigmoid(zp)
            silu_zp = zp * szp
            ypre = ppw * sc
            yp = (ypre * silu_zp).astype(BF16)
            yp_ref[...] = yp
            ap = _dot(yp, wupp_ref[...])

            sgp, sga = _sigmoid(gp), _sigmoid(ga)
            mg = (sgp * ap + sga * aa).astype(BF16)
            mg_ref[...] = mg
            h2 = xt + _dot(mg, wout_ref[...])
            r2 = lax.rsqrt(jnp.mean(h2 * h2, axis=-1, keepdims=True) + RMS_EPS)
            h2n = h2 * r2
            gf = gf_ref[...]
            diff = h2n * gf - t_ref[...]
            loss_ref[...] += 0.5 * jnp.sum(jnp.mean(diff * diff, axis=-1, keepdims=True), axis=0, keepdims=True)
            dy = diff * (1.0 / D_MODEL)
            dgf_ref[...] += jnp.sum(dy * h2n, axis=0, keepdims=True)
            dyg = dy * gf
            dh2 = r2 * (dyg - h2n * jnp.mean(dyg * h2n, axis=-1, keepdims=True))
            dh2_ref[...] = dh2
            dmg = _dot_nt(dh2.astype(BF16), wout_ref[...])
            dap = (dmg * sgp).astype(BF16)
            daa = (dmg * sga).astype(BF16)
            dap_ref[...] = dap
            daa_ref[...] = daa
            dmid_ref[:, MID_GP:MID_GA] = (dmg * ap * sgp * (1.0 - sgp)).astype(BF16)
            dmid_ref[:, MID_GA:] = (dmg * aa * sga * (1.0 - sga)).astype(BF16)
            dyp = _dot_nt(dap, wupp_ref[...])
            dya = _dot_nt(daa, wupa_ref[...])
            do_ref[...] = (dya * silu_za).astype(BF16)
            dmid_ref[:, MID_ZA:MID_GP] = (dya * of * (sza * (1.0 + za * (1.0 - sza)))).astype(BF16)

            dypre = dyp * silu_zp
            dmid_ref[:, :MID_ZA] = (dyp * ypre * (szp * (1.0 + zp * (1.0 - szp)))).astype(BF16)
            dsc_ref[...] += jnp.sum(dypre * ppw, axis=0, keepdims=True)
            dppw = (dypre * sc).astype(BF16)
            dpns = []
            for g in range(4):
                dg = dppw[:, POOL_GROUP * g:POOL_GROUP * (g + 1)]
                dpw_ref[g] += _dot_tn(pbs[g], dg)
                dpns.append(_dot_nt(dg, pw_ref[g]) / counts[g])
            dpn_ref[...] = jnp.concatenate(dpns, axis=1)

    real = lambda w: pl.BlockSpec((tm, w), lambda i: (jnp.maximum(i - 1, 0), 0))
    row = lambda w: pl.BlockSpec((tm, w), lambda i: (i, 0))
    in_specs = [
        real(D_MODEL), real(D_MODEL), row(D_MODEL), row(512), row(512),
        pl.BlockSpec((MAX_WINDOW, 512), lambda i: (jnp.maximum(i * halo_blocks - 1, 0), 0)), row(512),
        _const((2560, D_MODEL), (1, 0)), _const((POOL_WIDTH, D_MODEL)), _const((ATTN_WIDTH, D_MODEL)),
        _const((D_MODEL, D_MODEL)), _const((4, POOL_GROUP, POOL_GROUP)), _const((1, POOL_WIDTH)), _const((1, D_MODEL)),
    ]
    sd = jax.ShapeDtypeStruct
    out_shape = [
        sd((lp, D_MODEL), F32),
        sd((lp, D_MODEL), BF16),
        sd((lp, 512), BF16),
        sd((lp, 512), BF16),
        sd((lp, D_MODEL), BF16),
        sd((lp, D_MODEL), BF16),
        sd((lp, 512), BF16),
        sd((lp, MID_WIDTH), BF16),
        sd((lp, 512), F32),
        sd((1, LANES), F32),
        sd((1, D_MODEL), F32),
        sd((1, 512), F32),
        sd((4, POOL_GROUP, POOL_GROUP), F32),
    ]
    keep = lambda shape: pl.BlockSpec(shape, lambda i: (0,) * len(shape))
    out_specs = [row(D_MODEL), row(D_MODEL), row(512), row(512), row(D_MODEL), row(D_MODEL), row(512),
                 row(MID_WIDTH), row(512),
                 keep((1, LANES)), keep((1, D_MODEL)), keep((1, 512)), keep((4, POOL_GROUP, POOL_GROUP))]
    return pl.pallas_call(
        body, name="middle", grid=(nt,), out_shape=out_shape, in_specs=in_specs, out_specs=out_specs,
        compiler_params=_params(("arbitrary",)),
    )(x, target, h, o, u, u, zp, w_main, w_up_pool, w_up_attn, w_out, pool_w, pool_scale, final_g)


DUF_WIDTH = POOL_WIDTH + LANES


def _sequence_grads(dpn, dc, sneg):
    lp = dpn.shape[0]
    tm = ROW_TILE
    nt = lp // tm
    halo_blocks = tm // MAX_WINDOW
    last_halo = lp // MAX_WINDOW - 1

    def body(dpn_ref, dpnh_ref, dc_ref, sn_ref, duf_ref, dbf_ref, carry_ref):
        i = pl.program_id(0)
        t = nt - 1 - i

        @pl.when(i == 0)
        def _():
            carry_ref[...] = jnp.zeros_like(carry_ref)
            dbf_ref[...] = jnp.zeros_like(dbf_ref)

        dpn_t = dpn_ref[...]
        ahead = jnp.where(i == 0, jnp.zeros_like(dpnh_ref), dpnh_ref[...])
        ext = jnp.concatenate([dpn_t, ahead], axis=0)
        counts = _pool_counts(t * tm, tm)
        for g, w in enumerate(POOL_WINDOWS):
            s = ext[:, POOL_GROUP * g:POOL_GROUP * (g + 1)]
            sh = 1
            while sh < w:
                s = s + pltpu.roll(s, tm + MAX_WINDOW - sh, axis=0)
                sh *= 2
            du = s[:tm, :] - dpn_t[:, POOL_GROUP * g:POOL_GROUP * (g + 1)] * counts[g]
            duf_ref[:, POOL_GROUP * g:POOL_GROUP * (g + 1)] = du.astype(BF16)

        dct = dc_ref[:, 0:LANES]
        for hp in range(1, N_HEADS // 2):
            dct = dct + pltpu.roll(dc_ref[:, LANES * hp:LANES * (hp + 1)], 2 * hp, axis=1)
        rloc = lax.broadcasted_iota(jnp.int32, (tm, LANES), 0)
        sh = 1
        while sh < tm:
            dct = dct + jnp.where(rloc + sh < tm, pltpu.roll(dct, tm - sh, axis=0), 0.0)
            sh *= 2
        dct = dct + carry_ref[...]
        carry_ref[...] = dct[0:1, :]
        df = dct * sn_ref[...]
        dbf_ref[...] += jnp.sum(df, axis=0, keepdims=True)
        duf_ref[:, POOL_WIDTH:] = df.astype(BF16)

    rev = lambda w: pl.BlockSpec((tm, w), lambda i: (nt - 1 - i, 0))
    return pl.pallas_call(
        body, name="sequence_grads", grid=(nt,),
        out_shape=[jax.ShapeDtypeStruct((lp, DUF_WIDTH), BF16), jax.ShapeDtypeStruct((1, LANES), F32)],
        in_specs=[rev(512),
                  pl.BlockSpec((MAX_WINDOW, 512), lambda i: (jnp.minimum((nt - i) * halo_blocks, last_halo), 0)),
                  rev(512), rev(LANES)],
        out_specs=[rev(DUF_WIDTH), pl.BlockSpec((1, LANES), lambda i: (0, 0))],
        scratch_shapes=[pltpu.VMEM((1, LANES), F32)],
        compiler_params=_params(("arbitrary",)),
    )(dpn, dpn, dc, sneg)


def _backward_in(x, tile0, norm_g, dh2, duf, dqkv, dmid, w_main, w_f, behind):
    seq = x.shape[0]
    tm = ROW_TILE
    nt = seq // tm + 1
    nx = len(behind)

    def body(x_ref, t0_ref, g_ref, dh2_ref, du_ref, df_ref, dqkv_ref, dzp_ref, dza_ref, dgp_ref, dga_ref,
             wm_ref, wf_ref, *rest):
        gx_ref, gmeta_ref, dg_ref = rest[nx:nx + 3]
        dproj_ref = rest[2 * nx + 3]
        finish_exchange = _behind(behind, rest[:nx], rest[nx + 3:2 * nx + 3], rest[2 * nx + 4:])
        t = pl.program_id(0)

        @pl.when(t == 0)
        def _():
            dg_ref[...] = jnp.zeros_like(dg_ref)

        dproj_ref[:, 0:512] = du_ref[...]
        dproj_ref[:, 512:1024] = dzp_ref[...]
        dproj_ref[:, 1024:1536] = dqkv_ref[0]
        dproj_ref[:, 1536:2048] = dqkv_ref[1]
        dproj_ref[:, 2048:2560] = dqkv_ref[2]
        dproj_ref[:, 2560:3072] = dza_ref[...]
        dproj_ref[:, 3072:4096] = dgp_ref[...]
        dproj_ref[:, 4096:5120] = dga_ref[...]
        dh = _dot(dproj_ref[...], wm_ref[...]) + _dot(df_ref[...], wf_ref[...])
        xt = jnp.where(t == 0, t0_ref[...], x_ref[...])
        r = lax.rsqrt(jnp.mean(xt * xt, axis=-1, keepdims=True) + RMS_EPS)
        xn = xt * r
        dg_ref[...] += jnp.sum(dh * xn, axis=0, keepdims=True)
        dhg = dh * g_ref[...]
        dx = dh2_ref[...] + r * (dhg - xn * jnp.mean(dhg * xn, axis=-1, keepdims=True))

        @pl.when(t > 0)
        def _():
            gx_ref[...] = dx

        @pl.when(t == 0)
        def _():
            gmeta_ref[...] = dx[PAD:, :]
            gx_ref[...] = jnp.zeros_like(gx_ref)

        finish_exchange()

    row = lambda w, j=0: pl.BlockSpec((tm, w), lambda i: (i, j))
    real = pl.BlockSpec((tm, D_MODEL), lambda i: (jnp.maximum(i - 1, 0), 0))
    hbm = pl.BlockSpec(memory_space=pl.ANY)
    in_specs = [
        real, _const((tm, D_MODEL)), _const((1, D_MODEL)), row(D_MODEL),
        row(POOL_WIDTH), row(LANES, POOL_WIDTH // LANES), pl.BlockSpec((3, tm, ATTN_WIDTH), lambda i: (0, i, 0)),
        row(512, 0), row(512, 1), row(1024, 1), row(1024, 2),
        _const((N_MAIN, D_MODEL)), _const((LANES, D_MODEL)),
    ] + [hbm] * nx
    sd = jax.ShapeDtypeStruct
    out_shape = [sd((seq, D_MODEL), F32), sd((N_META, D_MODEL), F32), sd((1, D_MODEL), F32)] + _exchange_results(behind)
    keep = lambda shape: pl.BlockSpec(shape, lambda i: (0,) * len(shape))
    out_specs = [real, keep((N_META, D_MODEL)), keep((1, D_MODEL))] + [hbm] * nx
    return pl.pallas_call(
        body, name="backward_in", grid=(nt,), out_shape=out_shape, in_specs=in_specs, out_specs=out_specs,
        scratch_shapes=[pltpu.VMEM((tm, N_MAIN), BF16)] + _exchange_semaphores(nx),
        compiler_params=_params(("arbitrary",)),
    )(x, tile0, norm_g, dh2, duf, duf, dqkv, dmid, dmid, dmid, dmid, w_main, w_f, *[a for _, a, _ in behind])


def _matmul_tn(name, a, b, tn):
    lp, m = a.shape
    n = b.shape[1]

    def body(a_ref, b_ref, c_ref):
        c_ref[...] = _dot_tn(a_ref[...].astype(BF16), b_ref[...].astype(BF16))

    return pl.pallas_call(
        body, name=name, grid=(n // tn,), out_shape=jax.ShapeDtypeStruct((m, n), F32),
        in_specs=[_const((lp, m)), pl.BlockSpec((lp, tn), lambda j: (0, j))],
        out_specs=pl.BlockSpec((m, tn), lambda j: (0, j)),
        compiler_params=_params(("arbitrary",)),
    )(a, b)


def _matmul_tn_rows(name, a, b, tm):
    lp, m = a.shape
    n = b.shape[1]

    def body(a_ref, b_ref, c_ref):
        c_ref[...] = _dot_tn(a_ref[...].astype(BF16), b_ref[...].astype(BF16))

    return pl.pallas_call(
        body, name=name, grid=(m // tm,), out_shape=jax.ShapeDtypeStruct((m, n), F32),
        in_specs=[pl.BlockSpec((lp, tm), lambda j: (0, j)), _const((lp, n))],
        out_specs=pl.BlockSpec((tm, n), lambda j: (j, 0)),
        compiler_params=_params(("arbitrary",)),
    )(a, b)


def _matmul_tn_stack(name, a, b):
    n_blocks, lp, m = a.shape
    n = b.shape[1]

    def body(a_ref, b_ref, c_ref):
        c_ref[...] = _dot_tn(a_ref[...], b_ref[...])

    return pl.pallas_call(
        body, name=name, grid=(n_blocks,), out_shape=jax.ShapeDtypeStruct((n_blocks * m, n), F32),
        in_specs=[pl.BlockSpec((None, lp, m), lambda j: (j, 0, 0)), _const((lp, n))],
        out_specs=pl.BlockSpec((m, n), lambda j: (j, 0)),
        compiler_params=_params(("arbitrary",)),
    )(a, b)


def _adamw_step(p_ref, w_ref, m_ref, v_ref, g_ref, d_ref, mo_ref, vo_ref):
    g = p_ref[0].astype(F32)
    for s in range(1, p_ref.shape[0]):
        g = g + p_ref[s].astype(F32)
    m_new = ADAM_B1 * m_ref[...] + (1.0 - ADAM_B1) * g
    v_new = ADAM_B2 * v_ref[...] + (1.0 - ADAM_B2) * (g * g)
    m_hat = m_new / (1.0 - ADAM_B1 ** ADAM_STEP)
    v_hat = v_new / (1.0 - ADAM_B2 ** ADAM_STEP)
    g_ref[...] = g
    d_ref[...] = -ADAM_LR * (m_hat / (jnp.sqrt(v_hat) + ADAM_EPS) + ADAM_WD * w_ref[...])
    mo_ref[...] = m_new
    vo_ref[...] = v_new


def _adamw_small(name, groups, loss_parts):
    n = len(groups)

    def body(*refs):
        ins, outs = refs[:4 * n + 1], refs[4 * n + 1:]
        for j in range(n):
            _adamw_step(*ins[4 * j:4 * j + 4], *outs[4 * j:4 * j + 4])
        total = ins[-1][0]
        for s in range(1, N_DEV):
            total = total + ins[-1][s]
        outs[-1][...] = total

    vmem = pl.BlockSpec(memory_space=pltpu.VMEM)
    out_shape = [jax.ShapeDtypeStruct(w.shape, F32) for _, w, _, _ in groups for _ in range(4)]
    out_shape.append(jax.ShapeDtypeStruct(loss_parts.shape[1:], F32))
    res = pl.pallas_call(
        body, name=name, out_shape=out_shape, in_specs=[vmem] * (4 * n + 1), out_specs=[vmem] * (4 * n + 1),
        compiler_params=_params(),
    )(*[a for g in groups for a in g], loss_parts)
    return [res[4 * j:4 * j + 4] for j in range(n)], res[-1]


def _adamw(name, parts, w, m, v, rows, cols=None):
    r, c_all = w.shape
    c = cols or c_all
    n_parts = parts.shape[0]

    def body(p_ref, w_ref, m_ref, v_ref, g_ref, d_ref, mo_ref, vo_ref):
        _adamw_step(p_ref, w_ref, m_ref, v_ref, g_ref, d_ref, mo_ref, vo_ref)

    blk = pl.BlockSpec((rows, c), lambda i, j: (i, j))
    return pl.pallas_call(
        body, name=name, grid=(r // rows, c_all // c), out_shape=[jax.ShapeDtypeStruct((r, c_all), F32)] * 4,
        in_specs=[pl.BlockSpec((n_parts, rows, c), lambda i, j: (0, i, j)), blk, blk, blk],
        out_specs=[blk] * 4,
        compiler_params=_params(("arbitrary", "arbitrary")),
    )(parts, w, m, v)


def _pair_sum(name, mine, theirs, rows):
    n, r, c = mine.shape

    def body(a_ref, b_ref, o_ref):
        o_ref[...] = (a_ref[...].astype(F32) + b_ref[...].astype(F32)).astype(BF16)

    blk = pl.BlockSpec((1, rows, c), lambda j, i: (j, i, 0))
    return pl.pallas_call(
        body, name=name, grid=(n, r // rows), out_shape=jax.ShapeDtypeStruct((n, r, c), BF16),
        in_specs=[blk, blk], out_specs=blk,
        compiler_params=_params(("arbitrary", "arbitrary")),
    )(mine, theirs)


def _by_core(slots):
    by_core = slots.reshape((4, 2) + slots.shape[1:]).swapaxes(0, 1)
    c = lax.axis_index("c")
    return (lax.dynamic_index_in_dim(by_core, c, 0, keepdims=False),
            lax.dynamic_index_in_dim(by_core, 1 - c, 0, keepdims=False))


def _columns_to_slots(a):
    r, c8 = a.shape
    return a.reshape(r, N_DEV, c8 // N_DEV).transpose(1, 0, 2)


def _slots_to_columns(a):
    n, r, c = a.shape
    return a.transpose(1, 0, 2).reshape(r, n * c)


def kernel(x, meta_tokens, norm_g, w_in, b_forget, pool_w, pool_scale, w_up_pool, w_up_attn, w_out, final_norm_g, loss_target, m_meta_tokens, m_norm_g, m_w_in, m_b_forget, m_pool_w, m_pool_scale, m_w_up_pool, m_w_up_attn, m_w_out, m_final_norm_g, v_meta_tokens, v_norm_g, v_w_in, v_b_forget, v_pool_w, v_pool_scale, v_w_up_pool, v_w_up_attn, v_w_out, v_final_norm_g):
    xs = x[0]
    target = loss_target[0]

    g_in, g_meta = _gather_two_level("gather_weights", [w_in[0].T.astype(BF16), meta_tokens], (320, 8))
    w_full = g_in.reshape(N_DEV * g_in.shape[1], D_MODEL)
    w_main = jnp.concatenate([w_full[:N_BEFORE_F], w_full[N_BEFORE_F + N_HEADS:]], axis=0)
    w_f = jnp.pad(w_full[N_BEFORE_F:N_BEFORE_F + N_HEADS], ((0, LANES - N_HEADS), (0, 0)))
    meta = _slots_to_columns(g_meta)
    tile0 = jnp.concatenate([jnp.zeros((PAD, D_MODEL), F32), meta], axis=0)
    b_f = jnp.pad(b_forget, ((0, 0), (0, LANES - N_HEADS)))
    pw_b = pool_w[0].astype(BF16)
    final_g = final_norm_g.reshape(1, D_MODEL)

    h, u, zp, k, v, qt, kt, vt, sneg = _forward_in(xs, tile0, norm_g, w_main, w_f, b_f)
    o, lse, g_upp, g_upa, g_out = _attention_forward(
        qt, k, vt, [("gather", w.astype(BF16), ALL_PEERS) for w in (w_up_pool[0], w_up_attn[0], w_out[0])])
    wupp = _slots_to_columns(g_upp)
    wupa = _slots_to_columns(g_upa)
    wout = g_out.reshape(D_MODEL, D_MODEL)
    (dh2, mg, yp, ya, dap, daa, do, dmid, dpn,
     loss_part, d_final_g, d_scale, d_pool_w) = _middle(xs, target, h, o, u, zp, w_main, wupp, wupa, wout,
                                                        pw_b, pool_scale, final_g)
    dw_out = _matmul_tn("grad_w_out", mg, dh2, 256)
    dw_upp = _matmul_tn("grad_w_up_pool", yp, dap, 512)
    dw_upa = _matmul_tn("grad_w_up_attn", ya, daa, 512)
    dqkv, dc, p_upp, p_upa, p_out, p_pool_w, p_scale, p_final_g = _attention_backward(
        qt, k, kt, v, do, o, lse,
        [("scatter", _columns_to_slots(dw_upp).astype(BF16), ALL_PEERS),
         ("scatter", _columns_to_slots(dw_upa).astype(BF16), ALL_PEERS),
         ("scatter", dw_out.reshape(N_DEV, D_MODEL // N_DEV, D_MODEL).astype(BF16), ALL_PEERS),
         ("gather", d_pool_w.reshape(4 * POOL_GROUP, POOL_GROUP), ALL_PEERS),
         ("gather", d_scale, ALL_PEERS), ("gather", d_final_g, ALL_PEERS)])
    duf, d_bf = _sequence_grads(dpn, dc, sneg)
    g_uf = _matmul_tn_rows("grad_w_in_pool_forget", duf, h, DUF_WIDTH)
    g_qkv = _matmul_tn_stack("grad_w_in_attention", dqkv, h)
    g_mid = _matmul_tn_rows("grad_w_in_gates", dmid, h, 512)
    dw_in = jnp.concatenate([g_uf[:POOL_WIDTH], g_mid[:MID_ZA], g_qkv, g_mid[MID_ZA:MID_GP],
                             g_uf[POOL_WIDTH:POOL_WIDTH + N_HEADS], g_mid[MID_GP:]], axis=0)
    dw_in = dw_in.reshape(N_DEV, dw_in.shape[0] // N_DEV, D_MODEL)
    mine, for_sibling = _by_core(dw_in)
    from_sibling, = _exchange("swap_with_sibling", [("swap", for_sibling.astype(BF16), (SIBLING,))])
    pair_sums = _pair_sum("pair_sum", mine, from_sibling, dw_in.shape[1])
    grad_x, d_meta, d_norm_g, p_in, p_bf, p_loss = _backward_in(
        xs, tile0, norm_g, dh2, duf, dqkv, dmid, w_main, w_f,
        [("chips", pair_sums, SAME_CORE), ("gather", d_bf, ALL_PEERS), ("gather", loss_part, ALL_PEERS)])
    p_meta, p_norm_g = _exchange(
        "exchange_gradients", [("scatter", _columns_to_slots(d_meta), ALL_PEERS), ("gather", d_norm_g, ALL_PEERS)])


    def pad_f(a):
        return jnp.pad(a, ((0, 0), (0, LANES - N_HEADS)))

    res = {}
    res["w_in"] = [a.T for a in _adamw("adamw_w_in", p_in, w_in[0].T, m_w_in[0].T, v_w_in[0].T, p_in.shape[1], 256)]
    res["w_up_pool"] = _adamw("adamw_w_up_pool", p_upp, w_up_pool[0], m_w_up_pool[0], v_w_up_pool[0], 512)
    res["w_up_attn"] = _adamw("adamw_w_up_attn", p_upa, w_up_attn[0], m_w_up_attn[0], v_w_up_attn[0], 512)
    res["w_out"] = _adamw("adamw_w_out", p_out, w_out[0], m_w_out[0], v_w_out[0], 128)
    flat = lambda a: a.reshape(4 * POOL_GROUP, POOL_GROUP)
    row = lambda a: a.reshape(1, D_MODEL)
    small, loss_row = _adamw_small(
        "adamw_small",
        [(p_meta, meta_tokens, m_meta_tokens, v_meta_tokens),
         (p_norm_g, norm_g, m_norm_g, v_norm_g),
         (p_bf, pad_f(b_forget), pad_f(m_b_forget), pad_f(v_b_forget)),
         (p_pool_w, flat(pool_w), flat(m_pool_w), flat(v_pool_w)),
         (p_scale, pool_scale, m_pool_scale, v_pool_scale),
         (p_final_g, final_g, row(m_final_norm_g), row(v_final_norm_g))],
        p_loss)
    res["meta_tokens"], res["norm_g"], bf, pw, res["pool_scale"], fg = small
    res["b_forget"] = [a[:, :N_HEADS] for a in bf]
    res["pool_w"] = [a.reshape(pool_w.shape) for a in pw]
    res["final_norm_g"] = [a.reshape(D_MODEL) for a in fg]
    loss = loss_row[0, 0]
    for name in ("w_in", "w_up_pool", "w_up_attn", "w_out"):
        res[name] = [a[None] for a in res[name]]

    order = ["meta_tokens", "norm_g", "w_in", "b_forget", "pool_w", "pool_scale", "w_up_pool", "w_up_attn", "w_out",
             "final_norm_g"]
    outs = [loss, grad_x[None]]
    for part in range(4):
        outs += [res[name][part] for name in order]
    return tuple(outs)
```

```python
import jax
import jax.numpy as jnp
from jax import lax
from jax.experimental import pallas as pl
from jax.experimental.pallas import tpu as pltpu

F32 = jnp.float32
BF16 = jnp.bfloat16

D_MODEL = 1024
N_META = 16
POOL_WIDTH = 512
ATTN_WIDTH = 512
N_HEADS = 8
HEAD_DIM = 64
POOL_WINDOWS = (2, 4, 8, 16)
POOL_GROUP = 128
MAX_WINDOW = 16
RMS_EPS = 1e-6
N_MAIN = 5120
N_BEFORE_F = 3072
N_DEV = 8
LANES = 128

ROW_TILE = 256
ATT_TILE = 256
ATT_Q_BLOCKS_FWD = 8
ATT_Q_BLOCKS_BWD = 4
PAD = ROW_TILE - N_META
FIRST_KEY = PAD // LANES * LANES
VMEM_LIMIT = 56 * 1024 * 1024

ADAM_LR = 0.001
ADAM_B1 = 0.9
ADAM_B2 = 0.999
ADAM_EPS = 1e-08
ADAM_WD = 0.01
ADAM_STEP = 10

MID_ZA, MID_GP, MID_GA, MID_WIDTH = 512, 1024, 2048, 3072
NEG = -1e30
MESH = pl.DeviceIdType.MESH


def _params(sem=None):
    kw = dict(vmem_limit_bytes=VMEM_LIMIT)
    if sem is not None:
        kw["dimension_semantics"] = sem
    return pltpu.CompilerParams(**kw)


def _const(shape, block_index=None):
    idx = block_index or (0,) * len(shape)
    return pl.BlockSpec(shape, lambda i: idx, pipeline_mode=pl.Buffered(1))


def _sigmoid(x):
    return jax.nn.sigmoid(x)


def _dot(a, b):
    return jnp.dot(a, b, preferred_element_type=F32)


def _dot_nt(a, b):
    return lax.dot_general(a, b, (((1,), (1,)), ((), ())), preferred_element_type=F32)


def _dot_tn(a, b):
    return lax.dot_general(a, b, (((0,), (0,)), ((), ())), preferred_element_type=F32)


def _pool_counts(first_row, rows):
    row = first_row + lax.broadcasted_iota(jnp.int32, (rows, 1), 0)
    pos1 = row - PAD + 1
    return [jnp.clip(pos1, 1, w).astype(F32) for w in POOL_WINDOWS]


def _pool_means(u_ext, u, counts):
    rows = u.shape[0]
    out = []
    for g, w in enumerate(POOL_WINDOWS):
        s = u_ext[:, POOL_GROUP * g:POOL_GROUP * (g + 1)]
        sh = 1
        while sh < w:
            s = s + pltpu.roll(s, sh, axis=0)
            sh *= 2
        out.append(s[MAX_WINDOW:MAX_WINDOW + rows, :] / counts[g] - u[:, POOL_GROUP * g:POOL_GROUP * (g + 1)])
    return out


Q_BIAS, Q_ONES, Q_LSE = 64, 67, 70
K_ONES, K_BIAS, K_ONES2 = 64, 67, 70
V_ONES = 64
DO_BIAS = 64


def _lane_ones(lane, ranges):
    hit = None
    for lo, hi in ranges:
        r = (lane >= lo) & (lane < hi)
        hit = r if hit is None else hit | r
    return jnp.where(hit, 1.0, 0.0)


def _put3(base, lane, first, x):
    hi = x.astype(BF16).astype(F32)
    rest = x - hi
    mid = rest.astype(BF16).astype(F32)
    lo = (rest - mid).astype(BF16).astype(F32)
    for j, piece in enumerate((hi, mid, lo)):
        base = jnp.where(lane == first + j, piece, base)
    return base


SIBLING = 1
SAME_CORE = (2, 4, 6)
ALL_PEERS = (1, 2, 3, 4, 5, 6, 7)


def _place():
    return lax.axis_index("x"), lax.axis_index("y"), lax.axis_index("c")


def _peer(r):
    x, y, c = _place()
    return (1 - x if r & 4 else x, 1 - y if r & 2 else y, 1 - c if r & 1 else c)


def _device_slot(p):
    return 4 * p[0] + 2 * p[1] + p[2]


def _chip_slot(p):
    return 2 * p[0] + p[1]


def _exchange(name, items):
    n = len(items)

    def body(*refs):
        copies = _exchange_copies(items, refs[:n], refs[n:2 * n], *refs[2 * n:])
        for cp in copies:
            cp.start()
        for cp in copies:
            cp.wait()

    hbm = pl.BlockSpec(memory_space=pl.ANY)
    return pl.pallas_call(
        body, name=name, out_shape=_exchange_results(items),
        in_specs=[hbm] * n, out_specs=[hbm] * n,
        scratch_shapes=_exchange_semaphores(n),
    )(*[a for _, a, _ in items])


def _exchange_results(items):
    return [jax.ShapeDtypeStruct(((N_DEV,) if kind == "gather" else ()) + a.shape, a.dtype) for kind, a, _ in items]


def _exchange_semaphores(n):
    return [pltpu.SemaphoreType.DMA((n, N_DEV - 1)), pltpu.SemaphoreType.DMA((n, N_DEV - 1)),
            pltpu.SemaphoreType.DMA((n,))]


def _exchange_copies(items, ins, outs, send_sems, recv_sems, local_sems):
    me = _place()
    copies = []
    for a, (kind, _, peers) in enumerate(items):
        slot = _chip_slot if kind == "chips" else _device_slot
        for r in peers:
            peer = _peer(r)
            src = ins[a] if kind in ("swap", "gather") else ins[a].at[slot(peer)]
            dst = outs[a] if kind == "swap" else outs[a].at[slot(me)]
            copies.append(pltpu.make_async_remote_copy(
                src_ref=src, dst_ref=dst, send_sem=send_sems.at[a, r - 1], recv_sem=recv_sems.at[a, r - 1],
                device_id=peer, device_id_type=MESH))
        if kind != "swap":
            src = ins[a] if kind == "gather" else ins[a].at[slot(me)]
            copies.append(pltpu.make_async_copy(src, outs[a].at[slot(me)], local_sems.at[a]))
    return copies


def _gather_two_level(name, arrays, halves):
    n = len(arrays)
    x_flip, y_flip, both = 4, 2, 6
    to_sibling, to_x, to_y, on_over_y, on_over_x, x_to_sibling, y_to_sibling, d_to_sibling = range(8)

    def body(*refs):
        ins, outs = refs[:n], refs[n:2 * n]
        send_sems, recv_sems, local_sems = refs[2 * n:]
        me, sibling = _place(), _peer(SIBLING)
        xn, yn, dg = _peer(x_flip), _peer(y_flip), _peer(both)

        def rows(a, h):
            return pl.ds(0, halves[a]) if h == 0 else pl.ds(halves[a], arrays[a].shape[0] - halves[a])

        def copy(a, h, k, block, to, own=False):
            dst = outs[a].at[_device_slot(block)].at[rows(a, h)]
            return pltpu.make_async_remote_copy(
                src_ref=ins[a].at[rows(a, h)] if own else dst, dst_ref=dst,
                send_sem=send_sems.at[2 * a + h, k], recv_sem=recv_sems.at[2 * a + h, k],
                device_id=to, device_id_type=MESH)

        sends, mine = [], []

        def start(cp):
            cp.start()
            sends.append(cp)

        for a in range(n):
            cp = pltpu.make_async_copy(ins[a], outs[a].at[_device_slot(me)], local_sems.at[a])
            cp.start()
            mine.append(cp)
        for a in range(n):
            start(copy(a, 0, to_x, me, xn, own=True))
            start(copy(a, 1, to_y, me, yn, own=True))
        for a in range(n):
            start(copy(a, 0, to_y, me, yn, own=True))
            start(copy(a, 1, to_x, me, xn, own=True))
        for a in range(n):
            for h in range(2):
                start(copy(a, h, to_sibling, me, sibling, own=True))
        for a in range(n):
            copy(a, 0, to_x, xn, me).wait_recv()
            start(copy(a, 0, on_over_y, xn, yn))
            start(copy(a, 0, x_to_sibling, xn, sibling))
            copy(a, 1, to_y, yn, me).wait_recv()
            start(copy(a, 1, on_over_x, yn, xn))
            start(copy(a, 1, y_to_sibling, yn, sibling))
        for a in range(n):
            copy(a, 0, to_y, yn, me).wait_recv()
            start(copy(a, 0, y_to_sibling, yn, sibling))
            copy(a, 1, to_x, xn, me).wait_recv()
            start(copy(a, 1, x_to_sibling, xn, sibling))
        for a in range(n):
            copy(a, 0, on_over_y, dg, me).wait_recv()
            start(copy(a, 0, d_to_sibling, dg, sibling))
            copy(a, 1, on_over_x, dg, me).wait_recv()
            start(copy(a, 1, d_to_sibling, dg, sibling))
        for a in range(n):
            for h in range(2):
                copy(a, h, to_sibling, sibling, me).wait_recv()
                for k, r in ((x_to_sibling, x_flip), (y_to_sibling, y_flip), (d_to_sibling, both)):
                    copy(a, h, k, _peer(r | SIBLING), me).wait_recv()
        for cp in sends:
            cp.wait_send()
        for cp in mine:
            cp.wait()

    hbm = pl.BlockSpec(memory_space=pl.ANY)
    return pl.pallas_call(
        body, name=name, out_shape=[jax.ShapeDtypeStruct((N_DEV,) + a.shape, a.dtype) for a in arrays],
        in_specs=[hbm] * n, out_specs=[hbm] * n,
        scratch_shapes=[pltpu.SemaphoreType.DMA((2 * n, 8)), pltpu.SemaphoreType.DMA((2 * n, 8)),
                        pltpu.SemaphoreType.DMA((n,))],
    )(*arrays)


def _forward_in(x, tile0, norm_g, w_main, w_f, b_f):
    seq = x.shape[0]
    nt = seq // ROW_TILE + 1
    lp = nt * ROW_TILE
    tm = ROW_TILE

    def body(x_ref, t0_ref, g_ref, wa_ref, wf_ref, bf_ref,
             h_ref, u_ref, zp_ref, k_ref, v_ref, qt_ref, kt_ref, vt_ref, sn_ref, carry_ref):
        i = pl.program_id(0)

        @pl.when(i == 0)
        def _():
            carry_ref[...] = jnp.zeros_like(carry_ref)

        xt = jnp.where(i == 0, t0_ref[...], x_ref[...])
        r = lax.rsqrt(jnp.mean(xt * xt, axis=-1, keepdims=True) + RMS_EPS)
        h = (xt * r * g_ref[...]).astype(BF16)
        h_ref[...] = h
        pa = _dot_nt(h, wa_ref[...])
        u_ref[...] = pa[:, :512]
        zp_ref[...] = pa[:, 512:1024]

        fl = _dot_nt(h, wf_ref[...]) + bf_ref[...]
        row = i * tm + lax.broadcasted_iota(jnp.int32, (tm, LANES), 0)
        rloc = lax.broadcasted_iota(jnp.int32, (tm, LANES), 0)
        lane = lax.broadcasted_iota(jnp.int32, (tm, LANES), 1)
        live = (row >= PAD) & (lane < N_HEADS)
        logf = jnp.minimum(fl, 0.0) - jnp.log1p(jnp.exp(-jnp.abs(fl)))
        cs = jnp.where(live, logf, 0.0)
        sh = 1
        while sh < tm:
            cs = cs + jnp.where(rloc >= sh, pltpu.roll(cs, sh, axis=0), 0.0)
            sh *= 2
        cs = cs + carry_ref[...]
        carry_ref[...] = cs[tm - 1:tm, :]
        sn_ref[...] = jnp.where(live, _sigmoid(-fl), 0.0)

        rows1 = i * tm + lax.broadcasted_iota(jnp.int32, (tm, 1), 0)
        cols1 = i * tm + lax.broadcasted_iota(jnp.int32, (1, tm), 1)
        sub = lax.broadcasted_iota(jnp.int32, (LANES, tm), 0)
        ones_k = _lane_ones(lane, ((K_ONES, K_ONES + 3), (K_ONES2, K_ONES2 + 3)))
        ones_v = _lane_ones(lane, ((V_ONES, V_ONES + 3),))
        ones_q_t = _lane_ones(sub, ((Q_ONES, Q_ONES + 3),))
        ones_k_t = _lane_ones(sub, ((K_ONES, K_ONES + 3), (K_ONES2, K_ONES2 + 3)))
        ones_v_t = _lane_ones(sub, ((V_ONES, V_ONES + 3),))
        cs_t = cs.T
        for hp in range(N_HEADS // 2):
            kp = pa[:, 1536 + LANES * hp:1536 + LANES * (hp + 1)]
            vp = pa[:, 2048 + LANES * hp:2048 + LANES * (hp + 1)]
            qp_t = (pa[:, 1024 + LANES * hp:1024 + LANES * (hp + 1)] * 0.125).T
            kp_t, vp_t = kp.T, vp.T
            for e in range(2):
                head = 2 * hp + e
                if e:
                    kp, vp = pltpu.roll(kp, HEAD_DIM, axis=1), pltpu.roll(vp, HEAD_DIM, axis=1)
                    qp_t, kp_t, vp_t = (pltpu.roll(a, HEAD_DIM, axis=0) for a in (qp_t, kp_t, vp_t))
                c_row = cs_t[head:head + 1, :]
                minus_ck_row = jnp.where(cols1 >= PAD, -c_row, NEG)
                qt_ref[head] = jnp.where(sub < HEAD_DIM, qp_t, _rows3(Q_BIAS, c_row) + ones_q_t).astype(BF16)
                kt_ref[head] = jnp.where(sub < HEAD_DIM, kp_t, _rows3(K_BIAS, minus_ck_row) + ones_k_t).astype(BF16)
                vt_ref[head] = jnp.where(sub < HEAD_DIM, vp_t, ones_v_t).astype(BF16)
                minus_ck = jnp.where(rows1 >= PAD, -cs[:, head:head + 1], NEG)
                k_ref[head] = jnp.where(lane < HEAD_DIM, kp, _put3(ones_k, lane, K_BIAS, minus_ck)).astype(BF16)
                v_ref[head] = jnp.where(lane < HEAD_DIM, vp, ones_v).astype(BF16)

    row_f32 = lambda w: pl.BlockSpec((tm, w), lambda i: (i, 0))
    out_shape = [
        jax.ShapeDtypeStruct((lp, D_MODEL), BF16),
        jax.ShapeDtypeStruct((lp, POOL_WIDTH), F32),
        jax.ShapeDtypeStruct((lp, POOL_WIDTH), F32),
        jax.ShapeDtypeStruct((N_HEADS, lp, LANES), BF16),
        jax.ShapeDtypeStruct((N_HEADS, lp, LANES), BF16),
        jax.ShapeDtypeStruct((N_HEADS, LANES, lp), BF16),
        jax.ShapeDtypeStruct((N_HEADS, LANES, lp), BF16),
        jax.ShapeDtypeStruct((N_HEADS, LANES, lp), BF16),
        jax.ShapeDtypeStruct((lp, LANES), F32),
    ]
    heads = pl.BlockSpec((N_HEADS, tm, LANES), lambda i: (0, i, 0))
    heads_t = pl.BlockSpec((N_HEADS, LANES, tm), lambda i: (0, 0, i))
    out_specs = [row_f32(D_MODEL), row_f32(512), row_f32(512), heads, heads, heads_t, heads_t, heads_t,
                 row_f32(LANES)]
    in_specs = [
        pl.BlockSpec((tm, D_MODEL), lambda i: (jnp.maximum(i - 1, 0), 0)),
        _const((tm, D_MODEL)), _const((1, D_MODEL)),
        _const((2560, D_MODEL)), _const((LANES, D_MODEL)), _const((1, LANES)),
    ]
    return pl.pallas_call(
        body, name="forward_in", grid=(nt,), out_shape=out_shape, in_specs=in_specs, out_specs=out_specs,
        scratch_shapes=[pltpu.VMEM((1, LANES), F32)],
        compiler_params=_params(("arbitrary",)),
    )(x, tile0, norm_g, w_main, w_f, b_f)


def _pair_lanes(a0_t, a1_t):
    return jnp.concatenate([a0_t[:HEAD_DIM], a1_t[:HEAD_DIM]], axis=0).T


def _behind(items, ins, outs, sems):
    step, last = pl.program_id(0), pl.num_programs(0) - 1

    @pl.when(step == 0)
    def _():
        for cp in _exchange_copies(items, ins, outs, *sems):
            cp.start()

    def finish():
        @pl.when(step == last)
        def _():
            for cp in _exchange_copies(items, ins, outs, *sems):
                cp.wait()

    return finish


def _attention_forward(qt, k, vt, behind):
    lp = k.shape[1]
    tk = ATT_TILE
    q_blocks = ATT_Q_BLOCKS_FWD if (lp // tk - 1) % ATT_Q_BLOCKS_FWD == 0 else ATT_Q_BLOCKS_BWD
    tq_big = q_blocks * tk
    n_big = (lp // tk - 1) // q_blocks
    assert lp == tk + n_big * tq_big and q_blocks % 2 == 0
    nx = len(behind)

    def body(qt_ref, k_ref, vt_ref, *rest):
        o_ref, lse_ref = rest[nx:nx + 2]
        s_buf, m_scr, acc_scr = rest[2 * nx + 2:2 * nx + 5]
        finish_exchange = _behind(behind, rest[:nx], rest[nx + 2:2 * nx + 2], rest[2 * nx + 5:])

        def q_tile(q0, tq, pairs):
            first = q0 // tk
            qts = [qt_ref[e, :, pl.ds(q0, tq)] for e in range(2)]

            def block(kj):
                return pl.ds(kj * tk if isinstance(kj, int) else pl.multiple_of(kj * tk, tk), tk)

            def step(kj, rd, wr, c0=0, diagonal=False, keys=None):
                c1 = c0 + tk if diagonal else c0
                keys = block(kj) if keys is None else keys
                for e in range(2):
                    s = s_buf[rd, e, 0:keys.size, c0:tq]
                    if wr is not None:
                        s_buf[wr, e, :, c1:tq] = _dot(k_ref[e, block(kj + 1), :], qts[e][:, c1:tq])
                    if diagonal:
                        key = lax.broadcasted_iota(jnp.int32, s.shape, 0)
                        s = jnp.where(key <= lax.broadcasted_iota(jnp.int32, s.shape, 1), s, NEG)
                    m = m_scr[e, :, c0:tq]
                    m_new = jnp.maximum(m, jnp.max(s, axis=0, keepdims=True))
                    p = jnp.exp(s - m_new)
                    pv = _dot(vt_ref[e, :, keys], p.astype(BF16))
                    acc_scr[e, :, c0:tq] = jnp.exp(m - m_new) * acc_scr[e, :, c0:tq] + pv
                    m_scr[e, :, c0:tq] = m_new

            keys0 = block(0) if pairs is None else pl.ds(FIRST_KEY, tk - FIRST_KEY)
            for e in range(2):
                m_scr[e, :, 0:tq] = jnp.full((1, tq), NEG, F32)
                acc_scr[e, :, 0:tq] = jnp.zeros((LANES, tq), F32)
                s_buf[0, e, 0:keys0.size, 0:tq] = _dot(k_ref[e, keys0, :], qts[e])
            if pairs is None:
                step(0, 0, None, 0, True)
            else:
                step(0, 0, 1, keys=keys0)

                def two_steps(t, _):
                    step(1 + 2 * t, 1, 0)
                    step(2 + 2 * t, 0, 1)
                    return 0

                lax.fori_loop(0, pairs, two_steps, 0)
                for b in range(tq // tk):
                    step(first + b, (b + 1) % 2, b % 2 if (b + 1) * tk < tq else None, b * tk, True)
            outs, lses = [], []
            for e in range(2):
                acc = acc_scr[e, :, 0:tq]
                l = acc[V_ONES:V_ONES + 1, :]
                outs.append(acc / l)
                lses.append(m_scr[e, :, 0:tq] + jnp.log(l))
            o_ref[pl.ds(q0, tq), :] = _pair_lanes(outs[0], outs[1]).astype(BF16)
            lse_rows = jnp.concatenate(lses + [jnp.zeros((LANES - 2, tq), F32)], axis=0)
            lse_ref[pl.ds(q0, tq), :] = lse_rows.T

        q_tile(0, tk, None)

        def big_tile(i, _):
            q_tile(pl.multiple_of(tk + i * tq_big, tk), tq_big, (q_blocks // 2) * i)
            return 0

        lax.fori_loop(0, n_big, big_tile, 0)
        finish_exchange()

    pair = pl.BlockSpec((lp, LANES), lambda hp: (0, hp))
    heads = pl.BlockSpec((2, lp, LANES), lambda hp: (hp, 0, 0), pipeline_mode=pl.Buffered(1))
    heads_t = pl.BlockSpec((2, LANES, lp), lambda hp: (hp, 0, 0), pipeline_mode=pl.Buffered(1))
    hbm = pl.BlockSpec(memory_space=pl.ANY)
    return pl.pallas_call(
        body, name="attention_forward", grid=(N_HEADS // 2,),
        out_shape=[jax.ShapeDtypeStruct((lp, ATTN_WIDTH), BF16), jax.ShapeDtypeStruct((lp, ATTN_WIDTH), F32)]
        + _exchange_results(behind),
        in_specs=[heads_t, heads, heads_t] + [hbm] * nx,
        out_specs=[pair, pair] + [hbm] * nx,
        scratch_shapes=[pltpu.VMEM((2, 2, tk, tq_big), F32), pltpu.VMEM((2, 1, tq_big), F32),
                        pltpu.VMEM((2, LANES, tq_big), F32)] + _exchange_semaphores(nx),
        compiler_params=_params(("arbitrary",)),
    )(qt, k, vt, *[a for _, a, _ in behind])


def _rows3(first, x):
    sub = lax.broadcasted_iota(jnp.int32, (LANES, x.shape[1]), 0)
    hi = x.astype(BF16).astype(F32)
    rest = x - hi
    mid = rest.astype(BF16).astype(F32)
    lo = (rest - mid).astype(BF16).astype(F32)
    out = jnp.zeros((LANES, x.shape[1]), F32)
    for j, piece in enumerate((hi, mid, lo)):
        out = jnp.where(sub == first + j, piece, out)
    return out


def _attention_backward(qt, k, kt, v, do, o, lse, behind):
    lp = k.shape[1]
    tb = ATT_TILE
    nb = lp // tb
    tq_big = ATT_Q_BLOCKS_BWD * tb
    n_big = (nb - 1) // ATT_Q_BLOCKS_BWD
    assert lp == tb + n_big * tq_big and ATT_Q_BLOCKS_BWD % 2 == 0
    nx = len(behind)

    def body(qt_ref, k_ref, kt_ref, v_ref, do_ref, o_ref, lse_ref, *rest):
        dqkv_ref, dc_ref = rest[nx:nx + 2]
        q2_ref, do2_ref, dk_acc, dv_acc, dq_scr, s_buf = rest[2 * nx + 2:2 * nx + 8]
        finish_exchange = _behind(behind, rest[:nx], rest[nx + 2:2 * nx + 2], rest[2 * nx + 8:])
        sub = lax.broadcasted_iota(jnp.int32, (LANES, tb), 0)
        first_head = 2 * pl.program_id(0)

        def prepare(bi, _):
            r0 = pl.multiple_of(bi * tb, tb)
            queries = r0 + lax.broadcasted_iota(jnp.int32, (1, tb), 1)
            dob = do_ref[pl.ds(r0, tb), :].astype(F32)
            do_t = dob.T
            dd_t = (dob * o_ref[pl.ds(r0, tb), :].astype(F32)).T
            lse_t = lse_ref[pl.ds(r0, tb), :].T
            for e in range(2):
                delta = jnp.sum(dd_t[HEAD_DIM * e:HEAD_DIM * (e + 1), :], axis=0, keepdims=True)
                do_e = jnp.concatenate([do_t[HEAD_DIM * e:HEAD_DIM * (e + 1), :], jnp.zeros((HEAD_DIM, tb), F32)], axis=0)
                do2_ref[e, :, pl.ds(r0, tb)] = jnp.where(sub < HEAD_DIM, do_e, _rows3(DO_BIAS, -delta)).astype(BF16)
                minus_lse = jnp.where(queries >= PAD, -lse_t[e:e + 1, :], NEG)
                keep = (sub < Q_LSE) | (sub >= Q_LSE + 3)
                q2_ref[e, :, pl.ds(r0, tb)] = jnp.where(keep, qt_ref[e, :, pl.ds(r0, tb)].astype(F32),
                                                        _rows3(Q_LSE, minus_lse)).astype(BF16)
            return 0

        lax.fori_loop(0, nb, prepare, 0)
        dk_acc[...] = jnp.zeros_like(dk_acc)
        dv_acc[...] = jnp.zeros_like(dv_acc)

        def q_tile(q0, tq, pairs):
            first = q0 // tb
            qts = [q2_ref[e, :, pl.ds(q0, tq)] for e in range(2)]
            dots = [do2_ref[e, :, pl.ds(q0, tq)] for e in range(2)]

            def block(kj):
                return pl.ds(kj * tb if isinstance(kj, int) else pl.multiple_of(kj * tb, tb), tb)

            def step(kj, rd, wr, c0=0, diagonal=False, keys=None):
                c1 = c0 + tb if diagonal else c0
                keys = block(kj) if keys is None else keys
                for e in range(2):
                    s = s_buf[rd, e, 0:keys.size, c0:tq]
                    if wr is not None:
                        s_buf[wr, e, :, c1:tq] = _dot(k_ref[e, block(kj + 1), :], qts[e][:, c1:tq])
                    dpd = _dot(v_ref[e, keys, :], dots[e][:, c0:tq])
                    p = jnp.exp(s)
                    if diagonal:
                        key = lax.broadcasted_iota(jnp.int32, s.shape, 0)
                        p = jnp.where(key <= lax.broadcasted_iota(jnp.int32, s.shape, 1), p, 0.0)
                    dsb = (p * dpd).astype(BF16)
                    dv_acc[e, :, keys] += _dot_nt(dots[e][:, c0:tq], p.astype(BF16))
                    dk_acc[e, :, keys] += _dot_nt(qts[e][:, c0:tq], dsb)
                    dq_scr[e, :, c0:tq] += _dot(kt_ref[e, :, keys], dsb)

            keys0 = block(0) if pairs is None else pl.ds(FIRST_KEY, tb - FIRST_KEY)
            for e in range(2):
                dq_scr[e, :, 0:tq] = jnp.zeros((LANES, tq), F32)
                s_buf[0, e, 0:keys0.size, 0:tq] = _dot(k_ref[e, keys0, :], qts[e])
            if pairs is None:
                step(0, 0, None, 0, True)
            else:
                step(0, 0, 1, keys=keys0)

                def two_steps(t, _):
                    step(1 + 2 * t, 1, 0)
                    step(2 + 2 * t, 0, 1)
                    return 0

                lax.fori_loop(0, pairs, two_steps, 0)
                for b in range(tq // tb):
                    step(first + b, (b + 1) % 2, b % 2 if (b + 1) * tb < tq else None, b * tb, True)
            dq0, dq1 = dq_scr[0, :, 0:tq], dq_scr[1, :, 0:tq]
            dqkv_ref[0, pl.ds(q0, tq), :] = (_pair_lanes(dq0, dq1) * 0.125).astype(BF16)
            dc_ref[pl.ds(first_head, 1), pl.ds(q0, tq)] = dq0[K_ONES:K_ONES + 1, :]
            dc_ref[pl.ds(first_head + 1, 1), pl.ds(q0, tq)] = dq1[K_ONES:K_ONES + 1, :]

        q_tile(0, tb, None)

        def big_tile(i, _):
            q_tile(pl.multiple_of(tb + i * tq_big, tb), tq_big, (ATT_Q_BLOCKS_BWD // 2) * i)
            return 0

        lax.fori_loop(0, n_big, big_tile, 0)

        def finish(bi, _):
            r0 = pl.multiple_of(bi * tb, tb)
            dk0, dk1 = dk_acc[0, :, pl.ds(r0, tb)], dk_acc[1, :, pl.ds(r0, tb)]
            dqkv_ref[1, pl.ds(r0, tb), :] = _pair_lanes(dk0, dk1).astype(BF16)
            dqkv_ref[2, pl.ds(r0, tb), :] = _pair_lanes(dv_acc[0, :, pl.ds(r0, tb)],
                                                        dv_acc[1, :, pl.ds(r0, tb)]).astype(BF16)
            dc_ref[pl.ds(first_head, 1), pl.ds(r0, tb)] -= dk0[Q_ONES:Q_ONES + 1, :]
            dc_ref[pl.ds(first_head + 1, 1), pl.ds(r0, tb)] -= dk1[Q_ONES:Q_ONES + 1, :]
            return 0

        lax.fori_loop(0, nb, finish, 0)
        finish_exchange()

    once = pl.Buffered(1)
    pair = pl.BlockSpec((lp, LANES), lambda hp: (0, hp))
    pair_in = pl.BlockSpec((lp, LANES), lambda hp: (0, hp), pipeline_mode=once)
    heads = pl.BlockSpec((2, lp, LANES), lambda hp: (hp, 0, 0), pipeline_mode=once)
    heads_t = pl.BlockSpec((2, LANES, lp), lambda hp: (hp, 0, 0), pipeline_mode=once)
    hbm = pl.BlockSpec(memory_space=pl.ANY)
    return pl.pallas_call(
        body, name="attention_backward", grid=(N_HEADS // 2,),
        out_shape=[jax.ShapeDtypeStruct((3, lp, ATTN_WIDTH), BF16), jax.ShapeDtypeStruct((N_HEADS, lp), F32)]
        + _exchange_results(behind),
        in_specs=[heads_t, heads, heads_t, heads, pair_in, pair_in, pair_in] + [hbm] * nx,
        out_specs=[pl.BlockSpec((3, lp, LANES), lambda hp: (0, 0, hp)),
                   pl.BlockSpec((N_HEADS, lp), lambda hp: (0, 0))] + [hbm] * nx,
        scratch_shapes=[pltpu.VMEM((2, LANES, lp), BF16), pltpu.VMEM((2, LANES, lp), BF16),
                        pltpu.VMEM((2, LANES, lp), F32), pltpu.VMEM((2, LANES, lp), F32),
                        pltpu.VMEM((2, LANES, tq_big), F32), pltpu.VMEM((2, 2, tb, tq_big), F32)]
        + _exchange_semaphores(nx),
        compiler_params=_params(("arbitrary",)),
    )(qt, k, kt, v, do, o, lse, *[a for _, a, _ in behind])


def _middle(x, target, h, o, u, zp, w_main, w_up_pool, w_up_attn, w_out, pool_w, pool_scale, final_g):
    seq = x.shape[0]
    tm = ROW_TILE
    nt = seq // tm + 1
    lp = nt * tm
    halo_blocks = tm // MAX_WINDOW

    def body(x_ref, t_ref, h_ref, o_ref, u_ref, uh_ref, zp_ref,
             wc_ref, wupp_ref, wupa_ref, wout_ref, pw_ref, sc_ref, gf_ref,
             dh2_ref, mg_ref, yp_ref, ya_ref, dap_ref, daa_ref, do_ref, dmid_ref, dpn_ref,
             loss_ref, dgf_ref, dsc_ref, dpw_ref):
        i = pl.program_id(0)
        tiles = (dh2_ref, mg_ref, yp_ref, ya_ref, dap_ref, daa_ref, do_ref, dmid_ref, dpn_ref)

        @pl.when(i == 0)
        def _():
            for ref in tiles + (loss_ref, dgf_ref, dsc_ref, dpw_ref):
                ref[...] = jnp.zeros_like(ref)

        @pl.when(i > 0)
        def _():
            xt = x_ref[...]
            hb = h_ref[...]
            pc = _dot_nt(hb, wc_ref[...])
            za, gp, ga = pc[:, :512], pc[:, 512:1536], pc[:, 1536:]
            of = o_ref[...].astype(F32)
            sza = _sigmoid(za)
            silu_za = za * sza
            ya = (of * silu_za).astype(BF16)
            ya_ref[...] = ya
            aa = _dot(ya, wupa_ref[...])

            u = u_ref[...]
            zp = zp_ref[...]
            counts = _pool_counts(i * tm, tm)
            ps = _pool_means(jnp.concatenate([uh_ref[...], u], axis=0), u, counts)
            pbs = [p.astype(BF16) for p in ps]
            ppw = jnp.concatenate([_dot(pbs[g], pw_ref[g]) for g in range(4)], axis=1)
            sc = sc_ref[...]
            szp = _sigmoid(zp)
            silu_zp = zp * szp
            ypre = ppw * sc
            yp = (ypre * silu_zp).astype(BF16)
            yp_ref[...] = yp
            ap = _dot(yp, wupp_ref[...])

            sgp, sga = _sigmoid(gp), _sigmoid(ga)
            mg = (sgp * ap + sga * aa).astype(BF16)
            mg_ref[...] = mg
            h2 = xt + _dot(mg, wout_ref[...])
            r2 = lax.rsqrt(jnp.mean(h2 * h2, axis=-1, keepdims=True) + RMS_EPS)
            h2n = h2 * r2
            gf = gf_ref[...]
            diff = h2n * gf - t_ref[...]
            loss_ref[...] += 0.5 * jnp.sum(jnp.mean(diff * diff, axis=-1, keepdims=True), axis=0, keepdims=True)
            dy = diff * (1.0 / D_MODEL)
            dgf_ref[...] += jnp.sum(dy * h2n, axis=0, keepdims=True)
            dyg = dy * gf
            dh2 = r2 * (dyg - h2n * jnp.mean(dyg * h2n, axis=-1, keepdims=True))
            dh2_ref[...] = dh2
            dmg = _dot_nt(dh2.astype(BF16), wout_ref[...])
            dap = (dmg * sgp).astype(BF16)
            daa = (dmg * sga).astype(BF16)
            dap_ref[...] = dap
            daa_ref[...] = daa
            dmid_ref[:, MID_GP:MID_GA] = (dmg * ap * sgp * (1.0 - sgp)).astype(BF16)
            dmid_ref[:, MID_GA:] = (dmg * aa * sga * (1.0 - sga)).astype(BF16)
            dyp = _dot_nt(dap, wupp_ref[...])
            dya = _dot_nt(daa, wupa_ref[...])
            do_ref[...] = (dya * silu_za).astype(BF16)
            dmid_ref[:, MID_ZA:MID_GP] = (dya * of * (sza * (1.0 + za * (1.0 - sza)))).astype(BF16)

            dypre = dyp * silu_zp
            dmid_ref[:, :MID_ZA] = (dyp * ypre * (szp * (1.0 + zp * (1.0 - szp)))).astype(BF16)
            dsc_ref[...] += jnp.sum(dypre * ppw, axis=0, keepdims=True)
            dppw = (dypre * sc).astype(BF16)
            dpns = []
            for g in range(4):
                dg = dppw[:, POOL_GROUP * g:POOL_GROUP * (g + 1)]
                dpw_ref[g] += _dot_tn(pbs[g], dg)
                dpns.append(_dot_nt(dg, pw_ref[g]) / counts[g])
            dpn_ref[...] = jnp.concatenate(dpns, axis=1)

    real = lambda w: pl.BlockSpec((tm, w), lambda i: (jnp.maximum(i - 1, 0), 0))
    row = lambda w: pl.BlockSpec((tm, w), lambda i: (i, 0))
    in_specs = [
        real(D_MODEL), real(D_MODEL), row(D_MODEL), row(512), row(512),
        pl.BlockSpec((MAX_WINDOW, 512), lambda i: (jnp.maximum(i * halo_blocks - 1, 0), 0)), row(512),
        _const((2560, D_MODEL), (1, 0)), _const((POOL_WIDTH, D_MODEL)), _const((ATTN_WIDTH, D_MODEL)),
        _const((D_MODEL, D_MODEL)), _const((4, POOL_GROUP, POOL_GROUP)), _const((1, POOL_WIDTH)), _const((1, D_MODEL)),
    ]
    sd = jax.ShapeDtypeStruct
    out_shape = [
        sd((lp, D_MODEL), F32),
        sd((lp, D_MODEL), BF16),
        sd((lp, 512), BF16),
        sd((lp, 512), BF16),
        sd((lp, D_MODEL), BF16),
        sd((lp, D_MODEL), BF16),
        sd((lp, 512), BF16),
        sd((lp, MID_WIDTH), BF16),
        sd((lp, 512), F32),
        sd((1, LANES), F32),
        sd((1, D_MODEL), F32),
        sd((1, 512), F32),
        sd((4, POOL_GROUP, POOL_GROUP), F32),
    ]
    keep = lambda shape: pl.BlockSpec(shape, lambda i: (0,) * len(shape))
    out_specs = [row(D_MODEL), row(D_MODEL), row(512), row(512), row(D_MODEL), row(D_MODEL), row(512),
                 row(MID_WIDTH), row(512),
                 keep((1, LANES)), keep((1, D_MODEL)), keep((1, 512)), keep((4, POOL_GROUP, POOL_GROUP))]
    return pl.pallas_call(
        body, name="middle", grid=(nt,), out_shape=out_shape, in_specs=in_specs, out_specs=out_specs,
        compiler_params=_params(("arbitrary",)),
    )(x, target, h, o, u, u, zp, w_main, w_up_pool, w_up_attn, w_out, pool_w, pool_scale, final_g)


DUF_WIDTH = POOL_WIDTH + LANES


def _sequence_grads(dpn, dc, sneg):
    lp = dpn.shape[0]
    tm = ROW_TILE
    nt = lp // tm
    halo_blocks = tm // MAX_WINDOW
    last_halo = lp // MAX_WINDOW - 1

    def body(dpn_ref, dpnh_ref, dc_ref, sn_ref, duf_ref, dbf_ref, carry_ref):
        i = pl.program_id(0)
        t = nt - 1 - i

        @pl.when(i == 0)
        def _():
            carry_ref[...] = jnp.zeros_like(carry_ref)
            dbf_ref[...] = jnp.zeros_like(dbf_ref)

        dpn_t = dpn_ref[...]
        ahead = jnp.where(i == 0, jnp.zeros_like(dpnh_ref), dpnh_ref[...])
        ext = jnp.concatenate([dpn_t, ahead], axis=0)
        counts = _pool_counts(t * tm, tm)
        for g, w in enumerate(POOL_WINDOWS):
            s = ext[:, POOL_GROUP * g:POOL_GROUP * (g + 1)]
            sh = 1
            while sh < w:
                s = s + pltpu.roll(s, tm + MAX_WINDOW - sh, axis=0)
                sh *= 2
            du = s[:tm, :] - dpn_t[:, POOL_GROUP * g:POOL_GROUP * (g + 1)] * counts[g]
            duf_ref[:, POOL_GROUP * g:POOL_GROUP * (g + 1)] = du.astype(BF16)

        dct = jnp.concatenate([dc_ref[...], jnp.zeros((LANES - N_HEADS, tm), F32)], axis=0).T
        rloc = lax.broadcasted_iota(jnp.int32, (tm, LANES), 0)
        sh = 1
        while sh < tm:
            dct = dct + jnp.where(rloc + sh < tm, pltpu.roll(dct, tm - sh, axis=0), 0.0)
            sh *= 2
        dct = dct + carry_ref[...]
        carry_ref[...] = dct[0:1, :]
        df = dct * sn_ref[...]
        dbf_ref[...] += jnp.sum(df, axis=0, keepdims=True)
        duf_ref[:, POOL_WIDTH:] = df.astype(BF16)

    rev = lambda w: pl.BlockSpec((tm, w), lambda i: (nt - 1 - i, 0))
    return pl.pallas_call(
        body, name="sequence_grads", grid=(nt,),
        out_shape=[jax.ShapeDtypeStruct((lp, DUF_WIDTH), BF16), jax.ShapeDtypeStruct((1, LANES), F32)],
        in_specs=[rev(512),
                  pl.BlockSpec((MAX_WINDOW, 512), lambda i: (jnp.minimum((nt - i) * halo_blocks, last_halo), 0)),
                  pl.BlockSpec((N_HEADS, tm), lambda i: (0, nt - 1 - i)), rev(LANES)],
        out_specs=[rev(DUF_WIDTH), pl.BlockSpec((1, LANES), lambda i: (0, 0))],
        scratch_shapes=[pltpu.VMEM((1, LANES), F32)],
        compiler_params=_params(("arbitrary",)),
    )(dpn, dpn, dc, sneg)


def _backward_in(x, tile0, norm_g, dh2, duf, dqkv, dmid, w_main, w_f, behind):
    seq = x.shape[0]
    tm = ROW_TILE
    nt = seq // tm + 1
    nx = len(behind)

    def body(x_ref, t0_ref, g_ref, dh2_ref, du_ref, df_ref, dqkv_ref, dzp_ref, dza_ref, dgp_ref, dga_ref,
             wm_ref, wf_ref, *rest):
        gx_ref, gmeta_ref, dg_ref = rest[nx:nx + 3]
        dproj_ref = rest[2 * nx + 3]
        finish_exchange = _behind(behind, rest[:nx], rest[nx + 3:2 * nx + 3], rest[2 * nx + 4:])
        t = pl.program_id(0)

        @pl.when(t == 0)
        def _():
            dg_ref[...] = jnp.zeros_like(dg_ref)

        dproj_ref[:, 0:512] = du_ref[...]
        dproj_ref[:, 512:1024] = dzp_ref[...]
        dproj_ref[:, 1024:1536] = dqkv_ref[0]
        dproj_ref[:, 1536:2048] = dqkv_ref[1]
        dproj_ref[:, 2048:2560] = dqkv_ref[2]
        dproj_ref[:, 2560:3072] = dza_ref[...]
        dproj_ref[:, 3072:4096] = dgp_ref[...]
        dproj_ref[:, 4096:5120] = dga_ref[...]
        dh = _dot(dproj_ref[...], wm_ref[...]) + _dot(df_ref[...], wf_ref[...])
        xt = jnp.where(t == 0, t0_ref[...], x_ref[...])
        r = lax.rsqrt(jnp.mean(xt * xt, axis=-1, keepdims=True) + RMS_EPS)
        xn = xt * r
        dg_ref[...] += jnp.sum(dh * xn, axis=0, keepdims=True)
        dhg = dh * g_ref[...]
        dx = dh2_ref[...] + r * (dhg - xn * jnp.mean(dhg * xn, axis=-1, keepdims=True))

        @pl.when(t > 0)
        def _():
            gx_ref[...] = dx

        @pl.when(t == 0)
        def _():
            gmeta_ref[...] = dx[PAD:, :]
            gx_ref[...] = jnp.zeros_like(gx_ref)

        finish_exchange()

    row = lambda w, j=0: pl.BlockSpec((tm, w), lambda i: (i, j))
    real = pl.BlockSpec((tm, D_MODEL), lambda i: (jnp.maximum(i - 1, 0), 0))
    hbm = pl.BlockSpec(memory_space=pl.ANY)
    in_specs = [
        real, _const((tm, D_MODEL)), _const((1, D_MODEL)), row(D_MODEL),
        row(POOL_WIDTH), row(LANES, POOL_WIDTH // LANES), pl.BlockSpec((3, tm, ATTN_WIDTH), lambda i: (0, i, 0)),
        row(512, 0), row(512, 1), row(1024, 1), row(1024, 2),
        _const((N_MAIN, D_MODEL)), _const((LANES, D_MODEL)),
    ] + [hbm] * nx
    sd = jax.ShapeDtypeStruct
    out_shape = [sd((seq, D_MODEL), F32), sd((N_META, D_MODEL), F32), sd((1, D_MODEL), F32)] + _exchange_results(behind)
    keep = lambda shape: pl.BlockSpec(shape, lambda i: (0,) * len(shape))
    out_specs = [real, keep((N_META, D_MODEL)), keep((1, D_MODEL))] + [hbm] * nx
    return pl.pallas_call(
        body, name="backward_in", grid=(nt,), out_shape=out_shape, in_specs=in_specs, out_specs=out_specs,
        scratch_shapes=[pltpu.VMEM((tm, N_MAIN), BF16)] + _exchange_semaphores(nx),
        compiler_params=_params(("arbitrary",)),
    )(x, tile0, norm_g, dh2, duf, duf, dqkv, dmid, dmid, dmid, dmid, w_main, w_f, *[a for _, a, _ in behind])


def _matmul_tn(name, a, b, tn):
    lp, m = a.shape
    n = b.shape[1]

    def body(a_ref, b_ref, c_ref):
        c_ref[...] = _dot_tn(a_ref[...].astype(BF16), b_ref[...].astype(BF16))

    return pl.pallas_call(
        body, name=name, grid=(n // tn,), out_shape=jax.ShapeDtypeStruct((m, n), F32),
        in_specs=[_const((lp, m)), pl.BlockSpec((lp, tn), lambda j: (0, j))],
        out_specs=pl.BlockSpec((m, tn), lambda j: (0, j)),
        compiler_params=_params(("arbitrary",)),
    )(a, b)


def _matmul_tn_rows(name, a, b, tm):
    lp, m = a.shape
    n = b.shape[1]

    def body(a_ref, b_ref, c_ref):
        c_ref[...] = _dot_tn(a_ref[...].astype(BF16), b_ref[...].astype(BF16))

    return pl.pallas_call(
        body, name=name, grid=(m // tm,), out_shape=jax.ShapeDtypeStruct((m, n), F32),
        in_specs=[pl.BlockSpec((lp, tm), lambda j: (0, j)), _const((lp, n))],
        out_specs=pl.BlockSpec((tm, n), lambda j: (j, 0)),
        compiler_params=_params(("arbitrary",)),
    )(a, b)


def _matmul_tn_stack(name, a, b):
    n_blocks, lp, m = a.shape
    n = b.shape[1]

    def body(a_ref, b_ref, c_ref):
        c_ref[...] = _dot_tn(a_ref[...], b_ref[...])

    return pl.pallas_call(
        body, name=name, grid=(n_blocks,), out_shape=jax.ShapeDtypeStruct((n_blocks * m, n), F32),
        in_specs=[pl.BlockSpec((None, lp, m), lambda j: (j, 0, 0)), _const((lp, n))],
        out_specs=pl.BlockSpec((m, n), lambda j: (j, 0)),
        compiler_params=_params(("arbitrary",)),
    )(a, b)


def _adamw_step(p_ref, w_ref, m_ref, v_ref, g_ref, d_ref, mo_ref, vo_ref):
    g = p_ref[0].astype(F32)
    for s in range(1, p_ref.shape[0]):
        g = g + p_ref[s].astype(F32)
    m_new = ADAM_B1 * m_ref[...] + (1.0 - ADAM_B1) * g
    v_new = ADAM_B2 * v_ref[...] + (1.0 - ADAM_B2) * (g * g)
    m_hat = m_new / (1.0 - ADAM_B1 ** ADAM_STEP)
    v_hat = v_new / (1.0 - ADAM_B2 ** ADAM_STEP)
    g_ref[...] = g
    d_ref[...] = -ADAM_LR * (m_hat / (jnp.sqrt(v_hat) + ADAM_EPS) + ADAM_WD * w_ref[...])
    mo_ref[...] = m_new
    vo_ref[...] = v_new


def _adamw_small(name, groups, loss_parts):
    n = len(groups)

    def body(*refs):
        ins, outs = refs[:4 * n + 1], refs[4 * n + 1:]
        for j in range(n):
            _adamw_step(*ins[4 * j:4 * j + 4], *outs[4 * j:4 * j + 4])
        total = ins[-1][0]
        for s in range(1, N_DEV):
            total = total + ins[-1][s]
        outs[-1][...] = total

    vmem = pl.BlockSpec(memory_space=pltpu.VMEM)
    out_shape = [jax.ShapeDtypeStruct(w.shape, F32) for _, w, _, _ in groups for _ in range(4)]
    out_shape.append(jax.ShapeDtypeStruct(loss_parts.shape[1:], F32))
    res = pl.pallas_call(
        body, name=name, out_shape=out_shape, in_specs=[vmem] * (4 * n + 1), out_specs=[vmem] * (4 * n + 1),
        compiler_params=_params(),
    )(*[a for g in groups for a in g], loss_parts)
    return [res[4 * j:4 * j + 4] for j in range(n)], res[-1]


def _adamw(name, parts, w, m, v, rows, cols=None):
    r, c_all = w.shape
    c = cols or c_all
    n_parts = parts.shape[0]

    def body(p_ref, w_ref, m_ref, v_ref, g_ref, d_ref, mo_ref, vo_ref):
        _adamw_step(p_ref, w_ref, m_ref, v_ref, g_ref, d_ref, mo_ref, vo_ref)

    blk = pl.BlockSpec((rows, c), lambda i, j: (i, j))
    return pl.pallas_call(
        body, name=name, grid=(r // rows, c_all // c), out_shape=[jax.ShapeDtypeStruct((r, c_all), F32)] * 4,
        in_specs=[pl.BlockSpec((n_parts, rows, c), lambda i, j: (0, i, j)), blk, blk, blk],
        out_specs=[blk] * 4,
        compiler_params=_params(("arbitrary", "arbitrary")),
    )(parts, w, m, v)


def _pair_sum(name, mine, theirs, rows):
    n, r, c = mine.shape

    def body(a_ref, b_ref, o_ref):
        o_ref[...] = (a_ref[...].astype(F32) + b_ref[...].astype(F32)).astype(BF16)

    blk = pl.BlockSpec((1, rows, c), lambda j, i: (j, i, 0))
    return pl.pallas_call(
        body, name=name, grid=(n, r // rows), out_shape=jax.ShapeDtypeStruct((n, r, c), BF16),
        in_specs=[blk, blk], out_specs=blk,
        compiler_params=_params(("arbitrary", "arbitrary")),
    )(mine, theirs)


def _by_core(slots):
    by_core = slots.reshape((4, 2) + slots.shape[1:]).swapaxes(0, 1)
    c = lax.axis_index("c")
    return (lax.dynamic_index_in_dim(by_core, c, 0, keepdims=False),
            lax.dynamic_index_in_dim(by_core, 1 - c, 0, keepdims=False))


def _columns_to_slots(a):
    r, c8 = a.shape
    return a.reshape(r, N_DEV, c8 // N_DEV).transpose(1, 0, 2)


def _slots_to_columns(a):
    n, r, c = a.shape
    return a.transpose(1, 0, 2).reshape(r, n * c)


def kernel(x, meta_tokens, norm_g, w_in, b_forget, pool_w, pool_scale, w_up_pool, w_up_attn, w_out, final_norm_g, loss_target, m_meta_tokens, m_norm_g, m_w_in, m_b_forget, m_pool_w, m_pool_scale, m_w_up_pool, m_w_up_attn, m_w_out, m_final_norm_g, v_meta_tokens, v_norm_g, v_w_in, v_b_forget, v_pool_w, v_pool_scale, v_w_up_pool, v_w_up_attn, v_w_out, v_final_norm_g):
    xs = x[0]
    target = loss_target[0]

    g_in, g_meta = _gather_two_level("gather_weights", [w_in[0].T.astype(BF16), meta_tokens], (320, 8))
    w_full = g_in.reshape(N_DEV * g_in.shape[1], D_MODEL)
    w_main = jnp.concatenate([w_full[:N_BEFORE_F], w_full[N_BEFORE_F + N_HEADS:]], axis=0)
    w_f = jnp.pad(w_full[N_BEFORE_F:N_BEFORE_F + N_HEADS], ((0, LANES - N_HEADS), (0, 0)))
    meta = _slots_to_columns(g_meta)
    tile0 = jnp.concatenate([jnp.zeros((PAD, D_MODEL), F32), meta], axis=0)
    b_f = jnp.pad(b_forget, ((0, 0), (0, LANES - N_HEADS)))
    pw_b = pool_w[0].astype(BF16)
    final_g = final_norm_g.reshape(1, D_MODEL)

    h, u, zp, k, v, qt, kt, vt, sneg = _forward_in(xs, tile0, norm_g, w_main, w_f, b_f)
    o, lse, g_upp, g_upa, g_out = _attention_forward(
        qt, k, vt, [("gather", w.astype(BF16), ALL_PEERS) for w in (w_up_pool[0], w_up_attn[0], w_out[0])])
    wupp = _slots_to_columns(g_upp)
    wupa = _slots_to_columns(g_upa)
    wout = g_out.reshape(D_MODEL, D_MODEL)
    (dh2, mg, yp, ya, dap, daa, do, dmid, dpn,
     loss_part, d_final_g, d_scale, d_pool_w) = _middle(xs, target, h, o, u, zp, w_main, wupp, wupa, wout,
                                                        pw_b, pool_scale, final_g)
    dw_out = _matmul_tn("grad_w_out", mg, dh2, 256)
    dw_upp = _matmul_tn("grad_w_up_pool", yp, dap, 512)
    dw_upa = _matmul_tn("grad_w_up_attn", ya, daa, 512)
    dqkv, dc, p_upp, p_upa, p_out, p_pool_w, p_scale, p_final_g = _attention_backward(
        qt, k, kt, v, do, o, lse,
        [("scatter", _columns_to_slots(dw_upp).astype(BF16), ALL_PEERS),
         ("scatter", _columns_to_slots(dw_upa).astype(BF16), ALL_PEERS),
         ("scatter", dw_out.reshape(N_DEV, D_MODEL // N_DEV, D_MODEL).astype(BF16), ALL_PEERS),
         ("gather", d_pool_w.reshape(4 * POOL_GROUP, POOL_GROUP), ALL_PEERS),
         ("gather", d_scale, ALL_PEERS), ("gather", d_final_g, ALL_PEERS)])
    duf, d_bf = _sequence_grads(dpn, dc, sneg)
    g_uf = _matmul_tn_rows("grad_w_in_pool_forget", duf, h, DUF_WIDTH)
    g_qkv = _matmul_tn_stack("grad_w_in_attention", dqkv, h)
    g_mid = _matmul_tn_rows("grad_w_in_gates", dmid, h, 512)
    dw_in = jnp.concatenate([g_uf[:POOL_WIDTH], g_mid[:MID_ZA], g_qkv, g_mid[MID_ZA:MID_GP],
                             g_uf[POOL_WIDTH:POOL_WIDTH + N_HEADS], g_mid[MID_GP:]], axis=0)
    dw_in = dw_in.reshape(N_DEV, dw_in.shape[0] // N_DEV, D_MODEL)
    mine, for_sibling = _by_core(dw_in)
    from_sibling, = _exchange("swap_with_sibling", [("swap", for_sibling.astype(BF16), (SIBLING,))])
    pair_sums = _pair_sum("pair_sum", mine, from_sibling, dw_in.shape[1])
    grad_x, d_meta, d_norm_g, p_in, p_bf, p_loss = _backward_in(
        xs, tile0, norm_g, dh2, duf, dqkv, dmid, w_main, w_f,
        [("chips", pair_sums, SAME_CORE), ("gather", d_bf, ALL_PEERS), ("gather", loss_part, ALL_PEERS)])
    p_meta, p_norm_g = _exchange(
        "exchange_gradients", [("scatter", _columns_to_slots(d_meta), ALL_PEERS), ("gather", d_norm_g, ALL_PEERS)])


    def pad_f(a):
        return jnp.pad(a, ((0, 0), (0, LANES - N_HEADS)))

    res = {}
    res["w_in"] = [a.T for a in _adamw("adamw_w_in", p_in, w_in[0].T, m_w_in[0].T, v_w_in[0].T, p_in.shape[1], 256)]
    res["w_up_pool"] = _adamw("adamw_w_up_pool", p_upp, w_up_pool[0], m_w_up_pool[0], v_w_up_pool[0], 512)
    res["w_up_attn"] = _adamw("adamw_w_up_attn", p_upa, w_up_attn[0], m_w_up_attn[0], v_w_up_attn[0], 512)
    res["w_out"] = _adamw("adamw_w_out", p_out, w_out[0], m_w_out[0], v_w_out[0], 128)
    flat = lambda a: a.reshape(4 * POOL_GROUP, POOL_GROUP)
    row = lambda a: a.reshape(1, D_MODEL)
    small, loss_row = _adamw_small(
        "adamw_small",
        [(p_meta, meta_tokens, m_meta_tokens, v_meta_tokens),
         (p_norm_g, norm_g, m_norm_g, v_norm_g),
         (p_bf, pad_f(b_forget), pad_f(m_b_forget), pad_f(v_b_forget)),
         (p_pool_w, flat(pool_w), flat(m_pool_w), flat(v_pool_w)),
         (p_scale, pool_scale, m_pool_scale, v_pool_scale),
         (p_final_g, final_g, row(m_final_norm_g), row(v_final_norm_g))],
        p_loss)
    res["meta_tokens"], res["norm_g"], bf, pw, res["pool_scale"], fg = small
    res["b_forget"] = [a[:, :N_HEADS] for a in bf]
    res["pool_w"] = [a.reshape(pool_w.shape) for a in pw]
    res["final_norm_g"] = [a.reshape(D_MODEL) for a in fg]
    loss = loss_row[0, 0]
    for name in ("w_in", "w_up_pool", "w_up_attn", "w_out"):
        res[name] = [a[None] for a in res[name]]

    order = ["meta_tokens", "norm_g", "w_in", "b_forget", "pool_w", "pool_scale", "w_up_pool", "w_up_attn", "w_out",
             "final_norm_g"]
    outs = [loss, grad_x[None]]
    for part in range(4):
        outs += [res[name][part] for name in order]
    return tuple(outs)
```

```python
import jax
import jax.numpy as jnp
from jax import lax
from jax.experimental import pallas as pl
from jax.experimental.pallas import tpu as pltpu

F32 = jnp.float32
BF16 = jnp.bfloat16

D_MODEL = 1024
N_META = 16
POOL_WIDTH = 512
ATTN_WIDTH = 512
N_HEADS = 8
HEAD_DIM = 64
POOL_WINDOWS = (2, 4, 8, 16)
POOL_GROUP = 128
MAX_WINDOW = 16
RMS_EPS = 1e-6
N_MAIN = 5120
N_BEFORE_F = 3072
N_DEV = 8
LANES = 128

ROW_TILE = 256
ATT_TILE = 256
ATT_Q_BLOCKS_FWD = 8
ATT_Q_BLOCKS_BWD = 4
PAD = ROW_TILE - N_META
FIRST_KEY = PAD // LANES * LANES
VMEM_LIMIT = 56 * 1024 * 1024

ADAM_LR = 0.001
ADAM_B1 = 0.9
ADAM_B2 = 0.999
ADAM_EPS = 1e-08
ADAM_WD = 0.01
ADAM_STEP = 10

MID_ZA, MID_GP, MID_GA, MID_WIDTH = 512, 1024, 2048, 3072
NEG = -1e30
MESH = pl.DeviceIdType.MESH


def _params(sem=None):
    kw = dict(vmem_limit_bytes=VMEM_LIMIT)
    if sem is not None:
        kw["dimension_semantics"] = sem
    return pltpu.CompilerParams(**kw)


def _const(shape, block_index=None):
    idx = block_index or (0,) * len(shape)
    return pl.BlockSpec(shape, lambda i: idx, pipeline_mode=pl.Buffered(1))


def _sigmoid(x):
    return jax.nn.sigmoid(x)


def _dot(a, b):
    return jnp.dot(a, b, preferred_element_type=F32)


def _dot_nt(a, b):
    return lax.dot_general(a, b, (((1,), (1,)), ((), ())), preferred_element_type=F32)


def _dot_tn(a, b):
    return lax.dot_general(a, b, (((0,), (0,)), ((), ())), preferred_element_type=F32)


def _pool_counts(first_row, rows):
    row = first_row + lax.broadcasted_iota(jnp.int32, (rows, 1), 0)
    pos1 = row - PAD + 1
    return [jnp.clip(pos1, 1, w).astype(F32) for w in POOL_WINDOWS]


def _pool_means(u_ext, u, counts):
    rows = u.shape[0]
    out = []
    for g, w in enumerate(POOL_WINDOWS):
        s = u_ext[:, POOL_GROUP * g:POOL_GROUP * (g + 1)]
        sh = 1
        while sh < w:
            s = s + pltpu.roll(s, sh, axis=0)
            sh *= 2
        out.append(s[MAX_WINDOW:MAX_WINDOW + rows, :] / counts[g] - u[:, POOL_GROUP * g:POOL_GROUP * (g + 1)])
    return out


Q_BIAS, Q_ONES, Q_LSE = 64, 67, 70
K_ONES, K_BIAS, K_ONES2 = 64, 67, 70
V_ONES = 64
DO_BIAS = 64


def _lane_ones(lane, ranges):
    hit = None
    for lo, hi in ranges:
        r = (lane >= lo) & (lane < hi)
        hit = r if hit is None else hit | r
    return jnp.where(hit, 1.0, 0.0)


def _put3(base, lane, first, x):
    hi = x.astype(BF16).astype(F32)
    rest = x - hi
    mid = rest.astype(BF16).astype(F32)
    lo = (rest - mid).astype(BF16).astype(F32)
    for j, piece in enumerate((hi, mid, lo)):
        base = jnp.where(lane == first + j, piece, base)
    return base


SIBLING = 1
SAME_CORE = (2, 4, 6)
ALL_PEERS = (1, 2, 3, 4, 5, 6, 7)


def _place():
    return lax.axis_index("x"), lax.axis_index("y"), lax.axis_index("c")


def _peer(r):
    x, y, c = _place()
    return (1 - x if r & 4 else x, 1 - y if r & 2 else y, 1 - c if r & 1 else c)


def _device_slot(p):
    return 4 * p[0] + 2 * p[1] + p[2]


def _chip_slot(p):
    return 2 * p[0] + p[1]


def _exchange(name, items):
    n = len(items)

    def body(*refs):
        copies = _exchange_copies(items, refs[:n], refs[n:2 * n], *refs[2 * n:])
        for cp in copies:
            cp.start()
        for cp in copies:
            cp.wait()

    hbm = pl.BlockSpec(memory_space=pl.ANY)
    return pl.pallas_call(
        body, name=name, out_shape=_exchange_results(items),
        in_specs=[hbm] * n, out_specs=[hbm] * n,
        scratch_shapes=_exchange_semaphores(n),
    )(*[a for _, a, _ in items])


def _exchange_results(items):
    return [jax.ShapeDtypeStruct(((N_DEV,) if kind == "gather" else ()) + a.shape, a.dtype) for kind, a, _ in items]


def _exchange_semaphores(n):
    return [pltpu.SemaphoreType.DMA((n, N_DEV - 1)), pltpu.SemaphoreType.DMA((n, N_DEV - 1)),
            pltpu.SemaphoreType.DMA((n,))]


def _exchange_copies(items, ins, outs, send_sems, recv_sems, local_sems):
    me = _place()
    copies = []
    for a, (kind, _, peers) in enumerate(items):
        slot = _chip_slot if kind == "chips" else _device_slot
        for r in peers:
            peer = _peer(r)
            src = ins[a] if kind in ("swap", "gather") else ins[a].at[slot(peer)]
            dst = outs[a] if kind == "swap" else outs[a].at[slot(me)]
            copies.append(pltpu.make_async_remote_copy(
                src_ref=src, dst_ref=dst, send_sem=send_sems.at[a, r - 1], recv_sem=recv_sems.at[a, r - 1],
                device_id=peer, device_id_type=MESH))
        if kind != "swap":
            src = ins[a] if kind == "gather" else ins[a].at[slot(me)]
            copies.append(pltpu.make_async_copy(src, outs[a].at[slot(me)], local_sems.at[a]))
    return copies


def _gather_two_level(name, arrays, halves):
    n = len(arrays)
    x_flip, y_flip, both = 4, 2, 6
    to_sibling, to_x, to_y, on_over_y, on_over_x, x_to_sibling, y_to_sibling, d_to_sibling = range(8)

    def body(*refs):
        ins, outs = refs[:n], refs[n:2 * n]
        send_sems, recv_sems, local_sems = refs[2 * n:]
        me, sibling = _place(), _peer(SIBLING)
        xn, yn, dg = _peer(x_flip), _peer(y_flip), _peer(both)

        def rows(a, h):
            return pl.ds(0, halves[a]) if h == 0 else pl.ds(halves[a], arrays[a].shape[0] - halves[a])

        def copy(a, h, k, block, to, own=False):
            dst = outs[a].at[_device_slot(block)].at[rows(a, h)]
            return pltpu.make_async_remote_copy(
                src_ref=ins[a].at[rows(a, h)] if own else dst, dst_ref=dst,
                send_sem=send_sems.at[2 * a + h, k], recv_sem=recv_sems.at[2 * a + h, k],
                device_id=to, device_id_type=MESH)

        sends, mine = [], []

        def start(cp):
            cp.start()
            sends.append(cp)

        for a in range(n):
            cp = pltpu.make_async_copy(ins[a], outs[a].at[_device_slot(me)], local_sems.at[a])
            cp.start()
            mine.append(cp)
        for a in range(n):
            start(copy(a, 0, to_x, me, xn, own=True))
            start(copy(a, 1, to_y, me, yn, own=True))
        for a in range(n):
            start(copy(a, 0, to_y, me, yn, own=True))
            start(copy(a, 1, to_x, me, xn, own=True))
        for a in range(n):
            for h in range(2):
                start(copy(a, h, to_sibling, me, sibling, own=True))
        for a in range(n):
            copy(a, 0, to_x, xn, me).wait_recv()
            start(copy(a, 0, on_over_y, xn, yn))
            start(copy(a, 0, x_to_sibling, xn, sibling))
            copy(a, 1, to_y, yn, me).wait_recv()
            start(copy(a, 1, on_over_x, yn, xn))
            start(copy(a, 1, y_to_sibling, yn, sibling))
        for a in range(n):
            copy(a, 0, to_y, yn, me).wait_recv()
            start(copy(a, 0, y_to_sibling, yn, sibling))
            copy(a, 1, to_x, xn, me).wait_recv()
            start(copy(a, 1, x_to_sibling, xn, sibling))
        for a in range(n):
            copy(a, 0, on_over_y, dg, me).wait_recv()
            start(copy(a, 0, d_to_sibling, dg, sibling))
            copy(a, 1, on_over_x, dg, me).wait_recv()
            start(copy(a, 1, d_to_sibling, dg, sibling))
        for a in range(n):
            for h in range(2):
                copy(a, h, to_sibling, sibling, me).wait_recv()
                for k, r in ((x_to_sibling, x_flip), (y_to_sibling, y_flip), (d_to_sibling, both)):
                    copy(a, h, k, _peer(r | SIBLING), me).wait_recv()
        for cp in sends:
            cp.wait_send()
        for cp in mine:
            cp.wait()

    hbm = pl.BlockSpec(memory_space=pl.ANY)
    return pl.pallas_call(
        body, name=name, out_shape=[jax.ShapeDtypeStruct((N_DEV,) + a.shape, a.dtype) for a in arrays],
        in_specs=[hbm] * n, out_specs=[hbm] * n,
        scratch_shapes=[pltpu.SemaphoreType.DMA((2 * n, 8)), pltpu.SemaphoreType.DMA((2 * n, 8)),
                        pltpu.SemaphoreType.DMA((n,))],
    )(*arrays)


def _forward_in(x, tile0, norm_g, w_main, w_f, b_f):
    seq = x.shape[0]
    nt = seq // ROW_TILE + 1
    lp = nt * ROW_TILE
    tm = ROW_TILE

    def body(x_ref, t0_ref, g_ref, wa_ref, wf_ref, bf_ref,
             h_ref, u_ref, zp_ref, k_ref, v_ref, qt_ref, kt_ref, vt_ref, sn_ref, carry_ref):
        i = pl.program_id(0)

        @pl.when(i == 0)
        def _():
            carry_ref[...] = jnp.zeros_like(carry_ref)

        xt = jnp.where(i == 0, t0_ref[...], x_ref[...])
        r = lax.rsqrt(jnp.mean(xt * xt, axis=-1, keepdims=True) + RMS_EPS)
        h = (xt * r * g_ref[...]).astype(BF16)
        h_ref[...] = h
        pa = _dot_nt(h, wa_ref[...])
        u_ref[...] = pa[:, :512]
        zp_ref[...] = pa[:, 512:1024]

        fl = _dot_nt(h, wf_ref[...]) + bf_ref[...]
        row = i * tm + lax.broadcasted_iota(jnp.int32, (tm, LANES), 0)
        rloc = lax.broadcasted_iota(jnp.int32, (tm, LANES), 0)
        lane = lax.broadcasted_iota(jnp.int32, (tm, LANES), 1)
        live = (row >= PAD) & (lane < N_HEADS)
        logf = jnp.minimum(fl, 0.0) - jnp.log1p(jnp.exp(-jnp.abs(fl)))
        cs = jnp.where(live, logf, 0.0)
        sh = 1
        while sh < tm:
            cs = cs + jnp.where(rloc >= sh, pltpu.roll(cs, sh, axis=0), 0.0)
            sh *= 2
        cs = cs + carry_ref[...]
        carry_ref[...] = cs[tm - 1:tm, :]
        sn_ref[...] = jnp.where(live, _sigmoid(-fl), 0.0)

        rows1 = i * tm + lax.broadcasted_iota(jnp.int32, (tm, 1), 0)
        cols1 = i * tm + lax.broadcasted_iota(jnp.int32, (1, tm), 1)
        sub = lax.broadcasted_iota(jnp.int32, (LANES, tm), 0)
        ones_k = _lane_ones(lane, ((K_ONES, K_ONES + 3), (K_ONES2, K_ONES2 + 3)))
        ones_v = _lane_ones(lane, ((V_ONES, V_ONES + 3),))
        ones_q_t = _lane_ones(sub, ((Q_ONES, Q_ONES + 3),))
        ones_k_t = _lane_ones(sub, ((K_ONES, K_ONES + 3), (K_ONES2, K_ONES2 + 3)))
        ones_v_t = _lane_ones(sub, ((V_ONES, V_ONES + 3),))
        cs_t = cs.T
        for hp in range(N_HEADS // 2):
            kp = pa[:, 1536 + LANES * hp:1536 + LANES * (hp + 1)]
            vp = pa[:, 2048 + LANES * hp:2048 + LANES * (hp + 1)]
            qp_t = (pa[:, 1024 + LANES * hp:1024 + LANES * (hp + 1)] * 0.125).T
            kp_t, vp_t = kp.T, vp.T
            for e in range(2):
                head = 2 * hp + e
                if e:
                    kp, vp = pltpu.roll(kp, HEAD_DIM, axis=1), pltpu.roll(vp, HEAD_DIM, axis=1)
                    qp_t, kp_t, vp_t = (pltpu.roll(a, HEAD_DIM, axis=0) for a in (qp_t, kp_t, vp_t))
                c_row = cs_t[head:head + 1, :]
                minus_ck_row = jnp.where(cols1 >= PAD, -c_row, NEG)
                qt_ref[head] = jnp.where(sub < HEAD_DIM, qp_t, _rows3(Q_BIAS, c_row) + ones_q_t).astype(BF16)
                kt_ref[head] = jnp.where(sub < HEAD_DIM, kp_t, _rows3(K_BIAS, minus_ck_row) + ones_k_t).astype(BF16)
                vt_ref[head] = jnp.where(sub < HEAD_DIM, vp_t, ones_v_t).astype(BF16)
                minus_ck = jnp.where(rows1 >= PAD, -cs[:, head:head + 1], NEG)
                k_ref[head] = jnp.where(lane < HEAD_DIM, kp, _put3(ones_k, lane, K_BIAS, minus_ck)).astype(BF16)
                v_ref[head] = jnp.where(lane < HEAD_DIM, vp, ones_v).astype(BF16)

    row_f32 = lambda w: pl.BlockSpec((tm, w), lambda i: (i, 0))
    out_shape = [
        jax.ShapeDtypeStruct((lp, D_MODEL), BF16),
        jax.ShapeDtypeStruct((lp, POOL_WIDTH), F32),
        jax.ShapeDtypeStruct((lp, POOL_WIDTH), F32),
        jax.ShapeDtypeStruct((N_HEADS, lp, LANES), BF16),
        jax.ShapeDtypeStruct((N_HEADS, lp, LANES), BF16),
        jax.ShapeDtypeStruct((N_HEADS, LANES, lp), BF16),
        jax.ShapeDtypeStruct((N_HEADS, LANES, lp), BF16),
        jax.ShapeDtypeStruct((N_HEADS, LANES, lp), BF16),
        jax.ShapeDtypeStruct((lp, LANES), F32),
    ]
    heads = pl.BlockSpec((N_HEADS, tm, LANES), lambda i: (0, i, 0))
    heads_t = pl.BlockSpec((N_HEADS, LANES, tm), lambda i: (0, 0, i))
    out_specs = [row_f32(D_MODEL), row_f32(512), row_f32(512), heads, heads, heads_t, heads_t, heads_t,
                 row_f32(LANES)]
    in_specs = [
        pl.BlockSpec((tm, D_MODEL), lambda i: (jnp.maximum(i - 1, 0), 0)),
        _const((tm, D_MODEL)), _const((1, D_MODEL)),
        _const((2560, D_MODEL)), _const((LANES, D_MODEL)), _const((1, LANES)),
    ]
    return pl.pallas_call(
        body, name="forward_in", grid=(nt,), out_shape=out_shape, in_specs=in_specs, out_specs=out_specs,
        scratch_shapes=[pltpu.VMEM((1, LANES), F32)],
        compiler_params=_params(("arbitrary",)),
    )(x, tile0, norm_g, w_main, w_f, b_f)


def _pair_lanes(a0_t, a1_t):
    return jnp.concatenate([a0_t[:HEAD_DIM], a1_t[:HEAD_DIM]], axis=0).T


def _behind(items, ins, outs, sems):
    step, last = pl.program_id(0), pl.num_programs(0) - 1

    @pl.when(step == 0)
    def _():
        for cp in _exchange_copies(items, ins, outs, *sems):
            cp.start()

    def finish():
        @pl.when(step == last)
        def _():
            for cp in _exchange_copies(items, ins, outs, *sems):
                cp.wait()

    return finish


def _attention_forward(qt, k, vt, behind):
    lp = k.shape[1]
    tk = ATT_TILE
    q_blocks = ATT_Q_BLOCKS_FWD if (lp // tk - 1) % ATT_Q_BLOCKS_FWD == 0 else ATT_Q_BLOCKS_BWD
    tq_big = q_blocks * tk
    n_big = (lp // tk - 1) // q_blocks
    assert lp == tk + n_big * tq_big and q_blocks % 2 == 0
    nx = len(behind)

    def body(qt_ref, k_ref, vt_ref, *rest):
        o_ref, lse_ref = rest[nx:nx + 2]
        s_buf, m_scr, acc_scr = rest[2 * nx + 2:2 * nx + 5]
        finish_exchange = _behind(behind, rest[:nx], rest[nx + 2:2 * nx + 2], rest[2 * nx + 5:])

        def q_tile(q0, tq, pairs):
            first = q0 // tk
            qts = [qt_ref[e, :, pl.ds(q0, tq)] for e in range(2)]

            def block(kj):
                return pl.ds(kj * tk if isinstance(kj, int) else pl.multiple_of(kj * tk, tk), tk)

            def step(kj, rd, wr, c0=0, diagonal=False, keys=None):
                c1 = c0 + tk if diagonal else c0
                keys = block(kj) if keys is None else keys
                for e in range(2):
                    s = s_buf[rd, e, 0:keys.size, c0:tq]
                    if wr is not None:
                        s_buf[wr, e, :, c1:tq] = _dot(k_ref[e, block(kj + 1), :], qts[e][:, c1:tq])
                    if diagonal:
                        key = lax.broadcasted_iota(jnp.int32, s.shape, 0)
                        s = jnp.where(key <= lax.broadcasted_iota(jnp.int32, s.shape, 1), s, NEG)
                    m = m_scr[e, :, c0:tq]
                    m_new = jnp.maximum(m, jnp.max(s, axis=0, keepdims=True))
                    p = jnp.exp(s - m_new)
                    pv = _dot(vt_ref[e, :, keys], p.astype(BF16))
                    acc_scr[e, :, c0:tq] = jnp.exp(m - m_new) * acc_scr[e, :, c0:tq] + pv
                    m_scr[e, :, c0:tq] = m_new

            keys0 = block(0) if pairs is None else pl.ds(FIRST_KEY, tk - FIRST_KEY)
            for e in range(2):
                m_scr[e, :, 0:tq] = jnp.full((1, tq), NEG, F32)
                acc_scr[e, :, 0:tq] = jnp.zeros((LANES, tq), F32)
                s_buf[0, e, 0:keys0.size, 0:tq] = _dot(k_ref[e, keys0, :], qts[e])
            if pairs is None:
                step(0, 0, None, 0, True)
            else:
                step(0, 0, 1, keys=keys0)

                def two_steps(t, _):
                    step(1 + 2 * t, 1, 0)
                    step(2 + 2 * t, 0, 1)
                    return 0

                lax.fori_loop(0, pairs, two_steps, 0)
                for b in range(tq // tk):
                    step(first + b, (b + 1) % 2, b % 2 if (b + 1) * tk < tq else None, b * tk, True)
            outs, lses = [], []
            for e in range(2):
                acc = acc_scr[e, :, 0:tq]
                l = acc[V_ONES:V_ONES + 1, :]
                outs.append(acc / l)
                lses.append(m_scr[e, :, 0:tq] + jnp.log(l))
            o_ref[pl.ds(q0, tq), :] = _pair_lanes(outs[0], outs[1]).astype(BF16)
            for e in range(2):
                lse_ref[pl.ds(2 * pl.program_id(0) + e, 1), pl.ds(q0, tq)] = lses[e]

        q_tile(0, tk, None)

        def big_tile(i, _):
            q_tile(pl.multiple_of(tk + i * tq_big, tk), tq_big, (q_blocks // 2) * i)
            return 0

        lax.fori_loop(0, n_big, big_tile, 0)
        finish_exchange()

    pair = pl.BlockSpec((lp, LANES), lambda hp: (0, hp))
    heads = pl.BlockSpec((2, lp, LANES), lambda hp: (hp, 0, 0), pipeline_mode=pl.Buffered(1))
    heads_t = pl.BlockSpec((2, LANES, lp), lambda hp: (hp, 0, 0), pipeline_mode=pl.Buffered(1))
    hbm = pl.BlockSpec(memory_space=pl.ANY)
    return pl.pallas_call(
        body, name="attention_forward", grid=(N_HEADS // 2,),
        out_shape=[jax.ShapeDtypeStruct((lp, ATTN_WIDTH), BF16), jax.ShapeDtypeStruct((N_HEADS, lp), F32)]
        + _exchange_results(behind),
        in_specs=[heads_t, heads, heads_t] + [hbm] * nx,
        out_specs=[pair, pl.BlockSpec((N_HEADS, lp), lambda hp: (0, 0))] + [hbm] * nx,
        scratch_shapes=[pltpu.VMEM((2, 2, tk, tq_big), F32), pltpu.VMEM((2, 1, tq_big), F32),
                        pltpu.VMEM((2, LANES, tq_big), F32)] + _exchange_semaphores(nx),
        compiler_params=_params(("arbitrary",)),
    )(qt, k, vt, *[a for _, a, _ in behind])


def _rows3(first, x):
    sub = lax.broadcasted_iota(jnp.int32, (LANES, x.shape[1]), 0)
    hi = x.astype(BF16).astype(F32)
    rest = x - hi
    mid = rest.astype(BF16).astype(F32)
    lo = (rest - mid).astype(BF16).astype(F32)
    out = jnp.zeros((LANES, x.shape[1]), F32)
    for j, piece in enumerate((hi, mid, lo)):
        out = jnp.where(sub == first + j, piece, out)
    return out


def _attention_backward(qt, k, kt, v, do, o, lse, behind):
    lp = k.shape[1]
    tb = ATT_TILE
    nb = lp // tb
    tq_big = ATT_Q_BLOCKS_BWD * tb
    n_big = (nb - 1) // ATT_Q_BLOCKS_BWD
    assert lp == tb + n_big * tq_big and ATT_Q_BLOCKS_BWD % 2 == 0
    nx = len(behind)

    def body(qt_ref, k_ref, kt_ref, v_ref, do_ref, o_ref, lse_ref, *rest):
        dqkv_ref, dc_ref = rest[nx:nx + 2]
        q2_ref, do2_ref, dk_acc, dv_acc, dq_scr, s_buf = rest[2 * nx + 2:2 * nx + 8]
        finish_exchange = _behind(behind, rest[:nx], rest[nx + 2:2 * nx + 2], rest[2 * nx + 8:])
        sub = lax.broadcasted_iota(jnp.int32, (LANES, tb), 0)
        first_head = 2 * pl.program_id(0)

        def prepare(bi, _):
            r0 = pl.multiple_of(bi * tb, tb)
            queries = r0 + lax.broadcasted_iota(jnp.int32, (1, tb), 1)
            dob = do_ref[pl.ds(r0, tb), :].astype(F32)
            do_t = dob.T
            dd_t = (dob * o_ref[pl.ds(r0, tb), :].astype(F32)).T
            for e in range(2):
                delta = jnp.sum(dd_t[HEAD_DIM * e:HEAD_DIM * (e + 1), :], axis=0, keepdims=True)
                do_e = jnp.concatenate([do_t[HEAD_DIM * e:HEAD_DIM * (e + 1), :], jnp.zeros((HEAD_DIM, tb), F32)], axis=0)
                do2_ref[e, :, pl.ds(r0, tb)] = jnp.where(sub < HEAD_DIM, do_e, _rows3(DO_BIAS, -delta)).astype(BF16)
                minus_lse = jnp.where(queries >= PAD, -lse_ref[pl.ds(first_head + e, 1), pl.ds(r0, tb)], NEG)
                keep = (sub < Q_LSE) | (sub >= Q_LSE + 3)
                q2_ref[e, :, pl.ds(r0, tb)] = jnp.where(keep, qt_ref[e, :, pl.ds(r0, tb)].astype(F32),
                                                        _rows3(Q_LSE, minus_lse)).astype(BF16)
            return 0

        lax.fori_loop(0, nb, prepare, 0)
        dk_acc[...] = jnp.zeros_like(dk_acc)
        dv_acc[...] = jnp.zeros_like(dv_acc)

        def q_tile(q0, tq, pairs):
            first = q0 // tb
            qts = [q2_ref[e, :, pl.ds(q0, tq)] for e in range(2)]
            dots = [do2_ref[e, :, pl.ds(q0, tq)] for e in range(2)]

            def block(kj):
                return pl.ds(kj * tb if isinstance(kj, int) else pl.multiple_of(kj * tb, tb), tb)

            def step(kj, rd, wr, c0=0, diagonal=False, keys=None):
                c1 = c0 + tb if diagonal else c0
                keys = block(kj) if keys is None else keys
                for e in range(2):
                    s = s_buf[rd, e, 0:keys.size, c0:tq]
                    if wr is not None:
                        s_buf[wr, e, :, c1:tq] = _dot(k_ref[e, block(kj + 1), :], qts[e][:, c1:tq])
                    dpd = _dot(v_ref[e, keys, :], dots[e][:, c0:tq])
                    p = jnp.exp(s)
                    if diagonal:
                        key = lax.broadcasted_iota(jnp.int32, s.shape, 0)
                        p = jnp.where(key <= lax.broadcasted_iota(jnp.int32, s.shape, 1), p, 0.0)
                    dsb = (p * dpd).astype(BF16)
                    dv_acc[e, :, keys] += _dot_nt(dots[e][:, c0:tq], p.astype(BF16))
                    dk_acc[e, :, keys] += _dot_nt(qts[e][:, c0:tq], dsb)
                    dq_scr[e, :, c0:tq] += _dot(kt_ref[e, :, keys], dsb)

            keys0 = block(0) if pairs is None else pl.ds(FIRST_KEY, tb - FIRST_KEY)
            for e in range(2):
                dq_scr[e, :, 0:tq] = jnp.zeros((LANES, tq), F32)
                s_buf[0, e, 0:keys0.size, 0:tq] = _dot(k_ref[e, keys0, :], qts[e])
            if pairs is None:
                step(0, 0, None, 0, True)
            else:
                step(0, 0, 1, keys=keys0)

                def two_steps(t, _):
                    step(1 + 2 * t, 1, 0)
                    step(2 + 2 * t, 0, 1)
                    return 0

                lax.fori_loop(0, pairs, two_steps, 0)
                for b in range(tq // tb):
                    step(first + b, (b + 1) % 2, b % 2 if (b + 1) * tb < tq else None, b * tb, True)
            dq0, dq1 = dq_scr[0, :, 0:tq], dq_scr[1, :, 0:tq]
            dqkv_ref[0, pl.ds(q0, tq), :] = (_pair_lanes(dq0, dq1) * 0.125).astype(BF16)
            dc_ref[pl.ds(first_head, 1), pl.ds(q0, tq)] = dq0[K_ONES:K_ONES + 1, :]
            dc_ref[pl.ds(first_head + 1, 1), pl.ds(q0, tq)] = dq1[K_ONES:K_ONES + 1, :]

        q_tile(0, tb, None)

        def big_tile(i, _):
            q_tile(pl.multiple_of(tb + i * tq_big, tb), tq_big, (ATT_Q_BLOCKS_BWD // 2) * i)
            return 0

        lax.fori_loop(0, n_big, big_tile, 0)

        def finish(bi, _):
            r0 = pl.multiple_of(bi * tb, tb)
            dk0, dk1 = dk_acc[0, :, pl.ds(r0, tb)], dk_acc[1, :, pl.ds(r0, tb)]
            dqkv_ref[1, pl.ds(r0, tb), :] = _pair_lanes(dk0, dk1).astype(BF16)
            dqkv_ref[2, pl.ds(r0, tb), :] = _pair_lanes(dv_acc[0, :, pl.ds(r0, tb)],
                                                        dv_acc[1, :, pl.ds(r0, tb)]).astype(BF16)
            dc_ref[pl.ds(first_head, 1), pl.ds(r0, tb)] -= dk0[Q_ONES:Q_ONES + 1, :]
            dc_ref[pl.ds(first_head + 1, 1), pl.ds(r0, tb)] -= dk1[Q_ONES:Q_ONES + 1, :]
            return 0

        lax.fori_loop(0, nb, finish, 0)
        finish_exchange()

    once = pl.Buffered(1)
    pair = pl.BlockSpec((lp, LANES), lambda hp: (0, hp))
    pair_in = pl.BlockSpec((lp, LANES), lambda hp: (0, hp), pipeline_mode=once)
    heads = pl.BlockSpec((2, lp, LANES), lambda hp: (hp, 0, 0), pipeline_mode=once)
    heads_t = pl.BlockSpec((2, LANES, lp), lambda hp: (hp, 0, 0), pipeline_mode=once)
    hbm = pl.BlockSpec(memory_space=pl.ANY)
    return pl.pallas_call(
        body, name="attention_backward", grid=(N_HEADS // 2,),
        out_shape=[jax.ShapeDtypeStruct((3, lp, ATTN_WIDTH), BF16), jax.ShapeDtypeStruct((N_HEADS, lp), F32)]
        + _exchange_results(behind),
        in_specs=[heads_t, heads, heads_t, heads, pair_in, pair_in, _const((N_HEADS, lp))] + [hbm] * nx,
        out_specs=[pl.BlockSpec((3, lp, LANES), lambda hp: (0, 0, hp)),
                   pl.BlockSpec((N_HEADS, lp), lambda hp: (0, 0))] + [hbm] * nx,
        scratch_shapes=[pltpu.VMEM((2, LANES, lp), BF16), pltpu.VMEM((2, LANES, lp), BF16),
                        pltpu.VMEM((2, LANES, lp), F32), pltpu.VMEM((2, LANES, lp), F32),
                        pltpu.VMEM((2, LANES, tq_big), F32), pltpu.VMEM((2, 2, tb, tq_big), F32)]
        + _exchange_semaphores(nx),
        compiler_params=_params(("arbitrary",)),
    )(qt, k, kt, v, do, o, lse, *[a for _, a, _ in behind])


def _middle(x, target, h, o, u, zp, w_main, w_up_pool, w_up_attn, w_out, pool_w, pool_scale, final_g):
    seq = x.shape[0]
    tm = ROW_TILE
    nt = seq // tm + 1
    lp = nt * tm
    halo_blocks = tm // MAX_WINDOW

    def body(x_ref, t_ref, h_ref, o_ref, u_ref, uh_ref, zp_ref,
             wc_ref, wupp_ref, wupa_ref, wout_ref, pw_ref, sc_ref, gf_ref,
             dh2_ref, mg_ref, yp_ref, ya_ref, dap_ref, daa_ref, do_ref, dmid_ref, dpn_ref,
             loss_ref, dgf_ref, dsc_ref, dpw_ref):
        i = pl.program_id(0)
        tiles = (dh2_ref, mg_ref, yp_ref, ya_ref, dap_ref, daa_ref, do_ref, dmid_ref, dpn_ref)

        @pl.when(i == 0)
        def _():
            for ref in tiles + (loss_ref, dgf_ref, dsc_ref, dpw_ref):
                ref[...] = jnp.zeros_like(ref)

        @pl.when(i > 0)
        def _():
            xt = x_ref[...]
            hb = h_ref[...]
            pc = _dot_nt(hb, wc_ref[...])
            za, gp, ga = pc[:, :512], pc[:, 512:1536], pc[:, 1536:]
            of = o_ref[...].astype(F32)
            sza = _sigmoid(za)
            silu_za = za * sza
            ya = (of * silu_za).astype(BF16)
            ya_ref[...] = ya
            aa = _dot(ya, wupa_ref[...])

            u = u_ref[...]
            zp = zp_ref[...]
            counts = _pool_counts(i * tm, tm)
            ps = _pool_means(jnp.concatenate([uh_ref[...], u], axis=0), u, counts)
            pbs = [p.astype(BF16) for p in ps]
            ppw = jnp.concatenate([_dot(pbs[g], pw_ref[g]) for g in range(4)], axis=1)
            sc = sc_ref[...]
            szp = _sigmoid(zp)
            silu_zp = zp * szp
            ypre = ppw * sc
            yp = (ypre * silu_zp).astype(BF16)
            yp_ref[...] = yp
            ap = _dot(yp, wupp_ref[...])

            sgp, sga = _sigmoid(gp), _sigmoid(ga)
            mg = (sgp * ap + sga * aa).astype(BF16)
            mg_ref[...] = mg
            h2 = xt + _dot(mg, wout_ref[...])
            r2 = lax.rsqrt(jnp.mean(h2 * h2, axis=-1, keepdims=True) + RMS_EPS)
            h2n = h2 * r2
            gf = gf_ref[...]
            diff = h2n * gf - t_ref[...]
            loss_ref[...] += 0.5 * jnp.sum(jnp.mean(diff * diff, axis=-1, keepdims=True), axis=0, keepdims=True)
            dy = diff * (1.0 / D_MODEL)
            dgf_ref[...] += jnp.sum(dy * h2n, axis=0, keepdims=True)
            dyg = dy * gf
            dh2 = r2 * (dyg - h2n * jnp.mean(dyg * h2n, axis=-1, keepdims=True))
            dh2_ref[...] = dh2
            dmg = _dot_nt(dh2.astype(BF16), wout_ref[...])
            dap = (dmg * sgp).astype(BF16)
            daa = (dmg * sga).astype(BF16)
            dap_ref[...] = dap
            daa_ref[...] = daa
            dmid_ref[:, MID_GP:MID_GA] = (dmg * ap * sgp * (1.0 - sgp)).astype(BF16)
            dmid_ref[:, MID_GA:] = (dmg * aa * sga * (1.0 - sga)).astype(BF16)
            dyp = _dot_nt(dap, wupp_ref[...])
            dya = _dot_nt(daa, wupa_ref[...])
            do_ref[...] = (dya * silu_za).astype(BF16)
            dmid_ref[:, MID_ZA:MID_GP] = (dya * of * (sza * (1.0 + za * (1.0 - sza)))).astype(BF16)

            dypre = dyp * silu_zp
            dmid_ref[:, :MID_ZA] = (dyp * ypre * (szp * (1.0 + zp * (1.0 - szp)))).astype(BF16)
            dsc_ref[...] += jnp.sum(dypre * ppw, axis=0, keepdims=True)
            dppw = (dypre * sc).astype(BF16)
            dpns = []
            for g in range(4):
                dg = dppw[:, POOL_GROUP * g:POOL_GROUP * (g + 1)]
                dpw_ref[g] += _dot_tn(pbs[g], dg)
                dpns.append(_dot_nt(dg, pw_ref[g]) / counts[g])
            dpn_ref[...] = jnp.concatenate(dpns, axis=1)

    real = lambda w: pl.BlockSpec((tm, w), lambda i: (jnp.maximum(i - 1, 0), 0))
    row = lambda w: pl.BlockSpec((tm, w), lambda i: (i, 0))
    in_specs = [
        real(D_MODEL), real(D_MODEL), row(D_MODEL), row(512), row(512),
        pl.BlockSpec((MAX_WINDOW, 512), lambda i: (jnp.maximum(i * halo_blocks - 1, 0), 0)), row(512),
        _const((2560, D_MODEL), (1, 0)), _const((POOL_WIDTH, D_MODEL)), _const((ATTN_WIDTH, D_MODEL)),
        _const((D_MODEL, D_MODEL)), _const((4, POOL_GROUP, POOL_GROUP)), _const((1, POOL_WIDTH)), _const((1, D_MODEL)),
    ]
    sd = jax.ShapeDtypeStruct
    out_shape = [
        sd((lp, D_MODEL), F32),
        sd((lp, D_MODEL), BF16),
        sd((lp, 512), BF16),
        sd((lp, 512), BF16),
        sd((lp, D_MODEL), BF16),
        sd((lp, D_MODEL), BF16),
        sd((lp, 512), BF16),
        sd((lp, MID_WIDTH), BF16),
        sd((lp, 512), F32),
        sd((1, LANES), F32),
        sd((1, D_MODEL), F32),
        sd((1, 512), F32),
        sd((4, POOL_GROUP, POOL_GROUP), F32),
    ]
    keep = lambda shape: pl.BlockSpec(shape, lambda i: (0,) * len(shape))
    out_specs = [row(D_MODEL), row(D_MODEL), row(512), row(512), row(D_MODEL), row(D_MODEL), row(512),
                 row(MID_WIDTH), row(512),
                 keep((1, LANES)), keep((1, D_MODEL)), keep((1, 512)), keep((4, POOL_GROUP, POOL_GROUP))]
    return pl.pallas_call(
        body, name="middle", grid=(nt,), out_shape=out_shape, in_specs=in_specs, out_specs=out_specs,
        compiler_params=_params(("arbitrary",)),
    )(x, target, h, o, u, u, zp, w_main, w_up_pool, w_up_attn, w_out, pool_w, pool_scale, final_g)


DUF_WIDTH = POOL_WIDTH + LANES


def _sequence_grads(dpn, dc, sneg):
    lp = dpn.shape[0]
    tm = ROW_TILE
    nt = lp // tm
    halo_blocks = tm // MAX_WINDOW
    last_halo = lp // MAX_WINDOW - 1

    def body(dpn_ref, dpnh_ref, dc_ref, sn_ref, duf_ref, dbf_ref, carry_ref):
        i = pl.program_id(0)
        t = nt - 1 - i

        @pl.when(i == 0)
        def _():
            carry_ref[...] = jnp.zeros_like(carry_ref)
            dbf_ref[...] = jnp.zeros_like(dbf_ref)

        dpn_t = dpn_ref[...]
        ahead = jnp.where(i == 0, jnp.zeros_like(dpnh_ref), dpnh_ref[...])
        ext = jnp.concatenate([dpn_t, ahead], axis=0)
        counts = _pool_counts(t * tm, tm)
        for g, w in enumerate(POOL_WINDOWS):
            s = ext[:, POOL_GROUP * g:POOL_GROUP * (g + 1)]
            sh = 1
            while sh < w:
                s = s + pltpu.roll(s, tm + MAX_WINDOW - sh, axis=0)
                sh *= 2
            du = s[:tm, :] - dpn_t[:, POOL_GROUP * g:POOL_GROUP * (g + 1)] * counts[g]
            duf_ref[:, POOL_GROUP * g:POOL_GROUP * (g + 1)] = du.astype(BF16)

        dct = jnp.concatenate([dc_ref[...], jnp.zeros((LANES - N_HEADS, tm), F32)], axis=0).T
        rloc = lax.broadcasted_iota(jnp.int32, (tm, LANES), 0)
        sh = 1
        while sh < tm:
            dct = dct + jnp.where(rloc + sh < tm, pltpu.roll(dct, tm - sh, axis=0), 0.0)
            sh *= 2
        dct = dct + carry_ref[...]
        carry_ref[...] = dct[0:1, :]
        df = dct * sn_ref[...]
        dbf_ref[...] += jnp.sum(df, axis=0, keepdims=True)
        duf_ref[:, POOL_WIDTH:] = df.astype(BF16)

    rev = lambda w: pl.BlockSpec((tm, w), lambda i: (nt - 1 - i, 0))
    return pl.pallas_call(
        body, name="sequence_grads", grid=(nt,),
        out_shape=[jax.ShapeDtypeStruct((lp, DUF_WIDTH), BF16), jax.ShapeDtypeStruct((1, LANES), F32)],
        in_specs=[rev(512),
                  pl.BlockSpec((MAX_WINDOW, 512), lambda i: (jnp.minimum((nt - i) * halo_blocks, last_halo), 0)),
                  pl.BlockSpec((N_HEADS, tm), lambda i: (0, nt - 1 - i)), rev(LANES)],
        out_specs=[rev(DUF_WIDTH), pl.BlockSpec((1, LANES), lambda i: (0, 0))],
        scratch_shapes=[pltpu.VMEM((1, LANES), F32)],
        compiler_params=_params(("arbitrary",)),
    )(dpn, dpn, dc, sneg)


def _backward_in(x, tile0, norm_g, dh2, duf, dqkv, dmid, w_main, w_f, behind):
    seq = x.shape[0]
    tm = ROW_TILE
    nt = seq // tm + 1
    nx = len(behind)

    def body(x_ref, t0_ref, g_ref, dh2_ref, du_ref, df_ref, dqkv_ref, dzp_ref, dza_ref, dgp_ref, dga_ref,
             wm_ref, wf_ref, *rest):
        gx_ref, gmeta_ref, dg_ref = rest[nx:nx + 3]
        dproj_ref = rest[2 * nx + 3]
        finish_exchange = _behind(behind, rest[:nx], rest[nx + 3:2 * nx + 3], rest[2 * nx + 4:])
        t = pl.program_id(0)

        @pl.when(t == 0)
        def _():
            dg_ref[...] = jnp.zeros_like(dg_ref)

        dproj_ref[:, 0:512] = du_ref[...]
        dproj_ref[:, 512:1024] = dzp_ref[...]
        dproj_ref[:, 1024:1536] = dqkv_ref[0]
        dproj_ref[:, 1536:2048] = dqkv_ref[1]
        dproj_ref[:, 2048:2560] = dqkv_ref[2]
        dproj_ref[:, 2560:3072] = dza_ref[...]
        dproj_ref[:, 3072:4096] = dgp_ref[...]
        dproj_ref[:, 4096:5120] = dga_ref[...]
        dh = _dot(dproj_ref[...], wm_ref[...]) + _dot(df_ref[...], wf_ref[...])
        xt = jnp.where(t == 0, t0_ref[...], x_ref[...])
        r = lax.rsqrt(jnp.mean(xt * xt, axis=-1, keepdims=True) + RMS_EPS)
        xn = xt * r
        dg_ref[...] += jnp.sum(dh * xn, axis=0, keepdims=True)
        dhg = dh * g_ref[...]
        dx = dh2_ref[...] + r * (dhg - xn * jnp.mean(dhg * xn, axis=-1, keepdims=True))

        @pl.when(t > 0)
        def _():
            gx_ref[...] = dx

        @pl.when(t == 0)
        def _():
            gmeta_ref[...] = dx[PAD:, :]
            gx_ref[...] = jnp.zeros_like(gx_ref)

        finish_exchange()

    row = lambda w, j=0: pl.BlockSpec((tm, w), lambda i: (i, j))
    real = pl.BlockSpec((tm, D_MODEL), lambda i: (jnp.maximum(i - 1, 0), 0))
    hbm = pl.BlockSpec(memory_space=pl.ANY)
    in_specs = [
        real, _const((tm, D_MODEL)), _const((1, D_MODEL)), row(D_MODEL),
        row(POOL_WIDTH), row(LANES, POOL_WIDTH // LANES), pl.BlockSpec((3, tm, ATTN_WIDTH), lambda i: (0, i, 0)),
        row(512, 0), row(512, 1), row(1024, 1), row(1024, 2),
        _const((N_MAIN, D_MODEL)), _const((LANES, D_MODEL)),
    ] + [hbm] * nx
    sd = jax.ShapeDtypeStruct
    out_shape = [sd((seq, D_MODEL), F32), sd((N_META, D_MODEL), F32), sd((1, D_MODEL), F32)] + _exchange_results(behind)
    keep = lambda shape: pl.BlockSpec(shape, lambda i: (0,) * len(shape))
    out_specs = [real, keep((N_META, D_MODEL)), keep((1, D_MODEL))] + [hbm] * nx
    return pl.pallas_call(
        body, name="backward_in", grid=(nt,), out_shape=out_shape, in_specs=in_specs, out_specs=out_specs,
        scratch_shapes=[pltpu.VMEM((tm, N_MAIN), BF16)] + _exchange_semaphores(nx),
        compiler_params=_params(("arbitrary",)),
    )(x, tile0, norm_g, dh2, duf, duf, dqkv, dmid, dmid, dmid, dmid, w_main, w_f, *[a for _, a, _ in behind])


def _matmul_tn(name, a, b, tn):
    lp, m = a.shape
    n = b.shape[1]

    def body(a_ref, b_ref, c_ref):
        c_ref[...] = _dot_tn(a_ref[...].astype(BF16), b_ref[...].astype(BF16))

    return pl.pallas_call(
        body, name=name, grid=(n // tn,), out_shape=jax.ShapeDtypeStruct((m, n), F32),
        in_specs=[_const((lp, m)), pl.BlockSpec((lp, tn), lambda j: (0, j))],
        out_specs=pl.BlockSpec((m, tn), lambda j: (0, j)),
        compiler_params=_params(("arbitrary",)),
    )(a, b)


def _matmul_tn_rows(name, a, b, tm):
    lp, m = a.shape
    n = b.shape[1]

    def body(a_ref, b_ref, c_ref):
        c_ref[...] = _dot_tn(a_ref[...].astype(BF16), b_ref[...].astype(BF16))

    return pl.pallas_call(
        body, name=name, grid=(m // tm,), out_shape=jax.ShapeDtypeStruct((m, n), F32),
        in_specs=[pl.BlockSpec((lp, tm), lambda j: (0, j)), _const((lp, n))],
        out_specs=pl.BlockSpec((tm, n), lambda j: (j, 0)),
        compiler_params=_params(("arbitrary",)),
    )(a, b)


def _matmul_tn_stack(name, a, b):
    n_blocks, lp, m = a.shape
    n = b.shape[1]

    def body(a_ref, b_ref, c_ref):
        c_ref[...] = _dot_tn(a_ref[...], b_ref[...])

    return pl.pallas_call(
        body, name=name, grid=(n_blocks,), out_shape=jax.ShapeDtypeStruct((n_blocks * m, n), F32),
        in_specs=[pl.BlockSpec((None, lp, m), lambda j: (j, 0, 0)), _const((lp, n))],
        out_specs=pl.BlockSpec((m, n), lambda j: (j, 0)),
        compiler_params=_params(("arbitrary",)),
    )(a, b)


def _adamw_step(p_ref, w_ref, m_ref, v_ref, g_ref, d_ref, mo_ref, vo_ref):
    g = p_ref[0].astype(F32)
    for s in range(1, p_ref.shape[0]):
        g = g + p_ref[s].astype(F32)
    m_new = ADAM_B1 * m_ref[...] + (1.0 - ADAM_B1) * g
    v_new = ADAM_B2 * v_ref[...] + (1.0 - ADAM_B2) * (g * g)
    m_hat = m_new / (1.0 - ADAM_B1 ** ADAM_STEP)
    v_hat = v_new / (1.0 - ADAM_B2 ** ADAM_STEP)
    g_ref[...] = g
    d_ref[...] = -ADAM_LR * (m_hat / (jnp.sqrt(v_hat) + ADAM_EPS) + ADAM_WD * w_ref[...])
    mo_ref[...] = m_new
    vo_ref[...] = v_new


def _adamw_small(name, groups, loss_parts):
    n = len(groups)

    def body(*refs):
        ins, outs = refs[:4 * n + 1], refs[4 * n + 1:]
        for j in range(n):
            _adamw_step(*ins[4 * j:4 * j + 4], *outs[4 * j:4 * j + 4])
        total = ins[-1][0]
        for s in range(1, N_DEV):
            total = total + ins[-1][s]
        outs[-1][...] = total

    vmem = pl.BlockSpec(memory_space=pltpu.VMEM)
    out_shape = [jax.ShapeDtypeStruct(w.shape, F32) for _, w, _, _ in groups for _ in range(4)]
    out_shape.append(jax.ShapeDtypeStruct(loss_parts.shape[1:], F32))
    res = pl.pallas_call(
        body, name=name, out_shape=out_shape, in_specs=[vmem] * (4 * n + 1), out_specs=[vmem] * (4 * n + 1),
        compiler_params=_params(),
    )(*[a for g in groups for a in g], loss_parts)
    return [res[4 * j:4 * j + 4] for j in range(n)], res[-1]


def _adamw(name, parts, w, m, v, rows, cols=None):
    r, c_all = w.shape
    c = cols or c_all
    n_parts = parts.shape[0]

    def body(p_ref, w_ref, m_ref, v_ref, g_ref, d_ref, mo_ref, vo_ref):
        _adamw_step(p_ref, w_ref, m_ref, v_ref, g_ref, d_ref, mo_ref, vo_ref)

    blk = pl.BlockSpec((rows, c), lambda i, j: (i, j))
    return pl.pallas_call(
        body, name=name, grid=(r // rows, c_all // c), out_shape=[jax.ShapeDtypeStruct((r, c_all), F32)] * 4,
        in_specs=[pl.BlockSpec((n_parts, rows, c), lambda i, j: (0, i, j)), blk, blk, blk],
        out_specs=[blk] * 4,
        compiler_params=_params(("arbitrary", "arbitrary")),
    )(parts, w, m, v)


def _pair_sum(name, mine, theirs, rows):
    n, r, c = mine.shape

    def body(a_ref, b_ref, o_ref):
        o_ref[...] = (a_ref[...].astype(F32) + b_ref[...].astype(F32)).astype(BF16)

    blk = pl.BlockSpec((1, rows, c), lambda j, i: (j, i, 0))
    return pl.pallas_call(
        body, name=name, grid=(n, r // rows), out_shape=jax.ShapeDtypeStruct((n, r, c), BF16),
        in_specs=[blk, blk], out_specs=blk,
        compiler_params=_params(("arbitrary", "arbitrary")),
    )(mine, theirs)


def _by_core(slots):
    by_core = slots.reshape((4, 2) + slots.shape[1:]).swapaxes(0, 1)
    c = lax.axis_index("c")
    return (lax.dynamic_index_in_dim(by_core, c, 0, keepdims=False),
            lax.dynamic_index_in_dim(by_core, 1 - c, 0, keepdims=False))


def _columns_to_slots(a):
    r, c8 = a.shape
    return a.reshape(r, N_DEV, c8 // N_DEV).transpose(1, 0, 2)


def _slots_to_columns(a):
    n, r, c = a.shape
    return a.transpose(1, 0, 2).reshape(r, n * c)


def kernel(x, meta_tokens, norm_g, w_in, b_forget, pool_w, pool_scale, w_up_pool, w_up_attn, w_out, final_norm_g, loss_target, m_meta_tokens, m_norm_g, m_w_in, m_b_forget, m_pool_w, m_pool_scale, m_w_up_pool, m_w_up_attn, m_w_out, m_final_norm_g, v_meta_tokens, v_norm_g, v_w_in, v_b_forget, v_pool_w, v_pool_scale, v_w_up_pool, v_w_up_attn, v_w_out, v_final_norm_g):
    xs = x[0]
    target = loss_target[0]

    g_in, g_meta = _gather_two_level("gather_weights", [w_in[0].T.astype(BF16), meta_tokens], (320, 8))
    w_full = g_in.reshape(N_DEV * g_in.shape[1], D_MODEL)
    w_main = jnp.concatenate([w_full[:N_BEFORE_F], w_full[N_BEFORE_F + N_HEADS:]], axis=0)
    w_f = jnp.pad(w_full[N_BEFORE_F:N_BEFORE_F + N_HEADS], ((0, LANES - N_HEADS), (0, 0)))
    meta = _slots_to_columns(g_meta)
    tile0 = jnp.concatenate([jnp.zeros((PAD, D_MODEL), F32), meta], axis=0)
    b_f = jnp.pad(b_forget, ((0, 0), (0, LANES - N_HEADS)))
    pw_b = pool_w[0].astype(BF16)
    final_g = final_norm_g.reshape(1, D_MODEL)

    h, u, zp, k, v, qt, kt, vt, sneg = _forward_in(xs, tile0, norm_g, w_main, w_f, b_f)
    o, lse, g_upp, g_upa, g_out = _attention_forward(
        qt, k, vt, [("gather", w.astype(BF16), ALL_PEERS) for w in (w_up_pool[0], w_up_attn[0], w_out[0])])
    wupp = _slots_to_columns(g_upp)
    wupa = _slots_to_columns(g_upa)
    wout = g_out.reshape(D_MODEL, D_MODEL)
    (dh2, mg, yp, ya, dap, daa, do, dmid, dpn,
     loss_part, d_final_g, d_scale, d_pool_w) = _middle(xs, target, h, o, u, zp, w_main, wupp, wupa, wout,
                                                        pw_b, pool_scale, final_g)
    dw_out = _matmul_tn("grad_w_out", mg, dh2, 256)
    dw_upp = _matmul_tn("grad_w_up_pool", yp, dap, 512)
    dw_upa = _matmul_tn("grad_w_up_attn", ya, daa, 512)
    dqkv, dc, p_upp, p_upa, p_out, p_pool_w, p_scale, p_final_g = _attention_backward(
        qt, k, kt, v, do, o, lse,
        [("scatter", _columns_to_slots(dw_upp).astype(BF16), ALL_PEERS),
         ("scatter", _columns_to_slots(dw_upa).astype(BF16), ALL_PEERS),
         ("scatter", dw_out.reshape(N_DEV, D_MODEL // N_DEV, D_MODEL).astype(BF16), ALL_PEERS),
         ("gather", d_pool_w.reshape(4 * POOL_GROUP, POOL_GROUP), ALL_PEERS),
         ("gather", d_scale, ALL_PEERS), ("gather", d_final_g, ALL_PEERS)])
    duf, d_bf = _sequence_grads(dpn, dc, sneg)
    g_uf = _matmul_tn_rows("grad_w_in_pool_forget", duf, h, DUF_WIDTH)
    g_qkv = _matmul_tn_stack("grad_w_in_attention", dqkv, h)
    g_mid = _matmul_tn_rows("grad_w_in_gates", dmid, h, 512)
    dw_in = jnp.concatenate([g_uf[:POOL_WIDTH], g_mid[:MID_ZA], g_qkv, g_mid[MID_ZA:MID_GP],
                             g_uf[POOL_WIDTH:POOL_WIDTH + N_HEADS], g_mid[MID_GP:]], axis=0)
    dw_in = dw_in.reshape(N_DEV, dw_in.shape[0] // N_DEV, D_MODEL)
    mine, for_sibling = _by_core(dw_in)
    from_sibling, = _exchange("swap_with_sibling", [("swap", for_sibling.astype(BF16), (SIBLING,))])
    pair_sums = _pair_sum("pair_sum", mine, from_sibling, dw_in.shape[1])
    grad_x, d_meta, d_norm_g, p_in, p_bf, p_loss = _backward_in(
        xs, tile0, norm_g, dh2, duf, dqkv, dmid, w_main, w_f,
        [("chips", pair_sums, SAME_CORE), ("gather", d_bf, ALL_PEERS), ("gather", loss_part, ALL_PEERS)])
    p_meta, p_norm_g = _exchange(
        "exchange_gradients", [("scatter", _columns_to_slots(d_meta), ALL_PEERS), ("gather", d_norm_g, ALL_PEERS)])


    def pad_f(a):
        return jnp.pad(a, ((0, 0), (0, LANES - N_HEADS)))

    res = {}
    res["w_in"] = [a.T for a in _adamw("adamw_w_in", p_in, w_in[0].T, m_w_in[0].T, v_w_in[0].T, p_in.shape[1], 256)]
    res["w_up_pool"] = _adamw("adamw_w_up_pool", p_upp, w_up_pool[0], m_w_up_pool[0], v_w_up_pool[0], 512)
    res["w_up_attn"] = _adamw("adamw_w_up_attn", p_upa, w_up_attn[0], m_w_up_attn[0], v_w_up_attn[0], 512)
    res["w_out"] = _adamw("adamw_w_out", p_out, w_out[0], m_w_out[0], v_w_out[0], 128)
    flat = lambda a: a.reshape(4 * POOL_GROUP, POOL_GROUP)
    row = lambda a: a.reshape(1, D_MODEL)
    small, loss_row = _adamw_small(
        "adamw_small",
        [(p_meta, meta_tokens, m_meta_tokens, v_meta_tokens),
         (p_norm_g, norm_g, m_norm_g, v_norm_g),
         (p_bf, pad_f(b_forget), pad_f(m_b_forget), pad_f(v_b_forget)),
         (p_pool_w, flat(pool_w), flat(m_pool_w), flat(v_pool_w)),
         (p_scale, pool_scale, m_pool_scale, v_pool_scale),
         (p_final_g, final_g, row(m_final_norm_g), row(v_final_norm_g))],
        p_loss)
    res["meta_tokens"], res["norm_g"], bf, pw, res["pool_scale"], fg = small
    res["b_forget"] = [a[:, :N_HEADS] for a in bf]
    res["pool_w"] = [a.reshape(pool_w.shape) for a in pw]
    res["final_norm_g"] = [a.reshape(D_MODEL) for a in fg]
    loss = loss_row[0, 0]
    for name in ("w_in", "w_up_pool", "w_up_attn", "w_out"):
        res[name] = [a[None] for a in res[name]]

    order = ["meta_tokens", "norm_g", "w_in", "b_forget", "pool_w", "pool_scale", "w_up_pool", "w_up_attn", "w_out",
             "final_norm_g"]
    outs = [loss, grad_x[None]]
    for part in range(4):
        outs += [res[name][part] for name in order]
    return tuple(outs)
```
